```python
import math
import jax, jax.numpy as jnp
from jax import lax
import numpy as np

D_MODEL = 1024
BATCH = 16
SEQ = 4096
DEPTH = 2

CHUNK = 64
N_LEFT_CHUNKS = 8
D_S5 = D_MODEL // 4
S5_GROUP = 16
S5_GROUPS = D_S5 // S5_GROUP
S5_STATE = 64
D_ATT = D_MODEL // 2
ATT_HEAD_DIM = 64
ATT_HEADS = D_ATT // ATT_HEAD_DIM
MAX_REL = 128
D_CONV = D_MODEL // 4
CONV_WIDTH = 31
N_BRANCHES = 3
D_FF = 2816
EPS = 1e-6
DT_MIN = 1e-3
DT_MAX = 1e-1

SPLIT_POINTS = [D_S5, D_S5 + D_ATT, D_S5 + 2 * D_ATT, D_S5 + 3 * D_ATT,
                D_S5 + 3 * D_ATT + 2 * D_CONV]
IN_COLS = D_S5 + 3 * D_ATT + 2 * D_CONV + N_BRANCHES * D_MODEL

kernel_name = "hybrid_s5_chunkattn_conformer_conv_gated"


def rmsnorm(x, g):
    xf = x.astype(jnp.float32)
    y = xf * lax.rsqrt(jnp.mean(xf * xf, axis=-1, keepdims=True) + EPS)
    return (y * g.astype(jnp.float32)).astype(x.dtype)


def layernorm(x, g, b):
    xf = x.astype(jnp.float32)
    mu = jnp.mean(xf, axis=-1, keepdims=True)
    var = jnp.mean(jnp.square(xf - mu), axis=-1, keepdims=True)
    y = (xf - mu) * lax.rsqrt(var + EPS)
    return (y * g.astype(jnp.float32) + b.astype(jnp.float32)).astype(x.dtype)


def swiglu_ffn(x, w_up, w_down):
    a, b = jnp.split(x @ w_up, 2, axis=-1)
    return (jax.nn.silu(a) * b) @ w_down


def _complex_linear_combine(e1, e2):
    a1r, a1i, b1r, b1i = e1
    a2r, a2i, b2r, b2i = e2
    ar = a2r * a1r - a2i * a1i
    ai = a2r * a1i + a2i * a1r
    br = a2r * b1r - a2i * b1i + b2r
    bi = a2r * b1i + a2i * b1r + b2i
    return (ar, ai, br, bi)


def s5_mixer(u, lambda_re, lambda_im, log_dt, b_re, b_im, c_re, c_im, d_skip, w_glu):
    bsz, seq, _ = u.shape
    f32 = jnp.float32
    uf = u.astype(f32).reshape(bsz, seq, S5_GROUPS, S5_GROUP)
    lr = jnp.minimum(lambda_re.astype(f32), -1e-4)
    li = lambda_im.astype(f32)
    dt = jnp.exp(log_dt.astype(f32))[:, None]
    mag = jnp.exp(lr * dt)
    ar = mag * jnp.cos(li * dt)
    ai = mag * jnp.sin(li * dt)
    den = lr * lr + li * li
    coef_r = ((ar - 1.0) * lr + ai * li) / den
    coef_i = (ai * lr - (ar - 1.0) * li) / den
    br = b_re.astype(f32)
    bi = b_im.astype(f32)
    bbar_r = coef_r[..., None] * br - coef_i[..., None] * bi
    bbar_i = coef_r[..., None] * bi + coef_i[..., None] * br
    bu_r = jnp.einsum('bsgc,gpc->bsgp', uf, bbar_r)
    bu_i = jnp.einsum('bsgc,gpc->bsgp', uf, bbar_i)
    a_r = jnp.broadcast_to(ar, bu_r.shape)
    a_i = jnp.broadcast_to(ai, bu_i.shape)
    _, _, xr, xi = lax.associative_scan(_complex_linear_combine, (a_r, a_i, bu_r, bu_i), axis=1)
    y = (jnp.einsum('bsgp,gcp->bsgc', xr, c_re.astype(f32))
         - jnp.einsum('bsgp,gcp->bsgc', xi, c_im.astype(f32))
         + d_skip.astype(f32).reshape(S5_GROUPS, S5_GROUP) * uf)
    y = jax.nn.gelu(y.reshape(bsz, seq, D_S5)).astype(u.dtype)
    a, g = jnp.split(y @ w_glu, 2, axis=-1)
    return a * jax.nn.sigmoid(g)


def chunked_attention(q, k, v, q_gain, k_gain, rel_bias):
    bsz, seq, _ = q.shape
    n_chunks = seq // CHUNK
    q = rmsnorm(q.reshape(bsz, seq, ATT_HEADS, ATT_HEAD_DIM), q_gain)
    k = rmsnorm(k.reshape(bsz, seq, ATT_HEADS, ATT_HEAD_DIM), k_gain)
    v = v.reshape(bsz, seq, ATT_HEADS, ATT_HEAD_DIM)
    pad = N_LEFT_CHUNKS * CHUNK
    band = (N_LEFT_CHUNKS + 1) * CHUNK
    k_pad = jnp.pad(k, ((0, 0), (pad, 0), (0, 0), (0, 0)))
    v_pad = jnp.pad(v, ((0, 0), (pad, 0), (0, 0), (0, 0)))
    qi = jnp.arange(CHUNK)[:, None]
    kj = jnp.arange(band)[None, :]
    rel = jnp.clip(pad + qi - kj, -MAX_REL, MAX_REL) + MAX_REL
    bias = rel_bias.astype(jnp.float32)[:, rel]
    scale = ATT_HEAD_DIM ** -0.5

    def one_chunk(c):
        start = c * CHUNK
        qc = lax.dynamic_slice_in_dim(q, start, CHUNK, axis=1)
        kc = lax.dynamic_slice_in_dim(k_pad, start, band, axis=1)
        vc = lax.dynamic_slice_in_dim(v_pad, start, band, axis=1)
        s = jnp.einsum('bqhd,bkhd->bhqk', qc, kc).astype(jnp.float32) * scale + bias[None]
        valid = kj >= (pad - start)
        s = jnp.where(valid[None, None], s, -1e30)
        p = jax.nn.softmax(s, axis=-1).astype(vc.dtype)
        return jnp.einsum('bhqk,bkhd->bqhd', p, vc)

    out = lax.map(one_chunk, jnp.arange(n_chunks))
    return out.transpose(1, 0, 2, 3, 4).reshape(bsz, seq, D_ATT)


def conv_module(z, w_dw, b_dw, ln_g, ln_b, w_pw):
    a, g = jnp.split(z, 2, axis=-1)
    h = a * jax.nn.sigmoid(g)
    h = lax.conv_general_dilated(
        h, w_dw[:, None, :], window_strides=(1,), padding=[(CONV_WIDTH - 1, 0)],
        dimension_numbers=('NWC', 'WIO', 'NWC'), feature_group_count=D_CONV) + b_dw
    h = jax.nn.silu(layernorm(h, ln_g, ln_b))
    return h @ w_pw


def _fwd_setup_inputs(seed: int = 0) -> dict:
    key = jax.random.key(seed)
    ks = jax.random.split(key, 32)
    L = DEPTH

    def nrm(k, shape, scale):
        return jax.random.normal(k, shape, jnp.float32) * scale

    n_idx = jnp.arange(S5_STATE, dtype=jnp.float32)
    lam_im = jnp.broadcast_to(math.pi * n_idx, (L, S5_GROUPS, S5_STATE))
    return {
        "x": nrm(ks[0], (BATCH, SEQ, D_MODEL), 1.0),
        "ffn1_norm": 1.0 + nrm(ks[1], (L, D_MODEL), 0.02),
        "ffn1_w_up": nrm(ks[2], (L, D_MODEL, 2 * D_FF), D_MODEL ** -0.5),
        "ffn1_w_down": nrm(ks[3], (L, D_FF, D_MODEL), D_FF ** -0.5),
        "mix_norm": 1.0 + nrm(ks[4], (L, D_MODEL), 0.02),
        "w_in": nrm(ks[5], (L, D_MODEL, IN_COLS), D_MODEL ** -0.5),
        "b_gate": nrm(ks[6], (L, N_BRANCHES * D_MODEL), 0.01),
        "s5_lambda_re": -0.5 + nrm(ks[7], (L, S5_GROUPS, S5_STATE), 0.01),
        "s5_lambda_im": lam_im + nrm(ks[8], (L, S5_GROUPS, S5_STATE), 0.01),
        "s5_log_dt": jax.random.uniform(ks[9], (L, S5_GROUPS), jnp.float32,
                                        math.log(DT_MIN), math.log(DT_MAX)),
        "s5_b_re": nrm(ks[10], (L, S5_GROUPS, S5_STATE, S5_GROUP), (2 * S5_GROUP) ** -0.5),
        "s5_b_im": nrm(ks[11], (L, S5_GROUPS, S5_STATE, S5_GROUP), (2 * S5_GROUP) ** -0.5),
        "s5_c_re": nrm(ks[12], (L, S5_GROUPS, S5_GROUP, S5_STATE), (2 * S5_STATE) ** -0.5),
        "s5_c_im": nrm(ks[13], (L, S5_GROUPS, S5_GROUP, S5_STATE), (2 * S5_STATE) ** -0.5),
        "s5_d": nrm(ks[14], (L, D_S5), 1.0),
        "s5_w_glu": nrm(ks[15], (L, D_S5, 2 * D_S5), D_S5 ** -0.5),
        "w_br_s5": nrm(ks[16], (L, D_S5, D_MODEL), D_S5 ** -0.5),
        "attn_q_gain": 1.0 + nrm(ks[17], (L, ATT_HEAD_DIM), 0.02),
        "attn_k_gain": 1.0 + nrm(ks[18], (L, ATT_HEAD_DIM), 0.02),
        "attn_rel_bias": nrm(ks[19], (L, ATT_HEADS, 2 * MAX_REL + 1), 0.1),
        "w_br_attn": nrm(ks[20], (L, D_ATT, D_MODEL), D_ATT ** -0.5),
        "conv_w_dw": nrm(ks[21], (L, CONV_WIDTH, D_CONV), CONV_WIDTH ** -0.5),
        "conv_b_dw": nrm(ks[22], (L, D_CONV), 0.01),
        "conv_ln_g": 1.0 + nrm(ks[23], (L, D_CONV), 0.02),
        "conv_ln_b": nrm(ks[24], (L, D_CONV), 0.01),
        "w_br_conv": nrm(ks[25], (L, D_CONV, D_MODEL), D_CONV ** -0.5),
        "w_out": nrm(ks[26], (L, D_MODEL, D_MODEL), D_MODEL ** -0.5),
        "ffn2_norm": 1.0 + nrm(ks[27], (L, D_MODEL), 0.02),
        "ffn2_w_up": nrm(ks[28], (L, D_MODEL, 2 * D_FF), D_MODEL ** -0.5),
        "ffn2_w_down": nrm(ks[29], (L, D_FF, D_MODEL), D_FF ** -0.5),
    }


def _fwd_reference(x, ffn1_norm, ffn1_w_up, ffn1_w_down, mix_norm, w_in, b_gate,
              s5_lambda_re, s5_lambda_im, s5_log_dt, s5_b_re, s5_b_im, s5_c_re, s5_c_im,
              s5_d, s5_w_glu, w_br_s5, attn_q_gain, attn_k_gain, attn_rel_bias, w_br_attn,
              conv_w_dw, conv_b_dw, conv_ln_g, conv_ln_b, w_br_conv, w_out,
              ffn2_norm, ffn2_w_up, ffn2_w_down):
    for l in range(DEPTH):
        x = x + 0.5 * swiglu_ffn(rmsnorm(x, ffn1_norm[l]), ffn1_w_up[l], ffn1_w_down[l])

        h = rmsnorm(x, mix_norm[l])
        proj = h @ w_in[l]
        u_s5, q, k, v, z_conv, gate_logits = jnp.split(proj, SPLIT_POINTS, axis=-1)

        y_s5 = s5_mixer(u_s5, s5_lambda_re[l], s5_lambda_im[l], s5_log_dt[l],
                        s5_b_re[l], s5_b_im[l], s5_c_re[l], s5_c_im[l],
                        s5_d[l], s5_w_glu[l]) @ w_br_s5[l]
        y_attn = chunked_attention(q, k, v, attn_q_gain[l], attn_k_gain[l],
                                   attn_rel_bias[l]) @ w_br_attn[l]
        y_conv = conv_module(z_conv, conv_w_dw[l], conv_b_dw[l], conv_ln_g[l],
                             conv_ln_b[l], w_br_conv[l])

        gates = jax.nn.sigmoid(gate_logits + b_gate[l])
        g_s5, g_attn, g_conv = jnp.split(gates, N_BRANCHES, axis=-1)
        merged = g_s5 * y_s5 + g_attn * y_attn + g_conv * y_conv
        x = x + merged @ w_out[l]

        x = x + 0.5 * swiglu_ffn(rmsnorm(x, ffn2_norm[l]), ffn2_w_up[l], ffn2_w_down[l])
    return x


import jax as _jax
import jax.numpy as _jnp

TWIN_FORMAT = 'train_step'
FWD_PARAMS = ['x', 'ffn1_norm', 'ffn1_w_up', 'ffn1_w_down', 'mix_norm', 'w_in', 'b_gate', 's5_lambda_re', 's5_lambda_im', 's5_log_dt', 's5_b_re', 's5_b_im', 's5_c_re', 's5_c_im', 's5_d', 's5_w_glu', 'w_br_s5', 'attn_q_gain', 'attn_k_gain', 'attn_rel_bias', 'w_br_attn', 'conv_w_dw', 'conv_b_dw', 'conv_ln_g', 'conv_ln_b', 'w_br_conv', 'w_out', 'ffn2_norm', 'ffn2_w_up', 'ffn2_w_down']
TWIN_WEIGHTS = ['ffn1_norm', 'ffn1_w_up', 'ffn1_w_down', 'mix_norm', 'w_in', 'b_gate', 's5_lambda_re', 's5_lambda_im', 's5_log_dt', 's5_b_re', 's5_b_im', 's5_c_re', 's5_c_im', 's5_d', 's5_w_glu', 'w_br_s5', 'attn_q_gain', 'attn_k_gain', 'attn_rel_bias', 'w_br_attn', 'conv_w_dw', 'conv_b_dw', 'conv_ln_g', 'conv_ln_b', 'w_br_conv', 'w_out', 'ffn2_norm', 'ffn2_w_up', 'ffn2_w_down']
TWIN_DIFF_INPUT = 'x'
TWIN_INPUTS = ['x', 'ffn1_norm', 'ffn1_w_up', 'ffn1_w_down', 'mix_norm', 'w_in', 'b_gate', 's5_lambda_re', 's5_lambda_im', 's5_log_dt', 's5_b_re', 's5_b_im', 's5_c_re', 's5_c_im', 's5_d', 's5_w_glu', 'w_br_s5', 'attn_q_gain', 'attn_k_gain', 'attn_rel_bias', 'w_br_attn', 'conv_w_dw', 'conv_b_dw', 'conv_ln_g', 'conv_ln_b', 'w_br_conv', 'w_out', 'ffn2_norm', 'ffn2_w_up', 'ffn2_w_down', 'loss_target', 'm_ffn1_norm', 'm_ffn1_w_up', 'm_ffn1_w_down', 'm_mix_norm', 'm_w_in', 'm_b_gate', 'm_s5_lambda_re', 'm_s5_lambda_im', 'm_s5_log_dt', 'm_s5_b_re', 'm_s5_b_im', 'm_s5_c_re', 'm_s5_c_im', 'm_s5_d', 'm_s5_w_glu', 'm_w_br_s5', 'm_attn_q_gain', 'm_attn_k_gain', 'm_attn_rel_bias', 'm_w_br_attn', 'm_conv_w_dw', 'm_conv_b_dw', 'm_conv_ln_g', 'm_conv_ln_b', 'm_w_br_conv', 'm_w_out', 'm_ffn2_norm', 'm_ffn2_w_up', 'm_ffn2_w_down', 'v_ffn1_norm', 'v_ffn1_w_up', 'v_ffn1_w_down', 'v_mix_norm', 'v_w_in', 'v_b_gate', 'v_s5_lambda_re', 'v_s5_lambda_im', 'v_s5_log_dt', 'v_s5_b_re', 'v_s5_b_im', 'v_s5_c_re', 'v_s5_c_im', 'v_s5_d', 'v_s5_w_glu', 'v_w_br_s5', 'v_attn_q_gain', 'v_attn_k_gain', 'v_attn_rel_bias', 'v_w_br_attn', 'v_conv_w_dw', 'v_conv_b_dw', 'v_conv_ln_g', 'v_conv_ln_b', 'v_w_br_conv', 'v_w_out', 'v_ffn2_norm', 'v_ffn2_w_up', 'v_ffn2_w_down']
TWIN_OUTPUTS = ['loss', 'grad_x', 'grad_ffn1_norm', 'grad_ffn1_w_up', 'grad_ffn1_w_down', 'grad_mix_norm', 'grad_w_in', 'grad_b_gate', 'grad_s5_lambda_re', 'grad_s5_lambda_im', 'grad_s5_log_dt', 'grad_s5_b_re', 'grad_s5_b_im', 'grad_s5_c_re', 'grad_s5_c_im', 'grad_s5_d', 'grad_s5_w_glu', 'grad_w_br_s5', 'grad_attn_q_gain', 'grad_attn_k_gain', 'grad_attn_rel_bias', 'grad_w_br_attn', 'grad_conv_w_dw', 'grad_conv_b_dw', 'grad_conv_ln_g', 'grad_conv_ln_b', 'grad_w_br_conv', 'grad_w_out', 'grad_ffn2_norm', 'grad_ffn2_w_up', 'grad_ffn2_w_down', 'delta_ffn1_norm', 'delta_ffn1_w_up', 'delta_ffn1_w_down', 'delta_mix_norm', 'delta_w_in', 'delta_b_gate', 'delta_s5_lambda_re', 'delta_s5_lambda_im', 'delta_s5_log_dt', 'delta_s5_b_re', 'delta_s5_b_im', 'delta_s5_c_re', 'delta_s5_c_im', 'delta_s5_d', 'delta_s5_w_glu', 'delta_w_br_s5', 'delta_attn_q_gain', 'delta_attn_k_gain', 'delta_attn_rel_bias', 'delta_w_br_attn', 'delta_conv_w_dw', 'delta_conv_b_dw', 'delta_conv_ln_g', 'delta_conv_ln_b', 'delta_w_br_conv', 'delta_w_out', 'delta_ffn2_norm', 'delta_ffn2_w_up', 'delta_ffn2_w_down', 'new_m_ffn1_norm', 'new_m_ffn1_w_up', 'new_m_ffn1_w_down', 'new_m_mix_norm', 'new_m_w_in', 'new_m_b_gate', 'new_m_s5_lambda_re', 'new_m_s5_lambda_im', 'new_m_s5_log_dt', 'new_m_s5_b_re', 'new_m_s5_b_im', 'new_m_s5_c_re', 'new_m_s5_c_im', 'new_m_s5_d', 'new_m_s5_w_glu', 'new_m_w_br_s5', 'new_m_attn_q_gain', 'new_m_attn_k_gain', 'new_m_attn_rel_bias', 'new_m_w_br_attn', 'new_m_conv_w_dw', 'new_m_conv_b_dw', 'new_m_conv_ln_g', 'new_m_conv_ln_b', 'new_m_w_br_conv', 'new_m_w_out', 'new_m_ffn2_norm', 'new_m_ffn2_w_up', 'new_m_ffn2_w_down', 'new_v_ffn1_norm', 'new_v_ffn1_w_up', 'new_v_ffn1_w_down', 'new_v_mix_norm', 'new_v_w_in', 'new_v_b_gate', 'new_v_s5_lambda_re', 'new_v_s5_lambda_im', 'new_v_s5_log_dt', 'new_v_s5_b_re', 'new_v_s5_b_im', 'new_v_s5_c_re', 'new_v_s5_c_im', 'new_v_s5_d', 'new_v_s5_w_glu', 'new_v_w_br_s5', 'new_v_attn_q_gain', 'new_v_attn_k_gain', 'new_v_attn_rel_bias', 'new_v_w_br_attn', 'new_v_conv_w_dw', 'new_v_conv_b_dw', 'new_v_conv_ln_g', 'new_v_conv_ln_b', 'new_v_w_br_conv', 'new_v_w_out', 'new_v_ffn2_norm', 'new_v_ffn2_w_up', 'new_v_ffn2_w_down']
TWIN_LEAF_KINDS = {'loss': 'loss', 'grad_x': 'grad_x', 'grad_ffn1_norm': 'grad_w', 'grad_ffn1_w_up': 'grad_w', 'grad_ffn1_w_down': 'grad_w', 'grad_mix_norm': 'grad_w', 'grad_w_in': 'grad_w', 'grad_b_gate': 'grad_w', 'grad_s5_lambda_re': 'grad_w', 'grad_s5_lambda_im': 'grad_w', 'grad_s5_log_dt': 'grad_w', 'grad_s5_b_re': 'grad_w', 'grad_s5_b_im': 'grad_w', 'grad_s5_c_re': 'grad_w', 'grad_s5_c_im': 'grad_w', 'grad_s5_d': 'grad_w', 'grad_s5_w_glu': 'grad_w', 'grad_w_br_s5': 'grad_w', 'grad_attn_q_gain': 'grad_w', 'grad_attn_k_gain': 'grad_w', 'grad_attn_rel_bias': 'grad_w', 'grad_w_br_attn': 'grad_w', 'grad_conv_w_dw': 'grad_w', 'grad_conv_b_dw': 'grad_w', 'grad_conv_ln_g': 'grad_w', 'grad_conv_ln_b': 'grad_w', 'grad_w_br_conv': 'grad_w', 'grad_w_out': 'grad_w', 'grad_ffn2_norm': 'grad_w', 'grad_ffn2_w_up': 'grad_w', 'grad_ffn2_w_down': 'grad_w', 'delta_ffn1_norm': 'delta_w', 'delta_ffn1_w_up': 'delta_w', 'delta_ffn1_w_down': 'delta_w', 'delta_mix_norm': 'delta_w', 'delta_w_in': 'delta_w', 'delta_b_gate': 'delta_w', 'delta_s5_lambda_re': 'delta_w', 'delta_s5_lambda_im': 'delta_w', 'delta_s5_log_dt': 'delta_w', 'delta_s5_b_re': 'delta_w', 'delta_s5_b_im': 'delta_w', 'delta_s5_c_re': 'delta_w', 'delta_s5_c_im': 'delta_w', 'delta_s5_d': 'delta_w', 'delta_s5_w_glu': 'delta_w', 'delta_w_br_s5': 'delta_w', 'delta_attn_q_gain': 'delta_w', 'delta_attn_k_gain': 'delta_w', 'delta_attn_rel_bias': 'delta_w', 'delta_w_br_attn': 'delta_w', 'delta_conv_w_dw': 'delta_w', 'delta_conv_b_dw': 'delta_w', 'delta_conv_ln_g': 'delta_w', 'delta_conv_ln_b': 'delta_w', 'delta_w_br_conv': 'delta_w', 'delta_w_out': 'delta_w', 'delta_ffn2_norm': 'delta_w', 'delta_ffn2_w_up': 'delta_w', 'delta_ffn2_w_down': 'delta_w', 'new_m_ffn1_norm': 'new_m', 'new_m_ffn1_w_up': 'new_m', 'new_m_ffn1_w_down': 'new_m', 'new_m_mix_norm': 'new_m', 'new_m_w_in': 'new_m', 'new_m_b_gate': 'new_m', 'new_m_s5_lambda_re': 'new_m', 'new_m_s5_lambda_im': 'new_m', 'new_m_s5_log_dt': 'new_m', 'new_m_s5_b_re': 'new_m', 'new_m_s5_b_im': 'new_m', 'new_m_s5_c_re': 'new_m', 'new_m_s5_c_im': 'new_m', 'new_m_s5_d': 'new_m', 'new_m_s5_w_glu': 'new_m', 'new_m_w_br_s5': 'new_m', 'new_m_attn_q_gain': 'new_m', 'new_m_attn_k_gain': 'new_m', 'new_m_attn_rel_bias': 'new_m', 'new_m_w_br_attn': 'new_m', 'new_m_conv_w_dw': 'new_m', 'new_m_conv_b_dw': 'new_m', 'new_m_conv_ln_g': 'new_m', 'new_m_conv_ln_b': 'new_m', 'new_m_w_br_conv': 'new_m', 'new_m_w_out': 'new_m', 'new_m_ffn2_norm': 'new_m', 'new_m_ffn2_w_up': 'new_m', 'new_m_ffn2_w_down': 'new_m', 'new_v_ffn1_norm': 'new_v', 'new_v_ffn1_w_up': 'new_v', 'new_v_ffn1_w_down': 'new_v', 'new_v_mix_norm': 'new_v', 'new_v_w_in': 'new_v', 'new_v_b_gate': 'new_v', 'new_v_s5_lambda_re': 'new_v', 'new_v_s5_lambda_im': 'new_v', 'new_v_s5_log_dt': 'new_v', 'new_v_s5_b_re': 'new_v', 'new_v_s5_b_im': 'new_v', 'new_v_s5_c_re': 'new_v', 'new_v_s5_c_im': 'new_v', 'new_v_s5_d': 'new_v', 'new_v_s5_w_glu': 'new_v', 'new_v_w_br_s5': 'new_v', 'new_v_attn_q_gain': 'new_v', 'new_v_attn_k_gain': 'new_v', 'new_v_attn_rel_bias': 'new_v', 'new_v_w_br_attn': 'new_v', 'new_v_conv_w_dw': 'new_v', 'new_v_conv_b_dw': 'new_v', 'new_v_conv_ln_g': 'new_v', 'new_v_conv_ln_b': 'new_v', 'new_v_w_br_conv': 'new_v', 'new_v_w_out': 'new_v', 'new_v_ffn2_norm': 'new_v', 'new_v_ffn2_w_up': 'new_v', 'new_v_ffn2_w_down': 'new_v'}


def _forward(args):
    return _fwd_reference(*[args[k] for k in FWD_PARAMS])


def _output_shape():
    out = _jax.eval_shape(lambda: _forward(_fwd_setup_inputs(0)))
    return out.shape, out.dtype

N_MICROBATCH = 1
ADAM_LR = 0.001
ADAM_B1 = 0.9
ADAM_B2 = 0.999
ADAM_EPS = 1e-08
ADAM_WD = 0.01
ADAM_STEP = 10
PER_EXAMPLE_BATCH_AXIS = {'x': 0, 'loss_target': 0}
SHARED_INPUTS = []
_WEIGHT_DTYPES = {'ffn1_norm': _jnp.float32, 'ffn1_w_up': _jnp.float32, 'ffn1_w_down': _jnp.float32, 'mix_norm': _jnp.float32, 'w_in': _jnp.float32, 'b_gate': _jnp.float32, 's5_lambda_re': _jnp.float32, 's5_lambda_im': _jnp.float32, 's5_log_dt': _jnp.float32, 's5_b_re': _jnp.float32, 's5_b_im': _jnp.float32, 's5_c_re': _jnp.float32, 's5_c_im': _jnp.float32, 's5_d': _jnp.float32, 's5_w_glu': _jnp.float32, 'w_br_s5': _jnp.float32, 'attn_q_gain': _jnp.float32, 'attn_k_gain': _jnp.float32, 'attn_rel_bias': _jnp.float32, 'w_br_attn': _jnp.float32, 'conv_w_dw': _jnp.float32, 'conv_b_dw': _jnp.float32, 'conv_ln_g': _jnp.float32, 'conv_ln_b': _jnp.float32, 'w_br_conv': _jnp.float32, 'w_out': _jnp.float32, 'ffn2_norm': _jnp.float32, 'ffn2_w_up': _jnp.float32, 'ffn2_w_down': _jnp.float32}
MOMENT_SCALE = {'ffn1_norm': 1.220240e+01, 'ffn1_w_up': 1.318879e-01, 'ffn1_w_down': 2.208108e-01, 'mix_norm': 3.265124e+00, 'w_in': 2.737871e-01, 'b_gate': 1.560869e+00, 's5_lambda_re': 6.395572e-02, 's5_lambda_im': 6.035991e-02, 's5_log_dt': 1.001725e+01, 's5_b_re': 4.802805e-02, 's5_b_im': 4.215301e-02, 's5_c_re': 7.599448e-02, 's5_c_im': 1.038016e-01, 's5_d': 1.146364e+01, 's5_w_glu': 2.751320e+00, 'w_br_s5': 1.469642e+00, 'attn_q_gain': 6.537019e-01, 'attn_k_gain': 6.530536e-01, 'attn_rel_bias': 2.608844e-02, 'w_br_attn': 4.927421e-01, 'conv_w_dw': 9.048759e-01, 'conv_b_dw': 1.736186e+01, 'conv_ln_g': 3.451271e+01, 'conv_ln_b': 2.438086e+01, 'w_br_conv': 1.826038e+00, 'w_out': 1.818113e+00, 'ffn2_norm': 1.228363e+01, 'ffn2_w_up': 1.512978e-01, 'ffn2_w_down': 2.332623e-01}


def _to_microbatches(a, axis):
    t = _jnp.moveaxis(a, axis, 0)
    t = t.reshape((N_MICROBATCH, t.shape[0] // N_MICROBATCH) + t.shape[1:])
    return _jnp.moveaxis(t, 1, axis + 1)


def setup_inputs(seed: int = 0) -> dict:
    inp = _fwd_setup_inputs(seed)
    key = _jax.random.fold_in(_jax.random.key(seed), 7919)
    shape, _ = _output_shape()
    out = dict(inp)
    out["loss_target"] = _jax.random.normal(_jax.random.fold_in(key, 0), shape, _jnp.float32)
    for i, name in enumerate(TWIN_WEIGHTS):
        w = inp[name].astype(_jnp.float32)
        if MOMENT_SCALE is None:
            s = _jnp.sqrt(_jnp.mean(_jnp.square(w)) + 1e-30)
        else:
            s = MOMENT_SCALE[name]
        km, kv = _jax.random.split(_jax.random.fold_in(key, i + 1))
        out[name] = w
        out["m_" + name] = s * _jax.random.normal(km, w.shape, _jnp.float32)
        out["v_" + name] = (s * s) * _jax.random.uniform(kv, w.shape, _jnp.float32, 0.5, 1.5)
    if N_MICROBATCH > 1:
        for name, axis in PER_EXAMPLE_BATCH_AXIS.items():
            out[name] = _to_microbatches(out[name], axis)
    return {'x': out['x'], 'ffn1_norm': out['ffn1_norm'], 'ffn1_w_up': out['ffn1_w_up'], 'ffn1_w_down': out['ffn1_w_down'], 'mix_norm': out['mix_norm'], 'w_in': out['w_in'], 'b_gate': out['b_gate'], 's5_lambda_re': out['s5_lambda_re'], 's5_lambda_im': out['s5_lambda_im'], 's5_log_dt': out['s5_log_dt'], 's5_b_re': out['s5_b_re'], 's5_b_im': out['s5_b_im'], 's5_c_re': out['s5_c_re'], 's5_c_im': out['s5_c_im'], 's5_d': out['s5_d'], 's5_w_glu': out['s5_w_glu'], 'w_br_s5': out['w_br_s5'], 'attn_q_gain': out['attn_q_gain'], 'attn_k_gain': out['attn_k_gain'], 'attn_rel_bias': out['attn_rel_bias'], 'w_br_attn': out['w_br_attn'], 'conv_w_dw': out['conv_w_dw'], 'conv_b_dw': out['conv_b_dw'], 'conv_ln_g': out['conv_ln_g'], 'conv_ln_b': out['conv_ln_b'], 'w_br_conv': out['w_br_conv'], 'w_out': out['w_out'], 'ffn2_norm': out['ffn2_norm'], 'ffn2_w_up': out['ffn2_w_up'], 'ffn2_w_down': out['ffn2_w_down'], 'loss_target': out['loss_target'], 'm_ffn1_norm': out['m_ffn1_norm'], 'm_ffn1_w_up': out['m_ffn1_w_up'], 'm_ffn1_w_down': out['m_ffn1_w_down'], 'm_mix_norm': out['m_mix_norm'], 'm_w_in': out['m_w_in'], 'm_b_gate': out['m_b_gate'], 'm_s5_lambda_re': out['m_s5_lambda_re'], 'm_s5_lambda_im': out['m_s5_lambda_im'], 'm_s5_log_dt': out['m_s5_log_dt'], 'm_s5_b_re': out['m_s5_b_re'], 'm_s5_b_im': out['m_s5_b_im'], 'm_s5_c_re': out['m_s5_c_re'], 'm_s5_c_im': out['m_s5_c_im'], 'm_s5_d': out['m_s5_d'], 'm_s5_w_glu': out['m_s5_w_glu'], 'm_w_br_s5': out['m_w_br_s5'], 'm_attn_q_gain': out['m_attn_q_gain'], 'm_attn_k_gain': out['m_attn_k_gain'], 'm_attn_rel_bias': out['m_attn_rel_bias'], 'm_w_br_attn': out['m_w_br_attn'], 'm_conv_w_dw': out['m_conv_w_dw'], 'm_conv_b_dw': out['m_conv_b_dw'], 'm_conv_ln_g': out['m_conv_ln_g'], 'm_conv_ln_b': out['m_conv_ln_b'], 'm_w_br_conv': out['m_w_br_conv'], 'm_w_out': out['m_w_out'], 'm_ffn2_norm': out['m_ffn2_norm'], 'm_ffn2_w_up': out['m_ffn2_w_up'], 'm_ffn2_w_down': out['m_ffn2_w_down'], 'v_ffn1_norm': out['v_ffn1_norm'], 'v_ffn1_w_up': out['v_ffn1_w_up'], 'v_ffn1_w_down': out['v_ffn1_w_down'], 'v_mix_norm': out['v_mix_norm'], 'v_w_in': out['v_w_in'], 'v_b_gate': out['v_b_gate'], 'v_s5_lambda_re': out['v_s5_lambda_re'], 'v_s5_lambda_im': out['v_s5_lambda_im'], 'v_s5_log_dt': out['v_s5_log_dt'], 'v_s5_b_re': out['v_s5_b_re'], 'v_s5_b_im': out['v_s5_b_im'], 'v_s5_c_re': out['v_s5_c_re'], 'v_s5_c_im': out['v_s5_c_im'], 'v_s5_d': out['v_s5_d'], 'v_s5_w_glu': out['v_s5_w_glu'], 'v_w_br_s5': out['v_w_br_s5'], 'v_attn_q_gain': out['v_attn_q_gain'], 'v_attn_k_gain': out['v_attn_k_gain'], 'v_attn_rel_bias': out['v_attn_rel_bias'], 'v_w_br_attn': out['v_w_br_attn'], 'v_conv_w_dw': out['v_conv_w_dw'], 'v_conv_b_dw': out['v_conv_b_dw'], 'v_conv_ln_g': out['v_conv_ln_g'], 'v_conv_ln_b': out['v_conv_ln_b'], 'v_w_br_conv': out['v_w_br_conv'], 'v_w_out': out['v_w_out'], 'v_ffn2_norm': out['v_ffn2_norm'], 'v_ffn2_w_up': out['v_ffn2_w_up'], 'v_ffn2_w_down': out['v_ffn2_w_down']}


def _loss(weights, diff, rest, loss_target):
    with _jax.named_scope("forward"):
        args = {**rest, TWIN_DIFF_INPUT: diff, **{k: w.astype(_WEIGHT_DTYPES[k]) for k, w in weights.items()}}
        y = _forward(args)
    with _jax.named_scope("loss_head"):
        err = _jnp.square(y.astype(_jnp.float32) - loss_target)
        return 0.5 * _jnp.sum(_jnp.mean(err, axis=-1)) if err.ndim else 0.5 * err


def _adamw(w, g, m, v):
    m = ADAM_B1 * m + (1.0 - ADAM_B1) * g
    v = ADAM_B2 * v + (1.0 - ADAM_B2) * _jnp.square(g)
    m_hat = m / (1.0 - ADAM_B1 ** ADAM_STEP)
    v_hat = v / (1.0 - ADAM_B2 ** ADAM_STEP)
    delta = -ADAM_LR * (m_hat / (_jnp.sqrt(v_hat) + ADAM_EPS) + ADAM_WD * w)
    return delta, m, v


def reference(x, ffn1_norm, ffn1_w_up, ffn1_w_down, mix_norm, w_in, b_gate, s5_lambda_re, s5_lambda_im, s5_log_dt, s5_b_re, s5_b_im, s5_c_re, s5_c_im, s5_d, s5_w_glu, w_br_s5, attn_q_gain, attn_k_gain, attn_rel_bias, w_br_attn, conv_w_dw, conv_b_dw, conv_ln_g, conv_ln_b, w_br_conv, w_out, ffn2_norm, ffn2_w_up, ffn2_w_down, loss_target, m_ffn1_norm, m_ffn1_w_up, m_ffn1_w_down, m_mix_norm, m_w_in, m_b_gate, m_s5_lambda_re, m_s5_lambda_im, m_s5_log_dt, m_s5_b_re, m_s5_b_im, m_s5_c_re, m_s5_c_im, m_s5_d, m_s5_w_glu, m_w_br_s5, m_attn_q_gain, m_attn_k_gain, m_attn_rel_bias, m_w_br_attn, m_conv_w_dw, m_conv_b_dw, m_conv_ln_g, m_conv_ln_b, m_w_br_conv, m_w_out, m_ffn2_norm, m_ffn2_w_up, m_ffn2_w_down, v_ffn1_norm, v_ffn1_w_up, v_ffn1_w_down, v_mix_norm, v_w_in, v_b_gate, v_s5_lambda_re, v_s5_lambda_im, v_s5_log_dt, v_s5_b_re, v_s5_b_im, v_s5_c_re, v_s5_c_im, v_s5_d, v_s5_w_glu, v_w_br_s5, v_attn_q_gain, v_attn_k_gain, v_attn_rel_bias, v_w_br_attn, v_conv_w_dw, v_conv_b_dw, v_conv_ln_g, v_conv_ln_b, v_w_br_conv, v_w_out, v_ffn2_norm, v_ffn2_w_up, v_ffn2_w_down):
    given = dict(x=x, ffn1_norm=ffn1_norm, ffn1_w_up=ffn1_w_up, ffn1_w_down=ffn1_w_down, mix_norm=mix_norm, w_in=w_in, b_gate=b_gate, s5_lambda_re=s5_lambda_re, s5_lambda_im=s5_lambda_im, s5_log_dt=s5_log_dt, s5_b_re=s5_b_re, s5_b_im=s5_b_im, s5_c_re=s5_c_re, s5_c_im=s5_c_im, s5_d=s5_d, s5_w_glu=s5_w_glu, w_br_s5=w_br_s5, attn_q_gain=attn_q_gain, attn_k_gain=attn_k_gain, attn_rel_bias=attn_rel_bias, w_br_attn=w_br_attn, conv_w_dw=conv_w_dw, conv_b_dw=conv_b_dw, conv_ln_g=conv_ln_g, conv_ln_b=conv_ln_b, w_br_conv=w_br_conv, w_out=w_out, ffn2_norm=ffn2_norm, ffn2_w_up=ffn2_w_up, ffn2_w_down=ffn2_w_down, loss_target=loss_target, m_ffn1_norm=m_ffn1_norm, m_ffn1_w_up=m_ffn1_w_up, m_ffn1_w_down=m_ffn1_w_down, m_mix_norm=m_mix_norm, m_w_in=m_w_in, m_b_gate=m_b_gate, m_s5_lambda_re=m_s5_lambda_re, m_s5_lambda_im=m_s5_lambda_im, m_s5_log_dt=m_s5_log_dt, m_s5_b_re=m_s5_b_re, m_s5_b_im=m_s5_b_im, m_s5_c_re=m_s5_c_re, m_s5_c_im=m_s5_c_im, m_s5_d=m_s5_d, m_s5_w_glu=m_s5_w_glu, m_w_br_s5=m_w_br_s5, m_attn_q_gain=m_attn_q_gain, m_attn_k_gain=m_attn_k_gain, m_attn_rel_bias=m_attn_rel_bias, m_w_br_attn=m_w_br_attn, m_conv_w_dw=m_conv_w_dw, m_conv_b_dw=m_conv_b_dw, m_conv_ln_g=m_conv_ln_g, m_conv_ln_b=m_conv_ln_b, m_w_br_conv=m_w_br_conv, m_w_out=m_w_out, m_ffn2_norm=m_ffn2_norm, m_ffn2_w_up=m_ffn2_w_up, m_ffn2_w_down=m_ffn2_w_down, v_ffn1_norm=v_ffn1_norm, v_ffn1_w_up=v_ffn1_w_up, v_ffn1_w_down=v_ffn1_w_down, v_mix_norm=v_mix_norm, v_w_in=v_w_in, v_b_gate=v_b_gate, v_s5_lambda_re=v_s5_lambda_re, v_s5_lambda_im=v_s5_lambda_im, v_s5_log_dt=v_s5_log_dt, v_s5_b_re=v_s5_b_re, v_s5_b_im=v_s5_b_im, v_s5_c_re=v_s5_c_re, v_s5_c_im=v_s5_c_im, v_s5_d=v_s5_d, v_s5_w_glu=v_s5_w_glu, v_w_br_s5=v_w_br_s5, v_attn_q_gain=v_attn_q_gain, v_attn_k_gain=v_attn_k_gain, v_attn_rel_bias=v_attn_rel_bias, v_w_br_attn=v_w_br_attn, v_conv_w_dw=v_conv_w_dw, v_conv_b_dw=v_conv_b_dw, v_conv_ln_g=v_conv_ln_g, v_conv_ln_b=v_conv_ln_b, v_w_br_conv=v_w_br_conv, v_w_out=v_w_out, v_ffn2_norm=v_ffn2_norm, v_ffn2_w_up=v_ffn2_w_up, v_ffn2_w_down=v_ffn2_w_down)
    weights = {n: given[n] for n in TWIN_WEIGHTS}
    shared = {n: given[n] for n in SHARED_INPUTS}
    per_example = {n: given[n] for n in ['x']}
    grad_fn = _jax.value_and_grad(_loss, argnums=(0, 1))

    def one_microbatch(ex, loss_target):
        ex = dict(ex)
        diff = ex.pop(TWIN_DIFF_INPUT)
        return grad_fn(weights, diff, {**shared, **ex}, loss_target)

    if N_MICROBATCH == 1:
        loss, (grad_w, grad_x) = one_microbatch(per_example, given["loss_target"])
    else:
        def body(carry, xs):
            loss_sum, grad_sum = carry
            l_k, (gw_k, gx_k) = one_microbatch(xs[0], xs[1])
            with _jax.named_scope("update"):
                return (loss_sum + l_k, _jax.tree.map(_jnp.add, grad_sum, gw_k)), gx_k

        init = (_jnp.zeros((), _jnp.float32), _jax.tree.map(_jnp.zeros_like, weights))
        (loss, grad_w), grad_x = _jax.lax.scan(body, init, (per_example, given["loss_target"]))
    with _jax.named_scope("update"):
        delta_w, new_m, new_v = {}, {}, {}
        for n in TWIN_WEIGHTS:
            delta_w[n], new_m[n], new_v[n] = _adamw(weights[n], grad_w[n], given["m_" + n], given["v_" + n])
    return (loss, grad_x, *[grad_w[n] for n in TWIN_WEIGHTS], *[delta_w[n] for n in TWIN_WEIGHTS],
            *[new_m[n] for n in TWIN_WEIGHTS], *[new_v[n] for n in TWIN_WEIGHTS])
```

```python
import functools
import math

import numpy as np
import jax
import jax.numpy as jnp
from jax import lax
from jax.experimental import pallas as pl
from jax.experimental.pallas import tpu as pltpu

F32 = jnp.float32
BF16 = jnp.bfloat16
EPS = 1e-6
VMEM_LIMIT = 56 * 1024 * 1024
LANES = 128
HEAD_DIM = 64
CHUNK = 64
N_LEFT = 8
MAX_REL = 128
ATT_TQ = 256
CONV_W = 31
HALO = 32
S5_GROUP = 16
NEG = -1e30

ADAM_LR = 0.001
ADAM_B1 = 0.9
ADAM_B2 = 0.999
ADAM_EPS = 1e-08
ADAM_WD = 0.01
ADAM_STEP = 10

MESH = pl.DeviceIdType.MESH


def _cp(*sem):
    return pltpu.CompilerParams(dimension_semantics=sem, vmem_limit_bytes=VMEM_LIMIT)


def _sds(shape, dtype):
    return jax.ShapeDtypeStruct(shape, dtype)


def _tile(n, pref):
    t = min(n, pref)
    while n % t:
        t -= 8
    return t


def _sigmoid(x):
    return jax.nn.sigmoid(x)


_GELU_C = math.sqrt(2.0 / math.pi)


def _gelu(y):
    return 0.5 * y * (1.0 + jnp.tanh(_GELU_C * (y + 0.044715 * y * y * y)))


def _gelu_grad(y):
    th = jnp.tanh(_GELU_C * (y + 0.044715 * y * y * y))
    return 0.5 * (1.0 + th) + 0.5 * y * (1.0 - th * th) * _GELU_C * (1.0 + 3.0 * 0.044715 * y * y)


def _dot(a, b):
    return jnp.dot(a, b, preferred_element_type=F32)


def _dot_t0(a, b):
    return lax.dot_general(a, b, (((0,), (0,)), ((), ())), preferred_element_type=F32)


def _dot_t1(a, b):
    return lax.dot_general(a, b, (((1,), (1,)), ((), ())), preferred_element_type=F32)


def _norm_mm(x, g, w, *, tm, tn, pieces, name):
    n, d = x.shape
    m = w.shape[1]
    mp = m // pieces
    npj = mp // tn

    def body(x_ref, g_ref, w_ref, h_ref, y_ref, h_scr):
        @pl.when(pl.program_id(1) == 0)
        def _():
            xv = x_ref[...]
            r = lax.rsqrt(jnp.mean(xv * xv, axis=-1, keepdims=True) + EPS)
            hb = (xv * r * g_ref[...]).astype(BF16)
            h_scr[...] = hb
            h_ref[...] = hb

        y_ref[...] = _dot(h_scr[...], w_ref[...]).astype(BF16)

    return pl.pallas_call(
        body, grid=(n // tm, m // tn),
        in_specs=[pl.BlockSpec((tm, d), lambda i, j: (i, 0)),
                  pl.BlockSpec((1, d), lambda i, j: (0, 0)),
                  pl.BlockSpec((d, tn), lambda i, j: (0, j))],
        out_specs=[pl.BlockSpec((tm, d), lambda i, j: (i, 0)),
                   pl.BlockSpec((None, tm, tn), lambda i, j: (j // npj, i, j % npj))],
        out_shape=[_sds((n, d), BF16), _sds((pieces, n, mp), BF16)],
        scratch_shapes=[pltpu.VMEM((tm, d), BF16)],
        compiler_params=_cp("parallel", "arbitrary"), name=name)(x, g, w)


def _mm(a, w, *, tm, tn, name):
    n, k = a.shape
    m = w.shape[1]

    def body(a_ref, w_ref, y_ref):
        y_ref[...] = _dot(a_ref[...], w_ref[...]).astype(BF16)

    return pl.pallas_call(
        body, grid=(n // tm, m // tn),
        in_specs=[pl.BlockSpec((tm, k), lambda i, j: (i, 0)), pl.BlockSpec((k, tn), lambda i, j: (0, j))],
        out_specs=pl.BlockSpec((tm, tn), lambda i, j: (i, j)),
        out_shape=_sds((n, m), BF16),
        compiler_params=_cp("parallel", "arbitrary"), name=name)(a, w)


def _ffn_down(ab, wd, x, *, tm, tk, name):
    _, n, dff = ab.shape
    d = x.shape[1]
    nk = dff // tk

    def body(a_ref, b_ref, wd_ref, x_ref, o_ref, acc):
        k = pl.program_id(1)
        a = a_ref[...].astype(F32)
        b = b_ref[...].astype(F32)
        act = (a * _sigmoid(a) * b).astype(BF16)
        part = _dot(act, wd_ref[...])

        @pl.when(k == 0)
        def _():
            acc[...] = part

        @pl.when(k > 0)
        def _():
            acc[...] += part

        @pl.when(k == nk - 1)
        def _():
            o_ref[...] = x_ref[...] + 0.5 * acc[...]

    return pl.pallas_call(
        body, grid=(n // tm, nk),
        in_specs=[pl.BlockSpec((None, tm, tk), lambda i, k: (0, i, k)),
                  pl.BlockSpec((None, tm, tk), lambda i, k: (1, i, k)),
                  pl.BlockSpec((tk, d), lambda i, k: (k, 0)),
                  pl.BlockSpec((tm, d), lambda i, k: (i, 0))],
        out_specs=pl.BlockSpec((tm, d), lambda i, k: (i, 0)),
        out_shape=_sds((n, d), F32),
        scratch_shapes=[pltpu.VMEM((tm, d), F32)],
        compiler_params=_cp("parallel", "arbitrary"), name=name)(ab, ab, wd, x)


def _ffn_dact(dx, wdT, ab, *, tm, tk, name):
    n, d = dx.shape
    dff = ab.shape[2]

    def body(dx_ref, wdT_ref, a_ref, b_ref, dab_ref, act_ref, do_ref):
        do = (0.5 * dx_ref[...]).astype(BF16)

        @pl.when(pl.program_id(1) == 0)
        def _():
            do_ref[...] = do

        dact = _dot(do, wdT_ref[...])
        a = a_ref[...].astype(F32)
        b = b_ref[...].astype(F32)
        sg = _sigmoid(a)
        silu = a * sg
        act_ref[...] = (silu * b).astype(BF16)
        dab_ref[0] = (dact * b * (sg * (1.0 + a * (1.0 - sg)))).astype(BF16)
        dab_ref[1] = (dact * silu).astype(BF16)

    return pl.pallas_call(
        body, grid=(n // tm, dff // tk),
        in_specs=[pl.BlockSpec((tm, d), lambda i, j: (i, 0)),
                  pl.BlockSpec((d, tk), lambda i, j: (0, j)),
                  pl.BlockSpec((None, tm, tk), lambda i, j: (0, i, j)),
                  pl.BlockSpec((None, tm, tk), lambda i, j: (1, i, j))],
        out_specs=[pl.BlockSpec((2, tm, tk), lambda i, j: (0, i, j)),
                   pl.BlockSpec((tm, tk), lambda i, j: (i, j)),
                   pl.BlockSpec((tm, d), lambda i, j: (i, 0))],
        out_shape=[_sds((2, n, dff), BF16), _sds((n, dff), BF16), _sds((n, d), BF16)],
        compiler_params=_cp("parallel", "arbitrary"), name=name)(dx, wdT, ab, ab)


def _mm_rmsbwd(dy, wT, x, g, dres, *, tm, tk, name):
    p, n, mp = dy.shape
    d = x.shape[1]
    nkp = mp // tk
    nk = p * nkp

    def body(dy_ref, wT_ref, x_ref, g_ref, dres_ref, dx_ref, dg_ref, acc):
        i = pl.program_id(0)
        k = pl.program_id(1)
        part = _dot(dy_ref[...], wT_ref[...])

        @pl.when(k == 0)
        def _():
            acc[...] = part

        @pl.when(k > 0)
        def _():
            acc[...] += part

        @pl.when(k == nk - 1)
        def _():
            dh = acc[...]
            xv = x_ref[...]
            r = lax.rsqrt(jnp.mean(xv * xv, axis=-1, keepdims=True) + EPS)
            xn = xv * r
            dgp = jnp.sum(dh * xn, axis=0, keepdims=True)
            dxh = dh * g_ref[...]
            dx_ref[...] = dres_ref[...] + r * (dxh - xn * jnp.mean(dxh * xn, axis=-1, keepdims=True))

            @pl.when(i == 0)
            def _():
                dg_ref[...] = dgp

            @pl.when(i > 0)
            def _():
                dg_ref[...] += dgp

    return pl.pallas_call(
        body, grid=(n // tm, nk),
        in_specs=[pl.BlockSpec((None, tm, tk), lambda i, k: (k // nkp, i, k % nkp)),
                  pl.BlockSpec((tk, d), lambda i, k: (k, 0)),
                  pl.BlockSpec((tm, d), lambda i, k: (i, 0)),
                  pl.BlockSpec((1, d), lambda i, k: (0, 0)),
                  pl.BlockSpec((tm, d), lambda i, k: (i, 0))],
        out_specs=[pl.BlockSpec((tm, d), lambda i, k: (i, 0)),
                   pl.BlockSpec((1, d), lambda i, k: (0, 0))],
        out_shape=[_sds((n, d), F32), _sds((1, d), F32)],
        scratch_shapes=[pltpu.VMEM((tm, d), F32)],
        compiler_params=_cp("arbitrary", "arbitrary"), name=name)(dy, wT, x, g, dres)


def _mm_tn(a, b, *, ta, tb, tk, name):
    pa, n, ka = a.shape
    pb, _, kb = b.shape
    nap = ka // ta
    nbp = kb // tb

    def body(a_ref, b_ref, o_ref):
        @pl.when(pl.program_id(2) == 0)
        def _():
            o_ref[...] = jnp.zeros_like(o_ref)

        o_ref[...] += _dot_t0(a_ref[...], b_ref[...])

    return pl.pallas_call(
        body, grid=(pa * nap, pb * nbp, n // tk),
        in_specs=[pl.BlockSpec((None, tk, ta), lambda i, j, k: (i // nap, k, i % nap)),
                  pl.BlockSpec((None, tk, tb), lambda i, j, k: (j // nbp, k, j % nbp))],
        out_specs=pl.BlockSpec((ta, tb), lambda i, j, k: (i, j)),
        out_shape=_sds((pa * ka, pb * kb), F32),
        compiler_params=_cp("parallel", "parallel", "arbitrary"), name=name)(a, b)


def _loss_grad(y, t, *, tm, name):
    n, d = y.shape

    def body(y_ref, t_ref, dy_ref, l_ref):
        e = y_ref[...] - t_ref[...]
        dy_ref[...] = e * (1.0 / d)
        part = jnp.sum(e * e, axis=0, keepdims=True)

        @pl.when(pl.program_id(0) == 0)
        def _():
            l_ref[...] = part

        @pl.when(pl.program_id(0) > 0)
        def _():
            l_ref[...] += part

    return pl.pallas_call(
        body, grid=(n // tm,),
        in_specs=[pl.BlockSpec((tm, d), lambda i: (i, 0)), pl.BlockSpec((tm, d), lambda i: (i, 0))],
        out_specs=[pl.BlockSpec((tm, d), lambda i: (i, 0)), pl.BlockSpec((1, d), lambda i: (0, 0))],
        out_shape=[_sds((n, d), F32), _sds((1, d), F32)],
        compiler_params=_cp("arbitrary"), name=name)(y, t)


def _s5_fwd(proj, sp, wglu, *, bl, s, t, ucol, name):
    n = bl * s
    ds5, gp = sp["bblk_r"].shape
    nt = s // t
    nlog = int(math.log2(t))

    def body(u_ref, br_ref, bi_ref, a_ref, pw_ref, cr_ref, ci_ref, d_ref, wg_ref,
             xr_ref, xi_ref, yp_ref, zg_ref, o_ref, carry):
        @pl.when(pl.program_id(1) == 0)
        def _():
            carry[...] = jnp.zeros_like(carry)

        u = u_ref[...]
        rows = lax.broadcasted_iota(jnp.int32, (t, gp), 0)
        ar = a_ref[0:1, :]
        ai = a_ref[1:2, :]
        cr = carry[0:1, :]
        ci = carry[1:2, :]
        first = rows == 0
        xr = _dot(u, br_ref[...]) + jnp.where(first, ar * cr - ai * ci, 0.0)
        xi = _dot(u, bi_ref[...]) + jnp.where(first, ar * ci + ai * cr, 0.0)
        for k in range(nlog):
            sh = 1 << k
            pr = pw_ref[2 * k:2 * k + 1, :]
            pi = pw_ref[2 * k + 1:2 * k + 2, :]
            keep = rows >= sh
            sr = jnp.where(keep, pltpu.roll(xr, sh, 0), 0.0)
            si = jnp.where(keep, pltpu.roll(xi, sh, 0), 0.0)
            xr, xi = xr + pr * sr - pi * si, xi + pr * si + pi * sr
        last = rows == t - 1
        carry[0:1, :] = jnp.sum(jnp.where(last, xr, 0.0), axis=0, keepdims=True)
        carry[1:2, :] = jnp.sum(jnp.where(last, xi, 0.0), axis=0, keepdims=True)
        xr_ref[...] = xr
        xi_ref[...] = xi
        y = _dot(xr.astype(BF16), cr_ref[...]) + _dot(xi.astype(BF16), ci_ref[...]) + d_ref[...] * u.astype(F32)
        yp_ref[...] = y
        zg = _dot(_gelu(y).astype(BF16), wg_ref[...])
        zg_ref[...] = zg
        o_ref[...] = (zg[:, :ds5] * _sigmoid(zg[:, ds5:])).astype(BF16)

    const = lambda shape: pl.BlockSpec(shape, lambda b, i: (0, 0))
    row = lambda w: pl.BlockSpec((t, w), lambda b, i: (b * nt + i, 0))
    return pl.pallas_call(
        body, grid=(bl, nt),
        in_specs=[pl.BlockSpec((t, ds5), lambda b, i: (b * nt + i, ucol)),
                  const((ds5, gp)), const((ds5, gp)), const((2, gp)), const((2 * nlog, gp)),
                  const((gp, ds5)), const((gp, ds5)), const((1, ds5)), const((ds5, 2 * ds5))],
        out_specs=[row(gp), row(gp), row(ds5), row(2 * ds5), row(ds5)],
        out_shape=[_sds((n, gp), F32), _sds((n, gp), F32), _sds((n, ds5), F32), _sds((n, 2 * ds5), F32),
                   _sds((n, ds5), BF16)],
        scratch_shapes=[pltpu.VMEM((2, gp), F32)],
        compiler_params=_cp("arbitrary", "arbitrary"), name=name)(
            proj, sp["bblk_r"], sp["bblk_i"], sp["a"], sp["pw"], sp["cblk_r"], sp["cblk_in"], sp["d"], wglu)


def _s5_bwd(ds, yp, zg, xr, xi, proj, sp, wgluT, *, bl, s, t, ucol, name):
    n = bl * s
    ds5, gp = sp["bblk_r"].shape
    nt = s // t
    nlog = int(math.log2(t))
    tb = t // 8

    def body(ds_ref, yp_ref, zg_ref, xr_ref, xi_ref, hr_ref, hi_ref, u_ref, wgT_ref, crT_ref, ciT_ref,
             brT_ref, biT_ref, a_ref, pw_ref, d_ref,
             du_ref, dwg_ref, dd_ref, dcr_ref, dci_ref, dbr_ref, dbi_ref, da_ref, carry):
        b = pl.program_id(0)
        i = pl.program_id(1)
        tile = nt - 1 - i

        @pl.when((b == 0) & (i == 0))
        def _():
            for r in (dwg_ref, dd_ref, dcr_ref, dci_ref, dbr_ref, dbi_ref, da_ref):
                r[...] = jnp.zeros_like(r)

        @pl.when(i == 0)
        def _():
            carry[...] = jnp.zeros_like(carry)

        dsv = ds_ref[...].astype(F32)
        zgv = zg_ref[...]
        za = zgv[:, :ds5]
        sg = _sigmoid(zgv[:, ds5:])
        dzg = jnp.concatenate([dsv * sg, dsv * za * sg * (1.0 - sg)], axis=1).astype(BF16)
        y = yp_ref[...]
        dwg_ref[...] += _dot_t0(_gelu(y).astype(BF16), dzg)
        dy = _dot(dzg, wgT_ref[...]) * _gelu_grad(y)
        ub = u_ref[...]
        uf = ub.astype(F32)
        dd_ref[...] += jnp.sum(dy * uf, axis=0, keepdims=True)
        dyb = dy.astype(BF16)
        xrv = xr_ref[...]
        xiv = xi_ref[...]
        dcr_ref[...] += _dot_t0(xrv.astype(BF16), dyb)
        dci_ref[...] += _dot_t0(xiv.astype(BF16), dyb)

        rows = lax.broadcasted_iota(jnp.int32, (t, gp), 0)
        ar = a_ref[0:1, :]
        ai = a_ref[1:2, :]
        cr = carry[0:1, :]
        ci = carry[1:2, :]
        last = rows == t - 1
        gr = _dot(dyb, crT_ref[...]) + jnp.where(last, ar * cr + ai * ci, 0.0)
        gi = _dot(dyb, ciT_ref[...]) + jnp.where(last, ar * ci - ai * cr, 0.0)
        for k in range(nlog):
            sh = 1 << k
            pr = pw_ref[2 * k:2 * k + 1, :]
            pi = pw_ref[2 * k + 1:2 * k + 2, :]
            keep = rows < t - sh
            sr = jnp.where(keep, pltpu.roll(gr, t - sh, 0), 0.0)
            si = jnp.where(keep, pltpu.roll(gi, t - sh, 0), 0.0)
            gr, gi = gr + pr * sr + pi * si, gi + pr * si - pi * sr
        first = rows == 0
        carry[0:1, :] = jnp.sum(jnp.where(first, gr, 0.0), axis=0, keepdims=True)
        carry[1:2, :] = jnp.sum(jnp.where(first, gi, 0.0), axis=0, keepdims=True)

        live = jnp.where(tile > 0, 1.0, 0.0)
        xpr = jnp.where(first, hr_ref[7:8, :] * live, pltpu.roll(xrv, 1, 0))
        xpi = jnp.where(first, hi_ref[7:8, :] * live, pltpu.roll(xiv, 1, 0))
        da_ref[0:1, :] += jnp.sum(gr * xpr + gi * xpi, axis=0, keepdims=True)
        da_ref[1:2, :] += jnp.sum(gi * xpr - gr * xpi, axis=0, keepdims=True)

        grb = gr.astype(BF16)
        gib = gi.astype(BF16)
        dbr_ref[...] += _dot_t0(ub, grb)
        dbi_ref[...] += _dot_t0(ub, gib)
        du_ref[...] = (_dot(grb, brT_ref[...]) + _dot(gib, biT_ref[...]) + dy * d_ref[...]).astype(BF16)

    const = lambda shape: pl.BlockSpec(shape, lambda b, i: (0, 0))
    row = lambda w: pl.BlockSpec((t, w), lambda b, i: (b * nt + nt - 1 - i, 0))
    halo = pl.BlockSpec((8, gp), lambda b, i: (jnp.maximum((b * nt + nt - 1 - i) * tb - 1, 0), 0))
    return pl.pallas_call(
        body, grid=(bl, nt),
        in_specs=[row(ds5), row(ds5), row(2 * ds5), row(gp), row(gp), halo, halo,
                  pl.BlockSpec((t, ds5), lambda b, i: (b * nt + nt - 1 - i, ucol)),
                  const((2 * ds5, ds5)), const((ds5, gp)), const((ds5, gp)), const((gp, ds5)), const((gp, ds5)),
                  const((2, gp)), const((2 * nlog, gp)), const((1, ds5))],
        out_specs=[row(ds5), const((ds5, 2 * ds5)), const((1, ds5)), const((gp, ds5)), const((gp, ds5)),
                   const((ds5, gp)), const((ds5, gp)), const((2, gp))],
        out_shape=[_sds((n, ds5), BF16), _sds((ds5, 2 * ds5), F32), _sds((1, ds5), F32), _sds((gp, ds5), F32),
                   _sds((gp, ds5), F32), _sds((ds5, gp), F32), _sds((ds5, gp), F32), _sds((2, gp), F32)],
        scratch_shapes=[pltpu.VMEM((2, gp), F32)],
        compiler_params=_cp("arbitrary", "arbitrary"), name=name)(
            ds, yp, zg, xr, xi, xr, xi, proj, wgluT, sp["cblk_r"].T, sp["cblk_in"].T,
            sp["bblk_r"].T, sp["bblk_i"].T, sp["a"], sp["pw"], sp["d"])


def _head_norm(x, gain2, first):
    x2 = x * x
    sa = jnp.sum(jnp.where(first, x2, 0.0), axis=-1, keepdims=True)
    sb = jnp.sum(jnp.where(first, 0.0, x2), axis=-1, keepdims=True)
    r = jnp.where(first, lax.rsqrt(sa * (1.0 / HEAD_DIM) + EPS), lax.rsqrt(sb * (1.0 / HEAD_DIM) + EPS))
    return x * r, r


def _attn_specs(bl, s, datt):
    nq = s // ATT_TQ
    nb = datt // LANES
    col = lambda blk: (lambda b, h, q: (b * nq + q, blk * nb + h))
    win = lambda blk, j: (lambda b, h, q: (b * nq + jnp.maximum(q - 2 + j, 0), blk * nb + h))
    tile = lambda f: pl.BlockSpec((ATT_TQ, LANES), f)
    qs = tile(col(0))
    ks = [tile(win(1, j)) for j in range(3)]
    vs = [tile(win(2, j)) for j in range(3)]
    return nq, nb, qs, ks, vs


def _attn_probs(q_ref, k_refs, gq_ref, gk_ref, bias_ref):
    qt = pl.program_id(2)
    lane = lax.broadcasted_iota(jnp.int32, (1, LANES), 1)
    first = lane < HEAD_DIM
    qh, rq = _head_norm(q_ref[...].astype(F32), None, first)
    qn = qh * gq_ref[...]
    kc = jnp.concatenate([r[...] for r in k_refs], axis=0).astype(F32)
    kh, rk = _head_norm(kc, None, first)
    kn = (kh * gk_ref[...]).astype(BF16)
    kpos = (qt - 2) * ATT_TQ + lax.broadcasted_iota(jnp.int32, (1, 3 * ATT_TQ), 1)
    valid = kpos >= 0
    scale = HEAD_DIM ** -0.5
    masks = (first, jnp.logical_not(first))
    qas, ps = [], []
    for hh in range(2):
        qa = jnp.where(masks[hh], qn, 0.0).astype(BF16)
        sc = _dot_t1(qa, kn) * scale + bias_ref[hh]
        sc = jnp.where(valid, sc, NEG)
        e = jnp.exp(sc - jnp.max(sc, axis=-1, keepdims=True))
        ps.append(e / jnp.sum(e, axis=-1, keepdims=True))
        qas.append(qa)
    return first, masks, qh, rq, kh, rk, kn, qas, ps


def _attn_fwd(proj, gq2, gk2, bias, *, bl, s, datt, name):
    n = bl * s
    nq, nb, qs, ks, vs = _attn_specs(bl, s, datt)

    def body(q_ref, k0, k1, k2, v0, v1, v2, gq_ref, gk_ref, bias_ref, o_ref):
        first, masks, _, _, _, _, _, _, ps = _attn_probs(q_ref, (k0, k1, k2), gq_ref, gk_ref, bias_ref)
        vc = jnp.concatenate([v0[...], v1[...], v2[...]], axis=0)
        o0 = _dot(ps[0].astype(BF16), vc)
        o1 = _dot(ps[1].astype(BF16), vc)
        o_ref[...] = jnp.where(first, o0, o1).astype(BF16)

    gs = pl.BlockSpec((1, LANES), lambda b, h, q: (0, 0))
    return pl.pallas_call(
        body, grid=(bl, nb, nq),
        in_specs=[qs, *ks, *vs, gs, gs, pl.BlockSpec((2, ATT_TQ, 3 * ATT_TQ), lambda b, h, q: (h, 0, 0))],
        out_specs=pl.BlockSpec((ATT_TQ, LANES), lambda b, h, q: (b * nq + q, h)),
        out_shape=_sds((n, datt), BF16),
        compiler_params=_cp("parallel", "parallel", "arbitrary"), name=name)(
            proj, proj, proj, proj, proj, proj, proj, gq2, gk2, bias)


def _attn_bwd(do, proj, gq2, gk2, bias, *, bl, s, datt, name):
    n = bl * s
    nq, nb, qs, ks, vs = _attn_specs(bl, s, datt)
    sp = s + 2 * ATT_TQ
    scale = HEAD_DIM ** -0.5

    def body(do_ref, q_ref, k0, k1, k2, v0, v1, v2, gq_ref, gk_ref, bias_ref,
             dq_ref, dk_ref, dv_ref, db_ref, dgq_ref):
        qt = pl.program_id(2)

        @pl.when(qt == 0)
        def _():
            dk_ref[...] = jnp.zeros_like(dk_ref)
            dv_ref[...] = jnp.zeros_like(dv_ref)
            db_ref[...] = jnp.zeros_like(db_ref)
            dgq_ref[...] = jnp.zeros_like(dgq_ref)

        first, masks, qh, rq, kh, rk, kn, qas, ps = _attn_probs(q_ref, (k0, k1, k2), gq_ref, gk_ref, bias_ref)
        vc = jnp.concatenate([v0[...], v1[...], v2[...]], axis=0)
        dov = do_ref[...]
        dqn = jnp.zeros((ATT_TQ, LANES), F32)
        dkn = jnp.zeros((3 * ATT_TQ, LANES), F32)
        dv = jnp.zeros((3 * ATT_TQ, LANES), F32)
        for hh in range(2):
            doa = jnp.where(masks[hh], dov, jnp.zeros_like(dov))
            p = ps[hh]
            dp = _dot_t1(doa, vc)
            dsm = p * (dp - jnp.sum(dp * p, axis=-1, keepdims=True))
            db_ref[hh] += dsm
            dsc = (dsm * scale).astype(BF16)
            dqn = dqn + _dot(dsc, jnp.where(masks[hh], kn, jnp.zeros_like(kn)))
            dkn = dkn + _dot_t0(dsc, qas[hh])
            dv = dv + _dot_t0(p.astype(BF16), doa)
        start = pl.multiple_of(qt * ATT_TQ, ATT_TQ)
        dk_ref[pl.ds(start, 3 * ATT_TQ), :] += dkn
        dv_ref[pl.ds(start, 3 * ATT_TQ), :] += dv
        dgq_ref[...] += jnp.sum(dqn * qh, axis=0, keepdims=True)
        dqh = dqn * gq_ref[...]
        t = dqh * qh
        ma = jnp.sum(jnp.where(first, t, 0.0), axis=-1, keepdims=True) * (1.0 / HEAD_DIM)
        mb = jnp.sum(jnp.where(first, 0.0, t), axis=-1, keepdims=True) * (1.0 / HEAD_DIM)
        dq_ref[...] = (rq * (dqh - qh * jnp.where(first, ma, mb))).astype(BF16)

    gs = pl.BlockSpec((1, LANES), lambda b, h, q: (0, 0))
    acc = pl.BlockSpec((None, sp, LANES), lambda b, h, q: (b, 0, h))
    return pl.pallas_call(
        body, grid=(bl, nb, nq),
        in_specs=[pl.BlockSpec((ATT_TQ, LANES), lambda b, h, q: (b * nq + q, h)), qs, *ks, *vs, gs, gs,
                  pl.BlockSpec((2, ATT_TQ, 3 * ATT_TQ), lambda b, h, q: (h, 0, 0))],
        out_specs=[pl.BlockSpec((ATT_TQ, LANES), lambda b, h, q: (b * nq + q, h)), acc, acc,
                   pl.BlockSpec((None, 2, ATT_TQ, 3 * ATT_TQ), lambda b, h, q: (b, h, 0, 0)),
                   pl.BlockSpec((None, None, 1, LANES), lambda b, h, q: (b, h, 0, 0))],
        out_shape=[_sds((n, datt), BF16), _sds((bl, sp, datt), F32), _sds((bl, sp, datt), F32),
                   _sds((bl, 2 * nb, ATT_TQ, 3 * ATT_TQ), F32), _sds((bl, nb, 1, LANES), F32)],
        compiler_params=_cp("arbitrary", "arbitrary", "arbitrary"), name=name)(
            do, proj, proj, proj, proj, proj, proj, proj, gq2, gk2, bias)


def _attn_kv_bwd(dkn, dv, proj, gk, *, bl, s, datt, tm, name):
    n = bl * s
    ns = s // tm
    off = 2 * ATT_TQ // tm
    nb = datt // LANES

    def body(dkn_ref, dv_ref, k_ref, gk_ref, dk_ref, dvo_ref, dgk_ref):
        lane = lax.broadcasted_iota(jnp.int32, (1, LANES), 1)
        first = lane < HEAD_DIM

        @pl.when((pl.program_id(0) == 0) & (pl.program_id(1) == 0))
        def _():
            dgk_ref[...] = jnp.zeros_like(dgk_ref)

        dvo_ref[...] = dv_ref[...].astype(BF16)
        for c in range(nb):
            sl = slice(c * LANES, (c + 1) * LANES)
            kh, rk = _head_norm(k_ref[:, sl].astype(F32), None, first)
            dn = dkn_ref[:, sl]
            dgk_ref[:, sl] += jnp.sum(dn * kh, axis=0, keepdims=True)
            dh = dn * gk_ref[:, sl]
            t = dh * kh
            ma = jnp.sum(jnp.where(first, t, 0.0), axis=-1, keepdims=True) * (1.0 / HEAD_DIM)
            mb = jnp.sum(jnp.where(first, 0.0, t), axis=-1, keepdims=True) * (1.0 / HEAD_DIM)
            dk_ref[:, sl] = (rk * (dh - kh * jnp.where(first, ma, mb))).astype(BF16)

    accs = pl.BlockSpec((None, tm, datt), lambda b, i: (b, i + off, 0))
    outs = pl.BlockSpec((tm, datt), lambda b, i: (b * ns + i, 0))
    return pl.pallas_call(
        body, grid=(bl, ns),
        in_specs=[accs, accs, pl.BlockSpec((tm, datt), lambda b, i: (b * ns + i, 1)),
                  pl.BlockSpec((1, datt), lambda b, i: (0, 0))],
        out_specs=[outs, outs, pl.BlockSpec((1, datt), lambda b, i: (0, 0))],
        out_shape=[_sds((n, datt), BF16), _sds((n, datt), BF16), _sds((1, datt), F32)],
        compiler_params=_cp("arbitrary", "arbitrary"), name=name)(dkn, dv, proj, gk)


def _conv_fwd(proj, wdw, bdw, lng, lnb, *, bl, s, t, zcol, name):
    n = bl * s
    dc = wdw.shape[1]
    nt = s // t
    hb = t // HALO

    def body(z_ref, zh_ref, w_ref, b_ref, g_ref, be_ref, hg_ref, hc_ref, o_ref, ext):
        i = pl.program_id(1)
        z = z_ref[...].astype(F32)
        hg = z[:, :dc] * _sigmoid(z[:, dc:])
        zh = zh_ref[...].astype(F32)
        live = jnp.where(i > 0, 1.0, 0.0)
        ext[0:HALO, :] = zh[:, :dc] * _sigmoid(zh[:, dc:]) * live
        ext[HALO:HALO + t, :] = hg
        hg_ref[...] = hg
        acc = jnp.zeros((t, dc), F32) + b_ref[...]
        for j in range(CONV_W):
            acc = acc + w_ref[j:j + 1, :] * ext[pl.ds(HALO - (CONV_W - 1) + j, t), :]
        hc_ref[...] = acc
        mu = jnp.mean(acc, axis=-1, keepdims=True)
        xc = acc - mu
        rs = lax.rsqrt(jnp.mean(xc * xc, axis=-1, keepdims=True) + EPS)
        ln = xc * rs * g_ref[...] + be_ref[...]
        o_ref[...] = (ln * _sigmoid(ln)).astype(BF16)

    vec = pl.BlockSpec((1, dc), lambda b, i: (0, 0))
    row = pl.BlockSpec((t, dc), lambda b, i: (b * nt + i, 0))
    return pl.pallas_call(
        body, grid=(bl, nt),
        in_specs=[pl.BlockSpec((t, 2 * dc), lambda b, i: (b * nt + i, zcol)),
                  pl.BlockSpec((HALO, 2 * dc), lambda b, i: (jnp.maximum((b * nt + i) * hb - 1, 0), zcol)),
                  pl.BlockSpec((HALO, dc), lambda b, i: (0, 0)), vec, vec, vec],
        out_specs=[row, row, row],
        out_shape=[_sds((n, dc), F32), _sds((n, dc), F32), _sds((n, dc), BF16)],
        scratch_shapes=[pltpu.VMEM((HALO + t, dc), F32)],
        compiler_params=_cp("parallel", "arbitrary"), name=name)(proj, proj, wdw, bdw, lng, lnb)


def _conv_bwd_ln(dco, hc, lng, lnb, *, tm, name):
    n, dc = hc.shape

    def body(d_ref, hc_ref, g_ref, be_ref, dhc_ref, dg_ref, db_ref):
        @pl.when(pl.program_id(0) == 0)
        def _():
            dg_ref[...] = jnp.zeros_like(dg_ref)
            db_ref[...] = jnp.zeros_like(db_ref)

        hcv = hc_ref[...]
        mu = jnp.mean(hcv, axis=-1, keepdims=True)
        xc = hcv - mu
        rs = lax.rsqrt(jnp.mean(xc * xc, axis=-1, keepdims=True) + EPS)
        xh = xc * rs
        ln = xh * g_ref[...] + be_ref[...]
        sg = _sigmoid(ln)
        dln = d_ref[...].astype(F32) * (sg * (1.0 + ln * (1.0 - sg)))
        db_ref[...] += jnp.sum(dln, axis=0, keepdims=True)
        dg_ref[...] += jnp.sum(dln * xh, axis=0, keepdims=True)
        dxh = dln * g_ref[...]
        dhc_ref[...] = rs * (dxh - jnp.mean(dxh, axis=-1, keepdims=True)
                             - xh * jnp.mean(dxh * xh, axis=-1, keepdims=True))

    vec = pl.BlockSpec((1, dc), lambda i: (0, 0))
    row = pl.BlockSpec((tm, dc), lambda i: (i, 0))
    return pl.pallas_call(
        body, grid=(n // tm,), in_specs=[row, row, vec, vec], out_specs=[row, vec, vec],
        out_shape=[_sds((n, dc), F32), _sds((1, dc), F32), _sds((1, dc), F32)],
        compiler_params=_cp("arbitrary"), name=name)(dco, hc, lng, lnb)


def _conv_bwd_dw(dhc, hg, proj, wdw, *, bl, s, t, zcol, name):
    n = bl * s
    dc = wdw.shape[1]
    nt = s // t
    hb = t // HALO
    lastblk = n // HALO - 1

    def body(d_ref, dn_ref, hg_ref, hp_ref, z_ref, w_ref, dz_ref, dw_ref, dbias_ref, extd, exth):
        b = pl.program_id(0)
        i = pl.program_id(1)

        @pl.when((b == 0) & (i == 0))
        def _():
            dw_ref[...] = jnp.zeros_like(dw_ref)
            dbias_ref[...] = jnp.zeros_like(dbias_ref)

        dv = d_ref[...]
        extd[0:t, :] = dv
        extd[t:t + HALO, :] = dn_ref[...] * jnp.where(i < nt - 1, 1.0, 0.0)
        exth[0:HALO, :] = hp_ref[...] * jnp.where(i > 0, 1.0, 0.0)
        exth[HALO:HALO + t, :] = hg_ref[...]
        dbias_ref[...] += jnp.sum(dv, axis=0, keepdims=True)
        dhg = jnp.zeros((t, dc), F32)
        for j in range(CONV_W):
            dhg = dhg + w_ref[j:j + 1, :] * extd[pl.ds(CONV_W - 1 - j, t), :]
            dw_ref[j:j + 1, :] += jnp.sum(dv * exth[pl.ds(HALO - (CONV_W - 1) + j, t), :], axis=0, keepdims=True)
        z = z_ref[...].astype(F32)
        za = z[:, :dc]
        sg = _sigmoid(z[:, dc:])
        dz_ref[...] = jnp.concatenate([dhg * sg, dhg * za * sg * (1.0 - sg)], axis=1).astype(BF16)

    row = pl.BlockSpec((t, dc), lambda b, i: (b * nt + i, 0))
    nxt = pl.BlockSpec((HALO, dc), lambda b, i: (jnp.minimum((b * nt + i + 1) * hb, lastblk), 0))
    prv = pl.BlockSpec((HALO, dc), lambda b, i: (jnp.maximum((b * nt + i) * hb - 1, 0), 0))
    wsp = pl.BlockSpec((HALO, dc), lambda b, i: (0, 0))
    return pl.pallas_call(
        body, grid=(bl, nt),
        in_specs=[row, nxt, row, prv, pl.BlockSpec((t, 2 * dc), lambda b, i: (b * nt + i, zcol)), wsp],
        out_specs=[pl.BlockSpec((t, 2 * dc), lambda b, i: (b * nt + i, 0)), wsp,
                   pl.BlockSpec((1, dc), lambda b, i: (0, 0))],
        out_shape=[_sds((n, 2 * dc), BF16), _sds((HALO, dc), F32), _sds((1, dc), F32)],
        scratch_shapes=[pltpu.VMEM((t + HALO, dc), F32), pltpu.VMEM((HALO + t, dc), F32)],
        compiler_params=_cp("arbitrary", "arbitrary"), name=name)(dhc, dhc, hg, hg, proj, wdw)


def _mix_out_fwd(x, brs, gl, bg, wbs, wout, *, tm, name):
    n, d = x.shape

    def body(x_ref, s_ref, a_ref, c_ref, g0, g1, g2, bg_ref, ws, wa, wc, wo, o_ref):
        merged = jnp.zeros((tm, d), F32)
        for k, (br, gr, w) in enumerate(((s_ref, g0, ws), (a_ref, g1, wa), (c_ref, g2, wc))):
            gate = _sigmoid(gr[...].astype(F32) + bg_ref[:, k * d:(k + 1) * d])
            merged = merged + gate * _dot(br[...], w[...])
        o_ref[...] = x_ref[...] + _dot(merged.astype(BF16), wo[...])

    row = lambda w: pl.BlockSpec((tm, w), lambda i: (i, 0))
    full = lambda a: pl.BlockSpec(a.shape, lambda i: (0, 0))
    gls = [pl.BlockSpec((tm, d), functools.partial(lambda k, i: (i, k), k)) for k in range(3)]
    return pl.pallas_call(
        body, grid=(n // tm,),
        in_specs=[row(d), *[row(b.shape[1]) for b in brs], *gls, full(bg), *[full(w) for w in wbs], full(wout)],
        out_specs=row(d), out_shape=_sds((n, d), F32),
        compiler_params=_cp("parallel"), name=name)(x, *brs, gl, gl, gl, bg, *wbs, wout)


def _mix_out_bwd(dx, brs, gl, bg, wbs, wbTs, wout_t, *, tm, name):
    n, d = dx.shape
    widths = [b.shape[1] for b in brs]

    def body(dx_ref, s_ref, a_ref, c_ref, g0, g1, g2, bg_ref, ws, wa, wc, wsT, waT, wcT, woT,
             ds_ref, da_ref, dc_ref, dgl_ref, dbg_ref, dws, dwa, dwc, dwo):
        @pl.when(pl.program_id(0) == 0)
        def _():
            for r in (dbg_ref, dws, dwa, dwc, dwo):
                r[...] = jnp.zeros_like(r)

        dxb = dx_ref[...].astype(BF16)
        dm = _dot(dxb, woT[...])
        merged = jnp.zeros((tm, d), F32)
        for k, (br, gr, w, wT, dbr, dw) in enumerate(((s_ref, g0, ws, wsT, ds_ref, dws),
                                                       (a_ref, g1, wa, waT, da_ref, dwa),
                                                       (c_ref, g2, wc, wcT, dc_ref, dwc))):
            gate = _sigmoid(gr[...].astype(F32) + bg_ref[:, k * d:(k + 1) * d])
            brv = br[...]
            y = _dot(brv, w[...])
            merged = merged + gate * y
            dyb = (dm * gate).astype(BF16)
            dbr[...] = _dot(dyb, wT[...]).astype(BF16)
            dw[...] += _dot_t0(brv, dyb)
            dgl = dm * y * gate * (1.0 - gate)
            dgl_ref[:, k * d:(k + 1) * d] = dgl.astype(BF16)
            dbg_ref[:, k * d:(k + 1) * d] += jnp.sum(dgl, axis=0, keepdims=True)
        dwo[...] += _dot_t0(merged.astype(BF16), dxb)

    row = lambda w: pl.BlockSpec((tm, w), lambda i: (i, 0))
    full = lambda shape: pl.BlockSpec(shape, lambda i: (0, 0))
    gls = [pl.BlockSpec((tm, d), functools.partial(lambda k, i: (i, k), k)) for k in range(3)]
    return pl.pallas_call(
        body, grid=(n // tm,),
        in_specs=[row(d), *[row(w) for w in widths], *gls, full(bg.shape), *[full(w.shape) for w in wbs],
                  *[full(w.shape) for w in wbTs], full(wout_t.shape)],
        out_specs=[*[row(w) for w in widths], row(3 * d), full((1, 3 * d)), *[full((w, d)) for w in widths],
                   full((d, d))],
        out_shape=[*[_sds((n, w), BF16) for w in widths], _sds((n, 3 * d), BF16), _sds((1, 3 * d), F32),
                   *[_sds((w, d), F32) for w in widths], _sds((d, d), F32)],
        compiler_params=_cp("arbitrary"), name=name)(dx, *brs, gl, gl, gl, bg, *wbs, *wbTs, wout_t)


def _adamw(w, g, m, v, *, name):
    r, c = w.shape
    tm = _tile(r, 256)
    c1 = 1.0 - ADAM_B1 ** ADAM_STEP
    c2 = 1.0 - ADAM_B2 ** ADAM_STEP

    def body(w_ref, g_ref, m_ref, v_ref, d_ref, nm_ref, nv_ref):
        gv = g_ref[...]
        mn = ADAM_B1 * m_ref[...] + (1.0 - ADAM_B1) * gv
        vn = ADAM_B2 * v_ref[...] + (1.0 - ADAM_B2) * (gv * gv)
        nm_ref[...] = mn
        nv_ref[...] = vn
        d_ref[...] = -ADAM_LR * ((mn / c1) / (jnp.sqrt(vn / c2) + ADAM_EPS) + ADAM_WD * w_ref[...])

    blk = pl.BlockSpec((tm, c), lambda i: (i, 0))
    return pl.pallas_call(
        body, grid=(r // tm,), in_specs=[blk] * 4, out_specs=[blk] * 3,
        out_shape=[_sds((r, c), F32)] * 3, compiler_params=_cp("parallel"), name=name)(w, g, m, v)


def _blockdiag(w):
    g, r, c = w.shape
    eye = jnp.eye(g, dtype=w.dtype)
    return (w[:, :, None, :] * eye[:, None, :, None]).reshape(g * r, g * c)


def _s5_prep(lre, lim, log_dt, b_re, b_im, c_re, c_im, d_skip, nlog):
    lr = jnp.minimum(lre, -1e-4)
    li = lim
    dt = jnp.exp(log_dt)[:, None]
    mag = jnp.exp(lr * dt)
    ar = mag * jnp.cos(li * dt)
    ai = mag * jnp.sin(li * dt)
    den = lr * lr + li * li
    coef_r = ((ar - 1.0) * lr + ai * li) / den
    coef_i = (ai * lr - (ar - 1.0) * li) / den
    bbar_r = coef_r[..., None] * b_re - coef_i[..., None] * b_im
    bbar_i = coef_r[..., None] * b_im + coef_i[..., None] * b_re
    a = jnp.stack([ar.reshape(-1), ai.reshape(-1)])
    return dict(
        a=a,
        bblk_r=_blockdiag(bbar_r.transpose(0, 2, 1)), bblk_i=_blockdiag(bbar_i.transpose(0, 2, 1)),
        cblk_r=_blockdiag(c_re.transpose(0, 2, 1)), cblk_in=_blockdiag(-c_im.transpose(0, 2, 1)),
        d=d_skip.reshape(1, -1))


def _s5_powers(a, nlog):
    pr, pi = a[0], a[1]
    rows = []
    for _ in range(nlog):
        rows += [pr, pi]
        pr, pi = pr * pr - pi * pi, 2.0 * pr * pi
    return jnp.stack(rows)


def _bias_table(rel_bias):
    h = rel_bias.shape[0]
    tq, tw = ATT_TQ, 3 * ATT_TQ
    n_hi = tw - 1 - MAX_REL + 1
    n_lo = tq + tw - 1 - n_hi - (2 * MAX_REL - 1)
    fr = jnp.concatenate([
        jnp.broadcast_to(rel_bias[:, 2 * MAX_REL:], (h, n_hi)),
        jnp.flip(rel_bias[:, 1:2 * MAX_REL], axis=1),
        jnp.broadcast_to(rel_bias[:, :1], (h, n_lo)),
        jnp.zeros((h, 1), rel_bias.dtype)], axis=1)
    ln = tq + tw
    flat = jnp.broadcast_to(fr[:, None, :], (h, tq, ln)).reshape(h, tq * ln)[:, :tq * (ln - 1)]
    tab = flat.reshape(h, tq, ln - 1)[:, :, tq - 1:tq - 1 + tw]
    qc = np.arange(tq)[:, None] // CHUNK + N_LEFT
    kc = np.arange(tw)[None, :] // CHUNK
    band = (kc <= qc) & (kc >= qc - N_LEFT)
    return jnp.where(jnp.asarray(band)[None], tab, NEG)


def _dims(x, w):
    bl, s, d = x.shape
    dff = w["ffn1_w_down"].shape[1]
    ds5 = w["s5_d"].shape[1]
    datt = w["w_br_attn"].shape[1]
    dc = w["conv_b_dw"].shape[1]
    return bl, s, d, dff, ds5, datt, dc


def _layer_weights(w, l, dims):
    bl, s, d, dff, ds5, datt, dc = dims
    c0, c1, c2, c3, c4 = ds5, ds5 + datt, ds5 + 2 * datt, ds5 + 3 * datt, ds5 + 3 * datt + 2 * dc
    win = w["w_in"][l]
    win_a = jnp.concatenate([win[:, c0:c4], win[:, :c0]], axis=1)
    win_g = win[:, c4:]
    out = dict(win_a=win_a, win_g=win_g, wglu=w["s5_w_glu"][l], wbs=w["w_br_s5"][l], wba=w["w_br_attn"][l],
               wbc=w["w_br_conv"][l], wout=w["w_out"][l])
    for f in ("ffn1", "ffn2"):
        out[f + "_wu"] = w[f + "_w_up"][l]
        out[f + "_wd"] = w[f + "_w_down"][l]
    for k in list(out):
        out[k + "T"] = out[k].T
    return out


def _small_prep(w, l, nlog):
    sp = _s5_prep(w["s5_lambda_re"][l], w["s5_lambda_im"][l], w["s5_log_dt"][l], w["s5_b_re"][l], w["s5_b_im"][l],
                  w["s5_c_re"][l], w["s5_c_im"][l], w["s5_d"][l], nlog)
    bias = _bias_table(w["attn_rel_bias"][l])
    return sp, bias


def _local_step(x3, target3, w):
    dims = _dims(x3, w)
    bl, s, d, dff, ds5, datt, dc = dims
    n = bl * s
    nl = w["ffn1_norm"].shape[0]
    x = x3.reshape(n, d)
    target = target3.reshape(n, d)
    tm = _tile(n, 512)
    tmix = _tile(n, 256)
    ts5 = 256
    tconv = _tile(s, 512)
    nlog = int(math.log2(ts5))
    tff = dff // 2
    ma = 3 * datt + 2 * dc + ds5
    tna = ma // 3
    ucol = (3 * datt + 2 * dc) // ds5
    zcol = 3 * datt // (2 * dc)

    saved = []
    for l in range(nl):
        lw = _layer_weights(w, l, dims)
        (sp, bias), prep_vjp = jax.vjp(lambda ww: _small_prep(ww, l, nlog), {k: w[k] for k in _PREP_KEYS})
        sp = dict(sp)
        sp["pw"] = _s5_powers(lax.stop_gradient(sp["a"]), nlog)
        spb = dict(sp)
        for k in ("bblk_r", "bblk_i", "cblk_r", "cblk_in"):
            spb[k] = sp[k].astype(BF16)
        g1 = w["ffn1_norm"][l][None]
        g2 = w["ffn2_norm"][l][None]
        gm = w["mix_norm"][l][None]
        gq2 = jnp.tile(w["attn_q_gain"][l], 2)[None]
        gk2 = jnp.tile(w["attn_k_gain"][l], 2)[None]
        gk = jnp.tile(w["attn_k_gain"][l], datt // HEAD_DIM)[None]
        wdw = jnp.pad(w["conv_w_dw"][l], ((0, HALO - CONV_W), (0, 0)))
        bdw, lng, lnb = w["conv_b_dw"][l][None], w["conv_ln_g"][l][None], w["conv_ln_b"][l][None]
        bg = w["b_gate"][l][None]

        x0 = x
        h1, ab1 = _norm_mm(x0, g1, lw["ffn1_wu"], tm=tm, tn=tff, pieces=2, name=f"ffn1_up_{l}")
        x1 = _ffn_down(ab1, lw["ffn1_wd"], x0, tm=tm, tk=tff, name=f"ffn1_down_{l}")
        h2, pa = _norm_mm(x1, gm, lw["win_a"], tm=tm, tn=tna, pieces=1, name=f"win_a_{l}")
        pa = pa[0]
        gl = _mm(h2, lw["win_g"], tm=tm, tn=d, name=f"win_g_{l}")
        xr, xi, yp, zg, s5o = _s5_fwd(pa, spb, lw["wglu"], bl=bl, s=s, t=ts5, ucol=ucol, name=f"s5_fwd_{l}")
        atto = _attn_fwd(pa, gq2, gk2, bias, bl=bl, s=s, datt=datt, name=f"attn_fwd_{l}")
        hg, hc, convo = _conv_fwd(pa, wdw, bdw, lng, lnb, bl=bl, s=s, t=tconv, zcol=zcol, name=f"conv_fwd_{l}")
        brs = (s5o, atto, convo)
        wbs = (lw["wbs"], lw["wba"], lw["wbc"])
        x2 = _mix_out_fwd(x1, brs, gl, bg, wbs, lw["wout"], tm=tmix, name=f"mix_fwd_{l}")
        h3, ab2 = _norm_mm(x2, g2, lw["ffn2_wu"], tm=tm, tn=tff, pieces=2, name=f"ffn2_up_{l}")
        x3_ = _ffn_down(ab2, lw["ffn2_wd"], x2, tm=tm, tk=tff, name=f"ffn2_down_{l}")
        saved.append(dict(lw=lw, spb=spb, bias=bias, prep_vjp=prep_vjp, g1=g1, g2=g2, gm=gm, gq2=gq2, gk2=gk2, gk=gk,
                          wdw=wdw, lng=lng, lnb=lnb, bg=bg, x0=x0, h1=h1, ab1=ab1, x1=x1, h2=h2, pa=pa, gl=gl,
                          xr=xr, xi=xi, yp=yp, zg=zg, hg=hg, hc=hc, brs=brs, wbs=wbs, x2=x2, h3=h3, ab2=ab2))
        x = x3_

    dx, lsum = _loss_grad(x, target, tm=tm, name="loss")
    loss_part = 0.5 * jnp.sum(lsum) / d

    big = {k: [None] * nl for k in _BIG_KEYS}
    small = {k: [None] * nl for k in _SMALL_KEYS}
    for l in reversed(range(nl)):
        sv = saved[l]
        lw = sv["lw"]

        def ffn_bwd(dx, xin, h, ab, g, wu, wuT, wdT, tag):
            dab, act, dob = _ffn_dact(dx, wdT, ab, tm=tm, tk=tff, name=f"{tag}_dact_{l}")
            dwd = _mm_tn(act[None], dob[None], ta=tff, tb=d, tk=tm, name=f"{tag}_dwd_{l}")
            dwu = _mm_tn(h[None], dab, ta=d, tb=tff, tk=tm, name=f"{tag}_dwu_{l}")
            dxo, dg = _mm_rmsbwd(dab, wuT, xin, g, dx, tm=tm, tk=tff, name=f"{tag}_dx_{l}")
            return dxo, dg[0], dwu, dwd

        dx, small["ffn2_norm"][l], big["ffn2_w_up"][l], big["ffn2_w_down"][l] = ffn_bwd(
            dx, sv["x2"], sv["h3"], sv["ab2"], sv["g2"], lw["ffn2_wu"], lw["ffn2_wuT"], lw["ffn2_wdT"], "ffn2")

        wbTs = (lw["wbsT"], lw["wbaT"], lw["wbcT"])
        ds5o, datto, dconvo, dgl, dbg, dwbs, dwba, dwbc, dwout = _mix_out_bwd(
            dx, sv["brs"], sv["gl"], sv["bg"], sv["wbs"], wbTs, lw["woutT"], tm=tmix, name=f"mix_bwd_{l}")
        small["b_gate"][l] = dbg[0]
        big["w_br_s5"][l], big["w_br_attn"][l], big["w_br_conv"][l], big["w_out"][l] = dwbs, dwba, dwbc, dwout

        dhc, dlng, dlnb = _conv_bwd_ln(dconvo, sv["hc"], sv["lng"], sv["lnb"], tm=tm, name=f"conv_bwd_ln_{l}")
        dz, dwdw, dbdw = _conv_bwd_dw(dhc, sv["hg"], sv["pa"], sv["wdw"], bl=bl, s=s, t=tconv, zcol=zcol,
                                      name=f"conv_bwd_dw_{l}")
        small["conv_w_dw"][l] = dwdw[:CONV_W]
        small["conv_b_dw"][l], small["conv_ln_g"][l], small["conv_ln_b"][l] = dbdw[0], dlng[0], dlnb[0]

        dq, dkn, dvw, dbias, dgq = _attn_bwd(datto, sv["pa"], sv["gq2"], sv["gk2"], sv["bias"], bl=bl, s=s, datt=datt,
                                             name=f"attn_bwd_{l}")
        dk, dv, dgk = _attn_kv_bwd(dkn, dvw, sv["pa"], sv["gk"], bl=bl, s=s, datt=datt, tm=_tile(s, 512),
                                   name=f"attn_kv_bwd_{l}")
        small["attn_q_gain"][l] = jnp.sum(dgq.reshape(-1, HEAD_DIM), axis=0)
        small["attn_k_gain"][l] = jnp.sum(dgk.reshape(-1, HEAD_DIM), axis=0)

        du, dwglu, dd, dcr, dci, dbr, dbi, da = _s5_bwd(
            ds5o, sv["yp"], sv["zg"], sv["xr"], sv["xi"], sv["pa"], sv["spb"], lw["wgluT"], bl=bl, s=s, t=ts5,
            ucol=ucol, name=f"s5_bwd_{l}")
        big["s5_w_glu"][l] = dwglu
        prep_ct = (dict(a=da, bblk_r=dbr, bblk_i=dbi, cblk_r=dcr, cblk_in=dci, d=dd), jnp.sum(dbias, axis=0))
        (dprep,) = sv["prep_vjp"](prep_ct)
        for k in _PREP_KEYS:
            small[k][l] = dprep[k][l]

        dpa = jnp.concatenate([dq, dk, dv, dz, du], axis=1)
        dwa = _mm_tn(sv["h2"][None], dpa[None], ta=d, tb=tna, tk=tm, name=f"dwin_a_{l}")
        dwg = _mm_tn(sv["h2"][None], dgl[None], ta=d, tb=d, tk=tm, name=f"dwin_g_{l}")
        big["w_in"][l] = jnp.concatenate([dwa[:, ma - ds5:], dwa[:, :ma - ds5], dwg], axis=1)
        dcat = jnp.concatenate([dpa, dgl], axis=1)
        wT = jnp.concatenate([lw["win_aT"], lw["win_gT"]], axis=0)
        dx, dgm = _mm_rmsbwd(dcat[None], wT, sv["x1"], sv["gm"], dx, tm=tm, tk=tna, name=f"mix_dx_{l}")
        small["mix_norm"][l] = dgm[0]

        dx, small["ffn1_norm"][l], big["ffn1_w_up"][l], big["ffn1_w_down"][l] = ffn_bwd(
            dx, sv["x0"], sv["h1"], sv["ab1"], sv["g1"], lw["ffn1_wu"], lw["ffn1_wuT"], lw["ffn1_wdT"], "ffn1")

    big = {k: jnp.stack(v) for k, v in big.items()}
    small = {k: jnp.stack(v) for k, v in small.items()}
    return loss_part, dx.reshape(bl, s, d), big, small


_PREP_KEYS = ("s5_lambda_re", "s5_lambda_im", "s5_log_dt", "s5_b_re", "s5_b_im", "s5_c_re", "s5_c_im", "s5_d",
              "attn_rel_bias")
_BIG_KEYS = {"ffn1_w_up": 2, "ffn1_w_down": 1, "w_in": 2, "s5_w_glu": 2, "w_br_s5": 2, "w_br_attn": 2,
             "w_br_conv": 2, "w_out": 1, "ffn2_w_up": 2, "ffn2_w_down": 1}
_SMALL_KEYS = ("ffn1_norm", "mix_norm", "b_gate", "s5_lambda_re", "s5_lambda_im", "s5_log_dt", "s5_b_re", "s5_b_im",
               "s5_c_re", "s5_c_im", "s5_d", "attn_q_gain", "attn_k_gain", "attn_rel_bias", "conv_w_dw", "conv_b_dw",
               "conv_ln_g", "conv_ln_b", "ffn2_norm")
_WEIGHTS = ("ffn1_norm", "ffn1_w_up", "ffn1_w_down", "mix_norm", "w_in", "b_gate", "s5_lambda_re", "s5_lambda_im",
            "s5_log_dt", "s5_b_re", "s5_b_im", "s5_c_re", "s5_c_im", "s5_d", "s5_w_glu", "w_br_s5", "attn_q_gain",
            "attn_k_gain", "attn_rel_bias", "w_br_attn", "conv_w_dw", "conv_b_dw", "conv_ln_g", "conv_ln_b",
            "w_br_conv", "w_out", "ffn2_norm", "ffn2_w_up", "ffn2_w_down")
N_CHIPS = 4
PACK_COLS = 1024
PACK_ROW_ALIGN = 1024


def _place():
    x, y, c = lax.axis_index("x"), lax.axis_index("y"), lax.axis_index("c")
    chips = [(1 - x, y), (x, 1 - y), (1 - x, 1 - y)]
    return x, y, c, chips


def _remote(src, dst, send_sems, recv_sems, k, dev):
    return pltpu.make_async_remote_copy(src_ref=src, dst_ref=dst, send_sem=send_sems.at[k], recv_sem=recv_sems.at[k],
                                        device_id=dev, device_id_type=MESH)


_ANY = pl.BlockSpec(memory_space=pl.ANY)


def _all_gather_shards(pack):
    r, cols = pack.shape
    h = r // 2

    def body(p_ref, o_ref, send_sems, recv_sems, local_sem):
        x, y, c, chips = _place()
        s_me = 2 * x + y
        half = pl.ds(pl.multiple_of(c * h, 16), h)
        other = pl.ds(pl.multiple_of((1 - c) * h, 16), h)
        sibling = (x, y, 1 - c)
        mine = pltpu.make_async_copy(p_ref, o_ref.at[s_me], local_sem)
        mine.start()
        first = [_remote(p_ref.at[half], o_ref.at[s_me, half], send_sems, recv_sems, j, (cx, cy, c))
                 for j, (cx, cy) in enumerate(chips)]
        for cp in first:
            cp.start()
        passed = []
        for j, (cx, cy) in enumerate(chips):
            piece = o_ref.at[2 * cx + cy, half]
            _remote(piece, piece, send_sems, recv_sems, j, (cx, cy, c)).wait_recv()
            fw = _remote(piece, piece, send_sems, recv_sems, 3 + j, sibling)
            fw.start()
            passed.append(fw)
        for j, (cx, cy) in enumerate(chips):
            piece = o_ref.at[2 * cx + cy, other]
            _remote(piece, piece, send_sems, recv_sems, 3 + j, sibling).wait_recv()
        for cp in first + passed:
            cp.wait_send()
        mine.wait()

    return pl.pallas_call(
        body, in_specs=[_ANY], out_specs=_ANY, out_shape=_sds((N_CHIPS, r, cols), pack.dtype),
        scratch_shapes=[pltpu.SemaphoreType.DMA((6,)), pltpu.SemaphoreType.DMA((6,)), pltpu.SemaphoreType.DMA],
        name="all_gather_shards")(pack)


def _swap_halves(g4):
    _, r, cols = g4.shape
    h = r // 2

    def body(g_ref, o_ref, send_sems, recv_sems):
        x, y, c, _ = _place()
        other = pl.ds(pl.multiple_of((1 - c) * h, 8), h)
        cp = _remote(g_ref.at[:, other, :], o_ref, send_sems, recv_sems, 0, (x, y, 1 - c))
        cp.start()
        cp.wait()

    return pl.pallas_call(
        body, in_specs=[_ANY], out_specs=_ANY, out_shape=_sds((N_CHIPS, h, cols), g4.dtype),
        scratch_shapes=[pltpu.SemaphoreType.DMA((1,)), pltpu.SemaphoreType.DMA((1,))],
        name="rs_swap_halves")(g4)


def _exchange_chips(rsum):
    def body(r_ref, o_ref, send_sems, recv_sems, local_sem):
        x, y, c, chips = _place()
        s_me = 2 * x + y
        mine = pltpu.make_async_copy(r_ref.at[s_me], o_ref.at[s_me], local_sem)
        mine.start()
        cps = [_remote(r_ref.at[2 * cx + cy], o_ref.at[s_me], send_sems, recv_sems, j, (cx, cy, c))
               for j, (cx, cy) in enumerate(chips)]
        for cp in cps:
            cp.start()
        for j, (cx, cy) in enumerate(chips):
            slab = o_ref.at[2 * cx + cy]
            _remote(slab, slab, send_sems, recv_sems, j, (cx, cy, c)).wait_recv()
        for cp in cps:
            cp.wait_send()
        mine.wait()

    return pl.pallas_call(
        body, in_specs=[_ANY], out_specs=_ANY, out_shape=_sds(rsum.shape, rsum.dtype),
        scratch_shapes=[pltpu.SemaphoreType.DMA((3,)), pltpu.SemaphoreType.DMA((3,)), pltpu.SemaphoreType.DMA],
        name="rs_exchange_chips")(rsum)


def _join_halves(t):
    h, cols = t.shape

    def body(t_ref, o_ref, send_sems, recv_sems, local_sem):
        x, y, c, _ = _place()
        half = pl.ds(pl.multiple_of(c * h, 8), h)
        other = pl.ds(pl.multiple_of((1 - c) * h, 8), h)
        mine = pltpu.make_async_copy(t_ref, o_ref.at[half], local_sem)
        mine.start()
        cp = _remote(t_ref, o_ref.at[half], send_sems, recv_sems, 0, (x, y, 1 - c))
        cp.start()
        _remote(t_ref, o_ref.at[other], send_sems, recv_sems, 0, (x, y, 1 - c)).wait_recv()
        cp.wait_send()
        mine.wait()

    return pl.pallas_call(
        body, in_specs=[_ANY], out_specs=_ANY, out_shape=_sds((2 * h, cols), t.dtype),
        scratch_shapes=[pltpu.SemaphoreType.DMA((1,)), pltpu.SemaphoreType.DMA((1,)), pltpu.SemaphoreType.DMA],
        name="rs_join_halves")(t)


def _add_halves(g4, recv, c_idx, *, tr):
    _, r, cols = g4.shape
    h = r // 2
    nb = h // tr

    def body(c_ref, g_ref, r_ref, o_ref):
        o_ref[...] = g_ref[...] + r_ref[...]

    return pl.pallas_call(
        body,
        grid_spec=pltpu.PrefetchScalarGridSpec(
            num_scalar_prefetch=1, grid=(N_CHIPS, nb),
            in_specs=[pl.BlockSpec((None, tr, cols), lambda s, i, c_ref: (s, c_ref[0] * nb + i, 0)),
                      pl.BlockSpec((None, tr, cols), lambda s, i, c_ref: (s, i, 0))],
            out_specs=pl.BlockSpec((None, tr, cols), lambda s, i, c_ref: (s, i, 0))),
        out_shape=_sds((N_CHIPS, h, cols), F32),
        compiler_params=_cp("parallel", "parallel"), name="rs_add_halves")(c_idx, g4, recv)


def _add_chips(parts, *, tr):
    _, h, cols = parts.shape

    def body(p0, p1, p2, p3, o_ref):
        o_ref[...] = ((p0[...] + p1[...]) + p2[...]) + p3[...]

    specs = [pl.BlockSpec((None, tr, cols), functools.partial(lambda s, i: (s, i, 0), s)) for s in range(N_CHIPS)]
    return pl.pallas_call(
        body, grid=(h // tr,), in_specs=specs, out_specs=pl.BlockSpec((tr, cols), lambda i: (i, 0)),
        out_shape=_sds((h, cols), F32), compiler_params=_cp("parallel"), name="rs_add_chips")(parts, parts, parts, parts)


def _all_reduce_small(buf):
    r, cols = buf.shape
    nd = 8

    def body(b_ref, o_ref, recv, send_sems, recv_sems):
        x, y, c, _ = _place()
        me = 4 * x + 2 * y + c
        recv[0] = b_ref[...]
        cps = []
        for rel in range(1, nd):
            dev = (1 - x if rel & 4 else x, 1 - y if rel & 2 else y, 1 - c if rel & 1 else c)
            cp = _remote(b_ref, recv.at[rel], send_sems, recv_sems, rel - 1, dev)
            cp.start()
            cps.append(cp)
        for rel in range(1, nd):
            _remote(b_ref, recv.at[rel], send_sems, recv_sems, rel - 1, (x, y, c)).wait_recv()
        acc = recv[me]
        for d in range(1, nd):
            acc = acc + recv[lax.bitwise_xor(me, d)]
        o_ref[...] = acc
        for cp in cps:
            cp.wait_send()

    vm = pl.BlockSpec(memory_space=pltpu.VMEM)
    return pl.pallas_call(
        body, in_specs=[vm], out_specs=vm, out_shape=_sds((r, cols), F32),
        scratch_shapes=[pltpu.VMEM((nd, r, cols), F32), pltpu.SemaphoreType.DMA((nd - 1,)),
                        pltpu.SemaphoreType.DMA((nd - 1,))],
        compiler_params=pltpu.CompilerParams(vmem_limit_bytes=VMEM_LIMIT), name="all_reduce_small")(buf)


def _pack_rows(parts, dtype, row_align):
    flat = jnp.concatenate([p.reshape(-1).astype(dtype) for p in parts])
    per = PACK_COLS * row_align
    size = -(-flat.shape[0] // per) * per
    return jnp.pad(flat, (0, size - flat.shape[0])).reshape(-1, PACK_COLS)


def _unpack(flat, shapes):
    out, off = [], 0
    for shp in shapes:
        size = int(np.prod(shp))
        out.append(flat[off:off + size].reshape(shp))
        off += size
    return out


def _shards_to_full(st, axis):
    st = jnp.moveaxis(st, 0, axis)
    shp = st.shape
    return st.reshape(shp[:axis] + (shp[axis] * shp[axis + 1],) + shp[axis + 2:])


def _full_to_shards(full, axis):
    shp = full.shape
    st = full.reshape(shp[:axis] + (N_CHIPS, shp[axis] // N_CHIPS) + shp[axis + 1:])
    return jnp.moveaxis(st, axis, 0)


def kernel(x, ffn1_norm, ffn1_w_up, ffn1_w_down, mix_norm, w_in, b_gate, s5_lambda_re, s5_lambda_im, s5_log_dt, s5_b_re, s5_b_im, s5_c_re, s5_c_im, s5_d, s5_w_glu, w_br_s5, attn_q_gain, attn_k_gain, attn_rel_bias, w_br_attn, conv_w_dw, conv_b_dw, conv_ln_g, conv_ln_b, w_br_conv, w_out, ffn2_norm, ffn2_w_up, ffn2_w_down, loss_target, m_ffn1_norm, m_ffn1_w_up, m_ffn1_w_down, m_mix_norm, m_w_in, m_b_gate, m_s5_lambda_re, m_s5_lambda_im, m_s5_log_dt, m_s5_b_re, m_s5_b_im, m_s5_c_re, m_s5_c_im, m_s5_d, m_s5_w_glu, m_w_br_s5, m_attn_q_gain, m_attn_k_gain, m_attn_rel_bias, m_w_br_attn, m_conv_w_dw, m_conv_b_dw, m_conv_ln_g, m_conv_ln_b, m_w_br_conv, m_w_out, m_ffn2_norm, m_ffn2_w_up, m_ffn2_w_down, v_ffn1_norm, v_ffn1_w_up, v_ffn1_w_down, v_mix_norm, v_w_in, v_b_gate, v_s5_lambda_re, v_s5_lambda_im, v_s5_log_dt, v_s5_b_re, v_s5_b_im, v_s5_c_re, v_s5_c_im, v_s5_d, v_s5_w_glu, v_w_br_s5, v_attn_q_gain, v_attn_k_gain, v_attn_rel_bias, v_w_br_attn, v_conv_w_dw, v_conv_b_dw, v_conv_ln_g, v_conv_ln_b, v_w_br_conv, v_w_out, v_ffn2_norm, v_ffn2_w_up, v_ffn2_w_down):
    a = dict(locals())
    xi, yi, ci = lax.axis_index("x"), lax.axis_index("y"), lax.axis_index("c")
    s_me = 2 * xi + yi
    big_keys = list(_BIG_KEYS)

    taps_bits = lax.bitcast_convert_type(a["conv_w_dw"], BF16)
    pack = _pack_rows([a[k] for k in big_keys] + [taps_bits], BF16, PACK_ROW_ALIGN)
    gathered = _all_gather_shards(pack).reshape(N_CHIPS, -1)
    stacks = _unpack_stack(gathered, [a[k].shape for k in big_keys] + [taps_bits.shape])
    w = {k: a[k] for k in _WEIGHTS}
    for k, st in zip(big_keys, stacks):
        w[k] = _shards_to_full(st, _BIG_KEYS[k])
    w["conv_w_dw"] = _shards_to_full(lax.bitcast_convert_type(stacks[-1], F32), 2)

    loss_part, grad_x, gbig, gsmall = _local_step(a["x"], a["loss_target"], w)
    loss = lax.psum(loss_part, ("x", "y", "c"))

    g4 = jnp.concatenate([_full_to_shards(gbig[k], _BIG_KEYS[k]).reshape(N_CHIPS, -1) for k in big_keys], axis=1)
    per = PACK_COLS * PACK_ROW_ALIGN
    size = -(-g4.shape[1] // per) * per
    g4 = jnp.pad(g4, ((0, 0), (0, size - g4.shape[1]))).reshape(N_CHIPS, -1, PACK_COLS)
    tr = _tile(g4.shape[1] // 2, 512)
    recv = _swap_halves(g4)
    rsum = _add_halves(g4, recv, ci.astype(jnp.int32).reshape(1), tr=tr)
    parts = _exchange_chips(rsum)
    mine = _add_chips(parts, tr=tr)
    gshard = _join_halves(mine).reshape(-1)
    gb = dict(zip(big_keys, _unpack(gshard, [a[k].shape for k in big_keys])))

    small_keys = list(_SMALL_KEYS)
    spack = _pack_rows([gsmall[k] for k in small_keys], F32, 8)
    sred = _all_reduce_small(spack).reshape(-1)
    gs = dict(zip(small_keys, _unpack(sred, [gsmall[k].shape for k in small_keys])))
    blk = a["conv_w_dw"].shape[2]
    gs["conv_w_dw"] = lax.dynamic_slice_in_dim(gs["conv_w_dw"], s_me * blk, blk, axis=2)
    grads = {**gb, **gs}

    delta, new_m, new_v = {}, {}, {}
    for k in big_keys:
        shp = a[k].shape
        two_d = lambda t: t.reshape(-1, shp[-1])
        d_, m_, v_ = _adamw(two_d(a[k]), two_d(grads[k]), two_d(a["m_" + k]), two_d(a["v_" + k]), name=f"adamw_{k}")
        delta[k], new_m[k], new_v[k] = d_.reshape(shp), m_.reshape(shp), v_.reshape(shp)
    sm_shapes = [a[k].shape for k in small_keys]
    packs = [_pack_rows([src[k] for k in small_keys], F32, 8)
             for src in (a, grads, {k: a["m_" + k] for k in small_keys}, {k: a["v_" + k] for k in small_keys})]
    d_, m_, v_ = _adamw(*packs, name="adamw_small")
    for dst, res in ((delta, d_), (new_m, m_), (new_v, v_)):
        dst.update(zip(small_keys, _unpack(res.reshape(-1), sm_shapes)))

    return (loss, grad_x, *[grads[k] for k in _WEIGHTS], *[delta[k] for k in _WEIGHTS],
            *[new_m[k] for k in _WEIGHTS], *[new_v[k] for k in _WEIGHTS])


def _unpack_stack(gathered, shapes):
    out, off = [], 0
    for shp in shapes:
        size = int(np.prod(shp))
        out.append(gathered[:, off:off + size].reshape((N_CHIPS,) + tuple(shp)))
        off += size
    return out
```

```python
import functools
import math

import numpy as np
import jax
import jax.numpy as jnp
from jax import lax
from jax.experimental import pallas as pl
from jax.experimental.pallas import tpu as pltpu

F32 = jnp.float32
BF16 = jnp.bfloat16
EPS = 1e-6
VMEM_LIMIT = 56 * 1024 * 1024
LANES = 128
HEAD_DIM = 64
CHUNK = 64
N_LEFT = 8
MAX_REL = 128
ATT_TQ = 256
CONV_W = 31
HALO = 32
NEG = -1e30
N_CHIPS = 4
PACK_COLS = 1024

ADAM_LR = 0.001
ADAM_B1 = 0.9
ADAM_B2 = 0.999
ADAM_EPS = 1e-08
ADAM_WD = 0.01
ADAM_STEP = 10

MESH = pl.DeviceIdType.MESH
_ANY = pl.BlockSpec(memory_space=pl.ANY)


def _cp(*sem):
    return pltpu.CompilerParams(dimension_semantics=sem, vmem_limit_bytes=VMEM_LIMIT)


def _sds(shape, dtype):
    return jax.ShapeDtypeStruct(shape, dtype)


def _tile(n, pref):
    t = min(n, pref)
    while n % t:
        t -= 8
    return t


def _sigmoid(x):
    return jax.nn.sigmoid(x)


_GELU_C = math.sqrt(2.0 / math.pi)


def _gelu(y):
    return 0.5 * y * (1.0 + jnp.tanh(_GELU_C * (y + 0.044715 * y * y * y)))


def _gelu_grad(y):
    th = jnp.tanh(_GELU_C * (y + 0.044715 * y * y * y))
    return 0.5 * (1.0 + th) + 0.5 * y * (1.0 - th * th) * _GELU_C * (1.0 + 3.0 * 0.044715 * y * y)


def _dot(a, b):
    return jnp.dot(a, b, preferred_element_type=F32)


def _dot_t0(a, b):
    return lax.dot_general(a, b, (((0,), (0,)), ((), ())), preferred_element_type=F32)


def _dot_t1(a, b):
    return lax.dot_general(a, b, (((1,), (1,)), ((), ())), preferred_element_type=F32)


def _slab_out(nl, l, shape, buf, n_in):
    sds = _sds((nl,) + tuple(shape), F32)
    if buf is None:
        return [], [], sds, {}
    return [buf], [_ANY], sds, {n_in: 0}


def _norm_mm(x, g, w, l, *, tm, tn, ntiles, pieces, transposed, name):
    n, d = x.shape
    m = ntiles * tn
    mp = m // pieces
    npj = mp // tn

    def body(x_ref, g_ref, w_ref, h_ref, y_ref, h_scr):
        @pl.when(pl.program_id(1) == 0)
        def _():
            xv = x_ref[...]
            r = lax.rsqrt(jnp.mean(xv * xv, axis=-1, keepdims=True) + EPS)
            hb = (xv * r * g_ref[...]).astype(BF16)
            h_scr[...] = hb
            h_ref[...] = hb

        mm = _dot_t1 if transposed else _dot
        y_ref[...] = mm(h_scr[...], w_ref[...]).astype(BF16)

    wspec = (pl.BlockSpec((None, tn, d), lambda i, j: (l, j, 0)) if transposed
             else pl.BlockSpec((None, d, tn), lambda i, j: (l, 0, j)))
    return pl.pallas_call(
        body, grid=(n // tm, ntiles),
        in_specs=[pl.BlockSpec((tm, d), lambda i, j: (i, 0)), pl.BlockSpec((1, d), lambda i, j: (0, 0)), wspec],
        out_specs=[pl.BlockSpec((tm, d), lambda i, j: (i, 0)),
                   pl.BlockSpec((None, tm, tn), lambda i, j: (j // npj, i, j % npj))],
        out_shape=[_sds((n, d), BF16), _sds((pieces, n, mp), BF16)],
        scratch_shapes=[pltpu.VMEM((tm, d), BF16)],
        compiler_params=_cp("parallel", "arbitrary"), name=name)(x, g, w)


def _mm_t(a, w, l, *, tm, tn, off, ntiles, name):
    n, k = a.shape

    def body(a_ref, w_ref, y_ref):
        y_ref[...] = _dot_t1(a_ref[...], w_ref[...]).astype(BF16)

    return pl.pallas_call(
        body, grid=(n // tm, ntiles),
        in_specs=[pl.BlockSpec((tm, k), lambda i, j: (i, 0)), pl.BlockSpec((None, tn, k), lambda i, j: (l, off + j, 0))],
        out_specs=pl.BlockSpec((tm, tn), lambda i, j: (i, j)),
        out_shape=_sds((n, ntiles * tn), BF16),
        compiler_params=_cp("parallel", "arbitrary"), name=name)(a, w)


def _ffn_down(ab, wd, l, x, *, tm, tk, name):
    _, n, dff = ab.shape
    d = x.shape[1]
    nk = dff // tk

    def body(a_ref, b_ref, wd_ref, x_ref, o_ref, acc):
        k = pl.program_id(1)
        a = a_ref[...].astype(F32)
        b = b_ref[...].astype(F32)
        act = (a * _sigmoid(a) * b).astype(BF16)
        part = _dot(act, wd_ref[...])

        @pl.when(k == 0)
        def _():
            acc[...] = part

        @pl.when(k > 0)
        def _():
            acc[...] += part

        @pl.when(k == nk - 1)
        def _():
            o_ref[...] = x_ref[...] + 0.5 * acc[...]

    return pl.pallas_call(
        body, grid=(n // tm, nk),
        in_specs=[pl.BlockSpec((None, tm, tk), lambda i, k: (0, i, k)),
                  pl.BlockSpec((None, tm, tk), lambda i, k: (1, i, k)),
                  pl.BlockSpec((None, tk, d), lambda i, k: (l, k, 0)),
                  pl.BlockSpec((tm, d), lambda i, k: (i, 0))],
        out_specs=pl.BlockSpec((tm, d), lambda i, k: (i, 0)),
        out_shape=_sds((n, d), F32),
        scratch_shapes=[pltpu.VMEM((tm, d), F32)],
        compiler_params=_cp("parallel", "arbitrary"), name=name)(ab, ab, wd, x)


def _ffn_dact(dx, wd, l, ab, *, tm, tk, name):
    n, d = dx.shape
    dff = ab.shape[2]

    def body(dx_ref, wd_ref, a_ref, b_ref, dab_ref, act_ref, do_ref):
        do = (0.5 * dx_ref[...]).astype(BF16)

        @pl.when(pl.program_id(1) == 0)
        def _():
            do_ref[...] = do

        dact = _dot_t1(do, wd_ref[...])
        a = a_ref[...].astype(F32)
        b = b_ref[...].astype(F32)
        sg = _sigmoid(a)
        silu = a * sg
        act_ref[...] = (silu * b).astype(BF16)
        dab_ref[0] = (dact * b * (sg * (1.0 + a * (1.0 - sg)))).astype(BF16)
        dab_ref[1] = (dact * silu).astype(BF16)

    return pl.pallas_call(
        body, grid=(n // tm, dff // tk),
        in_specs=[pl.BlockSpec((tm, d), lambda i, j: (i, 0)),
                  pl.BlockSpec((None, tk, d), lambda i, j: (l, j, 0)),
                  pl.BlockSpec((None, tm, tk), lambda i, j: (0, i, j)),
                  pl.BlockSpec((None, tm, tk), lambda i, j: (1, i, j))],
        out_specs=[pl.BlockSpec((2, tm, tk), lambda i, j: (0, i, j)),
                   pl.BlockSpec((tm, tk), lambda i, j: (i, j)),
                   pl.BlockSpec((tm, d), lambda i, j: (i, 0))],
        out_shape=[_sds((2, n, dff), BF16), _sds((n, dff), BF16), _sds((n, d), BF16)],
        compiler_params=_cp("parallel", "arbitrary"), name=name)(dx, wd, ab, ab)


def _rms_bwd_epilogue(acc, x_ref, g_ref, dres_ref, dx_ref, dg_ref, i):
    dh = acc[...]
    xv = x_ref[...]
    r = lax.rsqrt(jnp.mean(xv * xv, axis=-1, keepdims=True) + EPS)
    xn = xv * r
    dgp = jnp.sum(dh * xn, axis=0, keepdims=True)
    dxh = dh * g_ref[...]
    dx_ref[...] = dres_ref[...] + r * (dxh - xn * jnp.mean(dxh * xn, axis=-1, keepdims=True))

    @pl.when(i == 0)
    def _():
        dg_ref[...] = dgp

    @pl.when(i > 0)
    def _():
        dg_ref[...] += dgp


def _ffn_dx(dab, wu, l, x, g, dres, *, tm, tk, name):
    p, n, mp = dab.shape
    d = x.shape[1]
    nkp = mp // tk
    nk = p * nkp

    def body(dy_ref, w_ref, x_ref, g_ref, dres_ref, dx_ref, dg_ref, acc):
        k = pl.program_id(1)
        part = _dot_t1(dy_ref[...], w_ref[...])

        @pl.when(k == 0)
        def _():
            acc[...] = part

        @pl.when(k > 0)
        def _():
            acc[...] += part

        @pl.when(k == nk - 1)
        def _():
            _rms_bwd_epilogue(acc, x_ref, g_ref, dres_ref, dx_ref, dg_ref, pl.program_id(0))

    return pl.pallas_call(
        body, grid=(n // tm, nk),
        in_specs=[pl.BlockSpec((None, tm, tk), lambda i, k: (k // nkp, i, k % nkp)),
                  pl.BlockSpec((None, d, tk), lambda i, k: (l, 0, k)),
                  pl.BlockSpec((tm, d), lambda i, k: (i, 0)),
                  pl.BlockSpec((1, d), lambda i, k: (0, 0)),
                  pl.BlockSpec((tm, d), lambda i, k: (i, 0))],
        out_specs=[pl.BlockSpec((tm, d), lambda i, k: (i, 0)), pl.BlockSpec((1, d), lambda i, k: (0, 0))],
        out_shape=[_sds((n, d), F32), _sds((1, d), F32)],
        scratch_shapes=[pltpu.VMEM((tm, d), F32)],
        compiler_params=_cp("arbitrary", "arbitrary"), name=name)(dab, wu, x, g, dres)


def _mix_dx(dpa, dgl, wt, l, x, g, dres, *, tm, tk, name):
    n, d = x.shape
    n1 = dpa.shape[1] // tk
    n2 = dgl.shape[1] // tk
    nk = n1 + n2

    def body(d1_ref, d2_ref, w_ref, x_ref, g_ref, dres_ref, dx_ref, dg_ref, acc):
        k = pl.program_id(1)

        @pl.when(k == 0)
        def _():
            acc[...] = _dot(d1_ref[...], w_ref[...])

        @pl.when((k > 0) & (k < n1))
        def _():
            acc[...] += _dot(d1_ref[...], w_ref[...])

        @pl.when(k >= n1)
        def _():
            acc[...] += _dot(d2_ref[...], w_ref[...])

        @pl.when(k == nk - 1)
        def _():
            _rms_bwd_epilogue(acc, x_ref, g_ref, dres_ref, dx_ref, dg_ref, pl.program_id(0))

    return pl.pallas_call(
        body, grid=(n // tm, nk),
        in_specs=[pl.BlockSpec((tm, tk), lambda i, k: (i, jnp.minimum(k, n1 - 1))),
                  pl.BlockSpec((tm, tk), lambda i, k: (i, jnp.maximum(k - n1, 0))),
                  pl.BlockSpec((None, tk, d), lambda i, k: (l, k, 0)),
                  pl.BlockSpec((tm, d), lambda i, k: (i, 0)),
                  pl.BlockSpec((1, d), lambda i, k: (0, 0)),
                  pl.BlockSpec((tm, d), lambda i, k: (i, 0))],
        out_specs=[pl.BlockSpec((tm, d), lambda i, k: (i, 0)), pl.BlockSpec((1, d), lambda i, k: (0, 0))],
        out_shape=[_sds((n, d), F32), _sds((1, d), F32)],
        scratch_shapes=[pltpu.VMEM((tm, d), F32)],
        compiler_params=_cp("arbitrary", "arbitrary"), name=name)(dpa, dgl, wt, x, g, dres)


def _mm_tn(a, b, l, nl, buf, *, ta, tb, tk, name):
    pa, n, ka = a.shape
    pb, _, kb = b.shape
    nap = ka // ta
    nbp = kb // tb

    def body(a_ref, b_ref, *rest):
        o_ref = rest[-1]

        @pl.when(pl.program_id(2) == 0)
        def _():
            o_ref[...] = jnp.zeros_like(o_ref)

        o_ref[...] += _dot_t0(a_ref[...], b_ref[...])

    extra, extra_specs, out_shape, aliases = _slab_out(nl, l, (pa * ka, pb * kb), buf, 2)
    return pl.pallas_call(
        body, grid=(pa * nap, pb * nbp, n // tk),
        in_specs=[pl.BlockSpec((None, tk, ta), lambda i, j, k: (i // nap, k, i % nap)),
                  pl.BlockSpec((None, tk, tb), lambda i, j, k: (j // nbp, k, j % nbp)), *extra_specs],
        out_specs=pl.BlockSpec((None, ta, tb), lambda i, j, k: (l, i, j)),
        out_shape=out_shape, input_output_aliases=aliases,
        compiler_params=_cp("parallel", "parallel", "arbitrary"), name=name)(a, b, *extra)


def _dwin_t(dpa, dgl, h, l, nl, buf, *, ta, tk, name):
    n, d = h.shape
    n1 = dpa.shape[1] // ta
    n2 = dgl.shape[1] // ta

    def body(a1_ref, a2_ref, h_ref, *rest):
        o_ref = rest[-1]
        i = pl.program_id(0)

        @pl.when(pl.program_id(1) == 0)
        def _():
            o_ref[...] = jnp.zeros_like(o_ref)

        @pl.when(i < n1)
        def _():
            o_ref[...] += _dot_t0(a1_ref[...], h_ref[...])

        @pl.when(i >= n1)
        def _():
            o_ref[...] += _dot_t0(a2_ref[...], h_ref[...])

    extra, extra_specs, out_shape, aliases = _slab_out(nl, l, ((n1 + n2) * ta, d), buf, 3)
    return pl.pallas_call(
        body, grid=(n1 + n2, n // tk),
        in_specs=[pl.BlockSpec((tk, ta), lambda i, k: (jnp.where(i < n1, k, 0), jnp.minimum(i, n1 - 1))),
                  pl.BlockSpec((tk, ta), lambda i, k: (jnp.where(i >= n1, k, 0), jnp.maximum(i - n1, 0))),
                  pl.BlockSpec((tk, d), lambda i, k: (k, 0)), *extra_specs],
        out_specs=pl.BlockSpec((None, ta, d), lambda i, k: (l, i, 0)),
        out_shape=out_shape, input_output_aliases=aliases,
        compiler_params=_cp("parallel", "arbitrary"), name=name)(dpa, dgl, h, *extra)


def _loss_grad(y, t, *, tm, name):
    n, d = y.shape

    def body(y_ref, t_ref, dy_ref, l_ref):
        e = y_ref[...] - t_ref[...]
        dy_ref[...] = e * (1.0 / d)
        part = jnp.sum(e * e, axis=0, keepdims=True)

        @pl.when(pl.program_id(0) == 0)
        def _():
            l_ref[...] = part

        @pl.when(pl.program_id(0) > 0)
        def _():
            l_ref[...] += part

    return pl.pallas_call(
        body, grid=(n // tm,),
        in_specs=[pl.BlockSpec((tm, d), lambda i: (i, 0)), pl.BlockSpec((tm, d), lambda i: (i, 0))],
        out_specs=[pl.BlockSpec((tm, d), lambda i: (i, 0)), pl.BlockSpec((1, d), lambda i: (0, 0))],
        out_shape=[_sds((n, d), F32), _sds((1, d), F32)],
        compiler_params=_cp("arbitrary"), name=name)(y, t)


def _s5_fwd(proj, sp, wglu, l, *, bl, s, t, name):
    n = bl * s
    ds5, gp = sp["bblk_r"].shape
    nt = s // t
    nlog = int(math.log2(t))

    def body(u_ref, br_ref, bi_ref, a_ref, pw_ref, cr_ref, ci_ref, d_ref, wg_ref,
             xr_ref, xi_ref, yp_ref, zg_ref, o_ref, carry):
        @pl.when(pl.program_id(1) == 0)
        def _():
            carry[...] = jnp.zeros_like(carry)

        u = u_ref[...]
        rows = lax.broadcasted_iota(jnp.int32, (t, gp), 0)
        ar = a_ref[0:1, :]
        ai = a_ref[1:2, :]
        cr = carry[0:1, :]
        ci = carry[1:2, :]
        first = rows == 0
        xr = _dot(u, br_ref[...]) + jnp.where(first, ar * cr - ai * ci, 0.0)
        xi = _dot(u, bi_ref[...]) + jnp.where(first, ar * ci + ai * cr, 0.0)
        for k in range(nlog):
            sh = 1 << k
            pr = pw_ref[2 * k:2 * k + 1, :]
            pi = pw_ref[2 * k + 1:2 * k + 2, :]
            keep = rows >= sh
            sr = jnp.where(keep, pltpu.roll(xr, sh, 0), 0.0)
            si = jnp.where(keep, pltpu.roll(xi, sh, 0), 0.0)
            xr, xi = xr + pr * sr - pi * si, xi + pr * si + pi * sr
        last = rows == t - 1
        carry[0:1, :] = jnp.sum(jnp.where(last, xr, 0.0), axis=0, keepdims=True)
        carry[1:2, :] = jnp.sum(jnp.where(last, xi, 0.0), axis=0, keepdims=True)
        xr_ref[...] = xr
        xi_ref[...] = xi
        y = _dot(xr.astype(BF16), cr_ref[...]) + _dot(xi.astype(BF16), ci_ref[...]) + d_ref[...] * u.astype(F32)
        yp_ref[...] = y
        zg = _dot(_gelu(y).astype(BF16), wg_ref[...])
        zg_ref[...] = zg
        o_ref[...] = (zg[:, :ds5] * _sigmoid(zg[:, ds5:])).astype(BF16)

    const = lambda shape: pl.BlockSpec(shape, lambda b, i: (0, 0))
    row = lambda w: pl.BlockSpec((t, w), lambda b, i: (b * nt + i, 0))
    return pl.pallas_call(
        body, grid=(bl, nt),
        in_specs=[row(ds5), const((ds5, gp)), const((ds5, gp)), const((2, gp)), const((2 * nlog, gp)),
                  const((gp, ds5)), const((gp, ds5)), const((1, ds5)),
                  pl.BlockSpec((None, ds5, 2 * ds5), lambda b, i: (l, 0, 0))],
        out_specs=[row(gp), row(gp), row(ds5), row(2 * ds5), row(ds5)],
        out_shape=[_sds((n, gp), F32), _sds((n, gp), F32), _sds((n, ds5), F32), _sds((n, 2 * ds5), F32),
                   _sds((n, ds5), BF16)],
        scratch_shapes=[pltpu.VMEM((2, gp), F32)],
        compiler_params=_cp("arbitrary", "arbitrary"), name=name)(
            proj, sp["bblk_r"], sp["bblk_i"], sp["a"], sp["pw"], sp["cblk_r"], sp["cblk_in"], sp["d"], wglu)


def _s5_bwd(ds, yp, zg, xr, xi, proj, sp, wglu, l, nl, dwg_buf, *, bl, s, t, name):
    n = bl * s
    ds5, gp = sp["bblk_r"].shape
    nt = s // t
    nlog = int(math.log2(t))
    tb = t // 8

    def body(ds_ref, yp_ref, zg_ref, xr_ref, xi_ref, hr_ref, hi_ref, u_ref, wg_ref, cr_ref, ci_ref,
             br_ref, bi_ref, a_ref, pw_ref, d_ref, *rest):
        du_ref, dd_ref, dcr_ref, dci_ref, dbr_ref, dbi_ref, da_ref, dwg_ref, carry = rest[-9:]
        b = pl.program_id(0)
        i = pl.program_id(1)
        tile = nt - 1 - i

        @pl.when((b == 0) & (i == 0))
        def _():
            for r in (dwg_ref, dd_ref, dcr_ref, dci_ref, dbr_ref, dbi_ref, da_ref):
                r[...] = jnp.zeros_like(r)

        @pl.when(i == 0)
        def _():
            carry[...] = jnp.zeros_like(carry)

        dsv = ds_ref[...].astype(F32)
        zgv = zg_ref[...]
        za = zgv[:, :ds5]
        sg = _sigmoid(zgv[:, ds5:])
        dzg = jnp.concatenate([dsv * sg, dsv * za * sg * (1.0 - sg)], axis=1).astype(BF16)
        y = yp_ref[...]
        dwg_ref[...] += _dot_t0(_gelu(y).astype(BF16), dzg)
        dy = _dot_t1(dzg, wg_ref[...]) * _gelu_grad(y)
        ub = u_ref[...]
        uf = ub.astype(F32)
        dd_ref[...] += jnp.sum(dy * uf, axis=0, keepdims=True)
        dyb = dy.astype(BF16)
        xrv = xr_ref[...]
        xiv = xi_ref[...]
        dcr_ref[...] += _dot_t0(xrv.astype(BF16), dyb)
        dci_ref[...] += _dot_t0(xiv.astype(BF16), dyb)

        rows = lax.broadcasted_iota(jnp.int32, (t, gp), 0)
        ar = a_ref[0:1, :]
        ai = a_ref[1:2, :]
        cr = carry[0:1, :]
        ci = carry[1:2, :]
        last = rows == t - 1
        gr = _dot_t1(dyb, cr_ref[...]) + jnp.where(last, ar * cr + ai * ci, 0.0)
        gi = _dot_t1(dyb, ci_ref[...]) + jnp.where(last, ar * ci - ai * cr, 0.0)
        for k in range(nlog):
            sh = 1 << k
            pr = pw_ref[2 * k:2 * k + 1, :]
            pi = pw_ref[2 * k + 1:2 * k + 2, :]
            keep = rows < t - sh
            sr = jnp.where(keep, pltpu.roll(gr, t - sh, 0), 0.0)
            si = jnp.where(keep, pltpu.roll(gi, t - sh, 0), 0.0)
            gr, gi = gr + pr * sr + pi * si, gi + pr * si - pi * sr
        first = rows == 0
        carry[0:1, :] = jnp.sum(jnp.where(first, gr, 0.0), axis=0, keepdims=True)
        carry[1:2, :] = jnp.sum(jnp.where(first, gi, 0.0), axis=0, keepdims=True)

        live = jnp.where(tile > 0, 1.0, 0.0)
        xpr = jnp.where(first, hr_ref[7:8, :] * live, pltpu.roll(xrv, 1, 0))
        xpi = jnp.where(first, hi_ref[7:8, :] * live, pltpu.roll(xiv, 1, 0))
        da_ref[0:1, :] += jnp.sum(gr * xpr + gi * xpi, axis=0, keepdims=True)
        da_ref[1:2, :] += jnp.sum(gi * xpr - gr * xpi, axis=0, keepdims=True)

        grb = gr.astype(BF16)
        gib = gi.astype(BF16)
        dbr_ref[...] += _dot_t0(ub, grb)
        dbi_ref[...] += _dot_t0(ub, gib)
        du_ref[...] = (_dot_t1(grb, br_ref[...]) + _dot_t1(gib, bi_ref[...]) + dy * d_ref[...]).astype(BF16)

    const = lambda shape: pl.BlockSpec(shape, lambda b, i: (0, 0))
    row = lambda w: pl.BlockSpec((t, w), lambda b, i: (b * nt + nt - 1 - i, 0))
    halo = pl.BlockSpec((8, gp), lambda b, i: (jnp.maximum((b * nt + nt - 1 - i) * tb - 1, 0), 0))
    extra, extra_specs, dwg_shape, aliases = _slab_out(nl, l, (ds5, 2 * ds5), dwg_buf, 16)
    aliases = {k: 7 for k in aliases}
    return pl.pallas_call(
        body, grid=(bl, nt),
        in_specs=[row(ds5), row(ds5), row(2 * ds5), row(gp), row(gp), halo, halo, row(ds5),
                  pl.BlockSpec((None, ds5, 2 * ds5), lambda b, i: (l, 0, 0)),
                  const((gp, ds5)), const((gp, ds5)), const((ds5, gp)), const((ds5, gp)),
                  const((2, gp)), const((2 * nlog, gp)), const((1, ds5)), *extra_specs],
        out_specs=[row(ds5), const((1, ds5)), const((gp, ds5)), const((gp, ds5)),
                   const((ds5, gp)), const((ds5, gp)), const((2, gp)),
                   pl.BlockSpec((None, ds5, 2 * ds5), lambda b, i: (l, 0, 0))],
        out_shape=[_sds((n, ds5), BF16), _sds((1, ds5), F32), _sds((gp, ds5), F32),
                   _sds((gp, ds5), F32), _sds((ds5, gp), F32), _sds((ds5, gp), F32), _sds((2, gp), F32), dwg_shape],
        input_output_aliases=aliases,
        scratch_shapes=[pltpu.VMEM((2, gp), F32)],
        compiler_params=_cp("arbitrary", "arbitrary"), name=name)(
            ds, yp, zg, xr, xi, xr, xi, proj, wglu, sp["cblk_r"], sp["cblk_in"],
            sp["bblk_r"], sp["bblk_i"], sp["a"], sp["pw"], sp["d"], *extra)


def _head_norm(x, first):
    x2 = x * x
    sa = jnp.sum(jnp.where(first, x2, 0.0), axis=-1, keepdims=True)
    sb = jnp.sum(jnp.where(first, 0.0, x2), axis=-1, keepdims=True)
    r = jnp.where(first, lax.rsqrt(sa * (1.0 / HEAD_DIM) + EPS), lax.rsqrt(sb * (1.0 / HEAD_DIM) + EPS))
    return x * r, r


def _attn_specs(bl, s, datt, qoff):
    nq = s // ATT_TQ
    nb = datt // LANES
    col = lambda blk: (lambda b, h, q: (b * nq + q, qoff + blk * nb + h))
    win = lambda blk, j: (lambda b, h, q: (b * nq + jnp.maximum(q - 2 + j, 0), qoff + blk * nb + h))
    tile = lambda f: pl.BlockSpec((ATT_TQ, LANES), f)
    qs = tile(col(0))
    ks = [tile(win(1, j)) for j in range(3)]
    vs = [tile(win(2, j)) for j in range(3)]
    return nq, nb, qs, ks, vs


def _attn_probs(q_ref, k_refs, gq_ref, gk_ref, bias_ref):
    qt = pl.program_id(2)
    lane = lax.broadcasted_iota(jnp.int32, (1, LANES), 1)
    first = lane < HEAD_DIM
    qh, rq = _head_norm(q_ref[...].astype(F32), first)
    qn = qh * gq_ref[...]
    kc = jnp.concatenate([r[...] for r in k_refs], axis=0).astype(F32)
    kh, _ = _head_norm(kc, first)
    kn = (kh * gk_ref[...]).astype(BF16)
    kpos = (qt - 2) * ATT_TQ + lax.broadcasted_iota(jnp.int32, (1, 3 * ATT_TQ), 1)
    valid = kpos >= 0
    scale = HEAD_DIM ** -0.5
    masks = (first, jnp.logical_not(first))
    qas, ps = [], []
    for hh in range(2):
        qa = jnp.where(masks[hh], qn, 0.0).astype(BF16)
        sc = _dot_t1(qa, kn) * scale + bias_ref[hh]
        sc = jnp.where(valid, sc, NEG)
        e = jnp.exp(sc - jnp.max(sc, axis=-1, keepdims=True))
        ps.append(e / jnp.sum(e, axis=-1, keepdims=True))
        qas.append(qa)
    return first, masks, qh, rq, kn, qas, ps


def _attn_fwd(proj, gq2, gk2, bias, *, bl, s, datt, qoff, name):
    n = bl * s
    nq, nb, qs, ks, vs = _attn_specs(bl, s, datt, qoff)

    def body(q_ref, k0, k1, k2, v0, v1, v2, gq_ref, gk_ref, bias_ref, o_ref):
        first, _, _, _, _, _, ps = _attn_probs(q_ref, (k0, k1, k2), gq_ref, gk_ref, bias_ref)
        vc = jnp.concatenate([v0[...], v1[...], v2[...]], axis=0)
        o0 = _dot(ps[0].astype(BF16), vc)
        o1 = _dot(ps[1].astype(BF16), vc)
        o_ref[...] = jnp.where(first, o0, o1).astype(BF16)

    gs = pl.BlockSpec((1, LANES), lambda b, h, q: (0, 0))
    return pl.pallas_call(
        body, grid=(bl, nb, nq),
        in_specs=[qs, *ks, *vs, gs, gs, pl.BlockSpec((2, ATT_TQ, 3 * ATT_TQ), lambda b, h, q: (h, 0, 0))],
        out_specs=pl.BlockSpec((ATT_TQ, LANES), lambda b, h, q: (b * nq + q, h)),
        out_shape=_sds((n, datt), BF16),
        compiler_params=_cp("parallel", "parallel", "arbitrary"), name=name)(
            proj, proj, proj, proj, proj, proj, proj, gq2, gk2, bias)


def _attn_bwd(do, proj, gq2, gk2, bias, *, bl, s, datt, qoff, name):
    n = bl * s
    nq, nb, qs, ks, vs = _attn_specs(bl, s, datt, qoff)
    srows = s + 2 * ATT_TQ
    scale = HEAD_DIM ** -0.5

    def body(do_ref, q_ref, k0, k1, k2, v0, v1, v2, gq_ref, gk_ref, bias_ref,
             dq_ref, dk_ref, dv_ref, db_ref, dgq_ref):
        qt = pl.program_id(2)

        @pl.when(qt == 0)
        def _():
            dk_ref[...] = jnp.zeros_like(dk_ref)
            dv_ref[...] = jnp.zeros_like(dv_ref)
            db_ref[...] = jnp.zeros_like(db_ref)
            dgq_ref[...] = jnp.zeros_like(dgq_ref)

        first, masks, qh, rq, kn, qas, ps = _attn_probs(q_ref, (k0, k1, k2), gq_ref, gk_ref, bias_ref)
        vc = jnp.concatenate([v0[...], v1[...], v2[...]], axis=0)
        dov = do_ref[...]
        dqn = jnp.zeros((ATT_TQ, LANES), F32)
        dkn = jnp.zeros((3 * ATT_TQ, LANES), F32)
        dv = jnp.zeros((3 * ATT_TQ, LANES), F32)
        for hh in range(2):
            doa = jnp.where(masks[hh], dov, jnp.zeros_like(dov))
            p = ps[hh]
            dp = _dot_t1(doa, vc)
            dsm = p * (dp - jnp.sum(dp * p, axis=-1, keepdims=True))
            db_ref[hh] += dsm
            dsc = (dsm * scale).astype(BF16)
            dqn = dqn + _dot(dsc, jnp.where(masks[hh], kn, jnp.zeros_like(kn)))
            dkn = dkn + _dot_t0(dsc, qas[hh])
            dv = dv + _dot_t0(p.astype(BF16), doa)
        start = pl.multiple_of(qt * ATT_TQ, ATT_TQ)
        dk_ref[pl.ds(start, 3 * ATT_TQ), :] += dkn
        dv_ref[pl.ds(start, 3 * ATT_TQ), :] += dv
        dgq_ref[...] += jnp.sum(dqn * qh, axis=0, keepdims=True)
        dqh = dqn * gq_ref[...]
        t = dqh * qh
        ma = jnp.sum(jnp.where(first, t, 0.0), axis=-1, keepdims=True) * (1.0 / HEAD_DIM)
        mb = jnp.sum(jnp.where(first, 0.0, t), axis=-1, keepdims=True) * (1.0 / HEAD_DIM)
        dq_ref[...] = (rq * (dqh - qh * jnp.where(first, ma, mb))).astype(BF16)

    gs = pl.BlockSpec((1, LANES), lambda b, h, q: (0, 0))
    acc = pl.BlockSpec((None, srows, LANES), lambda b, h, q: (b, 0, h))
    return pl.pallas_call(
        body, grid=(bl, nb, nq),
        in_specs=[pl.BlockSpec((ATT_TQ, LANES), lambda b, h, q: (b * nq + q, h)), qs, *ks, *vs, gs, gs,
                  pl.BlockSpec((2, ATT_TQ, 3 * ATT_TQ), lambda b, h, q: (h, 0, 0))],
        out_specs=[pl.BlockSpec((ATT_TQ, LANES), lambda b, h, q: (b * nq + q, h)), acc, acc,
                   pl.BlockSpec((None, 2, ATT_TQ, 3 * ATT_TQ), lambda b, h, q: (b, h, 0, 0)),
                   pl.BlockSpec((None, None, 1, LANES), lambda b, h, q: (b, h, 0, 0))],
        out_shape=[_sds((n, datt), BF16), _sds((bl, srows, datt), F32), _sds((bl, srows, datt), F32),
                   _sds((bl, 2 * nb, ATT_TQ, 3 * ATT_TQ), F32), _sds((bl, nb, 1, LANES), F32)],
        compiler_params=_cp("arbitrary", "arbitrary", "arbitrary"), name=name)(
            do, proj, proj, proj, proj, proj, proj, proj, gq2, gk2, bias)


def _attn_kv_bwd(dkn, dv, proj, gk2, *, bl, s, datt, tm, koff, name):
    n = bl * s
    ns = s // tm
    off = 2 * ATT_TQ // tm
    nb = datt // LANES

    def body(dkn_ref, dv_ref, k_ref, gk_ref, dk_ref, dvo_ref, dgk_ref):
        lane = lax.broadcasted_iota(jnp.int32, (1, LANES), 1)
        first = lane < HEAD_DIM

        @pl.when((pl.program_id(0) == 0) & (pl.program_id(1) == 0) & (pl.program_id(2) == 0))
        def _():
            dgk_ref[...] = jnp.zeros_like(dgk_ref)

        dvo_ref[...] = dv_ref[...].astype(BF16)
        kh, rk = _head_norm(k_ref[...].astype(F32), first)
        dn = dkn_ref[...]
        dgk_ref[...] += jnp.sum(dn * kh, axis=0, keepdims=True)
        dh = dn * gk_ref[...]
        t = dh * kh
        ma = jnp.sum(jnp.where(first, t, 0.0), axis=-1, keepdims=True) * (1.0 / HEAD_DIM)
        mb = jnp.sum(jnp.where(first, 0.0, t), axis=-1, keepdims=True) * (1.0 / HEAD_DIM)
        dk_ref[...] = (rk * (dh - kh * jnp.where(first, ma, mb))).astype(BF16)

    accs = pl.BlockSpec((None, tm, LANES), lambda b, i, c: (b, i + off, c))
    outs = pl.BlockSpec((tm, LANES), lambda b, i, c: (b * ns + i, c))
    vec = pl.BlockSpec((1, LANES), lambda b, i, c: (0, 0))
    return pl.pallas_call(
        body, grid=(bl, ns, nb),
        in_specs=[accs, accs, pl.BlockSpec((tm, LANES), lambda b, i, c: (b * ns + i, koff + c)), vec],
        out_specs=[outs, outs, vec],
        out_shape=[_sds((n, datt), BF16), _sds((n, datt), BF16), _sds((1, LANES), F32)],
        compiler_params=_cp("arbitrary", "arbitrary", "arbitrary"), name=name)(dkn, dv, proj, gk2)


def _conv_fwd(proj, wdw, bdw, lng, lnb, *, bl, s, t, acol, name):
    n = bl * s
    dc = wdw.shape[1]
    nt = s // t
    hb = t // HALO

    def body(za_ref, zg_ref, ha_ref, hgt_ref, w_ref, b_ref, g_ref, be_ref, hg_ref, hc_ref, o_ref, ext):
        i = pl.program_id(1)
        hg = za_ref[...].astype(F32) * _sigmoid(zg_ref[...].astype(F32))
        live = jnp.where(i > 0, 1.0, 0.0)
        ext[0:HALO, :] = ha_ref[...].astype(F32) * _sigmoid(hgt_ref[...].astype(F32)) * live
        ext[HALO:HALO + t, :] = hg
        hg_ref[...] = hg
        acc = jnp.zeros((t, dc), F32) + b_ref[...]
        for j in range(CONV_W):
            acc = acc + w_ref[j:j + 1, :] * ext[pl.ds(HALO - (CONV_W - 1) + j, t), :]
        hc_ref[...] = acc
        mu = jnp.mean(acc, axis=-1, keepdims=True)
        xc = acc - mu
        rs = lax.rsqrt(jnp.mean(xc * xc, axis=-1, keepdims=True) + EPS)
        ln = xc * rs * g_ref[...] + be_ref[...]
        o_ref[...] = (ln * _sigmoid(ln)).astype(BF16)

    vec = pl.BlockSpec((1, dc), lambda b, i: (0, 0))
    row = pl.BlockSpec((t, dc), lambda b, i: (b * nt + i, 0))
    tile = lambda c: pl.BlockSpec((t, dc), lambda b, i: (b * nt + i, c))
    halo = lambda c: pl.BlockSpec((HALO, dc), lambda b, i: (jnp.maximum((b * nt + i) * hb - 1, 0), c))
    return pl.pallas_call(
        body, grid=(bl, nt),
        in_specs=[tile(acol), tile(acol + 1), halo(acol), halo(acol + 1),
                  pl.BlockSpec((HALO, dc), lambda b, i: (0, 0)), vec, vec, vec],
        out_specs=[row, row, row],
        out_shape=[_sds((n, dc), F32), _sds((n, dc), F32), _sds((n, dc), BF16)],
        scratch_shapes=[pltpu.VMEM((HALO + t, dc), F32)],
        compiler_params=_cp("parallel", "arbitrary"), name=name)(proj, proj, proj, proj, wdw, bdw, lng, lnb)


def _conv_bwd_ln(dco, hc, lng, lnb, *, tm, name):
    n, dc = hc.shape

    def body(d_ref, hc_ref, g_ref, be_ref, dhc_ref, dg_ref, db_ref):
        @pl.when(pl.program_id(0) == 0)
        def _():
            dg_ref[...] = jnp.zeros_like(dg_ref)
            db_ref[...] = jnp.zeros_like(db_ref)

        hcv = hc_ref[...]
        mu = jnp.mean(hcv, axis=-1, keepdims=True)
        xc = hcv - mu
        rs = lax.rsqrt(jnp.mean(xc * xc, axis=-1, keepdims=True) + EPS)
        xh = xc * rs
        ln = xh * g_ref[...] + be_ref[...]
        sg = _sigmoid(ln)
        dln = d_ref[...].astype(F32) * (sg * (1.0 + ln * (1.0 - sg)))
        db_ref[...] += jnp.sum(dln, axis=0, keepdims=True)
        dg_ref[...] += jnp.sum(dln * xh, axis=0, keepdims=True)
        dxh = dln * g_ref[...]
        dhc_ref[...] = rs * (dxh - jnp.mean(dxh, axis=-1, keepdims=True)
                             - xh * jnp.mean(dxh * xh, axis=-1, keepdims=True))

    vec = pl.BlockSpec((1, dc), lambda i: (0, 0))
    row = pl.BlockSpec((tm, dc), lambda i: (i, 0))
    return pl.pallas_call(
        body, grid=(n // tm,), in_specs=[row, row, vec, vec], out_specs=[row, vec, vec],
        out_shape=[_sds((n, dc), F32), _sds((1, dc), F32), _sds((1, dc), F32)],
        compiler_params=_cp("arbitrary"), name=name)(dco, hc, lng, lnb)


def _conv_bwd_dw(dhc, hg, proj, wdw, *, bl, s, t, acol, name):
    n = bl * s
    dc = wdw.shape[1]
    nt = s // t
    hb = t // HALO
    lastblk = n // HALO - 1

    def body(d_ref, dn_ref, hg_ref, hp_ref, za_ref, zg_ref, w_ref, dz_ref, dw_ref, dbias_ref, extd, exth):
        b = pl.program_id(0)
        i = pl.program_id(1)

        @pl.when((b == 0) & (i == 0))
        def _():
            dw_ref[...] = jnp.zeros_like(dw_ref)
            dbias_ref[...] = jnp.zeros_like(dbias_ref)

        dv = d_ref[...]
        extd[0:t, :] = dv
        extd[t:t + HALO, :] = dn_ref[...] * jnp.where(i < nt - 1, 1.0, 0.0)
        exth[0:HALO, :] = hp_ref[...] * jnp.where(i > 0, 1.0, 0.0)
        exth[HALO:HALO + t, :] = hg_ref[...]
        dbias_ref[...] += jnp.sum(dv, axis=0, keepdims=True)
        dhg = jnp.zeros((t, dc), F32)
        for j in range(CONV_W):
            dhg = dhg + w_ref[j:j + 1, :] * extd[pl.ds(CONV_W - 1 - j, t), :]
            dw_ref[j:j + 1, :] += jnp.sum(dv * exth[pl.ds(HALO - (CONV_W - 1) + j, t), :], axis=0, keepdims=True)
        za = za_ref[...].astype(F32)
        sg = _sigmoid(zg_ref[...].astype(F32))
        dz_ref[...] = jnp.concatenate([dhg * sg, dhg * za * sg * (1.0 - sg)], axis=1).astype(BF16)

    row = pl.BlockSpec((t, dc), lambda b, i: (b * nt + i, 0))
    nxt = pl.BlockSpec((HALO, dc), lambda b, i: (jnp.minimum((b * nt + i + 1) * hb, lastblk), 0))
    prv = pl.BlockSpec((HALO, dc), lambda b, i: (jnp.maximum((b * nt + i) * hb - 1, 0), 0))
    wsp = pl.BlockSpec((HALO, dc), lambda b, i: (0, 0))
    tile = lambda c: pl.BlockSpec((t, dc), lambda b, i: (b * nt + i, c))
    return pl.pallas_call(
        body, grid=(bl, nt),
        in_specs=[row, nxt, row, prv, tile(acol), tile(acol + 1), wsp],
        out_specs=[pl.BlockSpec((t, 2 * dc), lambda b, i: (b * nt + i, 0)), wsp,
                   pl.BlockSpec((1, dc), lambda b, i: (0, 0))],
        out_shape=[_sds((n, 2 * dc), BF16), _sds((HALO, dc), F32), _sds((1, dc), F32)],
        scratch_shapes=[pltpu.VMEM((t + HALO, dc), F32), pltpu.VMEM((HALO + t, dc), F32)],
        compiler_params=_cp("arbitrary", "arbitrary"), name=name)(dhc, dhc, hg, hg, proj, proj, wdw)


def _mix_out_fwd(x, brs, gl, bg, wbs, wout, l, *, tm, name):
    n, d = x.shape

    def body(x_ref, s_ref, a_ref, c_ref, g0, g1, g2, bg_ref, ws, wa, wc, wo, o_ref):
        merged = jnp.zeros((tm, d), F32)
        for k, (br, gr, w) in enumerate(((s_ref, g0, ws), (a_ref, g1, wa), (c_ref, g2, wc))):
            gate = _sigmoid(gr[...].astype(F32) + bg_ref[:, k * d:(k + 1) * d])
            merged = merged + gate * _dot(br[...], w[...])
        o_ref[...] = x_ref[...] + _dot(merged.astype(BF16), wo[...])

    row = lambda w: pl.BlockSpec((tm, w), lambda i: (i, 0))
    wsp = lambda a: pl.BlockSpec((None,) + a.shape[1:], lambda i: (l, 0, 0))
    gls = [pl.BlockSpec((tm, d), functools.partial(lambda k, i: (i, k), k)) for k in range(3)]
    return pl.pallas_call(
        body, grid=(n // tm,),
        in_specs=[row(d), *[row(b.shape[1]) for b in brs], *gls, pl.BlockSpec(bg.shape, lambda i: (0, 0)),
                  *[wsp(w) for w in wbs], wsp(wout)],
        out_specs=row(d), out_shape=_sds((n, d), F32),
        compiler_params=_cp("parallel"), name=name)(x, *brs, gl, gl, gl, bg, *wbs, wout)


def _mix_out_bwd(dx, brs, gl, bg, wbs, wout, l, nl, bufs, *, tm, name):
    n, d = dx.shape
    widths = [b.shape[1] for b in brs]

    def body(dx_ref, s_ref, a_ref, c_ref, g0, g1, g2, bg_ref, ws, wa, wc, wo, *rest):
        ds_ref, da_ref, dc_ref, dgl_ref, dbg_ref, dws, dwa, dwc, dwo = rest[-9:]

        @pl.when(pl.program_id(0) == 0)
        def _():
            for r in (dbg_ref, dws, dwa, dwc, dwo):
                r[...] = jnp.zeros_like(r)

        dxb = dx_ref[...].astype(BF16)
        dm = _dot_t1(dxb, wo[...])
        merged = jnp.zeros((tm, d), F32)
        for k, (br, gr, w, dbr, dw) in enumerate(((s_ref, g0, ws, ds_ref, dws), (a_ref, g1, wa, da_ref, dwa),
                                                   (c_ref, g2, wc, dc_ref, dwc))):
            gate = _sigmoid(gr[...].astype(F32) + bg_ref[:, k * d:(k + 1) * d])
            brv = br[...]
            wv = w[...]
            y = _dot(brv, wv)
            merged = merged + gate * y
            dyb = (dm * gate).astype(BF16)
            dbr[...] = _dot_t1(dyb, wv).astype(BF16)
            dw[...] += _dot_t0(brv, dyb)
            dgl = dm * y * gate * (1.0 - gate)
            dgl_ref[:, k * d:(k + 1) * d] = dgl.astype(BF16)
            dbg_ref[:, k * d:(k + 1) * d] += jnp.sum(dgl, axis=0, keepdims=True)
        dwo[...] += _dot_t0(merged.astype(BF16), dxb)

    row = lambda w: pl.BlockSpec((tm, w), lambda i: (i, 0))
    wsp = lambda shape: pl.BlockSpec((None,) + tuple(shape), lambda i: (l, 0, 0))
    gls = [pl.BlockSpec((tm, d), functools.partial(lambda k, i: (i, k), k)) for k in range(3)]
    slabs = [(w, d) for w in widths] + [(d, d)]
    n_in = 12
    extra = [] if bufs is None else list(bufs)
    aliases = {} if bufs is None else {n_in + k: 5 + k for k in range(4)}
    return pl.pallas_call(
        body, grid=(n // tm,),
        in_specs=[row(d), *[row(w) for w in widths], *gls, pl.BlockSpec(bg.shape, lambda i: (0, 0)),
                  *[wsp(w.shape[1:]) for w in wbs], wsp(wout.shape[1:]), *[_ANY for _ in extra]],
        out_specs=[*[row(w) for w in widths], row(3 * d), pl.BlockSpec((1, 3 * d), lambda i: (0, 0)),
                   *[wsp(sh) for sh in slabs]],
        out_shape=[*[_sds((n, w), BF16) for w in widths], _sds((n, 3 * d), BF16), _sds((1, 3 * d), F32),
                   *[_sds((nl,) + sh, F32) for sh in slabs]],
        input_output_aliases=aliases,
        compiler_params=_cp("arbitrary"), name=name)(dx, *brs, gl, gl, gl, bg, *wbs, wout, *extra)


def _adamw(w, g, m, v, *, name):
    r, c = w.shape
    tm = _tile(r, 256)
    c1 = 1.0 - ADAM_B1 ** ADAM_STEP
    c2 = 1.0 - ADAM_B2 ** ADAM_STEP

    def body(w_ref, g_ref, m_ref, v_ref, d_ref, nm_ref, nv_ref):
        gv = g_ref[...]
        mn = ADAM_B1 * m_ref[...] + (1.0 - ADAM_B1) * gv
        vn = ADAM_B2 * v_ref[...] + (1.0 - ADAM_B2) * (gv * gv)
        nm_ref[...] = mn
        nv_ref[...] = vn
        d_ref[...] = -ADAM_LR * ((mn / c1) / (jnp.sqrt(vn / c2) + ADAM_EPS) + ADAM_WD * w_ref[...])

    blk = pl.BlockSpec((tm, c), lambda i: (i, 0))
    return pl.pallas_call(
        body, grid=(r // tm,), in_specs=[blk] * 4, out_specs=[blk] * 3,
        out_shape=[_sds((r, c), F32)] * 3, compiler_params=_cp("parallel"), name=name)(w, g, m, v)


def _add_sibling(g, recv, c_idx, *, name):
    _, a, b = g.shape
    ta = _tile(a, 256)

    def body(c_ref, g_ref, r_ref, o_ref):
        o_ref[...] = (g_ref[...] + r_ref[...]).astype(BF16)

    return pl.pallas_call(
        body,
        grid_spec=pltpu.PrefetchScalarGridSpec(
            num_scalar_prefetch=1, grid=(a // ta,),
            in_specs=[pl.BlockSpec((None, ta, b), lambda i, c_ref: (c_ref[0], i, 0)),
                      pl.BlockSpec((ta, b), lambda i, c_ref: (i, 0))],
            out_specs=pl.BlockSpec((ta, b), lambda i, c_ref: (i, 0))),
        out_shape=_sds((a, b), BF16), compiler_params=_cp("parallel"), name=name)(c_idx, g, recv)


def _add_chips(parts, *, name):
    _, a, b = parts.shape
    ta = _tile(a, 256)

    def body(p0, p1, p2, p3, o_ref):
        o_ref[...] = ((p0[...].astype(F32) + p1[...].astype(F32)) + p2[...].astype(F32)) + p3[...].astype(F32)

    specs = [pl.BlockSpec((None, ta, b), functools.partial(lambda s, i: (s, i, 0), s)) for s in range(N_CHIPS)]
    return pl.pallas_call(
        body, grid=(a // ta,), in_specs=specs, out_specs=pl.BlockSpec((ta, b), lambda i: (i, 0)),
        out_shape=_sds((a, b), F32), compiler_params=_cp("parallel"), name=name)(parts, parts, parts, parts)


def _blockdiag(w):
    g, r, c = w.shape
    eye = jnp.eye(g, dtype=w.dtype)
    return (w[:, :, None, :] * eye[:, None, :, None]).reshape(g * r, g * c)


def _s5_prep(lre, lim, log_dt, b_re, b_im, c_re, c_im, d_skip):
    lr = jnp.minimum(lre, -1e-4)
    li = lim
    dt = jnp.exp(log_dt)[:, None]
    mag = jnp.exp(lr * dt)
    ar = mag * jnp.cos(li * dt)
    ai = mag * jnp.sin(li * dt)
    den = lr * lr + li * li
    coef_r = ((ar - 1.0) * lr + ai * li) / den
    coef_i = (ai * lr - (ar - 1.0) * li) / den
    bbar_r = coef_r[..., None] * b_re - coef_i[..., None] * b_im
    bbar_i = coef_r[..., None] * b_im + coef_i[..., None] * b_re
    a = jnp.stack([ar.reshape(-1), ai.reshape(-1)])
    return dict(
        a=a,
        bblk_r=_blockdiag(bbar_r.transpose(0, 2, 1)), bblk_i=_blockdiag(bbar_i.transpose(0, 2, 1)),
        cblk_r=_blockdiag(c_re.transpose(0, 2, 1)), cblk_in=_blockdiag(-c_im.transpose(0, 2, 1)),
        d=d_skip.reshape(1, -1))


def _s5_powers(a, nlog):
    pr, pi = a[0], a[1]
    rows = []
    for _ in range(nlog):
        rows += [pr, pi]
        pr, pi = pr * pr - pi * pi, 2.0 * pr * pi
    return jnp.stack(rows)


def _bias_table(rel_bias):
    h = rel_bias.shape[0]
    tq, tw = ATT_TQ, 3 * ATT_TQ
    n_hi = tw - 1 - MAX_REL + 1
    n_lo = tq + tw - 1 - n_hi - (2 * MAX_REL - 1)
    fr = jnp.concatenate([
        jnp.broadcast_to(rel_bias[:, 2 * MAX_REL:], (h, n_hi)),
        jnp.flip(rel_bias[:, 1:2 * MAX_REL], axis=1),
        jnp.broadcast_to(rel_bias[:, :1], (h, n_lo)),
        jnp.zeros((h, 1), rel_bias.dtype)], axis=1)
    ln = tq + tw
    flat = jnp.broadcast_to(fr[:, None, :], (h, tq, ln)).reshape(h, tq * ln)[:, :tq * (ln - 1)]
    tab = flat.reshape(h, tq, ln - 1)[:, :, tq - 1:tq - 1 + tw]
    qc = np.arange(tq)[:, None] // CHUNK + N_LEFT
    kc = np.arange(tw)[None, :] // CHUNK
    band = (kc <= qc) & (kc >= qc - N_LEFT)
    return jnp.where(jnp.asarray(band)[None], tab, NEG)


def _small_prep(w, l):
    sp = _s5_prep(w["s5_lambda_re"][l], w["s5_lambda_im"][l], w["s5_log_dt"][l], w["s5_b_re"][l], w["s5_b_im"][l],
                  w["s5_c_re"][l], w["s5_c_im"][l], w["s5_d"][l])
    return sp, _bias_table(w["attn_rel_bias"][l])


_PREP_KEYS = ("s5_lambda_re", "s5_lambda_im", "s5_log_dt", "s5_b_re", "s5_b_im", "s5_c_re", "s5_c_im", "s5_d",
              "attn_rel_bias")
_BIG_KEYS = {"ffn1_w_up": 2, "ffn1_w_down": 1, "w_in": 2, "s5_w_glu": 2, "w_br_s5": 2, "w_br_attn": 2,
             "w_br_conv": 2, "w_out": 1, "ffn2_w_up": 2, "ffn2_w_down": 1}
_SMALL_KEYS = ("ffn1_norm", "mix_norm", "b_gate", "s5_lambda_re", "s5_lambda_im", "s5_log_dt", "s5_b_re", "s5_b_im",
               "s5_c_re", "s5_c_im", "s5_d", "attn_q_gain", "attn_k_gain", "attn_rel_bias", "conv_w_dw", "conv_b_dw",
               "conv_ln_g", "conv_ln_b", "ffn2_norm")
_WEIGHTS = ("ffn1_norm", "ffn1_w_up", "ffn1_w_down", "mix_norm", "w_in", "b_gate", "s5_lambda_re", "s5_lambda_im",
            "s5_log_dt", "s5_b_re", "s5_b_im", "s5_c_re", "s5_c_im", "s5_d", "s5_w_glu", "w_br_s5", "attn_q_gain",
            "attn_k_gain", "attn_rel_bias", "w_br_attn", "conv_w_dw", "conv_b_dw", "conv_ln_g", "conv_ln_b",
            "w_br_conv", "w_out", "ffn2_norm", "ffn2_w_up", "ffn2_w_down")


def _local_step(x3, target3, w):
    bl, s, d = x3.shape
    nl = w["ffn1_norm"].shape[0]
    dff = w["ffn1_w_down"].shape[1]
    ds5 = w["s5_d"].shape[1]
    datt = w["w_br_attn"].shape[1]
    dc = w["conv_b_dw"].shape[1]
    n = bl * s
    x = x3.reshape(n, d)
    target = target3.reshape(n, d)
    tm = _tile(n, 512)
    tmix = _tile(n, 256)
    ts5 = 256
    tconv = _tile(s, 512)
    nlog = int(math.log2(ts5))
    tff = dff // 2
    ma = ds5 + 3 * datt + 2 * dc
    tna = ma // 3
    assert (3 * d) % tna == 0 and dff % 2 == 0
    qoff = ds5 // LANES
    koff = (ds5 + datt) // LANES
    acol = (ds5 + 3 * datt) // dc
    wbs = (w["w_br_s5"], w["w_br_attn"], w["w_br_conv"])

    saved = []
    for l in range(nl):
        (sp, bias), prep_vjp = jax.vjp(lambda ww: _small_prep(ww, l), {k: w[k] for k in _PREP_KEYS})
        spb = dict(sp)
        spb["pw"] = _s5_powers(lax.stop_gradient(sp["a"]), nlog)
        for k in ("bblk_r", "bblk_i", "cblk_r", "cblk_in"):
            spb[k] = sp[k].astype(BF16)
        g1 = w["ffn1_norm"][l][None]
        g2 = w["ffn2_norm"][l][None]
        gm = w["mix_norm"][l][None]
        gq2 = jnp.tile(w["attn_q_gain"][l], 2)[None]
        gk2 = jnp.tile(w["attn_k_gain"][l], 2)[None]
        wdw = jnp.pad(w["conv_w_dw"][l], ((0, HALO - CONV_W), (0, 0)))
        bdw, lng, lnb = w["conv_b_dw"][l][None], w["conv_ln_g"][l][None], w["conv_ln_b"][l][None]
        bg = w["b_gate"][l][None]

        x0 = x
        h1, ab1 = _norm_mm(x0, g1, w["ffn1_w_up"], l, tm=tm, tn=tff, ntiles=4, pieces=2, transposed=False,
                           name=f"ffn1_up_{l}")
        x1 = _ffn_down(ab1, w["ffn1_w_down"], l, x0, tm=tm, tk=tff, name=f"ffn1_down_{l}")
        h2, pa = _norm_mm(x1, gm, w["w_in"], l, tm=tm, tn=tna, ntiles=3, pieces=1, transposed=True, name=f"win_a_{l}")
        pa = pa[0]
        gl = _mm_t(h2, w["w_in"], l, tm=tm, tn=tna, off=3, ntiles=3 * d // tna, name=f"win_g_{l}")
        xr, xi, yp, zg, s5o = _s5_fwd(pa, spb, w["s5_w_glu"], l, bl=bl, s=s, t=ts5, name=f"s5_fwd_{l}")
        atto = _attn_fwd(pa, gq2, gk2, bias, bl=bl, s=s, datt=datt, qoff=qoff, name=f"attn_fwd_{l}")
        hg, hc, convo = _conv_fwd(pa, wdw, bdw, lng, lnb, bl=bl, s=s, t=tconv, acol=acol, name=f"conv_fwd_{l}")
        brs = (s5o, atto, convo)
        x2 = _mix_out_fwd(x1, brs, gl, bg, wbs, w["w_out"], l, tm=tmix, name=f"mix_fwd_{l}")
        h3, ab2 = _norm_mm(x2, g2, w["ffn2_w_up"], l, tm=tm, tn=tff, ntiles=4, pieces=2, transposed=False,
                           name=f"ffn2_up_{l}")
        x = _ffn_down(ab2, w["ffn2_w_down"], l, x2, tm=tm, tk=tff, name=f"ffn2_down_{l}")
        saved.append(dict(spb=spb, bias=bias, prep_vjp=prep_vjp, g1=g1, g2=g2, gm=gm, gq2=gq2, gk2=gk2,
                          wdw=wdw, lng=lng, lnb=lnb, bg=bg, x0=x0, h1=h1, ab1=ab1, x1=x1, h2=h2, pa=pa, gl=gl,
                          xr=xr, xi=xi, yp=yp, zg=zg, hg=hg, hc=hc, brs=brs, x2=x2, h3=h3, ab2=ab2))

    dx, lsum = _loss_grad(x, target, tm=tm, name="loss")
    loss_part = 0.5 * jnp.sum(lsum) / d

    big = {k: None for k in _BIG_KEYS}
    small = {k: [None] * nl for k in _SMALL_KEYS}
    for l in reversed(range(nl)):
        sv = saved[l]

        def ffn_bwd(dx, xin, h, ab, g, tag):
            wu, wd = w[tag + "_w_up"], w[tag + "_w_down"]
            dab, act, dob = _ffn_dact(dx, wd, l, ab, tm=tm, tk=tff, name=f"{tag}_dact_{l}")
            big[tag + "_w_down"] = _mm_tn(act[None], dob[None], l, nl, big[tag + "_w_down"], ta=tff, tb=d, tk=tm,
                                          name=f"{tag}_dwd_{l}")
            big[tag + "_w_up"] = _mm_tn(h[None], dab, l, nl, big[tag + "_w_up"], ta=d, tb=tff, tk=tm,
                                        name=f"{tag}_dwu_{l}")
            dxo, dg = _ffn_dx(dab, wu, l, xin, g, dx, tm=tm, tk=tff, name=f"{tag}_dx_{l}")
            small[tag + "_norm"][l] = dg[0]
            return dxo

        dx = ffn_bwd(dx, sv["x2"], sv["h3"], sv["ab2"], sv["g2"], "ffn2")

        mix_keys = ("w_br_s5", "w_br_attn", "w_br_conv", "w_out")
        bufs = None if big["w_out"] is None else [big[k] for k in mix_keys]
        ds5o, datto, dconvo, dgl, dbg, *dws = _mix_out_bwd(
            dx, sv["brs"], sv["gl"], sv["bg"], wbs, w["w_out"], l, nl, bufs, tm=tmix, name=f"mix_bwd_{l}")
        small["b_gate"][l] = dbg[0]
        big.update(zip(mix_keys, dws))

        dhc, dlng, dlnb = _conv_bwd_ln(dconvo, sv["hc"], sv["lng"], sv["lnb"], tm=tm, name=f"conv_bwd_ln_{l}")
        dz, dwdw, dbdw = _conv_bwd_dw(dhc, sv["hg"], sv["pa"], sv["wdw"], bl=bl, s=s, t=tconv, acol=acol,
                                      name=f"conv_bwd_dw_{l}")
        small["conv_w_dw"][l] = dwdw[:CONV_W]
        small["conv_b_dw"][l], small["conv_ln_g"][l], small["conv_ln_b"][l] = dbdw[0], dlng[0], dlnb[0]

        dq, dkn, dvw, dbias, dgq = _attn_bwd(datto, sv["pa"], sv["gq2"], sv["gk2"], sv["bias"], bl=bl, s=s, datt=datt,
                                             qoff=qoff, name=f"attn_bwd_{l}")
        dk, dv, dgk = _attn_kv_bwd(dkn, dvw, sv["pa"], sv["gk2"], bl=bl, s=s, datt=datt, tm=_tile(s, 512), koff=koff,
                                   name=f"attn_kv_bwd_{l}")
        small["attn_q_gain"][l] = jnp.sum(dgq.reshape(-1, HEAD_DIM), axis=0)
        small["attn_k_gain"][l] = jnp.sum(dgk.reshape(-1, HEAD_DIM), axis=0)

        du, dd, dcr, dci, dbr, dbi, da, big["s5_w_glu"] = _s5_bwd(
            ds5o, sv["yp"], sv["zg"], sv["xr"], sv["xi"], sv["pa"], sv["spb"], w["s5_w_glu"], l, nl, big["s5_w_glu"],
            bl=bl, s=s, t=ts5, name=f"s5_bwd_{l}")
        prep_ct = (dict(a=da, bblk_r=dbr, bblk_i=dbi, cblk_r=dcr, cblk_in=dci, d=dd), jnp.sum(dbias, axis=0))
        (dprep,) = sv["prep_vjp"](prep_ct)
        for k in _PREP_KEYS:
            small[k][l] = dprep[k][l]

        dpa = jnp.concatenate([du, dq, dk, dv, dz], axis=1)
        big["w_in"] = _dwin_t(dpa, dgl, sv["h2"], l, nl, big["w_in"], ta=tna, tk=tm, name=f"dwin_{l}")
        dx, dgm = _mix_dx(dpa, dgl, w["w_in"], l, sv["x1"], sv["gm"], dx, tm=tm, tk=tna, name=f"mix_dx_{l}")
        small["mix_norm"][l] = dgm[0]

        dx = ffn_bwd(dx, sv["x0"], sv["h1"], sv["ab1"], sv["g1"], "ffn1")

    small = {k: jnp.stack(v) for k, v in small.items()}
    return loss_part, dx.reshape(bl, s, d), big, small


def _place():
    x, y, c = lax.axis_index("x"), lax.axis_index("y"), lax.axis_index("c")
    chips = [(1 - x, y), (x, 1 - y), (1 - x, 1 - y)]
    return x, y, c, chips


def _remote(src, dst, send_sems, recv_sems, k, dev):
    return pltpu.make_async_remote_copy(src_ref=src, dst_ref=dst, send_sem=send_sems.at[k], recv_sem=recv_sems.at[k],
                                        device_id=dev, device_id_type=MESH)


def _window(ref, lead, s, axis, blk):
    if axis == 1:
        sl = (pl.ds(pl.multiple_of(s * blk, 16), blk), slice(None))
    else:
        sl = (slice(None), pl.ds(pl.multiple_of(s * blk, LANES), blk))
    return ref.at[sl] if lead is None else ref.at[(lead,) + sl]


def _all_gather_weights(shards, axes, taps):
    nw = len(shards)
    nl = shards[0].shape[0]
    assert nl == 2
    blks = [sh.shape[ax] for sh, ax in zip(shards, axes)]
    fulls = [tuple(dim * N_CHIPS if i == ax else dim for i, dim in enumerate(sh.shape)) for sh, ax in zip(shards, axes)]

    def body(*refs):
        ins, taps_in = refs[:nw], refs[nw]
        outs, taps_out = refs[nw + 1:2 * nw + 1], refs[2 * nw + 1]
        send_sems, recv_sems, local_sems = refs[-3:]
        x, y, c, chips = _place()
        s_me = 2 * x + y
        sibling = (x, y, 1 - c)
        win = lambda i, lyr, s: _window(outs[i], lyr, s, axes[i], blks[i])
        local = []
        for i in range(nw):
            for lyr in range(nl):
                local.append(pltpu.make_async_copy(ins[i].at[lyr], win(i, lyr, s_me), local_sems.at[i * nl + lyr]))
        local.append(pltpu.make_async_copy(taps_in, taps_out.at[s_me], local_sems.at[nw * nl]))
        for cp in local:
            cp.start()
        sends = []
        for i in range(nw):
            for j, (cx, cy) in enumerate(chips):
                sends.append(_remote(ins[i].at[c], win(i, c, s_me), send_sems, recv_sems, 6 * i + j, (cx, cy, c)))
        for j, (cx, cy) in enumerate(chips):
            sends.append(_remote(taps_in, taps_out.at[s_me], send_sems, recv_sems, 6 * nw + j, (cx, cy, c)))
        for cp in sends:
            cp.start()
        for i in range(nw):
            for j, (cx, cy) in enumerate(chips):
                piece = win(i, c, 2 * cx + cy)
                _remote(piece, piece, send_sems, recv_sems, 6 * i + j, (cx, cy, c)).wait_recv()
                fw = _remote(piece, piece, send_sems, recv_sems, 6 * i + 3 + j, sibling)
                fw.start()
                sends.append(fw)
        for i in range(nw):
            for j, (cx, cy) in enumerate(chips):
                piece = win(i, 1 - c, 2 * cx + cy)
                _remote(piece, piece, send_sems, recv_sems, 6 * i + 3 + j, sibling).wait_recv()
        for j, (cx, cy) in enumerate(chips):
            slab = taps_out.at[2 * cx + cy]
            _remote(slab, slab, send_sems, recv_sems, 6 * nw + j, (cx, cy, c)).wait_recv()
        for cp in sends:
            cp.wait_send()
        for cp in local:
            cp.wait()

    nsem = 6 * nw + 3
    return pl.pallas_call(
        body, in_specs=[_ANY] * (nw + 1), out_specs=[_ANY] * (nw + 1),
        out_shape=[_sds(f, sh.dtype) for f, sh in zip(fulls, shards)] + [_sds((N_CHIPS,) + taps.shape, taps.dtype)],
        scratch_shapes=[pltpu.SemaphoreType.DMA((nsem,)), pltpu.SemaphoreType.DMA((nsem,)),
                        pltpu.SemaphoreType.DMA((nw * nl + 1,))],
        name="all_gather_weights")(*shards, taps)


def _rs_swap(grads):
    nw = len(grads)

    def body(*refs):
        ins, outs = refs[:nw], refs[nw:2 * nw]
        send_sems, recv_sems = refs[-2:]
        x, y, c, _ = _place()
        cps = [_remote(ins[i].at[1 - c], outs[i], send_sems, recv_sems, i, (x, y, 1 - c)) for i in range(nw)]
        for cp in cps:
            cp.start()
        for cp in cps:
            cp.wait()

    return pl.pallas_call(
        body, in_specs=[_ANY] * nw, out_specs=[_ANY] * nw, out_shape=[_sds(g.shape[1:], g.dtype) for g in grads],
        scratch_shapes=[pltpu.SemaphoreType.DMA((nw,)), pltpu.SemaphoreType.DMA((nw,))],
        name="rs_swap_layers")(*grads)


def _rs_exchange(rsums, axes):
    nw = len(rsums)
    blks = [r.shape[ax - 1] // N_CHIPS for r, ax in zip(rsums, axes)]
    shard = [tuple(dim // N_CHIPS if i == ax - 1 else dim for i, dim in enumerate(r.shape)) for r, ax in zip(rsums, axes)]

    def body(*refs):
        ins, outs = refs[:nw], refs[nw:2 * nw]
        send_sems, recv_sems, local_sems = refs[-3:]
        x, y, c, chips = _place()
        s_me = 2 * x + y
        win = lambda i, s: _window(ins[i], None, s, axes[i], blks[i])
        local = [pltpu.make_async_copy(win(i, s_me), outs[i].at[s_me], local_sems.at[i]) for i in range(nw)]
        for cp in local:
            cp.start()
        sends = [_remote(win(i, 2 * cx + cy), outs[i].at[s_me], send_sems, recv_sems, 3 * i + j, (cx, cy, c))
                 for i in range(nw) for j, (cx, cy) in enumerate(chips)]
        for cp in sends:
            cp.start()
        for i in range(nw):
            for j, (cx, cy) in enumerate(chips):
                slab = outs[i].at[2 * cx + cy]
                _remote(slab, slab, send_sems, recv_sems, 3 * i + j, (cx, cy, c)).wait_recv()
        for cp in sends:
            cp.wait_send()
        for cp in local:
            cp.wait()

    return pl.pallas_call(
        body, in_specs=[_ANY] * nw, out_specs=[_ANY] * nw,
        out_shape=[_sds((N_CHIPS,) + sh, r.dtype) for sh, r in zip(shard, rsums)],
        scratch_shapes=[pltpu.SemaphoreType.DMA((3 * nw,)), pltpu.SemaphoreType.DMA((3 * nw,)),
                        pltpu.SemaphoreType.DMA((nw,))],
        name="rs_exchange_chips")(*rsums)


def _rs_join(ts):
    nw = len(ts)

    def body(*refs):
        ins, outs = refs[:nw], refs[nw:2 * nw]
        send_sems, recv_sems, local_sems = refs[-3:]
        x, y, c, _ = _place()
        local = [pltpu.make_async_copy(ins[i], outs[i].at[c], local_sems.at[i]) for i in range(nw)]
        sends = [_remote(ins[i], outs[i].at[c], send_sems, recv_sems, i, (x, y, 1 - c)) for i in range(nw)]
        for cp in local + sends:
            cp.start()
        for i in range(nw):
            slab = outs[i].at[1 - c]
            _remote(slab, slab, send_sems, recv_sems, i, (x, y, 1 - c)).wait_recv()
        for cp in sends:
            cp.wait_send()
        for cp in local:
            cp.wait()

    return pl.pallas_call(
        body, in_specs=[_ANY] * nw, out_specs=[_ANY] * nw, out_shape=[_sds((2,) + t.shape, t.dtype) for t in ts],
        scratch_shapes=[pltpu.SemaphoreType.DMA((nw,)), pltpu.SemaphoreType.DMA((nw,)), pltpu.SemaphoreType.DMA((nw,))],
        name="rs_join_layers")(*ts)


def _all_reduce_small(buf):
    r, cols = buf.shape
    nd = 8

    def body(b_ref, o_ref, recv, send_sems, recv_sems):
        x, y, c, _ = _place()
        me = 4 * x + 2 * y + c
        recv[0] = b_ref[...]
        cps = []
        for rel in range(1, nd):
            dev = (1 - x if rel & 4 else x, 1 - y if rel & 2 else y, 1 - c if rel & 1 else c)
            cp = _remote(b_ref, recv.at[rel], send_sems, recv_sems, rel - 1, dev)
            cp.start()
            cps.append(cp)
        for rel in range(1, nd):
            _remote(b_ref, recv.at[rel], send_sems, recv_sems, rel - 1, (x, y, c)).wait_recv()
        acc = recv[me]
        for dv in range(1, nd):
            acc = acc + recv[lax.bitwise_xor(me, dv)]
        o_ref[...] = acc
        for cp in cps:
            cp.wait_send()

    vm = pl.BlockSpec(memory_space=pltpu.VMEM)
    return pl.pallas_call(
        body, in_specs=[vm], out_specs=vm, out_shape=_sds((r, cols), F32),
        scratch_shapes=[pltpu.VMEM((nd, r, cols), F32), pltpu.SemaphoreType.DMA((nd - 1,)),
                        pltpu.SemaphoreType.DMA((nd - 1,))],
        compiler_params=pltpu.CompilerParams(vmem_limit_bytes=VMEM_LIMIT), name="all_reduce_small")(buf)


def _pack_rows(parts, row_align):
    flat = jnp.concatenate([p.reshape(-1) for p in parts])
    per = PACK_COLS * row_align
    size = -(-flat.shape[0] // per) * per
    return jnp.pad(flat, (0, size - flat.shape[0])).reshape(-1, PACK_COLS)


def _unpack(flat, shapes):
    out, off = [], 0
    for shp in shapes:
        size = int(np.prod(shp))
        out.append(flat[off:off + size].reshape(shp))
        off += size
    return out


def kernel(x, ffn1_norm, ffn1_w_up, ffn1_w_down, mix_norm, w_in, b_gate, s5_lambda_re, s5_lambda_im, s5_log_dt, s5_b_re, s5_b_im, s5_c_re, s5_c_im, s5_d, s5_w_glu, w_br_s5, attn_q_gain, attn_k_gain, attn_rel_bias, w_br_attn, conv_w_dw, conv_b_dw, conv_ln_g, conv_ln_b, w_br_conv, w_out, ffn2_norm, ffn2_w_up, ffn2_w_down, loss_target, m_ffn1_norm, m_ffn1_w_up, m_ffn1_w_down, m_mix_norm, m_w_in, m_b_gate, m_s5_lambda_re, m_s5_lambda_im, m_s5_log_dt, m_s5_b_re, m_s5_b_im, m_s5_c_re, m_s5_c_im, m_s5_d, m_s5_w_glu, m_w_br_s5, m_attn_q_gain, m_attn_k_gain, m_attn_rel_bias, m_w_br_attn, m_conv_w_dw, m_conv_b_dw, m_conv_ln_g, m_conv_ln_b, m_w_br_conv, m_w_out, m_ffn2_norm, m_ffn2_w_up, m_ffn2_w_down, v_ffn1_norm, v_ffn1_w_up, v_ffn1_w_down, v_mix_norm, v_w_in, v_b_gate, v_s5_lambda_re, v_s5_lambda_im, v_s5_log_dt, v_s5_b_re, v_s5_b_im, v_s5_c_re, v_s5_c_im, v_s5_d, v_s5_w_glu, v_w_br_s5, v_attn_q_gain, v_attn_k_gain, v_attn_rel_bias, v_w_br_attn, v_conv_w_dw, v_conv_b_dw, v_conv_ln_g, v_conv_ln_b, v_w_br_conv, v_w_out, v_ffn2_norm, v_ffn2_w_up, v_ffn2_w_down):
    a = dict(locals())
    xi, yi, ci = lax.axis_index("x"), lax.axis_index("y"), lax.axis_index("c")
    s_me = 2 * xi + yi
    big_keys = list(_BIG_KEYS)
    axes = [1 if k == "w_in" else _BIG_KEYS[k] for k in big_keys]

    shards = [(jnp.swapaxes(a[k], 1, 2) if k == "w_in" else a[k]).astype(BF16) for k in big_keys]
    *fulls, taps = _all_gather_weights(shards, axes, a["conv_w_dw"])
    w = {k: a[k] for k in _WEIGHTS}
    w.update(zip(big_keys, fulls))
    w["conv_w_dw"] = jnp.moveaxis(taps, 0, 2).reshape(taps.shape[1], taps.shape[2], -1)

    loss_part, grad_x, gbig, gsmall = _local_step(a["x"], a["loss_target"], w)
    loss = lax.psum(loss_part, ("x", "y", "c"))

    c_idx = ci.astype(jnp.int32).reshape(1)
    glist = [gbig[k] for k in big_keys]
    recv = _rs_swap(glist)
    rsums = [_add_sibling(g, r, c_idx, name=f"rs_add_sibling_{k}") for g, r, k in zip(glist, recv, big_keys)]
    parts = _rs_exchange(rsums, axes)
    mine = [_add_chips(p, name=f"rs_add_chips_{k}") for p, k in zip(parts, big_keys)]
    gb = dict(zip(big_keys, _rs_join(mine)))
    gb["w_in"] = jnp.swapaxes(gb["w_in"], 1, 2)

    small_keys = list(_SMALL_KEYS)
    sred = _all_reduce_small(_pack_rows([gsmall[k] for k in small_keys], 8)).reshape(-1)
    gs = dict(zip(small_keys, _unpack(sred, [gsmall[k].shape for k in small_keys])))
    blk = a["conv_w_dw"].shape[2]
    gs["conv_w_dw"] = lax.dynamic_slice_in_dim(gs["conv_w_dw"], s_me * blk, blk, axis=2)
    grads = {**gb, **gs}

    delta, new_m, new_v = {}, {}, {}
    for k in big_keys:
        shp = a[k].shape
        two_d = lambda t: t.reshape(-1, shp[-1])
        d_, m_, v_ = _adamw(two_d(a[k]), two_d(grads[k]), two_d(a["m_" + k]), two_d(a["v_" + k]), name=f"adamw_{k}")
        delta[k], new_m[k], new_v[k] = d_.reshape(shp), m_.reshape(shp), v_.reshape(shp)
    sm_shapes = [a[k].shape for k in small_keys]
    packs = [_pack_rows([src[k] for k in small_keys], 8)
             for src in (a, grads, {k: a["m_" + k] for k in small_keys}, {k: a["v_" + k] for k in small_keys})]
    d_, m_, v_ = _adamw(*packs, name="adamw_small")
    for dst, res in ((delta, d_), (new_m, m_), (new_v, v_)):
        dst.update(zip(small_keys, _unpack(res.reshape(-1), sm_shapes)))

    return (loss, grad_x, *[grads[k] for k in _WEIGHTS], *[delta[k] for k in _WEIGHTS],
            *[new_m[k] for k in _WEIGHTS], *[new_v[k] for k in _WEIGHTS])
```

```python
import functools
import math

import numpy as np
import jax
import jax.numpy as jnp
from jax import lax
from jax.experimental import pallas as pl
from jax.experimental.pallas import tpu as pltpu

F32 = jnp.float32
BF16 = jnp.bfloat16
EPS = 1e-6
VMEM_LIMIT = 56 * 1024 * 1024
LANES = 128
HEAD_DIM = 64
CHUNK = 64
N_LEFT = 8
MAX_REL = 128
ATT_TQ = 256
CONV_W = 31
HALO = 32
NEG = -1e30
N_CHIPS = 4
PACK_COLS = 1024

ADAM_LR = 0.001
ADAM_B1 = 0.9
ADAM_B2 = 0.999
ADAM_EPS = 1e-08
ADAM_WD = 0.01
ADAM_STEP = 10

MESH = pl.DeviceIdType.MESH
_ANY = pl.BlockSpec(memory_space=pl.ANY)


def _cp(*sem):
    return pltpu.CompilerParams(dimension_semantics=sem, vmem_limit_bytes=VMEM_LIMIT)


def _sds(shape, dtype):
    return jax.ShapeDtypeStruct(shape, dtype)


def _tile(n, pref):
    t = min(n, pref)
    while n % t:
        t -= 8
    return t


def _sigmoid(x):
    return jax.nn.sigmoid(x)


_GELU_C = math.sqrt(2.0 / math.pi)


def _gelu(y):
    return 0.5 * y * (1.0 + jnp.tanh(_GELU_C * (y + 0.044715 * y * y * y)))


def _gelu_grad(y):
    th = jnp.tanh(_GELU_C * (y + 0.044715 * y * y * y))
    return 0.5 * (1.0 + th) + 0.5 * y * (1.0 - th * th) * _GELU_C * (1.0 + 3.0 * 0.044715 * y * y)


def _dot(a, b):
    return jnp.dot(a, b, preferred_element_type=F32)


def _dot_t0(a, b):
    return lax.dot_general(a, b, (((0,), (0,)), ((), ())), preferred_element_type=F32)


def _dot_t1(a, b):
    return lax.dot_general(a, b, (((1,), (1,)), ((), ())), preferred_element_type=F32)


def _slab_out(nl, l, shape, buf, n_in):
    sds = _sds((nl,) + tuple(shape), F32)
    if buf is None:
        return [], [], sds, {}
    return [buf], [_ANY], sds, {n_in: 0}


def _norm_mm(x, g, w, l, *, tm, tn, ntiles, pieces, transposed, name):
    n, d = x.shape
    m = ntiles * tn
    mp = m // pieces
    npj = mp // tn

    def body(x_ref, g_ref, w_ref, h_ref, y_ref, h_scr):
        @pl.when(pl.program_id(1) == 0)
        def _():
            xv = x_ref[...]
            r = lax.rsqrt(jnp.mean(xv * xv, axis=-1, keepdims=True) + EPS)
            hb = (xv * r * g_ref[...]).astype(BF16)
            h_scr[...] = hb
            h_ref[...] = hb

        mm = _dot_t1 if transposed else _dot
        y_ref[...] = mm(h_scr[...], w_ref[...]).astype(BF16)

    wspec = (pl.BlockSpec((None, tn, d), lambda i, j: (l, j, 0)) if transposed
             else pl.BlockSpec((None, d, tn), lambda i, j: (l, 0, j)))
    return pl.pallas_call(
        body, grid=(n // tm, ntiles),
        in_specs=[pl.BlockSpec((tm, d), lambda i, j: (i, 0)), pl.BlockSpec((1, d), lambda i, j: (0, 0)), wspec],
        out_specs=[pl.BlockSpec((tm, d), lambda i, j: (i, 0)),
                   pl.BlockSpec((None, tm, tn), lambda i, j: (j // npj, i, j % npj))],
        out_shape=[_sds((n, d), BF16), _sds((pieces, n, mp), BF16)],
        scratch_shapes=[pltpu.VMEM((tm, d), BF16)],
        compiler_params=_cp("parallel", "arbitrary"), name=name)(x, g, w)


def _mm_t(a, w, l, *, tm, tn, off, ntiles, name):
    n, k = a.shape

    def body(a_ref, w_ref, y_ref):
        y_ref[...] = _dot_t1(a_ref[...], w_ref[...]).astype(BF16)

    return pl.pallas_call(
        body, grid=(n // tm, ntiles),
        in_specs=[pl.BlockSpec((tm, k), lambda i, j: (i, 0)), pl.BlockSpec((None, tn, k), lambda i, j: (l, off + j, 0))],
        out_specs=pl.BlockSpec((tm, tn), lambda i, j: (i, j)),
        out_shape=_sds((n, ntiles * tn), BF16),
        compiler_params=_cp("parallel", "arbitrary"), name=name)(a, w)


def _ffn_down(ab, wd, l, x, *, tm, tk, name):
    _, n, dff = ab.shape
    d = x.shape[1]
    nk = dff // tk

    def body(a_ref, b_ref, wd_ref, x_ref, o_ref, acc):
        k = pl.program_id(1)
        a = a_ref[...].astype(F32)
        b = b_ref[...].astype(F32)
        act = (a * _sigmoid(a) * b).astype(BF16)
        part = _dot(act, wd_ref[...])

        @pl.when(k == 0)
        def _():
            acc[...] = part

        @pl.when(k > 0)
        def _():
            acc[...] += part

        @pl.when(k == nk - 1)
        def _():
            o_ref[...] = x_ref[...] + 0.5 * acc[...]

    return pl.pallas_call(
        body, grid=(n // tm, nk),
        in_specs=[pl.BlockSpec((None, tm, tk), lambda i, k: (0, i, k)),
                  pl.BlockSpec((None, tm, tk), lambda i, k: (1, i, k)),
                  pl.BlockSpec((None, tk, d), lambda i, k: (l, k, 0)),
                  pl.BlockSpec((tm, d), lambda i, k: (i, 0))],
        out_specs=pl.BlockSpec((tm, d), lambda i, k: (i, 0)),
        out_shape=_sds((n, d), F32),
        scratch_shapes=[pltpu.VMEM((tm, d), F32)],
        compiler_params=_cp("parallel", "arbitrary"), name=name)(ab, ab, wd, x)


def _ffn_dact(dx, wd, l, ab, *, tm, tk, name):
    n, d = dx.shape
    dff = ab.shape[2]

    def body(dx_ref, wd_ref, a_ref, b_ref, dab_ref, act_ref, do_ref):
        do = (0.5 * dx_ref[...]).astype(BF16)

        @pl.when(pl.program_id(1) == 0)
        def _():
            do_ref[...] = do

        dact = _dot_t1(do, wd_ref[...])
        a = a_ref[...].astype(F32)
        b = b_ref[...].astype(F32)
        sg = _sigmoid(a)
        silu = a * sg
        act_ref[...] = (silu * b).astype(BF16)
        dab_ref[0] = (dact * b * (sg * (1.0 + a * (1.0 - sg)))).astype(BF16)
        dab_ref[1] = (dact * silu).astype(BF16)

    return pl.pallas_call(
        body, grid=(n // tm, dff // tk),
        in_specs=[pl.BlockSpec((tm, d), lambda i, j: (i, 0)),
                  pl.BlockSpec((None, tk, d), lambda i, j: (l, j, 0)),
                  pl.BlockSpec((None, tm, tk), lambda i, j: (0, i, j)),
                  pl.BlockSpec((None, tm, tk), lambda i, j: (1, i, j))],
        out_specs=[pl.BlockSpec((2, tm, tk), lambda i, j: (0, i, j)),
                   pl.BlockSpec((tm, tk), lambda i, j: (i, j)),
                   pl.BlockSpec((tm, d), lambda i, j: (i, 0))],
        out_shape=[_sds((2, n, dff), BF16), _sds((n, dff), BF16), _sds((n, d), BF16)],
        compiler_params=_cp("parallel", "arbitrary"), name=name)(dx, wd, ab, ab)


def _rms_bwd_epilogue(acc, x_ref, g_ref, dres_ref, dx_ref, dg_ref, i):
    dh = acc[...]
    xv = x_ref[...]
    r = lax.rsqrt(jnp.mean(xv * xv, axis=-1, keepdims=True) + EPS)
    xn = xv * r
    dgp = jnp.sum(dh * xn, axis=0, keepdims=True)
    dxh = dh * g_ref[...]
    dx_ref[...] = dres_ref[...] + r * (dxh - xn * jnp.mean(dxh * xn, axis=-1, keepdims=True))

    @pl.when(i == 0)
    def _():
        dg_ref[...] = dgp

    @pl.when(i > 0)
    def _():
        dg_ref[...] += dgp


def _ffn_dx(dab, wu, l, x, g, dres, *, tm, tk, name):
    p, n, mp = dab.shape
    d = x.shape[1]
    nkp = mp // tk
    nk = p * nkp

    def body(dy_ref, w_ref, x_ref, g_ref, dres_ref, dx_ref, dg_ref, acc):
        k = pl.program_id(1)
        part = _dot_t1(dy_ref[...], w_ref[...])

        @pl.when(k == 0)
        def _():
            acc[...] = part

        @pl.when(k > 0)
        def _():
            acc[...] += part

        @pl.when(k == nk - 1)
        def _():
            _rms_bwd_epilogue(acc, x_ref, g_ref, dres_ref, dx_ref, dg_ref, pl.program_id(0))

    return pl.pallas_call(
        body, grid=(n // tm, nk),
        in_specs=[pl.BlockSpec((None, tm, tk), lambda i, k: (k // nkp, i, k % nkp)),
                  pl.BlockSpec((None, d, tk), lambda i, k: (l, 0, k)),
                  pl.BlockSpec((tm, d), lambda i, k: (i, 0)),
                  pl.BlockSpec((1, d), lambda i, k: (0, 0)),
                  pl.BlockSpec((tm, d), lambda i, k: (i, 0))],
        out_specs=[pl.BlockSpec((tm, d), lambda i, k: (i, 0)), pl.BlockSpec((1, d), lambda i, k: (0, 0))],
        out_shape=[_sds((n, d), F32), _sds((1, d), F32)],
        scratch_shapes=[pltpu.VMEM((tm, d), F32)],
        compiler_params=_cp("arbitrary", "arbitrary"), name=name)(dab, wu, x, g, dres)


def _mix_dx(dpa, dgl, wt, l, x, g, dres, *, tm, tk, name):
    n, d = x.shape
    n1 = dpa.shape[1] // tk
    n2 = dgl.shape[1] // tk
    nk = n1 + n2

    def body(d1_ref, d2_ref, w_ref, x_ref, g_ref, dres_ref, dx_ref, dg_ref, acc):
        k = pl.program_id(1)

        @pl.when(k == 0)
        def _():
            acc[...] = _dot(d1_ref[...], w_ref[...])

        @pl.when((k > 0) & (k < n1))
        def _():
            acc[...] += _dot(d1_ref[...], w_ref[...])

        @pl.when(k >= n1)
        def _():
            acc[...] += _dot(d2_ref[...], w_ref[...])

        @pl.when(k == nk - 1)
        def _():
            _rms_bwd_epilogue(acc, x_ref, g_ref, dres_ref, dx_ref, dg_ref, pl.program_id(0))

    return pl.pallas_call(
        body, grid=(n // tm, nk),
        in_specs=[pl.BlockSpec((tm, tk), lambda i, k: (i, jnp.minimum(k, n1 - 1))),
                  pl.BlockSpec((tm, tk), lambda i, k: (i, jnp.maximum(k - n1, 0))),
                  pl.BlockSpec((None, tk, d), lambda i, k: (l, k, 0)),
                  pl.BlockSpec((tm, d), lambda i, k: (i, 0)),
                  pl.BlockSpec((1, d), lambda i, k: (0, 0)),
                  pl.BlockSpec((tm, d), lambda i, k: (i, 0))],
        out_specs=[pl.BlockSpec((tm, d), lambda i, k: (i, 0)), pl.BlockSpec((1, d), lambda i, k: (0, 0))],
        out_shape=[_sds((n, d), F32), _sds((1, d), F32)],
        scratch_shapes=[pltpu.VMEM((tm, d), F32)],
        compiler_params=_cp("arbitrary", "arbitrary"), name=name)(dpa, dgl, wt, x, g, dres)


def _mm_tn(a, b, l, nl, buf, *, ta, tb, tk, name):
    pa, n, ka = a.shape
    pb, _, kb = b.shape
    nap = ka // ta
    nbp = kb // tb

    def body(a_ref, b_ref, *rest):
        o_ref = rest[-1]

        @pl.when(pl.program_id(2) == 0)
        def _():
            o_ref[...] = jnp.zeros_like(o_ref)

        o_ref[...] += _dot_t0(a_ref[...], b_ref[...])

    extra, extra_specs, out_shape, aliases = _slab_out(nl, l, (pa * ka, pb * kb), buf, 2)
    return pl.pallas_call(
        body, grid=(pa * nap, pb * nbp, n // tk),
        in_specs=[pl.BlockSpec((None, tk, ta), lambda i, j, k: (i // nap, k, i % nap)),
                  pl.BlockSpec((None, tk, tb), lambda i, j, k: (j // nbp, k, j % nbp)), *extra_specs],
        out_specs=pl.BlockSpec((None, ta, tb), lambda i, j, k: (l, i, j)),
        out_shape=out_shape, input_output_aliases=aliases,
        compiler_params=_cp("parallel", "parallel", "arbitrary"), name=name)(a, b, *extra)


def _dwin_t(dpa, dgl, h, l, nl, buf, *, ta, tk, name):
    n, d = h.shape
    n1 = dpa.shape[1] // ta
    n2 = dgl.shape[1] // ta

    def body(a1_ref, a2_ref, h_ref, *rest):
        o_ref = rest[-1]
        i = pl.program_id(0)

        @pl.when(pl.program_id(1) == 0)
        def _():
            o_ref[...] = jnp.zeros_like(o_ref)

        @pl.when(i < n1)
        def _():
            o_ref[...] += _dot_t0(a1_ref[...], h_ref[...])

        @pl.when(i >= n1)
        def _():
            o_ref[...] += _dot_t0(a2_ref[...], h_ref[...])

    extra, extra_specs, out_shape, aliases = _slab_out(nl, l, ((n1 + n2) * ta, d), buf, 3)
    return pl.pallas_call(
        body, grid=(n1 + n2, n // tk),
        in_specs=[pl.BlockSpec((tk, ta), lambda i, k: (jnp.where(i < n1, k, 0), jnp.minimum(i, n1 - 1))),
                  pl.BlockSpec((tk, ta), lambda i, k: (jnp.where(i >= n1, k, 0), jnp.maximum(i - n1, 0))),
                  pl.BlockSpec((tk, d), lambda i, k: (k, 0)), *extra_specs],
        out_specs=pl.BlockSpec((None, ta, d), lambda i, k: (l, i, 0)),
        out_shape=out_shape, input_output_aliases=aliases,
        compiler_params=_cp("parallel", "arbitrary"), name=name)(dpa, dgl, h, *extra)


def _loss_grad(y, t, *, tm, name):
    n, d = y.shape

    def body(y_ref, t_ref, dy_ref, l_ref):
        e = y_ref[...] - t_ref[...]
        dy_ref[...] = e * (1.0 / d)
        part = jnp.sum(e * e, axis=0, keepdims=True)

        @pl.when(pl.program_id(0) == 0)
        def _():
            l_ref[...] = part

        @pl.when(pl.program_id(0) > 0)
        def _():
            l_ref[...] += part

    return pl.pallas_call(
        body, grid=(n // tm,),
        in_specs=[pl.BlockSpec((tm, d), lambda i: (i, 0)), pl.BlockSpec((tm, d), lambda i: (i, 0))],
        out_specs=[pl.BlockSpec((tm, d), lambda i: (i, 0)), pl.BlockSpec((1, d), lambda i: (0, 0))],
        out_shape=[_sds((n, d), F32), _sds((1, d), F32)],
        compiler_params=_cp("arbitrary"), name=name)(y, t)


def _s5_fwd(proj, sp, wglu, l, *, bl, s, t, name):
    n = bl * s
    ds5, gp = sp["bblk_r"].shape
    nt = s // t
    nlog = int(math.log2(t))

    def body(u_ref, br_ref, bi_ref, a_ref, pw_ref, cr_ref, ci_ref, d_ref, wg_ref,
             xr_ref, xi_ref, yp_ref, zg_ref, o_ref, carry):
        @pl.when(pl.program_id(1) == 0)
        def _():
            carry[...] = jnp.zeros_like(carry)

        u = u_ref[...]
        rows = lax.broadcasted_iota(jnp.int32, (t, gp), 0)
        ar = a_ref[0:1, :]
        ai = a_ref[1:2, :]
        cr = carry[0:1, :]
        ci = carry[1:2, :]
        first = rows == 0
        xr = _dot(u, br_ref[...]) + jnp.where(first, ar * cr - ai * ci, 0.0)
        xi = _dot(u, bi_ref[...]) + jnp.where(first, ar * ci + ai * cr, 0.0)
        for k in range(nlog):
            sh = 1 << k
            pr = pw_ref[2 * k:2 * k + 1, :]
            pi = pw_ref[2 * k + 1:2 * k + 2, :]
            keep = rows >= sh
            sr = jnp.where(keep, pltpu.roll(xr, sh, 0), 0.0)
            si = jnp.where(keep, pltpu.roll(xi, sh, 0), 0.0)
            xr, xi = xr + pr * sr - pi * si, xi + pr * si + pi * sr
        last = rows == t - 1
        carry[0:1, :] = jnp.sum(jnp.where(last, xr, 0.0), axis=0, keepdims=True)
        carry[1:2, :] = jnp.sum(jnp.where(last, xi, 0.0), axis=0, keepdims=True)
        xr_ref[...] = xr
        xi_ref[...] = xi
        y = _dot(xr.astype(BF16), cr_ref[...]) + _dot(xi.astype(BF16), ci_ref[...]) + d_ref[...] * u.astype(F32)
        yp_ref[...] = y
        zg = _dot(_gelu(y).astype(BF16), wg_ref[...])
        zg_ref[...] = zg
        o_ref[...] = (zg[:, :ds5] * _sigmoid(zg[:, ds5:])).astype(BF16)

    const = lambda shape: pl.BlockSpec(shape, lambda b, i: (0, 0))
    row = lambda w: pl.BlockSpec((t, w), lambda b, i: (b * nt + i, 0))
    return pl.pallas_call(
        body, grid=(bl, nt),
        in_specs=[row(ds5), const((ds5, gp)), const((ds5, gp)), const((2, gp)), const((2 * nlog, gp)),
                  const((gp, ds5)), const((gp, ds5)), const((1, ds5)),
                  pl.BlockSpec((None, ds5, 2 * ds5), lambda b, i: (l, 0, 0))],
        out_specs=[row(gp), row(gp), row(ds5), row(2 * ds5), row(ds5)],
        out_shape=[_sds((n, gp), F32), _sds((n, gp), F32), _sds((n, ds5), F32), _sds((n, 2 * ds5), F32),
                   _sds((n, ds5), BF16)],
        scratch_shapes=[pltpu.VMEM((2, gp), F32)],
        compiler_params=_cp("arbitrary", "arbitrary"), name=name)(
            proj, sp["bblk_r"], sp["bblk_i"], sp["a"], sp["pw"], sp["cblk_r"], sp["cblk_in"], sp["d"], wglu)


def _s5_bwd(ds, yp, zg, xr, xi, proj, sp, wglu, l, nl, dwg_buf, *, bl, s, t, name):
    n = bl * s
    ds5, gp = sp["bblk_r"].shape
    nt = s // t
    nlog = int(math.log2(t))
    tb = t // 8

    def body(ds_ref, yp_ref, zg_ref, xr_ref, xi_ref, hr_ref, hi_ref, u_ref, wg_ref, cr_ref, ci_ref,
             br_ref, bi_ref, a_ref, pw_ref, d_ref, *rest):
        du_ref, dd_ref, dcr_ref, dci_ref, dbr_ref, dbi_ref, da_ref, dwg_ref, carry = rest[-9:]
        b = pl.program_id(0)
        i = pl.program_id(1)
        tile = nt - 1 - i

        @pl.when((b == 0) & (i == 0))
        def _():
            for r in (dwg_ref, dd_ref, dcr_ref, dci_ref, dbr_ref, dbi_ref, da_ref):
                r[...] = jnp.zeros_like(r)

        @pl.when(i == 0)
        def _():
            carry[...] = jnp.zeros_like(carry)

        dsv = ds_ref[...].astype(F32)
        zgv = zg_ref[...]
        za = zgv[:, :ds5]
        sg = _sigmoid(zgv[:, ds5:])
        dzg = jnp.concatenate([dsv * sg, dsv * za * sg * (1.0 - sg)], axis=1).astype(BF16)
        y = yp_ref[...]
        dwg_ref[...] += _dot_t0(_gelu(y).astype(BF16), dzg)
        dy = _dot_t1(dzg, wg_ref[...]) * _gelu_grad(y)
        ub = u_ref[...]
        uf = ub.astype(F32)
        dd_ref[...] += jnp.sum(dy * uf, axis=0, keepdims=True)
        dyb = dy.astype(BF16)
        xrv = xr_ref[...]
        xiv = xi_ref[...]
        dcr_ref[...] += _dot_t0(xrv.astype(BF16), dyb)
        dci_ref[...] += _dot_t0(xiv.astype(BF16), dyb)

        rows = lax.broadcasted_iota(jnp.int32, (t, gp), 0)
        ar = a_ref[0:1, :]
        ai = a_ref[1:2, :]
        cr = carry[0:1, :]
        ci = carry[1:2, :]
        last = rows == t - 1
        gr = _dot_t1(dyb, cr_ref[...]) + jnp.where(last, ar * cr + ai * ci, 0.0)
        gi = _dot_t1(dyb, ci_ref[...]) + jnp.where(last, ar * ci - ai * cr, 0.0)
        for k in range(nlog):
            sh = 1 << k
            pr = pw_ref[2 * k:2 * k + 1, :]
            pi = pw_ref[2 * k + 1:2 * k + 2, :]
            keep = rows < t - sh
            sr = jnp.where(keep, pltpu.roll(gr, t - sh, 0), 0.0)
            si = jnp.where(keep, pltpu.roll(gi, t - sh, 0), 0.0)
            gr, gi = gr + pr * sr + pi * si, gi + pr * si - pi * sr
        first = rows == 0
        carry[0:1, :] = jnp.sum(jnp.where(first, gr, 0.0), axis=0, keepdims=True)
        carry[1:2, :] = jnp.sum(jnp.where(first, gi, 0.0), axis=0, keepdims=True)

        live = jnp.where(tile > 0, 1.0, 0.0)
        xpr = jnp.where(first, hr_ref[7:8, :] * live, pltpu.roll(xrv, 1, 0))
        xpi = jnp.where(first, hi_ref[7:8, :] * live, pltpu.roll(xiv, 1, 0))
        da_ref[0:1, :] += jnp.sum(gr * xpr + gi * xpi, axis=0, keepdims=True)
        da_ref[1:2, :] += jnp.sum(gi * xpr - gr * xpi, axis=0, keepdims=True)

        grb = gr.astype(BF16)
        gib = gi.astype(BF16)
        dbr_ref[...] += _dot_t0(ub, grb)
        dbi_ref[...] += _dot_t0(ub, gib)
        du_ref[...] = (_dot_t1(grb, br_ref[...]) + _dot_t1(gib, bi_ref[...]) + dy * d_ref[...]).astype(BF16)

    const = lambda shape: pl.BlockSpec(shape, lambda b, i: (0, 0))
    row = lambda w: pl.BlockSpec((t, w), lambda b, i: (b * nt + nt - 1 - i, 0))
    halo = pl.BlockSpec((8, gp), lambda b, i: (jnp.maximum((b * nt + nt - 1 - i) * tb - 1, 0), 0))
    extra, extra_specs, dwg_shape, aliases = _slab_out(nl, l, (ds5, 2 * ds5), dwg_buf, 16)
    aliases = {k: 7 for k in aliases}
    return pl.pallas_call(
        body, grid=(bl, nt),
        in_specs=[row(ds5), row(ds5), row(2 * ds5), row(gp), row(gp), halo, halo, row(ds5),
                  pl.BlockSpec((None, ds5, 2 * ds5), lambda b, i: (l, 0, 0)),
                  const((gp, ds5)), const((gp, ds5)), const((ds5, gp)), const((ds5, gp)),
                  const((2, gp)), const((2 * nlog, gp)), const((1, ds5)), *extra_specs],
        out_specs=[row(ds5), const((1, ds5)), const((gp, ds5)), const((gp, ds5)),
                   const((ds5, gp)), const((ds5, gp)), const((2, gp)),
                   pl.BlockSpec((None, ds5, 2 * ds5), lambda b, i: (l, 0, 0))],
        out_shape=[_sds((n, ds5), BF16), _sds((1, ds5), F32), _sds((gp, ds5), F32),
                   _sds((gp, ds5), F32), _sds((ds5, gp), F32), _sds((ds5, gp), F32), _sds((2, gp), F32), dwg_shape],
        input_output_aliases=aliases,
        scratch_shapes=[pltpu.VMEM((2, gp), F32)],
        compiler_params=_cp("arbitrary", "arbitrary"), name=name)(
            ds, yp, zg, xr, xi, xr, xi, proj, wglu, sp["cblk_r"], sp["cblk_in"],
            sp["bblk_r"], sp["bblk_i"], sp["a"], sp["pw"], sp["d"], *extra)


def _head_norm(x, first):
    x2 = x * x
    sa = jnp.sum(jnp.where(first, x2, 0.0), axis=-1, keepdims=True)
    sb = jnp.sum(jnp.where(first, 0.0, x2), axis=-1, keepdims=True)
    r = jnp.where(first, lax.rsqrt(sa * (1.0 / HEAD_DIM) + EPS), lax.rsqrt(sb * (1.0 / HEAD_DIM) + EPS))
    return x * r, r


def _attn_specs(bl, s, datt, qoff):
    nq = s // ATT_TQ
    nb = datt // LANES
    col = lambda blk: (lambda b, h, q: (b * nq + q, qoff + blk * nb + h))
    win = lambda blk, j: (lambda b, h, q: (b * nq + jnp.maximum(q - 2 + j, 0), qoff + blk * nb + h))
    tile = lambda f: pl.BlockSpec((ATT_TQ, LANES), f)
    qs = tile(col(0))
    ks = [tile(win(1, j)) for j in range(3)]
    vs = [tile(win(2, j)) for j in range(3)]
    return nq, nb, qs, ks, vs


def _attn_probs(q_ref, k_refs, gq_ref, gk_ref, bias_ref):
    qt = pl.program_id(2)
    lane = lax.broadcasted_iota(jnp.int32, (1, LANES), 1)
    first = lane < HEAD_DIM
    qh, rq = _head_norm(q_ref[...].astype(F32), first)
    qn = qh * gq_ref[...]
    kc = jnp.concatenate([r[...] for r in k_refs], axis=0).astype(F32)
    kh, _ = _head_norm(kc, first)
    kn = (kh * gk_ref[...]).astype(BF16)
    kpos = (qt - 2) * ATT_TQ + lax.broadcasted_iota(jnp.int32, (1, 3 * ATT_TQ), 1)
    valid = kpos >= 0
    scale = HEAD_DIM ** -0.5
    masks = (first, jnp.logical_not(first))
    qas, ps = [], []
    for hh in range(2):
        qa = jnp.where(masks[hh], qn, 0.0).astype(BF16)
        sc = _dot_t1(qa, kn) * scale + bias_ref[hh]
        sc = jnp.where(valid, sc, NEG)
        e = jnp.exp(sc - jnp.max(sc, axis=-1, keepdims=True))
        ps.append(e / jnp.sum(e, axis=-1, keepdims=True))
        qas.append(qa)
    return first, masks, qh, rq, kn, qas, ps


def _attn_fwd(proj, gq2, gk2, bias, *, bl, s, datt, qoff, name):
    n = bl * s
    nq, nb, qs, ks, vs = _attn_specs(bl, s, datt, qoff)

    def body(q_ref, k0, k1, k2, v0, v1, v2, gq_ref, gk_ref, bias_ref, o_ref):
        first, _, _, _, _, _, ps = _attn_probs(q_ref, (k0, k1, k2), gq_ref, gk_ref, bias_ref)
        vc = jnp.concatenate([v0[...], v1[...], v2[...]], axis=0)
        o0 = _dot(ps[0].astype(BF16), vc)
        o1 = _dot(ps[1].astype(BF16), vc)
        o_ref[...] = jnp.where(first, o0, o1).astype(BF16)

    gs = pl.BlockSpec((1, LANES), lambda b, h, q: (0, 0))
    return pl.pallas_call(
        body, grid=(bl, nb, nq),
        in_specs=[qs, *ks, *vs, gs, gs, pl.BlockSpec((2, ATT_TQ, 3 * ATT_TQ), lambda b, h, q: (h, 0, 0))],
        out_specs=pl.BlockSpec((ATT_TQ, LANES), lambda b, h, q: (b * nq + q, h)),
        out_shape=_sds((n, datt), BF16),
        compiler_params=_cp("parallel", "parallel", "arbitrary"), name=name)(
            proj, proj, proj, proj, proj, proj, proj, gq2, gk2, bias)


def _attn_bwd(do, proj, gq2, gk2, bias, *, bl, s, datt, qoff, name):
    n = bl * s
    nq, nb, qs, ks, vs = _attn_specs(bl, s, datt, qoff)
    srows = s + 2 * ATT_TQ
    scale = HEAD_DIM ** -0.5

    def body(do_ref, q_ref, k0, k1, k2, v0, v1, v2, gq_ref, gk_ref, bias_ref,
             dq_ref, dk_ref, dv_ref, db_ref, dgq_ref):
        qt = pl.program_id(2)

        @pl.when(qt == 0)
        def _():
            dk_ref[...] = jnp.zeros_like(dk_ref)
            dv_ref[...] = jnp.zeros_like(dv_ref)
            db_ref[...] = jnp.zeros_like(db_ref)
            dgq_ref[...] = jnp.zeros_like(dgq_ref)

        first, masks, qh, rq, kn, qas, ps = _attn_probs(q_ref, (k0, k1, k2), gq_ref, gk_ref, bias_ref)
        vc = jnp.concatenate([v0[...], v1[...], v2[...]], axis=0)
        dov = do_ref[...]
        dqn = jnp.zeros((ATT_TQ, LANES), F32)
        dkn = jnp.zeros((3 * ATT_TQ, LANES), F32)
        dv = jnp.zeros((3 * ATT_TQ, LANES), F32)
        for hh in range(2):
            doa = jnp.where(masks[hh], dov, jnp.zeros_like(dov))
            p = ps[hh]
            dp = _dot_t1(doa, vc)
            dsm = p * (dp - jnp.sum(dp * p, axis=-1, keepdims=True))
            db_ref[hh] += dsm
            dsc = (dsm * scale).astype(BF16)
            dqn = dqn + _dot(dsc, jnp.where(masks[hh], kn, jnp.zeros_like(kn)))
            dkn = dkn + _dot_t0(dsc, qas[hh])
            dv = dv + _dot_t0(p.astype(BF16), doa)
        start = pl.multiple_of(qt * ATT_TQ, ATT_TQ)
        dk_ref[pl.ds(start, 3 * ATT_TQ), :] += dkn
        dv_ref[pl.ds(start, 3 * ATT_TQ), :] += dv
        dgq_ref[...] += jnp.sum(dqn * qh, axis=0, keepdims=True)
        dqh = dqn * gq_ref[...]
        t = dqh * qh
        ma = jnp.sum(jnp.where(first, t, 0.0), axis=-1, keepdims=True) * (1.0 / HEAD_DIM)
        mb = jnp.sum(jnp.where(first, 0.0, t), axis=-1, keepdims=True) * (1.0 / HEAD_DIM)
        dq_ref[...] = (rq * (dqh - qh * jnp.where(first, ma, mb))).astype(BF16)

    gs = pl.BlockSpec((1, LANES), lambda b, h, q: (0, 0))
    acc = pl.BlockSpec((None, srows, LANES), lambda b, h, q: (b, 0, h))
    return pl.pallas_call(
        body, grid=(bl, nb, nq),
        in_specs=[pl.BlockSpec((ATT_TQ, LANES), lambda b, h, q: (b * nq + q, h)), qs, *ks, *vs, gs, gs,
                  pl.BlockSpec((2, ATT_TQ, 3 * ATT_TQ), lambda b, h, q: (h, 0, 0))],
        out_specs=[pl.BlockSpec((ATT_TQ, LANES), lambda b, h, q: (b * nq + q, h)), acc, acc,
                   pl.BlockSpec((None, 2, ATT_TQ, 3 * ATT_TQ), lambda b, h, q: (b, h, 0, 0)),
                   pl.BlockSpec((None, None, 1, LANES), lambda b, h, q: (b, h, 0, 0))],
        out_shape=[_sds((n, datt), BF16), _sds((bl, srows, datt), F32), _sds((bl, srows, datt), F32),
                   _sds((bl, 2 * nb, ATT_TQ, 3 * ATT_TQ), F32), _sds((bl, nb, 1, LANES), F32)],
        compiler_params=_cp("arbitrary", "arbitrary", "arbitrary"), name=name)(
            do, proj, proj, proj, proj, proj, proj, proj, gq2, gk2, bias)


def _attn_kv_bwd(dkn, dv, proj, gk2, *, bl, s, datt, tm, koff, name):
    n = bl * s
    ns = s // tm
    off = 2 * ATT_TQ // tm
    nb = datt // LANES

    def body(dkn_ref, dv_ref, k_ref, gk_ref, dk_ref, dvo_ref, dgk_ref):
        lane = lax.broadcasted_iota(jnp.int32, (1, LANES), 1)
        first = lane < HEAD_DIM

        @pl.when((pl.program_id(0) == 0) & (pl.program_id(1) == 0) & (pl.program_id(2) == 0))
        def _():
            dgk_ref[...] = jnp.zeros_like(dgk_ref)

        dvo_ref[...] = dv_ref[...].astype(BF16)
        kh, rk = _head_norm(k_ref[...].astype(F32), first)
        dn = dkn_ref[...]
        dgk_ref[...] += jnp.sum(dn * kh, axis=0, keepdims=True)
        dh = dn * gk_ref[...]
        t = dh * kh
        ma = jnp.sum(jnp.where(first, t, 0.0), axis=-1, keepdims=True) * (1.0 / HEAD_DIM)
        mb = jnp.sum(jnp.where(first, 0.0, t), axis=-1, keepdims=True) * (1.0 / HEAD_DIM)
        dk_ref[...] = (rk * (dh - kh * jnp.where(first, ma, mb))).astype(BF16)

    accs = pl.BlockSpec((None, tm, LANES), lambda b, i, c: (b, i + off, c))
    outs = pl.BlockSpec((tm, LANES), lambda b, i, c: (b * ns + i, c))
    vec = pl.BlockSpec((1, LANES), lambda b, i, c: (0, 0))
    return pl.pallas_call(
        body, grid=(bl, ns, nb),
        in_specs=[accs, accs, pl.BlockSpec((tm, LANES), lambda b, i, c: (b * ns + i, koff + c)), vec],
        out_specs=[outs, outs, vec],
        out_shape=[_sds((n, datt), BF16), _sds((n, datt), BF16), _sds((1, LANES), F32)],
        compiler_params=_cp("arbitrary", "arbitrary", "arbitrary"), name=name)(dkn, dv, proj, gk2)


def _conv_fwd(proj, wdw, bdw, lng, lnb, *, bl, s, t, acol, name):
    n = bl * s
    dc = wdw.shape[1]
    nt = s // t
    hb = t // HALO

    def body(za_ref, zg_ref, ha_ref, hgt_ref, w_ref, b_ref, g_ref, be_ref, hg_ref, hc_ref, o_ref, ext):
        i = pl.program_id(1)
        hg = za_ref[...].astype(F32) * _sigmoid(zg_ref[...].astype(F32))
        live = jnp.where(i > 0, 1.0, 0.0)
        ext[0:HALO, :] = ha_ref[...].astype(F32) * _sigmoid(hgt_ref[...].astype(F32)) * live
        ext[HALO:HALO + t, :] = hg
        hg_ref[...] = hg
        acc = jnp.zeros((t, dc), F32) + b_ref[...]
        for j in range(CONV_W):
            acc = acc + w_ref[j:j + 1, :] * ext[pl.ds(HALO - (CONV_W - 1) + j, t), :]
        hc_ref[...] = acc
        mu = jnp.mean(acc, axis=-1, keepdims=True)
        xc = acc - mu
        rs = lax.rsqrt(jnp.mean(xc * xc, axis=-1, keepdims=True) + EPS)
        ln = xc * rs * g_ref[...] + be_ref[...]
        o_ref[...] = (ln * _sigmoid(ln)).astype(BF16)

    vec = pl.BlockSpec((1, dc), lambda b, i: (0, 0))
    row = pl.BlockSpec((t, dc), lambda b, i: (b * nt + i, 0))
    tile = lambda c: pl.BlockSpec((t, dc), lambda b, i: (b * nt + i, c))
    halo = lambda c: pl.BlockSpec((HALO, dc), lambda b, i: (jnp.maximum((b * nt + i) * hb - 1, 0), c))
    return pl.pallas_call(
        body, grid=(bl, nt),
        in_specs=[tile(acol), tile(acol + 1), halo(acol), halo(acol + 1),
                  pl.BlockSpec((HALO, dc), lambda b, i: (0, 0)), vec, vec, vec],
        out_specs=[row, row, row],
        out_shape=[_sds((n, dc), F32), _sds((n, dc), F32), _sds((n, dc), BF16)],
        scratch_shapes=[pltpu.VMEM((HALO + t, dc), F32)],
        compiler_params=_cp("parallel", "arbitrary"), name=name)(proj, proj, proj, proj, wdw, bdw, lng, lnb)


def _conv_bwd_ln(dco, hc, lng, lnb, *, tm, name):
    n, dc = hc.shape

    def body(d_ref, hc_ref, g_ref, be_ref, dhc_ref, dg_ref, db_ref):
        @pl.when(pl.program_id(0) == 0)
        def _():
            dg_ref[...] = jnp.zeros_like(dg_ref)
            db_ref[...] = jnp.zeros_like(db_ref)

        hcv = hc_ref[...]
        mu = jnp.mean(hcv, axis=-1, keepdims=True)
        xc = hcv - mu
        rs = lax.rsqrt(jnp.mean(xc * xc, axis=-1, keepdims=True) + EPS)
        xh = xc * rs
        ln = xh * g_ref[...] + be_ref[...]
        sg = _sigmoid(ln)
        dln = d_ref[...].astype(F32) * (sg * (1.0 + ln * (1.0 - sg)))
        db_ref[...] += jnp.sum(dln, axis=0, keepdims=True)
        dg_ref[...] += jnp.sum(dln * xh, axis=0, keepdims=True)
        dxh = dln * g_ref[...]
        dhc_ref[...] = rs * (dxh - jnp.mean(dxh, axis=-1, keepdims=True)
                             - xh * jnp.mean(dxh * xh, axis=-1, keepdims=True))

    vec = pl.BlockSpec((1, dc), lambda i: (0, 0))
    row = pl.BlockSpec((tm, dc), lambda i: (i, 0))
    return pl.pallas_call(
        body, grid=(n // tm,), in_specs=[row, row, vec, vec], out_specs=[row, vec, vec],
        out_shape=[_sds((n, dc), F32), _sds((1, dc), F32), _sds((1, dc), F32)],
        compiler_params=_cp("arbitrary"), name=name)(dco, hc, lng, lnb)


def _conv_bwd_dw(dhc, hg, proj, wdw, *, bl, s, t, acol, name):
    n = bl * s
    dc = wdw.shape[1]
    nt = s // t
    hb = t // HALO
    lastblk = n // HALO - 1

    def body(d_ref, dn_ref, hg_ref, hp_ref, za_ref, zg_ref, w_ref, dz_ref, dw_ref, dbias_ref, extd, exth):
        b = pl.program_id(0)
        i = pl.program_id(1)

        @pl.when((b == 0) & (i == 0))
        def _():
            dw_ref[...] = jnp.zeros_like(dw_ref)
            dbias_ref[...] = jnp.zeros_like(dbias_ref)

        dv = d_ref[...]
        extd[0:t, :] = dv
        extd[t:t + HALO, :] = dn_ref[...] * jnp.where(i < nt - 1, 1.0, 0.0)
        exth[0:HALO, :] = hp_ref[...] * jnp.where(i > 0, 1.0, 0.0)
        exth[HALO:HALO + t, :] = hg_ref[...]
        dbias_ref[...] += jnp.sum(dv, axis=0, keepdims=True)
        dhg = jnp.zeros((t, dc), F32)
        for j in range(CONV_W):
            dhg = dhg + w_ref[j:j + 1, :] * extd[pl.ds(CONV_W - 1 - j, t), :]
            dw_ref[j:j + 1, :] += jnp.sum(dv * exth[pl.ds(HALO - (CONV_W - 1) + j, t), :], axis=0, keepdims=True)
        za = za_ref[...].astype(F32)
        sg = _sigmoid(zg_ref[...].astype(F32))
        dz_ref[...] = jnp.concatenate([dhg * sg, dhg * za * sg * (1.0 - sg)], axis=1).astype(BF16)

    row = pl.BlockSpec((t, dc), lambda b, i: (b * nt + i, 0))
    nxt = pl.BlockSpec((HALO, dc), lambda b, i: (jnp.minimum((b * nt + i + 1) * hb, lastblk), 0))
    prv = pl.BlockSpec((HALO, dc), lambda b, i: (jnp.maximum((b * nt + i) * hb - 1, 0), 0))
    wsp = pl.BlockSpec((HALO, dc), lambda b, i: (0, 0))
    tile = lambda c: pl.BlockSpec((t, dc), lambda b, i: (b * nt + i, c))
    return pl.pallas_call(
        body, grid=(bl, nt),
        in_specs=[row, nxt, row, prv, tile(acol), tile(acol + 1), wsp],
        out_specs=[pl.BlockSpec((t, 2 * dc), lambda b, i: (b * nt + i, 0)), wsp,
                   pl.BlockSpec((1, dc), lambda b, i: (0, 0))],
        out_shape=[_sds((n, 2 * dc), BF16), _sds((HALO, dc), F32), _sds((1, dc), F32)],
        scratch_shapes=[pltpu.VMEM((t + HALO, dc), F32), pltpu.VMEM((HALO + t, dc), F32)],
        compiler_params=_cp("arbitrary", "arbitrary"), name=name)(dhc, dhc, hg, hg, proj, proj, wdw)


def _mix_out_fwd(x, brs, gl, bg, wbs, wout, l, *, tm, name):
    n, d = x.shape

    def body(x_ref, s_ref, a_ref, c_ref, g0, g1, g2, bg_ref, ws, wa, wc, wo, o_ref):
        merged = jnp.zeros((tm, d), F32)
        for k, (br, gr, w) in enumerate(((s_ref, g0, ws), (a_ref, g1, wa), (c_ref, g2, wc))):
            gate = _sigmoid(gr[...].astype(F32) + bg_ref[:, k * d:(k + 1) * d])
            merged = merged + gate * _dot(br[...], w[...])
        o_ref[...] = x_ref[...] + _dot(merged.astype(BF16), wo[...])

    row = lambda w: pl.BlockSpec((tm, w), lambda i: (i, 0))
    wsp = lambda a: pl.BlockSpec((None,) + a.shape[1:], lambda i: (l, 0, 0))
    gls = [pl.BlockSpec((tm, d), functools.partial(lambda k, i: (i, k), k)) for k in range(3)]
    return pl.pallas_call(
        body, grid=(n // tm,),
        in_specs=[row(d), *[row(b.shape[1]) for b in brs], *gls, pl.BlockSpec(bg.shape, lambda i: (0, 0)),
                  *[wsp(w) for w in wbs], wsp(wout)],
        out_specs=row(d), out_shape=_sds((n, d), F32),
        compiler_params=_cp("parallel"), name=name)(x, *brs, gl, gl, gl, bg, *wbs, wout)


def _mix_out_bwd(dx, brs, gl, bg, wbs, wout, l, nl, bufs, *, tm, name):
    n, d = dx.shape
    widths = [b.shape[1] for b in brs]

    def body(dx_ref, s_ref, a_ref, c_ref, g0, g1, g2, bg_ref, ws, wa, wc, wo, *rest):
        ds_ref, da_ref, dc_ref, dgl_ref, dbg_ref, dws, dwa, dwc, dwo = rest[-9:]

        @pl.when(pl.program_id(0) == 0)
        def _():
            for r in (dbg_ref, dws, dwa, dwc, dwo):
                r[...] = jnp.zeros_like(r)

        dxb = dx_ref[...].astype(BF16)
        dm = _dot_t1(dxb, wo[...])
        merged = jnp.zeros((tm, d), F32)
        for k, (br, gr, w, dbr, dw) in enumerate(((s_ref, g0, ws, ds_ref, dws), (a_ref, g1, wa, da_ref, dwa),
                                                   (c_ref, g2, wc, dc_ref, dwc))):
            gate = _sigmoid(gr[...].astype(F32) + bg_ref[:, k * d:(k + 1) * d])
            brv = br[...]
            wv = w[...]
            y = _dot(brv, wv)
            merged = merged + gate * y
            dyb = (dm * gate).astype(BF16)
            dbr[...] = _dot_t1(dyb, wv).astype(BF16)
            dw[...] += _dot_t0(brv, dyb)
            dgl = dm * y * gate * (1.0 - gate)
            dgl_ref[:, k * d:(k + 1) * d] = dgl.astype(BF16)
            dbg_ref[:, k * d:(k + 1) * d] += jnp.sum(dgl, axis=0, keepdims=True)
        dwo[...] += _dot_t0(merged.astype(BF16), dxb)

    row = lambda w: pl.BlockSpec((tm, w), lambda i: (i, 0))
    wsp = lambda shape: pl.BlockSpec((None,) + tuple(shape), lambda i: (l, 0, 0))
    gls = [pl.BlockSpec((tm, d), functools.partial(lambda k, i: (i, k), k)) for k in range(3)]
    slabs = [(w, d) for w in widths] + [(d, d)]
    n_in = 12
    extra = [] if bufs is None else list(bufs)
    aliases = {} if bufs is None else {n_in + k: 5 + k for k in range(4)}
    return pl.pallas_call(
        body, grid=(n // tm,),
        in_specs=[row(d), *[row(w) for w in widths], *gls, pl.BlockSpec(bg.shape, lambda i: (0, 0)),
                  *[wsp(w.shape[1:]) for w in wbs], wsp(wout.shape[1:]), *[_ANY for _ in extra]],
        out_specs=[*[row(w) for w in widths], row(3 * d), pl.BlockSpec((1, 3 * d), lambda i: (0, 0)),
                   *[wsp(sh) for sh in slabs]],
        out_shape=[*[_sds((n, w), BF16) for w in widths], _sds((n, 3 * d), BF16), _sds((1, 3 * d), F32),
                   *[_sds((nl,) + sh, F32) for sh in slabs]],
        input_output_aliases=aliases,
        compiler_params=_cp("arbitrary"), name=name)(dx, *brs, gl, gl, gl, bg, *wbs, wout, *extra)


def _adamw(w, g, m, v, *, name):
    r, c = w.shape
    tm = _tile(r, 256)
    c1 = 1.0 - ADAM_B1 ** ADAM_STEP
    c2 = 1.0 - ADAM_B2 ** ADAM_STEP

    def body(w_ref, g_ref, m_ref, v_ref, d_ref, nm_ref, nv_ref):
        gv = g_ref[...]
        mn = ADAM_B1 * m_ref[...] + (1.0 - ADAM_B1) * gv
        vn = ADAM_B2 * v_ref[...] + (1.0 - ADAM_B2) * (gv * gv)
        nm_ref[...] = mn
        nv_ref[...] = vn
        d_ref[...] = -ADAM_LR * ((mn / c1) / (jnp.sqrt(vn / c2) + ADAM_EPS) + ADAM_WD * w_ref[...])

    blk = pl.BlockSpec((tm, c), lambda i: (i, 0))
    return pl.pallas_call(
        body, grid=(r // tm,), in_specs=[blk] * 4, out_specs=[blk] * 3,
        out_shape=[_sds((r, c), F32)] * 3, compiler_params=_cp("parallel"), name=name)(w, g, m, v)


def _add_sibling(g, recv, c_idx, *, name):
    _, a, b = g.shape
    ta = _tile(a, 256)

    def body(c_ref, g_ref, r_ref, o_ref):
        o_ref[...] = (g_ref[...] + r_ref[...]).astype(BF16)

    return pl.pallas_call(
        body,
        grid_spec=pltpu.PrefetchScalarGridSpec(
            num_scalar_prefetch=1, grid=(a // ta,),
            in_specs=[pl.BlockSpec((None, ta, b), lambda i, c_ref: (c_ref[0], i, 0)),
                      pl.BlockSpec((ta, b), lambda i, c_ref: (i, 0))],
            out_specs=pl.BlockSpec((ta, b), lambda i, c_ref: (i, 0))),
        out_shape=_sds((a, b), BF16), compiler_params=_cp("parallel"), name=name)(c_idx, g, recv)


def _add_chips(rsum, parts, axis, sc_idx, *, name):
    _, a, b = parts.shape
    ta = _tile(a, 256)
    na = a // ta

    def body(sc_ref, own_ref, p0, p1, p2, p3, o_ref):
        own = own_ref[...].astype(F32)
        terms = [jnp.where(sc_ref[0] == s, own, p[...].astype(F32)) for s, p in enumerate((p0, p1, p2, p3))]
        o_ref[...] = ((terms[0] + terms[1]) + terms[2]) + terms[3]

    own_spec = (pl.BlockSpec((ta, b), lambda i, sc: (sc[0] * na + i, 0)) if axis == 1
                else pl.BlockSpec((ta, b), lambda i, sc: (i, sc[0])))
    part_spec = lambda s: pl.BlockSpec((None, ta, b), lambda i, sc: (jnp.where(sc[0] == s, s ^ 1, s), i, 0))
    return pl.pallas_call(
        body,
        grid_spec=pltpu.PrefetchScalarGridSpec(
            num_scalar_prefetch=1, grid=(na,),
            in_specs=[own_spec] + [part_spec(s) for s in range(N_CHIPS)],
            out_specs=pl.BlockSpec((None, ta, b), lambda i, sc: (sc[1], i, 0))),
        out_shape=_sds((2, a, b), F32), compiler_params=_cp("parallel"), name=name)(
            sc_idx, rsum, parts, parts, parts, parts)


def _place_shard(wloc, axis, sc_idx, *, name):
    nl, a, b = wloc.shape
    ta = _tile(a, 256)
    na = a // ta
    full = (nl, a * N_CHIPS, b) if axis == 1 else (nl, a, b * N_CHIPS)

    def body(sc_ref, w_ref, o_ref):
        o_ref[...] = w_ref[...].astype(BF16)

    out_spec = (pl.BlockSpec((None, ta, b), lambda l, i, sc: (l, sc[0] * na + i, 0)) if axis == 1
                else pl.BlockSpec((None, ta, b), lambda l, i, sc: (l, i, sc[0])))
    return pl.pallas_call(
        body,
        grid_spec=pltpu.PrefetchScalarGridSpec(
            num_scalar_prefetch=1, grid=(nl, na),
            in_specs=[pl.BlockSpec((None, ta, b), lambda l, i, sc: (l, i, 0))], out_specs=out_spec),
        out_shape=_sds(full, BF16), compiler_params=_cp("parallel", "parallel"), name=name)(sc_idx, wloc)


def _blockdiag(w):
    g, r, c = w.shape
    eye = jnp.eye(g, dtype=w.dtype)
    return (w[:, :, None, :] * eye[:, None, :, None]).reshape(g * r, g * c)


def _s5_prep(lre, lim, log_dt, b_re, b_im, c_re, c_im, d_skip):
    lr = jnp.minimum(lre, -1e-4)
    li = lim
    dt = jnp.exp(log_dt)[:, None]
    mag = jnp.exp(lr * dt)
    ar = mag * jnp.cos(li * dt)
    ai = mag * jnp.sin(li * dt)
    den = lr * lr + li * li
    coef_r = ((ar - 1.0) * lr + ai * li) / den
    coef_i = (ai * lr - (ar - 1.0) * li) / den
    bbar_r = coef_r[..., None] * b_re - coef_i[..., None] * b_im
    bbar_i = coef_r[..., None] * b_im + coef_i[..., None] * b_re
    a = jnp.stack([ar.reshape(-1), ai.reshape(-1)])
    return dict(
        a=a,
        bblk_r=_blockdiag(bbar_r.transpose(0, 2, 1)), bblk_i=_blockdiag(bbar_i.transpose(0, 2, 1)),
        cblk_r=_blockdiag(c_re.transpose(0, 2, 1)), cblk_in=_blockdiag(-c_im.transpose(0, 2, 1)),
        d=d_skip.reshape(1, -1))


def _s5_powers(a, nlog):
    pr, pi = a[0], a[1]
    rows = []
    for _ in range(nlog):
        rows += [pr, pi]
        pr, pi = pr * pr - pi * pi, 2.0 * pr * pi
    return jnp.stack(rows)


def _bias_table(rel_bias):
    h = rel_bias.shape[0]
    tq, tw = ATT_TQ, 3 * ATT_TQ
    n_hi = tw - 1 - MAX_REL + 1
    n_lo = tq + tw - 1 - n_hi - (2 * MAX_REL - 1)
    fr = jnp.concatenate([
        jnp.broadcast_to(rel_bias[:, 2 * MAX_REL:], (h, n_hi)),
        jnp.flip(rel_bias[:, 1:2 * MAX_REL], axis=1),
        jnp.broadcast_to(rel_bias[:, :1], (h, n_lo)),
        jnp.zeros((h, 1), rel_bias.dtype)], axis=1)
    ln = tq + tw
    flat = jnp.broadcast_to(fr[:, None, :], (h, tq, ln)).reshape(h, tq * ln)[:, :tq * (ln - 1)]
    tab = flat.reshape(h, tq, ln - 1)[:, :, tq - 1:tq - 1 + tw]
    qc = np.arange(tq)[:, None] // CHUNK + N_LEFT
    kc = np.arange(tw)[None, :] // CHUNK
    band = (kc <= qc) & (kc >= qc - N_LEFT)
    return jnp.where(jnp.asarray(band)[None], tab, NEG)


def _small_prep(w, l):
    sp = _s5_prep(w["s5_lambda_re"][l], w["s5_lambda_im"][l], w["s5_log_dt"][l], w["s5_b_re"][l], w["s5_b_im"][l],
                  w["s5_c_re"][l], w["s5_c_im"][l], w["s5_d"][l])
    return sp, _bias_table(w["attn_rel_bias"][l])


_PREP_KEYS = ("s5_lambda_re", "s5_lambda_im", "s5_log_dt", "s5_b_re", "s5_b_im", "s5_c_re", "s5_c_im", "s5_d",
              "attn_rel_bias")
_BIG_KEYS = {"ffn1_w_up": 2, "ffn1_w_down": 1, "w_in": 2, "s5_w_glu": 2, "w_br_s5": 2, "w_br_attn": 2,
             "w_br_conv": 2, "w_out": 1, "ffn2_w_up": 2, "ffn2_w_down": 1}
_SMALL_KEYS = ("ffn1_norm", "mix_norm", "b_gate", "s5_lambda_re", "s5_lambda_im", "s5_log_dt", "s5_b_re", "s5_b_im",
               "s5_c_re", "s5_c_im", "s5_d", "attn_q_gain", "attn_k_gain", "attn_rel_bias", "conv_w_dw", "conv_b_dw",
               "conv_ln_g", "conv_ln_b", "ffn2_norm")
_WEIGHTS = ("ffn1_norm", "ffn1_w_up", "ffn1_w_down", "mix_norm", "w_in", "b_gate", "s5_lambda_re", "s5_lambda_im",
            "s5_log_dt", "s5_b_re", "s5_b_im", "s5_c_re", "s5_c_im", "s5_d", "s5_w_glu", "w_br_s5", "attn_q_gain",
            "attn_k_gain", "attn_rel_bias", "w_br_attn", "conv_w_dw", "conv_b_dw", "conv_ln_g", "conv_ln_b",
            "w_br_conv", "w_out", "ffn2_norm", "ffn2_w_up", "ffn2_w_down")


def _local_step(x3, target3, w):
    bl, s, d = x3.shape
    nl = w["ffn1_norm"].shape[0]
    dff = w["ffn1_w_down"].shape[1]
    ds5 = w["s5_d"].shape[1]
    datt = w["w_br_attn"].shape[1]
    dc = w["conv_b_dw"].shape[1]
    n = bl * s
    x = x3.reshape(n, d)
    target = target3.reshape(n, d)
    tm = _tile(n, 512)
    tmix = _tile(n, 256)
    ts5 = 256
    tconv = _tile(s, 512)
    nlog = int(math.log2(ts5))
    tff = dff // 2
    ma = ds5 + 3 * datt + 2 * dc
    tna = ma // 3
    assert (3 * d) % tna == 0 and dff % 2 == 0
    qoff = ds5 // LANES
    koff = (ds5 + datt) // LANES
    acol = (ds5 + 3 * datt) // dc
    wbs = (w["w_br_s5"], w["w_br_attn"], w["w_br_conv"])

    saved = []
    for l in range(nl):
        (sp, bias), prep_vjp = jax.vjp(lambda ww: _small_prep(ww, l), {k: w[k] for k in _PREP_KEYS})
        spb = dict(sp)
        spb["pw"] = _s5_powers(lax.stop_gradient(sp["a"]), nlog)
        for k in ("bblk_r", "bblk_i", "cblk_r", "cblk_in"):
            spb[k] = sp[k].astype(BF16)
        g1 = w["ffn1_norm"][l][None]
        g2 = w["ffn2_norm"][l][None]
        gm = w["mix_norm"][l][None]
        gq2 = jnp.tile(w["attn_q_gain"][l], 2)[None]
        gk2 = jnp.tile(w["attn_k_gain"][l], 2)[None]
        wdw = jnp.pad(w["conv_w_dw"][l], ((0, HALO - CONV_W), (0, 0)))
        bdw, lng, lnb = w["conv_b_dw"][l][None], w["conv_ln_g"][l][None], w["conv_ln_b"][l][None]
        bg = w["b_gate"][l][None]

        x0 = x
        h1, ab1 = _norm_mm(x0, g1, w["ffn1_w_up"], l, tm=tm, tn=tff, ntiles=4, pieces=2, transposed=False,
                           name=f"ffn1_up_{l}")
        x1 = _ffn_down(ab1, w["ffn1_w_down"], l, x0, tm=tm, tk=tff, name=f"ffn1_down_{l}")
        h2, pa = _norm_mm(x1, gm, w["w_in"], l, tm=tm, tn=tna, ntiles=3, pieces=1, transposed=True, name=f"win_a_{l}")
        pa = pa[0]
        gl = _mm_t(h2, w["w_in"], l, tm=tm, tn=tna, off=3, ntiles=3 * d // tna, name=f"win_g_{l}")
        xr, xi, yp, zg, s5o = _s5_fwd(pa, spb, w["s5_w_glu"], l, bl=bl, s=s, t=ts5, name=f"s5_fwd_{l}")
        atto = _attn_fwd(pa, gq2, gk2, bias, bl=bl, s=s, datt=datt, qoff=qoff, name=f"attn_fwd_{l}")
        hg, hc, convo = _conv_fwd(pa, wdw, bdw, lng, lnb, bl=bl, s=s, t=tconv, acol=acol, name=f"conv_fwd_{l}")
        brs = (s5o, atto, convo)
        x2 = _mix_out_fwd(x1, brs, gl, bg, wbs, w["w_out"], l, tm=tmix, name=f"mix_fwd_{l}")
        h3, ab2 = _norm_mm(x2, g2, w["ffn2_w_up"], l, tm=tm, tn=tff, ntiles=4, pieces=2, transposed=False,
                           name=f"ffn2_up_{l}")
        x = _ffn_down(ab2, w["ffn2_w_down"], l, x2, tm=tm, tk=tff, name=f"ffn2_down_{l}")
        saved.append(dict(spb=spb, bias=bias, prep_vjp=prep_vjp, g1=g1, g2=g2, gm=gm, gq2=gq2, gk2=gk2,
                          wdw=wdw, lng=lng, lnb=lnb, bg=bg, x0=x0, h1=h1, ab1=ab1, x1=x1, h2=h2, pa=pa, gl=gl,
                          xr=xr, xi=xi, yp=yp, zg=zg, hg=hg, hc=hc, brs=brs, x2=x2, h3=h3, ab2=ab2))

    dx, lsum = _loss_grad(x, target, tm=tm, name="loss")
    loss_part = 0.5 * jnp.sum(lsum) / d

    big = {k: None for k in _BIG_KEYS}
    small = {k: [None] * nl for k in _SMALL_KEYS}
    for l in reversed(range(nl)):
        sv = saved[l]

        def ffn_bwd(dx, xin, h, ab, g, tag):
            wu, wd = w[tag + "_w_up"], w[tag + "_w_down"]
            dab, act, dob = _ffn_dact(dx, wd, l, ab, tm=tm, tk=tff, name=f"{tag}_dact_{l}")
            big[tag + "_w_down"] = _mm_tn(act[None], dob[None], l, nl, big[tag + "_w_down"], ta=tff, tb=d, tk=tm,
                                          name=f"{tag}_dwd_{l}")
            big[tag + "_w_up"] = _mm_tn(h[None], dab, l, nl, big[tag + "_w_up"], ta=d, tb=tff, tk=tm,
                                        name=f"{tag}_dwu_{l}")
            dxo, dg = _ffn_dx(dab, wu, l, xin, g, dx, tm=tm, tk=tff, name=f"{tag}_dx_{l}")
            small[tag + "_norm"][l] = dg[0]
            return dxo

        dx = ffn_bwd(dx, sv["x2"], sv["h3"], sv["ab2"], sv["g2"], "ffn2")

        mix_keys = ("w_br_s5", "w_br_attn", "w_br_conv", "w_out")
        bufs = None if big["w_out"] is None else [big[k] for k in mix_keys]
        ds5o, datto, dconvo, dgl, dbg, *dws = _mix_out_bwd(
            dx, sv["brs"], sv["gl"], sv["bg"], wbs, w["w_out"], l, nl, bufs, tm=tmix, name=f"mix_bwd_{l}")
        small["b_gate"][l] = dbg[0]
        big.update(zip(mix_keys, dws))

        dhc, dlng, dlnb = _conv_bwd_ln(dconvo, sv["hc"], sv["lng"], sv["lnb"], tm=tm, name=f"conv_bwd_ln_{l}")
        dz, dwdw, dbdw = _conv_bwd_dw(dhc, sv["hg"], sv["pa"], sv["wdw"], bl=bl, s=s, t=tconv, acol=acol,
                                      name=f"conv_bwd_dw_{l}")
        small["conv_w_dw"][l] = dwdw[:CONV_W]
        small["conv_b_dw"][l], small["conv_ln_g"][l], small["conv_ln_b"][l] = dbdw[0], dlng[0], dlnb[0]

        dq, dkn, dvw, dbias, dgq = _attn_bwd(datto, sv["pa"], sv["gq2"], sv["gk2"], sv["bias"], bl=bl, s=s, datt=datt,
                                             qoff=qoff, name=f"attn_bwd_{l}")
        dk, dv, dgk = _attn_kv_bwd(dkn, dvw, sv["pa"], sv["gk2"], bl=bl, s=s, datt=datt, tm=_tile(s, 512), koff=koff,
                                   name=f"attn_kv_bwd_{l}")
        small["attn_q_gain"][l] = jnp.sum(dgq.reshape(-1, HEAD_DIM), axis=0)
        small["attn_k_gain"][l] = jnp.sum(dgk.reshape(-1, HEAD_DIM), axis=0)

        du, dd, dcr, dci, dbr, dbi, da, big["s5_w_glu"] = _s5_bwd(
            ds5o, sv["yp"], sv["zg"], sv["xr"], sv["xi"], sv["pa"], sv["spb"], w["s5_w_glu"], l, nl, big["s5_w_glu"],
            bl=bl, s=s, t=ts5, name=f"s5_bwd_{l}")
        prep_ct = (dict(a=da, bblk_r=dbr, bblk_i=dbi, cblk_r=dcr, cblk_in=dci, d=dd), jnp.sum(dbias, axis=0))
        (dprep,) = sv["prep_vjp"](prep_ct)
        for k in _PREP_KEYS:
            small[k][l] = dprep[k][l]

        dpa = jnp.concatenate([du, dq, dk, dv, dz], axis=1)
        big["w_in"] = _dwin_t(dpa, dgl, sv["h2"], l, nl, big["w_in"], ta=tna, tk=tm, name=f"dwin_{l}")
        dx, dgm = _mix_dx(dpa, dgl, w["w_in"], l, sv["x1"], sv["gm"], dx, tm=tm, tk=tna, name=f"mix_dx_{l}")
        small["mix_norm"][l] = dgm[0]

        dx = ffn_bwd(dx, sv["x0"], sv["h1"], sv["ab1"], sv["g1"], "ffn1")

    small = {k: jnp.stack(v) for k, v in small.items()}
    return loss_part, dx.reshape(bl, s, d), big, small


def _place():
    x, y, c = lax.axis_index("x"), lax.axis_index("y"), lax.axis_index("c")
    chips = [(1 - x, y), (x, 1 - y), (1 - x, 1 - y)]
    return x, y, c, chips


def _remote(src, dst, send_sems, recv_sems, k, dev):
    return pltpu.make_async_remote_copy(src_ref=src, dst_ref=dst, send_sem=send_sems.at[k], recv_sem=recv_sems.at[k],
                                        device_id=dev, device_id_type=MESH)


def _window(ref, lead, s, axis, blk):
    if axis == 1:
        sl = (pl.ds(pl.multiple_of(s * blk, 16), blk), slice(None))
    else:
        sl = (slice(None), pl.ds(pl.multiple_of(s * blk, LANES), blk))
    return ref.at[sl] if lead is None else ref.at[(lead,) + sl]


def _all_gather_weights(fulls, axes, taps):
    nw = len(fulls)
    assert fulls[0].shape[0] == 2
    blks = [f.shape[ax] // N_CHIPS for f, ax in zip(fulls, axes)]

    def body(*refs):
        taps_in = refs[nw]
        outs, taps_out = refs[nw + 1:2 * nw + 1], refs[2 * nw + 1]
        send_sems, recv_sems, local_sem = refs[-3:]
        x, y, c, chips = _place()
        s_me = 2 * x + y
        sibling = (x, y, 1 - c)
        win = lambda i, lyr, s: _window(outs[i], lyr, s, axes[i], blks[i])
        own_taps = pltpu.make_async_copy(taps_in, taps_out.at[s_me], local_sem)
        own_taps.start()
        sends = []
        for i in range(nw):
            for j, (cx, cy) in enumerate(chips):
                mine = win(i, c, s_me)
                sends.append(_remote(mine, mine, send_sems, recv_sems, 6 * i + j, (cx, cy, c)))
        for j, (cx, cy) in enumerate(chips):
            sends.append(_remote(taps_in, taps_out.at[s_me], send_sems, recv_sems, 6 * nw + j, (cx, cy, c)))
        for cp in sends:
            cp.start()
        for i in range(nw):
            for j, (cx, cy) in enumerate(chips):
                piece = win(i, c, 2 * cx + cy)
                _remote(piece, piece, send_sems, recv_sems, 6 * i + j, (cx, cy, c)).wait_recv()
                fw = _remote(piece, piece, send_sems, recv_sems, 6 * i + 3 + j, sibling)
                fw.start()
                sends.append(fw)
        for i in range(nw):
            for j, (cx, cy) in enumerate(chips):
                piece = win(i, 1 - c, 2 * cx + cy)
                _remote(piece, piece, send_sems, recv_sems, 6 * i + 3 + j, sibling).wait_recv()
        for j, (cx, cy) in enumerate(chips):
            slab = taps_out.at[2 * cx + cy]
            _remote(slab, slab, send_sems, recv_sems, 6 * nw + j, (cx, cy, c)).wait_recv()
        for cp in sends:
            cp.wait_send()
        own_taps.wait()

    nsem = 6 * nw + 3
    return pl.pallas_call(
        body, in_specs=[_ANY] * (nw + 1), out_specs=[_ANY] * (nw + 1),
        out_shape=[_sds(f.shape, f.dtype) for f in fulls] + [_sds((N_CHIPS,) + taps.shape, taps.dtype)],
        input_output_aliases={i: i for i in range(nw)},
        scratch_shapes=[pltpu.SemaphoreType.DMA((nsem,)), pltpu.SemaphoreType.DMA((nsem,)), pltpu.SemaphoreType.DMA],
        name="all_gather_weights")(*fulls, taps)


def _rs_swap(grads):
    nw = len(grads)

    def body(*refs):
        ins, outs = refs[:nw], refs[nw:2 * nw]
        send_sems, recv_sems = refs[-2:]
        x, y, c, _ = _place()
        cps = [_remote(ins[i].at[1 - c], outs[i], send_sems, recv_sems, i, (x, y, 1 - c)) for i in range(nw)]
        for cp in cps:
            cp.start()
        for cp in cps:
            cp.wait()

    return pl.pallas_call(
        body, in_specs=[_ANY] * nw, out_specs=[_ANY] * nw, out_shape=[_sds(g.shape[1:], g.dtype) for g in grads],
        scratch_shapes=[pltpu.SemaphoreType.DMA((nw,)), pltpu.SemaphoreType.DMA((nw,))],
        name="rs_swap_layers")(*grads)


def _rs_exchange(rsums, axes):
    nw = len(rsums)
    blks = [r.shape[ax - 1] // N_CHIPS for r, ax in zip(rsums, axes)]
    shard = [tuple(dim // N_CHIPS if i == ax - 1 else dim for i, dim in enumerate(r.shape)) for r, ax in zip(rsums, axes)]

    def body(*refs):
        ins, outs = refs[:nw], refs[nw:2 * nw]
        send_sems, recv_sems = refs[-2:]
        x, y, c, chips = _place()
        s_me = 2 * x + y
        win = lambda i, s: _window(ins[i], None, s, axes[i], blks[i])
        sends = [_remote(win(i, 2 * cx + cy), outs[i].at[s_me], send_sems, recv_sems, 3 * i + j, (cx, cy, c))
                 for i in range(nw) for j, (cx, cy) in enumerate(chips)]
        for cp in sends:
            cp.start()
        for i in range(nw):
            for j, (cx, cy) in enumerate(chips):
                slab = outs[i].at[2 * cx + cy]
                _remote(slab, slab, send_sems, recv_sems, 3 * i + j, (cx, cy, c)).wait_recv()
        for cp in sends:
            cp.wait_send()

    return pl.pallas_call(
        body, in_specs=[_ANY] * nw, out_specs=[_ANY] * nw,
        out_shape=[_sds((N_CHIPS,) + sh, r.dtype) for sh, r in zip(shard, rsums)],
        scratch_shapes=[pltpu.SemaphoreType.DMA((3 * nw,)), pltpu.SemaphoreType.DMA((3 * nw,))],
        name="rs_exchange_chips")(*rsums)


def _rs_join(ts):
    nw = len(ts)

    def body(*refs):
        outs = refs[nw:2 * nw]
        send_sems, recv_sems = refs[-2:]
        x, y, c, _ = _place()
        sends = [_remote(outs[i].at[c], outs[i].at[c], send_sems, recv_sems, i, (x, y, 1 - c)) for i in range(nw)]
        for cp in sends:
            cp.start()
        for i in range(nw):
            slab = outs[i].at[1 - c]
            _remote(slab, slab, send_sems, recv_sems, i, (x, y, 1 - c)).wait_recv()
        for cp in sends:
            cp.wait_send()

    return pl.pallas_call(
        body, in_specs=[_ANY] * nw, out_specs=[_ANY] * nw, out_shape=[_sds(t.shape, t.dtype) for t in ts],
        input_output_aliases={i: i for i in range(nw)},
        scratch_shapes=[pltpu.SemaphoreType.DMA((nw,)), pltpu.SemaphoreType.DMA((nw,))],
        name="rs_join_layers")(*ts)


def _all_reduce_small(buf):
    r, cols = buf.shape
    nd = 8

    def body(b_ref, o_ref, recv, send_sems, recv_sems):
        x, y, c, _ = _place()
        me = 4 * x + 2 * y + c
        recv[0] = b_ref[...]
        cps = []
        for rel in range(1, nd):
            dev = (1 - x if rel & 4 else x, 1 - y if rel & 2 else y, 1 - c if rel & 1 else c)
            cp = _remote(b_ref, recv.at[rel], send_sems, recv_sems, rel - 1, dev)
            cp.start()
            cps.append(cp)
        for rel in range(1, nd):
            _remote(b_ref, recv.at[rel], send_sems, recv_sems, rel - 1, (x, y, c)).wait_recv()
        acc = recv[me]
        for dv in range(1, nd):
            acc = acc + recv[lax.bitwise_xor(me, dv)]
        o_ref[...] = acc
        for cp in cps:
            cp.wait_send()

    vm = pl.BlockSpec(memory_space=pltpu.VMEM)
    return pl.pallas_call(
        body, in_specs=[vm], out_specs=vm, out_shape=_sds((r, cols), F32),
        scratch_shapes=[pltpu.VMEM((nd, r, cols), F32), pltpu.SemaphoreType.DMA((nd - 1,)),
                        pltpu.SemaphoreType.DMA((nd - 1,))],
        compiler_params=pltpu.CompilerParams(vmem_limit_bytes=VMEM_LIMIT), name="all_reduce_small")(buf)


def _pack_rows(parts, row_align):
    flat = jnp.concatenate([p.reshape(-1) for p in parts])
    per = PACK_COLS * row_align
    size = -(-flat.shape[0] // per) * per
    return jnp.pad(flat, (0, size - flat.shape[0])).reshape(-1, PACK_COLS)


def _unpack(flat, shapes):
    out, off = [], 0
    for shp in shapes:
        size = int(np.prod(shp))
        out.append(flat[off:off + size].reshape(shp))
        off += size
    return out


def kernel(x, ffn1_norm, ffn1_w_up, ffn1_w_down, mix_norm, w_in, b_gate, s5_lambda_re, s5_lambda_im, s5_log_dt, s5_b_re, s5_b_im, s5_c_re, s5_c_im, s5_d, s5_w_glu, w_br_s5, attn_q_gain, attn_k_gain, attn_rel_bias, w_br_attn, conv_w_dw, conv_b_dw, conv_ln_g, conv_ln_b, w_br_conv, w_out, ffn2_norm, ffn2_w_up, ffn2_w_down, loss_target, m_ffn1_norm, m_ffn1_w_up, m_ffn1_w_down, m_mix_norm, m_w_in, m_b_gate, m_s5_lambda_re, m_s5_lambda_im, m_s5_log_dt, m_s5_b_re, m_s5_b_im, m_s5_c_re, m_s5_c_im, m_s5_d, m_s5_w_glu, m_w_br_s5, m_attn_q_gain, m_attn_k_gain, m_attn_rel_bias, m_w_br_attn, m_conv_w_dw, m_conv_b_dw, m_conv_ln_g, m_conv_ln_b, m_w_br_conv, m_w_out, m_ffn2_norm, m_ffn2_w_up, m_ffn2_w_down, v_ffn1_norm, v_ffn1_w_up, v_ffn1_w_down, v_mix_norm, v_w_in, v_b_gate, v_s5_lambda_re, v_s5_lambda_im, v_s5_log_dt, v_s5_b_re, v_s5_b_im, v_s5_c_re, v_s5_c_im, v_s5_d, v_s5_w_glu, v_w_br_s5, v_attn_q_gain, v_attn_k_gain, v_attn_rel_bias, v_w_br_attn, v_conv_w_dw, v_conv_b_dw, v_conv_ln_g, v_conv_ln_b, v_w_br_conv, v_w_out, v_ffn2_norm, v_ffn2_w_up, v_ffn2_w_down):
    a = dict(locals())
    xi, yi, ci = lax.axis_index("x"), lax.axis_index("y"), lax.axis_index("c")
    s_me = 2 * xi + yi
    big_keys = list(_BIG_KEYS)
    axes = [1 if k == "w_in" else _BIG_KEYS[k] for k in big_keys]

    sc_idx = jnp.stack([s_me, ci]).astype(jnp.int32)
    placed = [_place_shard(jnp.swapaxes(a[k], 1, 2).astype(BF16) if k == "w_in" else a[k], ax, sc_idx,
                           name=f"place_{k}") for k, ax in zip(big_keys, axes)]
    *fulls, taps = _all_gather_weights(placed, axes, a["conv_w_dw"])
    w = {k: a[k] for k in _WEIGHTS}
    w.update(zip(big_keys, fulls))
    w["conv_w_dw"] = jnp.moveaxis(taps, 0, 2).reshape(taps.shape[1], taps.shape[2], -1)

    loss_part, grad_x, gbig, gsmall = _local_step(a["x"], a["loss_target"], w)
    loss = lax.psum(loss_part, ("x", "y", "c"))

    c_idx = ci.astype(jnp.int32).reshape(1)
    glist = [gbig[k] for k in big_keys]
    recv = _rs_swap(glist)
    rsums = [_add_sibling(g, r, c_idx, name=f"rs_add_sibling_{k}") for g, r, k in zip(glist, recv, big_keys)]
    parts = _rs_exchange(rsums, axes)
    mine = [_add_chips(r, p, ax, sc_idx, name=f"rs_add_chips_{k}") for r, p, ax, k in zip(rsums, parts, axes, big_keys)]
    gb = dict(zip(big_keys, _rs_join(mine)))
    gb["w_in"] = jnp.swapaxes(gb["w_in"], 1, 2)

    small_keys = list(_SMALL_KEYS)
    sred = _all_reduce_small(_pack_rows([gsmall[k] for k in small_keys], 8)).reshape(-1)
    gs = dict(zip(small_keys, _unpack(sred, [gsmall[k].shape for k in small_keys])))
    blk = a["conv_w_dw"].shape[2]
    gs["conv_w_dw"] = lax.dynamic_slice_in_dim(gs["conv_w_dw"], s_me * blk, blk, axis=2)
    grads = {**gb, **gs}

    delta, new_m, new_v = {}, {}, {}
    for k in big_keys:
        shp = a[k].shape
        two_d = lambda t: t.reshape(-1, shp[-1])
        d_, m_, v_ = _adamw(two_d(a[k]), two_d(grads[k]), two_d(a["m_" + k]), two_d(a["v_" + k]), name=f"adamw_{k}")
        delta[k], new_m[k], new_v[k] = d_.reshape(shp), m_.reshape(shp), v_.reshape(shp)
    sm_shapes = [a[k].shape for k in small_keys]
    packs = [_pack_rows([src[k] for k in small_keys], 8)
             for src in (a, grads, {k: a["m_" + k] for k in small_keys}, {k: a["v_" + k] for k in small_keys})]
    d_, m_, v_ = _adamw(*packs, name="adamw_small")
    for dst, res in ((delta, d_), (new_m, m_), (new_v, v_)):
        dst.update(zip(small_keys, _unpack(res.reshape(-1), sm_shapes)))

    return (loss, grad_x, *[grads[k] for k in _WEIGHTS], *[delta[k] for k in _WEIGHTS],
            *[new_m[k] for k in _WEIGHTS], *[new_v[k] for k in _WEIGHTS])
```

```python
import functools
import math

import numpy as np
import jax
import jax.numpy as jnp
from jax import lax
from jax.experimental import pallas as pl
from jax.experimental.pallas import tpu as pltpu

F32 = jnp.float32
BF16 = jnp.bfloat16
EPS = 1e-6
VMEM_LIMIT = 56 * 1024 * 1024
LANES = 128
HEAD_DIM = 64
CHUNK = 64
N_LEFT = 8
MAX_REL = 128
ATT_TQ = 256
CONV_W = 31
HALO = 32
ROW_CHUNK = 256
NEG = -1e30
N_CHIPS = 4
PACK_COLS = 1024

ADAM_LR = 0.001
ADAM_B1 = 0.9
ADAM_B2 = 0.999
ADAM_EPS = 1e-08
ADAM_WD = 0.01
ADAM_STEP = 10

MESH = pl.DeviceIdType.MESH
_ANY = pl.BlockSpec(memory_space=pl.ANY)


def _cp(*sem):
    return pltpu.CompilerParams(dimension_semantics=sem, vmem_limit_bytes=VMEM_LIMIT)


def _sds(shape, dtype):
    return jax.ShapeDtypeStruct(shape, dtype)


def _tile(n, pref):
    t = min(n, pref)
    while n % t:
        t -= 8
    return t


def _sigmoid(x):
    return jax.nn.sigmoid(x)


_GELU_C = math.sqrt(2.0 / math.pi)


def _gelu(y):
    return 0.5 * y * (1.0 + jnp.tanh(_GELU_C * (y + 0.044715 * y * y * y)))


def _gelu_grad(y):
    th = jnp.tanh(_GELU_C * (y + 0.044715 * y * y * y))
    return 0.5 * (1.0 + th) + 0.5 * y * (1.0 - th * th) * _GELU_C * (1.0 + 3.0 * 0.044715 * y * y)


def _dot(a, b):
    return jnp.dot(a, b, preferred_element_type=F32)


def _dot_t0(a, b):
    return lax.dot_general(a, b, (((0,), (0,)), ((), ())), preferred_element_type=F32)


def _dot_t1(a, b):
    return lax.dot_general(a, b, (((1,), (1,)), ((), ())), preferred_element_type=F32)


def _slab_out(nl, l, shape, buf, n_in):
    sds = _sds((nl,) + tuple(shape), F32)
    if buf is None:
        return [], [], sds, {}
    return [buf], [_ANY], sds, {n_in: 0}


def _norm_mm(x, g, w, l, *, tm, tn, ntiles, pieces, transposed, name):
    n, d = x.shape
    m = ntiles * tn
    mp = m // pieces
    npj = mp // tn

    def body(x_ref, g_ref, w_ref, h_ref, y_ref, h_scr):
        @pl.when(pl.program_id(1) == 0)
        def _():
            for r0 in range(0, tm, ROW_CHUNK):
                rows = slice(r0, r0 + ROW_CHUNK)
                xv = x_ref[rows, :]
                r = lax.rsqrt(jnp.mean(xv * xv, axis=-1, keepdims=True) + EPS)
                hb = (xv * r * g_ref[...]).astype(BF16)
                h_scr[rows, :] = hb
                h_ref[rows, :] = hb

        mm = _dot_t1 if transposed else _dot
        y_ref[...] = mm(h_scr[...], w_ref[...]).astype(BF16)

    wspec = (pl.BlockSpec((None, tn, d), lambda i, j: (l, j, 0)) if transposed
             else pl.BlockSpec((None, d, tn), lambda i, j: (l, 0, j)))
    return pl.pallas_call(
        body, grid=(n // tm, ntiles),
        in_specs=[pl.BlockSpec((tm, d), lambda i, j: (i, 0)), pl.BlockSpec((1, d), lambda i, j: (0, 0)), wspec],
        out_specs=[pl.BlockSpec((tm, d), lambda i, j: (i, 0)),
                   pl.BlockSpec((None, tm, tn), lambda i, j: (j // npj, i, j % npj))],
        out_shape=[_sds((n, d), BF16), _sds((pieces, n, mp), BF16)],
        scratch_shapes=[pltpu.VMEM((tm, d), BF16)],
        compiler_params=_cp("parallel", "arbitrary"), name=name)(x, g, w)


def _mm_t(a, w, l, *, tm, tn, off, ntiles, name):
    n, k = a.shape

    def body(a_ref, w_ref, y_ref):
        y_ref[...] = _dot_t1(a_ref[...], w_ref[...]).astype(BF16)

    return pl.pallas_call(
        body, grid=(n // tm, ntiles),
        in_specs=[pl.BlockSpec((tm, k), lambda i, j: (i, 0)), pl.BlockSpec((None, tn, k), lambda i, j: (l, off + j, 0))],
        out_specs=pl.BlockSpec((tm, tn), lambda i, j: (i, j)),
        out_shape=_sds((n, ntiles * tn), BF16),
        compiler_params=_cp("parallel", "arbitrary"), name=name)(a, w)


def _ffn_down(ab, wd, l, x, *, tm, tk, name):
    _, n, dff = ab.shape
    d = x.shape[1]
    nk = dff // tk

    def body(a_ref, b_ref, wd_ref, x_ref, o_ref, acc):
        k = pl.program_id(1)
        a = a_ref[...].astype(F32)
        b = b_ref[...].astype(F32)
        act = (a * _sigmoid(a) * b).astype(BF16)
        part = _dot(act, wd_ref[pl.ds(pl.multiple_of(k * tk, tk), tk), :])

        @pl.when(k == 0)
        def _():
            acc[...] = part

        @pl.when(k > 0)
        def _():
            acc[...] += part

        @pl.when(k == nk - 1)
        def _():
            o_ref[...] = x_ref[...] + 0.5 * acc[...]

    return pl.pallas_call(
        body, grid=(n // tm, nk),
        in_specs=[pl.BlockSpec((None, tm, tk), lambda i, k: (0, i, k)),
                  pl.BlockSpec((None, tm, tk), lambda i, k: (1, i, k)),
                  pl.BlockSpec((None, dff, d), lambda i, k: (l, 0, 0)),
                  pl.BlockSpec((tm, d), lambda i, k: (i, 0))],
        out_specs=pl.BlockSpec((tm, d), lambda i, k: (i, 0)),
        out_shape=_sds((n, d), F32),
        scratch_shapes=[pltpu.VMEM((tm, d), F32)],
        compiler_params=_cp("parallel", "arbitrary"), name=name)(ab, ab, wd, x)


def _ffn_dact(dx, wd, l, ab, *, tm, tk, name):
    n, d = dx.shape
    dff = ab.shape[2]

    def body(dx_ref, wd_ref, a_ref, b_ref, dab_ref, act_ref, do_ref):
        do = (0.5 * dx_ref[...]).astype(BF16)

        @pl.when(pl.program_id(1) == 0)
        def _():
            do_ref[...] = do

        dact = _dot_t1(do, wd_ref[...])
        a = a_ref[...].astype(F32)
        b = b_ref[...].astype(F32)
        sg = _sigmoid(a)
        silu = a * sg
        act_ref[...] = (silu * b).astype(BF16)
        dab_ref[0] = (dact * b * (sg * (1.0 + a * (1.0 - sg)))).astype(BF16)
        dab_ref[1] = (dact * silu).astype(BF16)

    return pl.pallas_call(
        body, grid=(n // tm, dff // tk),
        in_specs=[pl.BlockSpec((tm, d), lambda i, j: (i, 0)),
                  pl.BlockSpec((None, tk, d), lambda i, j: (l, j, 0)),
                  pl.BlockSpec((None, tm, tk), lambda i, j: (0, i, j)),
                  pl.BlockSpec((None, tm, tk), lambda i, j: (1, i, j))],
        out_specs=[pl.BlockSpec((2, tm, tk), lambda i, j: (0, i, j)),
                   pl.BlockSpec((tm, tk), lambda i, j: (i, j)),
                   pl.BlockSpec((tm, d), lambda i, j: (i, 0))],
        out_shape=[_sds((2, n, dff), BF16), _sds((n, dff), BF16), _sds((n, d), BF16)],
        compiler_params=_cp("parallel", "arbitrary"), name=name)(dx, wd, ab, ab)


def _rms_bwd_epilogue(acc, x_ref, g_ref, dres_ref, dx_ref, dg_ref, i):
    dgp = jnp.zeros(dg_ref.shape, F32)
    for r0 in range(0, acc.shape[0], ROW_CHUNK):
        rows = slice(r0, r0 + ROW_CHUNK)
        dh = acc[rows, :]
        xv = x_ref[rows, :]
        r = lax.rsqrt(jnp.mean(xv * xv, axis=-1, keepdims=True) + EPS)
        xn = xv * r
        dgp = dgp + jnp.sum(dh * xn, axis=0, keepdims=True)
        dxh = dh * g_ref[...]
        dx_ref[rows, :] = dres_ref[rows, :] + r * (dxh - xn * jnp.mean(dxh * xn, axis=-1, keepdims=True))

    @pl.when(i == 0)
    def _():
        dg_ref[...] = dgp

    @pl.when(i > 0)
    def _():
        dg_ref[...] += dgp


def _ffn_dx(dab, wu, l, x, g, dres, *, tm, tk, name):
    p, n, mp = dab.shape
    d = x.shape[1]
    nkp = mp // tk
    nk = p * nkp

    def body(dy_ref, w_ref, x_ref, g_ref, dres_ref, dx_ref, dg_ref, acc):
        k = pl.program_id(1)
        part = _dot_t1(dy_ref[...], w_ref[...])

        @pl.when(k == 0)
        def _():
            acc[...] = part

        @pl.when(k > 0)
        def _():
            acc[...] += part

        @pl.when(k == nk - 1)
        def _():
            _rms_bwd_epilogue(acc, x_ref, g_ref, dres_ref, dx_ref, dg_ref, pl.program_id(0))

    return pl.pallas_call(
        body, grid=(n // tm, nk),
        in_specs=[pl.BlockSpec((None, tm, tk), lambda i, k: (k // nkp, i, k % nkp)),
                  pl.BlockSpec((None, d, tk), lambda i, k: (l, 0, k)),
                  pl.BlockSpec((tm, d), lambda i, k: (i, 0)),
                  pl.BlockSpec((1, d), lambda i, k: (0, 0)),
                  pl.BlockSpec((tm, d), lambda i, k: (i, 0))],
        out_specs=[pl.BlockSpec((tm, d), lambda i, k: (i, 0)), pl.BlockSpec((1, d), lambda i, k: (0, 0))],
        out_shape=[_sds((n, d), F32), _sds((1, d), F32)],
        scratch_shapes=[pltpu.VMEM((tm, d), F32)],
        compiler_params=_cp("arbitrary", "arbitrary"), name=name)(dab, wu, x, g, dres)


def _mix_dx(dpa, dgl, wt, l, x, g, dres, *, tm, tk, name):
    n, d = x.shape
    n1 = dpa.shape[1] // tk
    n2 = dgl.shape[1] // tk
    nk = n1 + n2

    def body(d1_ref, d2_ref, w_ref, x_ref, g_ref, dres_ref, dx_ref, dg_ref, acc):
        k = pl.program_id(1)

        @pl.when(k == 0)
        def _():
            acc[...] = _dot(d1_ref[...], w_ref[...])

        @pl.when((k > 0) & (k < n1))
        def _():
            acc[...] += _dot(d1_ref[...], w_ref[...])

        @pl.when(k >= n1)
        def _():
            acc[...] += _dot(d2_ref[...], w_ref[...])

        @pl.when(k == nk - 1)
        def _():
            _rms_bwd_epilogue(acc, x_ref, g_ref, dres_ref, dx_ref, dg_ref, pl.program_id(0))

    return pl.pallas_call(
        body, grid=(n // tm, nk),
        in_specs=[pl.BlockSpec((tm, tk), lambda i, k: (i, jnp.minimum(k, n1 - 1))),
                  pl.BlockSpec((tm, tk), lambda i, k: (i, jnp.maximum(k - n1, 0))),
                  pl.BlockSpec((None, tk, d), lambda i, k: (l, k, 0)),
                  pl.BlockSpec((tm, d), lambda i, k: (i, 0)),
                  pl.BlockSpec((1, d), lambda i, k: (0, 0)),
                  pl.BlockSpec((tm, d), lambda i, k: (i, 0))],
        out_specs=[pl.BlockSpec((tm, d), lambda i, k: (i, 0)), pl.BlockSpec((1, d), lambda i, k: (0, 0))],
        out_shape=[_sds((n, d), F32), _sds((1, d), F32)],
        scratch_shapes=[pltpu.VMEM((tm, d), F32)],
        compiler_params=_cp("arbitrary", "arbitrary"), name=name)(dpa, dgl, wt, x, g, dres)


def _mm_tn(a, b, l, nl, buf, *, ta, tb, tk, name):
    pa, n, ka = a.shape
    pb, _, kb = b.shape
    nap = ka // ta
    nbp = kb // tb

    def body(a_ref, b_ref, *rest):
        o_ref = rest[-1]

        @pl.when(pl.program_id(2) == 0)
        def _():
            o_ref[...] = jnp.zeros_like(o_ref)

        o_ref[...] += _dot_t0(a_ref[...], b_ref[...])

    extra, extra_specs, out_shape, aliases = _slab_out(nl, l, (pa * ka, pb * kb), buf, 2)
    return pl.pallas_call(
        body, grid=(pa * nap, pb * nbp, n // tk),
        in_specs=[pl.BlockSpec((None, tk, ta), lambda i, j, k: (i // nap, k, i % nap)),
                  pl.BlockSpec((None, tk, tb), lambda i, j, k: (j // nbp, k, j % nbp)), *extra_specs],
        out_specs=pl.BlockSpec((None, ta, tb), lambda i, j, k: (l, i, j)),
        out_shape=out_shape, input_output_aliases=aliases,
        compiler_params=_cp("parallel", "parallel", "arbitrary"), name=name)(a, b, *extra)


def _dwin_t(dpa, dgl, h, l, nl, buf, *, ta, tk, name):
    n, d = h.shape
    n1 = dpa.shape[1] // ta
    n2 = dgl.shape[1] // ta

    def body(a1_ref, a2_ref, h_ref, *rest):
        o_ref = rest[-1]
        i = pl.program_id(0)

        @pl.when(pl.program_id(1) == 0)
        def _():
            o_ref[...] = jnp.zeros_like(o_ref)

        @pl.when(i < n1)
        def _():
            o_ref[...] += _dot_t0(a1_ref[...], h_ref[...])

        @pl.when(i >= n1)
        def _():
            o_ref[...] += _dot_t0(a2_ref[...], h_ref[...])

    extra, extra_specs, out_shape, aliases = _slab_out(nl, l, ((n1 + n2) * ta, d), buf, 3)
    return pl.pallas_call(
        body, grid=(n1 + n2, n // tk),
        in_specs=[pl.BlockSpec((tk, ta), lambda i, k: (jnp.where(i < n1, k, 0), jnp.minimum(i, n1 - 1))),
                  pl.BlockSpec((tk, ta), lambda i, k: (jnp.where(i >= n1, k, 0), jnp.maximum(i - n1, 0))),
                  pl.BlockSpec((tk, d), lambda i, k: (k, 0)), *extra_specs],
        out_specs=pl.BlockSpec((None, ta, d), lambda i, k: (l, i, 0)),
        out_shape=out_shape, input_output_aliases=aliases,
        compiler_params=_cp("parallel", "arbitrary"), name=name)(dpa, dgl, h, *extra)


def _loss_grad(y, t, *, tm, name):
    n, d = y.shape

    def body(y_ref, t_ref, dy_ref, l_ref):
        e = y_ref[...] - t_ref[...]
        dy_ref[...] = e * (1.0 / d)
        part = jnp.sum(e * e, axis=0, keepdims=True)

        @pl.when(pl.program_id(0) == 0)
        def _():
            l_ref[...] = part

        @pl.when(pl.program_id(0) > 0)
        def _():
            l_ref[...] += part

    return pl.pallas_call(
        body, grid=(n // tm,),
        in_specs=[pl.BlockSpec((tm, d), lambda i: (i, 0)), pl.BlockSpec((tm, d), lambda i: (i, 0))],
        out_specs=[pl.BlockSpec((tm, d), lambda i: (i, 0)), pl.BlockSpec((1, d), lambda i: (0, 0))],
        out_shape=[_sds((n, d), F32), _sds((1, d), F32)],
        compiler_params=_cp("arbitrary"), name=name)(y, t)


def _s5_fwd(proj, sp, wglu, l, *, bl, s, t, name):
    n = bl * s
    ds5, gp = sp["bblk_r"].shape
    nt = s // t
    nlog = int(math.log2(t))

    def body(u_ref, br_ref, bi_ref, a_ref, pw_ref, cr_ref, ci_ref, d_ref, wg_ref,
             xr_ref, xi_ref, yp_ref, zg_ref, o_ref, carry):
        @pl.when(pl.program_id(1) == 0)
        def _():
            carry[...] = jnp.zeros_like(carry)

        u = u_ref[...]
        rows = lax.broadcasted_iota(jnp.int32, (t, gp), 0)
        ar = a_ref[0:1, :]
        ai = a_ref[1:2, :]
        cr = carry[0:1, :]
        ci = carry[1:2, :]
        first = rows == 0
        xr = _dot(u, br_ref[...]) + jnp.where(first, ar * cr - ai * ci, 0.0)
        xi = _dot(u, bi_ref[...]) + jnp.where(first, ar * ci + ai * cr, 0.0)
        for k in range(nlog):
            sh = 1 << k
            pr = pw_ref[2 * k:2 * k + 1, :]
            pi = pw_ref[2 * k + 1:2 * k + 2, :]
            keep = rows >= sh
            sr = jnp.where(keep, pltpu.roll(xr, sh, 0), 0.0)
            si = jnp.where(keep, pltpu.roll(xi, sh, 0), 0.0)
            xr, xi = xr + pr * sr - pi * si, xi + pr * si + pi * sr
        last = rows == t - 1
        carry[0:1, :] = jnp.sum(jnp.where(last, xr, 0.0), axis=0, keepdims=True)
        carry[1:2, :] = jnp.sum(jnp.where(last, xi, 0.0), axis=0, keepdims=True)
        xr_ref[...] = xr
        xi_ref[...] = xi
        y = _dot(xr.astype(BF16), cr_ref[...]) + _dot(xi.astype(BF16), ci_ref[...]) + d_ref[...] * u.astype(F32)
        yp_ref[...] = y
        zg = _dot(_gelu(y).astype(BF16), wg_ref[...])
        zg_ref[...] = zg
        o_ref[...] = (zg[:, :ds5] * _sigmoid(zg[:, ds5:])).astype(BF16)

    const = lambda shape: pl.BlockSpec(shape, lambda b, i: (0, 0))
    row = lambda w: pl.BlockSpec((t, w), lambda b, i: (b * nt + i, 0))
    return pl.pallas_call(
        body, grid=(bl, nt),
        in_specs=[row(ds5), const((ds5, gp)), const((ds5, gp)), const((2, gp)), const((2 * nlog, gp)),
                  const((gp, ds5)), const((gp, ds5)), const((1, ds5)),
                  pl.BlockSpec((None, ds5, 2 * ds5), lambda b, i: (l, 0, 0))],
        out_specs=[row(gp), row(gp), row(ds5), row(2 * ds5), row(ds5)],
        out_shape=[_sds((n, gp), F32), _sds((n, gp), F32), _sds((n, ds5), F32), _sds((n, 2 * ds5), F32),
                   _sds((n, ds5), BF16)],
        scratch_shapes=[pltpu.VMEM((2, gp), F32)],
        compiler_params=_cp("arbitrary", "arbitrary"), name=name)(
            proj, sp["bblk_r"], sp["bblk_i"], sp["a"], sp["pw"], sp["cblk_r"], sp["cblk_in"], sp["d"], wglu)


def _s5_bwd(ds, yp, zg, xr, xi, proj, sp, wglu, l, nl, dwg_buf, *, bl, s, t, name):
    n = bl * s
    ds5, gp = sp["bblk_r"].shape
    nt = s // t
    nlog = int(math.log2(t))
    tb = t // 8

    def body(ds_ref, yp_ref, zg_ref, xr_ref, xi_ref, hr_ref, hi_ref, u_ref, wg_ref, cr_ref, ci_ref,
             br_ref, bi_ref, a_ref, pw_ref, d_ref, *rest):
        du_ref, dd_ref, dcr_ref, dci_ref, dbr_ref, dbi_ref, da_ref, dwg_ref, carry = rest[-9:]
        b = pl.program_id(0)
        i = pl.program_id(1)
        tile = nt - 1 - i

        @pl.when((b == 0) & (i == 0))
        def _():
            for r in (dwg_ref, dd_ref, dcr_ref, dci_ref, dbr_ref, dbi_ref, da_ref):
                r[...] = jnp.zeros_like(r)

        @pl.when(i == 0)
        def _():
            carry[...] = jnp.zeros_like(carry)

        dsv = ds_ref[...].astype(F32)
        zgv = zg_ref[...]
        za = zgv[:, :ds5]
        sg = _sigmoid(zgv[:, ds5:])
        dzg = jnp.concatenate([dsv * sg, dsv * za * sg * (1.0 - sg)], axis=1).astype(BF16)
        y = yp_ref[...]
        dwg_ref[...] += _dot_t0(_gelu(y).astype(BF16), dzg)
        dy = _dot_t1(dzg, wg_ref[...]) * _gelu_grad(y)
        ub = u_ref[...]
        uf = ub.astype(F32)
        dd_ref[...] += jnp.sum(dy * uf, axis=0, keepdims=True)
        dyb = dy.astype(BF16)
        xrv = xr_ref[...]
        xiv = xi_ref[...]
        dcr_ref[...] += _dot_t0(xrv.astype(BF16), dyb)
        dci_ref[...] += _dot_t0(xiv.astype(BF16), dyb)

        rows = lax.broadcasted_iota(jnp.int32, (t, gp), 0)
        ar = a_ref[0:1, :]
        ai = a_ref[1:2, :]
        cr = carry[0:1, :]
        ci = carry[1:2, :]
        last = rows == t - 1
        gr = _dot_t1(dyb, cr_ref[...]) + jnp.where(last, ar * cr + ai * ci, 0.0)
        gi = _dot_t1(dyb, ci_ref[...]) + jnp.where(last, ar * ci - ai * cr, 0.0)
        for k in range(nlog):
            sh = 1 << k
            pr = pw_ref[2 * k:2 * k + 1, :]
            pi = pw_ref[2 * k + 1:2 * k + 2, :]
            keep = rows < t - sh
            sr = jnp.where(keep, pltpu.roll(gr, t - sh, 0), 0.0)
            si = jnp.where(keep, pltpu.roll(gi, t - sh, 0), 0.0)
            gr, gi = gr + pr * sr + pi * si, gi + pr * si - pi * sr
        first = rows == 0
        carry[0:1, :] = jnp.sum(jnp.where(first, gr, 0.0), axis=0, keepdims=True)
        carry[1:2, :] = jnp.sum(jnp.where(first, gi, 0.0), axis=0, keepdims=True)

        live = jnp.where(tile > 0, 1.0, 0.0)
        xpr = jnp.where(first, hr_ref[7:8, :] * live, pltpu.roll(xrv, 1, 0))
        xpi = jnp.where(first, hi_ref[7:8, :] * live, pltpu.roll(xiv, 1, 0))
        da_ref[0:1, :] += jnp.sum(gr * xpr + gi * xpi, axis=0, keepdims=True)
        da_ref[1:2, :] += jnp.sum(gi * xpr - gr * xpi, axis=0, keepdims=True)

        grb = gr.astype(BF16)
        gib = gi.astype(BF16)
        dbr_ref[...] += _dot_t0(ub, grb)
        dbi_ref[...] += _dot_t0(ub, gib)
        du_ref[...] = (_dot_t1(grb, br_ref[...]) + _dot_t1(gib, bi_ref[...]) + dy * d_ref[...]).astype(BF16)

    const = lambda shape: pl.BlockSpec(shape, lambda b, i: (0, 0))
    row = lambda w: pl.BlockSpec((t, w), lambda b, i: (b * nt + nt - 1 - i, 0))
    halo = pl.BlockSpec((8, gp), lambda b, i: (jnp.maximum((b * nt + nt - 1 - i) * tb - 1, 0), 0))
    extra, extra_specs, dwg_shape, aliases = _slab_out(nl, l, (ds5, 2 * ds5), dwg_buf, 16)
    aliases = {k: 7 for k in aliases}
    return pl.pallas_call(
        body, grid=(bl, nt),
        in_specs=[row(ds5), row(ds5), row(2 * ds5), row(gp), row(gp), halo, halo, row(ds5),
                  pl.BlockSpec((None, ds5, 2 * ds5), lambda b, i: (l, 0, 0)),
                  const((gp, ds5)), const((gp, ds5)), const((ds5, gp)), const((ds5, gp)),
                  const((2, gp)), const((2 * nlog, gp)), const((1, ds5)), *extra_specs],
        out_specs=[row(ds5), const((1, ds5)), const((gp, ds5)), const((gp, ds5)),
                   const((ds5, gp)), const((ds5, gp)), const((2, gp)),
                   pl.BlockSpec((None, ds5, 2 * ds5), lambda b, i: (l, 0, 0))],
        out_shape=[_sds((n, ds5), BF16), _sds((1, ds5), F32), _sds((gp, ds5), F32),
                   _sds((gp, ds5), F32), _sds((ds5, gp), F32), _sds((ds5, gp), F32), _sds((2, gp), F32), dwg_shape],
        input_output_aliases=aliases,
        scratch_shapes=[pltpu.VMEM((2, gp), F32)],
        compiler_params=_cp("arbitrary", "arbitrary"), name=name)(
            ds, yp, zg, xr, xi, xr, xi, proj, wglu, sp["cblk_r"], sp["cblk_in"],
            sp["bblk_r"], sp["bblk_i"], sp["a"], sp["pw"], sp["d"], *extra)


def _head_norm(x, first):
    x2 = x * x
    sa = jnp.sum(jnp.where(first, x2, 0.0), axis=-1, keepdims=True)
    sb = jnp.sum(jnp.where(first, 0.0, x2), axis=-1, keepdims=True)
    r = jnp.where(first, lax.rsqrt(sa * (1.0 / HEAD_DIM) + EPS), lax.rsqrt(sb * (1.0 / HEAD_DIM) + EPS))
    return x * r, r


def _attn_specs(bl, s, datt, qoff):
    nq = s // ATT_TQ
    nb = datt // LANES
    col = lambda blk: (lambda b, h, q: (b * nq + q, qoff + blk * nb + h))
    win = lambda blk, j: (lambda b, h, q: (b * nq + jnp.maximum(q - 2 + j, 0), qoff + blk * nb + h))
    tile = lambda f: pl.BlockSpec((ATT_TQ, LANES), f)
    qs = tile(col(0))
    ks = [tile(win(1, j)) for j in range(3)]
    vs = [tile(win(2, j)) for j in range(3)]
    return nq, nb, qs, ks, vs


def _attn_probs(q_ref, k_refs, gq_ref, gk_ref, bias_ref):
    qt = pl.program_id(2)
    lane = lax.broadcasted_iota(jnp.int32, (1, LANES), 1)
    first = lane < HEAD_DIM
    qh, rq = _head_norm(q_ref[...].astype(F32), first)
    qn = qh * gq_ref[...]
    kc = jnp.concatenate([r[...] for r in k_refs], axis=0).astype(F32)
    kh, _ = _head_norm(kc, first)
    kn = (kh * gk_ref[...]).astype(BF16)
    kpos = (qt - 2) * ATT_TQ + lax.broadcasted_iota(jnp.int32, (1, 3 * ATT_TQ), 1)
    valid = kpos >= 0
    scale = HEAD_DIM ** -0.5
    masks = (first, jnp.logical_not(first))
    qas, ps = [], []
    for hh in range(2):
        qa = jnp.where(masks[hh], qn, 0.0).astype(BF16)
        sc = _dot_t1(qa, kn) * scale + bias_ref[hh]
        sc = jnp.where(valid, sc, NEG)
        e = jnp.exp(sc - jnp.max(sc, axis=-1, keepdims=True))
        ps.append(e / jnp.sum(e, axis=-1, keepdims=True))
        qas.append(qa)
    return first, masks, qh, rq, kn, qas, ps


def _attn_fwd(proj, gq2, gk2, bias, *, bl, s, datt, qoff, name):
    n = bl * s
    nq, nb, qs, ks, vs = _attn_specs(bl, s, datt, qoff)

    def body(q_ref, k0, k1, k2, v0, v1, v2, gq_ref, gk_ref, bias_ref, o_ref):
        first, _, _, _, _, _, ps = _attn_probs(q_ref, (k0, k1, k2), gq_ref, gk_ref, bias_ref)
        vc = jnp.concatenate([v0[...], v1[...], v2[...]], axis=0)
        o0 = _dot(ps[0].astype(BF16), vc)
        o1 = _dot(ps[1].astype(BF16), vc)
        o_ref[...] = jnp.where(first, o0, o1).astype(BF16)

    gs = pl.BlockSpec((1, LANES), lambda b, h, q: (0, 0))
    return pl.pallas_call(
        body, grid=(bl, nb, nq),
        in_specs=[qs, *ks, *vs, gs, gs, pl.BlockSpec((2, ATT_TQ, 3 * ATT_TQ), lambda b, h, q: (h, 0, 0))],
        out_specs=pl.BlockSpec((ATT_TQ, LANES), lambda b, h, q: (b * nq + q, h)),
        out_shape=_sds((n, datt), BF16),
        compiler_params=_cp("parallel", "parallel", "arbitrary"), name=name)(
            proj, proj, proj, proj, proj, proj, proj, gq2, gk2, bias)


def _attn_bwd(do, proj, gq2, gk2, bias, *, bl, s, datt, qoff, name):
    n = bl * s
    nq, nb, qs, ks, vs = _attn_specs(bl, s, datt, qoff)
    srows = s + 2 * ATT_TQ
    scale = HEAD_DIM ** -0.5

    def body(do_ref, q_ref, k0, k1, k2, v0, v1, v2, gq_ref, gk_ref, bias_ref,
             dq_ref, dk_ref, dv_ref, db_ref, dgq_ref):
        qt = pl.program_id(2)

        @pl.when(qt == 0)
        def _():
            dk_ref[...] = jnp.zeros_like(dk_ref)
            dv_ref[...] = jnp.zeros_like(dv_ref)
            db_ref[...] = jnp.zeros_like(db_ref)
            dgq_ref[...] = jnp.zeros_like(dgq_ref)

        first, masks, qh, rq, kn, qas, ps = _attn_probs(q_ref, (k0, k1, k2), gq_ref, gk_ref, bias_ref)
        vc = jnp.concatenate([v0[...], v1[...], v2[...]], axis=0)
        dov = do_ref[...]
        dqn = jnp.zeros((ATT_TQ, LANES), F32)
        dkn = jnp.zeros((3 * ATT_TQ, LANES), F32)
        dv = jnp.zeros((3 * ATT_TQ, LANES), F32)
        for hh in range(2):
            doa = jnp.where(masks[hh], dov, jnp.zeros_like(dov))
            p = ps[hh]
            dp = _dot_t1(doa, vc)
            dsm = p * (dp - jnp.sum(dp * p, axis=-1, keepdims=True))
            db_ref[hh] += dsm
            dsc = (dsm * scale).astype(BF16)
            dqn = dqn + _dot(dsc, jnp.where(masks[hh], kn, jnp.zeros_like(kn)))
            dkn = dkn + _dot_t0(dsc, qas[hh])
            dv = dv + _dot_t0(p.astype(BF16), doa)
        start = pl.multiple_of(qt * ATT_TQ, ATT_TQ)
        dk_ref[pl.ds(start, 3 * ATT_TQ), :] += dkn
        dv_ref[pl.ds(start, 3 * ATT_TQ), :] += dv
        dgq_ref[...] += jnp.sum(dqn * qh, axis=0, keepdims=True)
        dqh = dqn * gq_ref[...]
        t = dqh * qh
        ma = jnp.sum(jnp.where(first, t, 0.0), axis=-1, keepdims=True) * (1.0 / HEAD_DIM)
        mb = jnp.sum(jnp.where(first, 0.0, t), axis=-1, keepdims=True) * (1.0 / HEAD_DIM)
        dq_ref[...] = (rq * (dqh - qh * jnp.where(first, ma, mb))).astype(BF16)

    gs = pl.BlockSpec((1, LANES), lambda b, h, q: (0, 0))
    acc = pl.BlockSpec((None, srows, LANES), lambda b, h, q: (b, 0, h))
    return pl.pallas_call(
        body, grid=(bl, nb, nq),
        in_specs=[pl.BlockSpec((ATT_TQ, LANES), lambda b, h, q: (b * nq + q, h)), qs, *ks, *vs, gs, gs,
                  pl.BlockSpec((2, ATT_TQ, 3 * ATT_TQ), lambda b, h, q: (h, 0, 0))],
        out_specs=[pl.BlockSpec((ATT_TQ, LANES), lambda b, h, q: (b * nq + q, h)), acc, acc,
                   pl.BlockSpec((None, 2, ATT_TQ, 3 * ATT_TQ), lambda b, h, q: (b, h, 0, 0)),
                   pl.BlockSpec((None, None, 1, LANES), lambda b, h, q: (b, h, 0, 0))],
        out_shape=[_sds((n, datt), BF16), _sds((bl, srows, datt), F32), _sds((bl, srows, datt), F32),
                   _sds((bl, 2 * nb, ATT_TQ, 3 * ATT_TQ), F32), _sds((bl, nb, 1, LANES), F32)],
        compiler_params=_cp("arbitrary", "arbitrary", "arbitrary"), name=name)(
            do, proj, proj, proj, proj, proj, proj, proj, gq2, gk2, bias)


def _attn_kv_bwd(dkn, dv, proj, gk2, *, bl, s, datt, tm, koff, name):
    n = bl * s
    ns = s // tm
    off = 2 * ATT_TQ // tm
    nb = datt // LANES

    def body(dkn_ref, dv_ref, k_ref, gk_ref, dk_ref, dvo_ref, dgk_ref):
        lane = lax.broadcasted_iota(jnp.int32, (1, LANES), 1)
        first = lane < HEAD_DIM

        @pl.when((pl.program_id(0) == 0) & (pl.program_id(1) == 0) & (pl.program_id(2) == 0))
        def _():
            dgk_ref[...] = jnp.zeros_like(dgk_ref)

        dvo_ref[...] = dv_ref[...].astype(BF16)
        kh, rk = _head_norm(k_ref[...].astype(F32), first)
        dn = dkn_ref[...]
        dgk_ref[...] += jnp.sum(dn * kh, axis=0, keepdims=True)
        dh = dn * gk_ref[...]
        t = dh * kh
        ma = jnp.sum(jnp.where(first, t, 0.0), axis=-1, keepdims=True) * (1.0 / HEAD_DIM)
        mb = jnp.sum(jnp.where(first, 0.0, t), axis=-1, keepdims=True) * (1.0 / HEAD_DIM)
        dk_ref[...] = (rk * (dh - kh * jnp.where(first, ma, mb))).astype(BF16)

    accs = pl.BlockSpec((None, tm, LANES), lambda b, i, c: (b, i + off, c))
    outs = pl.BlockSpec((tm, LANES), lambda b, i, c: (b * ns + i, c))
    vec = pl.BlockSpec((1, LANES), lambda b, i, c: (0, 0))
    return pl.pallas_call(
        body, grid=(bl, ns, nb),
        in_specs=[accs, accs, pl.BlockSpec((tm, LANES), lambda b, i, c: (b * ns + i, koff + c)), vec],
        out_specs=[outs, outs, vec],
        out_shape=[_sds((n, datt), BF16), _sds((n, datt), BF16), _sds((1, LANES), F32)],
        compiler_params=_cp("arbitrary", "arbitrary", "arbitrary"), name=name)(dkn, dv, proj, gk2)


def _conv_fwd(proj, wdw, bdw, lng, lnb, *, bl, s, t, acol, name):
    n = bl * s
    dc = wdw.shape[1]
    nt = s // t
    hb = t // HALO

    def body(za_ref, zg_ref, ha_ref, hgt_ref, w_ref, b_ref, g_ref, be_ref, hg_ref, hc_ref, o_ref, ext):
        i = pl.program_id(1)
        hg = za_ref[...].astype(F32) * _sigmoid(zg_ref[...].astype(F32))
        live = jnp.where(i > 0, 1.0, 0.0)
        ext[0:HALO, :] = ha_ref[...].astype(F32) * _sigmoid(hgt_ref[...].astype(F32)) * live
        ext[HALO:HALO + t, :] = hg
        hg_ref[...] = hg
        acc = jnp.zeros((t, dc), F32) + b_ref[...]
        for j in range(CONV_W):
            acc = acc + w_ref[j:j + 1, :] * ext[pl.ds(HALO - (CONV_W - 1) + j, t), :]
        hc_ref[...] = acc
        mu = jnp.mean(acc, axis=-1, keepdims=True)
        xc = acc - mu
        rs = lax.rsqrt(jnp.mean(xc * xc, axis=-1, keepdims=True) + EPS)
        ln = xc * rs * g_ref[...] + be_ref[...]
        o_ref[...] = (ln * _sigmoid(ln)).astype(BF16)

    vec = pl.BlockSpec((1, dc), lambda b, i: (0, 0))
    row = pl.BlockSpec((t, dc), lambda b, i: (b * nt + i, 0))
    tile = lambda c: pl.BlockSpec((t, dc), lambda b, i: (b * nt + i, c))
    halo = lambda c: pl.BlockSpec((HALO, dc), lambda b, i: (jnp.maximum((b * nt + i) * hb - 1, 0), c))
    return pl.pallas_call(
        body, grid=(bl, nt),
        in_specs=[tile(acol), tile(acol + 1), halo(acol), halo(acol + 1),
                  pl.BlockSpec((HALO, dc), lambda b, i: (0, 0)), vec, vec, vec],
        out_specs=[row, row, row],
        out_shape=[_sds((n, dc), F32), _sds((n, dc), F32), _sds((n, dc), BF16)],
        scratch_shapes=[pltpu.VMEM((HALO + t, dc), F32)],
        compiler_params=_cp("parallel", "arbitrary"), name=name)(proj, proj, proj, proj, wdw, bdw, lng, lnb)


def _conv_bwd_ln(dco, hc, lng, lnb, *, tm, name):
    n, dc = hc.shape

    def body(d_ref, hc_ref, g_ref, be_ref, dhc_ref, dg_ref, db_ref):
        @pl.when(pl.program_id(0) == 0)
        def _():
            dg_ref[...] = jnp.zeros_like(dg_ref)
            db_ref[...] = jnp.zeros_like(db_ref)

        hcv = hc_ref[...]
        mu = jnp.mean(hcv, axis=-1, keepdims=True)
        xc = hcv - mu
        rs = lax.rsqrt(jnp.mean(xc * xc, axis=-1, keepdims=True) + EPS)
        xh = xc * rs
        ln = xh * g_ref[...] + be_ref[...]
        sg = _sigmoid(ln)
        dln = d_ref[...].astype(F32) * (sg * (1.0 + ln * (1.0 - sg)))
        db_ref[...] += jnp.sum(dln, axis=0, keepdims=True)
        dg_ref[...] += jnp.sum(dln * xh, axis=0, keepdims=True)
        dxh = dln * g_ref[...]
        dhc_ref[...] = rs * (dxh - jnp.mean(dxh, axis=-1, keepdims=True)
                             - xh * jnp.mean(dxh * xh, axis=-1, keepdims=True))

    vec = pl.BlockSpec((1, dc), lambda i: (0, 0))
    row = pl.BlockSpec((tm, dc), lambda i: (i, 0))
    return pl.pallas_call(
        body, grid=(n // tm,), in_specs=[row, row, vec, vec], out_specs=[row, vec, vec],
        out_shape=[_sds((n, dc), F32), _sds((1, dc), F32), _sds((1, dc), F32)],
        compiler_params=_cp("arbitrary"), name=name)(dco, hc, lng, lnb)


def _conv_bwd_dw(dhc, hg, proj, wdw, *, bl, s, t, acol, name):
    n = bl * s
    dc = wdw.shape[1]
    nt = s // t
    hb = t // HALO
    lastblk = n // HALO - 1

    def body(d_ref, dn_ref, hg_ref, hp_ref, za_ref, zg_ref, w_ref, dz_ref, dw_ref, dbias_ref, extd, exth):
        b = pl.program_id(0)
        i = pl.program_id(1)

        @pl.when((b == 0) & (i == 0))
        def _():
            dw_ref[...] = jnp.zeros_like(dw_ref)
            dbias_ref[...] = jnp.zeros_like(dbias_ref)

        dv = d_ref[...]
        extd[0:t, :] = dv
        extd[t:t + HALO, :] = dn_ref[...] * jnp.where(i < nt - 1, 1.0, 0.0)
        exth[0:HALO, :] = hp_ref[...] * jnp.where(i > 0, 1.0, 0.0)
        exth[HALO:HALO + t, :] = hg_ref[...]
        dbias_ref[...] += jnp.sum(dv, axis=0, keepdims=True)
        dhg = jnp.zeros((t, dc), F32)
        for j in range(CONV_W):
            dhg = dhg + w_ref[j:j + 1, :] * extd[pl.ds(CONV_W - 1 - j, t), :]
            dw_ref[j:j + 1, :] += jnp.sum(dv * exth[pl.ds(HALO - (CONV_W - 1) + j, t), :], axis=0, keepdims=True)
        za = za_ref[...].astype(F32)
        sg = _sigmoid(zg_ref[...].astype(F32))
        dz_ref[...] = jnp.concatenate([dhg * sg, dhg * za * sg * (1.0 - sg)], axis=1).astype(BF16)

    row = pl.BlockSpec((t, dc), lambda b, i: (b * nt + i, 0))
    nxt = pl.BlockSpec((HALO, dc), lambda b, i: (jnp.minimum((b * nt + i + 1) * hb, lastblk), 0))
    prv = pl.BlockSpec((HALO, dc), lambda b, i: (jnp.maximum((b * nt + i) * hb - 1, 0), 0))
    wsp = pl.BlockSpec((HALO, dc), lambda b, i: (0, 0))
    tile = lambda c: pl.BlockSpec((t, dc), lambda b, i: (b * nt + i, c))
    return pl.pallas_call(
        body, grid=(bl, nt),
        in_specs=[row, nxt, row, prv, tile(acol), tile(acol + 1), wsp],
        out_specs=[pl.BlockSpec((t, 2 * dc), lambda b, i: (b * nt + i, 0)), wsp,
                   pl.BlockSpec((1, dc), lambda b, i: (0, 0))],
        out_shape=[_sds((n, 2 * dc), BF16), _sds((HALO, dc), F32), _sds((1, dc), F32)],
        scratch_shapes=[pltpu.VMEM((t + HALO, dc), F32), pltpu.VMEM((HALO + t, dc), F32)],
        compiler_params=_cp("arbitrary", "arbitrary"), name=name)(dhc, dhc, hg, hg, proj, proj, wdw)


def _mix_out_fwd(x, brs, gl, bg, wbs, wout, l, *, tm, name):
    n, d = x.shape

    def body(x_ref, s_ref, a_ref, c_ref, g0, g1, g2, bg_ref, ws, wa, wc, wo, o_ref):
        merged = jnp.zeros((tm, d), F32)
        for k, (br, gr, w) in enumerate(((s_ref, g0, ws), (a_ref, g1, wa), (c_ref, g2, wc))):
            gate = _sigmoid(gr[...].astype(F32) + bg_ref[:, k * d:(k + 1) * d])
            merged = merged + gate * _dot(br[...], w[...])
        o_ref[...] = x_ref[...] + _dot(merged.astype(BF16), wo[...])

    row = lambda w: pl.BlockSpec((tm, w), lambda i: (i, 0))
    wsp = lambda a: pl.BlockSpec((None,) + a.shape[1:], lambda i: (l, 0, 0))
    gls = [pl.BlockSpec((tm, d), functools.partial(lambda k, i: (i, k), k)) for k in range(3)]
    return pl.pallas_call(
        body, grid=(n // tm,),
        in_specs=[row(d), *[row(b.shape[1]) for b in brs], *gls, pl.BlockSpec(bg.shape, lambda i: (0, 0)),
                  *[wsp(w) for w in wbs], wsp(wout)],
        out_specs=row(d), out_shape=_sds((n, d), F32),
        compiler_params=_cp("parallel"), name=name)(x, *brs, gl, gl, gl, bg, *wbs, wout)


def _mix_out_bwd(dx, brs, gl, bg, wbs, wout, l, nl, bufs, *, tm, name):
    n, d = dx.shape
    widths = [b.shape[1] for b in brs]

    def body(dx_ref, s_ref, a_ref, c_ref, g0, g1, g2, bg_ref, ws, wa, wc, wo, *rest):
        ds_ref, da_ref, dc_ref, dgl_ref, dbg_ref, dws, dwa, dwc, dwo = rest[-9:]

        @pl.when(pl.program_id(0) == 0)
        def _():
            for r in (dbg_ref, dws, dwa, dwc, dwo):
                r[...] = jnp.zeros_like(r)

        dxb = dx_ref[...].astype(BF16)
        dm = _dot_t1(dxb, wo[...])
        merged = jnp.zeros((tm, d), F32)
        for k, (br, gr, w, dbr, dw) in enumerate(((s_ref, g0, ws, ds_ref, dws), (a_ref, g1, wa, da_ref, dwa),
                                                   (c_ref, g2, wc, dc_ref, dwc))):
            gate = _sigmoid(gr[...].astype(F32) + bg_ref[:, k * d:(k + 1) * d])
            brv = br[...]
            wv = w[...]
            y = _dot(brv, wv)
            merged = merged + gate * y
            dyb = (dm * gate).astype(BF16)
            dbr[...] = _dot_t1(dyb, wv).astype(BF16)
            dw[...] += _dot_t0(brv, dyb)
            dgl = dm * y * gate * (1.0 - gate)
            dgl_ref[:, k * d:(k + 1) * d] = dgl.astype(BF16)
            dbg_ref[:, k * d:(k + 1) * d] += jnp.sum(dgl, axis=0, keepdims=True)
        dwo[...] += _dot_t0(merged.astype(BF16), dxb)

    row = lambda w: pl.BlockSpec((tm, w), lambda i: (i, 0))
    wsp = lambda shape: pl.BlockSpec((None,) + tuple(shape), lambda i: (l, 0, 0))
    gls = [pl.BlockSpec((tm, d), functools.partial(lambda k, i: (i, k), k)) for k in range(3)]
    slabs = [(w, d) for w in widths] + [(d, d)]
    n_in = 12
    extra = [] if bufs is None else list(bufs)
    aliases = {} if bufs is None else {n_in + k: 5 + k for k in range(4)}
    return pl.pallas_call(
        body, grid=(n // tm,),
        in_specs=[row(d), *[row(w) for w in widths], *gls, pl.BlockSpec(bg.shape, lambda i: (0, 0)),
                  *[wsp(w.shape[1:]) for w in wbs], wsp(wout.shape[1:]), *[_ANY for _ in extra]],
        out_specs=[*[row(w) for w in widths], row(3 * d), pl.BlockSpec((1, 3 * d), lambda i: (0, 0)),
                   *[wsp(sh) for sh in slabs]],
        out_shape=[*[_sds((n, w), BF16) for w in widths], _sds((n, 3 * d), BF16), _sds((1, 3 * d), F32),
                   *[_sds((nl,) + sh, F32) for sh in slabs]],
        input_output_aliases=aliases,
        compiler_params=_cp("arbitrary"), name=name)(dx, *brs, gl, gl, gl, bg, *wbs, wout, *extra)


def _adamw(w, g, m, v, *, name):
    r, c = w.shape
    tm = _tile(r, 256)
    c1 = 1.0 - ADAM_B1 ** ADAM_STEP
    c2 = 1.0 - ADAM_B2 ** ADAM_STEP

    def body(w_ref, g_ref, m_ref, v_ref, d_ref, nm_ref, nv_ref):
        gv = g_ref[...]
        mn = ADAM_B1 * m_ref[...] + (1.0 - ADAM_B1) * gv
        vn = ADAM_B2 * v_ref[...] + (1.0 - ADAM_B2) * (gv * gv)
        nm_ref[...] = mn
        nv_ref[...] = vn
        d_ref[...] = -ADAM_LR * ((mn / c1) / (jnp.sqrt(vn / c2) + ADAM_EPS) + ADAM_WD * w_ref[...])

    blk = pl.BlockSpec((tm, c), lambda i: (i, 0))
    return pl.pallas_call(
        body, grid=(r // tm,), in_specs=[blk] * 4, out_specs=[blk] * 3,
        out_shape=[_sds((r, c), F32)] * 3, compiler_params=_cp("parallel"), name=name)(w, g, m, v)


def _add_sibling(g, recv, c_idx, *, name):
    _, a, b = g.shape
    ta = _tile(a, 256)

    def body(c_ref, g_ref, r_ref, o_ref):
        o_ref[...] = (g_ref[...] + r_ref[...]).astype(BF16)

    return pl.pallas_call(
        body,
        grid_spec=pltpu.PrefetchScalarGridSpec(
            num_scalar_prefetch=1, grid=(a // ta,),
            in_specs=[pl.BlockSpec((None, ta, b), lambda i, c_ref: (c_ref[0], i, 0)),
                      pl.BlockSpec((ta, b), lambda i, c_ref: (i, 0))],
            out_specs=pl.BlockSpec((ta, b), lambda i, c_ref: (i, 0))),
        out_shape=_sds((a, b), BF16), compiler_params=_cp("parallel"), name=name)(c_idx, g, recv)


def _add_chips(rsum, parts, axis, s_idx, c_idx, *, name):
    _, a, b = parts.shape
    ta = _tile(a, 256)
    na = a // ta

    def body(s_ref, c_ref, own_ref, p0, p1, p2, p3, o_ref):
        own = own_ref[...].astype(F32)
        terms = [jnp.where(s_ref[0] == s, own, p[...].astype(F32)) for s, p in enumerate((p0, p1, p2, p3))]
        o_ref[...] = ((terms[0] + terms[1]) + terms[2]) + terms[3]

    own_spec = (pl.BlockSpec((ta, b), lambda i, sr, cr: (sr[0] * na + i, 0)) if axis == 1
                else pl.BlockSpec((ta, b), lambda i, sr, cr: (i, sr[0])))
    part_spec = lambda s: pl.BlockSpec((None, ta, b), lambda i, sr, cr: (jnp.where(sr[0] == s, s ^ 1, s), i, 0))
    return pl.pallas_call(
        body,
        grid_spec=pltpu.PrefetchScalarGridSpec(
            num_scalar_prefetch=2, grid=(na,),
            in_specs=[own_spec] + [part_spec(s) for s in range(N_CHIPS)],
            out_specs=pl.BlockSpec((None, ta, b), lambda i, sr, cr: (cr[0], i, 0))),
        out_shape=_sds((2, a, b), F32), compiler_params=_cp("parallel"), name=name)(
            s_idx, c_idx, rsum, parts, parts, parts, parts)


def _place_shard(wloc, axis, s_idx, *, name):
    nl, a, b = wloc.shape
    ta = _tile(a, 256)
    na = a // ta
    full = (nl, a * N_CHIPS, b) if axis == 1 else (nl, a, b * N_CHIPS)

    def body(sc_ref, w_ref, o_ref):
        o_ref[...] = w_ref[...].astype(BF16)

    out_spec = (pl.BlockSpec((None, ta, b), lambda l, i, sc: (l, sc[0] * na + i, 0)) if axis == 1
                else pl.BlockSpec((None, ta, b), lambda l, i, sc: (l, i, sc[0])))
    return pl.pallas_call(
        body,
        grid_spec=pltpu.PrefetchScalarGridSpec(
            num_scalar_prefetch=1, grid=(nl, na),
            in_specs=[pl.BlockSpec((None, ta, b), lambda l, i, sc: (l, i, 0))], out_specs=out_spec),
        out_shape=_sds(full, BF16), compiler_params=_cp("parallel", "parallel"), name=name)(s_idx, wloc)


def _blockdiag(w):
    g, r, c = w.shape
    eye = jnp.eye(g, dtype=w.dtype)
    return (w[:, :, None, :] * eye[:, None, :, None]).reshape(g * r, g * c)


def _s5_prep(lre, lim, log_dt, b_re, b_im, c_re, c_im, d_skip):
    lr = jnp.minimum(lre, -1e-4)
    li = lim
    dt = jnp.exp(log_dt)[:, None]
    mag = jnp.exp(lr * dt)
    ar = mag * jnp.cos(li * dt)
    ai = mag * jnp.sin(li * dt)
    den = lr * lr + li * li
    coef_r = ((ar - 1.0) * lr + ai * li) / den
    coef_i = (ai * lr - (ar - 1.0) * li) / den
    bbar_r = coef_r[..., None] * b_re - coef_i[..., None] * b_im
    bbar_i = coef_r[..., None] * b_im + coef_i[..., None] * b_re
    a = jnp.stack([ar.reshape(-1), ai.reshape(-1)])
    return dict(
        a=a,
        bblk_r=_blockdiag(bbar_r.transpose(0, 2, 1)), bblk_i=_blockdiag(bbar_i.transpose(0, 2, 1)),
        cblk_r=_blockdiag(c_re.transpose(0, 2, 1)), cblk_in=_blockdiag(-c_im.transpose(0, 2, 1)),
        d=d_skip.reshape(1, -1))


def _s5_powers(a, nlog):
    pr, pi = a[0], a[1]
    rows = []
    for _ in range(nlog):
        rows += [pr, pi]
        pr, pi = pr * pr - pi * pi, 2.0 * pr * pi
    return jnp.stack(rows)


def _bias_table(rel_bias):
    h = rel_bias.shape[0]
    tq, tw = ATT_TQ, 3 * ATT_TQ
    n_hi = tw - 1 - MAX_REL + 1
    n_lo = tq + tw - 1 - n_hi - (2 * MAX_REL - 1)
    fr = jnp.concatenate([
        jnp.broadcast_to(rel_bias[:, 2 * MAX_REL:], (h, n_hi)),
        jnp.flip(rel_bias[:, 1:2 * MAX_REL], axis=1),
        jnp.broadcast_to(rel_bias[:, :1], (h, n_lo)),
        jnp.zeros((h, 1), rel_bias.dtype)], axis=1)
    ln = tq + tw
    flat = jnp.broadcast_to(fr[:, None, :], (h, tq, ln)).reshape(h, tq * ln)[:, :tq * (ln - 1)]
    tab = flat.reshape(h, tq, ln - 1)[:, :, tq - 1:tq - 1 + tw]
    qc = np.arange(tq)[:, None] // CHUNK + N_LEFT
    kc = np.arange(tw)[None, :] // CHUNK
    band = (kc <= qc) & (kc >= qc - N_LEFT)
    return jnp.where(jnp.asarray(band)[None], tab, NEG)


def _small_prep(w, l):
    sp = _s5_prep(w["s5_lambda_re"][l], w["s5_lambda_im"][l], w["s5_log_dt"][l], w["s5_b_re"][l], w["s5_b_im"][l],
                  w["s5_c_re"][l], w["s5_c_im"][l], w["s5_d"][l])
    return sp, _bias_table(w["attn_rel_bias"][l])


_PREP_KEYS = ("s5_lambda_re", "s5_lambda_im", "s5_log_dt", "s5_b_re", "s5_b_im", "s5_c_re", "s5_c_im", "s5_d",
              "attn_rel_bias")
_BIG_KEYS = {"ffn1_w_up": 2, "ffn1_w_down": 1, "w_in": 2, "s5_w_glu": 2, "w_br_s5": 2, "w_br_attn": 2,
             "w_br_conv": 2, "w_out": 1, "ffn2_w_up": 2, "ffn2_w_down": 1}
_SMALL_KEYS = ("ffn1_norm", "mix_norm", "b_gate", "s5_lambda_re", "s5_lambda_im", "s5_log_dt", "s5_b_re", "s5_b_im",
               "s5_c_re", "s5_c_im", "s5_d", "attn_q_gain", "attn_k_gain", "attn_rel_bias", "conv_w_dw", "conv_b_dw",
               "conv_ln_g", "conv_ln_b", "ffn2_norm")
_WEIGHTS = ("ffn1_norm", "ffn1_w_up", "ffn1_w_down", "mix_norm", "w_in", "b_gate", "s5_lambda_re", "s5_lambda_im",
            "s5_log_dt", "s5_b_re", "s5_b_im", "s5_c_re", "s5_c_im", "s5_d", "s5_w_glu", "w_br_s5", "attn_q_gain",
            "attn_k_gain", "attn_rel_bias", "w_br_attn", "conv_w_dw", "conv_b_dw", "conv_ln_g", "conv_ln_b",
            "w_br_conv", "w_out", "ffn2_norm", "ffn2_w_up", "ffn2_w_down")


def _local_step(x3, target3, w):
    bl, s, d = x3.shape
    nl = w["ffn1_norm"].shape[0]
    dff = w["ffn1_w_down"].shape[1]
    ds5 = w["s5_d"].shape[1]
    datt = w["w_br_attn"].shape[1]
    dc = w["conv_b_dw"].shape[1]
    n = bl * s
    x = x3.reshape(n, d)
    target = target3.reshape(n, d)
    tm = _tile(n, 512)
    tml = _tile(n, 1024)
    tmix = _tile(n, 256)
    ts5 = 256
    tconv = _tile(s, 512)
    nlog = int(math.log2(ts5))
    tff = dff // 2
    ma = ds5 + 3 * datt + 2 * dc
    tna = ma // 3
    assert (3 * d) % tna == 0 and dff % 2 == 0
    qoff = ds5 // LANES
    koff = (ds5 + datt) // LANES
    acol = (ds5 + 3 * datt) // dc
    wbs = (w["w_br_s5"], w["w_br_attn"], w["w_br_conv"])

    saved = []
    for l in range(nl):
        (sp, bias), prep_vjp = jax.vjp(lambda ww: _small_prep(ww, l), {k: w[k] for k in _PREP_KEYS})
        spb = dict(sp)
        spb["pw"] = _s5_powers(lax.stop_gradient(sp["a"]), nlog)
        for k in ("bblk_r", "bblk_i", "cblk_r", "cblk_in"):
            spb[k] = sp[k].astype(BF16)
        g1 = w["ffn1_norm"][l][None]
        g2 = w["ffn2_norm"][l][None]
        gm = w["mix_norm"][l][None]
        gq2 = jnp.tile(w["attn_q_gain"][l], 2)[None]
        gk2 = jnp.tile(w["attn_k_gain"][l], 2)[None]
        wdw = jnp.pad(w["conv_w_dw"][l], ((0, HALO - CONV_W), (0, 0)))
        bdw, lng, lnb = w["conv_b_dw"][l][None], w["conv_ln_g"][l][None], w["conv_ln_b"][l][None]
        bg = w["b_gate"][l][None]

        x0 = x
        h1, ab1 = _norm_mm(x0, g1, w["ffn1_w_up"], l, tm=tml, tn=tff, ntiles=4, pieces=2, transposed=False,
                           name=f"ffn1_up_{l}")
        x1 = _ffn_down(ab1, w["ffn1_w_down"], l, x0, tm=tm, tk=tff, name=f"ffn1_down_{l}")
        h2, pa = _norm_mm(x1, gm, w["w_in"], l, tm=tml, tn=tna, ntiles=3, pieces=1, transposed=True, name=f"win_a_{l}")
        pa = pa[0]
        gl = _mm_t(h2, w["w_in"], l, tm=tml, tn=tna, off=3, ntiles=3 * d // tna, name=f"win_g_{l}")
        xr, xi, yp, zg, s5o = _s5_fwd(pa, spb, w["s5_w_glu"], l, bl=bl, s=s, t=ts5, name=f"s5_fwd_{l}")
        atto = _attn_fwd(pa, gq2, gk2, bias, bl=bl, s=s, datt=datt, qoff=qoff, name=f"attn_fwd_{l}")
        hg, hc, convo = _conv_fwd(pa, wdw, bdw, lng, lnb, bl=bl, s=s, t=tconv, acol=acol, name=f"conv_fwd_{l}")
        brs = (s5o, atto, convo)
        x2 = _mix_out_fwd(x1, brs, gl, bg, wbs, w["w_out"], l, tm=tmix, name=f"mix_fwd_{l}")
        h3, ab2 = _norm_mm(x2, g2, w["ffn2_w_up"], l, tm=tml, tn=tff, ntiles=4, pieces=2, transposed=False,
                           name=f"ffn2_up_{l}")
        x = _ffn_down(ab2, w["ffn2_w_down"], l, x2, tm=tm, tk=tff, name=f"ffn2_down_{l}")
        saved.append(dict(spb=spb, bias=bias, prep_vjp=prep_vjp, g1=g1, g2=g2, gm=gm, gq2=gq2, gk2=gk2,
                          wdw=wdw, lng=lng, lnb=lnb, bg=bg, x0=x0, h1=h1, ab1=ab1, x1=x1, h2=h2, pa=pa, gl=gl,
                          xr=xr, xi=xi, yp=yp, zg=zg, hg=hg, hc=hc, brs=brs, x2=x2, h3=h3, ab2=ab2))

    dx, lsum = _loss_grad(x, target, tm=tm, name="loss")
    loss_part = 0.5 * jnp.sum(lsum) / d

    big = {k: None for k in _BIG_KEYS}
    small = {k: [None] * nl for k in _SMALL_KEYS}
    for l in reversed(range(nl)):
        sv = saved[l]

        def ffn_bwd(dx, xin, h, ab, g, tag):
            wu, wd = w[tag + "_w_up"], w[tag + "_w_down"]
            dab, act, dob = _ffn_dact(dx, wd, l, ab, tm=tm, tk=tff, name=f"{tag}_dact_{l}")
            big[tag + "_w_down"] = _mm_tn(act[None], dob[None], l, nl, big[tag + "_w_down"], ta=tff, tb=d, tk=tml,
                                          name=f"{tag}_dwd_{l}")
            big[tag + "_w_up"] = _mm_tn(h[None], dab, l, nl, big[tag + "_w_up"], ta=d, tb=tff, tk=tml,
                                        name=f"{tag}_dwu_{l}")
            dxo, dg = _ffn_dx(dab, wu, l, xin, g, dx, tm=tml, tk=tff, name=f"{tag}_dx_{l}")
            small[tag + "_norm"][l] = dg[0]
            return dxo

        dx = ffn_bwd(dx, sv["x2"], sv["h3"], sv["ab2"], sv["g2"], "ffn2")

        mix_keys = ("w_br_s5", "w_br_attn", "w_br_conv", "w_out")
        bufs = None if big["w_out"] is None else [big[k] for k in mix_keys]
        ds5o, datto, dconvo, dgl, dbg, *dws = _mix_out_bwd(
            dx, sv["brs"], sv["gl"], sv["bg"], wbs, w["w_out"], l, nl, bufs, tm=tmix, name=f"mix_bwd_{l}")
        small["b_gate"][l] = dbg[0]
        big.update(zip(mix_keys, dws))

        dhc, dlng, dlnb = _conv_bwd_ln(dconvo, sv["hc"], sv["lng"], sv["lnb"], tm=tm, name=f"conv_bwd_ln_{l}")
        dz, dwdw, dbdw = _conv_bwd_dw(dhc, sv["hg"], sv["pa"], sv["wdw"], bl=bl, s=s, t=tconv, acol=acol,
                                      name=f"conv_bwd_dw_{l}")
        small["conv_w_dw"][l] = dwdw[:CONV_W]
        small["conv_b_dw"][l], small["conv_ln_g"][l], small["conv_ln_b"][l] = dbdw[0], dlng[0], dlnb[0]

        dq, dkn, dvw, dbias, dgq = _attn_bwd(datto, sv["pa"], sv["gq2"], sv["gk2"], sv["bias"], bl=bl, s=s, datt=datt,
                                             qoff=qoff, name=f"attn_bwd_{l}")
        dk, dv, dgk = _attn_kv_bwd(dkn, dvw, sv["pa"], sv["gk2"], bl=bl, s=s, datt=datt, tm=_tile(s, 512), koff=koff,
                                   name=f"attn_kv_bwd_{l}")
        small["attn_q_gain"][l] = jnp.sum(dgq.reshape(-1, HEAD_DIM), axis=0)
        small["attn_k_gain"][l] = jnp.sum(dgk.reshape(-1, HEAD_DIM), axis=0)

        du, dd, dcr, dci, dbr, dbi, da, big["s5_w_glu"] = _s5_bwd(
            ds5o, sv["yp"], sv["zg"], sv["xr"], sv["xi"], sv["pa"], sv["spb"], w["s5_w_glu"], l, nl, big["s5_w_glu"],
            bl=bl, s=s, t=ts5, name=f"s5_bwd_{l}")
        prep_ct = (dict(a=da, bblk_r=dbr, bblk_i=dbi, cblk_r=dcr, cblk_in=dci, d=dd), jnp.sum(dbias, axis=0))
        (dprep,) = sv["prep_vjp"](prep_ct)
        for k in _PREP_KEYS:
            small[k][l] = dprep[k][l]

        dpa = jnp.concatenate([du, dq, dk, dv, dz], axis=1)
        big["w_in"] = _dwin_t(dpa, dgl, sv["h2"], l, nl, big["w_in"], ta=tna, tk=tml, name=f"dwin_{l}")
        dx, dgm = _mix_dx(dpa, dgl, w["w_in"], l, sv["x1"], sv["gm"], dx, tm=tml, tk=tna, name=f"mix_dx_{l}")
        small["mix_norm"][l] = dgm[0]

        dx = ffn_bwd(dx, sv["x0"], sv["h1"], sv["ab1"], sv["g1"], "ffn1")

    small = {k: jnp.stack(v) for k, v in small.items()}
    return loss_part, dx.reshape(bl, s, d), big, small


def _place():
    x, y, c = lax.axis_index("x"), lax.axis_index("y"), lax.axis_index("c")
    chips = [(1 - x, y), (x, 1 - y), (1 - x, 1 - y)]
    return x, y, c, chips


def _remote(src, dst, send_sems, recv_sems, k, dev):
    return pltpu.make_async_remote_copy(src_ref=src, dst_ref=dst, send_sem=send_sems.at[k], recv_sem=recv_sems.at[k],
                                        device_id=dev, device_id_type=MESH)


def _window(ref, lead, s, axis, blk):
    if axis == 1:
        sl = (pl.ds(pl.multiple_of(s * blk, 16), blk), slice(None))
    else:
        sl = (slice(None), pl.ds(pl.multiple_of(s * blk, LANES), blk))
    return ref.at[sl] if lead is None else ref.at[(lead,) + sl]


def _all_gather_weights(fulls, axes, taps):
    nw = len(fulls)
    assert fulls[0].shape[0] == 2
    blks = [f.shape[ax] // N_CHIPS for f, ax in zip(fulls, axes)]

    def body(*refs):
        taps_in = refs[nw]
        outs, taps_out = refs[nw + 1:2 * nw + 1], refs[2 * nw + 1]
        send_sems, recv_sems, local_sem = refs[-3:]
        x, y, c, chips = _place()
        s_me = 2 * x + y
        sibling = (x, y, 1 - c)
        win = lambda i, lyr, s: _window(outs[i], lyr, s, axes[i], blks[i])
        own_taps = pltpu.make_async_copy(taps_in, taps_out.at[s_me], local_sem)
        own_taps.start()
        sends = []
        for i in range(nw):
            for j, (cx, cy) in enumerate(chips):
                mine = win(i, c, s_me)
                sends.append(_remote(mine, mine, send_sems, recv_sems, 6 * i + j, (cx, cy, c)))
        for j, (cx, cy) in enumerate(chips):
            sends.append(_remote(taps_in, taps_out.at[s_me], send_sems, recv_sems, 6 * nw + j, (cx, cy, c)))
        for cp in sends:
            cp.start()
        for i in range(nw):
            for j, (cx, cy) in enumerate(chips):
                piece = win(i, c, 2 * cx + cy)
                _remote(piece, piece, send_sems, recv_sems, 6 * i + j, (cx, cy, c)).wait_recv()
                fw = _remote(piece, piece, send_sems, recv_sems, 6 * i + 3 + j, sibling)
                fw.start()
                sends.append(fw)
        for i in range(nw):
            for j, (cx, cy) in enumerate(chips):
                piece = win(i, 1 - c, 2 * cx + cy)
                _remote(piece, piece, send_sems, recv_sems, 6 * i + 3 + j, sibling).wait_recv()
        for j, (cx, cy) in enumerate(chips):
            slab = taps_out.at[2 * cx + cy]
            _remote(slab, slab, send_sems, recv_sems, 6 * nw + j, (cx, cy, c)).wait_recv()
        for cp in sends:
            cp.wait_send()
        own_taps.wait()

    nsem = 6 * nw + 3
    return pl.pallas_call(
        body, in_specs=[_ANY] * (nw + 1), out_specs=[_ANY] * (nw + 1),
        out_shape=[_sds(f.shape, f.dtype) for f in fulls] + [_sds((N_CHIPS,) + taps.shape, taps.dtype)],
        input_output_aliases={i: i for i in range(nw)},
        scratch_shapes=[pltpu.SemaphoreType.DMA((nsem,)), pltpu.SemaphoreType.DMA((nsem,)), pltpu.SemaphoreType.DMA],
        name="all_gather_weights")(*fulls, taps)


def _rs_swap(grads):
    nw = len(grads)

    def body(*refs):
        ins, outs = refs[:nw], refs[nw:2 * nw]
        send_sems, recv_sems = refs[-2:]
        x, y, c, _ = _place()
        cps = [_remote(ins[i].at[1 - c], outs[i], send_sems, recv_sems, i, (x, y, 1 - c)) for i in range(nw)]
        for cp in cps:
            cp.start()
        for cp in cps:
            cp.wait()

    return pl.pallas_call(
        body, in_specs=[_ANY] * nw, out_specs=[_ANY] * nw, out_shape=[_sds(g.shape[1:], g.dtype) for g in grads],
        scratch_shapes=[pltpu.SemaphoreType.DMA((nw,)), pltpu.SemaphoreType.DMA((nw,))],
        name="rs_swap_layers")(*grads)


def _rs_exchange(rsums, axes):
    nw = len(rsums)
    blks = [r.shape[ax - 1] // N_CHIPS for r, ax in zip(rsums, axes)]
    shard = [tuple(dim // N_CHIPS if i == ax - 1 else dim for i, dim in enumerate(r.shape)) for r, ax in zip(rsums, axes)]

    def body(*refs):
        ins, outs = refs[:nw], refs[nw:2 * nw]
        send_sems, recv_sems = refs[-2:]
        x, y, c, chips = _place()
        s_me = 2 * x + y
        win = lambda i, s: _window(ins[i], None, s, axes[i], blks[i])
        sends = [_remote(win(i, 2 * cx + cy), outs[i].at[s_me], send_sems, recv_sems, 3 * i + j, (cx, cy, c))
                 for i in range(nw) for j, (cx, cy) in enumerate(chips)]
        for cp in sends:
            cp.start()
        for i in range(nw):
            for j, (cx, cy) in enumerate(chips):
                slab = outs[i].at[2 * cx + cy]
                _remote(slab, slab, send_sems, recv_sems, 3 * i + j, (cx, cy, c)).wait_recv()
        for cp in sends:
            cp.wait_send()

    return pl.pallas_call(
        body, in_specs=[_ANY] * nw, out_specs=[_ANY] * nw,
        out_shape=[_sds((N_CHIPS,) + sh, r.dtype) for sh, r in zip(shard, rsums)],
        scratch_shapes=[pltpu.SemaphoreType.DMA((3 * nw,)), pltpu.SemaphoreType.DMA((3 * nw,))],
        name="rs_exchange_chips")(*rsums)


def _rs_join(ts):
    nw = len(ts)

    def body(*refs):
        outs = refs[nw:2 * nw]
        send_sems, recv_sems = refs[-2:]
        x, y, c, _ = _place()
        sends = [_remote(outs[i].at[c], outs[i].at[c], send_sems, recv_sems, i, (x, y, 1 - c)) for i in range(nw)]
        for cp in sends:
            cp.start()
        for i in range(nw):
            slab = outs[i].at[1 - c]
            _remote(slab, slab, send_sems, recv_sems, i, (x, y, 1 - c)).wait_recv()
        for cp in sends:
            cp.wait_send()

    return pl.pallas_call(
        body, in_specs=[_ANY] * nw, out_specs=[_ANY] * nw, out_shape=[_sds(t.shape, t.dtype) for t in ts],
        input_output_aliases={i: i for i in range(nw)},
        scratch_shapes=[pltpu.SemaphoreType.DMA((nw,)), pltpu.SemaphoreType.DMA((nw,))],
        name="rs_join_layers")(*ts)


def _all_reduce_small(buf):
    r, cols = buf.shape
    nd = 8

    def body(b_ref, o_ref, recv, send_sems, recv_sems):
        x, y, c, _ = _place()
        me = 4 * x + 2 * y + c
        recv[0] = b_ref[...]
        cps = []
        for rel in range(1, nd):
            dev = (1 - x if rel & 4 else x, 1 - y if rel & 2 else y, 1 - c if rel & 1 else c)
            cp = _remote(b_ref, recv.at[rel], send_sems, recv_sems, rel - 1, dev)
            cp.start()
            cps.append(cp)
        for rel in range(1, nd):
            _remote(b_ref, recv.at[rel], send_sems, recv_sems, rel - 1, (x, y, c)).wait_recv()
        acc = recv[me]
        for dv in range(1, nd):
            acc = acc + recv[lax.bitwise_xor(me, dv)]
        o_ref[...] = acc
        for cp in cps:
            cp.wait_send()

    vm = pl.BlockSpec(memory_space=pltpu.VMEM)
    return pl.pallas_call(
        body, in_specs=[vm], out_specs=vm, out_shape=_sds((r, cols), F32),
        scratch_shapes=[pltpu.VMEM((nd, r, cols), F32), pltpu.SemaphoreType.DMA((nd - 1,)),
                        pltpu.SemaphoreType.DMA((nd - 1,))],
        compiler_params=pltpu.CompilerParams(vmem_limit_bytes=VMEM_LIMIT), name="all_reduce_small")(buf)


def _pack_rows(parts, row_align):
    flat = jnp.concatenate([p.reshape(-1) for p in parts])
    per = PACK_COLS * row_align
    size = -(-flat.shape[0] // per) * per
    return jnp.pad(flat, (0, size - flat.shape[0])).reshape(-1, PACK_COLS)


def _unpack(flat, shapes):
    out, off = [], 0
    for shp in shapes:
        size = int(np.prod(shp))
        out.append(flat[off:off + size].reshape(shp))
        off += size
    return out


def kernel(x, ffn1_norm, ffn1_w_up, ffn1_w_down, mix_norm, w_in, b_gate, s5_lambda_re, s5_lambda_im, s5_log_dt, s5_b_re, s5_b_im, s5_c_re, s5_c_im, s5_d, s5_w_glu, w_br_s5, attn_q_gain, attn_k_gain, attn_rel_bias, w_br_attn, conv_w_dw, conv_b_dw, conv_ln_g, conv_ln_b, w_br_conv, w_out, ffn2_norm, ffn2_w_up, ffn2_w_down, loss_target, m_ffn1_norm, m_ffn1_w_up, m_ffn1_w_down, m_mix_norm, m_w_in, m_b_gate, m_s5_lambda_re, m_s5_lambda_im, m_s5_log_dt, m_s5_b_re, m_s5_b_im, m_s5_c_re, m_s5_c_im, m_s5_d, m_s5_w_glu, m_w_br_s5, m_attn_q_gain, m_attn_k_gain, m_attn_rel_bias, m_w_br_attn, m_conv_w_dw, m_conv_b_dw, m_conv_ln_g, m_conv_ln_b, m_w_br_conv, m_w_out, m_ffn2_norm, m_ffn2_w_up, m_ffn2_w_down, v_ffn1_norm, v_ffn1_w_up, v_ffn1_w_down, v_mix_norm, v_w_in, v_b_gate, v_s5_lambda_re, v_s5_lambda_im, v_s5_log_dt, v_s5_b_re, v_s5_b_im, v_s5_c_re, v_s5_c_im, v_s5_d, v_s5_w_glu, v_w_br_s5, v_attn_q_gain, v_attn_k_gain, v_attn_rel_bias, v_w_br_attn, v_conv_w_dw, v_conv_b_dw, v_conv_ln_g, v_conv_ln_b, v_w_br_conv, v_w_out, v_ffn2_norm, v_ffn2_w_up, v_ffn2_w_down):
    a = dict(locals())
    xi, yi, ci = lax.axis_index("x"), lax.axis_index("y"), lax.axis_index("c")
    s_me = 2 * xi + yi
    big_keys = list(_BIG_KEYS)
    axes = [1 if k == "w_in" else _BIG_KEYS[k] for k in big_keys]

    s_idx = s_me.astype(jnp.int32).reshape(1)
    c_idx = ci.astype(jnp.int32).reshape(1)
    placed = [_place_shard(jnp.swapaxes(a[k], 1, 2).astype(BF16) if k == "w_in" else a[k], ax, s_idx,
                           name=f"place_{k}") for k, ax in zip(big_keys, axes)]
    *fulls, taps = _all_gather_weights(placed, axes, a["conv_w_dw"])
    w = {k: a[k] for k in _WEIGHTS}
    w.update(zip(big_keys, fulls))
    w["conv_w_dw"] = jnp.moveaxis(taps, 0, 2).reshape(taps.shape[1], taps.shape[2], -1)

    loss_part, grad_x, gbig, gsmall = _local_step(a["x"], a["loss_target"], w)
    loss = lax.psum(loss_part, ("x", "y", "c"))

    glist = [gbig[k] for k in big_keys]
    recv = _rs_swap(glist)
    rsums = [_add_sibling(g, r, c_idx, name=f"rs_add_sibling_{k}") for g, r, k in zip(glist, recv, big_keys)]
    parts = _rs_exchange(rsums, axes)
    mine = [_add_chips(r, p, ax, s_idx, c_idx, name=f"rs_add_chips_{k}")
            for r, p, ax, k in zip(rsums, parts, axes, big_keys)]
    gb = dict(zip(big_keys, _rs_join(mine)))
    gb["w_in"] = jnp.swapaxes(gb["w_in"], 1, 2)

    small_keys = list(_SMALL_KEYS)
    sred = _all_reduce_small(_pack_rows([gsmall[k] for k in small_keys], 8)).reshape(-1)
    gs = dict(zip(small_keys, _unpack(sred, [gsmall[k].shape for k in small_keys])))
    blk = a["conv_w_dw"].shape[2]
    gs["conv_w_dw"] = lax.dynamic_slice_in_dim(gs["conv_w_dw"], s_me * blk, blk, axis=2)
    grads = {**gb, **gs}

    delta, new_m, new_v = {}, {}, {}
    for k in big_keys:
        shp = a[k].shape
        two_d = lambda t: t.reshape(-1, shp[-1])
        d_, m_, v_ = _adamw(two_d(a[k]), two_d(grads[k]), two_d(a["m_" + k]), two_d(a["v_" + k]), name=f"adamw_{k}")
        delta[k], new_m[k], new_v[k] = d_.reshape(shp), m_.reshape(shp), v_.reshape(shp)
    sm_shapes = [a[k].shape for k in small_keys]
    packs = [_pack_rows([src[k] for k in small_keys], 8)
             for src in (a, grads, {k: a["m_" + k] for k in small_keys}, {k: a["v_" + k] for k in small_keys})]
    d_, m_, v_ = _adamw(*packs, name="adamw_small")
    for dst, res in ((delta, d_), (new_m, m_), (new_v, v_)):
        dst.update(zip(small_keys, _unpack(res.reshape(-1), sm_shapes)))

    return (loss, grad_x, *[grads[k] for k in _WEIGHTS], *[delta[k] for k in _WEIGHTS],
            *[new_m[k] for k in _WEIGHTS], *[new_v[k] for k in _WEIGHTS])
```

```python
import functools
import math

import numpy as np
import jax
import jax.numpy as jnp
from jax import lax
from jax.experimental import pallas as pl
from jax.experimental.pallas import tpu as pltpu

F32 = jnp.float32
BF16 = jnp.bfloat16
EPS = 1e-6
VMEM_LIMIT = 56 * 1024 * 1024
LANES = 128
HEAD_DIM = 64
CHUNK = 64
N_LEFT = 8
MAX_REL = 128
ATT_TQ = 256
CONV_W = 31
HALO = 32
ROW_CHUNK = 256
NEG = -1e30
N_CHIPS = 4
PACK_COLS = 1024

ADAM_LR = 0.001
ADAM_B1 = 0.9
ADAM_B2 = 0.999
ADAM_EPS = 1e-08
ADAM_WD = 0.01
ADAM_STEP = 10

MESH = pl.DeviceIdType.MESH
_ANY = pl.BlockSpec(memory_space=pl.ANY)


def _cp(*sem):
    return pltpu.CompilerParams(dimension_semantics=sem, vmem_limit_bytes=VMEM_LIMIT)


def _sds(shape, dtype):
    return jax.ShapeDtypeStruct(shape, dtype)


def _tile(n, pref):
    t = min(n, pref)
    while n % t:
        t -= 8
    return t


def _sigmoid(x):
    return jax.nn.sigmoid(x)


_GELU_C = math.sqrt(2.0 / math.pi)


def _gelu(y):
    return 0.5 * y * (1.0 + jnp.tanh(_GELU_C * (y + 0.044715 * y * y * y)))


def _gelu_grad(y):
    th = jnp.tanh(_GELU_C * (y + 0.044715 * y * y * y))
    return 0.5 * (1.0 + th) + 0.5 * y * (1.0 - th * th) * _GELU_C * (1.0 + 3.0 * 0.044715 * y * y)


def _dot(a, b):
    return jnp.dot(a, b, preferred_element_type=F32)


def _dot_t0(a, b):
    return lax.dot_general(a, b, (((0,), (0,)), ((), ())), preferred_element_type=F32)


def _dot_t1(a, b):
    return lax.dot_general(a, b, (((1,), (1,)), ((), ())), preferred_element_type=F32)


def _slab_out(nl, l, shape, buf, n_in):
    sds = _sds((nl,) + tuple(shape), F32)
    if buf is None:
        return [], [], sds, {}
    return [buf], [_ANY], sds, {n_in: 0}


def _norm_mm(x, g, w, l, *, tm, tn, ntiles, pieces, transposed, name):
    n, d = x.shape
    m = ntiles * tn
    mp = m // pieces
    npj = mp // tn

    def body(x_ref, g_ref, w_ref, h_ref, y_ref, h_scr):
        @pl.when(pl.program_id(1) == 0)
        def _():
            for r0 in range(0, tm, ROW_CHUNK):
                rows = slice(r0, r0 + ROW_CHUNK)
                xv = x_ref[rows, :]
                r = lax.rsqrt(jnp.mean(xv * xv, axis=-1, keepdims=True) + EPS)
                hb = (xv * r * g_ref[...]).astype(BF16)
                h_scr[rows, :] = hb
                h_ref[rows, :] = hb

        mm = _dot_t1 if transposed else _dot
        y_ref[...] = mm(h_scr[...], w_ref[...]).astype(BF16)

    wspec = (pl.BlockSpec((None, tn, d), lambda i, j: (l, j, 0)) if transposed
             else pl.BlockSpec((None, d, tn), lambda i, j: (l, 0, j)))
    return pl.pallas_call(
        body, grid=(n // tm, ntiles),
        in_specs=[pl.BlockSpec((tm, d), lambda i, j: (i, 0)), pl.BlockSpec((1, d), lambda i, j: (0, 0)), wspec],
        out_specs=[pl.BlockSpec((tm, d), lambda i, j: (i, 0)),
                   pl.BlockSpec((None, tm, tn), lambda i, j: (j // npj, i, j % npj))],
        out_shape=[_sds((n, d), BF16), _sds((pieces, n, mp), BF16)],
        scratch_shapes=[pltpu.VMEM((tm, d), BF16)],
        compiler_params=_cp("parallel", "arbitrary"), name=name)(x, g, w)


def _mm_t(a, w, l, *, tm, tn, off, ntiles, name):
    n, k = a.shape

    def body(a_ref, w_ref, y_ref):
        y_ref[...] = _dot_t1(a_ref[...], w_ref[...]).astype(BF16)

    return pl.pallas_call(
        body, grid=(n // tm, ntiles),
        in_specs=[pl.BlockSpec((tm, k), lambda i, j: (i, 0)), pl.BlockSpec((None, tn, k), lambda i, j: (l, off + j, 0))],
        out_specs=pl.BlockSpec((tm, tn), lambda i, j: (i, j)),
        out_shape=_sds((n, ntiles * tn), BF16),
        compiler_params=_cp("parallel", "arbitrary"), name=name)(a, w)


def _ffn_down(ab, wd, l, x, *, tm, tk, name):
    _, n, dff = ab.shape
    d = x.shape[1]
    nk = dff // tk

    def body(a_ref, b_ref, wd_ref, x_ref, o_ref, acc):
        k = pl.program_id(1)
        a = a_ref[...].astype(F32)
        b = b_ref[...].astype(F32)
        act = (a * _sigmoid(a) * b).astype(BF16)
        part = _dot(act, wd_ref[pl.ds(pl.multiple_of(k * tk, tk), tk), :])

        @pl.when(k == 0)
        def _():
            acc[...] = part

        @pl.when(k > 0)
        def _():
            acc[...] += part

        @pl.when(k == nk - 1)
        def _():
            o_ref[...] = x_ref[...] + 0.5 * acc[...]

    return pl.pallas_call(
        body, grid=(n // tm, nk),
        in_specs=[pl.BlockSpec((None, tm, tk), lambda i, k: (0, i, k)),
                  pl.BlockSpec((None, tm, tk), lambda i, k: (1, i, k)),
                  pl.BlockSpec((None, dff, d), lambda i, k: (l, 0, 0)),
                  pl.BlockSpec((tm, d), lambda i, k: (i, 0))],
        out_specs=pl.BlockSpec((tm, d), lambda i, k: (i, 0)),
        out_shape=_sds((n, d), F32),
        scratch_shapes=[pltpu.VMEM((tm, d), F32)],
        compiler_params=_cp("parallel", "arbitrary"), name=name)(ab, ab, wd, x)


def _ffn_dact(dx, wd, l, ab, nl, dwd_buf, *, tm, tk, name):
    n, d = dx.shape
    dff = ab.shape[2]
    half = ((tk // LANES + 1) // 2) * LANES
    chunks = ((0, half), (half, tk))

    def body(dx_ref, wd_ref, a_ref, b_ref, *rest):
        dab_ref, dwd_ref = rest[-2:]
        do = (0.5 * dx_ref[...]).astype(BF16)

        @pl.when(pl.program_id(1) == 0)
        def _():
            dwd_ref[...] = jnp.zeros_like(dwd_ref)

        for c0, c1 in chunks:
            dact = _dot_t1(do, wd_ref[c0:c1, :])
            a = a_ref[:, c0:c1].astype(F32)
            b = b_ref[:, c0:c1].astype(F32)
            sg = _sigmoid(a)
            silu = a * sg
            dab_ref[0, :, c0:c1] = (dact * b * (sg * (1.0 + a * (1.0 - sg)))).astype(BF16)
            dab_ref[1, :, c0:c1] = (dact * silu).astype(BF16)
            dwd_ref[c0:c1, :] += _dot_t0((silu * b).astype(BF16), do)

    extra, extra_specs, dwd_shape, aliases = _slab_out(nl, l, (dff, d), dwd_buf, 4)
    aliases = {k: 1 for k in aliases}
    return pl.pallas_call(
        body, grid=(dff // tk, n // tm),
        in_specs=[pl.BlockSpec((tm, d), lambda j, i: (i, 0)),
                  pl.BlockSpec((None, tk, d), lambda j, i: (l, j, 0)),
                  pl.BlockSpec((None, tm, tk), lambda j, i: (0, i, j)),
                  pl.BlockSpec((None, tm, tk), lambda j, i: (1, i, j)), *extra_specs],
        out_specs=[pl.BlockSpec((2, tm, tk), lambda j, i: (0, i, j)),
                   pl.BlockSpec((None, tk, d), lambda j, i: (l, j, 0))],
        out_shape=[_sds((2, n, dff), BF16), dwd_shape],
        input_output_aliases=aliases,
        compiler_params=_cp("arbitrary", "arbitrary"), name=name)(dx, wd, ab, ab, *extra)


def _rms_bwd_epilogue(acc, x_ref, g_ref, dres_ref, dx_ref, dg_ref, i):
    dgp = jnp.zeros(dg_ref.shape, F32)
    for r0 in range(0, acc.shape[0], ROW_CHUNK):
        rows = slice(r0, r0 + ROW_CHUNK)
        dh = acc[rows, :]
        xv = x_ref[rows, :]
        r = lax.rsqrt(jnp.mean(xv * xv, axis=-1, keepdims=True) + EPS)
        xn = xv * r
        dgp = dgp + jnp.sum(dh * xn, axis=0, keepdims=True)
        dxh = dh * g_ref[...]
        dx_ref[rows, :] = dres_ref[rows, :] + r * (dxh - xn * jnp.mean(dxh * xn, axis=-1, keepdims=True))

    @pl.when(i == 0)
    def _():
        dg_ref[...] = dgp

    @pl.when(i > 0)
    def _():
        dg_ref[...] += dgp


def _ffn_dx(dab, wu, l, x, g, dres, *, tm, tk, name):
    p, n, mp = dab.shape
    d = x.shape[1]
    nkp = mp // tk
    nk = p * nkp

    def body(dy_ref, w_ref, x_ref, g_ref, dres_ref, dx_ref, dg_ref, acc):
        k = pl.program_id(1)
        part = _dot_t1(dy_ref[...], w_ref[...])

        @pl.when(k == 0)
        def _():
            acc[...] = part

        @pl.when(k > 0)
        def _():
            acc[...] += part

        @pl.when(k == nk - 1)
        def _():
            _rms_bwd_epilogue(acc, x_ref, g_ref, dres_ref, dx_ref, dg_ref, pl.program_id(0))

    return pl.pallas_call(
        body, grid=(n // tm, nk),
        in_specs=[pl.BlockSpec((None, tm, tk), lambda i, k: (k // nkp, i, k % nkp)),
                  pl.BlockSpec((None, d, tk), lambda i, k: (l, 0, k)),
                  pl.BlockSpec((tm, d), lambda i, k: (i, 0)),
                  pl.BlockSpec((1, d), lambda i, k: (0, 0)),
                  pl.BlockSpec((tm, d), lambda i, k: (i, 0))],
        out_specs=[pl.BlockSpec((tm, d), lambda i, k: (i, 0)), pl.BlockSpec((1, d), lambda i, k: (0, 0))],
        out_shape=[_sds((n, d), F32), _sds((1, d), F32)],
        scratch_shapes=[pltpu.VMEM((tm, d), F32)],
        compiler_params=_cp("arbitrary", "arbitrary"), name=name)(dab, wu, x, g, dres)


def _mix_dx(dpa, dgl, wt, l, x, g, dres, *, tm, tk, name):
    n, d = x.shape
    n1 = dpa.shape[1] // tk
    n2 = dgl.shape[1] // tk
    nk = n1 + n2

    def body(d1_ref, d2_ref, w_ref, x_ref, g_ref, dres_ref, dx_ref, dg_ref, acc):
        k = pl.program_id(1)

        @pl.when(k == 0)
        def _():
            acc[...] = _dot(d1_ref[...], w_ref[...])

        @pl.when((k > 0) & (k < n1))
        def _():
            acc[...] += _dot(d1_ref[...], w_ref[...])

        @pl.when(k >= n1)
        def _():
            acc[...] += _dot(d2_ref[...], w_ref[...])

        @pl.when(k == nk - 1)
        def _():
            _rms_bwd_epilogue(acc, x_ref, g_ref, dres_ref, dx_ref, dg_ref, pl.program_id(0))

    return pl.pallas_call(
        body, grid=(n // tm, nk),
        in_specs=[pl.BlockSpec((tm, tk), lambda i, k: (i, jnp.minimum(k, n1 - 1))),
                  pl.BlockSpec((tm, tk), lambda i, k: (i, jnp.maximum(k - n1, 0))),
                  pl.BlockSpec((None, tk, d), lambda i, k: (l, k, 0)),
                  pl.BlockSpec((tm, d), lambda i, k: (i, 0)),
                  pl.BlockSpec((1, d), lambda i, k: (0, 0)),
                  pl.BlockSpec((tm, d), lambda i, k: (i, 0))],
        out_specs=[pl.BlockSpec((tm, d), lambda i, k: (i, 0)), pl.BlockSpec((1, d), lambda i, k: (0, 0))],
        out_shape=[_sds((n, d), F32), _sds((1, d), F32)],
        scratch_shapes=[pltpu.VMEM((tm, d), F32)],
        compiler_params=_cp("arbitrary", "arbitrary"), name=name)(dpa, dgl, wt, x, g, dres)


def _mm_tn(a, b, l, nl, buf, *, ta, tb, tk, name):
    pa, n, ka = a.shape
    pb, _, kb = b.shape
    nap = ka // ta
    nbp = kb // tb

    def body(a_ref, b_ref, *rest):
        o_ref = rest[-1]

        @pl.when(pl.program_id(2) == 0)
        def _():
            o_ref[...] = jnp.zeros_like(o_ref)

        o_ref[...] += _dot_t0(a_ref[...], b_ref[...])

    extra, extra_specs, out_shape, aliases = _slab_out(nl, l, (pa * ka, pb * kb), buf, 2)
    return pl.pallas_call(
        body, grid=(pa * nap, pb * nbp, n // tk),
        in_specs=[pl.BlockSpec((None, tk, ta), lambda i, j, k: (i // nap, k, i % nap)),
                  pl.BlockSpec((None, tk, tb), lambda i, j, k: (j // nbp, k, j % nbp)), *extra_specs],
        out_specs=pl.BlockSpec((None, ta, tb), lambda i, j, k: (l, i, j)),
        out_shape=out_shape, input_output_aliases=aliases,
        compiler_params=_cp("parallel", "parallel", "arbitrary"), name=name)(a, b, *extra)


def _dwin_t(dpa, dgl, h, l, nl, buf, *, ta, tk, name):
    n, d = h.shape
    n1 = dpa.shape[1] // ta
    n2 = dgl.shape[1] // ta

    def body(a1_ref, a2_ref, h_ref, *rest):
        o_ref = rest[-1]
        i = pl.program_id(0)

        @pl.when(pl.program_id(1) == 0)
        def _():
            o_ref[...] = jnp.zeros_like(o_ref)

        @pl.when(i < n1)
        def _():
            o_ref[...] += _dot_t0(a1_ref[...], h_ref[...])

        @pl.when(i >= n1)
        def _():
            o_ref[...] += _dot_t0(a2_ref[...], h_ref[...])

    extra, extra_specs, out_shape, aliases = _slab_out(nl, l, ((n1 + n2) * ta, d), buf, 3)
    return pl.pallas_call(
        body, grid=(n1 + n2, n // tk),
        in_specs=[pl.BlockSpec((tk, ta), lambda i, k: (jnp.where(i < n1, k, 0), jnp.minimum(i, n1 - 1))),
                  pl.BlockSpec((tk, ta), lambda i, k: (jnp.where(i >= n1, k, 0), jnp.maximum(i - n1, 0))),
                  pl.BlockSpec((tk, d), lambda i, k: (k, 0)), *extra_specs],
        out_specs=pl.BlockSpec((None, ta, d), lambda i, k: (l, i, 0)),
        out_shape=out_shape, input_output_aliases=aliases,
        compiler_params=_cp("parallel", "arbitrary"), name=name)(dpa, dgl, h, *extra)


def _loss_grad(y, t, *, tm, name):
    n, d = y.shape

    def body(y_ref, t_ref, dy_ref, l_ref):
        e = y_ref[...] - t_ref[...]
        dy_ref[...] = e * (1.0 / d)
        part = jnp.sum(e * e, axis=0, keepdims=True)

        @pl.when(pl.program_id(0) == 0)
        def _():
            l_ref[...] = part

        @pl.when(pl.program_id(0) > 0)
        def _():
            l_ref[...] += part

    return pl.pallas_call(
        body, grid=(n // tm,),
        in_specs=[pl.BlockSpec((tm, d), lambda i: (i, 0)), pl.BlockSpec((tm, d), lambda i: (i, 0))],
        out_specs=[pl.BlockSpec((tm, d), lambda i: (i, 0)), pl.BlockSpec((1, d), lambda i: (0, 0))],
        out_shape=[_sds((n, d), F32), _sds((1, d), F32)],
        compiler_params=_cp("arbitrary"), name=name)(y, t)


def _s5_fwd(proj, sp, wglu, l, *, bl, s, t, name):
    n = bl * s
    ds5, gp = sp["bblk_r"].shape
    nt = s // t
    nlog = int(math.log2(t))

    def body(u_ref, br_ref, bi_ref, a_ref, pw_ref, cr_ref, ci_ref, d_ref, wg_ref,
             xr_ref, xi_ref, yp_ref, zg_ref, o_ref, carry):
        @pl.when(pl.program_id(1) == 0)
        def _():
            carry[...] = jnp.zeros_like(carry)

        u = u_ref[...]
        rows = lax.broadcasted_iota(jnp.int32, (t, gp), 0)
        ar = a_ref[0:1, :]
        ai = a_ref[1:2, :]
        cr = carry[0:1, :]
        ci = carry[1:2, :]
        first = rows == 0
        xr = _dot(u, br_ref[...]) + jnp.where(first, ar * cr - ai * ci, 0.0)
        xi = _dot(u, bi_ref[...]) + jnp.where(first, ar * ci + ai * cr, 0.0)
        for k in range(nlog):
            sh = 1 << k
            pr = pw_ref[2 * k:2 * k + 1, :]
            pi = pw_ref[2 * k + 1:2 * k + 2, :]
            keep = rows >= sh
            sr = jnp.where(keep, pltpu.roll(xr, sh, 0), 0.0)
            si = jnp.where(keep, pltpu.roll(xi, sh, 0), 0.0)
            xr, xi = xr + pr * sr - pi * si, xi + pr * si + pi * sr
        last = rows == t - 1
        carry[0:1, :] = jnp.sum(jnp.where(last, xr, 0.0), axis=0, keepdims=True)
        carry[1:2, :] = jnp.sum(jnp.where(last, xi, 0.0), axis=0, keepdims=True)
        xr_ref[...] = xr
        xi_ref[...] = xi
        y = _dot(xr.astype(BF16), cr_ref[...]) + _dot(xi.astype(BF16), ci_ref[...]) + d_ref[...] * u.astype(F32)
        yp_ref[...] = y
        zg = _dot(_gelu(y).astype(BF16), wg_ref[...])
        zg_ref[...] = zg
        o_ref[...] = (zg[:, :ds5] * _sigmoid(zg[:, ds5:])).astype(BF16)

    const = lambda shape: pl.BlockSpec(shape, lambda b, i: (0, 0))
    row = lambda w: pl.BlockSpec((t, w), lambda b, i: (b * nt + i, 0))
    return pl.pallas_call(
        body, grid=(bl, nt),
        in_specs=[row(ds5), const((ds5, gp)), const((ds5, gp)), const((2, gp)), const((2 * nlog, gp)),
                  const((gp, ds5)), const((gp, ds5)), const((1, ds5)),
                  pl.BlockSpec((None, ds5, 2 * ds5), lambda b, i: (l, 0, 0))],
        out_specs=[row(gp), row(gp), row(ds5), row(2 * ds5), row(ds5)],
        out_shape=[_sds((n, gp), F32), _sds((n, gp), F32), _sds((n, ds5), F32), _sds((n, 2 * ds5), F32),
                   _sds((n, ds5), BF16)],
        scratch_shapes=[pltpu.VMEM((2, gp), F32)],
        compiler_params=_cp("arbitrary", "arbitrary"), name=name)(
            proj, sp["bblk_r"], sp["bblk_i"], sp["a"], sp["pw"], sp["cblk_r"], sp["cblk_in"], sp["d"], wglu)


def _s5_bwd(ds, yp, zg, xr, xi, proj, sp, wglu, l, nl, dwg_buf, *, bl, s, t, name):
    n = bl * s
    ds5, gp = sp["bblk_r"].shape
    nt = s // t
    nlog = int(math.log2(t))
    tb = t // 8

    def body(ds_ref, yp_ref, zg_ref, xr_ref, xi_ref, hr_ref, hi_ref, u_ref, wg_ref, cr_ref, ci_ref,
             br_ref, bi_ref, a_ref, pw_ref, d_ref, *rest):
        du_ref, dd_ref, dcr_ref, dci_ref, dbr_ref, dbi_ref, da_ref, dwg_ref, carry = rest[-9:]
        b = pl.program_id(0)
        i = pl.program_id(1)
        tile = nt - 1 - i

        @pl.when((b == 0) & (i == 0))
        def _():
            for r in (dwg_ref, dd_ref, dcr_ref, dci_ref, dbr_ref, dbi_ref, da_ref):
                r[...] = jnp.zeros_like(r)

        @pl.when(i == 0)
        def _():
            carry[...] = jnp.zeros_like(carry)

        dsv = ds_ref[...].astype(F32)
        zgv = zg_ref[...]
        za = zgv[:, :ds5]
        sg = _sigmoid(zgv[:, ds5:])
        dzg = jnp.concatenate([dsv * sg, dsv * za * sg * (1.0 - sg)], axis=1).astype(BF16)
        y = yp_ref[...]
        dwg_ref[...] += _dot_t0(_gelu(y).astype(BF16), dzg)
        dy = _dot_t1(dzg, wg_ref[...]) * _gelu_grad(y)
        ub = u_ref[...]
        uf = ub.astype(F32)
        dd_ref[...] += jnp.sum(dy * uf, axis=0, keepdims=True)
        dyb = dy.astype(BF16)
        xrv = xr_ref[...]
        xiv = xi_ref[...]
        dcr_ref[...] += _dot_t0(xrv.astype(BF16), dyb)
        dci_ref[...] += _dot_t0(xiv.astype(BF16), dyb)

        rows = lax.broadcasted_iota(jnp.int32, (t, gp), 0)
        ar = a_ref[0:1, :]
        ai = a_ref[1:2, :]
        cr = carry[0:1, :]
        ci = carry[1:2, :]
        last = rows == t - 1
        gr = _dot_t1(dyb, cr_ref[...]) + jnp.where(last, ar * cr + ai * ci, 0.0)
        gi = _dot_t1(dyb, ci_ref[...]) + jnp.where(last, ar * ci - ai * cr, 0.0)
        for k in range(nlog):
            sh = 1 << k
            pr = pw_ref[2 * k:2 * k + 1, :]
            pi = pw_ref[2 * k + 1:2 * k + 2, :]
            keep = rows < t - sh
            sr = jnp.where(keep, pltpu.roll(gr, t - sh, 0), 0.0)
            si = jnp.where(keep, pltpu.roll(gi, t - sh, 0), 0.0)
            gr, gi = gr + pr * sr + pi * si, gi + pr * si - pi * sr
        first = rows == 0
        carry[0:1, :] = jnp.sum(jnp.where(first, gr, 0.0), axis=0, keepdims=True)
        carry[1:2, :] = jnp.sum(jnp.where(first, gi, 0.0), axis=0, keepdims=True)

        live = jnp.where(tile > 0, 1.0, 0.0)
        xpr = jnp.where(first, hr_ref[7:8, :] * live, pltpu.roll(xrv, 1, 0))
        xpi = jnp.where(first, hi_ref[7:8, :] * live, pltpu.roll(xiv, 1, 0))
        da_ref[0:1, :] += jnp.sum(gr * xpr + gi * xpi, axis=0, keepdims=True)
        da_ref[1:2, :] += jnp.sum(gi * xpr - gr * xpi, axis=0, keepdims=True)

        grb = gr.astype(BF16)
        gib = gi.astype(BF16)
        dbr_ref[...] += _dot_t0(ub, grb)
        dbi_ref[...] += _dot_t0(ub, gib)
        du_ref[...] = (_dot_t1(grb, br_ref[...]) + _dot_t1(gib, bi_ref[...]) + dy * d_ref[...]).astype(BF16)

    const = lambda shape: pl.BlockSpec(shape, lambda b, i: (0, 0))
    row = lambda w: pl.BlockSpec((t, w), lambda b, i: (b * nt + nt - 1 - i, 0))
    halo = pl.BlockSpec((8, gp), lambda b, i: (jnp.maximum((b * nt + nt - 1 - i) * tb - 1, 0), 0))
    extra, extra_specs, dwg_shape, aliases = _slab_out(nl, l, (ds5, 2 * ds5), dwg_buf, 16)
    aliases = {k: 7 for k in aliases}
    return pl.pallas_call(
        body, grid=(bl, nt),
        in_specs=[row(ds5), row(ds5), row(2 * ds5), row(gp), row(gp), halo, halo, row(ds5),
                  pl.BlockSpec((None, ds5, 2 * ds5), lambda b, i: (l, 0, 0)),
                  const((gp, ds5)), const((gp, ds5)), const((ds5, gp)), const((ds5, gp)),
                  const((2, gp)), const((2 * nlog, gp)), const((1, ds5)), *extra_specs],
        out_specs=[row(ds5), const((1, ds5)), const((gp, ds5)), const((gp, ds5)),
                   const((ds5, gp)), const((ds5, gp)), const((2, gp)),
                   pl.BlockSpec((None, ds5, 2 * ds5), lambda b, i: (l, 0, 0))],
        out_shape=[_sds((n, ds5), BF16), _sds((1, ds5), F32), _sds((gp, ds5), F32),
                   _sds((gp, ds5), F32), _sds((ds5, gp), F32), _sds((ds5, gp), F32), _sds((2, gp), F32), dwg_shape],
        input_output_aliases=aliases,
        scratch_shapes=[pltpu.VMEM((2, gp), F32)],
        compiler_params=_cp("arbitrary", "arbitrary"), name=name)(
            ds, yp, zg, xr, xi, xr, xi, proj, wglu, sp["cblk_r"], sp["cblk_in"],
            sp["bblk_r"], sp["bblk_i"], sp["a"], sp["pw"], sp["d"], *extra)


def _head_norm(x, first):
    x2 = x * x
    sa = jnp.sum(jnp.where(first, x2, 0.0), axis=-1, keepdims=True)
    sb = jnp.sum(jnp.where(first, 0.0, x2), axis=-1, keepdims=True)
    r = jnp.where(first, lax.rsqrt(sa * (1.0 / HEAD_DIM) + EPS), lax.rsqrt(sb * (1.0 / HEAD_DIM) + EPS))
    return x * r, r


def _attn_specs(bl, s, datt, qoff):
    nq = s // ATT_TQ
    nb = datt // LANES
    col = lambda blk: (lambda b, h, q: (b * nq + q, qoff + blk * nb + h))
    win = lambda blk, j: (lambda b, h, q: (b * nq + jnp.maximum(q - 2 + j, 0), qoff + blk * nb + h))
    tile = lambda f: pl.BlockSpec((ATT_TQ, LANES), f)
    qs = tile(col(0))
    ks = [tile(win(1, j)) for j in range(3)]
    vs = [tile(win(2, j)) for j in range(3)]
    return nq, nb, qs, ks, vs


def _attn_probs(q_ref, k_refs, gq_ref, gk_ref, bias_ref):
    qt = pl.program_id(2)
    lane = lax.broadcasted_iota(jnp.int32, (1, LANES), 1)
    first = lane < HEAD_DIM
    qh, rq = _head_norm(q_ref[...].astype(F32), first)
    qn = qh * gq_ref[...]
    kc = jnp.concatenate([r[...] for r in k_refs], axis=0).astype(F32)
    kh, _ = _head_norm(kc, first)
    kn = (kh * gk_ref[...]).astype(BF16)
    kpos = (qt - 2) * ATT_TQ + lax.broadcasted_iota(jnp.int32, (1, 3 * ATT_TQ), 1)
    valid = kpos >= 0
    scale = HEAD_DIM ** -0.5
    masks = (first, jnp.logical_not(first))
    qas, ps = [], []
    for hh in range(2):
        qa = jnp.where(masks[hh], qn, 0.0).astype(BF16)
        sc = _dot_t1(qa, kn) * scale + bias_ref[hh]
        sc = jnp.where(valid, sc, NEG)
        e = jnp.exp(sc - jnp.max(sc, axis=-1, keepdims=True))
        ps.append(e / jnp.sum(e, axis=-1, keepdims=True))
        qas.append(qa)
    return first, masks, qh, rq, kn, qas, ps


def _attn_fwd(proj, gq2, gk2, bias, *, bl, s, datt, qoff, name):
    n = bl * s
    nq, nb, qs, ks, vs = _attn_specs(bl, s, datt, qoff)

    def body(q_ref, k0, k1, k2, v0, v1, v2, gq_ref, gk_ref, bias_ref, o_ref):
        first, _, _, _, _, _, ps = _attn_probs(q_ref, (k0, k1, k2), gq_ref, gk_ref, bias_ref)
        vc = jnp.concatenate([v0[...], v1[...], v2[...]], axis=0)
        o0 = _dot(ps[0].astype(BF16), vc)
        o1 = _dot(ps[1].astype(BF16), vc)
        o_ref[...] = jnp.where(first, o0, o1).astype(BF16)

    gs = pl.BlockSpec((1, LANES), lambda b, h, q: (0, 0))
    return pl.pallas_call(
        body, grid=(bl, nb, nq),
        in_specs=[qs, *ks, *vs, gs, gs, pl.BlockSpec((2, ATT_TQ, 3 * ATT_TQ), lambda b, h, q: (h, 0, 0))],
        out_specs=pl.BlockSpec((ATT_TQ, LANES), lambda b, h, q: (b * nq + q, h)),
        out_shape=_sds((n, datt), BF16),
        compiler_params=_cp("parallel", "parallel", "arbitrary"), name=name)(
            proj, proj, proj, proj, proj, proj, proj, gq2, gk2, bias)


def _attn_bwd(do, proj, gq2, gk2, bias, *, bl, s, datt, qoff, name):
    n = bl * s
    nq, nb, qs, ks, vs = _attn_specs(bl, s, datt, qoff)
    srows = s + 2 * ATT_TQ
    scale = HEAD_DIM ** -0.5

    def body(do_ref, q_ref, k0, k1, k2, v0, v1, v2, gq_ref, gk_ref, bias_ref,
             dq_ref, dk_ref, dv_ref, db_ref, dgq_ref):
        qt = pl.program_id(2)

        @pl.when(qt == 0)
        def _():
            dk_ref[...] = jnp.zeros_like(dk_ref)
            dv_ref[...] = jnp.zeros_like(dv_ref)
            db_ref[...] = jnp.zeros_like(db_ref)
            dgq_ref[...] = jnp.zeros_like(dgq_ref)

        first, masks, qh, rq, kn, qas, ps = _attn_probs(q_ref, (k0, k1, k2), gq_ref, gk_ref, bias_ref)
        vc = jnp.concatenate([v0[...], v1[...], v2[...]], axis=0)
        dov = do_ref[...]
        dqn = jnp.zeros((ATT_TQ, LANES), F32)
        dkn = jnp.zeros((3 * ATT_TQ, LANES), F32)
        dv = jnp.zeros((3 * ATT_TQ, LANES), F32)
        for hh in range(2):
            doa = jnp.where(masks[hh], dov, jnp.zeros_like(dov))
            p = ps[hh]
            dp = _dot_t1(doa, vc)
            dsm = p * (dp - jnp.sum(dp * p, axis=-1, keepdims=True))
            db_ref[hh] += dsm
            dsc = (dsm * scale).astype(BF16)
            dqn = dqn + _dot(dsc, jnp.where(masks[hh], kn, jnp.zeros_like(kn)))
            dkn = dkn + _dot_t0(dsc, qas[hh])
            dv = dv + _dot_t0(p.astype(BF16), doa)
        start = pl.multiple_of(qt * ATT_TQ, ATT_TQ)
        dk_ref[pl.ds(start, 3 * ATT_TQ), :] += dkn
        dv_ref[pl.ds(start, 3 * ATT_TQ), :] += dv
        dgq_ref[...] += jnp.sum(dqn * qh, axis=0, keepdims=True)
        dqh = dqn * gq_ref[...]
        t = dqh * qh
        ma = jnp.sum(jnp.where(first, t, 0.0), axis=-1, keepdims=True) * (1.0 / HEAD_DIM)
        mb = jnp.sum(jnp.where(first, 0.0, t), axis=-1, keepdims=True) * (1.0 / HEAD_DIM)
        dq_ref[...] = (rq * (dqh - qh * jnp.where(first, ma, mb))).astype(BF16)

    gs = pl.BlockSpec((1, LANES), lambda b, h, q: (0, 0))
    acc = pl.BlockSpec((None, srows, LANES), lambda b, h, q: (b, 0, h))
    return pl.pallas_call(
        body, grid=(bl, nb, nq),
        in_specs=[pl.BlockSpec((ATT_TQ, LANES), lambda b, h, q: (b * nq + q, h)), qs, *ks, *vs, gs, gs,
                  pl.BlockSpec((2, ATT_TQ, 3 * ATT_TQ), lambda b, h, q: (h, 0, 0))],
        out_specs=[pl.BlockSpec((ATT_TQ, LANES), lambda b, h, q: (b * nq + q, h)), acc, acc,
                   pl.BlockSpec((None, 2, ATT_TQ, 3 * ATT_TQ), lambda b, h, q: (b, h, 0, 0)),
                   pl.BlockSpec((None, None, 1, LANES), lambda b, h, q: (b, h, 0, 0))],
        out_shape=[_sds((n, datt), BF16), _sds((bl, srows, datt), F32), _sds((bl, srows, datt), F32),
                   _sds((bl, 2 * nb, ATT_TQ, 3 * ATT_TQ), F32), _sds((bl, nb, 1, LANES), F32)],
        compiler_params=_cp("arbitrary", "arbitrary", "arbitrary"), name=name)(
            do, proj, proj, proj, proj, proj, proj, proj, gq2, gk2, bias)


def _attn_kv_bwd(dkn, dv, proj, gk2, *, bl, s, datt, tm, koff, name):
    n = bl * s
    ns = s // tm
    off = 2 * ATT_TQ // tm
    nb = datt // LANES

    def body(dkn_ref, dv_ref, k_ref, gk_ref, dk_ref, dvo_ref, dgk_ref):
        lane = lax.broadcasted_iota(jnp.int32, (1, LANES), 1)
        first = lane < HEAD_DIM

        @pl.when((pl.program_id(0) == 0) & (pl.program_id(1) == 0) & (pl.program_id(2) == 0))
        def _():
            dgk_ref[...] = jnp.zeros_like(dgk_ref)

        dvo_ref[...] = dv_ref[...].astype(BF16)
        kh, rk = _head_norm(k_ref[...].astype(F32), first)
        dn = dkn_ref[...]
        dgk_ref[...] += jnp.sum(dn * kh, axis=0, keepdims=True)
        dh = dn * gk_ref[...]
        t = dh * kh
        ma = jnp.sum(jnp.where(first, t, 0.0), axis=-1, keepdims=True) * (1.0 / HEAD_DIM)
        mb = jnp.sum(jnp.where(first, 0.0, t), axis=-1, keepdims=True) * (1.0 / HEAD_DIM)
        dk_ref[...] = (rk * (dh - kh * jnp.where(first, ma, mb))).astype(BF16)

    accs = pl.BlockSpec((None, tm, LANES), lambda b, i, c: (b, i + off, c))
    outs = pl.BlockSpec((tm, LANES), lambda b, i, c: (b * ns + i, c))
    vec = pl.BlockSpec((1, LANES), lambda b, i, c: (0, 0))
    return pl.pallas_call(
        body, grid=(bl, ns, nb),
        in_specs=[accs, accs, pl.BlockSpec((tm, LANES), lambda b, i, c: (b * ns + i, koff + c)), vec],
        out_specs=[outs, outs, vec],
        out_shape=[_sds((n, datt), BF16), _sds((n, datt), BF16), _sds((1, LANES), F32)],
        compiler_params=_cp("arbitrary", "arbitrary", "arbitrary"), name=name)(dkn, dv, proj, gk2)


def _conv_fwd(proj, wdw, bdw, lng, lnb, *, bl, s, t, acol, name):
    n = bl * s
    dc = wdw.shape[1]
    nt = s // t
    hb = t // HALO

    def body(za_ref, zg_ref, ha_ref, hgt_ref, w_ref, b_ref, g_ref, be_ref, hg_ref, hc_ref, o_ref, ext):
        i = pl.program_id(1)
        hg = za_ref[...].astype(F32) * _sigmoid(zg_ref[...].astype(F32))
        live = jnp.where(i > 0, 1.0, 0.0)
        ext[0:HALO, :] = ha_ref[...].astype(F32) * _sigmoid(hgt_ref[...].astype(F32)) * live
        ext[HALO:HALO + t, :] = hg
        hg_ref[...] = hg
        acc = jnp.zeros((t, dc), F32) + b_ref[...]
        for j in range(CONV_W):
            acc = acc + w_ref[j:j + 1, :] * ext[pl.ds(HALO - (CONV_W - 1) + j, t), :]
        hc_ref[...] = acc
        mu = jnp.mean(acc, axis=-1, keepdims=True)
        xc = acc - mu
        rs = lax.rsqrt(jnp.mean(xc * xc, axis=-1, keepdims=True) + EPS)
        ln = xc * rs * g_ref[...] + be_ref[...]
        o_ref[...] = (ln * _sigmoid(ln)).astype(BF16)

    vec = pl.BlockSpec((1, dc), lambda b, i: (0, 0))
    row = pl.BlockSpec((t, dc), lambda b, i: (b * nt + i, 0))
    tile = lambda c: pl.BlockSpec((t, dc), lambda b, i: (b * nt + i, c))
    halo = lambda c: pl.BlockSpec((HALO, dc), lambda b, i: (jnp.maximum((b * nt + i) * hb - 1, 0), c))
    return pl.pallas_call(
        body, grid=(bl, nt),
        in_specs=[tile(acol), tile(acol + 1), halo(acol), halo(acol + 1),
                  pl.BlockSpec((HALO, dc), lambda b, i: (0, 0)), vec, vec, vec],
        out_specs=[row, row, row],
        out_shape=[_sds((n, dc), F32), _sds((n, dc), F32), _sds((n, dc), BF16)],
        scratch_shapes=[pltpu.VMEM((HALO + t, dc), F32)],
        compiler_params=_cp("parallel", "arbitrary"), name=name)(proj, proj, proj, proj, wdw, bdw, lng, lnb)


def _conv_bwd_ln(dco, hc, lng, lnb, *, tm, name):
    n, dc = hc.shape

    def body(d_ref, hc_ref, g_ref, be_ref, dhc_ref, dg_ref, db_ref):
        @pl.when(pl.program_id(0) == 0)
        def _():
            dg_ref[...] = jnp.zeros_like(dg_ref)
            db_ref[...] = jnp.zeros_like(db_ref)

        hcv = hc_ref[...]
        mu = jnp.mean(hcv, axis=-1, keepdims=True)
        xc = hcv - mu
        rs = lax.rsqrt(jnp.mean(xc * xc, axis=-1, keepdims=True) + EPS)
        xh = xc * rs
        ln = xh * g_ref[...] + be_ref[...]
        sg = _sigmoid(ln)
        dln = d_ref[...].astype(F32) * (sg * (1.0 + ln * (1.0 - sg)))
        db_ref[...] += jnp.sum(dln, axis=0, keepdims=True)
        dg_ref[...] += jnp.sum(dln * xh, axis=0, keepdims=True)
        dxh = dln * g_ref[...]
        dhc_ref[...] = rs * (dxh - jnp.mean(dxh, axis=-1, keepdims=True)
                             - xh * jnp.mean(dxh * xh, axis=-1, keepdims=True))

    vec = pl.BlockSpec((1, dc), lambda i: (0, 0))
    row = pl.BlockSpec((tm, dc), lambda i: (i, 0))
    return pl.pallas_call(
        body, grid=(n // tm,), in_specs=[row, row, vec, vec], out_specs=[row, vec, vec],
        out_shape=[_sds((n, dc), F32), _sds((1, dc), F32), _sds((1, dc), F32)],
        compiler_params=_cp("arbitrary"), name=name)(dco, hc, lng, lnb)


def _conv_bwd_dw(dhc, hg, proj, wdw, *, bl, s, t, acol, name):
    n = bl * s
    dc = wdw.shape[1]
    nt = s // t
    hb = t // HALO
    lastblk = n // HALO - 1

    def body(d_ref, dn_ref, hg_ref, hp_ref, za_ref, zg_ref, w_ref, dz_ref, dw_ref, dbias_ref, extd, exth):
        b = pl.program_id(0)
        i = pl.program_id(1)

        @pl.when((b == 0) & (i == 0))
        def _():
            dw_ref[...] = jnp.zeros_like(dw_ref)
            dbias_ref[...] = jnp.zeros_like(dbias_ref)

        dv = d_ref[...]
        extd[0:t, :] = dv
        extd[t:t + HALO, :] = dn_ref[...] * jnp.where(i < nt - 1, 1.0, 0.0)
        exth[0:HALO, :] = hp_ref[...] * jnp.where(i > 0, 1.0, 0.0)
        exth[HALO:HALO + t, :] = hg_ref[...]
        dbias_ref[...] += jnp.sum(dv, axis=0, keepdims=True)
        dhg = jnp.zeros((t, dc), F32)
        for j in range(CONV_W):
            dhg = dhg + w_ref[j:j + 1, :] * extd[pl.ds(CONV_W - 1 - j, t), :]
            dw_ref[j:j + 1, :] += jnp.sum(dv * exth[pl.ds(HALO - (CONV_W - 1) + j, t), :], axis=0, keepdims=True)
        za = za_ref[...].astype(F32)
        sg = _sigmoid(zg_ref[...].astype(F32))
        dz_ref[...] = jnp.concatenate([dhg * sg, dhg * za * sg * (1.0 - sg)], axis=1).astype(BF16)

    row = pl.BlockSpec((t, dc), lambda b, i: (b * nt + i, 0))
    nxt = pl.BlockSpec((HALO, dc), lambda b, i: (jnp.minimum((b * nt + i + 1) * hb, lastblk), 0))
    prv = pl.BlockSpec((HALO, dc), lambda b, i: (jnp.maximum((b * nt + i) * hb - 1, 0), 0))
    wsp = pl.BlockSpec((HALO, dc), lambda b, i: (0, 0))
    tile = lambda c: pl.BlockSpec((t, dc), lambda b, i: (b * nt + i, c))
    return pl.pallas_call(
        body, grid=(bl, nt),
        in_specs=[row, nxt, row, prv, tile(acol), tile(acol + 1), wsp],
        out_specs=[pl.BlockSpec((t, 2 * dc), lambda b, i: (b * nt + i, 0)), wsp,
                   pl.BlockSpec((1, dc), lambda b, i: (0, 0))],
        out_shape=[_sds((n, 2 * dc), BF16), _sds((HALO, dc), F32), _sds((1, dc), F32)],
        scratch_shapes=[pltpu.VMEM((t + HALO, dc), F32), pltpu.VMEM((HALO + t, dc), F32)],
        compiler_params=_cp("arbitrary", "arbitrary"), name=name)(dhc, dhc, hg, hg, proj, proj, wdw)


def _mix_out_fwd(x, brs, gl, bg, wbs, wout, l, *, tm, name):
    n, d = x.shape

    def body(x_ref, s_ref, a_ref, c_ref, g0, g1, g2, bg_ref, ws, wa, wc, wo, o_ref):
        merged = jnp.zeros((tm, d), F32)
        for k, (br, gr, w) in enumerate(((s_ref, g0, ws), (a_ref, g1, wa), (c_ref, g2, wc))):
            gate = _sigmoid(gr[...].astype(F32) + bg_ref[:, k * d:(k + 1) * d])
            merged = merged + gate * _dot(br[...], w[...])
        o_ref[...] = x_ref[...] + _dot(merged.astype(BF16), wo[...])

    row = lambda w: pl.BlockSpec((tm, w), lambda i: (i, 0))
    wsp = lambda a: pl.BlockSpec((None,) + a.shape[1:], lambda i: (l, 0, 0))
    gls = [pl.BlockSpec((tm, d), functools.partial(lambda k, i: (i, k), k)) for k in range(3)]
    return pl.pallas_call(
        body, grid=(n // tm,),
        in_specs=[row(d), *[row(b.shape[1]) for b in brs], *gls, pl.BlockSpec(bg.shape, lambda i: (0, 0)),
                  *[wsp(w) for w in wbs], wsp(wout)],
        out_specs=row(d), out_shape=_sds((n, d), F32),
        compiler_params=_cp("parallel"), name=name)(x, *brs, gl, gl, gl, bg, *wbs, wout)


def _mix_out_bwd(dx, brs, gl, bg, wbs, wout, l, nl, bufs, *, tm, name):
    n, d = dx.shape
    widths = [b.shape[1] for b in brs]

    def body(dx_ref, s_ref, a_ref, c_ref, g0, g1, g2, bg_ref, ws, wa, wc, wo, *rest):
        ds_ref, da_ref, dc_ref, dgl_ref, dbg_ref, dws, dwa, dwc, dwo = rest[-9:]

        @pl.when(pl.program_id(0) == 0)
        def _():
            for r in (dbg_ref, dws, dwa, dwc, dwo):
                r[...] = jnp.zeros_like(r)

        dxb = dx_ref[...].astype(BF16)
        dm = _dot_t1(dxb, wo[...])
        merged = jnp.zeros((tm, d), F32)
        for k, (br, gr, w, dbr, dw) in enumerate(((s_ref, g0, ws, ds_ref, dws), (a_ref, g1, wa, da_ref, dwa),
                                                   (c_ref, g2, wc, dc_ref, dwc))):
            gate = _sigmoid(gr[...].astype(F32) + bg_ref[:, k * d:(k + 1) * d])
            brv = br[...]
            wv = w[...]
            y = _dot(brv, wv)
            merged = merged + gate * y
            dyb = (dm * gate).astype(BF16)
            dbr[...] = _dot_t1(dyb, wv).astype(BF16)
            dw[...] += _dot_t0(brv, dyb)
            dgl = dm * y * gate * (1.0 - gate)
            dgl_ref[:, k * d:(k + 1) * d] = dgl.astype(BF16)
            dbg_ref[:, k * d:(k + 1) * d] += jnp.sum(dgl, axis=0, keepdims=True)
        dwo[...] += _dot_t0(merged.astype(BF16), dxb)

    row = lambda w: pl.BlockSpec((tm, w), lambda i: (i, 0))
    wsp = lambda shape: pl.BlockSpec((None,) + tuple(shape), lambda i: (l, 0, 0))
    gls = [pl.BlockSpec((tm, d), functools.partial(lambda k, i: (i, k), k)) for k in range(3)]
    slabs = [(w, d) for w in widths] + [(d, d)]
    n_in = 12
    extra = [] if bufs is None else list(bufs)
    aliases = {} if bufs is None else {n_in + k: 5 + k for k in range(4)}
    return pl.pallas_call(
        body, grid=(n // tm,),
        in_specs=[row(d), *[row(w) for w in widths], *gls, pl.BlockSpec(bg.shape, lambda i: (0, 0)),
                  *[wsp(w.shape[1:]) for w in wbs], wsp(wout.shape[1:]), *[_ANY for _ in extra]],
        out_specs=[*[row(w) for w in widths], row(3 * d), pl.BlockSpec((1, 3 * d), lambda i: (0, 0)),
                   *[wsp(sh) for sh in slabs]],
        out_shape=[*[_sds((n, w), BF16) for w in widths], _sds((n, 3 * d), BF16), _sds((1, 3 * d), F32),
                   *[_sds((nl,) + sh, F32) for sh in slabs]],
        input_output_aliases=aliases,
        compiler_params=_cp("arbitrary"), name=name)(dx, *brs, gl, gl, gl, bg, *wbs, wout, *extra)


def _adamw(w, g, m, v, *, name):
    r, c = w.shape
    tm = _tile(r, 256)
    c1 = 1.0 - ADAM_B1 ** ADAM_STEP
    c2 = 1.0 - ADAM_B2 ** ADAM_STEP

    def body(w_ref, g_ref, m_ref, v_ref, d_ref, nm_ref, nv_ref):
        gv = g_ref[...]
        mn = ADAM_B1 * m_ref[...] + (1.0 - ADAM_B1) * gv
        vn = ADAM_B2 * v_ref[...] + (1.0 - ADAM_B2) * (gv * gv)
        nm_ref[...] = mn
        nv_ref[...] = vn
        d_ref[...] = -ADAM_LR * ((mn / c1) / (jnp.sqrt(vn / c2) + ADAM_EPS) + ADAM_WD * w_ref[...])

    blk = pl.BlockSpec((tm, c), lambda i: (i, 0))
    return pl.pallas_call(
        body, grid=(r // tm,), in_specs=[blk] * 4, out_specs=[blk] * 3,
        out_shape=[_sds((r, c), F32)] * 3, compiler_params=_cp("parallel"), name=name)(w, g, m, v)


def _add_sibling(g, recv, c_idx, *, name):
    _, a, b = g.shape
    ta = _tile(a, 256)

    def body(c_ref, g_ref, r_ref, o_ref):
        o_ref[...] = (g_ref[...] + r_ref[...]).astype(BF16)

    return pl.pallas_call(
        body,
        grid_spec=pltpu.PrefetchScalarGridSpec(
            num_scalar_prefetch=1, grid=(a // ta,),
            in_specs=[pl.BlockSpec((None, ta, b), lambda i, c_ref: (c_ref[0], i, 0)),
                      pl.BlockSpec((ta, b), lambda i, c_ref: (i, 0))],
            out_specs=pl.BlockSpec((ta, b), lambda i, c_ref: (i, 0))),
        out_shape=_sds((a, b), BF16), compiler_params=_cp("parallel"), name=name)(c_idx, g, recv)


def _add_chips(rsum, parts, axis, s_idx, c_idx, *, name):
    _, a, b = parts.shape
    ta = _tile(a, 256)
    na = a // ta

    def body(s_ref, c_ref, own_ref, p0, p1, p2, p3, o_ref):
        own = own_ref[...].astype(F32)
        terms = [jnp.where(s_ref[0] == s, own, p[...].astype(F32)) for s, p in enumerate((p0, p1, p2, p3))]
        o_ref[...] = ((terms[0] + terms[1]) + terms[2]) + terms[3]

    own_spec = (pl.BlockSpec((ta, b), lambda i, sr, cr: (sr[0] * na + i, 0)) if axis == 1
                else pl.BlockSpec((ta, b), lambda i, sr, cr: (i, sr[0])))
    part_spec = lambda s: pl.BlockSpec((None, ta, b), lambda i, sr, cr: (jnp.where(sr[0] == s, s ^ 1, s), i, 0))
    return pl.pallas_call(
        body,
        grid_spec=pltpu.PrefetchScalarGridSpec(
            num_scalar_prefetch=2, grid=(na,),
            in_specs=[own_spec] + [part_spec(s) for s in range(N_CHIPS)],
            out_specs=pl.BlockSpec((None, ta, b), lambda i, sr, cr: (cr[0], i, 0))),
        out_shape=_sds((2, a, b), F32), compiler_params=_cp("parallel"), name=name)(
            s_idx, c_idx, rsum, parts, parts, parts, parts)


def _place_shard(wloc, axis, s_idx, *, name):
    nl, a, b = wloc.shape
    ta = _tile(a, 256)
    na = a // ta
    full = (nl, a * N_CHIPS, b) if axis == 1 else (nl, a, b * N_CHIPS)

    def body(sc_ref, w_ref, o_ref):
        o_ref[...] = w_ref[...].astype(BF16)

    out_spec = (pl.BlockSpec((None, ta, b), lambda l, i, sc: (l, sc[0] * na + i, 0)) if axis == 1
                else pl.BlockSpec((None, ta, b), lambda l, i, sc: (l, i, sc[0])))
    return pl.pallas_call(
        body,
        grid_spec=pltpu.PrefetchScalarGridSpec(
            num_scalar_prefetch=1, grid=(nl, na),
            in_specs=[pl.BlockSpec((None, ta, b), lambda l, i, sc: (l, i, 0))], out_specs=out_spec),
        out_shape=_sds(full, BF16), compiler_params=_cp("parallel", "parallel"), name=name)(s_idx, wloc)


def _blockdiag(w):
    g, r, c = w.shape
    eye = jnp.eye(g, dtype=w.dtype)
    return (w[:, :, None, :] * eye[:, None, :, None]).reshape(g * r, g * c)


def _s5_prep(lre, lim, log_dt, b_re, b_im, c_re, c_im, d_skip):
    lr = jnp.minimum(lre, -1e-4)
    li = lim
    dt = jnp.exp(log_dt)[:, None]
    mag = jnp.exp(lr * dt)
    ar = mag * jnp.cos(li * dt)
    ai = mag * jnp.sin(li * dt)
    den = lr * lr + li * li
    coef_r = ((ar - 1.0) * lr + ai * li) / den
    coef_i = (ai * lr - (ar - 1.0) * li) / den
    bbar_r = coef_r[..., None] * b_re - coef_i[..., None] * b_im
    bbar_i = coef_r[..., None] * b_im + coef_i[..., None] * b_re
    a = jnp.stack([ar.reshape(-1), ai.reshape(-1)])
    return dict(
        a=a,
        bblk_r=_blockdiag(bbar_r.transpose(0, 2, 1)), bblk_i=_blockdiag(bbar_i.transpose(0, 2, 1)),
        cblk_r=_blockdiag(c_re.transpose(0, 2, 1)), cblk_in=_blockdiag(-c_im.transpose(0, 2, 1)),
        d=d_skip.reshape(1, -1))


def _s5_powers(a, nlog):
    pr, pi = a[0], a[1]
    rows = []
    for _ in range(nlog):
        rows += [pr, pi]
        pr, pi = pr * pr - pi * pi, 2.0 * pr * pi
    return jnp.stack(rows)


def _bias_table(rel_bias):
    h = rel_bias.shape[0]
    tq, tw = ATT_TQ, 3 * ATT_TQ
    n_hi = tw - 1 - MAX_REL + 1
    n_lo = tq + tw - 1 - n_hi - (2 * MAX_REL - 1)
    fr = jnp.concatenate([
        jnp.broadcast_to(rel_bias[:, 2 * MAX_REL:], (h, n_hi)),
        jnp.flip(rel_bias[:, 1:2 * MAX_REL], axis=1),
        jnp.broadcast_to(rel_bias[:, :1], (h, n_lo)),
        jnp.zeros((h, 1), rel_bias.dtype)], axis=1)
    ln = tq + tw
    flat = jnp.broadcast_to(fr[:, None, :], (h, tq, ln)).reshape(h, tq * ln)[:, :tq * (ln - 1)]
    tab = flat.reshape(h, tq, ln - 1)[:, :, tq - 1:tq - 1 + tw]
    qc = np.arange(tq)[:, None] // CHUNK + N_LEFT
    kc = np.arange(tw)[None, :] // CHUNK
    band = (kc <= qc) & (kc >= qc - N_LEFT)
    return jnp.where(jnp.asarray(band)[None], tab, NEG)


def _small_prep(w, l):
    g, p = w["s5_lambda_re"].shape[1:]
    b_shape, c_shape = (g, p, -1), (g, -1, p)
    sp = _s5_prep(w["s5_lambda_re"][l], w["s5_lambda_im"][l], w["s5_log_dt"][l], w["s5_b_re"][l].reshape(b_shape),
                  w["s5_b_im"][l].reshape(b_shape), w["s5_c_re"][l].reshape(c_shape), w["s5_c_im"][l].reshape(c_shape),
                  w["s5_d"][l])
    return sp, _bias_table(w["attn_rel_bias"][l])


_PREP_KEYS = ("s5_lambda_re", "s5_lambda_im", "s5_log_dt", "s5_b_re", "s5_b_im", "s5_c_re", "s5_c_im", "s5_d",
              "attn_rel_bias")
_BIG_KEYS = {"ffn1_w_up": 2, "ffn1_w_down": 1, "w_in": 2, "s5_w_glu": 2, "w_br_s5": 2, "w_br_attn": 2,
             "w_br_conv": 2, "w_out": 1, "ffn2_w_up": 2, "ffn2_w_down": 1}
_SMALL_KEYS = ("ffn1_norm", "mix_norm", "b_gate", "s5_lambda_re", "s5_lambda_im", "s5_log_dt", "s5_b_re", "s5_b_im",
               "s5_c_re", "s5_c_im", "s5_d", "attn_q_gain", "attn_k_gain", "attn_rel_bias", "conv_w_dw", "conv_b_dw",
               "conv_ln_g", "conv_ln_b", "ffn2_norm")
_WEIGHTS = ("ffn1_norm", "ffn1_w_up", "ffn1_w_down", "mix_norm", "w_in", "b_gate", "s5_lambda_re", "s5_lambda_im",
            "s5_log_dt", "s5_b_re", "s5_b_im", "s5_c_re", "s5_c_im", "s5_d", "s5_w_glu", "w_br_s5", "attn_q_gain",
            "attn_k_gain", "attn_rel_bias", "w_br_attn", "conv_w_dw", "conv_b_dw", "conv_ln_g", "conv_ln_b",
            "w_br_conv", "w_out", "ffn2_norm", "ffn2_w_up", "ffn2_w_down")


def _local_step(x3, target3, w):
    bl, s, d = x3.shape
    nl = w["ffn1_norm"].shape[0]
    dff = w["ffn1_w_down"].shape[1]
    ds5 = w["s5_d"].shape[1]
    datt = w["w_br_attn"].shape[1]
    dc = w["conv_b_dw"].shape[1]
    n = bl * s
    x = x3.reshape(n, d)
    target = target3.reshape(n, d)
    tm = _tile(n, 512)
    tml = _tile(n, 1024)
    tmix = _tile(n, 256)
    ts5 = 256
    tconv = _tile(s, 512)
    nlog = int(math.log2(ts5))
    tff = dff // 2
    ma = ds5 + 3 * datt + 2 * dc
    tna = ma // 3
    assert (3 * d) % tna == 0 and dff % 2 == 0
    qoff = ds5 // LANES
    koff = (ds5 + datt) // LANES
    acol = (ds5 + 3 * datt) // dc
    wbs = (w["w_br_s5"], w["w_br_attn"], w["w_br_conv"])

    saved = []
    for l in range(nl):
        (sp, bias), prep_vjp = jax.vjp(lambda ww: _small_prep(ww, l), {k: w[k] for k in _PREP_KEYS})
        spb = dict(sp)
        spb["pw"] = _s5_powers(lax.stop_gradient(sp["a"]), nlog)
        for k in ("bblk_r", "bblk_i", "cblk_r", "cblk_in"):
            spb[k] = sp[k].astype(BF16)
        g1 = w["ffn1_norm"][l][None]
        g2 = w["ffn2_norm"][l][None]
        gm = w["mix_norm"][l][None]
        gq2 = jnp.tile(w["attn_q_gain"][l], 2)[None]
        gk2 = jnp.tile(w["attn_k_gain"][l], 2)[None]
        wdw = jnp.pad(w["conv_w_dw"][l], ((0, HALO - CONV_W), (0, 0)))
        bdw, lng, lnb = w["conv_b_dw"][l][None], w["conv_ln_g"][l][None], w["conv_ln_b"][l][None]
        bg = w["b_gate"][l][None]

        x0 = x
        h1, ab1 = _norm_mm(x0, g1, w["ffn1_w_up"], l, tm=tml, tn=tff, ntiles=4, pieces=2, transposed=False,
                           name=f"ffn1_up_{l}")
        x1 = _ffn_down(ab1, w["ffn1_w_down"], l, x0, tm=tm, tk=tff, name=f"ffn1_down_{l}")
        h2, pa = _norm_mm(x1, gm, w["w_in"], l, tm=tml, tn=tna, ntiles=3, pieces=1, transposed=True, name=f"win_a_{l}")
        pa = pa[0]
        gl = _mm_t(h2, w["w_in"], l, tm=tml, tn=tna, off=3, ntiles=3 * d // tna, name=f"win_g_{l}")
        xr, xi, yp, zg, s5o = _s5_fwd(pa, spb, w["s5_w_glu"], l, bl=bl, s=s, t=ts5, name=f"s5_fwd_{l}")
        atto = _attn_fwd(pa, gq2, gk2, bias, bl=bl, s=s, datt=datt, qoff=qoff, name=f"attn_fwd_{l}")
        hg, hc, convo = _conv_fwd(pa, wdw, bdw, lng, lnb, bl=bl, s=s, t=tconv, acol=acol, name=f"conv_fwd_{l}")
        brs = (s5o, atto, convo)
        x2 = _mix_out_fwd(x1, brs, gl, bg, wbs, w["w_out"], l, tm=tmix, name=f"mix_fwd_{l}")
        h3, ab2 = _norm_mm(x2, g2, w["ffn2_w_up"], l, tm=tml, tn=tff, ntiles=4, pieces=2, transposed=False,
                           name=f"ffn2_up_{l}")
        x = _ffn_down(ab2, w["ffn2_w_down"], l, x2, tm=tm, tk=tff, name=f"ffn2_down_{l}")
        saved.append(dict(spb=spb, bias=bias, prep_vjp=prep_vjp, g1=g1, g2=g2, gm=gm, gq2=gq2, gk2=gk2,
                          wdw=wdw, lng=lng, lnb=lnb, bg=bg, x0=x0, h1=h1, ab1=ab1, x1=x1, h2=h2, pa=pa, gl=gl,
                          xr=xr, xi=xi, yp=yp, zg=zg, hg=hg, hc=hc, brs=brs, x2=x2, h3=h3, ab2=ab2))

    dx, lsum = _loss_grad(x, target, tm=tm, name="loss")
    loss_part = 0.5 * jnp.sum(lsum) / d

    big = {k: None for k in _BIG_KEYS}
    small = {k: [None] * nl for k in _SMALL_KEYS}
    for l in reversed(range(nl)):
        sv = saved[l]

        def ffn_bwd(dx, xin, h, ab, g, tag):
            wu, wd = w[tag + "_w_up"], w[tag + "_w_down"]
            dab, big[tag + "_w_down"] = _ffn_dact(dx, wd, l, ab, nl, big[tag + "_w_down"], tm=tm, tk=tff,
                                                  name=f"{tag}_dact_{l}")
            big[tag + "_w_up"] = _mm_tn(h[None], dab, l, nl, big[tag + "_w_up"], ta=d, tb=tff, tk=tml,
                                        name=f"{tag}_dwu_{l}")
            dxo, dg = _ffn_dx(dab, wu, l, xin, g, dx, tm=tml, tk=tff, name=f"{tag}_dx_{l}")
            small[tag + "_norm"][l] = dg[0]
            return dxo

        dx = ffn_bwd(dx, sv["x2"], sv["h3"], sv["ab2"], sv["g2"], "ffn2")

        mix_keys = ("w_br_s5", "w_br_attn", "w_br_conv", "w_out")
        bufs = None if big["w_out"] is None else [big[k] for k in mix_keys]
        ds5o, datto, dconvo, dgl, dbg, *dws = _mix_out_bwd(
            dx, sv["brs"], sv["gl"], sv["bg"], wbs, w["w_out"], l, nl, bufs, tm=tmix, name=f"mix_bwd_{l}")
        small["b_gate"][l] = dbg[0]
        big.update(zip(mix_keys, dws))

        dhc, dlng, dlnb = _conv_bwd_ln(dconvo, sv["hc"], sv["lng"], sv["lnb"], tm=tm, name=f"conv_bwd_ln_{l}")
        dz, dwdw, dbdw = _conv_bwd_dw(dhc, sv["hg"], sv["pa"], sv["wdw"], bl=bl, s=s, t=tconv, acol=acol,
                                      name=f"conv_bwd_dw_{l}")
        small["conv_w_dw"][l] = dwdw[:CONV_W]
        small["conv_b_dw"][l], small["conv_ln_g"][l], small["conv_ln_b"][l] = dbdw[0], dlng[0], dlnb[0]

        dq, dkn, dvw, dbias, dgq = _attn_bwd(datto, sv["pa"], sv["gq2"], sv["gk2"], sv["bias"], bl=bl, s=s, datt=datt,
                                             qoff=qoff, name=f"attn_bwd_{l}")
        dk, dv, dgk = _attn_kv_bwd(dkn, dvw, sv["pa"], sv["gk2"], bl=bl, s=s, datt=datt, tm=_tile(s, 512), koff=koff,
                                   name=f"attn_kv_bwd_{l}")
        small["attn_q_gain"][l] = jnp.sum(dgq.reshape(-1, HEAD_DIM), axis=0)
        small["attn_k_gain"][l] = jnp.sum(dgk.reshape(-1, HEAD_DIM), axis=0)

        du, dd, dcr, dci, dbr, dbi, da, big["s5_w_glu"] = _s5_bwd(
            ds5o, sv["yp"], sv["zg"], sv["xr"], sv["xi"], sv["pa"], sv["spb"], w["s5_w_glu"], l, nl, big["s5_w_glu"],
            bl=bl, s=s, t=ts5, name=f"s5_bwd_{l}")
        prep_ct = (dict(a=da, bblk_r=dbr, bblk_i=dbi, cblk_r=dcr, cblk_in=dci, d=dd), jnp.sum(dbias, axis=0))
        (dprep,) = sv["prep_vjp"](prep_ct)
        for k in _PREP_KEYS:
            small[k][l] = dprep[k][l]

        dpa = jnp.concatenate([du, dq, dk, dv, dz], axis=1)
        big["w_in"] = _dwin_t(dpa, dgl, sv["h2"], l, nl, big["w_in"], ta=tna, tk=tml, name=f"dwin_{l}")
        dx, dgm = _mix_dx(dpa, dgl, w["w_in"], l, sv["x1"], sv["gm"], dx, tm=tml, tk=tna, name=f"mix_dx_{l}")
        small["mix_norm"][l] = dgm[0]

        dx = ffn_bwd(dx, sv["x0"], sv["h1"], sv["ab1"], sv["g1"], "ffn1")

    small = {k: jnp.stack(v) for k, v in small.items()}
    return loss_part, dx.reshape(bl, s, d), big, small


def _place():
    x, y, c = lax.axis_index("x"), lax.axis_index("y"), lax.axis_index("c")
    chips = [(1 - x, y), (x, 1 - y), (1 - x, 1 - y)]
    return x, y, c, chips


def _remote(src, dst, send_sems, recv_sems, k, dev):
    return pltpu.make_async_remote_copy(src_ref=src, dst_ref=dst, send_sem=send_sems.at[k], recv_sem=recv_sems.at[k],
                                        device_id=dev, device_id_type=MESH)


def _window(ref, lead, s, axis, blk):
    if axis == 1:
        sl = (pl.ds(pl.multiple_of(s * blk, 16), blk), slice(None))
    else:
        sl = (slice(None), pl.ds(pl.multiple_of(s * blk, LANES), blk))
    return ref.at[sl] if lead is None else ref.at[(lead,) + sl]


def _all_gather_weights(fulls, axes, taps):
    nw = len(fulls)
    assert fulls[0].shape[0] == 2
    blks = [f.shape[ax] // N_CHIPS for f, ax in zip(fulls, axes)]

    def body(*refs):
        taps_in = refs[nw]
        outs, taps_out = refs[nw + 1:2 * nw + 1], refs[2 * nw + 1]
        send_sems, recv_sems, local_sem = refs[-3:]
        x, y, c, chips = _place()
        s_me = 2 * x + y
        sibling = (x, y, 1 - c)
        win = lambda i, lyr, s: _window(outs[i], lyr, s, axes[i], blks[i])
        own_taps = pltpu.make_async_copy(taps_in, taps_out.at[s_me], local_sem)
        own_taps.start()
        sends = []
        for i in range(nw):
            for j, (cx, cy) in enumerate(chips):
                mine = win(i, c, s_me)
                sends.append(_remote(mine, mine, send_sems, recv_sems, 6 * i + j, (cx, cy, c)))
        for j, (cx, cy) in enumerate(chips):
            sends.append(_remote(taps_in, taps_out.at[s_me], send_sems, recv_sems, 6 * nw + j, (cx, cy, c)))
        for cp in sends:
            cp.start()
        for i in range(nw):
            for j, (cx, cy) in enumerate(chips):
                piece = win(i, c, 2 * cx + cy)
                _remote(piece, piece, send_sems, recv_sems, 6 * i + j, (cx, cy, c)).wait_recv()
                fw = _remote(piece, piece, send_sems, recv_sems, 6 * i + 3 + j, sibling)
                fw.start()
                sends.append(fw)
        for i in range(nw):
            for j, (cx, cy) in enumerate(chips):
                piece = win(i, 1 - c, 2 * cx + cy)
                _remote(piece, piece, send_sems, recv_sems, 6 * i + 3 + j, sibling).wait_recv()
        for j, (cx, cy) in enumerate(chips):
            slab = taps_out.at[2 * cx + cy]
            _remote(slab, slab, send_sems, recv_sems, 6 * nw + j, (cx, cy, c)).wait_recv()
        for cp in sends:
            cp.wait_send()
        own_taps.wait()

    nsem = 6 * nw + 3
    return pl.pallas_call(
        body, in_specs=[_ANY] * (nw + 1), out_specs=[_ANY] * (nw + 1),
        out_shape=[_sds(f.shape, f.dtype) for f in fulls] + [_sds((N_CHIPS,) + taps.shape, taps.dtype)],
        input_output_aliases={i: i for i in range(nw)},
        scratch_shapes=[pltpu.SemaphoreType.DMA((nsem,)), pltpu.SemaphoreType.DMA((nsem,)), pltpu.SemaphoreType.DMA],
        name="all_gather_weights")(*fulls, taps)


def _rs_swap(grads):
    nw = len(grads)

    def body(*refs):
        ins, outs = refs[:nw], refs[nw:2 * nw]
        send_sems, recv_sems = refs[-2:]
        x, y, c, _ = _place()
        cps = [_remote(ins[i].at[1 - c], outs[i], send_sems, recv_sems, i, (x, y, 1 - c)) for i in range(nw)]
        for cp in cps:
            cp.start()
        for cp in cps:
            cp.wait()

    return pl.pallas_call(
        body, in_specs=[_ANY] * nw, out_specs=[_ANY] * nw, out_shape=[_sds(g.shape[1:], g.dtype) for g in grads],
        scratch_shapes=[pltpu.SemaphoreType.DMA((nw,)), pltpu.SemaphoreType.DMA((nw,))],
        name="rs_swap_layers")(*grads)


def _rs_exchange(rsums, axes):
    nw = len(rsums)
    blks = [r.shape[ax - 1] // N_CHIPS for r, ax in zip(rsums, axes)]
    shard = [tuple(dim // N_CHIPS if i == ax - 1 else dim for i, dim in enumerate(r.shape)) for r, ax in zip(rsums, axes)]

    def body(*refs):
        ins, outs = refs[:nw], refs[nw:2 * nw]
        send_sems, recv_sems = refs[-2:]
        x, y, c, chips = _place()
        s_me = 2 * x + y
        win = lambda i, s: _window(ins[i], None, s, axes[i], blks[i])
        sends = [_remote(win(i, 2 * cx + cy), outs[i].at[s_me], send_sems, recv_sems, 3 * i + j, (cx, cy, c))
                 for i in range(nw) for j, (cx, cy) in enumerate(chips)]
        for cp in sends:
            cp.start()
        for i in range(nw):
            for j, (cx, cy) in enumerate(chips):
                slab = outs[i].at[2 * cx + cy]
                _remote(slab, slab, send_sems, recv_sems, 3 * i + j, (cx, cy, c)).wait_recv()
        for cp in sends:
            cp.wait_send()

    return pl.pallas_call(
        body, in_specs=[_ANY] * nw, out_specs=[_ANY] * nw,
        out_shape=[_sds((N_CHIPS,) + sh, r.dtype) for sh, r in zip(shard, rsums)],
        scratch_shapes=[pltpu.SemaphoreType.DMA((3 * nw,)), pltpu.SemaphoreType.DMA((3 * nw,))],
        name="rs_exchange_chips")(*rsums)


def _rs_join(ts):
    nw = len(ts)

    def body(*refs):
        outs = refs[nw:2 * nw]
        send_sems, recv_sems = refs[-2:]
        x, y, c, _ = _place()
        sends = [_remote(outs[i].at[c], outs[i].at[c], send_sems, recv_sems, i, (x, y, 1 - c)) for i in range(nw)]
        for cp in sends:
            cp.start()
        for i in range(nw):
            slab = outs[i].at[1 - c]
            _remote(slab, slab, send_sems, recv_sems, i, (x, y, 1 - c)).wait_recv()
        for cp in sends:
            cp.wait_send()

    return pl.pallas_call(
        body, in_specs=[_ANY] * nw, out_specs=[_ANY] * nw, out_shape=[_sds(t.shape, t.dtype) for t in ts],
        input_output_aliases={i: i for i in range(nw)},
        scratch_shapes=[pltpu.SemaphoreType.DMA((nw,)), pltpu.SemaphoreType.DMA((nw,))],
        name="rs_join_layers")(*ts)


def _all_reduce_small(arrs):
    na = len(arrs)
    nd = 8

    def body(*refs):
        ins, outs, recvs = refs[:na], refs[na:2 * na], refs[2 * na:3 * na]
        send_sems, recv_sems = refs[-2:]
        x, y, c, _ = _place()
        me = 4 * x + 2 * y + c
        for i in range(na):
            recvs[i][0] = ins[i][...]
        cps = []
        for rel in range(1, nd):
            dev = (1 - x if rel & 4 else x, 1 - y if rel & 2 else y, 1 - c if rel & 1 else c)
            for i in range(na):
                cp = _remote(ins[i], recvs[i].at[rel], send_sems, recv_sems, (rel - 1) * na + i, dev)
                cp.start()
                cps.append(cp)
        for rel in range(1, nd):
            for i in range(na):
                _remote(ins[i], recvs[i].at[rel], send_sems, recv_sems, (rel - 1) * na + i, (x, y, c)).wait_recv()
        for i in range(na):
            acc = recvs[i][me]
            for dv in range(1, nd):
                acc = acc + recvs[i][lax.bitwise_xor(me, dv)]
            outs[i][...] = acc
        for cp in cps:
            cp.wait_send()

    vm = pl.BlockSpec(memory_space=pltpu.VMEM)
    nsem = (nd - 1) * na
    return pl.pallas_call(
        body, in_specs=[vm] * na, out_specs=[vm] * na, out_shape=[_sds(t.shape, F32) for t in arrs],
        scratch_shapes=[pltpu.VMEM((nd,) + t.shape, F32) for t in arrs]
        + [pltpu.SemaphoreType.DMA((nsem,)), pltpu.SemaphoreType.DMA((nsem,))],
        compiler_params=pltpu.CompilerParams(vmem_limit_bytes=VMEM_LIMIT), name="all_reduce_small")(*arrs)


def _adamw_small(ws, gs, ms, vs):
    na = len(ws)
    c1 = 1.0 - ADAM_B1 ** ADAM_STEP
    c2 = 1.0 - ADAM_B2 ** ADAM_STEP

    def body(*refs):
        w_r, g_r, m_r, v_r = (refs[k * na:(k + 1) * na] for k in range(4))
        d_r, nm_r, nv_r = (refs[(4 + k) * na:(5 + k) * na] for k in range(3))
        for i in range(na):
            gv = g_r[i][...]
            mn = ADAM_B1 * m_r[i][...] + (1.0 - ADAM_B1) * gv
            vn = ADAM_B2 * v_r[i][...] + (1.0 - ADAM_B2) * (gv * gv)
            nm_r[i][...] = mn
            nv_r[i][...] = vn
            d_r[i][...] = -ADAM_LR * ((mn / c1) / (jnp.sqrt(vn / c2) + ADAM_EPS) + ADAM_WD * w_r[i][...])

    vm = pl.BlockSpec(memory_space=pltpu.VMEM)
    res = pl.pallas_call(
        body, in_specs=[vm] * (4 * na), out_specs=[vm] * (3 * na), out_shape=[_sds(t.shape, F32) for t in ws] * 3,
        compiler_params=pltpu.CompilerParams(vmem_limit_bytes=VMEM_LIMIT), name="adamw_small")(*ws, *gs, *ms, *vs)
    return res[:na], res[na:2 * na], res[2 * na:]


def kernel(x, ffn1_norm, ffn1_w_up, ffn1_w_down, mix_norm, w_in, b_gate, s5_lambda_re, s5_lambda_im, s5_log_dt, s5_b_re, s5_b_im, s5_c_re, s5_c_im, s5_d, s5_w_glu, w_br_s5, attn_q_gain, attn_k_gain, attn_rel_bias, w_br_attn, conv_w_dw, conv_b_dw, conv_ln_g, conv_ln_b, w_br_conv, w_out, ffn2_norm, ffn2_w_up, ffn2_w_down, loss_target, m_ffn1_norm, m_ffn1_w_up, m_ffn1_w_down, m_mix_norm, m_w_in, m_b_gate, m_s5_lambda_re, m_s5_lambda_im, m_s5_log_dt, m_s5_b_re, m_s5_b_im, m_s5_c_re, m_s5_c_im, m_s5_d, m_s5_w_glu, m_w_br_s5, m_attn_q_gain, m_attn_k_gain, m_attn_rel_bias, m_w_br_attn, m_conv_w_dw, m_conv_b_dw, m_conv_ln_g, m_conv_ln_b, m_w_br_conv, m_w_out, m_ffn2_norm, m_ffn2_w_up, m_ffn2_w_down, v_ffn1_norm, v_ffn1_w_up, v_ffn1_w_down, v_mix_norm, v_w_in, v_b_gate, v_s5_lambda_re, v_s5_lambda_im, v_s5_log_dt, v_s5_b_re, v_s5_b_im, v_s5_c_re, v_s5_c_im, v_s5_d, v_s5_w_glu, v_w_br_s5, v_attn_q_gain, v_attn_k_gain, v_attn_rel_bias, v_w_br_attn, v_conv_w_dw, v_conv_b_dw, v_conv_ln_g, v_conv_ln_b, v_w_br_conv, v_w_out, v_ffn2_norm, v_ffn2_w_up, v_ffn2_w_down):
    a = dict(locals())
    xi, yi, ci = lax.axis_index("x"), lax.axis_index("y"), lax.axis_index("c")
    s_me = 2 * xi + yi
    big_keys = list(_BIG_KEYS)
    axes = [1 if k == "w_in" else _BIG_KEYS[k] for k in big_keys]

    s_idx = s_me.astype(jnp.int32).reshape(1)
    c_idx = ci.astype(jnp.int32).reshape(1)
    placed = [_place_shard(jnp.swapaxes(a[k], 1, 2).astype(BF16) if k == "w_in" else a[k], ax, s_idx,
                           name=f"place_{k}") for k, ax in zip(big_keys, axes)]
    *fulls, taps = _all_gather_weights(placed, axes, a["conv_w_dw"])
    flat = lambda t: t.reshape(t.shape[0], t.shape[1], -1) if t.ndim == 4 else t
    w = {k: flat(a[k]) for k in _WEIGHTS}
    w.update(zip(big_keys, fulls))
    w["conv_w_dw"] = jnp.moveaxis(taps, 0, 2).reshape(taps.shape[1], taps.shape[2], -1)

    loss_part, grad_x, gbig, gsmall = _local_step(a["x"], a["loss_target"], w)
    loss = lax.psum(loss_part, ("x", "y", "c"))

    glist = [gbig[k] for k in big_keys]
    recv = _rs_swap(glist)
    rsums = [_add_sibling(g, r, c_idx, name=f"rs_add_sibling_{k}") for g, r, k in zip(glist, recv, big_keys)]
    parts = _rs_exchange(rsums, axes)
    mine = [_add_chips(r, p, ax, s_idx, c_idx, name=f"rs_add_chips_{k}")
            for r, p, ax, k in zip(rsums, parts, axes, big_keys)]
    gb = dict(zip(big_keys, _rs_join(mine)))
    gb["w_in"] = jnp.swapaxes(gb["w_in"], 1, 2)

    small_keys = list(_SMALL_KEYS)
    gs = dict(zip(small_keys, _all_reduce_small([gsmall[k] for k in small_keys])))
    blk = a["conv_w_dw"].shape[2]
    gs["conv_w_dw"] = lax.dynamic_slice_in_dim(gs["conv_w_dw"], s_me * blk, blk, axis=2)

    delta, new_m, new_v = {}, {}, {}
    for k in big_keys:
        shp = a[k].shape
        two_d = lambda t: t.reshape(-1, shp[-1])
        d_, m_, v_ = _adamw(two_d(a[k]), two_d(gb[k]), two_d(a["m_" + k]), two_d(a["v_" + k]), name=f"adamw_{k}")
        delta[k], new_m[k], new_v[k] = d_.reshape(shp), m_.reshape(shp), v_.reshape(shp)
    res = _adamw_small([flat(a[k]) for k in small_keys], [gs[k] for k in small_keys],
                       [flat(a["m_" + k]) for k in small_keys], [flat(a["v_" + k]) for k in small_keys])
    for dst, vals in zip((delta, new_m, new_v), res):
        dst.update({k: t.reshape(a[k].shape) for k, t in zip(small_keys, vals)})
    grads = {**gb, **{k: t.reshape(a[k].shape) for k, t in gs.items()}}

    return (loss, grad_x, *[grads[k] for k in _WEIGHTS], *[delta[k] for k in _WEIGHTS],
            *[new_m[k] for k in _WEIGHTS], *[new_v[k] for k in _WEIGHTS])
```

```python
import functools
import math

import numpy as np
import jax
import jax.numpy as jnp
from jax import lax
from jax.experimental import pallas as pl
from jax.experimental.pallas import tpu as pltpu

F32 = jnp.float32
BF16 = jnp.bfloat16
EPS = 1e-6
VMEM_LIMIT = 56 * 1024 * 1024
LANES = 128
HEAD_DIM = 64
CHUNK = 64
N_LEFT = 8
MAX_REL = 128
ATT_TQ = 256
CONV_W = 31
HALO = 32
ROW_CHUNK = 256
SUBLANES = 8
GROUP_LOG = 3
NEG = -1e30
N_CHIPS = 4
PACK_COLS = 1024

ADAM_LR = 0.001
ADAM_B1 = 0.9
ADAM_B2 = 0.999
ADAM_EPS = 1e-08
ADAM_WD = 0.01
ADAM_STEP = 10

MESH = pl.DeviceIdType.MESH
_ANY = pl.BlockSpec(memory_space=pl.ANY)


def _cp(*sem):
    return pltpu.CompilerParams(dimension_semantics=sem, vmem_limit_bytes=VMEM_LIMIT)


def _sds(shape, dtype):
    return jax.ShapeDtypeStruct(shape, dtype)


def _tile(n, pref):
    t = min(n, pref)
    while n % t:
        t -= 8
    return t


def _sigmoid(x):
    return jax.nn.sigmoid(x)


_GELU_C = math.sqrt(2.0 / math.pi)


def _gelu(y):
    return 0.5 * y * (1.0 + jnp.tanh(_GELU_C * (y + 0.044715 * y * y * y)))


def _gelu_grad(y):
    th = jnp.tanh(_GELU_C * (y + 0.044715 * y * y * y))
    return 0.5 * (1.0 + th) + 0.5 * y * (1.0 - th * th) * _GELU_C * (1.0 + 3.0 * 0.044715 * y * y)


def _dot(a, b):
    return jnp.dot(a, b, preferred_element_type=F32)


def _dot_t0(a, b):
    return lax.dot_general(a, b, (((0,), (0,)), ((), ())), preferred_element_type=F32)


def _dot_t1(a, b):
    return lax.dot_general(a, b, (((1,), (1,)), ((), ())), preferred_element_type=F32)


def _slab_out(nl, l, shape, buf, n_in):
    sds = _sds((nl,) + tuple(shape), F32)
    if buf is None:
        return [], [], sds, {}
    return [buf], [_ANY], sds, {n_in: 0}


def _norm_mm(x, g, w, l, *, tm, tn, ntiles, pieces, transposed, name):
    n, d = x.shape
    m = ntiles * tn
    mp = m // pieces
    npj = mp // tn

    def body(x_ref, g_ref, w_ref, h_ref, y_ref, h_scr):
        @pl.when(pl.program_id(1) == 0)
        def _():
            for r0 in range(0, tm, ROW_CHUNK):
                rows = slice(r0, r0 + ROW_CHUNK)
                xv = x_ref[rows, :]
                r = lax.rsqrt(jnp.mean(xv * xv, axis=-1, keepdims=True) + EPS)
                hb = (xv * r * g_ref[...]).astype(BF16)
                h_scr[rows, :] = hb
                h_ref[rows, :] = hb

        mm = _dot_t1 if transposed else _dot
        y_ref[...] = mm(h_scr[...], w_ref[...]).astype(BF16)

    wspec = (pl.BlockSpec((None, tn, d), lambda i, j: (l, j, 0)) if transposed
             else pl.BlockSpec((None, d, tn), lambda i, j: (l, 0, j)))
    return pl.pallas_call(
        body, grid=(n // tm, ntiles),
        in_specs=[pl.BlockSpec((tm, d), lambda i, j: (i, 0)), pl.BlockSpec((1, d), lambda i, j: (0, 0)), wspec],
        out_specs=[pl.BlockSpec((tm, d), lambda i, j: (i, 0)),
                   pl.BlockSpec((None, tm, tn), lambda i, j: (j // npj, i, j % npj))],
        out_shape=[_sds((n, d), BF16), _sds((pieces, n, mp), BF16)],
        scratch_shapes=[pltpu.VMEM((tm, d), BF16)],
        compiler_params=_cp("parallel", "arbitrary"), name=name)(x, g, w)


def _mm_t(a, w, l, *, tm, tn, off, ntiles, name):
    n, k = a.shape

    def body(a_ref, w_ref, y_ref):
        y_ref[...] = _dot_t1(a_ref[...], w_ref[...]).astype(BF16)

    return pl.pallas_call(
        body, grid=(n // tm, ntiles),
        in_specs=[pl.BlockSpec((tm, k), lambda i, j: (i, 0)), pl.BlockSpec((None, tn, k), lambda i, j: (l, off + j, 0))],
        out_specs=pl.BlockSpec((tm, tn), lambda i, j: (i, j)),
        out_shape=_sds((n, ntiles * tn), BF16),
        compiler_params=_cp("parallel", "arbitrary"), name=name)(a, w)


def _ffn_down(ab, wd, l, x, *, tm, tk, name):
    _, n, dff = ab.shape
    d = x.shape[1]
    nk = dff // tk

    def body(a_ref, b_ref, wd_ref, x_ref, o_ref, acc):
        k = pl.program_id(1)
        a = a_ref[...].astype(F32)
        b = b_ref[...].astype(F32)
        act = (a * _sigmoid(a) * b).astype(BF16)
        part = _dot(act, wd_ref[pl.ds(pl.multiple_of(k * tk, tk), tk), :])

        @pl.when(k == 0)
        def _():
            acc[...] = part

        @pl.when(k > 0)
        def _():
            acc[...] += part

        @pl.when(k == nk - 1)
        def _():
            o_ref[...] = x_ref[...] + 0.5 * acc[...]

    return pl.pallas_call(
        body, grid=(n // tm, nk),
        in_specs=[pl.BlockSpec((None, tm, tk), lambda i, k: (0, i, k)),
                  pl.BlockSpec((None, tm, tk), lambda i, k: (1, i, k)),
                  pl.BlockSpec((None, dff, d), lambda i, k: (l, 0, 0)),
                  pl.BlockSpec((tm, d), lambda i, k: (i, 0))],
        out_specs=pl.BlockSpec((tm, d), lambda i, k: (i, 0)),
        out_shape=_sds((n, d), F32),
        scratch_shapes=[pltpu.VMEM((tm, d), F32)],
        compiler_params=_cp("parallel", "arbitrary"), name=name)(ab, ab, wd, x)


def _ffn_dact(dx, wd, l, ab, nl, dwd_buf, *, tm, tk, name):
    n, d = dx.shape
    dff = ab.shape[2]
    half = ((tk // LANES + 1) // 2) * LANES
    chunks = ((0, half), (half, tk))

    def body(dx_ref, wd_ref, a_ref, b_ref, *rest):
        dab_ref, dwd_ref = rest[-2:]
        do = (0.5 * dx_ref[...]).astype(BF16)

        @pl.when(pl.program_id(1) == 0)
        def _():
            dwd_ref[...] = jnp.zeros_like(dwd_ref)

        for c0, c1 in chunks:
            dact = _dot_t1(do, wd_ref[c0:c1, :])
            a = a_ref[:, c0:c1].astype(F32)
            b = b_ref[:, c0:c1].astype(F32)
            sg = _sigmoid(a)
            silu = a * sg
            dab_ref[0, :, c0:c1] = (dact * b * (sg * (1.0 + a * (1.0 - sg)))).astype(BF16)
            dab_ref[1, :, c0:c1] = (dact * silu).astype(BF16)
            dwd_ref[c0:c1, :] += _dot_t0((silu * b).astype(BF16), do)

    extra, extra_specs, dwd_shape, aliases = _slab_out(nl, l, (dff, d), dwd_buf, 4)
    aliases = {k: 1 for k in aliases}
    return pl.pallas_call(
        body, grid=(dff // tk, n // tm),
        in_specs=[pl.BlockSpec((tm, d), lambda j, i: (i, 0)),
                  pl.BlockSpec((None, tk, d), lambda j, i: (l, j, 0)),
                  pl.BlockSpec((None, tm, tk), lambda j, i: (0, i, j)),
                  pl.BlockSpec((None, tm, tk), lambda j, i: (1, i, j)), *extra_specs],
        out_specs=[pl.BlockSpec((2, tm, tk), lambda j, i: (0, i, j)),
                   pl.BlockSpec((None, tk, d), lambda j, i: (l, j, 0))],
        out_shape=[_sds((2, n, dff), BF16), dwd_shape],
        input_output_aliases=aliases,
        compiler_params=_cp("arbitrary", "arbitrary"), name=name)(dx, wd, ab, ab, *extra)


def _rms_bwd_epilogue(acc, x_ref, g_ref, dres_ref, dx_ref, dg_ref, i):
    dgp = jnp.zeros(dg_ref.shape, F32)
    for r0 in range(0, acc.shape[0], ROW_CHUNK):
        rows = slice(r0, r0 + ROW_CHUNK)
        dh = acc[rows, :]
        xv = x_ref[rows, :]
        r = lax.rsqrt(jnp.mean(xv * xv, axis=-1, keepdims=True) + EPS)
        xn = xv * r
        dgp = dgp + jnp.sum(dh * xn, axis=0, keepdims=True)
        dxh = dh * g_ref[...]
        dx_ref[rows, :] = dres_ref[rows, :] + r * (dxh - xn * jnp.mean(dxh * xn, axis=-1, keepdims=True))

    @pl.when(i == 0)
    def _():
        dg_ref[...] = dgp

    @pl.when(i > 0)
    def _():
        dg_ref[...] += dgp


def _ffn_dx(dab, wu, l, x, g, dres, *, tm, tk, name):
    p, n, mp = dab.shape
    d = x.shape[1]
    nkp = mp // tk
    nk = p * nkp

    def body(dy_ref, w_ref, x_ref, g_ref, dres_ref, dx_ref, dg_ref, acc):
        k = pl.program_id(1)
        part = _dot_t1(dy_ref[...], w_ref[...])

        @pl.when(k == 0)
        def _():
            acc[...] = part

        @pl.when(k > 0)
        def _():
            acc[...] += part

        @pl.when(k == nk - 1)
        def _():
            _rms_bwd_epilogue(acc, x_ref, g_ref, dres_ref, dx_ref, dg_ref, pl.program_id(0))

    return pl.pallas_call(
        body, grid=(n // tm, nk),
        in_specs=[pl.BlockSpec((None, tm, tk), lambda i, k: (k // nkp, i, k % nkp)),
                  pl.BlockSpec((None, d, tk), lambda i, k: (l, 0, k)),
                  pl.BlockSpec((tm, d), lambda i, k: (i, 0)),
                  pl.BlockSpec((1, d), lambda i, k: (0, 0)),
                  pl.BlockSpec((tm, d), lambda i, k: (i, 0))],
        out_specs=[pl.BlockSpec((tm, d), lambda i, k: (i, 0)), pl.BlockSpec((1, d), lambda i, k: (0, 0))],
        out_shape=[_sds((n, d), F32), _sds((1, d), F32)],
        scratch_shapes=[pltpu.VMEM((tm, d), F32)],
        compiler_params=_cp("arbitrary", "arbitrary"), name=name)(dab, wu, x, g, dres)


def _mix_dx(dpa, dgl, wt, l, x, g, dres, *, tm, tk, name):
    n, d = x.shape
    n1 = dpa.shape[1] // tk
    n2 = dgl.shape[1] // tk
    nk = n1 + n2

    def body(d1_ref, d2_ref, w_ref, x_ref, g_ref, dres_ref, dx_ref, dg_ref, acc):
        k = pl.program_id(1)

        @pl.when(k == 0)
        def _():
            acc[...] = _dot(d1_ref[...], w_ref[...])

        @pl.when((k > 0) & (k < n1))
        def _():
            acc[...] += _dot(d1_ref[...], w_ref[...])

        @pl.when(k >= n1)
        def _():
            acc[...] += _dot(d2_ref[...], w_ref[...])

        @pl.when(k == nk - 1)
        def _():
            _rms_bwd_epilogue(acc, x_ref, g_ref, dres_ref, dx_ref, dg_ref, pl.program_id(0))

    return pl.pallas_call(
        body, grid=(n // tm, nk),
        in_specs=[pl.BlockSpec((tm, tk), lambda i, k: (i, jnp.minimum(k, n1 - 1))),
                  pl.BlockSpec((tm, tk), lambda i, k: (i, jnp.maximum(k - n1, 0))),
                  pl.BlockSpec((None, tk, d), lambda i, k: (l, k, 0)),
                  pl.BlockSpec((tm, d), lambda i, k: (i, 0)),
                  pl.BlockSpec((1, d), lambda i, k: (0, 0)),
                  pl.BlockSpec((tm, d), lambda i, k: (i, 0))],
        out_specs=[pl.BlockSpec((tm, d), lambda i, k: (i, 0)), pl.BlockSpec((1, d), lambda i, k: (0, 0))],
        out_shape=[_sds((n, d), F32), _sds((1, d), F32)],
        scratch_shapes=[pltpu.VMEM((tm, d), F32)],
        compiler_params=_cp("arbitrary", "arbitrary"), name=name)(dpa, dgl, wt, x, g, dres)


def _mm_tn(a, b, l, nl, buf, *, ta, tb, tk, name):
    pa, n, ka = a.shape
    pb, _, kb = b.shape
    nap = ka // ta
    nbp = kb // tb

    def body(a_ref, b_ref, *rest):
        o_ref = rest[-1]

        @pl.when(pl.program_id(2) == 0)
        def _():
            o_ref[...] = jnp.zeros_like(o_ref)

        o_ref[...] += _dot_t0(a_ref[...], b_ref[...])

    extra, extra_specs, out_shape, aliases = _slab_out(nl, l, (pa * ka, pb * kb), buf, 2)
    return pl.pallas_call(
        body, grid=(pa * nap, pb * nbp, n // tk),
        in_specs=[pl.BlockSpec((None, tk, ta), lambda i, j, k: (i // nap, k, i % nap)),
                  pl.BlockSpec((None, tk, tb), lambda i, j, k: (j // nbp, k, j % nbp)), *extra_specs],
        out_specs=pl.BlockSpec((None, ta, tb), lambda i, j, k: (l, i, j)),
        out_shape=out_shape, input_output_aliases=aliases,
        compiler_params=_cp("parallel", "parallel", "arbitrary"), name=name)(a, b, *extra)


def _dwin_t(dpa, dgl, h, l, nl, buf, *, ta, tk, name):
    n, d = h.shape
    n1 = dpa.shape[1] // ta
    n2 = dgl.shape[1] // ta

    def body(a1_ref, a2_ref, h_ref, *rest):
        o_ref = rest[-1]
        i = pl.program_id(0)

        @pl.when(pl.program_id(1) == 0)
        def _():
            o_ref[...] = jnp.zeros_like(o_ref)

        @pl.when(i < n1)
        def _():
            o_ref[...] += _dot_t0(a1_ref[...], h_ref[...])

        @pl.when(i >= n1)
        def _():
            o_ref[...] += _dot_t0(a2_ref[...], h_ref[...])

    extra, extra_specs, out_shape, aliases = _slab_out(nl, l, ((n1 + n2) * ta, d), buf, 3)
    return pl.pallas_call(
        body, grid=(n1 + n2, n // tk),
        in_specs=[pl.BlockSpec((tk, ta), lambda i, k: (jnp.where(i < n1, k, 0), jnp.minimum(i, n1 - 1))),
                  pl.BlockSpec((tk, ta), lambda i, k: (jnp.where(i >= n1, k, 0), jnp.maximum(i - n1, 0))),
                  pl.BlockSpec((tk, d), lambda i, k: (k, 0)), *extra_specs],
        out_specs=pl.BlockSpec((None, ta, d), lambda i, k: (l, i, 0)),
        out_shape=out_shape, input_output_aliases=aliases,
        compiler_params=_cp("parallel", "arbitrary"), name=name)(dpa, dgl, h, *extra)


def _loss_grad(y, t, *, tm, name):
    n, d = y.shape

    def body(y_ref, t_ref, dy_ref, l_ref):
        e = y_ref[...] - t_ref[...]
        dy_ref[...] = e * (1.0 / d)
        part = jnp.sum(e * e, axis=0, keepdims=True)

        @pl.when(pl.program_id(0) == 0)
        def _():
            l_ref[...] = part

        @pl.when(pl.program_id(0) > 0)
        def _():
            l_ref[...] += part

    return pl.pallas_call(
        body, grid=(n // tm,),
        in_specs=[pl.BlockSpec((tm, d), lambda i: (i, 0)), pl.BlockSpec((tm, d), lambda i: (i, 0))],
        out_specs=[pl.BlockSpec((tm, d), lambda i: (i, 0)), pl.BlockSpec((1, d), lambda i: (0, 0))],
        out_shape=[_sds((n, d), F32), _sds((1, d), F32)],
        compiler_params=_cp("arbitrary"), name=name)(y, t)


def _s5_fwd(proj, sp, wglu, l, *, bl, s, t, name):
    n = bl * s
    ds5, gp = sp["bblk_r"].shape
    nt = s // t

    def body(u_ref, br_ref, bi_ref, pw_ref, p8_ref, cr_ref, ci_ref, d_ref, wg_ref,
             xr_ref, xi_ref, yp_ref, zg_ref, o_ref, carry):
        @pl.when(pl.program_id(1) == 0)
        def _():
            carry[...] = jnp.zeros_like(carry)

        u = u_ref[...]
        sub = lax.broadcasted_iota(jnp.int32, (t, gp), 0) % SUBLANES
        xr = _dot(u, br_ref[...])
        xi = _dot(u, bi_ref[...])
        for k in range(GROUP_LOG):
            sh = 1 << k
            pr = pw_ref[2 * k:2 * k + 1, :]
            pi = pw_ref[2 * k + 1:2 * k + 2, :]
            keep = sub >= sh
            sr = jnp.where(keep, pltpu.roll(xr, sh, 0), 0.0)
            si = jnp.where(keep, pltpu.roll(xi, sh, 0), 0.0)
            xr, xi = xr + pr * sr - pi * si, xi + pr * si + pi * sr
        xr_ref[...] = xr
        xi_ref[...] = xi
        p8r = p8_ref[0:SUBLANES, :]
        p8i = p8_ref[SUBLANES:2 * SUBLANES, :]
        cr = carry[0:1, :]
        ci = carry[1:2, :]
        for g in range(t // SUBLANES):
            grp = slice(g * SUBLANES, (g + 1) * SUBLANES)
            lr = xr_ref[grp, :]
            li = xi_ref[grp, :]
            xr_ref[grp, :] = lr + p8r * cr - p8i * ci
            xi_ref[grp, :] = li + p8r * ci + p8i * cr
            cr = xr_ref[(g + 1) * SUBLANES - 1:(g + 1) * SUBLANES, :]
            ci = xi_ref[(g + 1) * SUBLANES - 1:(g + 1) * SUBLANES, :]
        carry[0:1, :] = cr
        carry[1:2, :] = ci
        xr = xr_ref[...]
        xi = xi_ref[...]
        y = _dot(xr.astype(BF16), cr_ref[...]) + _dot(xi.astype(BF16), ci_ref[...]) + d_ref[...] * u.astype(F32)
        yp_ref[...] = y
        zg = _dot(_gelu(y).astype(BF16), wg_ref[...])
        zg_ref[...] = zg
        o_ref[...] = (zg[:, :ds5] * _sigmoid(zg[:, ds5:])).astype(BF16)

    const = lambda shape: pl.BlockSpec(shape, lambda b, i: (0, 0))
    row = lambda w: pl.BlockSpec((t, w), lambda b, i: (b * nt + i, 0))
    return pl.pallas_call(
        body, grid=(bl, nt),
        in_specs=[row(ds5), const((ds5, gp)), const((ds5, gp)), const((2 * GROUP_LOG, gp)), const((2 * SUBLANES, gp)),
                  const((gp, ds5)), const((gp, ds5)), const((1, ds5)),
                  pl.BlockSpec((None, ds5, 2 * ds5), lambda b, i: (l, 0, 0))],
        out_specs=[row(gp), row(gp), row(ds5), row(2 * ds5), row(ds5)],
        out_shape=[_sds((n, gp), F32), _sds((n, gp), F32), _sds((n, ds5), F32), _sds((n, 2 * ds5), F32),
                   _sds((n, ds5), BF16)],
        scratch_shapes=[pltpu.VMEM((2, gp), F32)],
        compiler_params=_cp("arbitrary", "arbitrary"), name=name)(
            proj, sp["bblk_r"], sp["bblk_i"], sp["pw"], sp["p8"], sp["cblk_r"], sp["cblk_in"], sp["d"], wglu)


def _s5_bwd(ds, yp, zg, xr, xi, proj, sp, wglu, l, nl, dwg_buf, *, bl, s, t, name):
    n = bl * s
    ds5, gp = sp["bblk_r"].shape
    nt = s // t
    tb = t // 8

    def body(ds_ref, yp_ref, zg_ref, xr_ref, xi_ref, hr_ref, hi_ref, u_ref, wg_ref, cr_ref, ci_ref,
             br_ref, bi_ref, pw_ref, q8_ref, d_ref, *rest):
        du_ref, dd_ref, dcr_ref, dci_ref, dbr_ref, dbi_ref, da_ref, dwg_ref, carry, gr_scr, gi_scr = rest[-11:]
        b = pl.program_id(0)
        i = pl.program_id(1)
        tile = nt - 1 - i

        @pl.when((b == 0) & (i == 0))
        def _():
            for r in (dwg_ref, dd_ref, dcr_ref, dci_ref, dbr_ref, dbi_ref, da_ref):
                r[...] = jnp.zeros_like(r)

        @pl.when(i == 0)
        def _():
            carry[...] = jnp.zeros_like(carry)

        dsv = ds_ref[...].astype(F32)
        zgv = zg_ref[...]
        za = zgv[:, :ds5]
        sg = _sigmoid(zgv[:, ds5:])
        dzg = jnp.concatenate([dsv * sg, dsv * za * sg * (1.0 - sg)], axis=1).astype(BF16)
        y = yp_ref[...]
        dwg_ref[...] += _dot_t0(_gelu(y).astype(BF16), dzg)
        dy = _dot_t1(dzg, wg_ref[...]) * _gelu_grad(y)
        ub = u_ref[...]
        uf = ub.astype(F32)
        dd_ref[...] += jnp.sum(dy * uf, axis=0, keepdims=True)
        dyb = dy.astype(BF16)
        xrv = xr_ref[...]
        xiv = xi_ref[...]
        dcr_ref[...] += _dot_t0(xrv.astype(BF16), dyb)
        dci_ref[...] += _dot_t0(xiv.astype(BF16), dyb)

        rows = lax.broadcasted_iota(jnp.int32, (t, gp), 0)
        sub = rows % SUBLANES
        gr = _dot_t1(dyb, cr_ref[...])
        gi = _dot_t1(dyb, ci_ref[...])
        for k in range(GROUP_LOG):
            sh = 1 << k
            pr = pw_ref[2 * k:2 * k + 1, :]
            pi = pw_ref[2 * k + 1:2 * k + 2, :]
            keep = sub < SUBLANES - sh
            sr = jnp.where(keep, pltpu.roll(gr, t - sh, 0), 0.0)
            si = jnp.where(keep, pltpu.roll(gi, t - sh, 0), 0.0)
            gr, gi = gr + pr * sr + pi * si, gi + pr * si - pi * sr
        gr_scr[...] = gr
        gi_scr[...] = gi
        q8r = q8_ref[0:SUBLANES, :]
        q8i = q8_ref[SUBLANES:2 * SUBLANES, :]
        cr = carry[0:1, :]
        ci = carry[1:2, :]
        for g in reversed(range(t // SUBLANES)):
            grp = slice(g * SUBLANES, (g + 1) * SUBLANES)
            lr = gr_scr[grp, :]
            li = gi_scr[grp, :]
            gr_scr[grp, :] = lr + q8r * cr - q8i * ci
            gi_scr[grp, :] = li + q8r * ci + q8i * cr
            cr = gr_scr[g * SUBLANES:g * SUBLANES + 1, :]
            ci = gi_scr[g * SUBLANES:g * SUBLANES + 1, :]
        carry[0:1, :] = cr
        carry[1:2, :] = ci
        gr = gr_scr[...]
        gi = gi_scr[...]
        first = rows == 0

        live = jnp.where(tile > 0, 1.0, 0.0)
        xpr = jnp.where(first, hr_ref[7:8, :] * live, pltpu.roll(xrv, 1, 0))
        xpi = jnp.where(first, hi_ref[7:8, :] * live, pltpu.roll(xiv, 1, 0))
        da_ref[0:1, :] += jnp.sum(gr * xpr + gi * xpi, axis=0, keepdims=True)
        da_ref[1:2, :] += jnp.sum(gi * xpr - gr * xpi, axis=0, keepdims=True)

        grb = gr.astype(BF16)
        gib = gi.astype(BF16)
        dbr_ref[...] += _dot_t0(ub, grb)
        dbi_ref[...] += _dot_t0(ub, gib)
        du_ref[...] = (_dot_t1(grb, br_ref[...]) + _dot_t1(gib, bi_ref[...]) + dy * d_ref[...]).astype(BF16)

    const = lambda shape: pl.BlockSpec(shape, lambda b, i: (0, 0))
    row = lambda w: pl.BlockSpec((t, w), lambda b, i: (b * nt + nt - 1 - i, 0))
    halo = pl.BlockSpec((8, gp), lambda b, i: (jnp.maximum((b * nt + nt - 1 - i) * tb - 1, 0), 0))
    extra, extra_specs, dwg_shape, aliases = _slab_out(nl, l, (ds5, 2 * ds5), dwg_buf, 16)
    aliases = {k: 7 for k in aliases}
    return pl.pallas_call(
        body, grid=(bl, nt),
        in_specs=[row(ds5), row(ds5), row(2 * ds5), row(gp), row(gp), halo, halo, row(ds5),
                  pl.BlockSpec((None, ds5, 2 * ds5), lambda b, i: (l, 0, 0)),
                  const((gp, ds5)), const((gp, ds5)), const((ds5, gp)), const((ds5, gp)),
                  const((2 * GROUP_LOG, gp)), const((2 * SUBLANES, gp)), const((1, ds5)), *extra_specs],
        out_specs=[row(ds5), const((1, ds5)), const((gp, ds5)), const((gp, ds5)),
                   const((ds5, gp)), const((ds5, gp)), const((2, gp)),
                   pl.BlockSpec((None, ds5, 2 * ds5), lambda b, i: (l, 0, 0))],
        out_shape=[_sds((n, ds5), BF16), _sds((1, ds5), F32), _sds((gp, ds5), F32),
                   _sds((gp, ds5), F32), _sds((ds5, gp), F32), _sds((ds5, gp), F32), _sds((2, gp), F32), dwg_shape],
        input_output_aliases=aliases,
        scratch_shapes=[pltpu.VMEM((2, gp), F32), pltpu.VMEM((t, gp), F32), pltpu.VMEM((t, gp), F32)],
        compiler_params=_cp("arbitrary", "arbitrary"), name=name)(
            ds, yp, zg, xr, xi, xr, xi, proj, wglu, sp["cblk_r"], sp["cblk_in"],
            sp["bblk_r"], sp["bblk_i"], sp["pw"], sp["q8"], sp["d"], *extra)


def _head_norm(x, first):
    x2 = x * x
    sa = jnp.sum(jnp.where(first, x2, 0.0), axis=-1, keepdims=True)
    sb = jnp.sum(jnp.where(first, 0.0, x2), axis=-1, keepdims=True)
    r = jnp.where(first, lax.rsqrt(sa * (1.0 / HEAD_DIM) + EPS), lax.rsqrt(sb * (1.0 / HEAD_DIM) + EPS))
    return x * r, r


def _attn_specs(bl, s, datt, qoff):
    nq = s // ATT_TQ
    nb = datt // LANES
    col = lambda blk: (lambda b, h, q: (b * nq + q, qoff + blk * nb + h))
    win = lambda blk, j: (lambda b, h, q: (b * nq + jnp.maximum(q - 2 + j, 0), qoff + blk * nb + h))
    tile = lambda f: pl.BlockSpec((ATT_TQ, LANES), f)
    qs = tile(col(0))
    ks = [tile(win(1, j)) for j in range(3)]
    vs = [tile(win(2, j)) for j in range(3)]
    return nq, nb, qs, ks, vs


def _attn_probs(q_ref, k_refs, gq_ref, gk_ref, bias_ref):
    qt = pl.program_id(2)
    lane = lax.broadcasted_iota(jnp.int32, (1, LANES), 1)
    first = lane < HEAD_DIM
    qh, rq = _head_norm(q_ref[...].astype(F32), first)
    qn = qh * gq_ref[...]
    kc = jnp.concatenate([r[...] for r in k_refs], axis=0).astype(F32)
    kh, _ = _head_norm(kc, first)
    kn = (kh * gk_ref[...]).astype(BF16)
    kpos = (qt - 2) * ATT_TQ + lax.broadcasted_iota(jnp.int32, (1, 3 * ATT_TQ), 1)
    valid = kpos >= 0
    scale = HEAD_DIM ** -0.5
    masks = (first, jnp.logical_not(first))
    qas, ps = [], []
    for hh in range(2):
        qa = jnp.where(masks[hh], qn * scale, 0.0).astype(BF16)
        sc = jnp.where(valid, _dot_t1(qa, kn) + bias_ref[hh], NEG)
        e = jnp.exp(sc - jnp.max(sc, axis=-1, keepdims=True))
        ps.append(e * (1.0 / jnp.sum(e, axis=-1, keepdims=True)))
        qas.append(qa)
    return first, masks, qh, rq, kn, qas, ps


def _attn_fwd(proj, gq2, gk2, bias, *, bl, s, datt, qoff, name):
    n = bl * s
    nq, nb, qs, ks, vs = _attn_specs(bl, s, datt, qoff)

    def body(q_ref, k0, k1, k2, v0, v1, v2, gq_ref, gk_ref, bias_ref, o_ref):
        first, _, _, _, _, _, ps = _attn_probs(q_ref, (k0, k1, k2), gq_ref, gk_ref, bias_ref)
        vc = jnp.concatenate([v0[...], v1[...], v2[...]], axis=0)
        o0 = _dot(ps[0].astype(BF16), vc)
        o1 = _dot(ps[1].astype(BF16), vc)
        o_ref[...] = jnp.where(first, o0, o1).astype(BF16)

    gs = pl.BlockSpec((1, LANES), lambda b, h, q: (0, 0))
    return pl.pallas_call(
        body, grid=(bl, nb, nq),
        in_specs=[qs, *ks, *vs, gs, gs, pl.BlockSpec((2, ATT_TQ, 3 * ATT_TQ), lambda b, h, q: (h, 0, 0))],
        out_specs=pl.BlockSpec((ATT_TQ, LANES), lambda b, h, q: (b * nq + q, h)),
        out_shape=_sds((n, datt), BF16),
        compiler_params=_cp("parallel", "parallel", "arbitrary"), name=name)(
            proj, proj, proj, proj, proj, proj, proj, gq2, gk2, bias)


def _attn_bwd(do, proj, gq2, gk2, bias, *, bl, s, datt, qoff, name):
    n = bl * s
    nq, nb, qs, ks, vs = _attn_specs(bl, s, datt, qoff)
    srows = s + 2 * ATT_TQ
    scale = HEAD_DIM ** -0.5

    def body(do_ref, q_ref, k0, k1, k2, v0, v1, v2, gq_ref, gk_ref, bias_ref,
             dq_ref, dk_ref, dv_ref, db_ref, dgq_ref):
        qt = pl.program_id(2)

        @pl.when(qt == 0)
        def _():
            dk_ref[...] = jnp.zeros_like(dk_ref)
            dv_ref[...] = jnp.zeros_like(dv_ref)
            db_ref[...] = jnp.zeros_like(db_ref)
            dgq_ref[...] = jnp.zeros_like(dgq_ref)

        first, masks, qh, rq, kn, qas, ps = _attn_probs(q_ref, (k0, k1, k2), gq_ref, gk_ref, bias_ref)
        vc = jnp.concatenate([v0[...], v1[...], v2[...]], axis=0)
        dov = do_ref[...]
        dqn = jnp.zeros((ATT_TQ, LANES), F32)
        dkn = jnp.zeros((3 * ATT_TQ, LANES), F32)
        dv = jnp.zeros((3 * ATT_TQ, LANES), F32)
        for hh in range(2):
            doa = jnp.where(masks[hh], dov, jnp.zeros_like(dov))
            p = ps[hh]
            dp = _dot_t1(doa, vc)
            dsm = p * (dp - jnp.sum(dp * p, axis=-1, keepdims=True))
            db_ref[hh] += dsm
            dsc = dsm.astype(BF16)
            dqn = dqn + scale * _dot(dsc, jnp.where(masks[hh], kn, jnp.zeros_like(kn)))
            dkn = dkn + _dot_t0(dsc, qas[hh])
            dv = dv + _dot_t0(p.astype(BF16), doa)
        start = pl.multiple_of(qt * ATT_TQ, ATT_TQ)
        dk_ref[pl.ds(start, 3 * ATT_TQ), :] += dkn
        dv_ref[pl.ds(start, 3 * ATT_TQ), :] += dv
        dgq_ref[...] += jnp.sum(dqn * qh, axis=0, keepdims=True)
        dqh = dqn * gq_ref[...]
        t = dqh * qh
        ma = jnp.sum(jnp.where(first, t, 0.0), axis=-1, keepdims=True) * (1.0 / HEAD_DIM)
        mb = jnp.sum(jnp.where(first, 0.0, t), axis=-1, keepdims=True) * (1.0 / HEAD_DIM)
        dq_ref[...] = (rq * (dqh - qh * jnp.where(first, ma, mb))).astype(BF16)

    gs = pl.BlockSpec((1, LANES), lambda b, h, q: (0, 0))
    acc = pl.BlockSpec((None, srows, LANES), lambda b, h, q: (b, 0, h))
    return pl.pallas_call(
        body, grid=(bl, nb, nq),
        in_specs=[pl.BlockSpec((ATT_TQ, LANES), lambda b, h, q: (b * nq + q, h)), qs, *ks, *vs, gs, gs,
                  pl.BlockSpec((2, ATT_TQ, 3 * ATT_TQ), lambda b, h, q: (h, 0, 0))],
        out_specs=[pl.BlockSpec((ATT_TQ, LANES), lambda b, h, q: (b * nq + q, h)), acc, acc,
                   pl.BlockSpec((None, 2, ATT_TQ, 3 * ATT_TQ), lambda b, h, q: (b, h, 0, 0)),
                   pl.BlockSpec((None, None, 1, LANES), lambda b, h, q: (b, h, 0, 0))],
        out_shape=[_sds((n, datt), BF16), _sds((bl, srows, datt), F32), _sds((bl, srows, datt), F32),
                   _sds((bl, 2 * nb, ATT_TQ, 3 * ATT_TQ), F32), _sds((bl, nb, 1, LANES), F32)],
        compiler_params=_cp("arbitrary", "arbitrary", "arbitrary"), name=name)(
            do, proj, proj, proj, proj, proj, proj, proj, gq2, gk2, bias)


def _attn_kv_bwd(dkn, dv, proj, gk2, *, bl, s, datt, tm, koff, name):
    n = bl * s
    ns = s // tm
    off = 2 * ATT_TQ // tm
    nb = datt // LANES

    def body(dkn_ref, dv_ref, k_ref, gk_ref, dk_ref, dvo_ref, dgk_ref):
        lane = lax.broadcasted_iota(jnp.int32, (1, LANES), 1)
        first = lane < HEAD_DIM

        @pl.when((pl.program_id(0) == 0) & (pl.program_id(1) == 0) & (pl.program_id(2) == 0))
        def _():
            dgk_ref[...] = jnp.zeros_like(dgk_ref)

        dvo_ref[...] = dv_ref[...].astype(BF16)
        kh, rk = _head_norm(k_ref[...].astype(F32), first)
        dn = dkn_ref[...]
        dgk_ref[...] += jnp.sum(dn * kh, axis=0, keepdims=True)
        dh = dn * gk_ref[...]
        t = dh * kh
        ma = jnp.sum(jnp.where(first, t, 0.0), axis=-1, keepdims=True) * (1.0 / HEAD_DIM)
        mb = jnp.sum(jnp.where(first, 0.0, t), axis=-1, keepdims=True) * (1.0 / HEAD_DIM)
        dk_ref[...] = (rk * (dh - kh * jnp.where(first, ma, mb))).astype(BF16)

    accs = pl.BlockSpec((None, tm, LANES), lambda b, i, c: (b, i + off, c))
    outs = pl.BlockSpec((tm, LANES), lambda b, i, c: (b * ns + i, c))
    vec = pl.BlockSpec((1, LANES), lambda b, i, c: (0, 0))
    return pl.pallas_call(
        body, grid=(bl, ns, nb),
        in_specs=[accs, accs, pl.BlockSpec((tm, LANES), lambda b, i, c: (b * ns + i, koff + c)), vec],
        out_specs=[outs, outs, vec],
        out_shape=[_sds((n, datt), BF16), _sds((n, datt), BF16), _sds((1, LANES), F32)],
        compiler_params=_cp("arbitrary", "arbitrary", "arbitrary"), name=name)(dkn, dv, proj, gk2)


def _conv_fwd(proj, wdw, bdw, lng, lnb, *, bl, s, t, acol, name):
    n = bl * s
    dc = wdw.shape[1]
    nt = s // t
    hb = t // HALO

    def body(za_ref, zg_ref, ha_ref, hgt_ref, w_ref, b_ref, g_ref, be_ref, hg_ref, hc_ref, o_ref, ext):
        i = pl.program_id(1)
        hg = za_ref[...].astype(F32) * _sigmoid(zg_ref[...].astype(F32))
        live = jnp.where(i > 0, 1.0, 0.0)
        ext[0:HALO, :] = ha_ref[...].astype(F32) * _sigmoid(hgt_ref[...].astype(F32)) * live
        ext[HALO:HALO + t, :] = hg
        hg_ref[...] = hg
        acc = jnp.zeros((t, dc), F32) + b_ref[...]
        for j in range(CONV_W):
            acc = acc + w_ref[j:j + 1, :] * ext[pl.ds(HALO - (CONV_W - 1) + j, t), :]
        hc_ref[...] = acc
        mu = jnp.mean(acc, axis=-1, keepdims=True)
        xc = acc - mu
        rs = lax.rsqrt(jnp.mean(xc * xc, axis=-1, keepdims=True) + EPS)
        ln = xc * rs * g_ref[...] + be_ref[...]
        o_ref[...] = (ln * _sigmoid(ln)).astype(BF16)

    vec = pl.BlockSpec((1, dc), lambda b, i: (0, 0))
    row = pl.BlockSpec((t, dc), lambda b, i: (b * nt + i, 0))
    tile = lambda c: pl.BlockSpec((t, dc), lambda b, i: (b * nt + i, c))
    halo = lambda c: pl.BlockSpec((HALO, dc), lambda b, i: (jnp.maximum((b * nt + i) * hb - 1, 0), c))
    return pl.pallas_call(
        body, grid=(bl, nt),
        in_specs=[tile(acol), tile(acol + 1), halo(acol), halo(acol + 1),
                  pl.BlockSpec((HALO, dc), lambda b, i: (0, 0)), vec, vec, vec],
        out_specs=[row, row, row],
        out_shape=[_sds((n, dc), F32), _sds((n, dc), F32), _sds((n, dc), BF16)],
        scratch_shapes=[pltpu.VMEM((HALO + t, dc), F32)],
        compiler_params=_cp("parallel", "arbitrary"), name=name)(proj, proj, proj, proj, wdw, bdw, lng, lnb)


def _conv_bwd_ln(dco, hc, lng, lnb, *, tm, name):
    n, dc = hc.shape

    def body(d_ref, hc_ref, g_ref, be_ref, dhc_ref, dg_ref, db_ref):
        @pl.when(pl.program_id(0) == 0)
        def _():
            dg_ref[...] = jnp.zeros_like(dg_ref)
            db_ref[...] = jnp.zeros_like(db_ref)

        hcv = hc_ref[...]
        mu = jnp.mean(hcv, axis=-1, keepdims=True)
        xc = hcv - mu
        rs = lax.rsqrt(jnp.mean(xc * xc, axis=-1, keepdims=True) + EPS)
        xh = xc * rs
        ln = xh * g_ref[...] + be_ref[...]
        sg = _sigmoid(ln)
        dln = d_ref[...].astype(F32) * (sg * (1.0 + ln * (1.0 - sg)))
        db_ref[...] += jnp.sum(dln, axis=0, keepdims=True)
        dg_ref[...] += jnp.sum(dln * xh, axis=0, keepdims=True)
        dxh = dln * g_ref[...]
        dhc_ref[...] = rs * (dxh - jnp.mean(dxh, axis=-1, keepdims=True)
                             - xh * jnp.mean(dxh * xh, axis=-1, keepdims=True))

    vec = pl.BlockSpec((1, dc), lambda i: (0, 0))
    row = pl.BlockSpec((tm, dc), lambda i: (i, 0))
    return pl.pallas_call(
        body, grid=(n // tm,), in_specs=[row, row, vec, vec], out_specs=[row, vec, vec],
        out_shape=[_sds((n, dc), F32), _sds((1, dc), F32), _sds((1, dc), F32)],
        compiler_params=_cp("arbitrary"), name=name)(dco, hc, lng, lnb)


def _conv_bwd_dw(dhc, hg, proj, wdw, *, bl, s, t, acol, name):
    n = bl * s
    dc = wdw.shape[1]
    nt = s // t
    hb = t // HALO
    lastblk = n // HALO - 1

    def body(d_ref, dn_ref, hg_ref, hp_ref, za_ref, zg_ref, w_ref, dz_ref, dw_ref, dbias_ref, extd, exth):
        b = pl.program_id(0)
        i = pl.program_id(1)

        @pl.when((b == 0) & (i == 0))
        def _():
            dw_ref[...] = jnp.zeros_like(dw_ref)
            dbias_ref[...] = jnp.zeros_like(dbias_ref)

        dv = d_ref[...]
        extd[0:t, :] = dv
        extd[t:t + HALO, :] = dn_ref[...] * jnp.where(i < nt - 1, 1.0, 0.0)
        exth[0:HALO, :] = hp_ref[...] * jnp.where(i > 0, 1.0, 0.0)
        exth[HALO:HALO + t, :] = hg_ref[...]
        dbias_ref[...] += jnp.sum(dv, axis=0, keepdims=True)
        dhg = jnp.zeros((t, dc), F32)
        for j in range(CONV_W):
            dhg = dhg + w_ref[j:j + 1, :] * extd[pl.ds(CONV_W - 1 - j, t), :]
            dw_ref[j:j + 1, :] += jnp.sum(dv * exth[pl.ds(HALO - (CONV_W - 1) + j, t), :], axis=0, keepdims=True)
        za = za_ref[...].astype(F32)
        sg = _sigmoid(zg_ref[...].astype(F32))
        dz_ref[...] = jnp.concatenate([dhg * sg, dhg * za * sg * (1.0 - sg)], axis=1).astype(BF16)

    row = pl.BlockSpec((t, dc), lambda b, i: (b * nt + i, 0))
    nxt = pl.BlockSpec((HALO, dc), lambda b, i: (jnp.minimum((b * nt + i + 1) * hb, lastblk), 0))
    prv = pl.BlockSpec((HALO, dc), lambda b, i: (jnp.maximum((b * nt + i) * hb - 1, 0), 0))
    wsp = pl.BlockSpec((HALO, dc), lambda b, i: (0, 0))
    tile = lambda c: pl.BlockSpec((t, dc), lambda b, i: (b * nt + i, c))
    return pl.pallas_call(
        body, grid=(bl, nt),
        in_specs=[row, nxt, row, prv, tile(acol), tile(acol + 1), wsp],
        out_specs=[pl.BlockSpec((t, 2 * dc), lambda b, i: (b * nt + i, 0)), wsp,
                   pl.BlockSpec((1, dc), lambda b, i: (0, 0))],
        out_shape=[_sds((n, 2 * dc), BF16), _sds((HALO, dc), F32), _sds((1, dc), F32)],
        scratch_shapes=[pltpu.VMEM((t + HALO, dc), F32), pltpu.VMEM((HALO + t, dc), F32)],
        compiler_params=_cp("arbitrary", "arbitrary"), name=name)(dhc, dhc, hg, hg, proj, proj, wdw)


def _mix_out_fwd(x, brs, gl, bg, wbs, wout, l, *, tm, name):
    n, d = x.shape

    def body(x_ref, s_ref, a_ref, c_ref, g0, g1, g2, bg_ref, ws, wa, wc, wo, o_ref):
        merged = jnp.zeros((tm, d), F32)
        for k, (br, gr, w) in enumerate(((s_ref, g0, ws), (a_ref, g1, wa), (c_ref, g2, wc))):
            gate = _sigmoid(gr[...].astype(F32) + bg_ref[:, k * d:(k + 1) * d])
            merged = merged + gate * _dot(br[...], w[...])
        o_ref[...] = x_ref[...] + _dot(merged.astype(BF16), wo[...])

    row = lambda w: pl.BlockSpec((tm, w), lambda i: (i, 0))
    wsp = lambda a: pl.BlockSpec((None,) + a.shape[1:], lambda i: (l, 0, 0))
    gls = [pl.BlockSpec((tm, d), functools.partial(lambda k, i: (i, k), k)) for k in range(3)]
    return pl.pallas_call(
        body, grid=(n // tm,),
        in_specs=[row(d), *[row(b.shape[1]) for b in brs], *gls, pl.BlockSpec(bg.shape, lambda i: (0, 0)),
                  *[wsp(w) for w in wbs], wsp(wout)],
        out_specs=row(d), out_shape=_sds((n, d), F32),
        compiler_params=_cp("parallel"), name=name)(x, *brs, gl, gl, gl, bg, *wbs, wout)


def _mix_out_bwd(dx, brs, gl, bg, wbs, wout, l, nl, bufs, *, tm, name):
    n, d = dx.shape
    widths = [b.shape[1] for b in brs]

    def body(dx_ref, s_ref, a_ref, c_ref, g0, g1, g2, bg_ref, ws, wa, wc, wo, *rest):
        ds_ref, da_ref, dc_ref, dgl_ref, dbg_ref, dws, dwa, dwc, dwo = rest[-9:]

        @pl.when(pl.program_id(0) == 0)
        def _():
            for r in (dbg_ref, dws, dwa, dwc, dwo):
                r[...] = jnp.zeros_like(r)

        dxb = dx_ref[...].astype(BF16)
        dm = _dot_t1(dxb, wo[...])
        merged = jnp.zeros((tm, d), F32)
        for k, (br, gr, w, dbr, dw) in enumerate(((s_ref, g0, ws, ds_ref, dws), (a_ref, g1, wa, da_ref, dwa),
                                                   (c_ref, g2, wc, dc_ref, dwc))):
            gate = _sigmoid(gr[...].astype(F32) + bg_ref[:, k * d:(k + 1) * d])
            brv = br[...]
            wv = w[...]
            y = _dot(brv, wv)
            merged = merged + gate * y
            dyb = (dm * gate).astype(BF16)
            dbr[...] = _dot_t1(dyb, wv).astype(BF16)
            dw[...] += _dot_t0(brv, dyb)
            dgl = dm * y * gate * (1.0 - gate)
            dgl_ref[:, k * d:(k + 1) * d] = dgl.astype(BF16)
            dbg_ref[:, k * d:(k + 1) * d] += jnp.sum(dgl, axis=0, keepdims=True)
        dwo[...] += _dot_t0(merged.astype(BF16), dxb)

    row = lambda w: pl.BlockSpec((tm, w), lambda i: (i, 0))
    wsp = lambda shape: pl.BlockSpec((None,) + tuple(shape), lambda i: (l, 0, 0))
    gls = [pl.BlockSpec((tm, d), functools.partial(lambda k, i: (i, k), k)) for k in range(3)]
    slabs = [(w, d) for w in widths] + [(d, d)]
    n_in = 12
    extra = [] if bufs is None else list(bufs)
    aliases = {} if bufs is None else {n_in + k: 5 + k for k in range(4)}
    return pl.pallas_call(
        body, grid=(n // tm,),
        in_specs=[row(d), *[row(w) for w in widths], *gls, pl.BlockSpec(bg.shape, lambda i: (0, 0)),
                  *[wsp(w.shape[1:]) for w in wbs], wsp(wout.shape[1:]), *[_ANY for _ in extra]],
        out_specs=[*[row(w) for w in widths], row(3 * d), pl.BlockSpec((1, 3 * d), lambda i: (0, 0)),
                   *[wsp(sh) for sh in slabs]],
        out_shape=[*[_sds((n, w), BF16) for w in widths], _sds((n, 3 * d), BF16), _sds((1, 3 * d), F32),
                   *[_sds((nl,) + sh, F32) for sh in slabs]],
        input_output_aliases=aliases,
        compiler_params=_cp("arbitrary"), name=name)(dx, *brs, gl, gl, gl, bg, *wbs, wout, *extra)


def _adamw(w, g, m, v, *, name):
    r, c = w.shape
    tm = _tile(r, 256)
    c1 = 1.0 - ADAM_B1 ** ADAM_STEP
    c2 = 1.0 - ADAM_B2 ** ADAM_STEP

    def body(w_ref, g_ref, m_ref, v_ref, d_ref, nm_ref, nv_ref):
        gv = g_ref[...]
        mn = ADAM_B1 * m_ref[...] + (1.0 - ADAM_B1) * gv
        vn = ADAM_B2 * v_ref[...] + (1.0 - ADAM_B2) * (gv * gv)
        nm_ref[...] = mn
        nv_ref[...] = vn
        d_ref[...] = -ADAM_LR * ((mn / c1) / (jnp.sqrt(vn / c2) + ADAM_EPS) + ADAM_WD * w_ref[...])

    blk = pl.BlockSpec((tm, c), lambda i: (i, 0))
    return pl.pallas_call(
        body, grid=(r // tm,), in_specs=[blk] * 4, out_specs=[blk] * 3,
        out_shape=[_sds((r, c), F32)] * 3, compiler_params=_cp("parallel"), name=name)(w, g, m, v)


def _add_sibling(g, recv, c_idx, *, name):
    _, a, b = g.shape
    ta = _tile(a, 256)

    def body(c_ref, g_ref, r_ref, o_ref):
        o_ref[...] = (g_ref[...] + r_ref[...]).astype(BF16)

    return pl.pallas_call(
        body,
        grid_spec=pltpu.PrefetchScalarGridSpec(
            num_scalar_prefetch=1, grid=(a // ta,),
            in_specs=[pl.BlockSpec((None, ta, b), lambda i, c_ref: (c_ref[0], i, 0)),
                      pl.BlockSpec((ta, b), lambda i, c_ref: (i, 0))],
            out_specs=pl.BlockSpec((ta, b), lambda i, c_ref: (i, 0))),
        out_shape=_sds((a, b), BF16), compiler_params=_cp("parallel"), name=name)(c_idx, g, recv)


def _add_chips(rsum, parts, axis, s_idx, c_idx, *, name):
    _, a, b = parts.shape
    ta = _tile(a, 256)
    na = a // ta

    def body(s_ref, c_ref, own_ref, p0, p1, p2, p3, o_ref):
        own = own_ref[...].astype(F32)
        terms = [jnp.where(s_ref[0] == s, own, p[...].astype(F32)) for s, p in enumerate((p0, p1, p2, p3))]
        o_ref[...] = ((terms[0] + terms[1]) + terms[2]) + terms[3]

    own_spec = (pl.BlockSpec((ta, b), lambda i, sr, cr: (sr[0] * na + i, 0)) if axis == 1
                else pl.BlockSpec((ta, b), lambda i, sr, cr: (i, sr[0])))
    part_spec = lambda s: pl.BlockSpec((None, ta, b), lambda i, sr, cr: (jnp.where(sr[0] == s, s ^ 1, s), i, 0))
    return pl.pallas_call(
        body,
        grid_spec=pltpu.PrefetchScalarGridSpec(
            num_scalar_prefetch=2, grid=(na,),
            in_specs=[own_spec] + [part_spec(s) for s in range(N_CHIPS)],
            out_specs=pl.BlockSpec((None, ta, b), lambda i, sr, cr: (cr[0], i, 0))),
        out_shape=_sds((2, a, b), F32), compiler_params=_cp("parallel"), name=name)(
            s_idx, c_idx, rsum, parts, parts, parts, parts)


def _place_shard(wloc, axis, s_idx, *, name):
    nl, a, b = wloc.shape
    ta = _tile(a, 256)
    na = a // ta
    full = (nl, a * N_CHIPS, b) if axis == 1 else (nl, a, b * N_CHIPS)

    def body(sc_ref, w_ref, o_ref):
        o_ref[...] = w_ref[...].astype(BF16)

    out_spec = (pl.BlockSpec((None, ta, b), lambda l, i, sc: (l, sc[0] * na + i, 0)) if axis == 1
                else pl.BlockSpec((None, ta, b), lambda l, i, sc: (l, i, sc[0])))
    return pl.pallas_call(
        body,
        grid_spec=pltpu.PrefetchScalarGridSpec(
            num_scalar_prefetch=1, grid=(nl, na),
            in_specs=[pl.BlockSpec((None, ta, b), lambda l, i, sc: (l, i, 0))], out_specs=out_spec),
        out_shape=_sds(full, BF16), compiler_params=_cp("parallel", "parallel"), name=name)(s_idx, wloc)


def _blockdiag(w):
    g, r, c = w.shape
    eye = jnp.eye(g, dtype=w.dtype)
    return (w[:, :, None, :] * eye[:, None, :, None]).reshape(g * r, g * c)


def _s5_prep(lre, lim, log_dt, b_re, b_im, c_re, c_im, d_skip):
    lr = jnp.minimum(lre, -1e-4)
    li = lim
    dt = jnp.exp(log_dt)[:, None]
    mag = jnp.exp(lr * dt)
    ar = mag * jnp.cos(li * dt)
    ai = mag * jnp.sin(li * dt)
    den = lr * lr + li * li
    coef_r = ((ar - 1.0) * lr + ai * li) / den
    coef_i = (ai * lr - (ar - 1.0) * li) / den
    bbar_r = coef_r[..., None] * b_re - coef_i[..., None] * b_im
    bbar_i = coef_r[..., None] * b_im + coef_i[..., None] * b_re
    a = jnp.stack([ar.reshape(-1), ai.reshape(-1)])
    return dict(
        a=a,
        bblk_r=_blockdiag(bbar_r.transpose(0, 2, 1)), bblk_i=_blockdiag(bbar_i.transpose(0, 2, 1)),
        cblk_r=_blockdiag(c_re.transpose(0, 2, 1)), cblk_in=_blockdiag(-c_im.transpose(0, 2, 1)),
        d=d_skip.reshape(1, -1))


def _s5_powers(a):
    ar, ai = a[0], a[1]
    pr, pi = ar, ai
    rows = []
    for _ in range(GROUP_LOG):
        rows += [pr, pi]
        pr, pi = pr * pr - pi * pi, 2.0 * pr * pi
    qr, qi = [ar], [ai]
    for _ in range(SUBLANES - 1):
        qr, qi = qr + [qr[-1] * ar - qi[-1] * ai], qi + [qr[-1] * ai + qi[-1] * ar]
    p8 = jnp.stack(qr + qi)
    q8 = jnp.stack(qr[::-1] + [-v for v in qi[::-1]])
    return jnp.stack(rows), p8, q8


def _bias_table(rel_bias):
    h = rel_bias.shape[0]
    tq, tw = ATT_TQ, 3 * ATT_TQ
    n_hi = tw - 1 - MAX_REL + 1
    n_lo = tq + tw - 1 - n_hi - (2 * MAX_REL - 1)
    fr = jnp.concatenate([
        jnp.broadcast_to(rel_bias[:, 2 * MAX_REL:], (h, n_hi)),
        jnp.flip(rel_bias[:, 1:2 * MAX_REL], axis=1),
        jnp.broadcast_to(rel_bias[:, :1], (h, n_lo)),
        jnp.zeros((h, 1), rel_bias.dtype)], axis=1)
    ln = tq + tw
    flat = jnp.broadcast_to(fr[:, None, :], (h, tq, ln)).reshape(h, tq * ln)[:, :tq * (ln - 1)]
    tab = flat.reshape(h, tq, ln - 1)[:, :, tq - 1:tq - 1 + tw]
    qc = np.arange(tq)[:, None] // CHUNK + N_LEFT
    kc = np.arange(tw)[None, :] // CHUNK
    band = (kc <= qc) & (kc >= qc - N_LEFT)
    return jnp.where(jnp.asarray(band)[None], tab, NEG)


def _small_prep(w, l):
    g, p = w["s5_lambda_re"].shape[1:]
    b_shape, c_shape = (g, p, -1), (g, -1, p)
    sp = _s5_prep(w["s5_lambda_re"][l], w["s5_lambda_im"][l], w["s5_log_dt"][l], w["s5_b_re"][l].reshape(b_shape),
                  w["s5_b_im"][l].reshape(b_shape), w["s5_c_re"][l].reshape(c_shape), w["s5_c_im"][l].reshape(c_shape),
                  w["s5_d"][l])
    return sp, _bias_table(w["attn_rel_bias"][l])


_PREP_KEYS = ("s5_lambda_re", "s5_lambda_im", "s5_log_dt", "s5_b_re", "s5_b_im", "s5_c_re", "s5_c_im", "s5_d",
              "attn_rel_bias")
_BIG_KEYS = {"ffn1_w_up": 2, "ffn1_w_down": 1, "w_in": 2, "s5_w_glu": 2, "w_br_s5": 2, "w_br_attn": 2,
             "w_br_conv": 2, "w_out": 1, "ffn2_w_up": 2, "ffn2_w_down": 1}
_SMALL_KEYS = ("ffn1_norm", "mix_norm", "b_gate", "s5_lambda_re", "s5_lambda_im", "s5_log_dt", "s5_b_re", "s5_b_im",
               "s5_c_re", "s5_c_im", "s5_d", "attn_q_gain", "attn_k_gain", "attn_rel_bias", "conv_w_dw", "conv_b_dw",
               "conv_ln_g", "conv_ln_b", "ffn2_norm")
_WEIGHTS = ("ffn1_norm", "ffn1_w_up", "ffn1_w_down", "mix_norm", "w_in", "b_gate", "s5_lambda_re", "s5_lambda_im",
            "s5_log_dt", "s5_b_re", "s5_b_im", "s5_c_re", "s5_c_im", "s5_d", "s5_w_glu", "w_br_s5", "attn_q_gain",
            "attn_k_gain", "attn_rel_bias", "w_br_attn", "conv_w_dw", "conv_b_dw", "conv_ln_g", "conv_ln_b",
            "w_br_conv", "w_out", "ffn2_norm", "ffn2_w_up", "ffn2_w_down")


def _local_step(x3, target3, w):
    bl, s, d = x3.shape
    nl = w["ffn1_norm"].shape[0]
    dff = w["ffn1_w_down"].shape[1]
    ds5 = w["s5_d"].shape[1]
    datt = w["w_br_attn"].shape[1]
    dc = w["conv_b_dw"].shape[1]
    n = bl * s
    x = x3.reshape(n, d)
    target = target3.reshape(n, d)
    tm = _tile(n, 512)
    tml = _tile(n, 1024)
    tmix = _tile(n, 256)
    ts5 = 256
    tconv = _tile(s, 512)
    tff = dff // 2
    ma = ds5 + 3 * datt + 2 * dc
    tna = ma // 3
    assert (3 * d) % tna == 0 and dff % 2 == 0
    qoff = ds5 // LANES
    koff = (ds5 + datt) // LANES
    acol = (ds5 + 3 * datt) // dc
    wbs = (w["w_br_s5"], w["w_br_attn"], w["w_br_conv"])

    saved = []
    for l in range(nl):
        (sp, bias), prep_vjp = jax.vjp(lambda ww: _small_prep(ww, l), {k: w[k] for k in _PREP_KEYS})
        spb = dict(sp)
        spb["pw"], spb["p8"], spb["q8"] = _s5_powers(lax.stop_gradient(sp["a"]))
        for k in ("bblk_r", "bblk_i", "cblk_r", "cblk_in"):
            spb[k] = sp[k].astype(BF16)
        g1 = w["ffn1_norm"][l][None]
        g2 = w["ffn2_norm"][l][None]
        gm = w["mix_norm"][l][None]
        gq2 = jnp.tile(w["attn_q_gain"][l], 2)[None]
        gk2 = jnp.tile(w["attn_k_gain"][l], 2)[None]
        wdw = jnp.pad(w["conv_w_dw"][l], ((0, HALO - CONV_W), (0, 0)))
        bdw, lng, lnb = w["conv_b_dw"][l][None], w["conv_ln_g"][l][None], w["conv_ln_b"][l][None]
        bg = w["b_gate"][l][None]

        x0 = x
        h1, ab1 = _norm_mm(x0, g1, w["ffn1_w_up"], l, tm=tml, tn=tff, ntiles=4, pieces=2, transposed=False,
                           name=f"ffn1_up_{l}")
        x1 = _ffn_down(ab1, w["ffn1_w_down"], l, x0, tm=tm, tk=tff, name=f"ffn1_down_{l}")
        h2, pa = _norm_mm(x1, gm, w["w_in"], l, tm=tml, tn=tna, ntiles=3, pieces=1, transposed=True, name=f"win_a_{l}")
        pa = pa[0]
        gl = _mm_t(h2, w["w_in"], l, tm=tml, tn=tna, off=3, ntiles=3 * d // tna, name=f"win_g_{l}")
        xr, xi, yp, zg, s5o = _s5_fwd(pa, spb, w["s5_w_glu"], l, bl=bl, s=s, t=ts5, name=f"s5_fwd_{l}")
        atto = _attn_fwd(pa, gq2, gk2, bias, bl=bl, s=s, datt=datt, qoff=qoff, name=f"attn_fwd_{l}")
        hg, hc, convo = _conv_fwd(pa, wdw, bdw, lng, lnb, bl=bl, s=s, t=tconv, acol=acol, name=f"conv_fwd_{l}")
        brs = (s5o, atto, convo)
        x2 = _mix_out_fwd(x1, brs, gl, bg, wbs, w["w_out"], l, tm=tmix, name=f"mix_fwd_{l}")
        h3, ab2 = _norm_mm(x2, g2, w["ffn2_w_up"], l, tm=tml, tn=tff, ntiles=4, pieces=2, transposed=False,
                           name=f"ffn2_up_{l}")
        x = _ffn_down(ab2, w["ffn2_w_down"], l, x2, tm=tm, tk=tff, name=f"ffn2_down_{l}")
        saved.append(dict(spb=spb, bias=bias, prep_vjp=prep_vjp, g1=g1, g2=g2, gm=gm, gq2=gq2, gk2=gk2,
                          wdw=wdw, lng=lng, lnb=lnb, bg=bg, x0=x0, h1=h1, ab1=ab1, x1=x1, h2=h2, pa=pa, gl=gl,
                          xr=xr, xi=xi, yp=yp, zg=zg, hg=hg, hc=hc, brs=brs, x2=x2, h3=h3, ab2=ab2))

    dx, lsum = _loss_grad(x, target, tm=tm, name="loss")
    loss_part = 0.5 * jnp.sum(lsum) / d

    big = {k: None for k in _BIG_KEYS}
    small = {k: [None] * nl for k in _SMALL_KEYS}
    for l in reversed(range(nl)):
        sv = saved[l]

        def ffn_bwd(dx, xin, h, ab, g, tag):
            wu, wd = w[tag + "_w_up"], w[tag + "_w_down"]
            dab, big[tag + "_w_down"] = _ffn_dact(dx, wd, l, ab, nl, big[tag + "_w_down"], tm=tm, tk=tff,
                                                  name=f"{tag}_dact_{l}")
            big[tag + "_w_up"] = _mm_tn(h[None], dab, l, nl, big[tag + "_w_up"], ta=d, tb=tff, tk=tml,
                                        name=f"{tag}_dwu_{l}")
            dxo, dg = _ffn_dx(dab, wu, l, xin, g, dx, tm=tml, tk=tff, name=f"{tag}_dx_{l}")
            small[tag + "_norm"][l] = dg[0]
            return dxo

        dx = ffn_bwd(dx, sv["x2"], sv["h3"], sv["ab2"], sv["g2"], "ffn2")

        mix_keys = ("w_br_s5", "w_br_attn", "w_br_conv", "w_out")
        bufs = None if big["w_out"] is None else [big[k] for k in mix_keys]
        ds5o, datto, dconvo, dgl, dbg, *dws = _mix_out_bwd(
            dx, sv["brs"], sv["gl"], sv["bg"], wbs, w["w_out"], l, nl, bufs, tm=tmix, name=f"mix_bwd_{l}")
        small["b_gate"][l] = dbg[0]
        big.update(zip(mix_keys, dws))

        dhc, dlng, dlnb = _conv_bwd_ln(dconvo, sv["hc"], sv["lng"], sv["lnb"], tm=tm, name=f"conv_bwd_ln_{l}")
        dz, dwdw, dbdw = _conv_bwd_dw(dhc, sv["hg"], sv["pa"], sv["wdw"], bl=bl, s=s, t=tconv, acol=acol,
                                      name=f"conv_bwd_dw_{l}")
        small["conv_w_dw"][l] = dwdw[:CONV_W]
        small["conv_b_dw"][l], small["conv_ln_g"][l], small["conv_ln_b"][l] = dbdw[0], dlng[0], dlnb[0]

        dq, dkn, dvw, dbias, dgq = _attn_bwd(datto, sv["pa"], sv["gq2"], sv["gk2"], sv["bias"], bl=bl, s=s, datt=datt,
                                             qoff=qoff, name=f"attn_bwd_{l}")
        dk, dv, dgk = _attn_kv_bwd(dkn, dvw, sv["pa"], sv["gk2"], bl=bl, s=s, datt=datt, tm=_tile(s, 512), koff=koff,
                                   name=f"attn_kv_bwd_{l}")
        small["attn_q_gain"][l] = jnp.sum(dgq.reshape(-1, HEAD_DIM), axis=0)
        small["attn_k_gain"][l] = jnp.sum(dgk.reshape(-1, HEAD_DIM), axis=0)

        du, dd, dcr, dci, dbr, dbi, da, big["s5_w_glu"] = _s5_bwd(
            ds5o, sv["yp"], sv["zg"], sv["xr"], sv["xi"], sv["pa"], sv["spb"], w["s5_w_glu"], l, nl, big["s5_w_glu"],
            bl=bl, s=s, t=ts5, name=f"s5_bwd_{l}")
        prep_ct = (dict(a=da, bblk_r=dbr, bblk_i=dbi, cblk_r=dcr, cblk_in=dci, d=dd), jnp.sum(dbias, axis=0))
        (dprep,) = sv["prep_vjp"](prep_ct)
        for k in _PREP_KEYS:
            small[k][l] = dprep[k][l]

        dpa = jnp.concatenate([du, dq, dk, dv, dz], axis=1)
        big["w_in"] = _dwin_t(dpa, dgl, sv["h2"], l, nl, big["w_in"], ta=tna, tk=tml, name=f"dwin_{l}")
        dx, dgm = _mix_dx(dpa, dgl, w["w_in"], l, sv["x1"], sv["gm"], dx, tm=tml, tk=tna, name=f"mix_dx_{l}")
        small["mix_norm"][l] = dgm[0]

        dx = ffn_bwd(dx, sv["x0"], sv["h1"], sv["ab1"], sv["g1"], "ffn1")

    small = {k: jnp.stack(v) for k, v in small.items()}
    return loss_part, dx.reshape(bl, s, d), big, small


def _place():
    x, y, c = lax.axis_index("x"), lax.axis_index("y"), lax.axis_index("c")
    chips = [(1 - x, y), (x, 1 - y), (1 - x, 1 - y)]
    return x, y, c, chips


def _remote(src, dst, send_sems, recv_sems, k, dev):
    return pltpu.make_async_remote_copy(src_ref=src, dst_ref=dst, send_sem=send_sems.at[k], recv_sem=recv_sems.at[k],
                                        device_id=dev, device_id_type=MESH)


def _window(ref, lead, s, axis, blk):
    if axis == 1:
        sl = (pl.ds(pl.multiple_of(s * blk, 16), blk), slice(None))
    else:
        sl = (slice(None), pl.ds(pl.multiple_of(s * blk, LANES), blk))
    return ref.at[sl] if lead is None else ref.at[(lead,) + sl]


def _all_gather_weights(fulls, axes, taps):
    nw = len(fulls)
    assert fulls[0].shape[0] == 2
    blks = [f.shape[ax] // N_CHIPS for f, ax in zip(fulls, axes)]

    def body(*refs):
        taps_in = refs[nw]
        outs, taps_out = refs[nw + 1:2 * nw + 1], refs[2 * nw + 1]
        send_sems, recv_sems, local_sem = refs[-3:]
        x, y, c, chips = _place()
        s_me = 2 * x + y
        sibling = (x, y, 1 - c)
        win = lambda i, lyr, s: _window(outs[i], lyr, s, axes[i], blks[i])
        own_taps = pltpu.make_async_copy(taps_in, taps_out.at[s_me], local_sem)
        own_taps.start()
        sends = []
        for i in range(nw):
            for j, (cx, cy) in enumerate(chips):
                mine = win(i, c, s_me)
                sends.append(_remote(mine, mine, send_sems, recv_sems, 6 * i + j, (cx, cy, c)))
        for j, (cx, cy) in enumerate(chips):
            sends.append(_remote(taps_in, taps_out.at[s_me], send_sems, recv_sems, 6 * nw + j, (cx, cy, c)))
        for cp in sends:
            cp.start()
        for i in range(nw):
            for j, (cx, cy) in enumerate(chips):
                piece = win(i, c, 2 * cx + cy)
                _remote(piece, piece, send_sems, recv_sems, 6 * i + j, (cx, cy, c)).wait_recv()
                fw = _remote(piece, piece, send_sems, recv_sems, 6 * i + 3 + j, sibling)
                fw.start()
                sends.append(fw)
        for i in range(nw):
            for j, (cx, cy) in enumerate(chips):
                piece = win(i, 1 - c, 2 * cx + cy)
                _remote(piece, piece, send_sems, recv_sems, 6 * i + 3 + j, sibling).wait_recv()
        for j, (cx, cy) in enumerate(chips):
            slab = taps_out.at[2 * cx + cy]
            _remote(slab, slab, send_sems, recv_sems, 6 * nw + j, (cx, cy, c)).wait_recv()
        for cp in sends:
            cp.wait_send()
        own_taps.wait()

    nsem = 6 * nw + 3
    return pl.pallas_call(
        body, in_specs=[_ANY] * (nw + 1), out_specs=[_ANY] * (nw + 1),
        out_shape=[_sds(f.shape, f.dtype) for f in fulls] + [_sds((N_CHIPS,) + taps.shape, taps.dtype)],
        input_output_aliases={i: i for i in range(nw)},
        scratch_shapes=[pltpu.SemaphoreType.DMA((nsem,)), pltpu.SemaphoreType.DMA((nsem,)), pltpu.SemaphoreType.DMA],
        name="all_gather_weights")(*fulls, taps)


def _rs_swap(grads):
    nw = len(grads)

    def body(*refs):
        ins, outs = refs[:nw], refs[nw:2 * nw]
        send_sems, recv_sems = refs[-2:]
        x, y, c, _ = _place()
        cps = [_remote(ins[i].at[1 - c], outs[i], send_sems, recv_sems, i, (x, y, 1 - c)) for i in range(nw)]
        for cp in cps:
            cp.start()
        for cp in cps:
            cp.wait()

    return pl.pallas_call(
        body, in_specs=[_ANY] * nw, out_specs=[_ANY] * nw, out_shape=[_sds(g.shape[1:], g.dtype) for g in grads],
        scratch_shapes=[pltpu.SemaphoreType.DMA((nw,)), pltpu.SemaphoreType.DMA((nw,))],
        name="rs_swap_layers")(*grads)


def _rs_exchange(rsums, axes):
    nw = len(rsums)
    blks = [r.shape[ax - 1] // N_CHIPS for r, ax in zip(rsums, axes)]
    shard = [tuple(dim // N_CHIPS if i == ax - 1 else dim for i, dim in enumerate(r.shape)) for r, ax in zip(rsums, axes)]

    def body(*refs):
        ins, outs = refs[:nw], refs[nw:2 * nw]
        send_sems, recv_sems = refs[-2:]
        x, y, c, chips = _place()
        s_me = 2 * x + y
        win = lambda i, s: _window(ins[i], None, s, axes[i], blks[i])
        sends = [_remote(win(i, 2 * cx + cy), outs[i].at[s_me], send_sems, recv_sems, 3 * i + j, (cx, cy, c))
                 for i in range(nw) for j, (cx, cy) in enumerate(chips)]
        for cp in sends:
            cp.start()
        for i in range(nw):
            for j, (cx, cy) in enumerate(chips):
                slab = outs[i].at[2 * cx + cy]
                _remote(slab, slab, send_sems, recv_sems, 3 * i + j, (cx, cy, c)).wait_recv()
        for cp in sends:
            cp.wait_send()

    return pl.pallas_call(
        body, in_specs=[_ANY] * nw, out_specs=[_ANY] * nw,
        out_shape=[_sds((N_CHIPS,) + sh, r.dtype) for sh, r in zip(shard, rsums)],
        scratch_shapes=[pltpu.SemaphoreType.DMA((3 * nw,)), pltpu.SemaphoreType.DMA((3 * nw,))],
        name="rs_exchange_chips")(*rsums)


def _rs_join(ts):
    nw = len(ts)

    def body(*refs):
        outs = refs[nw:2 * nw]
        send_sems, recv_sems = refs[-2:]
        x, y, c, _ = _place()
        sends = [_remote(outs[i].at[c], outs[i].at[c], send_sems, recv_sems, i, (x, y, 1 - c)) for i in range(nw)]
        for cp in sends:
            cp.start()
        for i in range(nw):
            slab = outs[i].at[1 - c]
            _remote(slab, slab, send_sems, recv_sems, i, (x, y, 1 - c)).wait_recv()
        for cp in sends:
            cp.wait_send()

    return pl.pallas_call(
        body, in_specs=[_ANY] * nw, out_specs=[_ANY] * nw, out_shape=[_sds(t.shape, t.dtype) for t in ts],
        input_output_aliases={i: i for i in range(nw)},
        scratch_shapes=[pltpu.SemaphoreType.DMA((nw,)), pltpu.SemaphoreType.DMA((nw,))],
        name="rs_join_layers")(*ts)


def _all_reduce_small(arrs):
    na = len(arrs)
    nd = 8

    def body(*refs):
        ins, outs, recvs = refs[:na], refs[na:2 * na], refs[2 * na:3 * na]
        send_sems, recv_sems = refs[-2:]
        x, y, c, _ = _place()
        me = 4 * x + 2 * y + c
        for i in range(na):
            recvs[i][0] = ins[i][...]
        cps = []
        for rel in range(1, nd):
            dev = (1 - x if rel & 4 else x, 1 - y if rel & 2 else y, 1 - c if rel & 1 else c)
            for i in range(na):
                cp = _remote(ins[i], recvs[i].at[rel], send_sems, recv_sems, (rel - 1) * na + i, dev)
                cp.start()
                cps.append(cp)
        for rel in range(1, nd):
            for i in range(na):
                _remote(ins[i], recvs[i].at[rel], send_sems, recv_sems, (rel - 1) * na + i, (x, y, c)).wait_recv()
        for i in range(na):
            acc = recvs[i][me]
            for dv in range(1, nd):
                acc = acc + recvs[i][lax.bitwise_xor(me, dv)]
            outs[i][...] = acc
        for cp in cps:
            cp.wait_send()

    vm = pl.BlockSpec(memory_space=pltpu.VMEM)
    nsem = (nd - 1) * na
    return pl.pallas_call(
        body, in_specs=[vm] * na, out_specs=[vm] * na, out_shape=[_sds(t.shape, F32) for t in arrs],
        scratch_shapes=[pltpu.VMEM((nd,) + t.shape, F32) for t in arrs]
        + [pltpu.SemaphoreType.DMA((nsem,)), pltpu.SemaphoreType.DMA((nsem,))],
        compiler_params=pltpu.CompilerParams(vmem_limit_bytes=VMEM_LIMIT), name="all_reduce_small")(*arrs)


def _adamw_small(ws, gs, ms, vs):
    na = len(ws)
    c1 = 1.0 - ADAM_B1 ** ADAM_STEP
    c2 = 1.0 - ADAM_B2 ** ADAM_STEP

    def body(*refs):
        w_r, g_r, m_r, v_r = (refs[k * na:(k + 1) * na] for k in range(4))
        d_r, nm_r, nv_r = (refs[(4 + k) * na:(5 + k) * na] for k in range(3))
        for i in range(na):
            gv = g_r[i][...]
            mn = ADAM_B1 * m_r[i][...] + (1.0 - ADAM_B1) * gv
            vn = ADAM_B2 * v_r[i][...] + (1.0 - ADAM_B2) * (gv * gv)
            nm_r[i][...] = mn
            nv_r[i][...] = vn
            d_r[i][...] = -ADAM_LR * ((mn / c1) / (jnp.sqrt(vn / c2) + ADAM_EPS) + ADAM_WD * w_r[i][...])

    vm = pl.BlockSpec(memory_space=pltpu.VMEM)
    res = pl.pallas_call(
        body, in_specs=[vm] * (4 * na), out_specs=[vm] * (3 * na), out_shape=[_sds(t.shape, F32) for t in ws] * 3,
        compiler_params=pltpu.CompilerParams(vmem_limit_bytes=VMEM_LIMIT), name="adamw_small")(*ws, *gs, *ms, *vs)
    return res[:na], res[na:2 * na], res[2 * na:]


def kernel(x, ffn1_norm, ffn1_w_up, ffn1_w_down, mix_norm, w_in, b_gate, s5_lambda_re, s5_lambda_im, s5_log_dt, s5_b_re, s5_b_im, s5_c_re, s5_c_im, s5_d, s5_w_glu, w_br_s5, attn_q_gain, attn_k_gain, attn_rel_bias, w_br_attn, conv_w_dw, conv_b_dw, conv_ln_g, conv_ln_b, w_br_conv, w_out, ffn2_norm, ffn2_w_up, ffn2_w_down, loss_target, m_ffn1_norm, m_ffn1_w_up, m_ffn1_w_down, m_mix_norm, m_w_in, m_b_gate, m_s5_lambda_re, m_s5_lambda_im, m_s5_log_dt, m_s5_b_re, m_s5_b_im, m_s5_c_re, m_s5_c_im, m_s5_d, m_s5_w_glu, m_w_br_s5, m_attn_q_gain, m_attn_k_gain, m_attn_rel_bias, m_w_br_attn, m_conv_w_dw, m_conv_b_dw, m_conv_ln_g, m_conv_ln_b, m_w_br_conv, m_w_out, m_ffn2_norm, m_ffn2_w_up, m_ffn2_w_down, v_ffn1_norm, v_ffn1_w_up, v_ffn1_w_down, v_mix_norm, v_w_in, v_b_gate, v_s5_lambda_re, v_s5_lambda_im, v_s5_log_dt, v_s5_b_re, v_s5_b_im, v_s5_c_re, v_s5_c_im, v_s5_d, v_s5_w_glu, v_w_br_s5, v_attn_q_gain, v_attn_k_gain, v_attn_rel_bias, v_w_br_attn, v_conv_w_dw, v_conv_b_dw, v_conv_ln_g, v_conv_ln_b, v_w_br_conv, v_w_out, v_ffn2_norm, v_ffn2_w_up, v_ffn2_w_down):
    a = dict(locals())
    xi, yi, ci = lax.axis_index("x"), lax.axis_index("y"), lax.axis_index("c")
    s_me = 2 * xi + yi
    big_keys = list(_BIG_KEYS)
    axes = [1 if k == "w_in" else _BIG_KEYS[k] for k in big_keys]

    s_idx = s_me.astype(jnp.int32).reshape(1)
    c_idx = ci.astype(jnp.int32).reshape(1)
    placed = [_place_shard(jnp.swapaxes(a[k], 1, 2).astype(BF16) if k == "w_in" else a[k], ax, s_idx,
                           name=f"place_{k}") for k, ax in zip(big_keys, axes)]
    *fulls, taps = _all_gather_weights(placed, axes, a["conv_w_dw"])
    flat = lambda t: t.reshape(t.shape[0], t.shape[1], -1) if t.ndim == 4 else t
    w = {k: flat(a[k]) for k in _WEIGHTS}
    w.update(zip(big_keys, fulls))
    w["conv_w_dw"] = jnp.moveaxis(taps, 0, 2).reshape(taps.shape[1], taps.shape[2], -1)

    loss_part, grad_x, gbig, gsmall = _local_step(a["x"], a["loss_target"], w)
    loss = lax.psum(loss_part, ("x", "y", "c"))

    glist = [gbig[k] for k in big_keys]
    recv = _rs_swap(glist)
    rsums = [_add_sibling(g, r, c_idx, name=f"rs_add_sibling_{k}") for g, r, k in zip(glist, recv, big_keys)]
    parts = _rs_exchange(rsums, axes)
    mine = [_add_chips(r, p, ax, s_idx, c_idx, name=f"rs_add_chips_{k}")
            for r, p, ax, k in zip(rsums, parts, axes, big_keys)]
    gb = dict(zip(big_keys, _rs_join(mine)))
    gb["w_in"] = jnp.swapaxes(gb["w_in"], 1, 2)

    small_keys = list(_SMALL_KEYS)
    gs = dict(zip(small_keys, _all_reduce_small([gsmall[k] for k in small_keys])))
    blk = a["conv_w_dw"].shape[2]
    gs["conv_w_dw"] = lax.dynamic_slice_in_dim(gs["conv_w_dw"], s_me * blk, blk, axis=2)

    delta, new_m, new_v = {}, {}, {}
    for k in big_keys:
        shp = a[k].shape
        two_d = lambda t: t.reshape(-1, shp[-1])
        d_, m_, v_ = _adamw(two_d(a[k]), two_d(gb[k]), two_d(a["m_" + k]), two_d(a["v_" + k]), name=f"adamw_{k}")
        delta[k], new_m[k], new_v[k] = d_.reshape(shp), m_.reshape(shp), v_.reshape(shp)
    res = _adamw_small([flat(a[k]) for k in small_keys], [gs[k] for k in small_keys],
                       [flat(a["m_" + k]) for k in small_keys], [flat(a["v_" + k]) for k in small_keys])
    for dst, vals in zip((delta, new_m, new_v), res):
        dst.update({k: t.reshape(a[k].shape) for k, t in zip(small_keys, vals)})
    grads = {**gb, **{k: t.reshape(a[k].shape) for k, t in gs.items()}}

    return (loss, grad_x, *[grads[k] for k in _WEIGHTS], *[delta[k] for k in _WEIGHTS],
            *[new_m[k] for k in _WEIGHTS], *[new_v[k] for k in _WEIGHTS])
```

```python
import functools
import math

import numpy as np
import jax
import jax.numpy as jnp
from jax import lax
from jax.experimental import pallas as pl
from jax.experimental.pallas import tpu as pltpu

F32 = jnp.float32
BF16 = jnp.bfloat16
EPS = 1e-6
VMEM_LIMIT = 56 * 1024 * 1024
LANES = 128
HEAD_DIM = 64
CHUNK = 64
N_LEFT = 8
MAX_REL = 128
ATT_TQ = 256
CONV_W = 31
HALO = 32
ROW_CHUNK = 256
SUBLANES = 8
GROUP_LOG = 3
NEG = -1e30
N_CHIPS = 4
PACK_COLS = 1024

ADAM_LR = 0.001
ADAM_B1 = 0.9
ADAM_B2 = 0.999
ADAM_EPS = 1e-08
ADAM_WD = 0.01
ADAM_STEP = 10

MESH = pl.DeviceIdType.MESH
_ANY = pl.BlockSpec(memory_space=pl.ANY)


def _cp(*sem):
    return pltpu.CompilerParams(dimension_semantics=sem, vmem_limit_bytes=VMEM_LIMIT)


def _sds(shape, dtype):
    return jax.ShapeDtypeStruct(shape, dtype)


def _tile(n, pref):
    t = min(n, pref)
    while n % t:
        t -= 8
    return t


def _sigmoid(x):
    return jax.nn.sigmoid(x)


_GELU_C = math.sqrt(2.0 / math.pi)


def _gelu(y):
    return 0.5 * y * (1.0 + jnp.tanh(_GELU_C * (y + 0.044715 * y * y * y)))


def _gelu_grad(y):
    th = jnp.tanh(_GELU_C * (y + 0.044715 * y * y * y))
    return 0.5 * (1.0 + th) + 0.5 * y * (1.0 - th * th) * _GELU_C * (1.0 + 3.0 * 0.044715 * y * y)


def _dot(a, b):
    return jnp.dot(a, b, preferred_element_type=F32)


def _dot_t0(a, b):
    return lax.dot_general(a, b, (((0,), (0,)), ((), ())), preferred_element_type=F32)


def _dot_t1(a, b):
    return lax.dot_general(a, b, (((1,), (1,)), ((), ())), preferred_element_type=F32)


def _slab_out(nl, l, shape, buf, n_in):
    sds = _sds((nl,) + tuple(shape), F32)
    if buf is None:
        return [], [], sds, {}
    return [buf], [_ANY], sds, {n_in: 0}


def _norm_mm(x, g, w, l, *, tm, tn, ntiles, pieces, transposed, name):
    n, d = x.shape
    m = ntiles * tn
    mp = m // pieces
    npj = mp // tn

    def body(x_ref, g_ref, w_ref, h_ref, y_ref, h_scr):
        @pl.when(pl.program_id(1) == 0)
        def _():
            for r0 in range(0, tm, ROW_CHUNK):
                rows = slice(r0, r0 + ROW_CHUNK)
                xv = x_ref[rows, :]
                r = lax.rsqrt(jnp.mean(xv * xv, axis=-1, keepdims=True) + EPS)
                hb = (xv * r * g_ref[...]).astype(BF16)
                h_scr[rows, :] = hb
                h_ref[rows, :] = hb

        mm = _dot_t1 if transposed else _dot
        y_ref[...] = mm(h_scr[...], w_ref[...]).astype(BF16)

    wspec = (pl.BlockSpec((None, tn, d), lambda i, j: (l, j, 0)) if transposed
             else pl.BlockSpec((None, d, tn), lambda i, j: (l, 0, j)))
    return pl.pallas_call(
        body, grid=(n // tm, ntiles),
        in_specs=[pl.BlockSpec((tm, d), lambda i, j: (i, 0)), pl.BlockSpec((1, d), lambda i, j: (0, 0)), wspec],
        out_specs=[pl.BlockSpec((tm, d), lambda i, j: (i, 0)),
                   pl.BlockSpec((None, tm, tn), lambda i, j: (j // npj, i, j % npj))],
        out_shape=[_sds((n, d), BF16), _sds((pieces, n, mp), BF16)],
        scratch_shapes=[pltpu.VMEM((tm, d), BF16)],
        compiler_params=_cp("parallel", "arbitrary"), name=name)(x, g, w)


def _mm_t(a, w, l, *, tm, tn, off, ntiles, name):
    n, k = a.shape

    def body(a_ref, w_ref, y_ref):
        y_ref[...] = _dot_t1(a_ref[...], w_ref[...]).astype(BF16)

    return pl.pallas_call(
        body, grid=(n // tm, ntiles),
        in_specs=[pl.BlockSpec((tm, k), lambda i, j: (i, 0)), pl.BlockSpec((None, tn, k), lambda i, j: (l, off + j, 0))],
        out_specs=pl.BlockSpec((tm, tn), lambda i, j: (i, j)),
        out_shape=_sds((n, ntiles * tn), BF16),
        compiler_params=_cp("parallel", "arbitrary"), name=name)(a, w)


def _ffn_down(ab, wd, l, x, *, tm, tk, name):
    _, n, dff = ab.shape
    d = x.shape[1]
    nk = dff // tk

    def body(a_ref, b_ref, wd_ref, x_ref, o_ref, acc):
        k = pl.program_id(1)
        a = a_ref[...].astype(F32)
        b = b_ref[...].astype(F32)
        act = (a * _sigmoid(a) * b).astype(BF16)
        part = _dot(act, wd_ref[pl.ds(pl.multiple_of(k * tk, tk), tk), :])

        @pl.when(k == 0)
        def _():
            acc[...] = part

        @pl.when(k > 0)
        def _():
            acc[...] += part

        @pl.when(k == nk - 1)
        def _():
            o_ref[...] = x_ref[...] + 0.5 * acc[...]

    return pl.pallas_call(
        body, grid=(n // tm, nk),
        in_specs=[pl.BlockSpec((None, tm, tk), lambda i, k: (0, i, k)),
                  pl.BlockSpec((None, tm, tk), lambda i, k: (1, i, k)),
                  pl.BlockSpec((None, dff, d), lambda i, k: (l, 0, 0)),
                  pl.BlockSpec((tm, d), lambda i, k: (i, 0))],
        out_specs=pl.BlockSpec((tm, d), lambda i, k: (i, 0)),
        out_shape=_sds((n, d), F32),
        scratch_shapes=[pltpu.VMEM((tm, d), F32)],
        compiler_params=_cp("parallel", "arbitrary"), name=name)(ab, ab, wd, x)


def _ffn_dact(dx, wd, l, ab, nl, dwd_buf, *, tm, tk, name):
    n, d = dx.shape
    dff = ab.shape[2]
    half = ((tk // LANES + 1) // 2) * LANES
    chunks = ((0, half), (half, tk))

    def body(dx_ref, wd_ref, a_ref, b_ref, *rest):
        dab_ref, dwd_ref = rest[-2:]
        do = (0.5 * dx_ref[...]).astype(BF16)

        @pl.when(pl.program_id(1) == 0)
        def _():
            dwd_ref[...] = jnp.zeros_like(dwd_ref)

        for c0, c1 in chunks:
            dact = _dot_t1(do, wd_ref[c0:c1, :])
            a = a_ref[:, c0:c1].astype(F32)
            b = b_ref[:, c0:c1].astype(F32)
            sg = _sigmoid(a)
            silu = a * sg
            dab_ref[0, :, c0:c1] = (dact * b * (sg * (1.0 + a * (1.0 - sg)))).astype(BF16)
            dab_ref[1, :, c0:c1] = (dact * silu).astype(BF16)
            dwd_ref[c0:c1, :] += _dot_t0((silu * b).astype(BF16), do)

    extra, extra_specs, dwd_shape, aliases = _slab_out(nl, l, (dff, d), dwd_buf, 4)
    aliases = {k: 1 for k in aliases}
    return pl.pallas_call(
        body, grid=(dff // tk, n // tm),
        in_specs=[pl.BlockSpec((tm, d), lambda j, i: (i, 0)),
                  pl.BlockSpec((None, tk, d), lambda j, i: (l, j, 0)),
                  pl.BlockSpec((None, tm, tk), lambda j, i: (0, i, j)),
                  pl.BlockSpec((None, tm, tk), lambda j, i: (1, i, j)), *extra_specs],
        out_specs=[pl.BlockSpec((2, tm, tk), lambda j, i: (0, i, j)),
                   pl.BlockSpec((None, tk, d), lambda j, i: (l, j, 0))],
        out_shape=[_sds((2, n, dff), BF16), dwd_shape],
        input_output_aliases=aliases,
        compiler_params=_cp("arbitrary", "arbitrary"), name=name)(dx, wd, ab, ab, *extra)


def _rms_bwd_epilogue(acc, x_ref, g_ref, dres_ref, dx_ref, dg_ref, i):
    dgp = jnp.zeros(dg_ref.shape, F32)
    for r0 in range(0, acc.shape[0], ROW_CHUNK):
        rows = slice(r0, r0 + ROW_CHUNK)
        dh = acc[rows, :]
        xv = x_ref[rows, :]
        r = lax.rsqrt(jnp.mean(xv * xv, axis=-1, keepdims=True) + EPS)
        xn = xv * r
        dgp = dgp + jnp.sum(dh * xn, axis=0, keepdims=True)
        dxh = dh * g_ref[...]
        dx_ref[rows, :] = dres_ref[rows, :] + r * (dxh - xn * jnp.mean(dxh * xn, axis=-1, keepdims=True))

    @pl.when(i == 0)
    def _():
        dg_ref[...] = dgp

    @pl.when(i > 0)
    def _():
        dg_ref[...] += dgp


def _ffn_dx(dab, wu, l, x, g, dres, *, tm, tk, name):
    p, n, mp = dab.shape
    d = x.shape[1]
    nkp = mp // tk
    nk = p * nkp

    def body(dy_ref, w_ref, x_ref, g_ref, dres_ref, dx_ref, dg_ref, acc):
        k = pl.program_id(1)
        part = _dot_t1(dy_ref[...], w_ref[...])

        @pl.when(k == 0)
        def _():
            acc[...] = part

        @pl.when(k > 0)
        def _():
            acc[...] += part

        @pl.when(k == nk - 1)
        def _():
            _rms_bwd_epilogue(acc, x_ref, g_ref, dres_ref, dx_ref, dg_ref, pl.program_id(0))

    return pl.pallas_call(
        body, grid=(n // tm, nk),
        in_specs=[pl.BlockSpec((None, tm, tk), lambda i, k: (k // nkp, i, k % nkp)),
                  pl.BlockSpec((None, d, tk), lambda i, k: (l, 0, k)),
                  pl.BlockSpec((tm, d), lambda i, k: (i, 0)),
                  pl.BlockSpec((1, d), lambda i, k: (0, 0)),
                  pl.BlockSpec((tm, d), lambda i, k: (i, 0))],
        out_specs=[pl.BlockSpec((tm, d), lambda i, k: (i, 0)), pl.BlockSpec((1, d), lambda i, k: (0, 0))],
        out_shape=[_sds((n, d), F32), _sds((1, d), F32)],
        scratch_shapes=[pltpu.VMEM((tm, d), F32)],
        compiler_params=_cp("arbitrary", "arbitrary"), name=name)(dab, wu, x, g, dres)


def _mix_dx(dpa, dgl, wt, l, x, g, dres, *, tm, tk, name):
    n, d = x.shape
    n1 = dpa.shape[1] // tk
    n2 = dgl.shape[1] // tk
    nk = n1 + n2

    def body(d1_ref, d2_ref, w_ref, x_ref, g_ref, dres_ref, dx_ref, dg_ref, acc):
        k = pl.program_id(1)

        @pl.when(k == 0)
        def _():
            acc[...] = _dot(d1_ref[...], w_ref[...])

        @pl.when((k > 0) & (k < n1))
        def _():
            acc[...] += _dot(d1_ref[...], w_ref[...])

        @pl.when(k >= n1)
        def _():
            acc[...] += _dot(d2_ref[...], w_ref[...])

        @pl.when(k == nk - 1)
        def _():
            _rms_bwd_epilogue(acc, x_ref, g_ref, dres_ref, dx_ref, dg_ref, pl.program_id(0))

    return pl.pallas_call(
        body, grid=(n // tm, nk),
        in_specs=[pl.BlockSpec((tm, tk), lambda i, k: (i, jnp.minimum(k, n1 - 1))),
                  pl.BlockSpec((tm, tk), lambda i, k: (i, jnp.maximum(k - n1, 0))),
                  pl.BlockSpec((None, tk, d), lambda i, k: (l, k, 0)),
                  pl.BlockSpec((tm, d), lambda i, k: (i, 0)),
                  pl.BlockSpec((1, d), lambda i, k: (0, 0)),
                  pl.BlockSpec((tm, d), lambda i, k: (i, 0))],
        out_specs=[pl.BlockSpec((tm, d), lambda i, k: (i, 0)), pl.BlockSpec((1, d), lambda i, k: (0, 0))],
        out_shape=[_sds((n, d), F32), _sds((1, d), F32)],
        scratch_shapes=[pltpu.VMEM((tm, d), F32)],
        compiler_params=_cp("arbitrary", "arbitrary"), name=name)(dpa, dgl, wt, x, g, dres)


def _mm_tn(a, b, l, nl, buf, *, ta, tb, tk, name):
    pa, n, ka = a.shape
    pb, _, kb = b.shape
    nap = ka // ta
    nbp = kb // tb

    def body(a_ref, b_ref, *rest):
        o_ref = rest[-1]

        @pl.when(pl.program_id(2) == 0)
        def _():
            o_ref[...] = jnp.zeros_like(o_ref)

        o_ref[...] += _dot_t0(a_ref[...], b_ref[...])

    extra, extra_specs, out_shape, aliases = _slab_out(nl, l, (pa * ka, pb * kb), buf, 2)
    return pl.pallas_call(
        body, grid=(pa * nap, pb * nbp, n // tk),
        in_specs=[pl.BlockSpec((None, tk, ta), lambda i, j, k: (i // nap, k, i % nap)),
                  pl.BlockSpec((None, tk, tb), lambda i, j, k: (j // nbp, k, j % nbp)), *extra_specs],
        out_specs=pl.BlockSpec((None, ta, tb), lambda i, j, k: (l, i, j)),
        out_shape=out_shape, input_output_aliases=aliases,
        compiler_params=_cp("parallel", "parallel", "arbitrary"), name=name)(a, b, *extra)


def _dwin_t(dpa, dgl, h, l, nl, buf, *, ta, tk, name):
    n, d = h.shape
    n1 = dpa.shape[1] // ta
    n2 = dgl.shape[1] // ta

    def body(a1_ref, a2_ref, h_ref, *rest):
        o_ref = rest[-1]
        i = pl.program_id(0)

        @pl.when(pl.program_id(1) == 0)
        def _():
            o_ref[...] = jnp.zeros_like(o_ref)

        @pl.when(i < n1)
        def _():
            o_ref[...] += _dot_t0(a1_ref[...], h_ref[...])

        @pl.when(i >= n1)
        def _():
            o_ref[...] += _dot_t0(a2_ref[...], h_ref[...])

    extra, extra_specs, out_shape, aliases = _slab_out(nl, l, ((n1 + n2) * ta, d), buf, 3)
    return pl.pallas_call(
        body, grid=(n1 + n2, n // tk),
        in_specs=[pl.BlockSpec((tk, ta), lambda i, k: (jnp.where(i < n1, k, 0), jnp.minimum(i, n1 - 1))),
                  pl.BlockSpec((tk, ta), lambda i, k: (jnp.where(i >= n1, k, 0), jnp.maximum(i - n1, 0))),
                  pl.BlockSpec((tk, d), lambda i, k: (k, 0)), *extra_specs],
        out_specs=pl.BlockSpec((None, ta, d), lambda i, k: (l, i, 0)),
        out_shape=out_shape, input_output_aliases=aliases,
        compiler_params=_cp("parallel", "arbitrary"), name=name)(dpa, dgl, h, *extra)


def _loss_grad(y, t, *, tm, name):
    n, d = y.shape

    def body(y_ref, t_ref, dy_ref, l_ref):
        e = y_ref[...] - t_ref[...]
        dy_ref[...] = e * (1.0 / d)
        part = jnp.sum(e * e, axis=0, keepdims=True)

        @pl.when(pl.program_id(0) == 0)
        def _():
            l_ref[...] = part

        @pl.when(pl.program_id(0) > 0)
        def _():
            l_ref[...] += part

    return pl.pallas_call(
        body, grid=(n // tm,),
        in_specs=[pl.BlockSpec((tm, d), lambda i: (i, 0)), pl.BlockSpec((tm, d), lambda i: (i, 0))],
        out_specs=[pl.BlockSpec((tm, d), lambda i: (i, 0)), pl.BlockSpec((1, d), lambda i: (0, 0))],
        out_shape=[_sds((n, d), F32), _sds((1, d), F32)],
        compiler_params=_cp("arbitrary"), name=name)(y, t)


def _s5_fwd(proj, sp, wglu, l, *, bl, s, t, name):
    n = bl * s
    ds5, gp = sp["bblk_r"].shape
    nt = s // t
    ng = t // SUBLANES
    glog = int(math.log2(ng))

    def body(u_ref, br_ref, bi_ref, pw_ref, p8_ref, cr_ref, ci_ref, d_ref, wg_ref,
             xr_ref, xi_ref, yp_ref, zg_ref, o_ref, carry, st):
        @pl.when(pl.program_id(1) == 0)
        def _():
            carry[...] = jnp.zeros_like(carry)

        u = u_ref[...]
        sub = lax.broadcasted_iota(jnp.int32, (t, gp), 0) % SUBLANES
        xr = _dot(u, br_ref[...])
        xi = _dot(u, bi_ref[...])
        for k in range(GROUP_LOG):
            sh = 1 << k
            pr = pw_ref[2 * k:2 * k + 1, :]
            pi = pw_ref[2 * k + 1:2 * k + 2, :]
            keep = sub >= sh
            sr = jnp.where(keep, pltpu.roll(xr, sh, 0), 0.0)
            si = jnp.where(keep, pltpu.roll(xi, sh, 0), 0.0)
            xr, xi = xr + pr * sr - pi * si, xi + pr * si + pi * sr
        xr_ref[...] = xr
        xi_ref[...] = xi
        grow = lax.broadcasted_iota(jnp.int32, (ng, gp), 0)
        cr = carry[0:1, :]
        ci = carry[1:2, :]
        a8r = pw_ref[2 * GROUP_LOG:2 * GROUP_LOG + 1, :]
        a8i = pw_ref[2 * GROUP_LOG + 1:2 * GROUP_LOG + 2, :]
        head = grow == 0
        for g in range(ng):
            st[g:g + 1, :] = xr_ref[(g + 1) * SUBLANES - 1:(g + 1) * SUBLANES, :]
            st[ng + g:ng + g + 1, :] = xi_ref[(g + 1) * SUBLANES - 1:(g + 1) * SUBLANES, :]
        sr_ = st[0:ng, :] + jnp.where(head, a8r * cr - a8i * ci, 0.0)
        si_ = st[ng:2 * ng, :] + jnp.where(head, a8r * ci + a8i * cr, 0.0)
        for k in range(glog):
            sh = 1 << k
            pr = pw_ref[2 * (GROUP_LOG + k):2 * (GROUP_LOG + k) + 1, :]
            pi = pw_ref[2 * (GROUP_LOG + k) + 1:2 * (GROUP_LOG + k) + 2, :]
            keep = grow >= sh
            tr = jnp.where(keep, pltpu.roll(sr_, sh, 0), 0.0)
            ti = jnp.where(keep, pltpu.roll(si_, sh, 0), 0.0)
            sr_, si_ = sr_ + pr * tr - pi * ti, si_ + pr * ti + pi * tr
        tail = grow == ng - 1
        carry[0:1, :] = jnp.sum(jnp.where(tail, sr_, 0.0), axis=0, keepdims=True)
        carry[1:2, :] = jnp.sum(jnp.where(tail, si_, 0.0), axis=0, keepdims=True)
        st[0:ng, :] = jnp.where(head, cr, pltpu.roll(sr_, 1, 0))
        st[ng:2 * ng, :] = jnp.where(head, ci, pltpu.roll(si_, 1, 0))
        p8r = p8_ref[0:SUBLANES, :]
        p8i = p8_ref[SUBLANES:2 * SUBLANES, :]
        for g in range(ng):
            grp = slice(g * SUBLANES, (g + 1) * SUBLANES)
            pr = st[g:g + 1, :]
            pi = st[ng + g:ng + g + 1, :]
            xr_ref[grp, :] = xr_ref[grp, :] + p8r * pr - p8i * pi
            xi_ref[grp, :] = xi_ref[grp, :] + p8r * pi + p8i * pr
        xr = xr_ref[...]
        xi = xi_ref[...]
        y = _dot(xr.astype(BF16), cr_ref[...]) + _dot(xi.astype(BF16), ci_ref[...]) + d_ref[...] * u.astype(F32)
        yp_ref[...] = y
        zg = _dot(_gelu(y).astype(BF16), wg_ref[...])
        zg_ref[...] = zg
        o_ref[...] = (zg[:, :ds5] * _sigmoid(zg[:, ds5:])).astype(BF16)

    const = lambda shape: pl.BlockSpec(shape, lambda b, i: (0, 0))
    row = lambda w: pl.BlockSpec((t, w), lambda b, i: (b * nt + i, 0))
    return pl.pallas_call(
        body, grid=(bl, nt),
        in_specs=[row(ds5), const((ds5, gp)), const((ds5, gp)), const((2 * (GROUP_LOG + glog), gp)),
                  const((2 * SUBLANES, gp)),
                  const((gp, ds5)), const((gp, ds5)), const((1, ds5)),
                  pl.BlockSpec((None, ds5, 2 * ds5), lambda b, i: (l, 0, 0))],
        out_specs=[row(gp), row(gp), row(ds5), row(2 * ds5), row(ds5)],
        out_shape=[_sds((n, gp), F32), _sds((n, gp), F32), _sds((n, ds5), F32), _sds((n, 2 * ds5), F32),
                   _sds((n, ds5), BF16)],
        scratch_shapes=[pltpu.VMEM((2, gp), F32), pltpu.VMEM((2 * ng, gp), F32)],
        compiler_params=_cp("arbitrary", "arbitrary"), name=name)(
            proj, sp["bblk_r"], sp["bblk_i"], sp["pw"], sp["p8"], sp["cblk_r"], sp["cblk_in"], sp["d"], wglu)


def _s5_bwd(ds, yp, zg, xr, xi, proj, sp, wglu, l, nl, dwg_buf, *, bl, s, t, name):
    n = bl * s
    ds5, gp = sp["bblk_r"].shape
    nt = s // t
    tb = t // 8
    ng = t // SUBLANES
    glog = int(math.log2(ng))

    def body(ds_ref, yp_ref, zg_ref, xr_ref, xi_ref, hr_ref, hi_ref, u_ref, wg_ref, cr_ref, ci_ref,
             br_ref, bi_ref, pw_ref, q8_ref, d_ref, *rest):
        du_ref, dd_ref, dcr_ref, dci_ref, dbr_ref, dbi_ref, da_ref, dwg_ref, carry, gr_scr, gi_scr, st = rest[-12:]
        b = pl.program_id(0)
        i = pl.program_id(1)
        tile = nt - 1 - i

        @pl.when((b == 0) & (i == 0))
        def _():
            for r in (dwg_ref, dd_ref, dcr_ref, dci_ref, dbr_ref, dbi_ref, da_ref):
                r[...] = jnp.zeros_like(r)

        @pl.when(i == 0)
        def _():
            carry[...] = jnp.zeros_like(carry)

        dsv = ds_ref[...].astype(F32)
        zgv = zg_ref[...]
        za = zgv[:, :ds5]
        sg = _sigmoid(zgv[:, ds5:])
        dzg = jnp.concatenate([dsv * sg, dsv * za * sg * (1.0 - sg)], axis=1).astype(BF16)
        y = yp_ref[...]
        dwg_ref[...] += _dot_t0(_gelu(y).astype(BF16), dzg)
        dy = _dot_t1(dzg, wg_ref[...]) * _gelu_grad(y)
        ub = u_ref[...]
        uf = ub.astype(F32)
        dd_ref[...] += jnp.sum(dy * uf, axis=0, keepdims=True)
        dyb = dy.astype(BF16)
        xrv = xr_ref[...]
        xiv = xi_ref[...]
        dcr_ref[...] += _dot_t0(xrv.astype(BF16), dyb)
        dci_ref[...] += _dot_t0(xiv.astype(BF16), dyb)

        rows = lax.broadcasted_iota(jnp.int32, (t, gp), 0)
        sub = rows % SUBLANES
        gr = _dot_t1(dyb, cr_ref[...])
        gi = _dot_t1(dyb, ci_ref[...])
        for k in range(GROUP_LOG):
            sh = 1 << k
            pr = pw_ref[2 * k:2 * k + 1, :]
            pi = pw_ref[2 * k + 1:2 * k + 2, :]
            keep = sub < SUBLANES - sh
            sr = jnp.where(keep, pltpu.roll(gr, t - sh, 0), 0.0)
            si = jnp.where(keep, pltpu.roll(gi, t - sh, 0), 0.0)
            gr, gi = gr + pr * sr + pi * si, gi + pr * si - pi * sr
        gr_scr[...] = gr
        gi_scr[...] = gi
        grow = lax.broadcasted_iota(jnp.int32, (ng, gp), 0)
        cr = carry[0:1, :]
        ci = carry[1:2, :]
        a8r = pw_ref[2 * GROUP_LOG:2 * GROUP_LOG + 1, :]
        a8i = pw_ref[2 * GROUP_LOG + 1:2 * GROUP_LOG + 2, :]
        tail = grow == ng - 1
        for g in range(ng):
            st[g:g + 1, :] = gr_scr[g * SUBLANES:g * SUBLANES + 1, :]
            st[ng + g:ng + g + 1, :] = gi_scr[g * SUBLANES:g * SUBLANES + 1, :]
        sr_ = st[0:ng, :] + jnp.where(tail, a8r * cr + a8i * ci, 0.0)
        si_ = st[ng:2 * ng, :] + jnp.where(tail, a8r * ci - a8i * cr, 0.0)
        for k in range(glog):
            sh = 1 << k
            pr = pw_ref[2 * (GROUP_LOG + k):2 * (GROUP_LOG + k) + 1, :]
            pi = pw_ref[2 * (GROUP_LOG + k) + 1:2 * (GROUP_LOG + k) + 2, :]
            keep = grow < ng - sh
            tr = jnp.where(keep, pltpu.roll(sr_, ng - sh, 0), 0.0)
            ti = jnp.where(keep, pltpu.roll(si_, ng - sh, 0), 0.0)
            sr_, si_ = sr_ + pr * tr + pi * ti, si_ + pr * ti - pi * tr
        head = grow == 0
        carry[0:1, :] = jnp.sum(jnp.where(head, sr_, 0.0), axis=0, keepdims=True)
        carry[1:2, :] = jnp.sum(jnp.where(head, si_, 0.0), axis=0, keepdims=True)
        st[0:ng, :] = jnp.where(tail, cr, pltpu.roll(sr_, ng - 1, 0))
        st[ng:2 * ng, :] = jnp.where(tail, ci, pltpu.roll(si_, ng - 1, 0))
        q8r = q8_ref[0:SUBLANES, :]
        q8i = q8_ref[SUBLANES:2 * SUBLANES, :]
        for g in range(ng):
            grp = slice(g * SUBLANES, (g + 1) * SUBLANES)
            pr = st[g:g + 1, :]
            pi = st[ng + g:ng + g + 1, :]
            gr_scr[grp, :] = gr_scr[grp, :] + q8r * pr - q8i * pi
            gi_scr[grp, :] = gi_scr[grp, :] + q8r * pi + q8i * pr
        gr = gr_scr[...]
        gi = gi_scr[...]
        first = rows == 0

        live = jnp.where(tile > 0, 1.0, 0.0)
        xpr = jnp.where(first, hr_ref[7:8, :] * live, pltpu.roll(xrv, 1, 0))
        xpi = jnp.where(first, hi_ref[7:8, :] * live, pltpu.roll(xiv, 1, 0))
        da_ref[0:1, :] += jnp.sum(gr * xpr + gi * xpi, axis=0, keepdims=True)
        da_ref[1:2, :] += jnp.sum(gi * xpr - gr * xpi, axis=0, keepdims=True)

        grb = gr.astype(BF16)
        gib = gi.astype(BF16)
        dbr_ref[...] += _dot_t0(ub, grb)
        dbi_ref[...] += _dot_t0(ub, gib)
        du_ref[...] = (_dot_t1(grb, br_ref[...]) + _dot_t1(gib, bi_ref[...]) + dy * d_ref[...]).astype(BF16)

    const = lambda shape: pl.BlockSpec(shape, lambda b, i: (0, 0))
    row = lambda w: pl.BlockSpec((t, w), lambda b, i: (b * nt + nt - 1 - i, 0))
    halo = pl.BlockSpec((8, gp), lambda b, i: (jnp.maximum((b * nt + nt - 1 - i) * tb - 1, 0), 0))
    extra, extra_specs, dwg_shape, aliases = _slab_out(nl, l, (ds5, 2 * ds5), dwg_buf, 16)
    aliases = {k: 7 for k in aliases}
    return pl.pallas_call(
        body, grid=(bl, nt),
        in_specs=[row(ds5), row(ds5), row(2 * ds5), row(gp), row(gp), halo, halo, row(ds5),
                  pl.BlockSpec((None, ds5, 2 * ds5), lambda b, i: (l, 0, 0)),
                  const((gp, ds5)), const((gp, ds5)), const((ds5, gp)), const((ds5, gp)),
                  const((2 * (GROUP_LOG + glog), gp)), const((2 * SUBLANES, gp)), const((1, ds5)), *extra_specs],
        out_specs=[row(ds5), const((1, ds5)), const((gp, ds5)), const((gp, ds5)),
                   const((ds5, gp)), const((ds5, gp)), const((2, gp)),
                   pl.BlockSpec((None, ds5, 2 * ds5), lambda b, i: (l, 0, 0))],
        out_shape=[_sds((n, ds5), BF16), _sds((1, ds5), F32), _sds((gp, ds5), F32),
                   _sds((gp, ds5), F32), _sds((ds5, gp), F32), _sds((ds5, gp), F32), _sds((2, gp), F32), dwg_shape],
        input_output_aliases=aliases,
        scratch_shapes=[pltpu.VMEM((2, gp), F32), pltpu.VMEM((t, gp), F32), pltpu.VMEM((t, gp), F32),
                        pltpu.VMEM((2 * ng, gp), F32)],
        compiler_params=_cp("arbitrary", "arbitrary"), name=name)(
            ds, yp, zg, xr, xi, xr, xi, proj, wglu, sp["cblk_r"], sp["cblk_in"],
            sp["bblk_r"], sp["bblk_i"], sp["pw"], sp["q8"], sp["d"], *extra)


def _head_norm(x, first):
    x2 = x * x
    sa = jnp.sum(jnp.where(first, x2, 0.0), axis=-1, keepdims=True)
    sb = jnp.sum(jnp.where(first, 0.0, x2), axis=-1, keepdims=True)
    r = jnp.where(first, lax.rsqrt(sa * (1.0 / HEAD_DIM) + EPS), lax.rsqrt(sb * (1.0 / HEAD_DIM) + EPS))
    return x * r, r


def _attn_specs(bl, s, datt, qoff):
    nq = s // ATT_TQ
    nb = datt // LANES
    col = lambda blk: (lambda b, h, q: (b * nq + q, qoff + blk * nb + h))
    win = lambda blk, j: (lambda b, h, q: (b * nq + jnp.maximum(q - 2 + j, 0), qoff + blk * nb + h))
    tile = lambda f: pl.BlockSpec((ATT_TQ, LANES), f)
    qs = tile(col(0))
    ks = [tile(win(1, j)) for j in range(3)]
    vs = [tile(win(2, j)) for j in range(3)]
    return nq, nb, qs, ks, vs


def _attn_probs(q_ref, k_refs, gq_ref, gk_ref, bias_ref):
    qt = pl.program_id(2)
    lane = lax.broadcasted_iota(jnp.int32, (1, LANES), 1)
    first = lane < HEAD_DIM
    qh, rq = _head_norm(q_ref[...].astype(F32), first)
    qn = qh * gq_ref[...]
    kc = jnp.concatenate([r[...] for r in k_refs], axis=0).astype(F32)
    kh, _ = _head_norm(kc, first)
    kn = (kh * gk_ref[...]).astype(BF16)
    kpos = (qt - 2) * ATT_TQ + lax.broadcasted_iota(jnp.int32, (1, 3 * ATT_TQ), 1)
    valid = kpos >= 0
    scale = HEAD_DIM ** -0.5
    masks = (first, jnp.logical_not(first))
    qas, ps = [], []
    for hh in range(2):
        qa = jnp.where(masks[hh], qn, 0.0).astype(BF16)
        sc = _dot_t1(qa, kn) * scale + bias_ref[hh]
        sc = jnp.where(valid, sc, NEG)
        e = jnp.exp(sc - jnp.max(sc, axis=-1, keepdims=True))
        ps.append(e / jnp.sum(e, axis=-1, keepdims=True))
        qas.append(qa)
    return first, masks, qh, rq, kn, qas, ps


def _attn_fwd(proj, gq2, gk2, bias, *, bl, s, datt, qoff, name):
    n = bl * s
    nq, nb, qs, ks, vs = _attn_specs(bl, s, datt, qoff)

    def body(q_ref, k0, k1, k2, v0, v1, v2, gq_ref, gk_ref, bias_ref, o_ref):
        first, _, _, _, _, _, ps = _attn_probs(q_ref, (k0, k1, k2), gq_ref, gk_ref, bias_ref)
        vc = jnp.concatenate([v0[...], v1[...], v2[...]], axis=0)
        o0 = _dot(ps[0].astype(BF16), vc)
        o1 = _dot(ps[1].astype(BF16), vc)
        o_ref[...] = jnp.where(first, o0, o1).astype(BF16)

    gs = pl.BlockSpec((1, LANES), lambda b, h, q: (0, 0))
    return pl.pallas_call(
        body, grid=(bl, nb, nq),
        in_specs=[qs, *ks, *vs, gs, gs, pl.BlockSpec((2, ATT_TQ, 3 * ATT_TQ), lambda b, h, q: (h, 0, 0))],
        out_specs=pl.BlockSpec((ATT_TQ, LANES), lambda b, h, q: (b * nq + q, h)),
        out_shape=_sds((n, datt), BF16),
        compiler_params=_cp("parallel", "parallel", "arbitrary"), name=name)(
            proj, proj, proj, proj, proj, proj, proj, gq2, gk2, bias)


def _attn_bwd(do, proj, gq2, gk2, bias, *, bl, s, datt, qoff, name, comm=None):
    n = bl * s
    nq, nb, qs, ks, vs = _attn_specs(bl, s, datt, qoff)
    srows = s + 2 * ATT_TQ
    scale = HEAD_DIM ** -0.5
    nc = 0 if comm is None else len(comm[0])

    def body(do_ref, q_ref, k0, k1, k2, v0, v1, v2, gq_ref, gk_ref, bias_ref, *rest):
        dq_ref, dk_ref, dv_ref, db_ref, dgq_ref = rest[nc:nc + 5]
        qt = pl.program_id(2)
        if comm is not None:
            start, wait = _exchange_ops(rest[:nc], rest[nc + 5:2 * nc + 5], rest[-2], rest[-1], comm[1], comm[2])
            step = (pl.program_id(0) * nb + pl.program_id(1)) * nq + qt

            @pl.when(step == 0)
            def _():
                start()

        @pl.when(qt == 0)
        def _():
            dk_ref[...] = jnp.zeros_like(dk_ref)
            dv_ref[...] = jnp.zeros_like(dv_ref)
            db_ref[...] = jnp.zeros_like(db_ref)
            dgq_ref[...] = jnp.zeros_like(dgq_ref)

        first, masks, qh, rq, kn, qas, ps = _attn_probs(q_ref, (k0, k1, k2), gq_ref, gk_ref, bias_ref)
        vc = jnp.concatenate([v0[...], v1[...], v2[...]], axis=0)
        dov = do_ref[...]
        dqn = jnp.zeros((ATT_TQ, LANES), F32)
        dkn = jnp.zeros((3 * ATT_TQ, LANES), F32)
        dv = jnp.zeros((3 * ATT_TQ, LANES), F32)
        for hh in range(2):
            doa = jnp.where(masks[hh], dov, jnp.zeros_like(dov))
            p = ps[hh]
            dp = _dot_t1(doa, vc)
            dsm = p * (dp - jnp.sum(dp * p, axis=-1, keepdims=True))
            db_ref[hh] += dsm
            dsc = (dsm * scale).astype(BF16)
            dqn = dqn + _dot(dsc, jnp.where(masks[hh], kn, jnp.zeros_like(kn)))
            dkn = dkn + _dot_t0(dsc, qas[hh])
            dv = dv + _dot_t0(p.astype(BF16), doa)
        start = pl.multiple_of(qt * ATT_TQ, ATT_TQ)
        dk_ref[pl.ds(start, 3 * ATT_TQ), :] += dkn
        dv_ref[pl.ds(start, 3 * ATT_TQ), :] += dv
        dgq_ref[...] += jnp.sum(dqn * qh, axis=0, keepdims=True)
        dqh = dqn * gq_ref[...]
        t = dqh * qh
        ma = jnp.sum(jnp.where(first, t, 0.0), axis=-1, keepdims=True) * (1.0 / HEAD_DIM)
        mb = jnp.sum(jnp.where(first, 0.0, t), axis=-1, keepdims=True) * (1.0 / HEAD_DIM)
        dq_ref[...] = (rq * (dqh - qh * jnp.where(first, ma, mb))).astype(BF16)
        if comm is not None:
            @pl.when(step == bl * nb * nq - 1)
            def _():
                wait()

    gs = pl.BlockSpec((1, LANES), lambda b, h, q: (0, 0))
    acc = pl.BlockSpec((None, srows, LANES), lambda b, h, q: (b, 0, h))
    comm_in = [] if comm is None else list(comm[0])
    comm_out = [] if comm is None else _parts_shapes(comm[0], comm[1])
    comm_scr = [] if comm is None else [pltpu.SemaphoreType.DMA((3 * nc,)), pltpu.SemaphoreType.DMA((3 * nc,))]
    return pl.pallas_call(
        body, grid=(bl, nb, nq),
        in_specs=[pl.BlockSpec((ATT_TQ, LANES), lambda b, h, q: (b * nq + q, h)), qs, *ks, *vs, gs, gs,
                  pl.BlockSpec((2, ATT_TQ, 3 * ATT_TQ), lambda b, h, q: (h, 0, 0))] + [_ANY] * nc,
        out_specs=[pl.BlockSpec((ATT_TQ, LANES), lambda b, h, q: (b * nq + q, h)), acc, acc,
                   pl.BlockSpec((None, 2, ATT_TQ, 3 * ATT_TQ), lambda b, h, q: (b, h, 0, 0)),
                   pl.BlockSpec((None, None, 1, LANES), lambda b, h, q: (b, h, 0, 0))] + [_ANY] * nc,
        out_shape=[_sds((n, datt), BF16), _sds((bl, srows, datt), F32), _sds((bl, srows, datt), F32),
                   _sds((bl, 2 * nb, ATT_TQ, 3 * ATT_TQ), F32), _sds((bl, nb, 1, LANES), F32)] + comm_out,
        scratch_shapes=comm_scr,
        compiler_params=_cp("arbitrary", "arbitrary", "arbitrary"), name=name)(
            do, proj, proj, proj, proj, proj, proj, proj, gq2, gk2, bias, *comm_in)


def _attn_kv_bwd(dkn, dv, proj, gk2, *, bl, s, datt, tm, koff, name):
    n = bl * s
    ns = s // tm
    off = 2 * ATT_TQ // tm
    nb = datt // LANES

    def body(dkn_ref, dv_ref, k_ref, gk_ref, dk_ref, dvo_ref, dgk_ref):
        lane = lax.broadcasted_iota(jnp.int32, (1, LANES), 1)
        first = lane < HEAD_DIM

        @pl.when((pl.program_id(0) == 0) & (pl.program_id(1) == 0) & (pl.program_id(2) == 0))
        def _():
            dgk_ref[...] = jnp.zeros_like(dgk_ref)

        dvo_ref[...] = dv_ref[...].astype(BF16)
        kh, rk = _head_norm(k_ref[...].astype(F32), first)
        dn = dkn_ref[...]
        dgk_ref[...] += jnp.sum(dn * kh, axis=0, keepdims=True)
        dh = dn * gk_ref[...]
        t = dh * kh
        ma = jnp.sum(jnp.where(first, t, 0.0), axis=-1, keepdims=True) * (1.0 / HEAD_DIM)
        mb = jnp.sum(jnp.where(first, 0.0, t), axis=-1, keepdims=True) * (1.0 / HEAD_DIM)
        dk_ref[...] = (rk * (dh - kh * jnp.where(first, ma, mb))).astype(BF16)

    accs = pl.BlockSpec((None, tm, LANES), lambda b, i, c: (b, i + off, c))
    outs = pl.BlockSpec((tm, LANES), lambda b, i, c: (b * ns + i, c))
    vec = pl.BlockSpec((1, LANES), lambda b, i, c: (0, 0))
    return pl.pallas_call(
        body, grid=(bl, ns, nb),
        in_specs=[accs, accs, pl.BlockSpec((tm, LANES), lambda b, i, c: (b * ns + i, koff + c)), vec],
        out_specs=[outs, outs, vec],
        out_shape=[_sds((n, datt), BF16), _sds((n, datt), BF16), _sds((1, LANES), F32)],
        compiler_params=_cp("arbitrary", "arbitrary", "arbitrary"), name=name)(dkn, dv, proj, gk2)


def _conv_fwd(proj, wdw, bdw, lng, lnb, *, bl, s, t, acol, name):
    n = bl * s
    dc = wdw.shape[1]
    nt = s // t
    hb = t // HALO

    def body(za_ref, zg_ref, ha_ref, hgt_ref, w_ref, b_ref, g_ref, be_ref, hg_ref, hc_ref, o_ref, ext):
        i = pl.program_id(1)
        hg = za_ref[...].astype(F32) * _sigmoid(zg_ref[...].astype(F32))
        live = jnp.where(i > 0, 1.0, 0.0)
        ext[0:HALO, :] = ha_ref[...].astype(F32) * _sigmoid(hgt_ref[...].astype(F32)) * live
        ext[HALO:HALO + t, :] = hg
        hg_ref[...] = hg
        acc = jnp.zeros((t, dc), F32) + b_ref[...]
        for j in range(CONV_W):
            acc = acc + w_ref[j:j + 1, :] * ext[pl.ds(HALO - (CONV_W - 1) + j, t), :]
        hc_ref[...] = acc
        mu = jnp.mean(acc, axis=-1, keepdims=True)
        xc = acc - mu
        rs = lax.rsqrt(jnp.mean(xc * xc, axis=-1, keepdims=True) + EPS)
        ln = xc * rs * g_ref[...] + be_ref[...]
        o_ref[...] = (ln * _sigmoid(ln)).astype(BF16)

    vec = pl.BlockSpec((1, dc), lambda b, i: (0, 0))
    row = pl.BlockSpec((t, dc), lambda b, i: (b * nt + i, 0))
    tile = lambda c: pl.BlockSpec((t, dc), lambda b, i: (b * nt + i, c))
    halo = lambda c: pl.BlockSpec((HALO, dc), lambda b, i: (jnp.maximum((b * nt + i) * hb - 1, 0), c))
    return pl.pallas_call(
        body, grid=(bl, nt),
        in_specs=[tile(acol), tile(acol + 1), halo(acol), halo(acol + 1),
                  pl.BlockSpec((HALO, dc), lambda b, i: (0, 0)), vec, vec, vec],
        out_specs=[row, row, row],
        out_shape=[_sds((n, dc), F32), _sds((n, dc), F32), _sds((n, dc), BF16)],
        scratch_shapes=[pltpu.VMEM((HALO + t, dc), F32)],
        compiler_params=_cp("parallel", "arbitrary"), name=name)(proj, proj, proj, proj, wdw, bdw, lng, lnb)


def _conv_bwd_ln(dco, hc, lng, lnb, *, tm, name):
    n, dc = hc.shape

    def body(d_ref, hc_ref, g_ref, be_ref, dhc_ref, dg_ref, db_ref):
        @pl.when(pl.program_id(0) == 0)
        def _():
            dg_ref[...] = jnp.zeros_like(dg_ref)
            db_ref[...] = jnp.zeros_like(db_ref)

        hcv = hc_ref[...]
        mu = jnp.mean(hcv, axis=-1, keepdims=True)
        xc = hcv - mu
        rs = lax.rsqrt(jnp.mean(xc * xc, axis=-1, keepdims=True) + EPS)
        xh = xc * rs
        ln = xh * g_ref[...] + be_ref[...]
        sg = _sigmoid(ln)
        dln = d_ref[...].astype(F32) * (sg * (1.0 + ln * (1.0 - sg)))
        db_ref[...] += jnp.sum(dln, axis=0, keepdims=True)
        dg_ref[...] += jnp.sum(dln * xh, axis=0, keepdims=True)
        dxh = dln * g_ref[...]
        dhc_ref[...] = rs * (dxh - jnp.mean(dxh, axis=-1, keepdims=True)
                             - xh * jnp.mean(dxh * xh, axis=-1, keepdims=True))

    vec = pl.BlockSpec((1, dc), lambda i: (0, 0))
    row = pl.BlockSpec((tm, dc), lambda i: (i, 0))
    return pl.pallas_call(
        body, grid=(n // tm,), in_specs=[row, row, vec, vec], out_specs=[row, vec, vec],
        out_shape=[_sds((n, dc), F32), _sds((1, dc), F32), _sds((1, dc), F32)],
        compiler_params=_cp("arbitrary"), name=name)(dco, hc, lng, lnb)


def _conv_bwd_dw(dhc, hg, proj, wdw, *, bl, s, t, acol, name):
    n = bl * s
    dc = wdw.shape[1]
    nt = s // t
    hb = t // HALO
    lastblk = n // HALO - 1

    def body(d_ref, dn_ref, hg_ref, hp_ref, za_ref, zg_ref, w_ref, dz_ref, dw_ref, dbias_ref, extd, exth):
        b = pl.program_id(0)
        i = pl.program_id(1)

        @pl.when((b == 0) & (i == 0))
        def _():
            dw_ref[...] = jnp.zeros_like(dw_ref)
            dbias_ref[...] = jnp.zeros_like(dbias_ref)

        dv = d_ref[...]
        extd[0:t, :] = dv
        extd[t:t + HALO, :] = dn_ref[...] * jnp.where(i < nt - 1, 1.0, 0.0)
        exth[0:HALO, :] = hp_ref[...] * jnp.where(i > 0, 1.0, 0.0)
        exth[HALO:HALO + t, :] = hg_ref[...]
        dbias_ref[...] += jnp.sum(dv, axis=0, keepdims=True)
        dhg = jnp.zeros((t, dc), F32)
        for j in range(CONV_W):
            dhg = dhg + w_ref[j:j + 1, :] * extd[pl.ds(CONV_W - 1 - j, t), :]
            dw_ref[j:j + 1, :] += jnp.sum(dv * exth[pl.ds(HALO - (CONV_W - 1) + j, t), :], axis=0, keepdims=True)
        za = za_ref[...].astype(F32)
        sg = _sigmoid(zg_ref[...].astype(F32))
        dz_ref[...] = jnp.concatenate([dhg * sg, dhg * za * sg * (1.0 - sg)], axis=1).astype(BF16)

    row = pl.BlockSpec((t, dc), lambda b, i: (b * nt + i, 0))
    nxt = pl.BlockSpec((HALO, dc), lambda b, i: (jnp.minimum((b * nt + i + 1) * hb, lastblk), 0))
    prv = pl.BlockSpec((HALO, dc), lambda b, i: (jnp.maximum((b * nt + i) * hb - 1, 0), 0))
    wsp = pl.BlockSpec((HALO, dc), lambda b, i: (0, 0))
    tile = lambda c: pl.BlockSpec((t, dc), lambda b, i: (b * nt + i, c))
    return pl.pallas_call(
        body, grid=(bl, nt),
        in_specs=[row, nxt, row, prv, tile(acol), tile(acol + 1), wsp],
        out_specs=[pl.BlockSpec((t, 2 * dc), lambda b, i: (b * nt + i, 0)), wsp,
                   pl.BlockSpec((1, dc), lambda b, i: (0, 0))],
        out_shape=[_sds((n, 2 * dc), BF16), _sds((HALO, dc), F32), _sds((1, dc), F32)],
        scratch_shapes=[pltpu.VMEM((t + HALO, dc), F32), pltpu.VMEM((HALO + t, dc), F32)],
        compiler_params=_cp("arbitrary", "arbitrary"), name=name)(dhc, dhc, hg, hg, proj, proj, wdw)


def _mix_out_fwd(x, brs, gl, bg, wbs, wout, l, *, tm, name):
    n, d = x.shape

    def body(x_ref, s_ref, a_ref, c_ref, g0, g1, g2, bg_ref, ws, wa, wc, wo, o_ref):
        merged = jnp.zeros((tm, d), F32)
        for k, (br, gr, w) in enumerate(((s_ref, g0, ws), (a_ref, g1, wa), (c_ref, g2, wc))):
            gate = _sigmoid(gr[...].astype(F32) + bg_ref[:, k * d:(k + 1) * d])
            merged = merged + gate * _dot(br[...], w[...])
        o_ref[...] = x_ref[...] + _dot(merged.astype(BF16), wo[...])

    row = lambda w: pl.BlockSpec((tm, w), lambda i: (i, 0))
    wsp = lambda a: pl.BlockSpec((None,) + a.shape[1:], lambda i: (l, 0, 0))
    gls = [pl.BlockSpec((tm, d), functools.partial(lambda k, i: (i, k), k)) for k in range(3)]
    return pl.pallas_call(
        body, grid=(n // tm,),
        in_specs=[row(d), *[row(b.shape[1]) for b in brs], *gls, pl.BlockSpec(bg.shape, lambda i: (0, 0)),
                  *[wsp(w) for w in wbs], wsp(wout)],
        out_specs=row(d), out_shape=_sds((n, d), F32),
        compiler_params=_cp("parallel"), name=name)(x, *brs, gl, gl, gl, bg, *wbs, wout)


def _mix_out_bwd(dx, brs, gl, bg, wbs, wout, l, nl, bufs, *, tm, name):
    n, d = dx.shape
    widths = [b.shape[1] for b in brs]

    def body(dx_ref, s_ref, a_ref, c_ref, g0, g1, g2, bg_ref, ws, wa, wc, wo, *rest):
        ds_ref, da_ref, dc_ref, dgl_ref, dbg_ref, dws, dwa, dwc, dwo = rest[-9:]

        @pl.when(pl.program_id(0) == 0)
        def _():
            for r in (dbg_ref, dws, dwa, dwc, dwo):
                r[...] = jnp.zeros_like(r)

        dxb = dx_ref[...].astype(BF16)
        dm = _dot_t1(dxb, wo[...])
        merged = jnp.zeros((tm, d), F32)
        for k, (br, gr, w, dbr, dw) in enumerate(((s_ref, g0, ws, ds_ref, dws), (a_ref, g1, wa, da_ref, dwa),
                                                   (c_ref, g2, wc, dc_ref, dwc))):
            gate = _sigmoid(gr[...].astype(F32) + bg_ref[:, k * d:(k + 1) * d])
            brv = br[...]
            wv = w[...]
            y = _dot(brv, wv)
            merged = merged + gate * y
            dyb = (dm * gate).astype(BF16)
            dbr[...] = _dot_t1(dyb, wv).astype(BF16)
            dw[...] += _dot_t0(brv, dyb)
            dgl = dm * y * gate * (1.0 - gate)
            dgl_ref[:, k * d:(k + 1) * d] = dgl.astype(BF16)
            dbg_ref[:, k * d:(k + 1) * d] += jnp.sum(dgl, axis=0, keepdims=True)
        dwo[...] += _dot_t0(merged.astype(BF16), dxb)

    row = lambda w: pl.BlockSpec((tm, w), lambda i: (i, 0))
    wsp = lambda shape: pl.BlockSpec((None,) + tuple(shape), lambda i: (l, 0, 0))
    gls = [pl.BlockSpec((tm, d), functools.partial(lambda k, i: (i, k), k)) for k in range(3)]
    slabs = [(w, d) for w in widths] + [(d, d)]
    n_in = 12
    extra = [] if bufs is None else list(bufs)
    aliases = {} if bufs is None else {n_in + k: 5 + k for k in range(4)}
    return pl.pallas_call(
        body, grid=(n // tm,),
        in_specs=[row(d), *[row(w) for w in widths], *gls, pl.BlockSpec(bg.shape, lambda i: (0, 0)),
                  *[wsp(w.shape[1:]) for w in wbs], wsp(wout.shape[1:]), *[_ANY for _ in extra]],
        out_specs=[*[row(w) for w in widths], row(3 * d), pl.BlockSpec((1, 3 * d), lambda i: (0, 0)),
                   *[wsp(sh) for sh in slabs]],
        out_shape=[*[_sds((n, w), BF16) for w in widths], _sds((n, 3 * d), BF16), _sds((1, 3 * d), F32),
                   *[_sds((nl,) + sh, F32) for sh in slabs]],
        input_output_aliases=aliases,
        compiler_params=_cp("arbitrary"), name=name)(dx, *brs, gl, gl, gl, bg, *wbs, wout, *extra)


def _adamw(w, g, m, v, *, name):
    r, c = w.shape
    tm = _tile(r, 256)
    c1 = 1.0 - ADAM_B1 ** ADAM_STEP
    c2 = 1.0 - ADAM_B2 ** ADAM_STEP

    def body(w_ref, g_ref, m_ref, v_ref, d_ref, nm_ref, nv_ref):
        gv = g_ref[...]
        mn = ADAM_B1 * m_ref[...] + (1.0 - ADAM_B1) * gv
        vn = ADAM_B2 * v_ref[...] + (1.0 - ADAM_B2) * (gv * gv)
        nm_ref[...] = mn
        nv_ref[...] = vn
        d_ref[...] = -ADAM_LR * ((mn / c1) / (jnp.sqrt(vn / c2) + ADAM_EPS) + ADAM_WD * w_ref[...])

    blk = pl.BlockSpec((tm, c), lambda i: (i, 0))
    return pl.pallas_call(
        body, grid=(r // tm,), in_specs=[blk] * 4, out_specs=[blk] * 3,
        out_shape=[_sds((r, c), F32)] * 3, compiler_params=_cp("parallel"), name=name)(w, g, m, v)


def _add_sibling(g, recv, lyr, *, name):
    _, a, b = g.shape
    ta = _tile(a, 256)

    def body(g_ref, r_ref, o_ref):
        o_ref[...] = (g_ref[...] + r_ref[...]).astype(BF16)

    return pl.pallas_call(
        body, grid=(a // ta,),
        in_specs=[pl.BlockSpec((None, ta, b), lambda i: (lyr, i, 0)), pl.BlockSpec((ta, b), lambda i: (i, 0))],
        out_specs=pl.BlockSpec((ta, b), lambda i: (i, 0)),
        out_shape=_sds((a, b), BF16), compiler_params=_cp("parallel"), name=name)(g, recv)


def _add_chips(rsum, parts, axis, s_idx, lyr, buf, *, name):
    _, a, b = parts.shape
    ta = _tile(a, 256)
    na = a // ta

    def body(s_ref, own_ref, p0, p1, p2, p3, *rest):
        o_ref = rest[-1]
        own = own_ref[...].astype(F32)
        terms = [jnp.where(s_ref[0] == s, own, p[...].astype(F32)) for s, p in enumerate((p0, p1, p2, p3))]
        o_ref[...] = ((terms[0] + terms[1]) + terms[2]) + terms[3]

    own_spec = (pl.BlockSpec((ta, b), lambda i, sr: (sr[0] * na + i, 0)) if axis == 1
                else pl.BlockSpec((ta, b), lambda i, sr: (i, sr[0])))
    part_spec = lambda s: pl.BlockSpec((None, ta, b), lambda i, sr: (jnp.where(sr[0] == s, s ^ 1, s), i, 0))
    extra, extra_specs, out_shape, aliases = _slab_out(2, lyr, (a, b), buf, 6)
    return pl.pallas_call(
        body,
        grid_spec=pltpu.PrefetchScalarGridSpec(
            num_scalar_prefetch=1, grid=(na,),
            in_specs=[own_spec] + [part_spec(s) for s in range(N_CHIPS)] + extra_specs,
            out_specs=pl.BlockSpec((None, ta, b), lambda i, sr: (lyr, i, 0))),
        out_shape=out_shape, input_output_aliases=aliases, compiler_params=_cp("parallel"), name=name)(
            s_idx, rsum, parts, parts, parts, parts, *extra)


def _place_shard(wloc, axis, s_idx, *, name):
    nl, a, b = wloc.shape
    ta = _tile(a, 256)
    na = a // ta
    full = (nl, a * N_CHIPS, b) if axis == 1 else (nl, a, b * N_CHIPS)

    def body(sc_ref, w_ref, o_ref):
        o_ref[...] = w_ref[...].astype(BF16)

    out_spec = (pl.BlockSpec((None, ta, b), lambda l, i, sc: (l, sc[0] * na + i, 0)) if axis == 1
                else pl.BlockSpec((None, ta, b), lambda l, i, sc: (l, i, sc[0])))
    return pl.pallas_call(
        body,
        grid_spec=pltpu.PrefetchScalarGridSpec(
            num_scalar_prefetch=1, grid=(nl, na),
            in_specs=[pl.BlockSpec((None, ta, b), lambda l, i, sc: (l, i, 0))], out_specs=out_spec),
        out_shape=_sds(full, BF16), compiler_params=_cp("parallel", "parallel"), name=name)(s_idx, wloc)


def _blockdiag(w):
    g, r, c = w.shape
    eye = jnp.eye(g, dtype=w.dtype)
    return (w[:, :, None, :] * eye[:, None, :, None]).reshape(g * r, g * c)


def _s5_prep(lre, lim, log_dt, b_re, b_im, c_re, c_im, d_skip):
    lr = jnp.minimum(lre, -1e-4)
    li = lim
    dt = jnp.exp(log_dt)[:, None]
    mag = jnp.exp(lr * dt)
    ar = mag * jnp.cos(li * dt)
    ai = mag * jnp.sin(li * dt)
    den = lr * lr + li * li
    coef_r = ((ar - 1.0) * lr + ai * li) / den
    coef_i = (ai * lr - (ar - 1.0) * li) / den
    bbar_r = coef_r[..., None] * b_re - coef_i[..., None] * b_im
    bbar_i = coef_r[..., None] * b_im + coef_i[..., None] * b_re
    a = jnp.stack([ar.reshape(-1), ai.reshape(-1)])
    return dict(
        a=a,
        bblk_r=_blockdiag(bbar_r.transpose(0, 2, 1)), bblk_i=_blockdiag(bbar_i.transpose(0, 2, 1)),
        cblk_r=_blockdiag(c_re.transpose(0, 2, 1)), cblk_in=_blockdiag(-c_im.transpose(0, 2, 1)),
        d=d_skip.reshape(1, -1))


def _s5_powers(a, nlog):
    ar, ai = a[0], a[1]
    pr, pi = ar, ai
    rows = []
    for _ in range(nlog):
        rows += [pr, pi]
        pr, pi = pr * pr - pi * pi, 2.0 * pr * pi
    qr, qi = [ar], [ai]
    for _ in range(SUBLANES - 1):
        qr, qi = qr + [qr[-1] * ar - qi[-1] * ai], qi + [qr[-1] * ai + qi[-1] * ar]
    p8 = jnp.stack(qr + qi)
    q8 = jnp.stack(qr[::-1] + [-v for v in qi[::-1]])
    return jnp.stack(rows), p8, q8


def _bias_table(rel_bias):
    h = rel_bias.shape[0]
    tq, tw = ATT_TQ, 3 * ATT_TQ
    n_hi = tw - 1 - MAX_REL + 1
    n_lo = tq + tw - 1 - n_hi - (2 * MAX_REL - 1)
    fr = jnp.concatenate([
        jnp.broadcast_to(rel_bias[:, 2 * MAX_REL:], (h, n_hi)),
        jnp.flip(rel_bias[:, 1:2 * MAX_REL], axis=1),
        jnp.broadcast_to(rel_bias[:, :1], (h, n_lo)),
        jnp.zeros((h, 1), rel_bias.dtype)], axis=1)
    ln = tq + tw
    flat = jnp.broadcast_to(fr[:, None, :], (h, tq, ln)).reshape(h, tq * ln)[:, :tq * (ln - 1)]
    tab = flat.reshape(h, tq, ln - 1)[:, :, tq - 1:tq - 1 + tw]
    qc = np.arange(tq)[:, None] // CHUNK + N_LEFT
    kc = np.arange(tw)[None, :] // CHUNK
    band = (kc <= qc) & (kc >= qc - N_LEFT)
    return jnp.where(jnp.asarray(band)[None], tab, NEG)


def _small_prep(w, l):
    g, p = w["s5_lambda_re"].shape[1:]
    b_shape, c_shape = (g, p, -1), (g, -1, p)
    sp = _s5_prep(w["s5_lambda_re"][l], w["s5_lambda_im"][l], w["s5_log_dt"][l], w["s5_b_re"][l].reshape(b_shape),
                  w["s5_b_im"][l].reshape(b_shape), w["s5_c_re"][l].reshape(c_shape), w["s5_c_im"][l].reshape(c_shape),
                  w["s5_d"][l])
    return sp, _bias_table(w["attn_rel_bias"][l])


_PREP_KEYS = ("s5_lambda_re", "s5_lambda_im", "s5_log_dt", "s5_b_re", "s5_b_im", "s5_c_re", "s5_c_im", "s5_d",
              "attn_rel_bias")
_BIG_KEYS = {"ffn1_w_up": 2, "ffn1_w_down": 1, "w_in": 2, "s5_w_glu": 2, "w_br_s5": 2, "w_br_attn": 2,
             "w_br_conv": 2, "w_out": 1, "ffn2_w_up": 2, "ffn2_w_down": 1}
_SMALL_KEYS = ("ffn1_norm", "mix_norm", "b_gate", "s5_lambda_re", "s5_lambda_im", "s5_log_dt", "s5_b_re", "s5_b_im",
               "s5_c_re", "s5_c_im", "s5_d", "attn_q_gain", "attn_k_gain", "attn_rel_bias", "conv_w_dw", "conv_b_dw",
               "conv_ln_g", "conv_ln_b", "ffn2_norm")
_WEIGHTS = ("ffn1_norm", "ffn1_w_up", "ffn1_w_down", "mix_norm", "w_in", "b_gate", "s5_lambda_re", "s5_lambda_im",
            "s5_log_dt", "s5_b_re", "s5_b_im", "s5_c_re", "s5_c_im", "s5_d", "s5_w_glu", "w_br_s5", "attn_q_gain",
            "attn_k_gain", "attn_rel_bias", "w_br_attn", "conv_w_dw", "conv_b_dw", "conv_ln_g", "conv_ln_b",
            "w_br_conv", "w_out", "ffn2_norm", "ffn2_w_up", "ffn2_w_down")


def _local_step(x3, target3, w, rs=None):
    bl, s, d = x3.shape
    nl = w["ffn1_norm"].shape[0]
    dff = w["ffn1_w_down"].shape[1]
    ds5 = w["s5_d"].shape[1]
    datt = w["w_br_attn"].shape[1]
    dc = w["conv_b_dw"].shape[1]
    n = bl * s
    x = x3.reshape(n, d)
    target = target3.reshape(n, d)
    tm = _tile(n, 512)
    tml = _tile(n, 1024)
    tmix = _tile(n, 256)
    ts5 = 256
    tconv = _tile(s, 512)
    tff = dff // 2
    ma = ds5 + 3 * datt + 2 * dc
    tna = ma // 3
    assert (3 * d) % tna == 0 and dff % 2 == 0
    qoff = ds5 // LANES
    koff = (ds5 + datt) // LANES
    acol = (ds5 + 3 * datt) // dc
    wbs = (w["w_br_s5"], w["w_br_attn"], w["w_br_conv"])

    saved = []
    for l in range(nl):
        (sp, bias), prep_vjp = jax.vjp(lambda ww: _small_prep(ww, l), {k: w[k] for k in _PREP_KEYS})
        spb = dict(sp)
        spb["pw"], spb["p8"], spb["q8"] = _s5_powers(lax.stop_gradient(sp["a"]), int(math.log2(ts5)))
        for k in ("bblk_r", "bblk_i", "cblk_r", "cblk_in"):
            spb[k] = sp[k].astype(BF16)
        g1 = w["ffn1_norm"][l][None]
        g2 = w["ffn2_norm"][l][None]
        gm = w["mix_norm"][l][None]
        gq2 = jnp.tile(w["attn_q_gain"][l], 2)[None]
        gk2 = jnp.tile(w["attn_k_gain"][l], 2)[None]
        wdw = jnp.pad(w["conv_w_dw"][l], ((0, HALO - CONV_W), (0, 0)))
        bdw, lng, lnb = w["conv_b_dw"][l][None], w["conv_ln_g"][l][None], w["conv_ln_b"][l][None]
        bg = w["b_gate"][l][None]

        x0 = x
        h1, ab1 = _norm_mm(x0, g1, w["ffn1_w_up"], l, tm=tml, tn=tff, ntiles=4, pieces=2, transposed=False,
                           name=f"ffn1_up_{l}")
        x1 = _ffn_down(ab1, w["ffn1_w_down"], l, x0, tm=tm, tk=tff, name=f"ffn1_down_{l}")
        h2, pa = _norm_mm(x1, gm, w["w_in"], l, tm=tml, tn=tna, ntiles=3, pieces=1, transposed=True, name=f"win_a_{l}")
        pa = pa[0]
        gl = _mm_t(h2, w["w_in"], l, tm=tml, tn=tna, off=3, ntiles=3 * d // tna, name=f"win_g_{l}")
        xr, xi, yp, zg, s5o = _s5_fwd(pa, spb, w["s5_w_glu"], l, bl=bl, s=s, t=ts5, name=f"s5_fwd_{l}")
        atto = _attn_fwd(pa, gq2, gk2, bias, bl=bl, s=s, datt=datt, qoff=qoff, name=f"attn_fwd_{l}")
        hg, hc, convo = _conv_fwd(pa, wdw, bdw, lng, lnb, bl=bl, s=s, t=tconv, acol=acol, name=f"conv_fwd_{l}")
        brs = (s5o, atto, convo)
        x2 = _mix_out_fwd(x1, brs, gl, bg, wbs, w["w_out"], l, tm=tmix, name=f"mix_fwd_{l}")
        h3, ab2 = _norm_mm(x2, g2, w["ffn2_w_up"], l, tm=tml, tn=tff, ntiles=4, pieces=2, transposed=False,
                           name=f"ffn2_up_{l}")
        x = _ffn_down(ab2, w["ffn2_w_down"], l, x2, tm=tm, tk=tff, name=f"ffn2_down_{l}")
        saved.append(dict(spb=spb, bias=bias, prep_vjp=prep_vjp, g1=g1, g2=g2, gm=gm, gq2=gq2, gk2=gk2,
                          wdw=wdw, lng=lng, lnb=lnb, bg=bg, x0=x0, h1=h1, ab1=ab1, x1=x1, h2=h2, pa=pa, gl=gl,
                          xr=xr, xi=xi, yp=yp, zg=zg, hg=hg, hc=hc, brs=brs, x2=x2, h3=h3, ab2=ab2))

    dx, lsum = _loss_grad(x, target, tm=tm, name="loss")
    loss_part = 0.5 * jnp.sum(lsum) / d

    big = {k: None for k in _BIG_KEYS}
    small = {k: [None] * nl for k in _SMALL_KEYS}
    pending = None
    for l in reversed(range(nl)):
        sv = saved[l]

        def ffn_bwd(dx, xin, h, ab, g, tag):
            wu, wd = w[tag + "_w_up"], w[tag + "_w_down"]
            dab, big[tag + "_w_down"] = _ffn_dact(dx, wd, l, ab, nl, big[tag + "_w_down"], tm=tm, tk=tff,
                                                  name=f"{tag}_dact_{l}")
            big[tag + "_w_up"] = _mm_tn(h[None], dab, l, nl, big[tag + "_w_up"], ta=d, tb=tff, tk=tml,
                                        name=f"{tag}_dwu_{l}")
            dxo, dg = _ffn_dx(dab, wu, l, xin, g, dx, tm=tml, tk=tff, name=f"{tag}_dx_{l}")
            small[tag + "_norm"][l] = dg[0]
            return dxo

        dx = ffn_bwd(dx, sv["x2"], sv["h3"], sv["ab2"], sv["g2"], "ffn2")

        mix_keys = ("w_br_s5", "w_br_attn", "w_br_conv", "w_out")
        bufs = None if big["w_out"] is None else [big[k] for k in mix_keys]
        ds5o, datto, dconvo, dgl, dbg, *dws = _mix_out_bwd(
            dx, sv["brs"], sv["gl"], sv["bg"], wbs, w["w_out"], l, nl, bufs, tm=tmix, name=f"mix_bwd_{l}")
        small["b_gate"][l] = dbg[0]
        big.update(zip(mix_keys, dws))

        dhc, dlng, dlnb = _conv_bwd_ln(dconvo, sv["hc"], sv["lng"], sv["lnb"], tm=tm, name=f"conv_bwd_ln_{l}")
        dz, dwdw, dbdw = _conv_bwd_dw(dhc, sv["hg"], sv["pa"], sv["wdw"], bl=bl, s=s, t=tconv, acol=acol,
                                      name=f"conv_bwd_dw_{l}")
        small["conv_w_dw"][l] = dwdw[:CONV_W]
        small["conv_b_dw"][l], small["conv_ln_g"][l], small["conv_ln_b"][l] = dbdw[0], dlng[0], dlnb[0]

        comm = pending if l == 0 else None
        dq, dkn, dvw, dbias, dgq, *hosted = _attn_bwd(datto, sv["pa"], sv["gq2"], sv["gk2"], sv["bias"], bl=bl, s=s,
                                                      datt=datt, qoff=qoff, name=f"attn_bwd_{l}", comm=comm)
        if comm is not None:
            rs.parts = hosted
        dk, dv, dgk = _attn_kv_bwd(dkn, dvw, sv["pa"], sv["gk2"], bl=bl, s=s, datt=datt, tm=_tile(s, 512), koff=koff,
                                   name=f"attn_kv_bwd_{l}")
        small["attn_q_gain"][l] = jnp.sum(dgq.reshape(-1, HEAD_DIM), axis=0)
        small["attn_k_gain"][l] = jnp.sum(dgk.reshape(-1, HEAD_DIM), axis=0)

        du, dd, dcr, dci, dbr, dbi, da, big["s5_w_glu"] = _s5_bwd(
            ds5o, sv["yp"], sv["zg"], sv["xr"], sv["xi"], sv["pa"], sv["spb"], w["s5_w_glu"], l, nl, big["s5_w_glu"],
            bl=bl, s=s, t=ts5, name=f"s5_bwd_{l}")
        prep_ct = (dict(a=da, bblk_r=dbr, bblk_i=dbi, cblk_r=dcr, cblk_in=dci, d=dd), jnp.sum(dbias, axis=0))
        (dprep,) = sv["prep_vjp"](prep_ct)
        for k in _PREP_KEYS:
            small[k][l] = dprep[k][l]

        dpa = jnp.concatenate([du, dq, dk, dv, dz], axis=1)
        big["w_in"] = _dwin_t(dpa, dgl, sv["h2"], l, nl, big["w_in"], ta=tna, tk=tml, name=f"dwin_{l}")
        dx, dgm = _mix_dx(dpa, dgl, w["w_in"], l, sv["x1"], sv["gm"], dx, tm=tml, tk=tna, name=f"mix_dx_{l}")
        small["mix_norm"][l] = dgm[0]

        dx = ffn_bwd(dx, sv["x0"], sv["h1"], sv["ab1"], sv["g1"], "ffn1")
        if rs is not None and l == nl - 1 and nl > 1:
            pending = rs.prepare(big, l)

    small = {k: jnp.stack(v) for k, v in small.items()}
    return loss_part, dx.reshape(bl, s, d), big, small


def _place():
    x, y, c = lax.axis_index("x"), lax.axis_index("y"), lax.axis_index("c")
    chips = [(1 - x, y), (x, 1 - y), (1 - x, 1 - y)]
    return x, y, c, chips


def _remote(src, dst, send_sems, recv_sems, k, dev):
    return pltpu.make_async_remote_copy(src_ref=src, dst_ref=dst, send_sem=send_sems.at[k], recv_sem=recv_sems.at[k],
                                        device_id=dev, device_id_type=MESH)


def _window(ref, lead, s, axis, blk):
    if axis == 1:
        sl = (pl.ds(pl.multiple_of(s * blk, 16), blk), slice(None))
    else:
        sl = (slice(None), pl.ds(pl.multiple_of(s * blk, LANES), blk))
    return ref.at[sl] if lead is None else ref.at[(lead,) + sl]


def _all_gather_weights(fulls, axes, taps):
    nw = len(fulls)
    assert fulls[0].shape[0] == 2
    blks = [f.shape[ax] // N_CHIPS for f, ax in zip(fulls, axes)]

    def body(*refs):
        taps_in = refs[nw]
        outs, taps_out = refs[nw + 1:2 * nw + 1], refs[2 * nw + 1]
        send_sems, recv_sems, local_sem = refs[-3:]
        x, y, c, chips = _place()
        s_me = 2 * x + y
        sibling = (x, y, 1 - c)
        win = lambda i, lyr, s: _window(outs[i], lyr, s, axes[i], blks[i])
        own_taps = pltpu.make_async_copy(taps_in, taps_out.at[s_me], local_sem)
        own_taps.start()
        sends = []
        for i in range(nw):
            for j, (cx, cy) in enumerate(chips):
                mine = win(i, c, s_me)
                sends.append(_remote(mine, mine, send_sems, recv_sems, 6 * i + j, (cx, cy, c)))
        for j, (cx, cy) in enumerate(chips):
            sends.append(_remote(taps_in, taps_out.at[s_me], send_sems, recv_sems, 6 * nw + j, (cx, cy, c)))
        for cp in sends:
            cp.start()
        for i in range(nw):
            for j, (cx, cy) in enumerate(chips):
                piece = win(i, c, 2 * cx + cy)
                _remote(piece, piece, send_sems, recv_sems, 6 * i + j, (cx, cy, c)).wait_recv()
                fw = _remote(piece, piece, send_sems, recv_sems, 6 * i + 3 + j, sibling)
                fw.start()
                sends.append(fw)
        for i in range(nw):
            for j, (cx, cy) in enumerate(chips):
                piece = win(i, 1 - c, 2 * cx + cy)
                _remote(piece, piece, send_sems, recv_sems, 6 * i + 3 + j, sibling).wait_recv()
        for j, (cx, cy) in enumerate(chips):
            slab = taps_out.at[2 * cx + cy]
            _remote(slab, slab, send_sems, recv_sems, 6 * nw + j, (cx, cy, c)).wait_recv()
        for cp in sends:
            cp.wait_send()
        own_taps.wait()

    nsem = 6 * nw + 3
    return pl.pallas_call(
        body, in_specs=[_ANY] * (nw + 1), out_specs=[_ANY] * (nw + 1),
        out_shape=[_sds(f.shape, f.dtype) for f in fulls] + [_sds((N_CHIPS,) + taps.shape, taps.dtype)],
        input_output_aliases={i: i for i in range(nw)},
        scratch_shapes=[pltpu.SemaphoreType.DMA((nsem,)), pltpu.SemaphoreType.DMA((nsem,)), pltpu.SemaphoreType.DMA],
        name="all_gather_weights")(*fulls, taps)


def _rs_swap(grads, lyr):
    nw = len(grads)

    def body(*refs):
        ins, outs = refs[:nw], refs[nw:2 * nw]
        send_sems, recv_sems = refs[-2:]
        x, y, c, _ = _place()
        cps = [_remote(ins[i].at[lyr], outs[i], send_sems, recv_sems, i, (x, y, lyr)) for i in range(nw)]

        @pl.when(c != lyr)
        def _():
            for cp in cps:
                cp.start()
            for cp in cps:
                cp.wait_send()

        @pl.when(c == lyr)
        def _():
            for cp in cps:
                cp.wait_recv()

    return pl.pallas_call(
        body, in_specs=[_ANY] * nw, out_specs=[_ANY] * nw, out_shape=[_sds(g.shape[1:], g.dtype) for g in grads],
        scratch_shapes=[pltpu.SemaphoreType.DMA((nw,)), pltpu.SemaphoreType.DMA((nw,))],
        name=f"rs_swap_l{lyr}")(*grads)


def _exchange_ops(ins, outs, send_sems, recv_sems, axes, lyr):
    x, y, c, chips = _place()
    s_me = 2 * x + y
    nw = len(ins)
    blks = [r.shape[ax - 1] // N_CHIPS for r, ax in zip(ins, axes)]
    win = lambda i, s: _window(ins[i], None, s, axes[i], blks[i])
    sends = [_remote(win(i, 2 * cx + cy), outs[i].at[s_me], send_sems, recv_sems, 3 * i + j, (cx, cy, lyr))
             for i in range(nw) for j, (cx, cy) in enumerate(chips)]

    def start():
        @pl.when(c == lyr)
        def _():
            for cp in sends:
                cp.start()

    def wait():
        @pl.when(c == lyr)
        def _():
            for i in range(nw):
                for j, (cx, cy) in enumerate(chips):
                    slab = outs[i].at[2 * cx + cy]
                    _remote(slab, slab, send_sems, recv_sems, 3 * i + j, (cx, cy, lyr)).wait_recv()
            for cp in sends:
                cp.wait_send()

    return start, wait


def _parts_shapes(rsums, axes):
    shard = [tuple(dim // N_CHIPS if i == ax - 1 else dim for i, dim in enumerate(r.shape)) for r, ax in zip(rsums, axes)]
    return [_sds((N_CHIPS,) + sh, r.dtype) for sh, r in zip(shard, rsums)]


def _rs_exchange(rsums, axes, lyr):
    nw = len(rsums)

    def body(*refs):
        start, wait = _exchange_ops(refs[:nw], refs[nw:2 * nw], refs[-2], refs[-1], axes, lyr)
        start()
        wait()

    return pl.pallas_call(
        body, in_specs=[_ANY] * nw, out_specs=[_ANY] * nw, out_shape=_parts_shapes(rsums, axes),
        scratch_shapes=[pltpu.SemaphoreType.DMA((3 * nw,)), pltpu.SemaphoreType.DMA((3 * nw,))],
        name=f"rs_exchange_l{lyr}")(*rsums)


def _rs_join(ts):
    nw = len(ts)

    def body(*refs):
        outs = refs[nw:2 * nw]
        send_sems, recv_sems = refs[-2:]
        x, y, c, _ = _place()
        sends = [_remote(outs[i].at[c], outs[i].at[c], send_sems, recv_sems, i, (x, y, 1 - c)) for i in range(nw)]
        for cp in sends:
            cp.start()
        for i in range(nw):
            slab = outs[i].at[1 - c]
            _remote(slab, slab, send_sems, recv_sems, i, (x, y, 1 - c)).wait_recv()
        for cp in sends:
            cp.wait_send()

    return pl.pallas_call(
        body, in_specs=[_ANY] * nw, out_specs=[_ANY] * nw, out_shape=[_sds(t.shape, t.dtype) for t in ts],
        input_output_aliases={i: i for i in range(nw)},
        scratch_shapes=[pltpu.SemaphoreType.DMA((nw,)), pltpu.SemaphoreType.DMA((nw,))],
        name="rs_join_layers")(*ts)


def _all_reduce_small(arrs):
    na = len(arrs)
    nd = 8

    def body(*refs):
        ins, outs, recvs = refs[:na], refs[na:2 * na], refs[2 * na:3 * na]
        send_sems, recv_sems = refs[-2:]
        x, y, c, _ = _place()
        me = 4 * x + 2 * y + c
        for i in range(na):
            recvs[i][0] = ins[i][...]
        cps = []
        for rel in range(1, nd):
            dev = (1 - x if rel & 4 else x, 1 - y if rel & 2 else y, 1 - c if rel & 1 else c)
            for i in range(na):
                cp = _remote(ins[i], recvs[i].at[rel], send_sems, recv_sems, (rel - 1) * na + i, dev)
                cp.start()
                cps.append(cp)
        for rel in range(1, nd):
            for i in range(na):
                _remote(ins[i], recvs[i].at[rel], send_sems, recv_sems, (rel - 1) * na + i, (x, y, c)).wait_recv()
        for i in range(na):
            acc = recvs[i][me]
            for dv in range(1, nd):
                acc = acc + recvs[i][lax.bitwise_xor(me, dv)]
            outs[i][...] = acc
        for cp in cps:
            cp.wait_send()

    vm = pl.BlockSpec(memory_space=pltpu.VMEM)
    nsem = (nd - 1) * na
    return pl.pallas_call(
        body, in_specs=[vm] * na, out_specs=[vm] * na, out_shape=[_sds(t.shape, F32) for t in arrs],
        scratch_shapes=[pltpu.VMEM((nd,) + t.shape, F32) for t in arrs]
        + [pltpu.SemaphoreType.DMA((nsem,)), pltpu.SemaphoreType.DMA((nsem,))],
        compiler_params=pltpu.CompilerParams(vmem_limit_bytes=VMEM_LIMIT), name="all_reduce_small")(*arrs)


def _adamw_small(ws, gs, ms, vs):
    na = len(ws)
    c1 = 1.0 - ADAM_B1 ** ADAM_STEP
    c2 = 1.0 - ADAM_B2 ** ADAM_STEP

    def body(*refs):
        w_r, g_r, m_r, v_r = (refs[k * na:(k + 1) * na] for k in range(4))
        d_r, nm_r, nv_r = (refs[(4 + k) * na:(5 + k) * na] for k in range(3))
        for i in range(na):
            gv = g_r[i][...]
            mn = ADAM_B1 * m_r[i][...] + (1.0 - ADAM_B1) * gv
            vn = ADAM_B2 * v_r[i][...] + (1.0 - ADAM_B2) * (gv * gv)
            nm_r[i][...] = mn
            nv_r[i][...] = vn
            d_r[i][...] = -ADAM_LR * ((mn / c1) / (jnp.sqrt(vn / c2) + ADAM_EPS) + ADAM_WD * w_r[i][...])

    vm = pl.BlockSpec(memory_space=pltpu.VMEM)
    res = pl.pallas_call(
        body, in_specs=[vm] * (4 * na), out_specs=[vm] * (3 * na), out_shape=[_sds(t.shape, F32) for t in ws] * 3,
        compiler_params=pltpu.CompilerParams(vmem_limit_bytes=VMEM_LIMIT), name="adamw_small")(*ws, *gs, *ms, *vs)
    return res[:na], res[na:2 * na], res[2 * na:]


def kernel(x, ffn1_norm, ffn1_w_up, ffn1_w_down, mix_norm, w_in, b_gate, s5_lambda_re, s5_lambda_im, s5_log_dt, s5_b_re, s5_b_im, s5_c_re, s5_c_im, s5_d, s5_w_glu, w_br_s5, attn_q_gain, attn_k_gain, attn_rel_bias, w_br_attn, conv_w_dw, conv_b_dw, conv_ln_g, conv_ln_b, w_br_conv, w_out, ffn2_norm, ffn2_w_up, ffn2_w_down, loss_target, m_ffn1_norm, m_ffn1_w_up, m_ffn1_w_down, m_mix_norm, m_w_in, m_b_gate, m_s5_lambda_re, m_s5_lambda_im, m_s5_log_dt, m_s5_b_re, m_s5_b_im, m_s5_c_re, m_s5_c_im, m_s5_d, m_s5_w_glu, m_w_br_s5, m_attn_q_gain, m_attn_k_gain, m_attn_rel_bias, m_w_br_attn, m_conv_w_dw, m_conv_b_dw, m_conv_ln_g, m_conv_ln_b, m_w_br_conv, m_w_out, m_ffn2_norm, m_ffn2_w_up, m_ffn2_w_down, v_ffn1_norm, v_ffn1_w_up, v_ffn1_w_down, v_mix_norm, v_w_in, v_b_gate, v_s5_lambda_re, v_s5_lambda_im, v_s5_log_dt, v_s5_b_re, v_s5_b_im, v_s5_c_re, v_s5_c_im, v_s5_d, v_s5_w_glu, v_w_br_s5, v_attn_q_gain, v_attn_k_gain, v_attn_rel_bias, v_w_br_attn, v_conv_w_dw, v_conv_b_dw, v_conv_ln_g, v_conv_ln_b, v_w_br_conv, v_w_out, v_ffn2_norm, v_ffn2_w_up, v_ffn2_w_down):
    a = dict(locals())
    xi, yi, ci = lax.axis_index("x"), lax.axis_index("y"), lax.axis_index("c")
    s_me = 2 * xi + yi
    big_keys = list(_BIG_KEYS)
    axes = [1 if k == "w_in" else _BIG_KEYS[k] for k in big_keys]

    s_idx = s_me.astype(jnp.int32).reshape(1)
    c_idx = ci.astype(jnp.int32).reshape(1)
    placed = [_place_shard(jnp.swapaxes(a[k], 1, 2).astype(BF16) if k == "w_in" else a[k], ax, s_idx,
                           name=f"place_{k}") for k, ax in zip(big_keys, axes)]
    *fulls, taps = _all_gather_weights(placed, axes, a["conv_w_dw"])
    flat = lambda t: t.reshape(t.shape[0], t.shape[1], -1) if t.ndim == 4 else t
    w = {k: flat(a[k]) for k in _WEIGHTS}
    w.update(zip(big_keys, fulls))
    w["conv_w_dw"] = jnp.moveaxis(taps, 0, 2).reshape(taps.shape[1], taps.shape[2], -1)

    class _ReduceScatter:
        parts = None

        @staticmethod
        def prepare(big, lyr):
            glist = [big[k] for k in big_keys]
            recv = _rs_swap(glist, lyr)
            rsums = [_add_sibling(g, r, lyr, name=f"rs_add_sibling_{k}_l{lyr}") for g, r, k in zip(glist, recv, big_keys)]
            if lyr == nl - 1:
                _ReduceScatter.rsums = rsums
            return rsums, axes, lyr

    nl = a["ffn1_norm"].shape[0]
    rs = _ReduceScatter()
    loss_part, grad_x, gbig, gsmall = _local_step(a["x"], a["loss_target"], w, rs)
    loss = lax.psum(loss_part, ("x", "y", "c"))

    mine = [None] * len(big_keys)
    for lyr in reversed(range(nl)):
        if lyr == nl - 1 and rs.parts is not None:
            rsums, parts = rs.rsums, rs.parts
        else:
            rsums = rs.prepare(gbig, lyr)[0]
            parts = _rs_exchange(rsums, axes, lyr)
        mine = [_add_chips(r, p, ax, s_idx, lyr, buf, name=f"rs_add_chips_{k}_l{lyr}")
                for r, p, ax, buf, k in zip(rsums, parts, axes, mine, big_keys)]
    gb = dict(zip(big_keys, _rs_join(mine)))
    gb["w_in"] = jnp.swapaxes(gb["w_in"], 1, 2)

    small_keys = list(_SMALL_KEYS)
    gs = dict(zip(small_keys, _all_reduce_small([gsmall[k] for k in small_keys])))
    blk = a["conv_w_dw"].shape[2]
    gs["conv_w_dw"] = lax.dynamic_slice_in_dim(gs["conv_w_dw"], s_me * blk, blk, axis=2)

    delta, new_m, new_v = {}, {}, {}
    for k in big_keys:
        shp = a[k].shape
        two_d = lambda t: t.reshape(-1, shp[-1])
        d_, m_, v_ = _adamw(two_d(a[k]), two_d(gb[k]), two_d(a["m_" + k]), two_d(a["v_" + k]), name=f"adamw_{k}")
        delta[k], new_m[k], new_v[k] = d_.reshape(shp), m_.reshape(shp), v_.reshape(shp)
    res = _adamw_small([flat(a[k]) for k in small_keys], [gs[k] for k in small_keys],
                       [flat(a["m_" + k]) for k in small_keys], [flat(a["v_" + k]) for k in small_keys])
    for dst, vals in zip((delta, new_m, new_v), res):
        dst.update({k: t.reshape(a[k].shape) for k, t in zip(small_keys, vals)})
    grads = {**gb, **{k: t.reshape(a[k].shape) for k, t in gs.items()}}

    return (loss, grad_x, *[grads[k] for k in _WEIGHTS], *[delta[k] for k in _WEIGHTS],
            *[new_m[k] for k in _WEIGHTS], *[new_v[k] for k in _WEIGHTS])
```

```python
import functools
import math

import numpy as np
import jax
import jax.numpy as jnp
from jax import lax
from jax.experimental import pallas as pl
from jax.experimental.pallas import tpu as pltpu

F32 = jnp.float32
BF16 = jnp.bfloat16
EPS = 1e-6
VMEM_LIMIT = 56 * 1024 * 1024
LANES = 128
HEAD_DIM = 64
CHUNK = 64
N_LEFT = 8
MAX_REL = 128
ATT_TQ = 256
CONV_W = 31
HALO = 32
ROW_CHUNK = 256
SUBLANES = 8
GROUP_LOG = 3
NEG = -1e30
N_CHIPS = 4
PACK_COLS = 1024

ADAM_LR = 0.001
ADAM_B1 = 0.9
ADAM_B2 = 0.999
ADAM_EPS = 1e-08
ADAM_WD = 0.01
ADAM_STEP = 10

MESH = pl.DeviceIdType.MESH
_ANY = pl.BlockSpec(memory_space=pl.ANY)


def _cp(*sem):
    return pltpu.CompilerParams(dimension_semantics=sem, vmem_limit_bytes=VMEM_LIMIT)


def _sds(shape, dtype):
    return jax.ShapeDtypeStruct(shape, dtype)


def _tile(n, pref):
    t = min(n, pref)
    while n % t:
        t -= 8
    return t


def _sigmoid(x):
    return jax.nn.sigmoid(x)


_GELU_C = math.sqrt(2.0 / math.pi)


def _gelu(y):
    return 0.5 * y * (1.0 + jnp.tanh(_GELU_C * (y + 0.044715 * y * y * y)))


def _gelu_grad(y):
    th = jnp.tanh(_GELU_C * (y + 0.044715 * y * y * y))
    return 0.5 * (1.0 + th) + 0.5 * y * (1.0 - th * th) * _GELU_C * (1.0 + 3.0 * 0.044715 * y * y)


def _dot(a, b):
    return jnp.dot(a, b, preferred_element_type=F32)


def _dot_t0(a, b):
    return lax.dot_general(a, b, (((0,), (0,)), ((), ())), preferred_element_type=F32)


def _dot_t1(a, b):
    return lax.dot_general(a, b, (((1,), (1,)), ((), ())), preferred_element_type=F32)


def _slab_out(nl, l, shape, buf, n_in):
    sds = _sds((nl,) + tuple(shape), F32)
    if buf is None:
        return [], [], sds, {}
    return [buf], [_ANY], sds, {n_in: 0}


def _norm_mm(x, g, w, l, *, tm, tn, ntiles, pieces, transposed, name):
    n, d = x.shape
    m = ntiles * tn
    mp = m // pieces
    npj = mp // tn

    def body(x_ref, g_ref, w_ref, h_ref, y_ref, h_scr):
        @pl.when(pl.program_id(1) == 0)
        def _():
            for r0 in range(0, tm, ROW_CHUNK):
                rows = slice(r0, r0 + ROW_CHUNK)
                xv = x_ref[rows, :]
                r = lax.rsqrt(jnp.mean(xv * xv, axis=-1, keepdims=True) + EPS)
                hb = (xv * r * g_ref[...]).astype(BF16)
                h_scr[rows, :] = hb
                h_ref[rows, :] = hb

        mm = _dot_t1 if transposed else _dot
        y_ref[...] = mm(h_scr[...], w_ref[...]).astype(BF16)

    wspec = (pl.BlockSpec((None, tn, d), lambda i, j: (l, j, 0)) if transposed
             else pl.BlockSpec((None, d, tn), lambda i, j: (l, 0, j)))
    return pl.pallas_call(
        body, grid=(n // tm, ntiles),
        in_specs=[pl.BlockSpec((tm, d), lambda i, j: (i, 0)), pl.BlockSpec((1, d), lambda i, j: (0, 0)), wspec],
        out_specs=[pl.BlockSpec((tm, d), lambda i, j: (i, 0)),
                   pl.BlockSpec((None, tm, tn), lambda i, j: (j // npj, i, j % npj))],
        out_shape=[_sds((n, d), BF16), _sds((pieces, n, mp), BF16)],
        scratch_shapes=[pltpu.VMEM((tm, d), BF16)],
        compiler_params=_cp("parallel", "arbitrary"), name=name)(x, g, w)


def _mm_t(a, w, l, *, tm, tn, off, ntiles, name):
    n, k = a.shape

    def body(a_ref, w_ref, y_ref):
        y_ref[...] = _dot_t1(a_ref[...], w_ref[...]).astype(BF16)

    return pl.pallas_call(
        body, grid=(n // tm, ntiles),
        in_specs=[pl.BlockSpec((tm, k), lambda i, j: (i, 0)), pl.BlockSpec((None, tn, k), lambda i, j: (l, off + j, 0))],
        out_specs=pl.BlockSpec((tm, tn), lambda i, j: (i, j)),
        out_shape=_sds((n, ntiles * tn), BF16),
        compiler_params=_cp("parallel", "arbitrary"), name=name)(a, w)


def _ffn_down(ab, wd, l, x, *, tm, tk, name):
    _, n, dff = ab.shape
    d = x.shape[1]
    nk = dff // tk

    def body(a_ref, b_ref, wd_ref, x_ref, o_ref, acc):
        k = pl.program_id(1)
        a = a_ref[...].astype(F32)
        b = b_ref[...].astype(F32)
        act = (a * _sigmoid(a) * b).astype(BF16)
        part = _dot(act, wd_ref[pl.ds(pl.multiple_of(k * tk, tk), tk), :])

        @pl.when(k == 0)
        def _():
            acc[...] = part

        @pl.when(k > 0)
        def _():
            acc[...] += part

        @pl.when(k == nk - 1)
        def _():
            o_ref[...] = x_ref[...] + 0.5 * acc[...]

    return pl.pallas_call(
        body, grid=(n // tm, nk),
        in_specs=[pl.BlockSpec((None, tm, tk), lambda i, k: (0, i, k)),
                  pl.BlockSpec((None, tm, tk), lambda i, k: (1, i, k)),
                  pl.BlockSpec((None, dff, d), lambda i, k: (l, 0, 0)),
                  pl.BlockSpec((tm, d), lambda i, k: (i, 0))],
        out_specs=pl.BlockSpec((tm, d), lambda i, k: (i, 0)),
        out_shape=_sds((n, d), F32),
        scratch_shapes=[pltpu.VMEM((tm, d), F32)],
        compiler_params=_cp("parallel", "arbitrary"), name=name)(ab, ab, wd, x)


def _ffn_dact(dx, wd, l, ab, nl, dwd_buf, *, tm, tk, name):
    n, d = dx.shape
    dff = ab.shape[2]
    half = ((tk // LANES + 1) // 2) * LANES
    chunks = ((0, half), (half, tk))

    def body(dx_ref, wd_ref, a_ref, b_ref, *rest):
        dab_ref, dwd_ref = rest[-2:]
        do = (0.5 * dx_ref[...]).astype(BF16)

        @pl.when(pl.program_id(1) == 0)
        def _():
            dwd_ref[...] = jnp.zeros_like(dwd_ref)

        for c0, c1 in chunks:
            dact = _dot_t1(do, wd_ref[c0:c1, :])
            a = a_ref[:, c0:c1].astype(F32)
            b = b_ref[:, c0:c1].astype(F32)
            sg = _sigmoid(a)
            silu = a * sg
            dab_ref[0, :, c0:c1] = (dact * b * (sg * (1.0 + a * (1.0 - sg)))).astype(BF16)
            dab_ref[1, :, c0:c1] = (dact * silu).astype(BF16)
            dwd_ref[c0:c1, :] += _dot_t0((silu * b).astype(BF16), do)

    extra, extra_specs, dwd_shape, aliases = _slab_out(nl, l, (dff, d), dwd_buf, 4)
    aliases = {k: 1 for k in aliases}
    return pl.pallas_call(
        body, grid=(dff // tk, n // tm),
        in_specs=[pl.BlockSpec((tm, d), lambda j, i: (i, 0)),
                  pl.BlockSpec((None, tk, d), lambda j, i: (l, j, 0)),
                  pl.BlockSpec((None, tm, tk), lambda j, i: (0, i, j)),
                  pl.BlockSpec((None, tm, tk), lambda j, i: (1, i, j)), *extra_specs],
        out_specs=[pl.BlockSpec((2, tm, tk), lambda j, i: (0, i, j)),
                   pl.BlockSpec((None, tk, d), lambda j, i: (l, j, 0))],
        out_shape=[_sds((2, n, dff), BF16), dwd_shape],
        input_output_aliases=aliases,
        compiler_params=_cp("arbitrary", "arbitrary"), name=name)(dx, wd, ab, ab, *extra)


def _rms_bwd_epilogue(acc, x_ref, g_ref, dres_ref, dx_ref, dg_ref, i):
    dgp = jnp.zeros(dg_ref.shape, F32)
    for r0 in range(0, acc.shape[0], ROW_CHUNK):
        rows = slice(r0, r0 + ROW_CHUNK)
        dh = acc[rows, :]
        xv = x_ref[rows, :]
        r = lax.rsqrt(jnp.mean(xv * xv, axis=-1, keepdims=True) + EPS)
        xn = xv * r
        dgp = dgp + jnp.sum(dh * xn, axis=0, keepdims=True)
        dxh = dh * g_ref[...]
        dx_ref[rows, :] = dres_ref[rows, :] + r * (dxh - xn * jnp.mean(dxh * xn, axis=-1, keepdims=True))

    @pl.when(i == 0)
    def _():
        dg_ref[...] = dgp

    @pl.when(i > 0)
    def _():
        dg_ref[...] += dgp


def _ffn_dx(dab, wu, l, x, g, dres, *, tm, tk, name, comm=None):
    p, n, mp = dab.shape
    d = x.shape[1]
    nkp = mp // tk
    nk = p * nkp
    ni = n // tm
    nc = 0 if comm is None else len(comm[0])

    def body(dy_ref, w_ref, x_ref, g_ref, dres_ref, *rest):
        dx_ref, dg_ref = rest[nc:nc + 2]
        acc = rest[2 * nc + 2]
        k = pl.program_id(1)
        if comm is not None:
            start, wait = _swap_ops(rest[:nc], rest[nc + 2:2 * nc + 2], rest[-2], rest[-1], comm[1])

            @pl.when((pl.program_id(0) == 0) & (k == 0))
            def _():
                start()

        part = _dot_t1(dy_ref[...], w_ref[...])

        @pl.when(k == 0)
        def _():
            acc[...] = part

        @pl.when(k > 0)
        def _():
            acc[...] += part

        @pl.when(k == nk - 1)
        def _():
            _rms_bwd_epilogue(acc, x_ref, g_ref, dres_ref, dx_ref, dg_ref, pl.program_id(0))

        if comm is not None:
            @pl.when((pl.program_id(0) == ni - 1) & (k == nk - 1))
            def _():
                wait()

    comm_in = [] if comm is None else list(comm[0])
    comm_out = [_sds(t.shape[1:], t.dtype) for t in comm_in]
    comm_scr = [] if comm is None else [pltpu.SemaphoreType.DMA((nc,)), pltpu.SemaphoreType.DMA((nc,))]
    return pl.pallas_call(
        body, grid=(ni, nk),
        in_specs=[pl.BlockSpec((None, tm, tk), lambda i, k: (k // nkp, i, k % nkp)),
                  pl.BlockSpec((None, d, tk), lambda i, k: (l, 0, k)),
                  pl.BlockSpec((tm, d), lambda i, k: (i, 0)),
                  pl.BlockSpec((1, d), lambda i, k: (0, 0)),
                  pl.BlockSpec((tm, d), lambda i, k: (i, 0))] + [_ANY] * nc,
        out_specs=[pl.BlockSpec((tm, d), lambda i, k: (i, 0)), pl.BlockSpec((1, d), lambda i, k: (0, 0))] + [_ANY] * nc,
        out_shape=[_sds((n, d), F32), _sds((1, d), F32)] + comm_out,
        scratch_shapes=[pltpu.VMEM((tm, d), F32)] + comm_scr,
        compiler_params=_cp("arbitrary", "arbitrary"), name=name)(dab, wu, x, g, dres, *comm_in)


def _mix_dx(dpa, dgl, wt, l, x, g, dres, *, tm, tk, name):
    n, d = x.shape
    n1 = dpa.shape[1] // tk
    n2 = dgl.shape[1] // tk
    nk = n1 + n2

    def body(d1_ref, d2_ref, w_ref, x_ref, g_ref, dres_ref, dx_ref, dg_ref, acc):
        k = pl.program_id(1)

        @pl.when(k == 0)
        def _():
            acc[...] = _dot(d1_ref[...], w_ref[...])

        @pl.when((k > 0) & (k < n1))
        def _():
            acc[...] += _dot(d1_ref[...], w_ref[...])

        @pl.when(k >= n1)
        def _():
            acc[...] += _dot(d2_ref[...], w_ref[...])

        @pl.when(k == nk - 1)
        def _():
            _rms_bwd_epilogue(acc, x_ref, g_ref, dres_ref, dx_ref, dg_ref, pl.program_id(0))

    return pl.pallas_call(
        body, grid=(n // tm, nk),
        in_specs=[pl.BlockSpec((tm, tk), lambda i, k: (i, jnp.minimum(k, n1 - 1))),
                  pl.BlockSpec((tm, tk), lambda i, k: (i, jnp.maximum(k - n1, 0))),
                  pl.BlockSpec((None, tk, d), lambda i, k: (l, k, 0)),
                  pl.BlockSpec((tm, d), lambda i, k: (i, 0)),
                  pl.BlockSpec((1, d), lambda i, k: (0, 0)),
                  pl.BlockSpec((tm, d), lambda i, k: (i, 0))],
        out_specs=[pl.BlockSpec((tm, d), lambda i, k: (i, 0)), pl.BlockSpec((1, d), lambda i, k: (0, 0))],
        out_shape=[_sds((n, d), F32), _sds((1, d), F32)],
        scratch_shapes=[pltpu.VMEM((tm, d), F32)],
        compiler_params=_cp("arbitrary", "arbitrary"), name=name)(dpa, dgl, wt, x, g, dres)


def _mm_tn(a, b, l, nl, buf, *, ta, tb, tk, name):
    pa, n, ka = a.shape
    pb, _, kb = b.shape
    nap = ka // ta
    nbp = kb // tb

    def body(a_ref, b_ref, *rest):
        o_ref = rest[-1]

        @pl.when(pl.program_id(2) == 0)
        def _():
            o_ref[...] = jnp.zeros_like(o_ref)

        o_ref[...] += _dot_t0(a_ref[...], b_ref[...])

    extra, extra_specs, out_shape, aliases = _slab_out(nl, l, (pa * ka, pb * kb), buf, 2)
    return pl.pallas_call(
        body, grid=(pa * nap, pb * nbp, n // tk),
        in_specs=[pl.BlockSpec((None, tk, ta), lambda i, j, k: (i // nap, k, i % nap)),
                  pl.BlockSpec((None, tk, tb), lambda i, j, k: (j // nbp, k, j % nbp)), *extra_specs],
        out_specs=pl.BlockSpec((None, ta, tb), lambda i, j, k: (l, i, j)),
        out_shape=out_shape, input_output_aliases=aliases,
        compiler_params=_cp("parallel", "parallel", "arbitrary"), name=name)(a, b, *extra)


def _dwin_t(dpa, dgl, h, l, nl, buf, *, ta, tk, name):
    n, d = h.shape
    n1 = dpa.shape[1] // ta
    n2 = dgl.shape[1] // ta

    def body(a1_ref, a2_ref, h_ref, *rest):
        o_ref = rest[-1]
        i = pl.program_id(0)

        @pl.when(pl.program_id(1) == 0)
        def _():
            o_ref[...] = jnp.zeros_like(o_ref)

        @pl.when(i < n1)
        def _():
            o_ref[...] += _dot_t0(a1_ref[...], h_ref[...])

        @pl.when(i >= n1)
        def _():
            o_ref[...] += _dot_t0(a2_ref[...], h_ref[...])

    extra, extra_specs, out_shape, aliases = _slab_out(nl, l, ((n1 + n2) * ta, d), buf, 3)
    return pl.pallas_call(
        body, grid=(n1 + n2, n // tk),
        in_specs=[pl.BlockSpec((tk, ta), lambda i, k: (jnp.where(i < n1, k, 0), jnp.minimum(i, n1 - 1))),
                  pl.BlockSpec((tk, ta), lambda i, k: (jnp.where(i >= n1, k, 0), jnp.maximum(i - n1, 0))),
                  pl.BlockSpec((tk, d), lambda i, k: (k, 0)), *extra_specs],
        out_specs=pl.BlockSpec((None, ta, d), lambda i, k: (l, i, 0)),
        out_shape=out_shape, input_output_aliases=aliases,
        compiler_params=_cp("parallel", "arbitrary"), name=name)(dpa, dgl, h, *extra)


def _loss_grad(y, t, *, tm, name):
    n, d = y.shape

    def body(y_ref, t_ref, dy_ref, l_ref):
        e = y_ref[...] - t_ref[...]
        dy_ref[...] = e * (1.0 / d)
        part = jnp.sum(e * e, axis=0, keepdims=True)

        @pl.when(pl.program_id(0) == 0)
        def _():
            l_ref[...] = part

        @pl.when(pl.program_id(0) > 0)
        def _():
            l_ref[...] += part

    return pl.pallas_call(
        body, grid=(n // tm,),
        in_specs=[pl.BlockSpec((tm, d), lambda i: (i, 0)), pl.BlockSpec((tm, d), lambda i: (i, 0))],
        out_specs=[pl.BlockSpec((tm, d), lambda i: (i, 0)), pl.BlockSpec((1, d), lambda i: (0, 0))],
        out_shape=[_sds((n, d), F32), _sds((1, d), F32)],
        compiler_params=_cp("arbitrary"), name=name)(y, t)


def _s5_fwd(proj, sp, wglu, l, *, bl, s, t, name):
    n = bl * s
    ds5, gp = sp["bblk_r"].shape
    nt = s // t
    ng = t // SUBLANES
    glog = int(math.log2(ng))

    def body(u_ref, br_ref, bi_ref, pw_ref, p8_ref, cr_ref, ci_ref, d_ref, wg_ref,
             xr_ref, xi_ref, yp_ref, zg_ref, o_ref, carry, st):
        @pl.when(pl.program_id(1) == 0)
        def _():
            carry[...] = jnp.zeros_like(carry)

        u = u_ref[...]
        sub = lax.broadcasted_iota(jnp.int32, (t, gp), 0) % SUBLANES
        xr = _dot(u, br_ref[...])
        xi = _dot(u, bi_ref[...])
        for k in range(GROUP_LOG):
            sh = 1 << k
            pr = pw_ref[2 * k:2 * k + 1, :]
            pi = pw_ref[2 * k + 1:2 * k + 2, :]
            keep = sub >= sh
            sr = jnp.where(keep, pltpu.roll(xr, sh, 0), 0.0)
            si = jnp.where(keep, pltpu.roll(xi, sh, 0), 0.0)
            xr, xi = xr + pr * sr - pi * si, xi + pr * si + pi * sr
        xr_ref[...] = xr
        xi_ref[...] = xi
        grow = lax.broadcasted_iota(jnp.int32, (ng, gp), 0)
        cr = carry[0:1, :]
        ci = carry[1:2, :]
        a8r = pw_ref[2 * GROUP_LOG:2 * GROUP_LOG + 1, :]
        a8i = pw_ref[2 * GROUP_LOG + 1:2 * GROUP_LOG + 2, :]
        head = grow == 0
        for g in range(ng):
            st[g:g + 1, :] = xr_ref[(g + 1) * SUBLANES - 1:(g + 1) * SUBLANES, :]
            st[ng + g:ng + g + 1, :] = xi_ref[(g + 1) * SUBLANES - 1:(g + 1) * SUBLANES, :]
        sr_ = st[0:ng, :] + jnp.where(head, a8r * cr - a8i * ci, 0.0)
        si_ = st[ng:2 * ng, :] + jnp.where(head, a8r * ci + a8i * cr, 0.0)
        for k in range(glog):
            sh = 1 << k
            pr = pw_ref[2 * (GROUP_LOG + k):2 * (GROUP_LOG + k) + 1, :]
            pi = pw_ref[2 * (GROUP_LOG + k) + 1:2 * (GROUP_LOG + k) + 2, :]
            keep = grow >= sh
            tr = jnp.where(keep, pltpu.roll(sr_, sh, 0), 0.0)
            ti = jnp.where(keep, pltpu.roll(si_, sh, 0), 0.0)
            sr_, si_ = sr_ + pr * tr - pi * ti, si_ + pr * ti + pi * tr
        tail = grow == ng - 1
        carry[0:1, :] = jnp.sum(jnp.where(tail, sr_, 0.0), axis=0, keepdims=True)
        carry[1:2, :] = jnp.sum(jnp.where(tail, si_, 0.0), axis=0, keepdims=True)
        st[0:ng, :] = jnp.where(head, cr, pltpu.roll(sr_, 1, 0))
        st[ng:2 * ng, :] = jnp.where(head, ci, pltpu.roll(si_, 1, 0))
        p8r = p8_ref[0:SUBLANES, :]
        p8i = p8_ref[SUBLANES:2 * SUBLANES, :]
        for g in range(ng):
            grp = slice(g * SUBLANES, (g + 1) * SUBLANES)
            pr = st[g:g + 1, :]
            pi = st[ng + g:ng + g + 1, :]
            xr_ref[grp, :] = xr_ref[grp, :] + p8r * pr - p8i * pi
            xi_ref[grp, :] = xi_ref[grp, :] + p8r * pi + p8i * pr
        xr = xr_ref[...]
        xi = xi_ref[...]
        y = _dot(xr.astype(BF16), cr_ref[...]) + _dot(xi.astype(BF16), ci_ref[...]) + d_ref[...] * u.astype(F32)
        yp_ref[...] = y
        zg = _dot(_gelu(y).astype(BF16), wg_ref[...])
        zg_ref[...] = zg
        o_ref[...] = (zg[:, :ds5] * _sigmoid(zg[:, ds5:])).astype(BF16)

    const = lambda shape: pl.BlockSpec(shape, lambda b, i: (0, 0))
    row = lambda w: pl.BlockSpec((t, w), lambda b, i: (b * nt + i, 0))
    return pl.pallas_call(
        body, grid=(bl, nt),
        in_specs=[row(ds5), const((ds5, gp)), const((ds5, gp)), const((2 * (GROUP_LOG + glog), gp)),
                  const((2 * SUBLANES, gp)),
                  const((gp, ds5)), const((gp, ds5)), const((1, ds5)),
                  pl.BlockSpec((None, ds5, 2 * ds5), lambda b, i: (l, 0, 0))],
        out_specs=[row(gp), row(gp), row(ds5), row(2 * ds5), row(ds5)],
        out_shape=[_sds((n, gp), F32), _sds((n, gp), F32), _sds((n, ds5), F32), _sds((n, 2 * ds5), F32),
                   _sds((n, ds5), BF16)],
        scratch_shapes=[pltpu.VMEM((2, gp), F32), pltpu.VMEM((2 * ng, gp), F32)],
        compiler_params=_cp("arbitrary", "arbitrary"), name=name)(
            proj, sp["bblk_r"], sp["bblk_i"], sp["pw"], sp["p8"], sp["cblk_r"], sp["cblk_in"], sp["d"], wglu)


def _s5_bwd(ds, yp, zg, xr, xi, proj, sp, wglu, l, nl, dwg_buf, *, bl, s, t, name):
    n = bl * s
    ds5, gp = sp["bblk_r"].shape
    nt = s // t
    tb = t // 8
    ng = t // SUBLANES
    glog = int(math.log2(ng))

    def body(ds_ref, yp_ref, zg_ref, xr_ref, xi_ref, hr_ref, hi_ref, u_ref, wg_ref, cr_ref, ci_ref,
             br_ref, bi_ref, pw_ref, q8_ref, d_ref, *rest):
        du_ref, dd_ref, dcr_ref, dci_ref, dbr_ref, dbi_ref, da_ref, dwg_ref, carry, gr_scr, gi_scr, st = rest[-12:]
        b = pl.program_id(0)
        i = pl.program_id(1)
        tile = nt - 1 - i

        @pl.when((b == 0) & (i == 0))
        def _():
            for r in (dwg_ref, dd_ref, dcr_ref, dci_ref, dbr_ref, dbi_ref, da_ref):
                r[...] = jnp.zeros_like(r)

        @pl.when(i == 0)
        def _():
            carry[...] = jnp.zeros_like(carry)

        dsv = ds_ref[...].astype(F32)
        zgv = zg_ref[...]
        za = zgv[:, :ds5]
        sg = _sigmoid(zgv[:, ds5:])
        dzg = jnp.concatenate([dsv * sg, dsv * za * sg * (1.0 - sg)], axis=1).astype(BF16)
        y = yp_ref[...]
        dwg_ref[...] += _dot_t0(_gelu(y).astype(BF16), dzg)
        dy = _dot_t1(dzg, wg_ref[...]) * _gelu_grad(y)
        ub = u_ref[...]
        uf = ub.astype(F32)
        dd_ref[...] += jnp.sum(dy * uf, axis=0, keepdims=True)
        dyb = dy.astype(BF16)
        xrv = xr_ref[...]
        xiv = xi_ref[...]
        dcr_ref[...] += _dot_t0(xrv.astype(BF16), dyb)
        dci_ref[...] += _dot_t0(xiv.astype(BF16), dyb)

        rows = lax.broadcasted_iota(jnp.int32, (t, gp), 0)
        sub = rows % SUBLANES
        gr = _dot_t1(dyb, cr_ref[...])
        gi = _dot_t1(dyb, ci_ref[...])
        for k in range(GROUP_LOG):
            sh = 1 << k
            pr = pw_ref[2 * k:2 * k + 1, :]
            pi = pw_ref[2 * k + 1:2 * k + 2, :]
            keep = sub < SUBLANES - sh
            sr = jnp.where(keep, pltpu.roll(gr, t - sh, 0), 0.0)
            si = jnp.where(keep, pltpu.roll(gi, t - sh, 0), 0.0)
            gr, gi = gr + pr * sr + pi * si, gi + pr * si - pi * sr
        gr_scr[...] = gr
        gi_scr[...] = gi
        grow = lax.broadcasted_iota(jnp.int32, (ng, gp), 0)
        cr = carry[0:1, :]
        ci = carry[1:2, :]
        a8r = pw_ref[2 * GROUP_LOG:2 * GROUP_LOG + 1, :]
        a8i = pw_ref[2 * GROUP_LOG + 1:2 * GROUP_LOG + 2, :]
        tail = grow == ng - 1
        for g in range(ng):
            st[g:g + 1, :] = gr_scr[g * SUBLANES:g * SUBLANES + 1, :]
            st[ng + g:ng + g + 1, :] = gi_scr[g * SUBLANES:g * SUBLANES + 1, :]
        sr_ = st[0:ng, :] + jnp.where(tail, a8r * cr + a8i * ci, 0.0)
        si_ = st[ng:2 * ng, :] + jnp.where(tail, a8r * ci - a8i * cr, 0.0)
        for k in range(glog):
            sh = 1 << k
            pr = pw_ref[2 * (GROUP_LOG + k):2 * (GROUP_LOG + k) + 1, :]
            pi = pw_ref[2 * (GROUP_LOG + k) + 1:2 * (GROUP_LOG + k) + 2, :]
            keep = grow < ng - sh
            tr = jnp.where(keep, pltpu.roll(sr_, ng - sh, 0), 0.0)
            ti = jnp.where(keep, pltpu.roll(si_, ng - sh, 0), 0.0)
            sr_, si_ = sr_ + pr * tr + pi * ti, si_ + pr * ti - pi * tr
        head = grow == 0
        carry[0:1, :] = jnp.sum(jnp.where(head, sr_, 0.0), axis=0, keepdims=True)
        carry[1:2, :] = jnp.sum(jnp.where(head, si_, 0.0), axis=0, keepdims=True)
        st[0:ng, :] = jnp.where(tail, cr, pltpu.roll(sr_, ng - 1, 0))
        st[ng:2 * ng, :] = jnp.where(tail, ci, pltpu.roll(si_, ng - 1, 0))
        q8r = q8_ref[0:SUBLANES, :]
        q8i = q8_ref[SUBLANES:2 * SUBLANES, :]
        for g in range(ng):
            grp = slice(g * SUBLANES, (g + 1) * SUBLANES)
            pr = st[g:g + 1, :]
            pi = st[ng + g:ng + g + 1, :]
            gr_scr[grp, :] = gr_scr[grp, :] + q8r * pr - q8i * pi
            gi_scr[grp, :] = gi_scr[grp, :] + q8r * pi + q8i * pr
        gr = gr_scr[...]
        gi = gi_scr[...]
        first = rows == 0

        live = jnp.where(tile > 0, 1.0, 0.0)
        xpr = jnp.where(first, hr_ref[7:8, :] * live, pltpu.roll(xrv, 1, 0))
        xpi = jnp.where(first, hi_ref[7:8, :] * live, pltpu.roll(xiv, 1, 0))
        da_ref[0:1, :] += jnp.sum(gr * xpr + gi * xpi, axis=0, keepdims=True)
        da_ref[1:2, :] += jnp.sum(gi * xpr - gr * xpi, axis=0, keepdims=True)

        grb = gr.astype(BF16)
        gib = gi.astype(BF16)
        dbr_ref[...] += _dot_t0(ub, grb)
        dbi_ref[...] += _dot_t0(ub, gib)
        du_ref[...] = (_dot_t1(grb, br_ref[...]) + _dot_t1(gib, bi_ref[...]) + dy * d_ref[...]).astype(BF16)

    const = lambda shape: pl.BlockSpec(shape, lambda b, i: (0, 0))
    row = lambda w: pl.BlockSpec((t, w), lambda b, i: (b * nt + nt - 1 - i, 0))
    halo = pl.BlockSpec((8, gp), lambda b, i: (jnp.maximum((b * nt + nt - 1 - i) * tb - 1, 0), 0))
    extra, extra_specs, dwg_shape, aliases = _slab_out(nl, l, (ds5, 2 * ds5), dwg_buf, 16)
    aliases = {k: 7 for k in aliases}
    return pl.pallas_call(
        body, grid=(bl, nt),
        in_specs=[row(ds5), row(ds5), row(2 * ds5), row(gp), row(gp), halo, halo, row(ds5),
                  pl.BlockSpec((None, ds5, 2 * ds5), lambda b, i: (l, 0, 0)),
                  const((gp, ds5)), const((gp, ds5)), const((ds5, gp)), const((ds5, gp)),
                  const((2 * (GROUP_LOG + glog), gp)), const((2 * SUBLANES, gp)), const((1, ds5)), *extra_specs],
        out_specs=[row(ds5), const((1, ds5)), const((gp, ds5)), const((gp, ds5)),
                   const((ds5, gp)), const((ds5, gp)), const((2, gp)),
                   pl.BlockSpec((None, ds5, 2 * ds5), lambda b, i: (l, 0, 0))],
        out_shape=[_sds((n, ds5), BF16), _sds((1, ds5), F32), _sds((gp, ds5), F32),
                   _sds((gp, ds5), F32), _sds((ds5, gp), F32), _sds((ds5, gp), F32), _sds((2, gp), F32), dwg_shape],
        input_output_aliases=aliases,
        scratch_shapes=[pltpu.VMEM((2, gp), F32), pltpu.VMEM((t, gp), F32), pltpu.VMEM((t, gp), F32),
                        pltpu.VMEM((2 * ng, gp), F32)],
        compiler_params=_cp("arbitrary", "arbitrary"), name=name)(
            ds, yp, zg, xr, xi, xr, xi, proj, wglu, sp["cblk_r"], sp["cblk_in"],
            sp["bblk_r"], sp["bblk_i"], sp["pw"], sp["q8"], sp["d"], *extra)


def _head_norm(x, first):
    x2 = x * x
    sa = jnp.sum(jnp.where(first, x2, 0.0), axis=-1, keepdims=True)
    sb = jnp.sum(jnp.where(first, 0.0, x2), axis=-1, keepdims=True)
    r = jnp.where(first, lax.rsqrt(sa * (1.0 / HEAD_DIM) + EPS), lax.rsqrt(sb * (1.0 / HEAD_DIM) + EPS))
    return x * r, r


def _attn_specs(bl, s, datt, qoff):
    nq = s // ATT_TQ
    nb = datt // LANES
    col = lambda blk: (lambda b, h, q: (b * nq + q, qoff + blk * nb + h))
    win = lambda blk, j: (lambda b, h, q: (b * nq + jnp.maximum(q - 2 + j, 0), qoff + blk * nb + h))
    tile = lambda f: pl.BlockSpec((ATT_TQ, LANES), f)
    qs = tile(col(0))
    ks = [tile(win(1, j)) for j in range(3)]
    vs = [tile(win(2, j)) for j in range(3)]
    return nq, nb, qs, ks, vs


def _attn_probs(q_ref, k_refs, gq_ref, gk_ref, bias_ref):
    qt = pl.program_id(2)
    lane = lax.broadcasted_iota(jnp.int32, (1, LANES), 1)
    first = lane < HEAD_DIM
    qh, rq = _head_norm(q_ref[...].astype(F32), first)
    qn = qh * gq_ref[...]
    kc = jnp.concatenate([r[...] for r in k_refs], axis=0).astype(F32)
    kh, _ = _head_norm(kc, first)
    kn = (kh * gk_ref[...]).astype(BF16)
    kpos = (qt - 2) * ATT_TQ + lax.broadcasted_iota(jnp.int32, (1, 3 * ATT_TQ), 1)
    valid = kpos >= 0
    scale = HEAD_DIM ** -0.5
    masks = (first, jnp.logical_not(first))
    qas, ps = [], []
    for hh in range(2):
        qa = jnp.where(masks[hh], qn, 0.0).astype(BF16)
        sc = _dot_t1(qa, kn) * scale + bias_ref[hh]
        sc = jnp.where(valid, sc, NEG)
        e = jnp.exp(sc - jnp.max(sc, axis=-1, keepdims=True))
        ps.append(e / jnp.sum(e, axis=-1, keepdims=True))
        qas.append(qa)
    return first, masks, qh, rq, kn, qas, ps


def _attn_fwd(proj, gq2, gk2, bias, *, bl, s, datt, qoff, name):
    n = bl * s
    nq, nb, qs, ks, vs = _attn_specs(bl, s, datt, qoff)

    def body(q_ref, k0, k1, k2, v0, v1, v2, gq_ref, gk_ref, bias_ref, o_ref):
        first, _, _, _, _, _, ps = _attn_probs(q_ref, (k0, k1, k2), gq_ref, gk_ref, bias_ref)
        vc = jnp.concatenate([v0[...], v1[...], v2[...]], axis=0)
        o0 = _dot(ps[0].astype(BF16), vc)
        o1 = _dot(ps[1].astype(BF16), vc)
        o_ref[...] = jnp.where(first, o0, o1).astype(BF16)

    gs = pl.BlockSpec((1, LANES), lambda b, h, q: (0, 0))
    return pl.pallas_call(
        body, grid=(bl, nb, nq),
        in_specs=[qs, *ks, *vs, gs, gs, pl.BlockSpec((2, ATT_TQ, 3 * ATT_TQ), lambda b, h, q: (h, 0, 0))],
        out_specs=pl.BlockSpec((ATT_TQ, LANES), lambda b, h, q: (b * nq + q, h)),
        out_shape=_sds((n, datt), BF16),
        compiler_params=_cp("parallel", "parallel", "arbitrary"), name=name)(
            proj, proj, proj, proj, proj, proj, proj, gq2, gk2, bias)


def _attn_bwd(do, proj, gq2, gk2, bias, *, bl, s, datt, qoff, name, comm=None):
    n = bl * s
    nq, nb, qs, ks, vs = _attn_specs(bl, s, datt, qoff)
    srows = s + 2 * ATT_TQ
    scale = HEAD_DIM ** -0.5
    nc = 0 if comm is None else len(comm[0])

    def body(do_ref, q_ref, k0, k1, k2, v0, v1, v2, gq_ref, gk_ref, bias_ref, *rest):
        dq_ref, dk_ref, dv_ref, db_ref, dgq_ref = rest[nc:nc + 5]
        qt = pl.program_id(2)
        if comm is not None:
            start, wait = _exchange_ops(rest[:nc], rest[nc + 5:2 * nc + 5], rest[-2], rest[-1], comm[1], comm[2])
            step = (pl.program_id(0) * nb + pl.program_id(1)) * nq + qt

            @pl.when(step == 0)
            def _():
                start()

        @pl.when(qt == 0)
        def _():
            dk_ref[...] = jnp.zeros_like(dk_ref)
            dv_ref[...] = jnp.zeros_like(dv_ref)
            db_ref[...] = jnp.zeros_like(db_ref)
            dgq_ref[...] = jnp.zeros_like(dgq_ref)

        first, masks, qh, rq, kn, qas, ps = _attn_probs(q_ref, (k0, k1, k2), gq_ref, gk_ref, bias_ref)
        vc = jnp.concatenate([v0[...], v1[...], v2[...]], axis=0)
        dov = do_ref[...]
        dqn = jnp.zeros((ATT_TQ, LANES), F32)
        dkn = jnp.zeros((3 * ATT_TQ, LANES), F32)
        dv = jnp.zeros((3 * ATT_TQ, LANES), F32)
        for hh in range(2):
            doa = jnp.where(masks[hh], dov, jnp.zeros_like(dov))
            p = ps[hh]
            dp = _dot_t1(doa, vc)
            dsm = p * (dp - jnp.sum(dp * p, axis=-1, keepdims=True))
            db_ref[hh] += dsm
            dsc = (dsm * scale).astype(BF16)
            dqn = dqn + _dot(dsc, jnp.where(masks[hh], kn, jnp.zeros_like(kn)))
            dkn = dkn + _dot_t0(dsc, qas[hh])
            dv = dv + _dot_t0(p.astype(BF16), doa)
        start = pl.multiple_of(qt * ATT_TQ, ATT_TQ)
        dk_ref[pl.ds(start, 3 * ATT_TQ), :] += dkn
        dv_ref[pl.ds(start, 3 * ATT_TQ), :] += dv
        dgq_ref[...] += jnp.sum(dqn * qh, axis=0, keepdims=True)
        dqh = dqn * gq_ref[...]
        t = dqh * qh
        ma = jnp.sum(jnp.where(first, t, 0.0), axis=-1, keepdims=True) * (1.0 / HEAD_DIM)
        mb = jnp.sum(jnp.where(first, 0.0, t), axis=-1, keepdims=True) * (1.0 / HEAD_DIM)
        dq_ref[...] = (rq * (dqh - qh * jnp.where(first, ma, mb))).astype(BF16)
        if comm is not None:
            @pl.when(step == bl * nb * nq - 1)
            def _():
                wait()

    gs = pl.BlockSpec((1, LANES), lambda b, h, q: (0, 0))
    acc = pl.BlockSpec((None, srows, LANES), lambda b, h, q: (b, 0, h))
    comm_in = [] if comm is None else list(comm[0])
    comm_out = [] if comm is None else _parts_shapes(comm[0], comm[1])
    comm_scr = [] if comm is None else [pltpu.SemaphoreType.DMA((3 * nc,)), pltpu.SemaphoreType.DMA((3 * nc,))]
    return pl.pallas_call(
        body, grid=(bl, nb, nq),
        in_specs=[pl.BlockSpec((ATT_TQ, LANES), lambda b, h, q: (b * nq + q, h)), qs, *ks, *vs, gs, gs,
                  pl.BlockSpec((2, ATT_TQ, 3 * ATT_TQ), lambda b, h, q: (h, 0, 0))] + [_ANY] * nc,
        out_specs=[pl.BlockSpec((ATT_TQ, LANES), lambda b, h, q: (b * nq + q, h)), acc, acc,
                   pl.BlockSpec((None, 2, ATT_TQ, 3 * ATT_TQ), lambda b, h, q: (b, h, 0, 0)),
                   pl.BlockSpec((None, None, 1, LANES), lambda b, h, q: (b, h, 0, 0))] + [_ANY] * nc,
        out_shape=[_sds((n, datt), BF16), _sds((bl, srows, datt), F32), _sds((bl, srows, datt), F32),
                   _sds((bl, 2 * nb, ATT_TQ, 3 * ATT_TQ), F32), _sds((bl, nb, 1, LANES), F32)] + comm_out,
        scratch_shapes=comm_scr,
        compiler_params=_cp("arbitrary", "arbitrary", "arbitrary"), name=name)(
            do, proj, proj, proj, proj, proj, proj, proj, gq2, gk2, bias, *comm_in)


def _attn_kv_bwd(dkn, dv, proj, gk2, *, bl, s, datt, tm, koff, name):
    n = bl * s
    ns = s // tm
    off = 2 * ATT_TQ // tm
    nb = datt // LANES

    def body(dkn_ref, dv_ref, k_ref, gk_ref, dk_ref, dvo_ref, dgk_ref):
        lane = lax.broadcasted_iota(jnp.int32, (1, LANES), 1)
        first = lane < HEAD_DIM

        @pl.when((pl.program_id(0) == 0) & (pl.program_id(1) == 0) & (pl.program_id(2) == 0))
        def _():
            dgk_ref[...] = jnp.zeros_like(dgk_ref)

        dvo_ref[...] = dv_ref[...].astype(BF16)
        kh, rk = _head_norm(k_ref[...].astype(F32), first)
        dn = dkn_ref[...]
        dgk_ref[...] += jnp.sum(dn * kh, axis=0, keepdims=True)
        dh = dn * gk_ref[...]
        t = dh * kh
        ma = jnp.sum(jnp.where(first, t, 0.0), axis=-1, keepdims=True) * (1.0 / HEAD_DIM)
        mb = jnp.sum(jnp.where(first, 0.0, t), axis=-1, keepdims=True) * (1.0 / HEAD_DIM)
        dk_ref[...] = (rk * (dh - kh * jnp.where(first, ma, mb))).astype(BF16)

    accs = pl.BlockSpec((None, tm, LANES), lambda b, i, c: (b, i + off, c))
    outs = pl.BlockSpec((tm, LANES), lambda b, i, c: (b * ns + i, c))
    vec = pl.BlockSpec((1, LANES), lambda b, i, c: (0, 0))
    return pl.pallas_call(
        body, grid=(bl, ns, nb),
        in_specs=[accs, accs, pl.BlockSpec((tm, LANES), lambda b, i, c: (b * ns + i, koff + c)), vec],
        out_specs=[outs, outs, vec],
        out_shape=[_sds((n, datt), BF16), _sds((n, datt), BF16), _sds((1, LANES), F32)],
        compiler_params=_cp("arbitrary", "arbitrary", "arbitrary"), name=name)(dkn, dv, proj, gk2)


def _conv_fwd(proj, wdw, bdw, lng, lnb, *, bl, s, t, acol, name):
    n = bl * s
    dc = wdw.shape[1]
    nt = s // t
    hb = t // HALO

    def body(za_ref, zg_ref, ha_ref, hgt_ref, w_ref, b_ref, g_ref, be_ref, hg_ref, hc_ref, o_ref, ext):
        i = pl.program_id(1)
        hg = za_ref[...].astype(F32) * _sigmoid(zg_ref[...].astype(F32))
        live = jnp.where(i > 0, 1.0, 0.0)
        ext[0:HALO, :] = ha_ref[...].astype(F32) * _sigmoid(hgt_ref[...].astype(F32)) * live
        ext[HALO:HALO + t, :] = hg
        hg_ref[...] = hg
        acc = jnp.zeros((t, dc), F32) + b_ref[...]
        for j in range(CONV_W):
            acc = acc + w_ref[j:j + 1, :] * ext[pl.ds(HALO - (CONV_W - 1) + j, t), :]
        hc_ref[...] = acc
        mu = jnp.mean(acc, axis=-1, keepdims=True)
        xc = acc - mu
        rs = lax.rsqrt(jnp.mean(xc * xc, axis=-1, keepdims=True) + EPS)
        ln = xc * rs * g_ref[...] + be_ref[...]
        o_ref[...] = (ln * _sigmoid(ln)).astype(BF16)

    vec = pl.BlockSpec((1, dc), lambda b, i: (0, 0))
    row = pl.BlockSpec((t, dc), lambda b, i: (b * nt + i, 0))
    tile = lambda c: pl.BlockSpec((t, dc), lambda b, i: (b * nt + i, c))
    halo = lambda c: pl.BlockSpec((HALO, dc), lambda b, i: (jnp.maximum((b * nt + i) * hb - 1, 0), c))
    return pl.pallas_call(
        body, grid=(bl, nt),
        in_specs=[tile(acol), tile(acol + 1), halo(acol), halo(acol + 1),
                  pl.BlockSpec((HALO, dc), lambda b, i: (0, 0)), vec, vec, vec],
        out_specs=[row, row, row],
        out_shape=[_sds((n, dc), F32), _sds((n, dc), F32), _sds((n, dc), BF16)],
        scratch_shapes=[pltpu.VMEM((HALO + t, dc), F32)],
        compiler_params=_cp("parallel", "arbitrary"), name=name)(proj, proj, proj, proj, wdw, bdw, lng, lnb)


def _conv_bwd_ln(dco, hc, lng, lnb, *, tm, name):
    n, dc = hc.shape

    def body(d_ref, hc_ref, g_ref, be_ref, dhc_ref, dg_ref, db_ref):
        @pl.when(pl.program_id(0) == 0)
        def _():
            dg_ref[...] = jnp.zeros_like(dg_ref)
            db_ref[...] = jnp.zeros_like(db_ref)

        hcv = hc_ref[...]
        mu = jnp.mean(hcv, axis=-1, keepdims=True)
        xc = hcv - mu
        rs = lax.rsqrt(jnp.mean(xc * xc, axis=-1, keepdims=True) + EPS)
        xh = xc * rs
        ln = xh * g_ref[...] + be_ref[...]
        sg = _sigmoid(ln)
        dln = d_ref[...].astype(F32) * (sg * (1.0 + ln * (1.0 - sg)))
        db_ref[...] += jnp.sum(dln, axis=0, keepdims=True)
        dg_ref[...] += jnp.sum(dln * xh, axis=0, keepdims=True)
        dxh = dln * g_ref[...]
        dhc_ref[...] = rs * (dxh - jnp.mean(dxh, axis=-1, keepdims=True)
                             - xh * jnp.mean(dxh * xh, axis=-1, keepdims=True))

    vec = pl.BlockSpec((1, dc), lambda i: (0, 0))
    row = pl.BlockSpec((tm, dc), lambda i: (i, 0))
    return pl.pallas_call(
        body, grid=(n // tm,), in_specs=[row, row, vec, vec], out_specs=[row, vec, vec],
        out_shape=[_sds((n, dc), F32), _sds((1, dc), F32), _sds((1, dc), F32)],
        compiler_params=_cp("arbitrary"), name=name)(dco, hc, lng, lnb)


def _conv_bwd_dw(dhc, hg, proj, wdw, *, bl, s, t, acol, name):
    n = bl * s
    dc = wdw.shape[1]
    nt = s // t
    hb = t // HALO
    lastblk = n // HALO - 1

    def body(d_ref, dn_ref, hg_ref, hp_ref, za_ref, zg_ref, w_ref, dz_ref, dw_ref, dbias_ref, extd, exth):
        b = pl.program_id(0)
        i = pl.program_id(1)

        @pl.when((b == 0) & (i == 0))
        def _():
            dw_ref[...] = jnp.zeros_like(dw_ref)
            dbias_ref[...] = jnp.zeros_like(dbias_ref)

        dv = d_ref[...]
        extd[0:t, :] = dv
        extd[t:t + HALO, :] = dn_ref[...] * jnp.where(i < nt - 1, 1.0, 0.0)
        exth[0:HALO, :] = hp_ref[...] * jnp.where(i > 0, 1.0, 0.0)
        exth[HALO:HALO + t, :] = hg_ref[...]
        dbias_ref[...] += jnp.sum(dv, axis=0, keepdims=True)
        dhg = jnp.zeros((t, dc), F32)
        for j in range(CONV_W):
            dhg = dhg + w_ref[j:j + 1, :] * extd[pl.ds(CONV_W - 1 - j, t), :]
            dw_ref[j:j + 1, :] += jnp.sum(dv * exth[pl.ds(HALO - (CONV_W - 1) + j, t), :], axis=0, keepdims=True)
        za = za_ref[...].astype(F32)
        sg = _sigmoid(zg_ref[...].astype(F32))
        dz_ref[...] = jnp.concatenate([dhg * sg, dhg * za * sg * (1.0 - sg)], axis=1).astype(BF16)

    row = pl.BlockSpec((t, dc), lambda b, i: (b * nt + i, 0))
    nxt = pl.BlockSpec((HALO, dc), lambda b, i: (jnp.minimum((b * nt + i + 1) * hb, lastblk), 0))
    prv = pl.BlockSpec((HALO, dc), lambda b, i: (jnp.maximum((b * nt + i) * hb - 1, 0), 0))
    wsp = pl.BlockSpec((HALO, dc), lambda b, i: (0, 0))
    tile = lambda c: pl.BlockSpec((t, dc), lambda b, i: (b * nt + i, c))
    return pl.pallas_call(
        body, grid=(bl, nt),
        in_specs=[row, nxt, row, prv, tile(acol), tile(acol + 1), wsp],
        out_specs=[pl.BlockSpec((t, 2 * dc), lambda b, i: (b * nt + i, 0)), wsp,
                   pl.BlockSpec((1, dc), lambda b, i: (0, 0))],
        out_shape=[_sds((n, 2 * dc), BF16), _sds((HALO, dc), F32), _sds((1, dc), F32)],
        scratch_shapes=[pltpu.VMEM((t + HALO, dc), F32), pltpu.VMEM((HALO + t, dc), F32)],
        compiler_params=_cp("arbitrary", "arbitrary"), name=name)(dhc, dhc, hg, hg, proj, proj, wdw)


def _mix_out_fwd(x, brs, gl, bg, wbs, wout, l, *, tm, name):
    n, d = x.shape

    def body(x_ref, s_ref, a_ref, c_ref, g0, g1, g2, bg_ref, ws, wa, wc, wo, o_ref):
        merged = jnp.zeros((tm, d), F32)
        for k, (br, gr, w) in enumerate(((s_ref, g0, ws), (a_ref, g1, wa), (c_ref, g2, wc))):
            gate = _sigmoid(gr[...].astype(F32) + bg_ref[:, k * d:(k + 1) * d])
            merged = merged + gate * _dot(br[...], w[...])
        o_ref[...] = x_ref[...] + _dot(merged.astype(BF16), wo[...])

    row = lambda w: pl.BlockSpec((tm, w), lambda i: (i, 0))
    wsp = lambda a: pl.BlockSpec((None,) + a.shape[1:], lambda i: (l, 0, 0))
    gls = [pl.BlockSpec((tm, d), functools.partial(lambda k, i: (i, k), k)) for k in range(3)]
    return pl.pallas_call(
        body, grid=(n // tm,),
        in_specs=[row(d), *[row(b.shape[1]) for b in brs], *gls, pl.BlockSpec(bg.shape, lambda i: (0, 0)),
                  *[wsp(w) for w in wbs], wsp(wout)],
        out_specs=row(d), out_shape=_sds((n, d), F32),
        compiler_params=_cp("parallel"), name=name)(x, *brs, gl, gl, gl, bg, *wbs, wout)


def _mix_out_bwd(dx, brs, gl, bg, wbs, wout, l, nl, bufs, *, tm, name):
    n, d = dx.shape
    widths = [b.shape[1] for b in brs]

    def body(dx_ref, s_ref, a_ref, c_ref, g0, g1, g2, bg_ref, ws, wa, wc, wo, *rest):
        ds_ref, da_ref, dc_ref, dgl_ref, dbg_ref, dws, dwa, dwc, dwo = rest[-9:]

        @pl.when(pl.program_id(0) == 0)
        def _():
            for r in (dbg_ref, dws, dwa, dwc, dwo):
                r[...] = jnp.zeros_like(r)

        dxb = dx_ref[...].astype(BF16)
        dm = _dot_t1(dxb, wo[...])
        merged = jnp.zeros((tm, d), F32)
        for k, (br, gr, w, dbr, dw) in enumerate(((s_ref, g0, ws, ds_ref, dws), (a_ref, g1, wa, da_ref, dwa),
                                                   (c_ref, g2, wc, dc_ref, dwc))):
            gate = _sigmoid(gr[...].astype(F32) + bg_ref[:, k * d:(k + 1) * d])
            brv = br[...]
            wv = w[...]
            y = _dot(brv, wv)
            merged = merged + gate * y
            dyb = (dm * gate).astype(BF16)
            dbr[...] = _dot_t1(dyb, wv).astype(BF16)
            dw[...] += _dot_t0(brv, dyb)
            dgl = dm * y * gate * (1.0 - gate)
            dgl_ref[:, k * d:(k + 1) * d] = dgl.astype(BF16)
            dbg_ref[:, k * d:(k + 1) * d] += jnp.sum(dgl, axis=0, keepdims=True)
        dwo[...] += _dot_t0(merged.astype(BF16), dxb)

    row = lambda w: pl.BlockSpec((tm, w), lambda i: (i, 0))
    wsp = lambda shape: pl.BlockSpec((None,) + tuple(shape), lambda i: (l, 0, 0))
    gls = [pl.BlockSpec((tm, d), functools.partial(lambda k, i: (i, k), k)) for k in range(3)]
    slabs = [(w, d) for w in widths] + [(d, d)]
    n_in = 12
    extra = [] if bufs is None else list(bufs)
    aliases = {} if bufs is None else {n_in + k: 5 + k for k in range(4)}
    return pl.pallas_call(
        body, grid=(n // tm,),
        in_specs=[row(d), *[row(w) for w in widths], *gls, pl.BlockSpec(bg.shape, lambda i: (0, 0)),
                  *[wsp(w.shape[1:]) for w in wbs], wsp(wout.shape[1:]), *[_ANY for _ in extra]],
        out_specs=[*[row(w) for w in widths], row(3 * d), pl.BlockSpec((1, 3 * d), lambda i: (0, 0)),
                   *[wsp(sh) for sh in slabs]],
        out_shape=[*[_sds((n, w), BF16) for w in widths], _sds((n, 3 * d), BF16), _sds((1, 3 * d), F32),
                   *[_sds((nl,) + sh, F32) for sh in slabs]],
        input_output_aliases=aliases,
        compiler_params=_cp("arbitrary"), name=name)(dx, *brs, gl, gl, gl, bg, *wbs, wout, *extra)


def _adamw(w, g, m, v, *, name):
    r, c = w.shape
    tm = _tile(r, 256)
    c1 = 1.0 - ADAM_B1 ** ADAM_STEP
    c2 = 1.0 - ADAM_B2 ** ADAM_STEP

    def body(w_ref, g_ref, m_ref, v_ref, d_ref, nm_ref, nv_ref):
        gv = g_ref[...]
        mn = ADAM_B1 * m_ref[...] + (1.0 - ADAM_B1) * gv
        vn = ADAM_B2 * v_ref[...] + (1.0 - ADAM_B2) * (gv * gv)
        nm_ref[...] = mn
        nv_ref[...] = vn
        d_ref[...] = -ADAM_LR * ((mn / c1) / (jnp.sqrt(vn / c2) + ADAM_EPS) + ADAM_WD * w_ref[...])

    blk = pl.BlockSpec((tm, c), lambda i: (i, 0))
    return pl.pallas_call(
        body, grid=(r // tm,), in_specs=[blk] * 4, out_specs=[blk] * 3,
        out_shape=[_sds((r, c), F32)] * 3, compiler_params=_cp("parallel"), name=name)(w, g, m, v)


def _add_sibling(g, recv, lyr, *, name):
    _, a, b = g.shape
    ta = _tile(a, 256)

    def body(g_ref, r_ref, o_ref):
        o_ref[...] = (g_ref[...] + r_ref[...]).astype(BF16)

    return pl.pallas_call(
        body, grid=(a // ta,),
        in_specs=[pl.BlockSpec((None, ta, b), lambda i: (lyr, i, 0)), pl.BlockSpec((ta, b), lambda i: (i, 0))],
        out_specs=pl.BlockSpec((ta, b), lambda i: (i, 0)),
        out_shape=_sds((a, b), BF16), compiler_params=_cp("parallel"), name=name)(g, recv)


def _add_chips(rsum, parts, axis, s_idx, lyr, buf, *, name):
    _, a, b = parts.shape
    ta = _tile(a, 256)
    na = a // ta

    def body(s_ref, own_ref, p0, p1, p2, p3, *rest):
        o_ref = rest[-1]
        own = own_ref[...].astype(F32)
        terms = [jnp.where(s_ref[0] == s, own, p[...].astype(F32)) for s, p in enumerate((p0, p1, p2, p3))]
        o_ref[...] = ((terms[0] + terms[1]) + terms[2]) + terms[3]

    own_spec = (pl.BlockSpec((ta, b), lambda i, sr: (sr[0] * na + i, 0)) if axis == 1
                else pl.BlockSpec((ta, b), lambda i, sr: (i, sr[0])))
    part_spec = lambda s: pl.BlockSpec((None, ta, b), lambda i, sr: (jnp.where(sr[0] == s, s ^ 1, s), i, 0))
    extra, extra_specs, out_shape, aliases = _slab_out(2, lyr, (a, b), buf, 6)
    return pl.pallas_call(
        body,
        grid_spec=pltpu.PrefetchScalarGridSpec(
            num_scalar_prefetch=1, grid=(na,),
            in_specs=[own_spec] + [part_spec(s) for s in range(N_CHIPS)] + extra_specs,
            out_specs=pl.BlockSpec((None, ta, b), lambda i, sr: (lyr, i, 0))),
        out_shape=out_shape, input_output_aliases=aliases, compiler_params=_cp("parallel"), name=name)(
            s_idx, rsum, parts, parts, parts, parts, *extra)


def _place_shard(wloc, axis, s_idx, *, name):
    nl, a, b = wloc.shape
    ta = _tile(a, 256)
    na = a // ta
    full = (nl, a * N_CHIPS, b) if axis == 1 else (nl, a, b * N_CHIPS)

    def body(sc_ref, w_ref, o_ref):
        o_ref[...] = w_ref[...].astype(BF16)

    out_spec = (pl.BlockSpec((None, ta, b), lambda l, i, sc: (l, sc[0] * na + i, 0)) if axis == 1
                else pl.BlockSpec((None, ta, b), lambda l, i, sc: (l, i, sc[0])))
    return pl.pallas_call(
        body,
        grid_spec=pltpu.PrefetchScalarGridSpec(
            num_scalar_prefetch=1, grid=(nl, na),
            in_specs=[pl.BlockSpec((None, ta, b), lambda l, i, sc: (l, i, 0))], out_specs=out_spec),
        out_shape=_sds(full, BF16), compiler_params=_cp("parallel", "parallel"), name=name)(s_idx, wloc)


def _blockdiag(w):
    g, r, c = w.shape
    eye = jnp.eye(g, dtype=w.dtype)
    return (w[:, :, None, :] * eye[:, None, :, None]).reshape(g * r, g * c)


def _s5_prep(lre, lim, log_dt, b_re, b_im, c_re, c_im, d_skip):
    lr = jnp.minimum(lre, -1e-4)
    li = lim
    dt = jnp.exp(log_dt)[:, None]
    mag = jnp.exp(lr * dt)
    ar = mag * jnp.cos(li * dt)
    ai = mag * jnp.sin(li * dt)
    den = lr * lr + li * li
    coef_r = ((ar - 1.0) * lr + ai * li) / den
    coef_i = (ai * lr - (ar - 1.0) * li) / den
    bbar_r = coef_r[..., None] * b_re - coef_i[..., None] * b_im
    bbar_i = coef_r[..., None] * b_im + coef_i[..., None] * b_re
    a = jnp.stack([ar.reshape(-1), ai.reshape(-1)])
    return dict(
        a=a,
        bblk_r=_blockdiag(bbar_r.transpose(0, 2, 1)), bblk_i=_blockdiag(bbar_i.transpose(0, 2, 1)),
        cblk_r=_blockdiag(c_re.transpose(0, 2, 1)), cblk_in=_blockdiag(-c_im.transpose(0, 2, 1)),
        d=d_skip.reshape(1, -1))


def _s5_powers(a, nlog):
    ar, ai = a[0], a[1]
    pr, pi = ar, ai
    rows = []
    for _ in range(nlog):
        rows += [pr, pi]
        pr, pi = pr * pr - pi * pi, 2.0 * pr * pi
    qr, qi = [ar], [ai]
    for _ in range(SUBLANES - 1):
        qr, qi = qr + [qr[-1] * ar - qi[-1] * ai], qi + [qr[-1] * ai + qi[-1] * ar]
    p8 = jnp.stack(qr + qi)
    q8 = jnp.stack(qr[::-1] + [-v for v in qi[::-1]])
    return jnp.stack(rows), p8, q8


def _bias_table(rel_bias):
    h = rel_bias.shape[0]
    tq, tw = ATT_TQ, 3 * ATT_TQ
    n_hi = tw - 1 - MAX_REL + 1
    n_lo = tq + tw - 1 - n_hi - (2 * MAX_REL - 1)
    fr = jnp.concatenate([
        jnp.broadcast_to(rel_bias[:, 2 * MAX_REL:], (h, n_hi)),
        jnp.flip(rel_bias[:, 1:2 * MAX_REL], axis=1),
        jnp.broadcast_to(rel_bias[:, :1], (h, n_lo)),
        jnp.zeros((h, 1), rel_bias.dtype)], axis=1)
    ln = tq + tw
    flat = jnp.broadcast_to(fr[:, None, :], (h, tq, ln)).reshape(h, tq * ln)[:, :tq * (ln - 1)]
    tab = flat.reshape(h, tq, ln - 1)[:, :, tq - 1:tq - 1 + tw]
    qc = np.arange(tq)[:, None] // CHUNK + N_LEFT
    kc = np.arange(tw)[None, :] // CHUNK
    band = (kc <= qc) & (kc >= qc - N_LEFT)
    return jnp.where(jnp.asarray(band)[None], tab, NEG)


def _small_prep(w, l):
    g, p = w["s5_lambda_re"].shape[1:]
    b_shape, c_shape = (g, p, -1), (g, -1, p)
    sp = _s5_prep(w["s5_lambda_re"][l], w["s5_lambda_im"][l], w["s5_log_dt"][l], w["s5_b_re"][l].reshape(b_shape),
                  w["s5_b_im"][l].reshape(b_shape), w["s5_c_re"][l].reshape(c_shape), w["s5_c_im"][l].reshape(c_shape),
                  w["s5_d"][l])
    return sp, _bias_table(w["attn_rel_bias"][l])


_PREP_KEYS = ("s5_lambda_re", "s5_lambda_im", "s5_log_dt", "s5_b_re", "s5_b_im", "s5_c_re", "s5_c_im", "s5_d",
              "attn_rel_bias")
_BIG_KEYS = {"ffn1_w_up": 2, "ffn1_w_down": 1, "w_in": 2, "s5_w_glu": 2, "w_br_s5": 2, "w_br_attn": 2,
             "w_br_conv": 2, "w_out": 1, "ffn2_w_up": 2, "ffn2_w_down": 1}
_SMALL_KEYS = ("ffn1_norm", "mix_norm", "b_gate", "s5_lambda_re", "s5_lambda_im", "s5_log_dt", "s5_b_re", "s5_b_im",
               "s5_c_re", "s5_c_im", "s5_d", "attn_q_gain", "attn_k_gain", "attn_rel_bias", "conv_w_dw", "conv_b_dw",
               "conv_ln_g", "conv_ln_b", "ffn2_norm")
_WEIGHTS = ("ffn1_norm", "ffn1_w_up", "ffn1_w_down", "mix_norm", "w_in", "b_gate", "s5_lambda_re", "s5_lambda_im",
            "s5_log_dt", "s5_b_re", "s5_b_im", "s5_c_re", "s5_c_im", "s5_d", "s5_w_glu", "w_br_s5", "attn_q_gain",
            "attn_k_gain", "attn_rel_bias", "w_br_attn", "conv_w_dw", "conv_b_dw", "conv_ln_g", "conv_ln_b",
            "w_br_conv", "w_out", "ffn2_norm", "ffn2_w_up", "ffn2_w_down")


def _local_step(x3, target3, w, rs=None):
    bl, s, d = x3.shape
    nl = w["ffn1_norm"].shape[0]
    dff = w["ffn1_w_down"].shape[1]
    ds5 = w["s5_d"].shape[1]
    datt = w["w_br_attn"].shape[1]
    dc = w["conv_b_dw"].shape[1]
    n = bl * s
    x = x3.reshape(n, d)
    target = target3.reshape(n, d)
    tm = _tile(n, 512)
    tml = _tile(n, 1024)
    tmix = _tile(n, 256)
    ts5 = 256
    tconv = _tile(s, 512)
    tff = dff // 2
    ma = ds5 + 3 * datt + 2 * dc
    tna = ma // 3
    assert (3 * d) % tna == 0 and dff % 2 == 0
    qoff = ds5 // LANES
    koff = (ds5 + datt) // LANES
    acol = (ds5 + 3 * datt) // dc
    wbs = (w["w_br_s5"], w["w_br_attn"], w["w_br_conv"])

    saved = []
    for l in range(nl):
        (sp, bias), prep_vjp = jax.vjp(lambda ww: _small_prep(ww, l), {k: w[k] for k in _PREP_KEYS})
        spb = dict(sp)
        spb["pw"], spb["p8"], spb["q8"] = _s5_powers(lax.stop_gradient(sp["a"]), int(math.log2(ts5)))
        for k in ("bblk_r", "bblk_i", "cblk_r", "cblk_in"):
            spb[k] = sp[k].astype(BF16)
        g1 = w["ffn1_norm"][l][None]
        g2 = w["ffn2_norm"][l][None]
        gm = w["mix_norm"][l][None]
        gq2 = jnp.tile(w["attn_q_gain"][l], 2)[None]
        gk2 = jnp.tile(w["attn_k_gain"][l], 2)[None]
        wdw = jnp.pad(w["conv_w_dw"][l], ((0, HALO - CONV_W), (0, 0)))
        bdw, lng, lnb = w["conv_b_dw"][l][None], w["conv_ln_g"][l][None], w["conv_ln_b"][l][None]
        bg = w["b_gate"][l][None]

        x0 = x
        h1, ab1 = _norm_mm(x0, g1, w["ffn1_w_up"], l, tm=tml, tn=tff, ntiles=4, pieces=2, transposed=False,
                           name=f"ffn1_up_{l}")
        x1 = _ffn_down(ab1, w["ffn1_w_down"], l, x0, tm=tm, tk=tff, name=f"ffn1_down_{l}")
        h2, pa = _norm_mm(x1, gm, w["w_in"], l, tm=tml, tn=tna, ntiles=3, pieces=1, transposed=True, name=f"win_a_{l}")
        pa = pa[0]
        gl = _mm_t(h2, w["w_in"], l, tm=tml, tn=tna, off=3, ntiles=3 * d // tna, name=f"win_g_{l}")
        xr, xi, yp, zg, s5o = _s5_fwd(pa, spb, w["s5_w_glu"], l, bl=bl, s=s, t=ts5, name=f"s5_fwd_{l}")
        atto = _attn_fwd(pa, gq2, gk2, bias, bl=bl, s=s, datt=datt, qoff=qoff, name=f"attn_fwd_{l}")
        hg, hc, convo = _conv_fwd(pa, wdw, bdw, lng, lnb, bl=bl, s=s, t=tconv, acol=acol, name=f"conv_fwd_{l}")
        brs = (s5o, atto, convo)
        x2 = _mix_out_fwd(x1, brs, gl, bg, wbs, w["w_out"], l, tm=tmix, name=f"mix_fwd_{l}")
        h3, ab2 = _norm_mm(x2, g2, w["ffn2_w_up"], l, tm=tml, tn=tff, ntiles=4, pieces=2, transposed=False,
                           name=f"ffn2_up_{l}")
        x = _ffn_down(ab2, w["ffn2_w_down"], l, x2, tm=tm, tk=tff, name=f"ffn2_down_{l}")
        saved.append(dict(spb=spb, bias=bias, prep_vjp=prep_vjp, g1=g1, g2=g2, gm=gm, gq2=gq2, gk2=gk2,
                          wdw=wdw, lng=lng, lnb=lnb, bg=bg, x0=x0, h1=h1, ab1=ab1, x1=x1, h2=h2, pa=pa, gl=gl,
                          xr=xr, xi=xi, yp=yp, zg=zg, hg=hg, hc=hc, brs=brs, x2=x2, h3=h3, ab2=ab2))

    dx, lsum = _loss_grad(x, target, tm=tm, name="loss")
    loss_part = 0.5 * jnp.sum(lsum) / d

    big = {k: None for k in _BIG_KEYS}
    small = {k: [None] * nl for k in _SMALL_KEYS}
    hooks = {"pending": None}
    assert rs is None or nl == 2
    for l in reversed(range(nl)):
        sv = saved[l]

        def ffn_bwd(dx, xin, h, ab, g, tag):
            wu, wd = w[tag + "_w_up"], w[tag + "_w_down"]
            dab, big[tag + "_w_down"] = _ffn_dact(dx, wd, l, ab, nl, big[tag + "_w_down"], tm=tm, tk=tff,
                                                  name=f"{tag}_dact_{l}")
            big[tag + "_w_up"] = _mm_tn(h[None], dab, l, nl, big[tag + "_w_up"], ta=d, tb=tff, tk=tml,
                                        name=f"{tag}_dwu_{l}")
            comm = None
            if rs is not None and l == 0:
                comm = ([big[k] for k in _BIG_KEYS], 1 if tag == "ffn2" else 0)
            dxo, dg, *recv = _ffn_dx(dab, wu, l, xin, g, dx, tm=tml, tk=tff, name=f"{tag}_dx_{l}", comm=comm)
            small[tag + "_norm"][l] = dg[0]
            if comm is not None and tag == "ffn2":
                hooks["pending"] = rs.sums(big, recv, 1)
            elif comm is not None:
                rs.recv0 = recv
            return dxo

        dx = ffn_bwd(dx, sv["x2"], sv["h3"], sv["ab2"], sv["g2"], "ffn2")

        mix_keys = ("w_br_s5", "w_br_attn", "w_br_conv", "w_out")
        bufs = None if big["w_out"] is None else [big[k] for k in mix_keys]
        ds5o, datto, dconvo, dgl, dbg, *dws = _mix_out_bwd(
            dx, sv["brs"], sv["gl"], sv["bg"], wbs, w["w_out"], l, nl, bufs, tm=tmix, name=f"mix_bwd_{l}")
        small["b_gate"][l] = dbg[0]
        big.update(zip(mix_keys, dws))

        dhc, dlng, dlnb = _conv_bwd_ln(dconvo, sv["hc"], sv["lng"], sv["lnb"], tm=tm, name=f"conv_bwd_ln_{l}")
        dz, dwdw, dbdw = _conv_bwd_dw(dhc, sv["hg"], sv["pa"], sv["wdw"], bl=bl, s=s, t=tconv, acol=acol,
                                      name=f"conv_bwd_dw_{l}")
        small["conv_w_dw"][l] = dwdw[:CONV_W]
        small["conv_b_dw"][l], small["conv_ln_g"][l], small["conv_ln_b"][l] = dbdw[0], dlng[0], dlnb[0]

        comm = hooks["pending"] if l == 0 else None
        dq, dkn, dvw, dbias, dgq, *hosted = _attn_bwd(datto, sv["pa"], sv["gq2"], sv["gk2"], sv["bias"], bl=bl, s=s,
                                                      datt=datt, qoff=qoff, name=f"attn_bwd_{l}", comm=comm)
        if comm is not None:
            rs.parts = hosted
        dk, dv, dgk = _attn_kv_bwd(dkn, dvw, sv["pa"], sv["gk2"], bl=bl, s=s, datt=datt, tm=_tile(s, 512), koff=koff,
                                   name=f"attn_kv_bwd_{l}")
        small["attn_q_gain"][l] = jnp.sum(dgq.reshape(-1, HEAD_DIM), axis=0)
        small["attn_k_gain"][l] = jnp.sum(dgk.reshape(-1, HEAD_DIM), axis=0)

        du, dd, dcr, dci, dbr, dbi, da, big["s5_w_glu"] = _s5_bwd(
            ds5o, sv["yp"], sv["zg"], sv["xr"], sv["xi"], sv["pa"], sv["spb"], w["s5_w_glu"], l, nl, big["s5_w_glu"],
            bl=bl, s=s, t=ts5, name=f"s5_bwd_{l}")
        prep_ct = (dict(a=da, bblk_r=dbr, bblk_i=dbi, cblk_r=dcr, cblk_in=dci, d=dd), jnp.sum(dbias, axis=0))
        (dprep,) = sv["prep_vjp"](prep_ct)
        for k in _PREP_KEYS:
            small[k][l] = dprep[k][l]

        dpa = jnp.concatenate([du, dq, dk, dv, dz], axis=1)
        big["w_in"] = _dwin_t(dpa, dgl, sv["h2"], l, nl, big["w_in"], ta=tna, tk=tml, name=f"dwin_{l}")
        dx, dgm = _mix_dx(dpa, dgl, w["w_in"], l, sv["x1"], sv["gm"], dx, tm=tml, tk=tna, name=f"mix_dx_{l}")
        small["mix_norm"][l] = dgm[0]

        dx = ffn_bwd(dx, sv["x0"], sv["h1"], sv["ab1"], sv["g1"], "ffn1")

    small = {k: jnp.stack(v) for k, v in small.items()}
    return loss_part, dx.reshape(bl, s, d), big, small


def _place():
    x, y, c = lax.axis_index("x"), lax.axis_index("y"), lax.axis_index("c")
    chips = [(1 - x, y), (x, 1 - y), (1 - x, 1 - y)]
    return x, y, c, chips


def _remote(src, dst, send_sems, recv_sems, k, dev):
    return pltpu.make_async_remote_copy(src_ref=src, dst_ref=dst, send_sem=send_sems.at[k], recv_sem=recv_sems.at[k],
                                        device_id=dev, device_id_type=MESH)


def _window(ref, lead, s, axis, blk):
    if axis == 1:
        sl = (pl.ds(pl.multiple_of(s * blk, 16), blk), slice(None))
    else:
        sl = (slice(None), pl.ds(pl.multiple_of(s * blk, LANES), blk))
    return ref.at[sl] if lead is None else ref.at[(lead,) + sl]


def _all_gather_weights(fulls, axes, taps):
    nw = len(fulls)
    assert fulls[0].shape[0] == 2
    blks = [f.shape[ax] // N_CHIPS for f, ax in zip(fulls, axes)]

    def body(*refs):
        taps_in = refs[nw]
        outs, taps_out = refs[nw + 1:2 * nw + 1], refs[2 * nw + 1]
        send_sems, recv_sems, local_sem = refs[-3:]
        x, y, c, chips = _place()
        s_me = 2 * x + y
        sibling = (x, y, 1 - c)
        win = lambda i, lyr, s: _window(outs[i], lyr, s, axes[i], blks[i])
        own_taps = pltpu.make_async_copy(taps_in, taps_out.at[s_me], local_sem)
        own_taps.start()
        sends = []
        for i in range(nw):
            for j, (cx, cy) in enumerate(chips):
                mine = win(i, c, s_me)
                sends.append(_remote(mine, mine, send_sems, recv_sems, 6 * i + j, (cx, cy, c)))
        for j, (cx, cy) in enumerate(chips):
            sends.append(_remote(taps_in, taps_out.at[s_me], send_sems, recv_sems, 6 * nw + j, (cx, cy, c)))
        for cp in sends:
            cp.start()
        for i in range(nw):
            for j, (cx, cy) in enumerate(chips):
                piece = win(i, c, 2 * cx + cy)
                _remote(piece, piece, send_sems, recv_sems, 6 * i + j, (cx, cy, c)).wait_recv()
                fw = _remote(piece, piece, send_sems, recv_sems, 6 * i + 3 + j, sibling)
                fw.start()
                sends.append(fw)
        for i in range(nw):
            for j, (cx, cy) in enumerate(chips):
                piece = win(i, 1 - c, 2 * cx + cy)
                _remote(piece, piece, send_sems, recv_sems, 6 * i + 3 + j, sibling).wait_recv()
        for j, (cx, cy) in enumerate(chips):
            slab = taps_out.at[2 * cx + cy]
            _remote(slab, slab, send_sems, recv_sems, 6 * nw + j, (cx, cy, c)).wait_recv()
        for cp in sends:
            cp.wait_send()
        own_taps.wait()

    nsem = 6 * nw + 3
    return pl.pallas_call(
        body, in_specs=[_ANY] * (nw + 1), out_specs=[_ANY] * (nw + 1),
        out_shape=[_sds(f.shape, f.dtype) for f in fulls] + [_sds((N_CHIPS,) + taps.shape, taps.dtype)],
        input_output_aliases={i: i for i in range(nw)},
        scratch_shapes=[pltpu.SemaphoreType.DMA((nsem,)), pltpu.SemaphoreType.DMA((nsem,)), pltpu.SemaphoreType.DMA],
        name="all_gather_weights")(*fulls, taps)


def _swap_ops(ins, outs, send_sems, recv_sems, lyr):
    x, y, c, _ = _place()
    cps = [_remote(ins[i].at[lyr], outs[i], send_sems, recv_sems, i, (x, y, lyr)) for i in range(len(ins))]

    def start():
        @pl.when(c != lyr)
        def _():
            for cp in cps:
                cp.start()

    def wait():
        @pl.when(c != lyr)
        def _():
            for cp in cps:
                cp.wait_send()

        @pl.when(c == lyr)
        def _():
            for cp in cps:
                cp.wait_recv()

    return start, wait


def _exchange_ops(ins, outs, send_sems, recv_sems, axes, lyr):
    x, y, c, chips = _place()
    s_me = 2 * x + y
    nw = len(ins)
    blks = [r.shape[ax - 1] // N_CHIPS for r, ax in zip(ins, axes)]
    win = lambda i, s: _window(ins[i], None, s, axes[i], blks[i])
    sends = [_remote(win(i, 2 * cx + cy), outs[i].at[s_me], send_sems, recv_sems, 3 * i + j, (cx, cy, lyr))
             for i in range(nw) for j, (cx, cy) in enumerate(chips)]

    def start():
        @pl.when(c == lyr)
        def _():
            for cp in sends:
                cp.start()

    def wait():
        @pl.when(c == lyr)
        def _():
            for i in range(nw):
                for j, (cx, cy) in enumerate(chips):
                    slab = outs[i].at[2 * cx + cy]
                    _remote(slab, slab, send_sems, recv_sems, 3 * i + j, (cx, cy, lyr)).wait_recv()
            for cp in sends:
                cp.wait_send()

    return start, wait


def _parts_shapes(rsums, axes):
    shard = [tuple(dim // N_CHIPS if i == ax - 1 else dim for i, dim in enumerate(r.shape)) for r, ax in zip(rsums, axes)]
    return [_sds((N_CHIPS,) + sh, r.dtype) for sh, r in zip(shard, rsums)]


def _rs_exchange(rsums, axes, lyr):
    nw = len(rsums)

    def body(*refs):
        start, wait = _exchange_ops(refs[:nw], refs[nw:2 * nw], refs[-2], refs[-1], axes, lyr)
        start()
        wait()

    return pl.pallas_call(
        body, in_specs=[_ANY] * nw, out_specs=[_ANY] * nw, out_shape=_parts_shapes(rsums, axes),
        scratch_shapes=[pltpu.SemaphoreType.DMA((3 * nw,)), pltpu.SemaphoreType.DMA((3 * nw,))],
        name=f"rs_exchange_l{lyr}")(*rsums)


def _rs_join(ts):
    nw = len(ts)

    def body(*refs):
        outs = refs[nw:2 * nw]
        send_sems, recv_sems = refs[-2:]
        x, y, c, _ = _place()
        sends = [_remote(outs[i].at[c], outs[i].at[c], send_sems, recv_sems, i, (x, y, 1 - c)) for i in range(nw)]
        for cp in sends:
            cp.start()
        for i in range(nw):
            slab = outs[i].at[1 - c]
            _remote(slab, slab, send_sems, recv_sems, i, (x, y, 1 - c)).wait_recv()
        for cp in sends:
            cp.wait_send()

    return pl.pallas_call(
        body, in_specs=[_ANY] * nw, out_specs=[_ANY] * nw, out_shape=[_sds(t.shape, t.dtype) for t in ts],
        input_output_aliases={i: i for i in range(nw)},
        scratch_shapes=[pltpu.SemaphoreType.DMA((nw,)), pltpu.SemaphoreType.DMA((nw,))],
        name="rs_join_layers")(*ts)


def _all_reduce_small(arrs):
    na = len(arrs)
    nd = 8

    def body(*refs):
        ins, outs, recvs = refs[:na], refs[na:2 * na], refs[2 * na:3 * na]
        send_sems, recv_sems = refs[-2:]
        x, y, c, _ = _place()
        me = 4 * x + 2 * y + c
        for i in range(na):
            recvs[i][0] = ins[i][...]
        cps = []
        for rel in range(1, nd):
            dev = (1 - x if rel & 4 else x, 1 - y if rel & 2 else y, 1 - c if rel & 1 else c)
            for i in range(na):
                cp = _remote(ins[i], recvs[i].at[rel], send_sems, recv_sems, (rel - 1) * na + i, dev)
                cp.start()
                cps.append(cp)
        for rel in range(1, nd):
            for i in range(na):
                _remote(ins[i], recvs[i].at[rel], send_sems, recv_sems, (rel - 1) * na + i, (x, y, c)).wait_recv()
        for i in range(na):
            acc = recvs[i][me]
            for dv in range(1, nd):
                acc = acc + recvs[i][lax.bitwise_xor(me, dv)]
            outs[i][...] = acc
        for cp in cps:
            cp.wait_send()

    vm = pl.BlockSpec(memory_space=pltpu.VMEM)
    nsem = (nd - 1) * na
    return pl.pallas_call(
        body, in_specs=[vm] * na, out_specs=[vm] * na, out_shape=[_sds(t.shape, F32) for t in arrs],
        scratch_shapes=[pltpu.VMEM((nd,) + t.shape, F32) for t in arrs]
        + [pltpu.SemaphoreType.DMA((nsem,)), pltpu.SemaphoreType.DMA((nsem,))],
        compiler_params=pltpu.CompilerParams(vmem_limit_bytes=VMEM_LIMIT), name="all_reduce_small")(*arrs)


def _adamw_small(ws, gs, ms, vs):
    na = len(ws)
    c1 = 1.0 - ADAM_B1 ** ADAM_STEP
    c2 = 1.0 - ADAM_B2 ** ADAM_STEP

    def body(*refs):
        w_r, g_r, m_r, v_r = (refs[k * na:(k + 1) * na] for k in range(4))
        d_r, nm_r, nv_r = (refs[(4 + k) * na:(5 + k) * na] for k in range(3))
        for i in range(na):
            gv = g_r[i][...]
            mn = ADAM_B1 * m_r[i][...] + (1.0 - ADAM_B1) * gv
            vn = ADAM_B2 * v_r[i][...] + (1.0 - ADAM_B2) * (gv * gv)
            nm_r[i][...] = mn
            nv_r[i][...] = vn
            d_r[i][...] = -ADAM_LR * ((mn / c1) / (jnp.sqrt(vn / c2) + ADAM_EPS) + ADAM_WD * w_r[i][...])

    vm = pl.BlockSpec(memory_space=pltpu.VMEM)
    res = pl.pallas_call(
        body, in_specs=[vm] * (4 * na), out_specs=[vm] * (3 * na), out_shape=[_sds(t.shape, F32) for t in ws] * 3,
        compiler_params=pltpu.CompilerParams(vmem_limit_bytes=VMEM_LIMIT), name="adamw_small")(*ws, *gs, *ms, *vs)
    return res[:na], res[na:2 * na], res[2 * na:]


def kernel(x, ffn1_norm, ffn1_w_up, ffn1_w_down, mix_norm, w_in, b_gate, s5_lambda_re, s5_lambda_im, s5_log_dt, s5_b_re, s5_b_im, s5_c_re, s5_c_im, s5_d, s5_w_glu, w_br_s5, attn_q_gain, attn_k_gain, attn_rel_bias, w_br_attn, conv_w_dw, conv_b_dw, conv_ln_g, conv_ln_b, w_br_conv, w_out, ffn2_norm, ffn2_w_up, ffn2_w_down, loss_target, m_ffn1_norm, m_ffn1_w_up, m_ffn1_w_down, m_mix_norm, m_w_in, m_b_gate, m_s5_lambda_re, m_s5_lambda_im, m_s5_log_dt, m_s5_b_re, m_s5_b_im, m_s5_c_re, m_s5_c_im, m_s5_d, m_s5_w_glu, m_w_br_s5, m_attn_q_gain, m_attn_k_gain, m_attn_rel_bias, m_w_br_attn, m_conv_w_dw, m_conv_b_dw, m_conv_ln_g, m_conv_ln_b, m_w_br_conv, m_w_out, m_ffn2_norm, m_ffn2_w_up, m_ffn2_w_down, v_ffn1_norm, v_ffn1_w_up, v_ffn1_w_down, v_mix_norm, v_w_in, v_b_gate, v_s5_lambda_re, v_s5_lambda_im, v_s5_log_dt, v_s5_b_re, v_s5_b_im, v_s5_c_re, v_s5_c_im, v_s5_d, v_s5_w_glu, v_w_br_s5, v_attn_q_gain, v_attn_k_gain, v_attn_rel_bias, v_w_br_attn, v_conv_w_dw, v_conv_b_dw, v_conv_ln_g, v_conv_ln_b, v_w_br_conv, v_w_out, v_ffn2_norm, v_ffn2_w_up, v_ffn2_w_down):
    a = dict(locals())
    xi, yi, ci = lax.axis_index("x"), lax.axis_index("y"), lax.axis_index("c")
    s_me = 2 * xi + yi
    big_keys = list(_BIG_KEYS)
    axes = [1 if k == "w_in" else _BIG_KEYS[k] for k in big_keys]

    s_idx = s_me.astype(jnp.int32).reshape(1)
    placed = [_place_shard(jnp.swapaxes(a[k], 1, 2).astype(BF16) if k == "w_in" else a[k], ax, s_idx,
                           name=f"place_{k}") for k, ax in zip(big_keys, axes)]
    *fulls, taps = _all_gather_weights(placed, axes, a["conv_w_dw"])
    flat = lambda t: t.reshape(t.shape[0], t.shape[1], -1) if t.ndim == 4 else t
    w = {k: flat(a[k]) for k in _WEIGHTS}
    w.update(zip(big_keys, fulls))
    w["conv_w_dw"] = jnp.moveaxis(taps, 0, 2).reshape(taps.shape[1], taps.shape[2], -1)

    class _ReduceScatter:
        parts = recv0 = rsums1 = None

        def sums(self, big, recv, lyr):
            rsums = [_add_sibling(big[k], r, lyr, name=f"rs_add_sibling_{k}_l{lyr}") for r, k in zip(recv, big_keys)]
            if lyr == 1:
                self.rsums1 = rsums
            return rsums, axes, lyr

    rs = _ReduceScatter()
    loss_part, grad_x, gbig, gsmall = _local_step(a["x"], a["loss_target"], w, rs)
    loss = lax.psum(loss_part, ("x", "y", "c"))

    rsums0 = rs.sums(gbig, rs.recv0, 0)[0]
    mine = [None] * len(big_keys)
    for lyr, rsums, parts in ((1, rs.rsums1, rs.parts), (0, rsums0, _rs_exchange(rsums0, axes, 0))):
        mine = [_add_chips(r, p, ax, s_idx, lyr, buf, name=f"rs_add_chips_{k}_l{lyr}")
                for r, p, ax, buf, k in zip(rsums, parts, axes, mine, big_keys)]
    gb = dict(zip(big_keys, _rs_join(mine)))
    gb["w_in"] = jnp.swapaxes(gb["w_in"], 1, 2)

    small_keys = list(_SMALL_KEYS)
    gs = dict(zip(small_keys, _all_reduce_small([gsmall[k] for k in small_keys])))
    blk = a["conv_w_dw"].shape[2]
    gs["conv_w_dw"] = lax.dynamic_slice_in_dim(gs["conv_w_dw"], s_me * blk, blk, axis=2)

    delta, new_m, new_v = {}, {}, {}
    for k in big_keys:
        shp = a[k].shape
        two_d = lambda t: t.reshape(-1, shp[-1])
        d_, m_, v_ = _adamw(two_d(a[k]), two_d(gb[k]), two_d(a["m_" + k]), two_d(a["v_" + k]), name=f"adamw_{k}")
        delta[k], new_m[k], new_v[k] = d_.reshape(shp), m_.reshape(shp), v_.reshape(shp)
    res = _adamw_small([flat(a[k]) for k in small_keys], [gs[k] for k in small_keys],
                       [flat(a["m_" + k]) for k in small_keys], [flat(a["v_" + k]) for k in small_keys])
    for dst, vals in zip((delta, new_m, new_v), res):
        dst.update({k: t.reshape(a[k].shape) for k, t in zip(small_keys, vals)})
    grads = {**gb, **{k: t.reshape(a[k].shape) for k, t in gs.items()}}

    return (loss, grad_x, *[grads[k] for k in _WEIGHTS], *[delta[k] for k in _WEIGHTS],
            *[new_m[k] for k in _WEIGHTS], *[new_v[k] for k in _WEIGHTS])
```

```python
import functools
import math

import numpy as np
import jax
import jax.numpy as jnp
from jax import lax
from jax.experimental import pallas as pl
from jax.experimental.pallas import tpu as pltpu

F32 = jnp.float32
BF16 = jnp.bfloat16
EPS = 1e-6
VMEM_LIMIT = 56 * 1024 * 1024
LANES = 128
HEAD_DIM = 64
CHUNK = 64
N_LEFT = 8
MAX_REL = 128
ATT_TQ = 256
CONV_W = 31
HALO = 32
ROW_CHUNK = 256
SUBLANES = 8
GROUP_LOG = 3
NEG = -1e30
N_CHIPS = 4
PACK_COLS = 1024

ADAM_LR = 0.001
ADAM_B1 = 0.9
ADAM_B2 = 0.999
ADAM_EPS = 1e-08
ADAM_WD = 0.01
ADAM_STEP = 10

MESH = pl.DeviceIdType.MESH
_ANY = pl.BlockSpec(memory_space=pl.ANY)


def _cp(*sem):
    return pltpu.CompilerParams(dimension_semantics=sem, vmem_limit_bytes=VMEM_LIMIT)


def _sds(shape, dtype):
    return jax.ShapeDtypeStruct(shape, dtype)


def _tile(n, pref):
    t = min(n, pref)
    while n % t:
        t -= 8
    return t


def _sigmoid(x):
    return jax.nn.sigmoid(x)


_GELU_C = math.sqrt(2.0 / math.pi)


def _gelu(y):
    return 0.5 * y * (1.0 + jnp.tanh(_GELU_C * (y + 0.044715 * y * y * y)))


def _gelu_grad(y):
    th = jnp.tanh(_GELU_C * (y + 0.044715 * y * y * y))
    return 0.5 * (1.0 + th) + 0.5 * y * (1.0 - th * th) * _GELU_C * (1.0 + 3.0 * 0.044715 * y * y)


def _dot(a, b):
    return jnp.dot(a, b, preferred_element_type=F32)


def _dot_t0(a, b):
    return lax.dot_general(a, b, (((0,), (0,)), ((), ())), preferred_element_type=F32)


def _dot_t1(a, b):
    return lax.dot_general(a, b, (((1,), (1,)), ((), ())), preferred_element_type=F32)


def _slab_out(nl, l, shape, buf, n_in):
    sds = _sds((nl,) + tuple(shape), F32)
    if buf is None:
        return [], [], sds, {}
    return [buf], [_ANY], sds, {n_in: 0}


def _norm_mm(x, g, w, l, *, tm, tn, ntiles, pieces, transposed, name, comm=None):
    n, d = x.shape
    m = ntiles * tn
    mp = m // pieces
    npj = mp // tn
    nc = 0 if comm is None else len(comm[0])

    def body(x_ref, g_ref, w_ref, *rest):
        h_ref, y_ref = rest[nc:nc + 2]
        h_scr = rest[2 * nc + 2]
        if comm is not None:
            finish = _hosted_gather_steps(comm, rest[nc + 2:2 * nc + 2], rest[-2:],
                                          pl.program_id(0) * ntiles + pl.program_id(1), (n // tm) * ntiles)

        @pl.when(pl.program_id(1) == 0)
        def _():
            for r0 in range(0, tm, ROW_CHUNK):
                rows = slice(r0, r0 + ROW_CHUNK)
                xv = x_ref[rows, :]
                r = lax.rsqrt(jnp.mean(xv * xv, axis=-1, keepdims=True) + EPS)
                hb = (xv * r * g_ref[...]).astype(BF16)
                h_scr[rows, :] = hb
                h_ref[rows, :] = hb

        mm = _dot_t1 if transposed else _dot
        y_ref[...] = mm(h_scr[...], w_ref[...]).astype(BF16)
        if comm is not None:
            finish()

    wspec = (pl.BlockSpec((None, tn, d), lambda i, j: (l, j, 0)) if transposed
             else pl.BlockSpec((None, d, tn), lambda i, j: (l, 0, j)))
    c_in, c_ispec, c_ospec, c_oshape, c_scr, aliases = _hosted_gather(comm, 3, 2)
    return pl.pallas_call(
        body, grid=(n // tm, ntiles),
        in_specs=[pl.BlockSpec((tm, d), lambda i, j: (i, 0)), pl.BlockSpec((1, d), lambda i, j: (0, 0)), wspec] + c_ispec,
        out_specs=[pl.BlockSpec((tm, d), lambda i, j: (i, 0)),
                   pl.BlockSpec((None, tm, tn), lambda i, j: (j // npj, i, j % npj))] + c_ospec,
        out_shape=[_sds((n, d), BF16), _sds((pieces, n, mp), BF16)] + c_oshape,
        input_output_aliases=aliases,
        scratch_shapes=[pltpu.VMEM((tm, d), BF16)] + c_scr,
        compiler_params=_cp("arbitrary", "arbitrary"), name=name)(x, g, w, *c_in)


def _mm_t(a, w, l, *, tm, tn, off, ntiles, name):
    n, k = a.shape

    def body(a_ref, w_ref, y_ref):
        y_ref[...] = _dot_t1(a_ref[...], w_ref[...]).astype(BF16)

    return pl.pallas_call(
        body, grid=(n // tm, ntiles),
        in_specs=[pl.BlockSpec((tm, k), lambda i, j: (i, 0)), pl.BlockSpec((None, tn, k), lambda i, j: (l, off + j, 0))],
        out_specs=pl.BlockSpec((tm, tn), lambda i, j: (i, j)),
        out_shape=_sds((n, ntiles * tn), BF16),
        compiler_params=_cp("parallel", "arbitrary"), name=name)(a, w)


def _ffn_down(ab, wd, l, x, *, tm, tk, name):
    _, n, dff = ab.shape
    d = x.shape[1]
    nk = dff // tk

    def body(a_ref, b_ref, wd_ref, x_ref, o_ref, acc):
        k = pl.program_id(1)
        a = a_ref[...].astype(F32)
        b = b_ref[...].astype(F32)
        act = (a * _sigmoid(a) * b).astype(BF16)
        part = _dot(act, wd_ref[pl.ds(pl.multiple_of(k * tk, tk), tk), :])

        @pl.when(k == 0)
        def _():
            acc[...] = part

        @pl.when(k > 0)
        def _():
            acc[...] += part

        @pl.when(k == nk - 1)
        def _():
            o_ref[...] = x_ref[...] + 0.5 * acc[...]

    return pl.pallas_call(
        body, grid=(n // tm, nk),
        in_specs=[pl.BlockSpec((None, tm, tk), lambda i, k: (0, i, k)),
                  pl.BlockSpec((None, tm, tk), lambda i, k: (1, i, k)),
                  pl.BlockSpec((None, dff, d), lambda i, k: (l, 0, 0)),
                  pl.BlockSpec((tm, d), lambda i, k: (i, 0))],
        out_specs=pl.BlockSpec((tm, d), lambda i, k: (i, 0)),
        out_shape=_sds((n, d), F32),
        scratch_shapes=[pltpu.VMEM((tm, d), F32)],
        compiler_params=_cp("parallel", "arbitrary"), name=name)(ab, ab, wd, x)


def _ffn_dact(dx, wd, l, ab, nl, dwd_buf, *, tm, tk, name):
    n, d = dx.shape
    dff = ab.shape[2]
    half = ((tk // LANES + 1) // 2) * LANES
    chunks = ((0, half), (half, tk))

    def body(dx_ref, wd_ref, a_ref, b_ref, *rest):
        dab_ref, dwd_ref = rest[-2:]
        do = (0.5 * dx_ref[...]).astype(BF16)

        @pl.when(pl.program_id(1) == 0)
        def _():
            dwd_ref[...] = jnp.zeros_like(dwd_ref)

        for c0, c1 in chunks:
            dact = _dot_t1(do, wd_ref[c0:c1, :])
            a = a_ref[:, c0:c1].astype(F32)
            b = b_ref[:, c0:c1].astype(F32)
            sg = _sigmoid(a)
            silu = a * sg
            dab_ref[0, :, c0:c1] = (dact * b * (sg * (1.0 + a * (1.0 - sg)))).astype(BF16)
            dab_ref[1, :, c0:c1] = (dact * silu).astype(BF16)
            dwd_ref[c0:c1, :] += _dot_t0((silu * b).astype(BF16), do)

    extra, extra_specs, dwd_shape, aliases = _slab_out(nl, l, (dff, d), dwd_buf, 4)
    aliases = {k: 1 for k in aliases}
    return pl.pallas_call(
        body, grid=(dff // tk, n // tm),
        in_specs=[pl.BlockSpec((tm, d), lambda j, i: (i, 0)),
                  pl.BlockSpec((None, tk, d), lambda j, i: (l, j, 0)),
                  pl.BlockSpec((None, tm, tk), lambda j, i: (0, i, j)),
                  pl.BlockSpec((None, tm, tk), lambda j, i: (1, i, j)), *extra_specs],
        out_specs=[pl.BlockSpec((2, tm, tk), lambda j, i: (0, i, j)),
                   pl.BlockSpec((None, tk, d), lambda j, i: (l, j, 0))],
        out_shape=[_sds((2, n, dff), BF16), dwd_shape],
        input_output_aliases=aliases,
        compiler_params=_cp("arbitrary", "arbitrary"), name=name)(dx, wd, ab, ab, *extra)


def _rms_bwd_epilogue(acc, x_ref, g_ref, dres_ref, dx_ref, dg_ref, i):
    dgp = jnp.zeros(dg_ref.shape, F32)
    for r0 in range(0, acc.shape[0], ROW_CHUNK):
        rows = slice(r0, r0 + ROW_CHUNK)
        dh = acc[rows, :]
        xv = x_ref[rows, :]
        r = lax.rsqrt(jnp.mean(xv * xv, axis=-1, keepdims=True) + EPS)
        xn = xv * r
        dgp = dgp + jnp.sum(dh * xn, axis=0, keepdims=True)
        dxh = dh * g_ref[...]
        dx_ref[rows, :] = dres_ref[rows, :] + r * (dxh - xn * jnp.mean(dxh * xn, axis=-1, keepdims=True))

    @pl.when(i == 0)
    def _():
        dg_ref[...] = dgp

    @pl.when(i > 0)
    def _():
        dg_ref[...] += dgp


def _ffn_dx(dab, wu, l, x, g, dres, *, tm, tk, name, comm=None):
    p, n, mp = dab.shape
    d = x.shape[1]
    nkp = mp // tk
    nk = p * nkp
    ni = n // tm
    nc = 0 if comm is None else len(comm[0])

    def body(dy_ref, w_ref, x_ref, g_ref, dres_ref, *rest):
        dx_ref, dg_ref = rest[nc:nc + 2]
        acc = rest[2 * nc + 2]
        k = pl.program_id(1)
        if comm is not None:
            start, wait = _swap_ops(rest[:nc], rest[nc + 2:2 * nc + 2], rest[-2], rest[-1], comm[1])

            @pl.when((pl.program_id(0) == 0) & (k == 0))
            def _():
                start()

        part = _dot_t1(dy_ref[...], w_ref[...])

        @pl.when(k == 0)
        def _():
            acc[...] = part

        @pl.when(k > 0)
        def _():
            acc[...] += part

        @pl.when(k == nk - 1)
        def _():
            _rms_bwd_epilogue(acc, x_ref, g_ref, dres_ref, dx_ref, dg_ref, pl.program_id(0))

        if comm is not None:
            @pl.when((pl.program_id(0) == ni - 1) & (k == nk - 1))
            def _():
                wait()

    comm_in = [] if comm is None else list(comm[0])
    comm_out = [_sds(t.shape[1:], t.dtype) for t in comm_in]
    comm_scr = [] if comm is None else [pltpu.SemaphoreType.DMA((nc,)), pltpu.SemaphoreType.DMA((nc,))]
    return pl.pallas_call(
        body, grid=(ni, nk),
        in_specs=[pl.BlockSpec((None, tm, tk), lambda i, k: (k // nkp, i, k % nkp)),
                  pl.BlockSpec((None, d, tk), lambda i, k: (l, 0, k)),
                  pl.BlockSpec((tm, d), lambda i, k: (i, 0)),
                  pl.BlockSpec((1, d), lambda i, k: (0, 0)),
                  pl.BlockSpec((tm, d), lambda i, k: (i, 0))] + [_ANY] * nc,
        out_specs=[pl.BlockSpec((tm, d), lambda i, k: (i, 0)), pl.BlockSpec((1, d), lambda i, k: (0, 0))] + [_ANY] * nc,
        out_shape=[_sds((n, d), F32), _sds((1, d), F32)] + comm_out,
        scratch_shapes=[pltpu.VMEM((tm, d), F32)] + comm_scr,
        compiler_params=_cp("arbitrary", "arbitrary"), name=name)(dab, wu, x, g, dres, *comm_in)


def _mix_dx(dpa, dgl, wt, l, x, g, dres, *, tm, tk, name):
    n, d = x.shape
    n1 = dpa.shape[1] // tk
    n2 = dgl.shape[1] // tk
    nk = n1 + n2

    def body(d1_ref, d2_ref, w_ref, x_ref, g_ref, dres_ref, dx_ref, dg_ref, acc):
        k = pl.program_id(1)

        @pl.when(k == 0)
        def _():
            acc[...] = _dot(d1_ref[...], w_ref[...])

        @pl.when((k > 0) & (k < n1))
        def _():
            acc[...] += _dot(d1_ref[...], w_ref[...])

        @pl.when(k >= n1)
        def _():
            acc[...] += _dot(d2_ref[...], w_ref[...])

        @pl.when(k == nk - 1)
        def _():
            _rms_bwd_epilogue(acc, x_ref, g_ref, dres_ref, dx_ref, dg_ref, pl.program_id(0))

    return pl.pallas_call(
        body, grid=(n // tm, nk),
        in_specs=[pl.BlockSpec((tm, tk), lambda i, k: (i, jnp.minimum(k, n1 - 1))),
                  pl.BlockSpec((tm, tk), lambda i, k: (i, jnp.maximum(k - n1, 0))),
                  pl.BlockSpec((None, tk, d), lambda i, k: (l, k, 0)),
                  pl.BlockSpec((tm, d), lambda i, k: (i, 0)),
                  pl.BlockSpec((1, d), lambda i, k: (0, 0)),
                  pl.BlockSpec((tm, d), lambda i, k: (i, 0))],
        out_specs=[pl.BlockSpec((tm, d), lambda i, k: (i, 0)), pl.BlockSpec((1, d), lambda i, k: (0, 0))],
        out_shape=[_sds((n, d), F32), _sds((1, d), F32)],
        scratch_shapes=[pltpu.VMEM((tm, d), F32)],
        compiler_params=_cp("arbitrary", "arbitrary"), name=name)(dpa, dgl, wt, x, g, dres)


def _mm_tn(a, b, l, nl, buf, *, ta, tb, tk, name):
    pa, n, ka = a.shape
    pb, _, kb = b.shape
    nap = ka // ta
    nbp = kb // tb

    def body(a_ref, b_ref, *rest):
        o_ref = rest[-1]

        @pl.when(pl.program_id(2) == 0)
        def _():
            o_ref[...] = jnp.zeros_like(o_ref)

        o_ref[...] += _dot_t0(a_ref[...], b_ref[...])

    extra, extra_specs, out_shape, aliases = _slab_out(nl, l, (pa * ka, pb * kb), buf, 2)
    return pl.pallas_call(
        body, grid=(pa * nap, pb * nbp, n // tk),
        in_specs=[pl.BlockSpec((None, tk, ta), lambda i, j, k: (i // nap, k, i % nap)),
                  pl.BlockSpec((None, tk, tb), lambda i, j, k: (j // nbp, k, j % nbp)), *extra_specs],
        out_specs=pl.BlockSpec((None, ta, tb), lambda i, j, k: (l, i, j)),
        out_shape=out_shape, input_output_aliases=aliases,
        compiler_params=_cp("parallel", "parallel", "arbitrary"), name=name)(a, b, *extra)


def _dwin_t(dpa, dgl, h, l, nl, buf, *, ta, tk, name):
    n, d = h.shape
    n1 = dpa.shape[1] // ta
    n2 = dgl.shape[1] // ta

    def body(a1_ref, a2_ref, h_ref, *rest):
        o_ref = rest[-1]
        i = pl.program_id(0)

        @pl.when(pl.program_id(1) == 0)
        def _():
            o_ref[...] = jnp.zeros_like(o_ref)

        @pl.when(i < n1)
        def _():
            o_ref[...] += _dot_t0(a1_ref[...], h_ref[...])

        @pl.when(i >= n1)
        def _():
            o_ref[...] += _dot_t0(a2_ref[...], h_ref[...])

    extra, extra_specs, out_shape, aliases = _slab_out(nl, l, ((n1 + n2) * ta, d), buf, 3)
    return pl.pallas_call(
        body, grid=(n1 + n2, n // tk),
        in_specs=[pl.BlockSpec((tk, ta), lambda i, k: (jnp.where(i < n1, k, 0), jnp.minimum(i, n1 - 1))),
                  pl.BlockSpec((tk, ta), lambda i, k: (jnp.where(i >= n1, k, 0), jnp.maximum(i - n1, 0))),
                  pl.BlockSpec((tk, d), lambda i, k: (k, 0)), *extra_specs],
        out_specs=pl.BlockSpec((None, ta, d), lambda i, k: (l, i, 0)),
        out_shape=out_shape, input_output_aliases=aliases,
        compiler_params=_cp("parallel", "arbitrary"), name=name)(dpa, dgl, h, *extra)


def _loss_grad(y, t, *, tm, name):
    n, d = y.shape

    def body(y_ref, t_ref, dy_ref, l_ref):
        e = y_ref[...] - t_ref[...]
        dy_ref[...] = e * (1.0 / d)
        part = jnp.sum(e * e, axis=0, keepdims=True)

        @pl.when(pl.program_id(0) == 0)
        def _():
            l_ref[...] = part

        @pl.when(pl.program_id(0) > 0)
        def _():
            l_ref[...] += part

    return pl.pallas_call(
        body, grid=(n // tm,),
        in_specs=[pl.BlockSpec((tm, d), lambda i: (i, 0)), pl.BlockSpec((tm, d), lambda i: (i, 0))],
        out_specs=[pl.BlockSpec((tm, d), lambda i: (i, 0)), pl.BlockSpec((1, d), lambda i: (0, 0))],
        out_shape=[_sds((n, d), F32), _sds((1, d), F32)],
        compiler_params=_cp("arbitrary"), name=name)(y, t)


def _s5_fwd(proj, sp, wglu, l, *, bl, s, t, name):
    n = bl * s
    ds5, gp = sp["bblk_r"].shape
    nt = s // t
    ng = t // SUBLANES
    glog = int(math.log2(ng))

    def body(u_ref, br_ref, bi_ref, pw_ref, p8_ref, cr_ref, ci_ref, d_ref, wg_ref,
             xr_ref, xi_ref, yp_ref, zg_ref, o_ref, carry, st):
        @pl.when(pl.program_id(1) == 0)
        def _():
            carry[...] = jnp.zeros_like(carry)

        u = u_ref[...]
        sub = lax.broadcasted_iota(jnp.int32, (t, gp), 0) % SUBLANES
        xr = _dot(u, br_ref[...])
        xi = _dot(u, bi_ref[...])
        for k in range(GROUP_LOG):
            sh = 1 << k
            pr = pw_ref[2 * k:2 * k + 1, :]
            pi = pw_ref[2 * k + 1:2 * k + 2, :]
            keep = sub >= sh
            sr = jnp.where(keep, pltpu.roll(xr, sh, 0), 0.0)
            si = jnp.where(keep, pltpu.roll(xi, sh, 0), 0.0)
            xr, xi = xr + pr * sr - pi * si, xi + pr * si + pi * sr
        xr_ref[...] = xr
        xi_ref[...] = xi
        grow = lax.broadcasted_iota(jnp.int32, (ng, gp), 0)
        cr = carry[0:1, :]
        ci = carry[1:2, :]
        a8r = pw_ref[2 * GROUP_LOG:2 * GROUP_LOG + 1, :]
        a8i = pw_ref[2 * GROUP_LOG + 1:2 * GROUP_LOG + 2, :]
        head = grow == 0
        for g in range(ng):
            st[g:g + 1, :] = xr_ref[(g + 1) * SUBLANES - 1:(g + 1) * SUBLANES, :]
            st[ng + g:ng + g + 1, :] = xi_ref[(g + 1) * SUBLANES - 1:(g + 1) * SUBLANES, :]
        sr_ = st[0:ng, :] + jnp.where(head, a8r * cr - a8i * ci, 0.0)
        si_ = st[ng:2 * ng, :] + jnp.where(head, a8r * ci + a8i * cr, 0.0)
        for k in range(glog):
            sh = 1 << k
            pr = pw_ref[2 * (GROUP_LOG + k):2 * (GROUP_LOG + k) + 1, :]
            pi = pw_ref[2 * (GROUP_LOG + k) + 1:2 * (GROUP_LOG + k) + 2, :]
            keep = grow >= sh
            tr = jnp.where(keep, pltpu.roll(sr_, sh, 0), 0.0)
            ti = jnp.where(keep, pltpu.roll(si_, sh, 0), 0.0)
            sr_, si_ = sr_ + pr * tr - pi * ti, si_ + pr * ti + pi * tr
        tail = grow == ng - 1
        carry[0:1, :] = jnp.sum(jnp.where(tail, sr_, 0.0), axis=0, keepdims=True)
        carry[1:2, :] = jnp.sum(jnp.where(tail, si_, 0.0), axis=0, keepdims=True)
        st[0:ng, :] = jnp.where(head, cr, pltpu.roll(sr_, 1, 0))
        st[ng:2 * ng, :] = jnp.where(head, ci, pltpu.roll(si_, 1, 0))
        p8r = p8_ref[0:SUBLANES, :]
        p8i = p8_ref[SUBLANES:2 * SUBLANES, :]
        for g in range(ng):
            grp = slice(g * SUBLANES, (g + 1) * SUBLANES)
            pr = st[g:g + 1, :]
            pi = st[ng + g:ng + g + 1, :]
            xr_ref[grp, :] = xr_ref[grp, :] + p8r * pr - p8i * pi
            xi_ref[grp, :] = xi_ref[grp, :] + p8r * pi + p8i * pr
        xr = xr_ref[...]
        xi = xi_ref[...]
        y = _dot(xr.astype(BF16), cr_ref[...]) + _dot(xi.astype(BF16), ci_ref[...]) + d_ref[...] * u.astype(F32)
        yp_ref[...] = y
        zg = _dot(_gelu(y).astype(BF16), wg_ref[...])
        zg_ref[...] = zg
        o_ref[...] = (zg[:, :ds5] * _sigmoid(zg[:, ds5:])).astype(BF16)

    const = lambda shape: pl.BlockSpec(shape, lambda b, i: (0, 0))
    row = lambda w: pl.BlockSpec((t, w), lambda b, i: (b * nt + i, 0))
    return pl.pallas_call(
        body, grid=(bl, nt),
        in_specs=[row(ds5), const((ds5, gp)), const((ds5, gp)), const((2 * (GROUP_LOG + glog), gp)),
                  const((2 * SUBLANES, gp)),
                  const((gp, ds5)), const((gp, ds5)), const((1, ds5)),
                  pl.BlockSpec((None, ds5, 2 * ds5), lambda b, i: (l, 0, 0))],
        out_specs=[row(gp), row(gp), row(ds5), row(2 * ds5), row(ds5)],
        out_shape=[_sds((n, gp), F32), _sds((n, gp), F32), _sds((n, ds5), F32), _sds((n, 2 * ds5), F32),
                   _sds((n, ds5), BF16)],
        scratch_shapes=[pltpu.VMEM((2, gp), F32), pltpu.VMEM((2 * ng, gp), F32)],
        compiler_params=_cp("arbitrary", "arbitrary"), name=name)(
            proj, sp["bblk_r"], sp["bblk_i"], sp["pw"], sp["p8"], sp["cblk_r"], sp["cblk_in"], sp["d"], wglu)


def _s5_bwd(ds, yp, zg, xr, xi, proj, sp, wglu, l, nl, dwg_buf, *, bl, s, t, name):
    n = bl * s
    ds5, gp = sp["bblk_r"].shape
    nt = s // t
    tb = t // 8
    ng = t // SUBLANES
    glog = int(math.log2(ng))

    def body(ds_ref, yp_ref, zg_ref, xr_ref, xi_ref, hr_ref, hi_ref, u_ref, wg_ref, cr_ref, ci_ref,
             br_ref, bi_ref, pw_ref, q8_ref, d_ref, *rest):
        du_ref, dd_ref, dcr_ref, dci_ref, dbr_ref, dbi_ref, da_ref, dwg_ref, carry, gr_scr, gi_scr, st = rest[-12:]
        b = pl.program_id(0)
        i = pl.program_id(1)
        tile = nt - 1 - i

        @pl.when((b == 0) & (i == 0))
        def _():
            for r in (dwg_ref, dd_ref, dcr_ref, dci_ref, dbr_ref, dbi_ref, da_ref):
                r[...] = jnp.zeros_like(r)

        @pl.when(i == 0)
        def _():
            carry[...] = jnp.zeros_like(carry)

        dsv = ds_ref[...].astype(F32)
        zgv = zg_ref[...]
        za = zgv[:, :ds5]
        sg = _sigmoid(zgv[:, ds5:])
        dzg = jnp.concatenate([dsv * sg, dsv * za * sg * (1.0 - sg)], axis=1).astype(BF16)
        y = yp_ref[...]
        dwg_ref[...] += _dot_t0(_gelu(y).astype(BF16), dzg)
        dy = _dot_t1(dzg, wg_ref[...]) * _gelu_grad(y)
        ub = u_ref[...]
        uf = ub.astype(F32)
        dd_ref[...] += jnp.sum(dy * uf, axis=0, keepdims=True)
        dyb = dy.astype(BF16)
        xrv = xr_ref[...]
        xiv = xi_ref[...]
        dcr_ref[...] += _dot_t0(xrv.astype(BF16), dyb)
        dci_ref[...] += _dot_t0(xiv.astype(BF16), dyb)

        rows = lax.broadcasted_iota(jnp.int32, (t, gp), 0)
        sub = rows % SUBLANES
        gr = _dot_t1(dyb, cr_ref[...])
        gi = _dot_t1(dyb, ci_ref[...])
        for k in range(GROUP_LOG):
            sh = 1 << k
            pr = pw_ref[2 * k:2 * k + 1, :]
            pi = pw_ref[2 * k + 1:2 * k + 2, :]
            keep = sub < SUBLANES - sh
            sr = jnp.where(keep, pltpu.roll(gr, t - sh, 0), 0.0)
            si = jnp.where(keep, pltpu.roll(gi, t - sh, 0), 0.0)
            gr, gi = gr + pr * sr + pi * si, gi + pr * si - pi * sr
        gr_scr[...] = gr
        gi_scr[...] = gi
        grow = lax.broadcasted_iota(jnp.int32, (ng, gp), 0)
        cr = carry[0:1, :]
        ci = carry[1:2, :]
        a8r = pw_ref[2 * GROUP_LOG:2 * GROUP_LOG + 1, :]
        a8i = pw_ref[2 * GROUP_LOG + 1:2 * GROUP_LOG + 2, :]
        tail = grow == ng - 1
        for g in range(ng):
            st[g:g + 1, :] = gr_scr[g * SUBLANES:g * SUBLANES + 1, :]
            st[ng + g:ng + g + 1, :] = gi_scr[g * SUBLANES:g * SUBLANES + 1, :]
        sr_ = st[0:ng, :] + jnp.where(tail, a8r * cr + a8i * ci, 0.0)
        si_ = st[ng:2 * ng, :] + jnp.where(tail, a8r * ci - a8i * cr, 0.0)
        for k in range(glog):
            sh = 1 << k
            pr = pw_ref[2 * (GROUP_LOG + k):2 * (GROUP_LOG + k) + 1, :]
            pi = pw_ref[2 * (GROUP_LOG + k) + 1:2 * (GROUP_LOG + k) + 2, :]
            keep = grow < ng - sh
            tr = jnp.where(keep, pltpu.roll(sr_, ng - sh, 0), 0.0)
            ti = jnp.where(keep, pltpu.roll(si_, ng - sh, 0), 0.0)
            sr_, si_ = sr_ + pr * tr + pi * ti, si_ + pr * ti - pi * tr
        head = grow == 0
        carry[0:1, :] = jnp.sum(jnp.where(head, sr_, 0.0), axis=0, keepdims=True)
        carry[1:2, :] = jnp.sum(jnp.where(head, si_, 0.0), axis=0, keepdims=True)
        st[0:ng, :] = jnp.where(tail, cr, pltpu.roll(sr_, ng - 1, 0))
        st[ng:2 * ng, :] = jnp.where(tail, ci, pltpu.roll(si_, ng - 1, 0))
        q8r = q8_ref[0:SUBLANES, :]
        q8i = q8_ref[SUBLANES:2 * SUBLANES, :]
        for g in range(ng):
            grp = slice(g * SUBLANES, (g + 1) * SUBLANES)
            pr = st[g:g + 1, :]
            pi = st[ng + g:ng + g + 1, :]
            gr_scr[grp, :] = gr_scr[grp, :] + q8r * pr - q8i * pi
            gi_scr[grp, :] = gi_scr[grp, :] + q8r * pi + q8i * pr
        gr = gr_scr[...]
        gi = gi_scr[...]
        first = rows == 0

        live = jnp.where(tile > 0, 1.0, 0.0)
        xpr = jnp.where(first, hr_ref[7:8, :] * live, pltpu.roll(xrv, 1, 0))
        xpi = jnp.where(first, hi_ref[7:8, :] * live, pltpu.roll(xiv, 1, 0))
        da_ref[0:1, :] += jnp.sum(gr * xpr + gi * xpi, axis=0, keepdims=True)
        da_ref[1:2, :] += jnp.sum(gi * xpr - gr * xpi, axis=0, keepdims=True)

        grb = gr.astype(BF16)
        gib = gi.astype(BF16)
        dbr_ref[...] += _dot_t0(ub, grb)
        dbi_ref[...] += _dot_t0(ub, gib)
        du_ref[...] = (_dot_t1(grb, br_ref[...]) + _dot_t1(gib, bi_ref[...]) + dy * d_ref[...]).astype(BF16)

    const = lambda shape: pl.BlockSpec(shape, lambda b, i: (0, 0))
    row = lambda w: pl.BlockSpec((t, w), lambda b, i: (b * nt + nt - 1 - i, 0))
    halo = pl.BlockSpec((8, gp), lambda b, i: (jnp.maximum((b * nt + nt - 1 - i) * tb - 1, 0), 0))
    extra, extra_specs, dwg_shape, aliases = _slab_out(nl, l, (ds5, 2 * ds5), dwg_buf, 16)
    aliases = {k: 7 for k in aliases}
    return pl.pallas_call(
        body, grid=(bl, nt),
        in_specs=[row(ds5), row(ds5), row(2 * ds5), row(gp), row(gp), halo, halo, row(ds5),
                  pl.BlockSpec((None, ds5, 2 * ds5), lambda b, i: (l, 0, 0)),
                  const((gp, ds5)), const((gp, ds5)), const((ds5, gp)), const((ds5, gp)),
                  const((2 * (GROUP_LOG + glog), gp)), const((2 * SUBLANES, gp)), const((1, ds5)), *extra_specs],
        out_specs=[row(ds5), const((1, ds5)), const((gp, ds5)), const((gp, ds5)),
                   const((ds5, gp)), const((ds5, gp)), const((2, gp)),
                   pl.BlockSpec((None, ds5, 2 * ds5), lambda b, i: (l, 0, 0))],
        out_shape=[_sds((n, ds5), BF16), _sds((1, ds5), F32), _sds((gp, ds5), F32),
                   _sds((gp, ds5), F32), _sds((ds5, gp), F32), _sds((ds5, gp), F32), _sds((2, gp), F32), dwg_shape],
        input_output_aliases=aliases,
        scratch_shapes=[pltpu.VMEM((2, gp), F32), pltpu.VMEM((t, gp), F32), pltpu.VMEM((t, gp), F32),
                        pltpu.VMEM((2 * ng, gp), F32)],
        compiler_params=_cp("arbitrary", "arbitrary"), name=name)(
            ds, yp, zg, xr, xi, xr, xi, proj, wglu, sp["cblk_r"], sp["cblk_in"],
            sp["bblk_r"], sp["bblk_i"], sp["pw"], sp["q8"], sp["d"], *extra)


def _head_norm(x, first):
    x2 = x * x
    sa = jnp.sum(jnp.where(first, x2, 0.0), axis=-1, keepdims=True)
    sb = jnp.sum(jnp.where(first, 0.0, x2), axis=-1, keepdims=True)
    r = jnp.where(first, lax.rsqrt(sa * (1.0 / HEAD_DIM) + EPS), lax.rsqrt(sb * (1.0 / HEAD_DIM) + EPS))
    return x * r, r


def _attn_specs(bl, s, datt, qoff):
    nq = s // ATT_TQ
    nb = datt // LANES
    col = lambda blk: (lambda b, h, q: (b * nq + q, qoff + blk * nb + h))
    win = lambda blk, j: (lambda b, h, q: (b * nq + jnp.maximum(q - 2 + j, 0), qoff + blk * nb + h))
    tile = lambda f: pl.BlockSpec((ATT_TQ, LANES), f)
    qs = tile(col(0))
    ks = [tile(win(1, j)) for j in range(3)]
    vs = [tile(win(2, j)) for j in range(3)]
    return nq, nb, qs, ks, vs


def _attn_probs(q_ref, k_refs, gq_ref, gk_ref, bias_ref):
    qt = pl.program_id(2)
    lane = lax.broadcasted_iota(jnp.int32, (1, LANES), 1)
    first = lane < HEAD_DIM
    qh, rq = _head_norm(q_ref[...].astype(F32), first)
    qn = qh * gq_ref[...]
    kc = jnp.concatenate([r[...] for r in k_refs], axis=0).astype(F32)
    kh, _ = _head_norm(kc, first)
    kn = (kh * gk_ref[...]).astype(BF16)
    kpos = (qt - 2) * ATT_TQ + lax.broadcasted_iota(jnp.int32, (1, 3 * ATT_TQ), 1)
    valid = kpos >= 0
    scale = HEAD_DIM ** -0.5
    masks = (first, jnp.logical_not(first))
    qas, ps = [], []
    for hh in range(2):
        qa = jnp.where(masks[hh], qn, 0.0).astype(BF16)
        sc = _dot_t1(qa, kn) * scale + bias_ref[hh]
        sc = jnp.where(valid, sc, NEG)
        e = jnp.exp(sc - jnp.max(sc, axis=-1, keepdims=True))
        ps.append(e / jnp.sum(e, axis=-1, keepdims=True))
        qas.append(qa)
    return first, masks, qh, rq, kn, qas, ps


def _attn_fwd(proj, gq2, gk2, bias, *, bl, s, datt, qoff, name, comm=None):
    n = bl * s
    nq, nb, qs, ks, vs = _attn_specs(bl, s, datt, qoff)
    nc = 0 if comm is None else len(comm[0])

    def body(q_ref, k0, k1, k2, v0, v1, v2, gq_ref, gk_ref, bias_ref, *rest):
        o_ref = rest[nc]
        if comm is not None:
            finish = _hosted_gather_steps(comm, rest[nc + 1:2 * nc + 1], rest[-2:],
                                          (pl.program_id(0) * nb + pl.program_id(1)) * nq + pl.program_id(2),
                                          bl * nb * nq)
        first, _, _, _, _, _, ps = _attn_probs(q_ref, (k0, k1, k2), gq_ref, gk_ref, bias_ref)
        vc = jnp.concatenate([v0[...], v1[...], v2[...]], axis=0)
        o0 = _dot(ps[0].astype(BF16), vc)
        o1 = _dot(ps[1].astype(BF16), vc)
        o_ref[...] = jnp.where(first, o0, o1).astype(BF16)
        if comm is not None:
            finish()

    gs = pl.BlockSpec((1, LANES), lambda b, h, q: (0, 0))
    c_in, c_ispec, c_ospec, c_oshape, c_scr, aliases = _hosted_gather(comm, 10, 1)
    res = pl.pallas_call(
        body, grid=(bl, nb, nq),
        in_specs=[qs, *ks, *vs, gs, gs, pl.BlockSpec((2, ATT_TQ, 3 * ATT_TQ), lambda b, h, q: (h, 0, 0))] + c_ispec,
        out_specs=[pl.BlockSpec((ATT_TQ, LANES), lambda b, h, q: (b * nq + q, h))] + c_ospec,
        out_shape=[_sds((n, datt), BF16)] + c_oshape,
        input_output_aliases=aliases, scratch_shapes=c_scr,
        compiler_params=_cp("arbitrary", "arbitrary", "arbitrary"), name=name)(
            proj, proj, proj, proj, proj, proj, proj, gq2, gk2, bias, *c_in)
    return res[0], res[1:]


def _attn_bwd(do, proj, gq2, gk2, bias, *, bl, s, datt, qoff, name, comm=None):
    n = bl * s
    nq, nb, qs, ks, vs = _attn_specs(bl, s, datt, qoff)
    srows = s + 2 * ATT_TQ
    scale = HEAD_DIM ** -0.5
    nc = 0 if comm is None else len(comm[0])

    def body(do_ref, q_ref, k0, k1, k2, v0, v1, v2, gq_ref, gk_ref, bias_ref, *rest):
        dq_ref, dk_ref, dv_ref, db_ref, dgq_ref = rest[nc:nc + 5]
        qt = pl.program_id(2)
        if comm is not None:
            start, wait = _exchange_ops(rest[:nc], rest[nc + 5:2 * nc + 5], rest[-2], rest[-1], comm[1], comm[2])
            step = (pl.program_id(0) * nb + pl.program_id(1)) * nq + qt

            @pl.when(step == 0)
            def _():
                start()

        @pl.when(qt == 0)
        def _():
            dk_ref[...] = jnp.zeros_like(dk_ref)
            dv_ref[...] = jnp.zeros_like(dv_ref)
            db_ref[...] = jnp.zeros_like(db_ref)
            dgq_ref[...] = jnp.zeros_like(dgq_ref)

        first, masks, qh, rq, kn, qas, ps = _attn_probs(q_ref, (k0, k1, k2), gq_ref, gk_ref, bias_ref)
        vc = jnp.concatenate([v0[...], v1[...], v2[...]], axis=0)
        dov = do_ref[...]
        dqn = jnp.zeros((ATT_TQ, LANES), F32)
        dkn = jnp.zeros((3 * ATT_TQ, LANES), F32)
        dv = jnp.zeros((3 * ATT_TQ, LANES), F32)
        for hh in range(2):
            doa = jnp.where(masks[hh], dov, jnp.zeros_like(dov))
            p = ps[hh]
            dp = _dot_t1(doa, vc)
            dsm = p * (dp - jnp.sum(dp * p, axis=-1, keepdims=True))
            db_ref[hh] += dsm
            dsc = (dsm * scale).astype(BF16)
            dqn = dqn + _dot(dsc, jnp.where(masks[hh], kn, jnp.zeros_like(kn)))
            dkn = dkn + _dot_t0(dsc, qas[hh])
            dv = dv + _dot_t0(p.astype(BF16), doa)
        start = pl.multiple_of(qt * ATT_TQ, ATT_TQ)
        dk_ref[pl.ds(start, 3 * ATT_TQ), :] += dkn
        dv_ref[pl.ds(start, 3 * ATT_TQ), :] += dv
        dgq_ref[...] += jnp.sum(dqn * qh, axis=0, keepdims=True)
        dqh = dqn * gq_ref[...]
        t = dqh * qh
        ma = jnp.sum(jnp.where(first, t, 0.0), axis=-1, keepdims=True) * (1.0 / HEAD_DIM)
        mb = jnp.sum(jnp.where(first, 0.0, t), axis=-1, keepdims=True) * (1.0 / HEAD_DIM)
        dq_ref[...] = (rq * (dqh - qh * jnp.where(first, ma, mb))).astype(BF16)
        if comm is not None:
            @pl.when(step == bl * nb * nq - 1)
            def _():
                wait()

    gs = pl.BlockSpec((1, LANES), lambda b, h, q: (0, 0))
    acc = pl.BlockSpec((None, srows, LANES), lambda b, h, q: (b, 0, h))
    comm_in = [] if comm is None else list(comm[0])
    comm_out = [] if comm is None else _parts_shapes(comm[0], comm[1])
    comm_scr = [] if comm is None else [pltpu.SemaphoreType.DMA((3 * nc,)), pltpu.SemaphoreType.DMA((3 * nc,))]
    return pl.pallas_call(
        body, grid=(bl, nb, nq),
        in_specs=[pl.BlockSpec((ATT_TQ, LANES), lambda b, h, q: (b * nq + q, h)), qs, *ks, *vs, gs, gs,
                  pl.BlockSpec((2, ATT_TQ, 3 * ATT_TQ), lambda b, h, q: (h, 0, 0))] + [_ANY] * nc,
        out_specs=[pl.BlockSpec((ATT_TQ, LANES), lambda b, h, q: (b * nq + q, h)), acc, acc,
                   pl.BlockSpec((None, 2, ATT_TQ, 3 * ATT_TQ), lambda b, h, q: (b, h, 0, 0)),
                   pl.BlockSpec((None, None, 1, LANES), lambda b, h, q: (b, h, 0, 0))] + [_ANY] * nc,
        out_shape=[_sds((n, datt), BF16), _sds((bl, srows, datt), F32), _sds((bl, srows, datt), F32),
                   _sds((bl, 2 * nb, ATT_TQ, 3 * ATT_TQ), F32), _sds((bl, nb, 1, LANES), F32)] + comm_out,
        scratch_shapes=comm_scr,
        compiler_params=_cp("arbitrary", "arbitrary", "arbitrary"), name=name)(
            do, proj, proj, proj, proj, proj, proj, proj, gq2, gk2, bias, *comm_in)


def _attn_kv_bwd(dkn, dv, proj, gk2, *, bl, s, datt, tm, koff, name):
    n = bl * s
    ns = s // tm
    off = 2 * ATT_TQ // tm
    nb = datt // LANES

    def body(dkn_ref, dv_ref, k_ref, gk_ref, dk_ref, dvo_ref, dgk_ref):
        lane = lax.broadcasted_iota(jnp.int32, (1, LANES), 1)
        first = lane < HEAD_DIM

        @pl.when((pl.program_id(0) == 0) & (pl.program_id(1) == 0) & (pl.program_id(2) == 0))
        def _():
            dgk_ref[...] = jnp.zeros_like(dgk_ref)

        dvo_ref[...] = dv_ref[...].astype(BF16)
        kh, rk = _head_norm(k_ref[...].astype(F32), first)
        dn = dkn_ref[...]
        dgk_ref[...] += jnp.sum(dn * kh, axis=0, keepdims=True)
        dh = dn * gk_ref[...]
        t = dh * kh
        ma = jnp.sum(jnp.where(first, t, 0.0), axis=-1, keepdims=True) * (1.0 / HEAD_DIM)
        mb = jnp.sum(jnp.where(first, 0.0, t), axis=-1, keepdims=True) * (1.0 / HEAD_DIM)
        dk_ref[...] = (rk * (dh - kh * jnp.where(first, ma, mb))).astype(BF16)

    accs = pl.BlockSpec((None, tm, LANES), lambda b, i, c: (b, i + off, c))
    outs = pl.BlockSpec((tm, LANES), lambda b, i, c: (b * ns + i, c))
    vec = pl.BlockSpec((1, LANES), lambda b, i, c: (0, 0))
    return pl.pallas_call(
        body, grid=(bl, ns, nb),
        in_specs=[accs, accs, pl.BlockSpec((tm, LANES), lambda b, i, c: (b * ns + i, koff + c)), vec],
        out_specs=[outs, outs, vec],
        out_shape=[_sds((n, datt), BF16), _sds((n, datt), BF16), _sds((1, LANES), F32)],
        compiler_params=_cp("arbitrary", "arbitrary", "arbitrary"), name=name)(dkn, dv, proj, gk2)


def _conv_fwd(proj, wdw, bdw, lng, lnb, *, bl, s, t, acol, name):
    n = bl * s
    dc = wdw.shape[1]
    nt = s // t
    hb = t // HALO

    def body(za_ref, zg_ref, ha_ref, hgt_ref, w_ref, b_ref, g_ref, be_ref, hg_ref, hc_ref, o_ref, ext):
        i = pl.program_id(1)
        hg = za_ref[...].astype(F32) * _sigmoid(zg_ref[...].astype(F32))
        live = jnp.where(i > 0, 1.0, 0.0)
        ext[0:HALO, :] = ha_ref[...].astype(F32) * _sigmoid(hgt_ref[...].astype(F32)) * live
        ext[HALO:HALO + t, :] = hg
        hg_ref[...] = hg
        acc = jnp.zeros((t, dc), F32) + b_ref[...]
        for j in range(CONV_W):
            acc = acc + w_ref[j:j + 1, :] * ext[pl.ds(HALO - (CONV_W - 1) + j, t), :]
        hc_ref[...] = acc
        mu = jnp.mean(acc, axis=-1, keepdims=True)
        xc = acc - mu
        rs = lax.rsqrt(jnp.mean(xc * xc, axis=-1, keepdims=True) + EPS)
        ln = xc * rs * g_ref[...] + be_ref[...]
        o_ref[...] = (ln * _sigmoid(ln)).astype(BF16)

    vec = pl.BlockSpec((1, dc), lambda b, i: (0, 0))
    row = pl.BlockSpec((t, dc), lambda b, i: (b * nt + i, 0))
    tile = lambda c: pl.BlockSpec((t, dc), lambda b, i: (b * nt + i, c))
    halo = lambda c: pl.BlockSpec((HALO, dc), lambda b, i: (jnp.maximum((b * nt + i) * hb - 1, 0), c))
    return pl.pallas_call(
        body, grid=(bl, nt),
        in_specs=[tile(acol), tile(acol + 1), halo(acol), halo(acol + 1),
                  pl.BlockSpec((HALO, dc), lambda b, i: (0, 0)), vec, vec, vec],
        out_specs=[row, row, row],
        out_shape=[_sds((n, dc), F32), _sds((n, dc), F32), _sds((n, dc), BF16)],
        scratch_shapes=[pltpu.VMEM((HALO + t, dc), F32)],
        compiler_params=_cp("parallel", "arbitrary"), name=name)(proj, proj, proj, proj, wdw, bdw, lng, lnb)


def _conv_bwd_ln(dco, hc, lng, lnb, *, tm, name):
    n, dc = hc.shape

    def body(d_ref, hc_ref, g_ref, be_ref, dhc_ref, dg_ref, db_ref):
        @pl.when(pl.program_id(0) == 0)
        def _():
            dg_ref[...] = jnp.zeros_like(dg_ref)
            db_ref[...] = jnp.zeros_like(db_ref)

        hcv = hc_ref[...]
        mu = jnp.mean(hcv, axis=-1, keepdims=True)
        xc = hcv - mu
        rs = lax.rsqrt(jnp.mean(xc * xc, axis=-1, keepdims=True) + EPS)
        xh = xc * rs
        ln = xh * g_ref[...] + be_ref[...]
        sg = _sigmoid(ln)
        dln = d_ref[...].astype(F32) * (sg * (1.0 + ln * (1.0 - sg)))
        db_ref[...] += jnp.sum(dln, axis=0, keepdims=True)
        dg_ref[...] += jnp.sum(dln * xh, axis=0, keepdims=True)
        dxh = dln * g_ref[...]
        dhc_ref[...] = rs * (dxh - jnp.mean(dxh, axis=-1, keepdims=True)
                             - xh * jnp.mean(dxh * xh, axis=-1, keepdims=True))

    vec = pl.BlockSpec((1, dc), lambda i: (0, 0))
    row = pl.BlockSpec((tm, dc), lambda i: (i, 0))
    return pl.pallas_call(
        body, grid=(n // tm,), in_specs=[row, row, vec, vec], out_specs=[row, vec, vec],
        out_shape=[_sds((n, dc), F32), _sds((1, dc), F32), _sds((1, dc), F32)],
        compiler_params=_cp("arbitrary"), name=name)(dco, hc, lng, lnb)


def _conv_bwd_dw(dhc, hg, proj, wdw, *, bl, s, t, acol, name):
    n = bl * s
    dc = wdw.shape[1]
    nt = s // t
    hb = t // HALO
    lastblk = n // HALO - 1

    def body(d_ref, dn_ref, hg_ref, hp_ref, za_ref, zg_ref, w_ref, dz_ref, dw_ref, dbias_ref, extd, exth):
        b = pl.program_id(0)
        i = pl.program_id(1)

        @pl.when((b == 0) & (i == 0))
        def _():
            dw_ref[...] = jnp.zeros_like(dw_ref)
            dbias_ref[...] = jnp.zeros_like(dbias_ref)

        dv = d_ref[...]
        extd[0:t, :] = dv
        extd[t:t + HALO, :] = dn_ref[...] * jnp.where(i < nt - 1, 1.0, 0.0)
        exth[0:HALO, :] = hp_ref[...] * jnp.where(i > 0, 1.0, 0.0)
        exth[HALO:HALO + t, :] = hg_ref[...]
        dbias_ref[...] += jnp.sum(dv, axis=0, keepdims=True)
        dhg = jnp.zeros((t, dc), F32)
        for j in range(CONV_W):
            dhg = dhg + w_ref[j:j + 1, :] * extd[pl.ds(CONV_W - 1 - j, t), :]
            dw_ref[j:j + 1, :] += jnp.sum(dv * exth[pl.ds(HALO - (CONV_W - 1) + j, t), :], axis=0, keepdims=True)
        za = za_ref[...].astype(F32)
        sg = _sigmoid(zg_ref[...].astype(F32))
        dz_ref[...] = jnp.concatenate([dhg * sg, dhg * za * sg * (1.0 - sg)], axis=1).astype(BF16)

    row = pl.BlockSpec((t, dc), lambda b, i: (b * nt + i, 0))
    nxt = pl.BlockSpec((HALO, dc), lambda b, i: (jnp.minimum((b * nt + i + 1) * hb, lastblk), 0))
    prv = pl.BlockSpec((HALO, dc), lambda b, i: (jnp.maximum((b * nt + i) * hb - 1, 0), 0))
    wsp = pl.BlockSpec((HALO, dc), lambda b, i: (0, 0))
    tile = lambda c: pl.BlockSpec((t, dc), lambda b, i: (b * nt + i, c))
    return pl.pallas_call(
        body, grid=(bl, nt),
        in_specs=[row, nxt, row, prv, tile(acol), tile(acol + 1), wsp],
        out_specs=[pl.BlockSpec((t, 2 * dc), lambda b, i: (b * nt + i, 0)), wsp,
                   pl.BlockSpec((1, dc), lambda b, i: (0, 0))],
        out_shape=[_sds((n, 2 * dc), BF16), _sds((HALO, dc), F32), _sds((1, dc), F32)],
        scratch_shapes=[pltpu.VMEM((t + HALO, dc), F32), pltpu.VMEM((HALO + t, dc), F32)],
        compiler_params=_cp("arbitrary", "arbitrary"), name=name)(dhc, dhc, hg, hg, proj, proj, wdw)


def _mix_out_fwd(x, brs, gl, bg, wbs, wout, l, *, tm, name):
    n, d = x.shape

    def body(x_ref, s_ref, a_ref, c_ref, g0, g1, g2, bg_ref, ws, wa, wc, wo, o_ref):
        merged = jnp.zeros((tm, d), F32)
        for k, (br, gr, w) in enumerate(((s_ref, g0, ws), (a_ref, g1, wa), (c_ref, g2, wc))):
            gate = _sigmoid(gr[...].astype(F32) + bg_ref[:, k * d:(k + 1) * d])
            merged = merged + gate * _dot(br[...], w[...])
        o_ref[...] = x_ref[...] + _dot(merged.astype(BF16), wo[...])

    row = lambda w: pl.BlockSpec((tm, w), lambda i: (i, 0))
    wsp = lambda a: pl.BlockSpec((None,) + a.shape[1:], lambda i: (l, 0, 0))
    gls = [pl.BlockSpec((tm, d), functools.partial(lambda k, i: (i, k), k)) for k in range(3)]
    return pl.pallas_call(
        body, grid=(n // tm,),
        in_specs=[row(d), *[row(b.shape[1]) for b in brs], *gls, pl.BlockSpec(bg.shape, lambda i: (0, 0)),
                  *[wsp(w) for w in wbs], wsp(wout)],
        out_specs=row(d), out_shape=_sds((n, d), F32),
        compiler_params=_cp("parallel"), name=name)(x, *brs, gl, gl, gl, bg, *wbs, wout)


def _mix_out_bwd(dx, brs, gl, bg, wbs, wout, l, nl, bufs, *, tm, name):
    n, d = dx.shape
    widths = [b.shape[1] for b in brs]

    def body(dx_ref, s_ref, a_ref, c_ref, g0, g1, g2, bg_ref, ws, wa, wc, wo, *rest):
        ds_ref, da_ref, dc_ref, dgl_ref, dbg_ref, dws, dwa, dwc, dwo = rest[-9:]

        @pl.when(pl.program_id(0) == 0)
        def _():
            for r in (dbg_ref, dws, dwa, dwc, dwo):
                r[...] = jnp.zeros_like(r)

        dxb = dx_ref[...].astype(BF16)
        dm = _dot_t1(dxb, wo[...])
        merged = jnp.zeros((tm, d), F32)
        for k, (br, gr, w, dbr, dw) in enumerate(((s_ref, g0, ws, ds_ref, dws), (a_ref, g1, wa, da_ref, dwa),
                                                   (c_ref, g2, wc, dc_ref, dwc))):
            gate = _sigmoid(gr[...].astype(F32) + bg_ref[:, k * d:(k + 1) * d])
            brv = br[...]
            wv = w[...]
            y = _dot(brv, wv)
            merged = merged + gate * y
            dyb = (dm * gate).astype(BF16)
            dbr[...] = _dot_t1(dyb, wv).astype(BF16)
            dw[...] += _dot_t0(brv, dyb)
            dgl = dm * y * gate * (1.0 - gate)
            dgl_ref[:, k * d:(k + 1) * d] = dgl.astype(BF16)
            dbg_ref[:, k * d:(k + 1) * d] += jnp.sum(dgl, axis=0, keepdims=True)
        dwo[...] += _dot_t0(merged.astype(BF16), dxb)

    row = lambda w: pl.BlockSpec((tm, w), lambda i: (i, 0))
    wsp = lambda shape: pl.BlockSpec((None,) + tuple(shape), lambda i: (l, 0, 0))
    gls = [pl.BlockSpec((tm, d), functools.partial(lambda k, i: (i, k), k)) for k in range(3)]
    slabs = [(w, d) for w in widths] + [(d, d)]
    n_in = 12
    extra = [] if bufs is None else list(bufs)
    aliases = {} if bufs is None else {n_in + k: 5 + k for k in range(4)}
    return pl.pallas_call(
        body, grid=(n // tm,),
        in_specs=[row(d), *[row(w) for w in widths], *gls, pl.BlockSpec(bg.shape, lambda i: (0, 0)),
                  *[wsp(w.shape[1:]) for w in wbs], wsp(wout.shape[1:]), *[_ANY for _ in extra]],
        out_specs=[*[row(w) for w in widths], row(3 * d), pl.BlockSpec((1, 3 * d), lambda i: (0, 0)),
                   *[wsp(sh) for sh in slabs]],
        out_shape=[*[_sds((n, w), BF16) for w in widths], _sds((n, 3 * d), BF16), _sds((1, 3 * d), F32),
                   *[_sds((nl,) + sh, F32) for sh in slabs]],
        input_output_aliases=aliases,
        compiler_params=_cp("arbitrary"), name=name)(dx, *brs, gl, gl, gl, bg, *wbs, wout, *extra)


def _adamw(w, g, m, v, *, name):
    r, c = w.shape
    tm = _tile(r, 256)
    c1 = 1.0 - ADAM_B1 ** ADAM_STEP
    c2 = 1.0 - ADAM_B2 ** ADAM_STEP

    def body(w_ref, g_ref, m_ref, v_ref, d_ref, nm_ref, nv_ref):
        gv = g_ref[...]
        mn = ADAM_B1 * m_ref[...] + (1.0 - ADAM_B1) * gv
        vn = ADAM_B2 * v_ref[...] + (1.0 - ADAM_B2) * (gv * gv)
        nm_ref[...] = mn
        nv_ref[...] = vn
        d_ref[...] = -ADAM_LR * ((mn / c1) / (jnp.sqrt(vn / c2) + ADAM_EPS) + ADAM_WD * w_ref[...])

    blk = pl.BlockSpec((tm, c), lambda i: (i, 0))
    return pl.pallas_call(
        body, grid=(r // tm,), in_specs=[blk] * 4, out_specs=[blk] * 3,
        out_shape=[_sds((r, c), F32)] * 3, compiler_params=_cp("parallel"), name=name)(w, g, m, v)


def _add_sibling(g, recv, lyr, *, name):
    _, a, b = g.shape
    ta = _tile(a, 256)

    def body(g_ref, r_ref, o_ref):
        o_ref[...] = (g_ref[...] + r_ref[...]).astype(BF16)

    return pl.pallas_call(
        body, grid=(a // ta,),
        in_specs=[pl.BlockSpec((None, ta, b), lambda i: (lyr, i, 0)), pl.BlockSpec((ta, b), lambda i: (i, 0))],
        out_specs=pl.BlockSpec((ta, b), lambda i: (i, 0)),
        out_shape=_sds((a, b), BF16), compiler_params=_cp("parallel"), name=name)(g, recv)


def _add_chips(rsum, parts, axis, s_idx, lyr, buf, *, name):
    _, a, b = parts.shape
    ta = _tile(a, 256)
    na = a // ta

    def body(s_ref, own_ref, p0, p1, p2, p3, *rest):
        o_ref = rest[-1]
        own = own_ref[...].astype(F32)
        terms = [jnp.where(s_ref[0] == s, own, p[...].astype(F32)) for s, p in enumerate((p0, p1, p2, p3))]
        o_ref[...] = ((terms[0] + terms[1]) + terms[2]) + terms[3]

    own_spec = (pl.BlockSpec((ta, b), lambda i, sr: (sr[0] * na + i, 0)) if axis == 1
                else pl.BlockSpec((ta, b), lambda i, sr: (i, sr[0])))
    part_spec = lambda s: pl.BlockSpec((None, ta, b), lambda i, sr: (jnp.where(sr[0] == s, s ^ 1, s), i, 0))
    extra, extra_specs, out_shape, aliases = _slab_out(2, lyr, (a, b), buf, 6)
    return pl.pallas_call(
        body,
        grid_spec=pltpu.PrefetchScalarGridSpec(
            num_scalar_prefetch=1, grid=(na,),
            in_specs=[own_spec] + [part_spec(s) for s in range(N_CHIPS)] + extra_specs,
            out_specs=pl.BlockSpec((None, ta, b), lambda i, sr: (lyr, i, 0))),
        out_shape=out_shape, input_output_aliases=aliases, compiler_params=_cp("parallel"), name=name)(
            s_idx, rsum, parts, parts, parts, parts, *extra)


def _place_shard(wloc, axis, s_idx, *, name):
    nl, a, b = wloc.shape
    ta = _tile(a, 256)
    na = a // ta
    full = (nl, a * N_CHIPS, b) if axis == 1 else (nl, a, b * N_CHIPS)

    def body(sc_ref, w_ref, o_ref):
        o_ref[...] = w_ref[...].astype(BF16)

    out_spec = (pl.BlockSpec((None, ta, b), lambda l, i, sc: (l, sc[0] * na + i, 0)) if axis == 1
                else pl.BlockSpec((None, ta, b), lambda l, i, sc: (l, i, sc[0])))
    return pl.pallas_call(
        body,
        grid_spec=pltpu.PrefetchScalarGridSpec(
            num_scalar_prefetch=1, grid=(nl, na),
            in_specs=[pl.BlockSpec((None, ta, b), lambda l, i, sc: (l, i, 0))], out_specs=out_spec),
        out_shape=_sds(full, BF16), compiler_params=_cp("parallel", "parallel"), name=name)(s_idx, wloc)


def _blockdiag(w):
    g, r, c = w.shape
    eye = jnp.eye(g, dtype=w.dtype)
    return (w[:, :, None, :] * eye[:, None, :, None]).reshape(g * r, g * c)


def _s5_prep(lre, lim, log_dt, b_re, b_im, c_re, c_im, d_skip):
    lr = jnp.minimum(lre, -1e-4)
    li = lim
    dt = jnp.exp(log_dt)[:, None]
    mag = jnp.exp(lr * dt)
    ar = mag * jnp.cos(li * dt)
    ai = mag * jnp.sin(li * dt)
    den = lr * lr + li * li
    coef_r = ((ar - 1.0) * lr + ai * li) / den
    coef_i = (ai * lr - (ar - 1.0) * li) / den
    bbar_r = coef_r[..., None] * b_re - coef_i[..., None] * b_im
    bbar_i = coef_r[..., None] * b_im + coef_i[..., None] * b_re
    a = jnp.stack([ar.reshape(-1), ai.reshape(-1)])
    return dict(
        a=a,
        bblk_r=_blockdiag(bbar_r.transpose(0, 2, 1)), bblk_i=_blockdiag(bbar_i.transpose(0, 2, 1)),
        cblk_r=_blockdiag(c_re.transpose(0, 2, 1)), cblk_in=_blockdiag(-c_im.transpose(0, 2, 1)),
        d=d_skip.reshape(1, -1))


def _s5_powers(a, nlog):
    ar, ai = a[0], a[1]
    pr, pi = ar, ai
    rows = []
    for _ in range(nlog):
        rows += [pr, pi]
        pr, pi = pr * pr - pi * pi, 2.0 * pr * pi
    qr, qi = [ar], [ai]
    for _ in range(SUBLANES - 1):
        qr, qi = qr + [qr[-1] * ar - qi[-1] * ai], qi + [qr[-1] * ai + qi[-1] * ar]
    p8 = jnp.stack(qr + qi)
    q8 = jnp.stack(qr[::-1] + [-v for v in qi[::-1]])
    return jnp.stack(rows), p8, q8


def _bias_table(rel_bias):
    h = rel_bias.shape[0]
    tq, tw = ATT_TQ, 3 * ATT_TQ
    n_hi = tw - 1 - MAX_REL + 1
    n_lo = tq + tw - 1 - n_hi - (2 * MAX_REL - 1)
    fr = jnp.concatenate([
        jnp.broadcast_to(rel_bias[:, 2 * MAX_REL:], (h, n_hi)),
        jnp.flip(rel_bias[:, 1:2 * MAX_REL], axis=1),
        jnp.broadcast_to(rel_bias[:, :1], (h, n_lo)),
        jnp.zeros((h, 1), rel_bias.dtype)], axis=1)
    ln = tq + tw
    flat = jnp.broadcast_to(fr[:, None, :], (h, tq, ln)).reshape(h, tq * ln)[:, :tq * (ln - 1)]
    tab = flat.reshape(h, tq, ln - 1)[:, :, tq - 1:tq - 1 + tw]
    qc = np.arange(tq)[:, None] // CHUNK + N_LEFT
    kc = np.arange(tw)[None, :] // CHUNK
    band = (kc <= qc) & (kc >= qc - N_LEFT)
    return jnp.where(jnp.asarray(band)[None], tab, NEG)


def _small_prep(w, l):
    g, p = w["s5_lambda_re"].shape[1:]
    b_shape, c_shape = (g, p, -1), (g, -1, p)
    sp = _s5_prep(w["s5_lambda_re"][l], w["s5_lambda_im"][l], w["s5_log_dt"][l], w["s5_b_re"][l].reshape(b_shape),
                  w["s5_b_im"][l].reshape(b_shape), w["s5_c_re"][l].reshape(c_shape), w["s5_c_im"][l].reshape(c_shape),
                  w["s5_d"][l])
    return sp, _bias_table(w["attn_rel_bias"][l])


_PREP_KEYS = ("s5_lambda_re", "s5_lambda_im", "s5_log_dt", "s5_b_re", "s5_b_im", "s5_c_re", "s5_c_im", "s5_d",
              "attn_rel_bias")
_BIG_KEYS = {"ffn1_w_up": 2, "ffn1_w_down": 1, "w_in": 2, "s5_w_glu": 2, "w_br_s5": 2, "w_br_attn": 2,
             "w_br_conv": 2, "w_out": 1, "ffn2_w_up": 2, "ffn2_w_down": 1}
_SMALL_KEYS = ("ffn1_norm", "mix_norm", "b_gate", "s5_lambda_re", "s5_lambda_im", "s5_log_dt", "s5_b_re", "s5_b_im",
               "s5_c_re", "s5_c_im", "s5_d", "attn_q_gain", "attn_k_gain", "attn_rel_bias", "conv_w_dw", "conv_b_dw",
               "conv_ln_g", "conv_ln_b", "ffn2_norm")
_WEIGHTS = ("ffn1_norm", "ffn1_w_up", "ffn1_w_down", "mix_norm", "w_in", "b_gate", "s5_lambda_re", "s5_lambda_im",
            "s5_log_dt", "s5_b_re", "s5_b_im", "s5_c_re", "s5_c_im", "s5_d", "s5_w_glu", "w_br_s5", "attn_q_gain",
            "attn_k_gain", "attn_rel_bias", "w_br_attn", "conv_w_dw", "conv_b_dw", "conv_ln_g", "conv_ln_b",
            "w_br_conv", "w_out", "ffn2_norm", "ffn2_w_up", "ffn2_w_down")


def _local_step(x3, target3, w, rs=None, gather=None):
    w = dict(w)
    bl, s, d = x3.shape
    nl = w["ffn1_norm"].shape[0]
    dff = w["ffn1_w_down"].shape[1]
    ds5 = w["s5_d"].shape[1]
    datt = w["w_br_attn"].shape[1]
    dc = w["conv_b_dw"].shape[1]
    n = bl * s
    x = x3.reshape(n, d)
    target = target3.reshape(n, d)
    tm = _tile(n, 512)
    tml = _tile(n, 1024)
    tmix = _tile(n, 256)
    ts5 = 256
    tconv = _tile(s, 512)
    tff = dff // 2
    ma = ds5 + 3 * datt + 2 * dc
    tna = ma // 3
    assert (3 * d) % tna == 0 and dff % 2 == 0
    qoff = ds5 // LANES
    koff = (ds5 + datt) // LANES
    acol = (ds5 + 3 * datt) // dc
    wbs = lambda: (w["w_br_s5"], w["w_br_attn"], w["w_br_conv"])

    def host(tag, l):
        if gather is None or l != 0:
            return None
        keys, kaxes = gather[tag]
        return [w[k] for k in keys], kaxes, 1

    def hosted(tag, l, arrays):
        if gather is not None and l == 0:
            w.update(zip(gather[tag][0], arrays))

    saved = []
    for l in range(nl):
        (sp, bias), prep_vjp = jax.vjp(lambda ww: _small_prep(ww, l), {k: w[k] for k in _PREP_KEYS})
        spb = dict(sp)
        spb["pw"], spb["p8"], spb["q8"] = _s5_powers(lax.stop_gradient(sp["a"]), int(math.log2(ts5)))
        for k in ("bblk_r", "bblk_i", "cblk_r", "cblk_in"):
            spb[k] = sp[k].astype(BF16)
        g1 = w["ffn1_norm"][l][None]
        g2 = w["ffn2_norm"][l][None]
        gm = w["mix_norm"][l][None]
        gq2 = jnp.tile(w["attn_q_gain"][l], 2)[None]
        gk2 = jnp.tile(w["attn_k_gain"][l], 2)[None]
        wdw = jnp.pad(w["conv_w_dw"][l], ((0, HALO - CONV_W), (0, 0)))
        bdw, lng, lnb = w["conv_b_dw"][l][None], w["conv_ln_g"][l][None], w["conv_ln_b"][l][None]
        bg = w["b_gate"][l][None]

        x0 = x
        h1, ab1 = _norm_mm(x0, g1, w["ffn1_w_up"], l, tm=tml, tn=tff, ntiles=4, pieces=2, transposed=False,
                           name=f"ffn1_up_{l}")
        x1 = _ffn_down(ab1, w["ffn1_w_down"], l, x0, tm=tm, tk=tff, name=f"ffn1_down_{l}")
        h2, pa = _norm_mm(x1, gm, w["w_in"], l, tm=tml, tn=tna, ntiles=3, pieces=1, transposed=True, name=f"win_a_{l}")
        pa = pa[0]
        gl = _mm_t(h2, w["w_in"], l, tm=tml, tn=tna, off=3, ntiles=3 * d // tna, name=f"win_g_{l}")
        xr, xi, yp, zg, s5o = _s5_fwd(pa, spb, w["s5_w_glu"], l, bl=bl, s=s, t=ts5, name=f"s5_fwd_{l}")
        atto, got = _attn_fwd(pa, gq2, gk2, bias, bl=bl, s=s, datt=datt, qoff=qoff, name=f"attn_fwd_{l}",
                              comm=host("attn_fwd", l))
        hosted("attn_fwd", l, got)
        hg, hc, convo = _conv_fwd(pa, wdw, bdw, lng, lnb, bl=bl, s=s, t=tconv, acol=acol, name=f"conv_fwd_{l}")
        brs = (s5o, atto, convo)
        x2 = _mix_out_fwd(x1, brs, gl, bg, wbs(), w["w_out"], l, tm=tmix, name=f"mix_fwd_{l}")
        h3, ab2, *got = _norm_mm(x2, g2, w["ffn2_w_up"], l, tm=tml, tn=tff, ntiles=4, pieces=2, transposed=False,
                                 name=f"ffn2_up_{l}", comm=host("ffn2_up", l))
        hosted("ffn2_up", l, got)
        x = _ffn_down(ab2, w["ffn2_w_down"], l, x2, tm=tm, tk=tff, name=f"ffn2_down_{l}")
        saved.append(dict(spb=spb, bias=bias, prep_vjp=prep_vjp, g1=g1, g2=g2, gm=gm, gq2=gq2, gk2=gk2,
                          wdw=wdw, lng=lng, lnb=lnb, bg=bg, x0=x0, h1=h1, ab1=ab1, x1=x1, h2=h2, pa=pa, gl=gl,
                          xr=xr, xi=xi, yp=yp, zg=zg, hg=hg, hc=hc, brs=brs, x2=x2, h3=h3, ab2=ab2))

    dx, lsum = _loss_grad(x, target, tm=tm, name="loss")
    loss_part = 0.5 * jnp.sum(lsum) / d

    big = {k: None for k in _BIG_KEYS}
    small = {k: [None] * nl for k in _SMALL_KEYS}
    hooks = {"pending": None}
    assert rs is None or nl == 2
    for l in reversed(range(nl)):
        sv = saved[l]

        def ffn_bwd(dx, xin, h, ab, g, tag):
            wu, wd = w[tag + "_w_up"], w[tag + "_w_down"]
            dab, big[tag + "_w_down"] = _ffn_dact(dx, wd, l, ab, nl, big[tag + "_w_down"], tm=tm, tk=tff,
                                                  name=f"{tag}_dact_{l}")
            big[tag + "_w_up"] = _mm_tn(h[None], dab, l, nl, big[tag + "_w_up"], ta=d, tb=tff, tk=tml,
                                        name=f"{tag}_dwu_{l}")
            comm = None
            if rs is not None and l == 0:
                comm = ([big[k] for k in _BIG_KEYS], 1 if tag == "ffn2" else 0)
            dxo, dg, *recv = _ffn_dx(dab, wu, l, xin, g, dx, tm=tml, tk=tff, name=f"{tag}_dx_{l}", comm=comm)
            small[tag + "_norm"][l] = dg[0]
            if comm is not None and tag == "ffn2":
                hooks["pending"] = rs.sums(big, recv, 1)
            elif comm is not None:
                rs.recv0 = recv
            return dxo

        dx = ffn_bwd(dx, sv["x2"], sv["h3"], sv["ab2"], sv["g2"], "ffn2")

        mix_keys = ("w_br_s5", "w_br_attn", "w_br_conv", "w_out")
        bufs = None if big["w_out"] is None else [big[k] for k in mix_keys]
        ds5o, datto, dconvo, dgl, dbg, *dws = _mix_out_bwd(
            dx, sv["brs"], sv["gl"], sv["bg"], wbs(), w["w_out"], l, nl, bufs, tm=tmix, name=f"mix_bwd_{l}")
        small["b_gate"][l] = dbg[0]
        big.update(zip(mix_keys, dws))

        dhc, dlng, dlnb = _conv_bwd_ln(dconvo, sv["hc"], sv["lng"], sv["lnb"], tm=tm, name=f"conv_bwd_ln_{l}")
        dz, dwdw, dbdw = _conv_bwd_dw(dhc, sv["hg"], sv["pa"], sv["wdw"], bl=bl, s=s, t=tconv, acol=acol,
                                      name=f"conv_bwd_dw_{l}")
        small["conv_w_dw"][l] = dwdw[:CONV_W]
        small["conv_b_dw"][l], small["conv_ln_g"][l], small["conv_ln_b"][l] = dbdw[0], dlng[0], dlnb[0]

        comm = hooks["pending"] if l == 0 else None
        dq, dkn, dvw, dbias, dgq, *hosted = _attn_bwd(datto, sv["pa"], sv["gq2"], sv["gk2"], sv["bias"], bl=bl, s=s,
                                                      datt=datt, qoff=qoff, name=f"attn_bwd_{l}", comm=comm)
        if comm is not None:
            rs.parts = hosted
        dk, dv, dgk = _attn_kv_bwd(dkn, dvw, sv["pa"], sv["gk2"], bl=bl, s=s, datt=datt, tm=_tile(s, 512), koff=koff,
                                   name=f"attn_kv_bwd_{l}")
        small["attn_q_gain"][l] = jnp.sum(dgq.reshape(-1, HEAD_DIM), axis=0)
        small["attn_k_gain"][l] = jnp.sum(dgk.reshape(-1, HEAD_DIM), axis=0)

        du, dd, dcr, dci, dbr, dbi, da, big["s5_w_glu"] = _s5_bwd(
            ds5o, sv["yp"], sv["zg"], sv["xr"], sv["xi"], sv["pa"], sv["spb"], w["s5_w_glu"], l, nl, big["s5_w_glu"],
            bl=bl, s=s, t=ts5, name=f"s5_bwd_{l}")
        prep_ct = (dict(a=da, bblk_r=dbr, bblk_i=dbi, cblk_r=dcr, cblk_in=dci, d=dd), jnp.sum(dbias, axis=0))
        (dprep,) = sv["prep_vjp"](prep_ct)
        for k in _PREP_KEYS:
            small[k][l] = dprep[k][l]

        dpa = jnp.concatenate([du, dq, dk, dv, dz], axis=1)
        big["w_in"] = _dwin_t(dpa, dgl, sv["h2"], l, nl, big["w_in"], ta=tna, tk=tml, name=f"dwin_{l}")
        dx, dgm = _mix_dx(dpa, dgl, w["w_in"], l, sv["x1"], sv["gm"], dx, tm=tml, tk=tna, name=f"mix_dx_{l}")
        small["mix_norm"][l] = dgm[0]

        dx = ffn_bwd(dx, sv["x0"], sv["h1"], sv["ab1"], sv["g1"], "ffn1")

    small = {k: jnp.stack(v) for k, v in small.items()}
    return loss_part, dx.reshape(bl, s, d), big, small


def _place():
    x, y, c = lax.axis_index("x"), lax.axis_index("y"), lax.axis_index("c")
    chips = [(1 - x, y), (x, 1 - y), (1 - x, 1 - y)]
    return x, y, c, chips


def _remote(src, dst, send_sems, recv_sems, k, dev):
    return pltpu.make_async_remote_copy(src_ref=src, dst_ref=dst, send_sem=send_sems.at[k], recv_sem=recv_sems.at[k],
                                        device_id=dev, device_id_type=MESH)


def _window(ref, lead, s, axis, blk):
    if axis == 1:
        sl = (pl.ds(pl.multiple_of(s * blk, 16), blk), slice(None))
    else:
        sl = (slice(None), pl.ds(pl.multiple_of(s * blk, LANES), blk))
    return ref.at[sl] if lead is None else ref.at[(lead,) + sl]


def _gather_ops(bufs, axes, send_sems, recv_sems, lyr):
    x, y, c, chips = _place()
    s_me = 2 * x + y
    sibling = (x, y, 1 - lyr)
    nw = len(bufs)
    blks = [f.shape[ax] // N_CHIPS for f, ax in zip(bufs, axes)]
    win = lambda i, s: _window(bufs[i], lyr, s, axes[i], blks[i])
    pairs = [(i, j, cx, cy) for i in range(nw) for j, (cx, cy) in enumerate(chips)]
    sends = [_remote(win(i, s_me), win(i, s_me), send_sems, recv_sems, 6 * i + j, (cx, cy, lyr)) for i, j, cx, cy in pairs]
    passed = [_remote(win(i, 2 * cx + cy), win(i, 2 * cx + cy), send_sems, recv_sems, 6 * i + 3 + j, sibling)
              for i, j, cx, cy in pairs]

    def start():
        @pl.when(c == lyr)
        def _():
            for cp in sends:
                cp.start()

    def forward():
        @pl.when(c == lyr)
        def _():
            for (i, j, cx, cy), fw in zip(pairs, passed):
                piece = win(i, 2 * cx + cy)
                _remote(piece, piece, send_sems, recv_sems, 6 * i + j, (cx, cy, lyr)).wait_recv()
                fw.start()

    def wait():
        @pl.when(c == lyr)
        def _():
            for cp in sends + passed:
                cp.wait_send()

        @pl.when(c != lyr)
        def _():
            for fw in passed:
                fw.wait_recv()

    return start, forward, wait


def _all_gather_weights(fulls, axes, taps):
    nw = len(fulls)

    def body(*refs):
        taps_in = refs[nw]
        outs, taps_out = refs[nw + 1:2 * nw + 1], refs[2 * nw + 1]
        send_sems, recv_sems, tap_send, tap_recv, local_sem = refs[-5:]
        x, y, c, chips = _place()
        s_me = 2 * x + y
        start, forward, wait = _gather_ops(outs, axes, send_sems, recv_sems, 0)
        own_taps = pltpu.make_async_copy(taps_in, taps_out.at[s_me], local_sem)
        own_taps.start()
        tap_sends = [_remote(taps_in, taps_out.at[s_me], tap_send, tap_recv, j, (cx, cy, c))
                     for j, (cx, cy) in enumerate(chips)]
        for cp in tap_sends:
            cp.start()
        start()
        forward()
        wait()
        for j, (cx, cy) in enumerate(chips):
            slab = taps_out.at[2 * cx + cy]
            _remote(slab, slab, tap_send, tap_recv, j, (cx, cy, c)).wait_recv()
        for cp in tap_sends:
            cp.wait_send()
        own_taps.wait()

    return pl.pallas_call(
        body, in_specs=[_ANY] * (nw + 1), out_specs=[_ANY] * (nw + 1),
        out_shape=[_sds(f.shape, f.dtype) for f in fulls] + [_sds((N_CHIPS,) + taps.shape, taps.dtype)],
        input_output_aliases={i: i for i in range(nw)},
        scratch_shapes=[pltpu.SemaphoreType.DMA((6 * nw,)), pltpu.SemaphoreType.DMA((6 * nw,)),
                        pltpu.SemaphoreType.DMA((3,)), pltpu.SemaphoreType.DMA((3,)), pltpu.SemaphoreType.DMA],
        name="all_gather_weights")(*fulls, taps)


def _hosted_gather(comm, n_in, n_out):
    if comm is None:
        return [], [], [], [], [], {}
    bufs = list(comm[0])
    nw = len(bufs)
    return (bufs, [_ANY] * nw, [_ANY] * nw, [_sds(f.shape, f.dtype) for f in bufs],
            [pltpu.SemaphoreType.DMA((6 * nw,)), pltpu.SemaphoreType.DMA((6 * nw,))],
            {n_in + k: n_out + k for k in range(nw)})


def _hosted_gather_steps(comm, out_refs, sems, step, total):
    start, forward, wait = _gather_ops(out_refs, comm[1], sems[0], sems[1], comm[2])

    @pl.when(step == 0)
    def _():
        start()

    def finish():
        @pl.when(step == (3 * total) // 4)
        def _():
            forward()

        @pl.when(step == total - 1)
        def _():
            wait()

    return finish


def _swap_ops(ins, outs, send_sems, recv_sems, lyr):
    x, y, c, _ = _place()
    cps = [_remote(ins[i].at[lyr], outs[i], send_sems, recv_sems, i, (x, y, lyr)) for i in range(len(ins))]

    def start():
        @pl.when(c != lyr)
        def _():
            for cp in cps:
                cp.start()

    def wait():
        @pl.when(c != lyr)
        def _():
            for cp in cps:
                cp.wait_send()

        @pl.when(c == lyr)
        def _():
            for cp in cps:
                cp.wait_recv()

    return start, wait


def _exchange_ops(ins, outs, send_sems, recv_sems, axes, lyr):
    x, y, c, chips = _place()
    s_me = 2 * x + y
    nw = len(ins)
    blks = [r.shape[ax - 1] // N_CHIPS for r, ax in zip(ins, axes)]
    win = lambda i, s: _window(ins[i], None, s, axes[i], blks[i])
    sends = [_remote(win(i, 2 * cx + cy), outs[i].at[s_me], send_sems, recv_sems, 3 * i + j, (cx, cy, lyr))
             for i in range(nw) for j, (cx, cy) in enumerate(chips)]

    def start():
        @pl.when(c == lyr)
        def _():
            for cp in sends:
                cp.start()

    def wait():
        @pl.when(c == lyr)
        def _():
            for i in range(nw):
                for j, (cx, cy) in enumerate(chips):
                    slab = outs[i].at[2 * cx + cy]
                    _remote(slab, slab, send_sems, recv_sems, 3 * i + j, (cx, cy, lyr)).wait_recv()
            for cp in sends:
                cp.wait_send()

    return start, wait


def _parts_shapes(rsums, axes):
    shard = [tuple(dim // N_CHIPS if i == ax - 1 else dim for i, dim in enumerate(r.shape)) for r, ax in zip(rsums, axes)]
    return [_sds((N_CHIPS,) + sh, r.dtype) for sh, r in zip(shard, rsums)]


def _rs_exchange(rsums, axes, lyr):
    nw = len(rsums)

    def body(*refs):
        start, wait = _exchange_ops(refs[:nw], refs[nw:2 * nw], refs[-2], refs[-1], axes, lyr)
        start()
        wait()

    return pl.pallas_call(
        body, in_specs=[_ANY] * nw, out_specs=[_ANY] * nw, out_shape=_parts_shapes(rsums, axes),
        scratch_shapes=[pltpu.SemaphoreType.DMA((3 * nw,)), pltpu.SemaphoreType.DMA((3 * nw,))],
        name=f"rs_exchange_l{lyr}")(*rsums)


def _rs_join(ts):
    nw = len(ts)

    def body(*refs):
        outs = refs[nw:2 * nw]
        send_sems, recv_sems = refs[-2:]
        x, y, c, _ = _place()
        sends = [_remote(outs[i].at[c], outs[i].at[c], send_sems, recv_sems, i, (x, y, 1 - c)) for i in range(nw)]
        for cp in sends:
            cp.start()
        for i in range(nw):
            slab = outs[i].at[1 - c]
            _remote(slab, slab, send_sems, recv_sems, i, (x, y, 1 - c)).wait_recv()
        for cp in sends:
            cp.wait_send()

    return pl.pallas_call(
        body, in_specs=[_ANY] * nw, out_specs=[_ANY] * nw, out_shape=[_sds(t.shape, t.dtype) for t in ts],
        input_output_aliases={i: i for i in range(nw)},
        scratch_shapes=[pltpu.SemaphoreType.DMA((nw,)), pltpu.SemaphoreType.DMA((nw,))],
        name="rs_join_layers")(*ts)


def _all_reduce_small(arrs):
    na = len(arrs)
    nd = 8

    def body(*refs):
        ins, outs, recvs = refs[:na], refs[na:2 * na], refs[2 * na:3 * na]
        send_sems, recv_sems = refs[-2:]
        x, y, c, _ = _place()
        me = 4 * x + 2 * y + c
        for i in range(na):
            recvs[i][0] = ins[i][...]
        cps = []
        for rel in range(1, nd):
            dev = (1 - x if rel & 4 else x, 1 - y if rel & 2 else y, 1 - c if rel & 1 else c)
            for i in range(na):
                cp = _remote(ins[i], recvs[i].at[rel], send_sems, recv_sems, (rel - 1) * na + i, dev)
                cp.start()
                cps.append(cp)
        for rel in range(1, nd):
            for i in range(na):
                _remote(ins[i], recvs[i].at[rel], send_sems, recv_sems, (rel - 1) * na + i, (x, y, c)).wait_recv()
        for i in range(na):
            acc = recvs[i][me]
            for dv in range(1, nd):
                acc = acc + recvs[i][lax.bitwise_xor(me, dv)]
            outs[i][...] = acc
        for cp in cps:
            cp.wait_send()

    vm = pl.BlockSpec(memory_space=pltpu.VMEM)
    nsem = (nd - 1) * na
    return pl.pallas_call(
        body, in_specs=[vm] * na, out_specs=[vm] * na, out_shape=[_sds(t.shape, F32) for t in arrs],
        scratch_shapes=[pltpu.VMEM((nd,) + t.shape, F32) for t in arrs]
        + [pltpu.SemaphoreType.DMA((nsem,)), pltpu.SemaphoreType.DMA((nsem,))],
        compiler_params=pltpu.CompilerParams(vmem_limit_bytes=VMEM_LIMIT), name="all_reduce_small")(*arrs)


def _adamw_small(ws, gs, ms, vs):
    na = len(ws)
    c1 = 1.0 - ADAM_B1 ** ADAM_STEP
    c2 = 1.0 - ADAM_B2 ** ADAM_STEP

    def body(*refs):
        w_r, g_r, m_r, v_r = (refs[k * na:(k + 1) * na] for k in range(4))
        d_r, nm_r, nv_r = (refs[(4 + k) * na:(5 + k) * na] for k in range(3))
        for i in range(na):
            gv = g_r[i][...]
            mn = ADAM_B1 * m_r[i][...] + (1.0 - ADAM_B1) * gv
            vn = ADAM_B2 * v_r[i][...] + (1.0 - ADAM_B2) * (gv * gv)
            nm_r[i][...] = mn
            nv_r[i][...] = vn
            d_r[i][...] = -ADAM_LR * ((mn / c1) / (jnp.sqrt(vn / c2) + ADAM_EPS) + ADAM_WD * w_r[i][...])

    vm = pl.BlockSpec(memory_space=pltpu.VMEM)
    res = pl.pallas_call(
        body, in_specs=[vm] * (4 * na), out_specs=[vm] * (3 * na), out_shape=[_sds(t.shape, F32) for t in ws] * 3,
        compiler_params=pltpu.CompilerParams(vmem_limit_bytes=VMEM_LIMIT), name="adamw_small")(*ws, *gs, *ms, *vs)
    return res[:na], res[na:2 * na], res[2 * na:]


def kernel(x, ffn1_norm, ffn1_w_up, ffn1_w_down, mix_norm, w_in, b_gate, s5_lambda_re, s5_lambda_im, s5_log_dt, s5_b_re, s5_b_im, s5_c_re, s5_c_im, s5_d, s5_w_glu, w_br_s5, attn_q_gain, attn_k_gain, attn_rel_bias, w_br_attn, conv_w_dw, conv_b_dw, conv_ln_g, conv_ln_b, w_br_conv, w_out, ffn2_norm, ffn2_w_up, ffn2_w_down, loss_target, m_ffn1_norm, m_ffn1_w_up, m_ffn1_w_down, m_mix_norm, m_w_in, m_b_gate, m_s5_lambda_re, m_s5_lambda_im, m_s5_log_dt, m_s5_b_re, m_s5_b_im, m_s5_c_re, m_s5_c_im, m_s5_d, m_s5_w_glu, m_w_br_s5, m_attn_q_gain, m_attn_k_gain, m_attn_rel_bias, m_w_br_attn, m_conv_w_dw, m_conv_b_dw, m_conv_ln_g, m_conv_ln_b, m_w_br_conv, m_w_out, m_ffn2_norm, m_ffn2_w_up, m_ffn2_w_down, v_ffn1_norm, v_ffn1_w_up, v_ffn1_w_down, v_mix_norm, v_w_in, v_b_gate, v_s5_lambda_re, v_s5_lambda_im, v_s5_log_dt, v_s5_b_re, v_s5_b_im, v_s5_c_re, v_s5_c_im, v_s5_d, v_s5_w_glu, v_w_br_s5, v_attn_q_gain, v_attn_k_gain, v_attn_rel_bias, v_w_br_attn, v_conv_w_dw, v_conv_b_dw, v_conv_ln_g, v_conv_ln_b, v_w_br_conv, v_w_out, v_ffn2_norm, v_ffn2_w_up, v_ffn2_w_down):
    a = dict(locals())
    xi, yi, ci = lax.axis_index("x"), lax.axis_index("y"), lax.axis_index("c")
    s_me = 2 * xi + yi
    big_keys = list(_BIG_KEYS)
    axes = [1 if k == "w_in" else _BIG_KEYS[k] for k in big_keys]

    s_idx = s_me.astype(jnp.int32).reshape(1)
    placed = [_place_shard(jnp.swapaxes(a[k], 1, 2).astype(BF16) if k == "w_in" else a[k], ax, s_idx,
                           name=f"place_{k}") for k, ax in zip(big_keys, axes)]
    *fulls, taps = _all_gather_weights(placed, axes, a["conv_w_dw"])
    flat = lambda t: t.reshape(t.shape[0], t.shape[1], -1) if t.ndim == 4 else t
    w = {k: flat(a[k]) for k in _WEIGHTS}
    w.update(zip(big_keys, fulls))
    w["conv_w_dw"] = jnp.moveaxis(taps, 0, 2).reshape(taps.shape[1], taps.shape[2], -1)

    class _ReduceScatter:
        parts = recv0 = rsums1 = None

        def sums(self, big, recv, lyr):
            rsums = [_add_sibling(big[k], r, lyr, name=f"rs_add_sibling_{k}_l{lyr}") for r, k in zip(recv, big_keys)]
            if lyr == 1:
                self.rsums1 = rsums
            return rsums, axes, lyr

    rs = _ReduceScatter()
    early = ("ffn1_w_up", "ffn1_w_down", "w_in")
    sets = [[k for k in big_keys if k in early], [k for k in big_keys if k not in early]]
    gather = {tag: (keys, [axes[big_keys.index(k)] for k in keys]) for tag, keys in zip(("attn_fwd", "ffn2_up"), sets)}
    loss_part, grad_x, gbig, gsmall = _local_step(a["x"], a["loss_target"], w, rs, gather)
    loss = lax.psum(loss_part, ("x", "y", "c"))

    rsums0 = rs.sums(gbig, rs.recv0, 0)[0]
    mine = [None] * len(big_keys)
    for lyr, rsums, parts in ((1, rs.rsums1, rs.parts), (0, rsums0, _rs_exchange(rsums0, axes, 0))):
        mine = [_add_chips(r, p, ax, s_idx, lyr, buf, name=f"rs_add_chips_{k}_l{lyr}")
                for r, p, ax, buf, k in zip(rsums, parts, axes, mine, big_keys)]
    gb = dict(zip(big_keys, _rs_join(mine)))
    gb["w_in"] = jnp.swapaxes(gb["w_in"], 1, 2)

    small_keys = list(_SMALL_KEYS)
    gs = dict(zip(small_keys, _all_reduce_small([gsmall[k] for k in small_keys])))
    blk = a["conv_w_dw"].shape[2]
    gs["conv_w_dw"] = lax.dynamic_slice_in_dim(gs["conv_w_dw"], s_me * blk, blk, axis=2)

    delta, new_m, new_v = {}, {}, {}
    for k in big_keys:
        shp = a[k].shape
        two_d = lambda t: t.reshape(-1, shp[-1])
        d_, m_, v_ = _adamw(two_d(a[k]), two_d(gb[k]), two_d(a["m_" + k]), two_d(a["v_" + k]), name=f"adamw_{k}")
        delta[k], new_m[k], new_v[k] = d_.reshape(shp), m_.reshape(shp), v_.reshape(shp)
    res = _adamw_small([flat(a[k]) for k in small_keys], [gs[k] for k in small_keys],
                       [flat(a["m_" + k]) for k in small_keys], [flat(a["v_" + k]) for k in small_keys])
    for dst, vals in zip((delta, new_m, new_v), res):
        dst.update({k: t.reshape(a[k].shape) for k, t in zip(small_keys, vals)})
    grads = {**gb, **{k: t.reshape(a[k].shape) for k, t in gs.items()}}

    return (loss, grad_x, *[grads[k] for k in _WEIGHTS], *[delta[k] for k in _WEIGHTS],
            *[new_m[k] for k in _WEIGHTS], *[new_v[k] for k in _WEIGHTS])
```

```python
import functools
import math

import numpy as np
import jax
import jax.numpy as jnp
from jax import lax
from jax.experimental import pallas as pl
from jax.experimental.pallas import tpu as pltpu

F32 = jnp.float32
BF16 = jnp.bfloat16
EPS = 1e-6
VMEM_LIMIT = 56 * 1024 * 1024
LANES = 128
HEAD_DIM = 64
CHUNK = 64
N_LEFT = 8
MAX_REL = 128
ATT_TQ = 256
CONV_W = 31
HALO = 32
ROW_CHUNK = 256
SUBLANES = 8
GROUP_LOG = 3
NEG = -1e30
N_CHIPS = 4
PACK_COLS = 1024

ADAM_LR = 0.001
ADAM_B1 = 0.9
ADAM_B2 = 0.999
ADAM_EPS = 1e-08
ADAM_WD = 0.01
ADAM_STEP = 10

MESH = pl.DeviceIdType.MESH
_ANY = pl.BlockSpec(memory_space=pl.ANY)


def _cp(*sem):
    return pltpu.CompilerParams(dimension_semantics=sem, vmem_limit_bytes=VMEM_LIMIT)


def _sds(shape, dtype):
    return jax.ShapeDtypeStruct(shape, dtype)


def _tile(n, pref):
    t = min(n, pref)
    while n % t:
        t -= 8
    return t


def _sigmoid(x):
    return jax.nn.sigmoid(x)


_GELU_C = math.sqrt(2.0 / math.pi)


def _gelu(y):
    return 0.5 * y * (1.0 + jnp.tanh(_GELU_C * (y + 0.044715 * y * y * y)))


def _gelu_grad(y):
    th = jnp.tanh(_GELU_C * (y + 0.044715 * y * y * y))
    return 0.5 * (1.0 + th) + 0.5 * y * (1.0 - th * th) * _GELU_C * (1.0 + 3.0 * 0.044715 * y * y)


def _dot(a, b):
    return jnp.dot(a, b, preferred_element_type=F32)


def _dot_t0(a, b):
    return lax.dot_general(a, b, (((0,), (0,)), ((), ())), preferred_element_type=F32)


def _dot_t1(a, b):
    return lax.dot_general(a, b, (((1,), (1,)), ((), ())), preferred_element_type=F32)


def _slab_out(nl, l, shape, buf, n_in):
    sds = _sds((nl,) + tuple(shape), F32)
    if buf is None:
        return [], [], sds, {}
    return [buf], [_ANY], sds, {n_in: 0}


def _norm_mm(x, g, w, l, *, tm, tn, ntiles, pieces, transposed, name, comm=None):
    n, d = x.shape
    m = ntiles * tn
    mp = m // pieces
    npj = mp // tn
    nc = 0 if comm is None else len(comm[0])

    def body(x_ref, g_ref, w_ref, *rest):
        h_ref, y_ref = rest[nc:nc + 2]
        h_scr = rest[2 * nc + 2]
        if comm is not None:
            finish = _hosted_gather_steps(comm, rest[nc + 2:2 * nc + 2], rest[-2:],
                                          pl.program_id(0) * ntiles + pl.program_id(1), (n // tm) * ntiles)

        @pl.when(pl.program_id(1) == 0)
        def _():
            for r0 in range(0, tm, ROW_CHUNK):
                rows = slice(r0, r0 + ROW_CHUNK)
                xv = x_ref[rows, :]
                r = lax.rsqrt(jnp.mean(xv * xv, axis=-1, keepdims=True) + EPS)
                hb = (xv * r * g_ref[...]).astype(BF16)
                h_scr[rows, :] = hb
                h_ref[rows, :] = hb

        mm = _dot_t1 if transposed else _dot
        y_ref[...] = mm(h_scr[...], w_ref[...]).astype(BF16)
        if comm is not None:
            finish()

    wspec = (pl.BlockSpec((None, tn, d), lambda i, j: (l, j, 0)) if transposed
             else pl.BlockSpec((None, d, tn), lambda i, j: (l, 0, j)))
    c_in, c_ispec, c_ospec, c_oshape, c_scr, aliases = _hosted_gather(comm, 3, 2)
    return pl.pallas_call(
        body, grid=(n // tm, ntiles),
        in_specs=[pl.BlockSpec((tm, d), lambda i, j: (i, 0)), pl.BlockSpec((1, d), lambda i, j: (0, 0)), wspec] + c_ispec,
        out_specs=[pl.BlockSpec((tm, d), lambda i, j: (i, 0)),
                   pl.BlockSpec((None, tm, tn), lambda i, j: (j // npj, i, j % npj))] + c_ospec,
        out_shape=[_sds((n, d), BF16), _sds((pieces, n, mp), BF16)] + c_oshape,
        input_output_aliases=aliases,
        scratch_shapes=[pltpu.VMEM((tm, d), BF16)] + c_scr,
        compiler_params=_cp("arbitrary", "arbitrary"), name=name)(x, g, w, *c_in)


def _mm_t(a, w, l, *, tm, tn, off, ntiles, name):
    n, k = a.shape

    def body(a_ref, w_ref, y_ref):
        y_ref[...] = _dot_t1(a_ref[...], w_ref[...]).astype(BF16)

    return pl.pallas_call(
        body, grid=(n // tm, ntiles),
        in_specs=[pl.BlockSpec((tm, k), lambda i, j: (i, 0)), pl.BlockSpec((None, tn, k), lambda i, j: (l, off + j, 0))],
        out_specs=pl.BlockSpec((tm, tn), lambda i, j: (i, j)),
        out_shape=_sds((n, ntiles * tn), BF16),
        compiler_params=_cp("parallel", "arbitrary"), name=name)(a, w)


def _ffn_down(ab, wd, l, x, *, tm, tk, name):
    _, n, dff = ab.shape
    d = x.shape[1]
    nk = dff // tk

    def body(a_ref, b_ref, wd_ref, x_ref, o_ref, acc):
        k = pl.program_id(1)
        a = a_ref[...].astype(F32)
        b = b_ref[...].astype(F32)
        act = (a * _sigmoid(a) * b).astype(BF16)
        part = _dot(act, wd_ref[pl.ds(pl.multiple_of(k * tk, tk), tk), :])

        @pl.when(k == 0)
        def _():
            acc[...] = part

        @pl.when(k > 0)
        def _():
            acc[...] += part

        @pl.when(k == nk - 1)
        def _():
            o_ref[...] = x_ref[...] + 0.5 * acc[...]

    return pl.pallas_call(
        body, grid=(n // tm, nk),
        in_specs=[pl.BlockSpec((None, tm, tk), lambda i, k: (0, i, k)),
                  pl.BlockSpec((None, tm, tk), lambda i, k: (1, i, k)),
                  pl.BlockSpec((None, dff, d), lambda i, k: (l, 0, 0)),
                  pl.BlockSpec((tm, d), lambda i, k: (i, 0))],
        out_specs=pl.BlockSpec((tm, d), lambda i, k: (i, 0)),
        out_shape=_sds((n, d), F32),
        scratch_shapes=[pltpu.VMEM((tm, d), F32)],
        compiler_params=_cp("parallel", "arbitrary"), name=name)(ab, ab, wd, x)


def _ffn_dact(dx, wd, l, ab, nl, dwd_buf, *, tm, tk, name):
    n, d = dx.shape
    dff = ab.shape[2]
    half = ((tk // LANES + 1) // 2) * LANES
    chunks = ((0, half), (half, tk))

    def body(dx_ref, wd_ref, a_ref, b_ref, *rest):
        dab_ref, dwd_ref = rest[-2:]
        do = (0.5 * dx_ref[...]).astype(BF16)

        @pl.when(pl.program_id(1) == 0)
        def _():
            dwd_ref[...] = jnp.zeros_like(dwd_ref)

        for c0, c1 in chunks:
            dact = _dot_t1(do, wd_ref[c0:c1, :])
            a = a_ref[:, c0:c1].astype(F32)
            b = b_ref[:, c0:c1].astype(F32)
            sg = _sigmoid(a)
            silu = a * sg
            dab_ref[0, :, c0:c1] = (dact * b * (sg * (1.0 + a * (1.0 - sg)))).astype(BF16)
            dab_ref[1, :, c0:c1] = (dact * silu).astype(BF16)
            dwd_ref[c0:c1, :] += _dot_t0((silu * b).astype(BF16), do)

    extra, extra_specs, dwd_shape, aliases = _slab_out(nl, l, (dff, d), dwd_buf, 4)
    aliases = {k: 1 for k in aliases}
    return pl.pallas_call(
        body, grid=(dff // tk, n // tm),
        in_specs=[pl.BlockSpec((tm, d), lambda j, i: (i, 0)),
                  pl.BlockSpec((None, tk, d), lambda j, i: (l, j, 0)),
                  pl.BlockSpec((None, tm, tk), lambda j, i: (0, i, j)),
                  pl.BlockSpec((None, tm, tk), lambda j, i: (1, i, j)), *extra_specs],
        out_specs=[pl.BlockSpec((2, tm, tk), lambda j, i: (0, i, j)),
                   pl.BlockSpec((None, tk, d), lambda j, i: (l, j, 0))],
        out_shape=[_sds((2, n, dff), BF16), dwd_shape],
        input_output_aliases=aliases,
        compiler_params=_cp("arbitrary", "arbitrary"), name=name)(dx, wd, ab, ab, *extra)


def _rms_bwd_epilogue(acc, x_ref, g_ref, dres_ref, dx_ref, dg_ref, i):
    dgp = jnp.zeros(dg_ref.shape, F32)
    for r0 in range(0, acc.shape[0], ROW_CHUNK):
        rows = slice(r0, r0 + ROW_CHUNK)
        dh = acc[rows, :]
        xv = x_ref[rows, :]
        r = lax.rsqrt(jnp.mean(xv * xv, axis=-1, keepdims=True) + EPS)
        xn = xv * r
        dgp = dgp + jnp.sum(dh * xn, axis=0, keepdims=True)
        dxh = dh * g_ref[...]
        dx_ref[rows, :] = dres_ref[rows, :] + r * (dxh - xn * jnp.mean(dxh * xn, axis=-1, keepdims=True))

    @pl.when(i == 0)
    def _():
        dg_ref[...] = dgp

    @pl.when(i > 0)
    def _():
        dg_ref[...] += dgp


def _ffn_dx(dab, wu, l, x, g, dres, *, tm, tk, name, comm=None):
    p, n, mp = dab.shape
    d = x.shape[1]
    nkp = mp // tk
    nk = p * nkp
    ni = n // tm
    nc = 0 if comm is None else len(comm[0])

    def body(dy_ref, w_ref, x_ref, g_ref, dres_ref, *rest):
        dx_ref, dg_ref = rest[nc:nc + 2]
        acc = rest[2 * nc + 2]
        k = pl.program_id(1)
        if comm is not None:
            start, wait = _swap_ops(rest[:nc], rest[nc + 2:2 * nc + 2], rest[-2], rest[-1], comm[1])

            @pl.when((pl.program_id(0) == 0) & (k == 0))
            def _():
                start()

        part = _dot_t1(dy_ref[...], w_ref[...])

        @pl.when(k == 0)
        def _():
            acc[...] = part

        @pl.when(k > 0)
        def _():
            acc[...] += part

        @pl.when(k == nk - 1)
        def _():
            _rms_bwd_epilogue(acc, x_ref, g_ref, dres_ref, dx_ref, dg_ref, pl.program_id(0))

        if comm is not None:
            @pl.when((pl.program_id(0) == ni - 1) & (k == nk - 1))
            def _():
                wait()

    comm_in = [] if comm is None else list(comm[0])
    comm_out = [_sds(t.shape[1:], t.dtype) for t in comm_in]
    comm_scr = [] if comm is None else [pltpu.SemaphoreType.DMA((nc,)), pltpu.SemaphoreType.DMA((nc,))]
    return pl.pallas_call(
        body, grid=(ni, nk),
        in_specs=[pl.BlockSpec((None, tm, tk), lambda i, k: (k // nkp, i, k % nkp)),
                  pl.BlockSpec((None, d, tk), lambda i, k: (l, 0, k)),
                  pl.BlockSpec((tm, d), lambda i, k: (i, 0)),
                  pl.BlockSpec((1, d), lambda i, k: (0, 0)),
                  pl.BlockSpec((tm, d), lambda i, k: (i, 0))] + [_ANY] * nc,
        out_specs=[pl.BlockSpec((tm, d), lambda i, k: (i, 0)), pl.BlockSpec((1, d), lambda i, k: (0, 0))] + [_ANY] * nc,
        out_shape=[_sds((n, d), F32), _sds((1, d), F32)] + comm_out,
        scratch_shapes=[pltpu.VMEM((tm, d), F32)] + comm_scr,
        compiler_params=_cp("arbitrary", "arbitrary"), name=name)(dab, wu, x, g, dres, *comm_in)


def _mix_dx(dpa, dgl, wt, l, x, g, dres, *, tm, tk, name):
    n, d = x.shape
    n1 = dpa.shape[1] // tk
    n2 = dgl.shape[1] // tk
    nk = n1 + n2

    def body(d1_ref, d2_ref, w_ref, x_ref, g_ref, dres_ref, dx_ref, dg_ref, acc):
        k = pl.program_id(1)

        @pl.when(k == 0)
        def _():
            acc[...] = _dot(d1_ref[...], w_ref[...])

        @pl.when((k > 0) & (k < n1))
        def _():
            acc[...] += _dot(d1_ref[...], w_ref[...])

        @pl.when(k >= n1)
        def _():
            acc[...] += _dot(d2_ref[...], w_ref[...])

        @pl.when(k == nk - 1)
        def _():
            _rms_bwd_epilogue(acc, x_ref, g_ref, dres_ref, dx_ref, dg_ref, pl.program_id(0))

    return pl.pallas_call(
        body, grid=(n // tm, nk),
        in_specs=[pl.BlockSpec((tm, tk), lambda i, k: (i, jnp.minimum(k, n1 - 1))),
                  pl.BlockSpec((tm, tk), lambda i, k: (i, jnp.maximum(k - n1, 0))),
                  pl.BlockSpec((None, tk, d), lambda i, k: (l, k, 0)),
                  pl.BlockSpec((tm, d), lambda i, k: (i, 0)),
                  pl.BlockSpec((1, d), lambda i, k: (0, 0)),
                  pl.BlockSpec((tm, d), lambda i, k: (i, 0))],
        out_specs=[pl.BlockSpec((tm, d), lambda i, k: (i, 0)), pl.BlockSpec((1, d), lambda i, k: (0, 0))],
        out_shape=[_sds((n, d), F32), _sds((1, d), F32)],
        scratch_shapes=[pltpu.VMEM((tm, d), F32)],
        compiler_params=_cp("arbitrary", "arbitrary"), name=name)(dpa, dgl, wt, x, g, dres)


def _mm_tn(a, b, l, nl, buf, *, ta, tb, tk, name):
    pa, n, ka = a.shape
    pb, _, kb = b.shape
    nap = ka // ta
    nbp = kb // tb

    def body(a_ref, b_ref, *rest):
        o_ref = rest[-1]

        @pl.when(pl.program_id(2) == 0)
        def _():
            o_ref[...] = jnp.zeros_like(o_ref)

        o_ref[...] += _dot_t0(a_ref[...], b_ref[...])

    extra, extra_specs, out_shape, aliases = _slab_out(nl, l, (pa * ka, pb * kb), buf, 2)
    return pl.pallas_call(
        body, grid=(pa * nap, pb * nbp, n // tk),
        in_specs=[pl.BlockSpec((None, tk, ta), lambda i, j, k: (i // nap, k, i % nap)),
                  pl.BlockSpec((None, tk, tb), lambda i, j, k: (j // nbp, k, j % nbp)), *extra_specs],
        out_specs=pl.BlockSpec((None, ta, tb), lambda i, j, k: (l, i, j)),
        out_shape=out_shape, input_output_aliases=aliases,
        compiler_params=_cp("parallel", "parallel", "arbitrary"), name=name)(a, b, *extra)


def _dwin_t(dpa, dgl, h, l, nl, buf, *, ta, tk, name):
    n, d = h.shape
    n1 = dpa.shape[1] // ta
    n2 = dgl.shape[1] // ta

    def body(a1_ref, a2_ref, h_ref, *rest):
        o_ref = rest[-1]
        i = pl.program_id(0)

        @pl.when(pl.program_id(1) == 0)
        def _():
            o_ref[...] = jnp.zeros_like(o_ref)

        @pl.when(i < n1)
        def _():
            o_ref[...] += _dot_t0(a1_ref[...], h_ref[...])

        @pl.when(i >= n1)
        def _():
            o_ref[...] += _dot_t0(a2_ref[...], h_ref[...])

    extra, extra_specs, out_shape, aliases = _slab_out(nl, l, ((n1 + n2) * ta, d), buf, 3)
    return pl.pallas_call(
        body, grid=(n1 + n2, n // tk),
        in_specs=[pl.BlockSpec((tk, ta), lambda i, k: (jnp.where(i < n1, k, 0), jnp.minimum(i, n1 - 1))),
                  pl.BlockSpec((tk, ta), lambda i, k: (jnp.where(i >= n1, k, 0), jnp.maximum(i - n1, 0))),
                  pl.BlockSpec((tk, d), lambda i, k: (k, 0)), *extra_specs],
        out_specs=pl.BlockSpec((None, ta, d), lambda i, k: (l, i, 0)),
        out_shape=out_shape, input_output_aliases=aliases,
        compiler_params=_cp("parallel", "arbitrary"), name=name)(dpa, dgl, h, *extra)


def _loss_grad(y, t, *, tm, name):
    n, d = y.shape

    def body(y_ref, t_ref, dy_ref, l_ref):
        e = y_ref[...] - t_ref[...]
        dy_ref[...] = e * (1.0 / d)
        part = jnp.sum(e * e, axis=0, keepdims=True)

        @pl.when(pl.program_id(0) == 0)
        def _():
            l_ref[...] = part

        @pl.when(pl.program_id(0) > 0)
        def _():
            l_ref[...] += part

    return pl.pallas_call(
        body, grid=(n // tm,),
        in_specs=[pl.BlockSpec((tm, d), lambda i: (i, 0)), pl.BlockSpec((tm, d), lambda i: (i, 0))],
        out_specs=[pl.BlockSpec((tm, d), lambda i: (i, 0)), pl.BlockSpec((1, d), lambda i: (0, 0))],
        out_shape=[_sds((n, d), F32), _sds((1, d), F32)],
        compiler_params=_cp("arbitrary"), name=name)(y, t)


def _s5_fwd(proj, sp, wglu, l, *, bl, s, t, name):
    n = bl * s
    ds5, gp = sp["bblk_r"].shape
    nt = s // t
    ng = t // SUBLANES
    glog = int(math.log2(ng))

    def body(u_ref, br_ref, bi_ref, pw_ref, p8_ref, cr_ref, ci_ref, d_ref, wg_ref,
             xr_ref, xi_ref, yp_ref, zg_ref, o_ref, carry, st):
        @pl.when(pl.program_id(1) == 0)
        def _():
            carry[...] = jnp.zeros_like(carry)

        u = u_ref[...]
        sub = lax.broadcasted_iota(jnp.int32, (t, gp), 0) % SUBLANES
        xr = _dot(u, br_ref[...])
        xi = _dot(u, bi_ref[...])
        for k in range(GROUP_LOG):
            sh = 1 << k
            pr = pw_ref[2 * k:2 * k + 1, :]
            pi = pw_ref[2 * k + 1:2 * k + 2, :]
            keep = sub >= sh
            sr = jnp.where(keep, pltpu.roll(xr, sh, 0), 0.0)
            si = jnp.where(keep, pltpu.roll(xi, sh, 0), 0.0)
            xr, xi = xr + pr * sr - pi * si, xi + pr * si + pi * sr
        xr_ref[...] = xr
        xi_ref[...] = xi
        grow = lax.broadcasted_iota(jnp.int32, (ng, gp), 0)
        cr = carry[0:1, :]
        ci = carry[1:2, :]
        a8r = pw_ref[2 * GROUP_LOG:2 * GROUP_LOG + 1, :]
        a8i = pw_ref[2 * GROUP_LOG + 1:2 * GROUP_LOG + 2, :]
        head = grow == 0
        for g in range(ng):
            st[g:g + 1, :] = xr_ref[(g + 1) * SUBLANES - 1:(g + 1) * SUBLANES, :]
            st[ng + g:ng + g + 1, :] = xi_ref[(g + 1) * SUBLANES - 1:(g + 1) * SUBLANES, :]
        sr_ = st[0:ng, :] + jnp.where(head, a8r * cr - a8i * ci, 0.0)
        si_ = st[ng:2 * ng, :] + jnp.where(head, a8r * ci + a8i * cr, 0.0)
        for k in range(glog):
            sh = 1 << k
            pr = pw_ref[2 * (GROUP_LOG + k):2 * (GROUP_LOG + k) + 1, :]
            pi = pw_ref[2 * (GROUP_LOG + k) + 1:2 * (GROUP_LOG + k) + 2, :]
            keep = grow >= sh
            tr = jnp.where(keep, pltpu.roll(sr_, sh, 0), 0.0)
            ti = jnp.where(keep, pltpu.roll(si_, sh, 0), 0.0)
            sr_, si_ = sr_ + pr * tr - pi * ti, si_ + pr * ti + pi * tr
        tail = grow == ng - 1
        carry[0:1, :] = jnp.sum(jnp.where(tail, sr_, 0.0), axis=0, keepdims=True)
        carry[1:2, :] = jnp.sum(jnp.where(tail, si_, 0.0), axis=0, keepdims=True)
        st[0:ng, :] = jnp.where(head, cr, pltpu.roll(sr_, 1, 0))
        st[ng:2 * ng, :] = jnp.where(head, ci, pltpu.roll(si_, 1, 0))
        p8r = p8_ref[0:SUBLANES, :]
        p8i = p8_ref[SUBLANES:2 * SUBLANES, :]
        for g in range(ng):
            grp = slice(g * SUBLANES, (g + 1) * SUBLANES)
            pr = st[g:g + 1, :]
            pi = st[ng + g:ng + g + 1, :]
            xr_ref[grp, :] = xr_ref[grp, :] + p8r * pr - p8i * pi
            xi_ref[grp, :] = xi_ref[grp, :] + p8r * pi + p8i * pr
        xr = xr_ref[...]
        xi = xi_ref[...]
        y = _dot(xr.astype(BF16), cr_ref[...]) + _dot(xi.astype(BF16), ci_ref[...]) + d_ref[...] * u.astype(F32)
        yp_ref[...] = y
        zg = _dot(_gelu(y).astype(BF16), wg_ref[...])
        zg_ref[...] = zg
        o_ref[...] = (zg[:, :ds5] * _sigmoid(zg[:, ds5:])).astype(BF16)

    const = lambda shape: pl.BlockSpec(shape, lambda b, i: (0, 0))
    row = lambda w: pl.BlockSpec((t, w), lambda b, i: (b * nt + i, 0))
    return pl.pallas_call(
        body, grid=(bl, nt),
        in_specs=[row(ds5), const((ds5, gp)), const((ds5, gp)), const((2 * (GROUP_LOG + glog), gp)),
                  const((2 * SUBLANES, gp)),
                  const((gp, ds5)), const((gp, ds5)), const((1, ds5)),
                  pl.BlockSpec((None, ds5, 2 * ds5), lambda b, i: (l, 0, 0))],
        out_specs=[row(gp), row(gp), row(ds5), row(2 * ds5), row(ds5)],
        out_shape=[_sds((n, gp), F32), _sds((n, gp), F32), _sds((n, ds5), F32), _sds((n, 2 * ds5), F32),
                   _sds((n, ds5), BF16)],
        scratch_shapes=[pltpu.VMEM((2, gp), F32), pltpu.VMEM((2 * ng, gp), F32)],
        compiler_params=_cp("arbitrary", "arbitrary"), name=name)(
            proj, sp["bblk_r"], sp["bblk_i"], sp["pw"], sp["p8"], sp["cblk_r"], sp["cblk_in"], sp["d"], wglu)


def _s5_bwd(ds, yp, zg, xr, xi, proj, sp, wglu, l, nl, dwg_buf, *, bl, s, t, name):
    n = bl * s
    ds5, gp = sp["bblk_r"].shape
    nt = s // t
    tb = t // 8
    ng = t // SUBLANES
    glog = int(math.log2(ng))

    def body(ds_ref, yp_ref, zg_ref, xr_ref, xi_ref, hr_ref, hi_ref, u_ref, wg_ref, cr_ref, ci_ref,
             br_ref, bi_ref, pw_ref, q8_ref, d_ref, *rest):
        du_ref, dd_ref, dcr_ref, dci_ref, dbr_ref, dbi_ref, da_ref, dwg_ref, carry, gr_scr, gi_scr, st = rest[-12:]
        b = pl.program_id(0)
        i = pl.program_id(1)
        tile = nt - 1 - i

        @pl.when((b == 0) & (i == 0))
        def _():
            for r in (dwg_ref, dd_ref, dcr_ref, dci_ref, dbr_ref, dbi_ref, da_ref):
                r[...] = jnp.zeros_like(r)

        @pl.when(i == 0)
        def _():
            carry[...] = jnp.zeros_like(carry)

        dsv = ds_ref[...].astype(F32)
        zgv = zg_ref[...]
        za = zgv[:, :ds5]
        sg = _sigmoid(zgv[:, ds5:])
        dzg = jnp.concatenate([dsv * sg, dsv * za * sg * (1.0 - sg)], axis=1).astype(BF16)
        y = yp_ref[...]
        dwg_ref[...] += _dot_t0(_gelu(y).astype(BF16), dzg)
        dy = _dot_t1(dzg, wg_ref[...]) * _gelu_grad(y)
        ub = u_ref[...]
        uf = ub.astype(F32)
        dd_ref[...] += jnp.sum(dy * uf, axis=0, keepdims=True)
        dyb = dy.astype(BF16)
        xrv = xr_ref[...]
        xiv = xi_ref[...]
        dcr_ref[...] += _dot_t0(xrv.astype(BF16), dyb)
        dci_ref[...] += _dot_t0(xiv.astype(BF16), dyb)

        rows = lax.broadcasted_iota(jnp.int32, (t, gp), 0)
        sub = rows % SUBLANES
        gr = _dot_t1(dyb, cr_ref[...])
        gi = _dot_t1(dyb, ci_ref[...])
        for k in range(GROUP_LOG):
            sh = 1 << k
            pr = pw_ref[2 * k:2 * k + 1, :]
            pi = pw_ref[2 * k + 1:2 * k + 2, :]
            keep = sub < SUBLANES - sh
            sr = jnp.where(keep, pltpu.roll(gr, t - sh, 0), 0.0)
            si = jnp.where(keep, pltpu.roll(gi, t - sh, 0), 0.0)
            gr, gi = gr + pr * sr + pi * si, gi + pr * si - pi * sr
        gr_scr[...] = gr
        gi_scr[...] = gi
        grow = lax.broadcasted_iota(jnp.int32, (ng, gp), 0)
        cr = carry[0:1, :]
        ci = carry[1:2, :]
        a8r = pw_ref[2 * GROUP_LOG:2 * GROUP_LOG + 1, :]
        a8i = pw_ref[2 * GROUP_LOG + 1:2 * GROUP_LOG + 2, :]
        tail = grow == ng - 1
        for g in range(ng):
            st[g:g + 1, :] = gr_scr[g * SUBLANES:g * SUBLANES + 1, :]
            st[ng + g:ng + g + 1, :] = gi_scr[g * SUBLANES:g * SUBLANES + 1, :]
        sr_ = st[0:ng, :] + jnp.where(tail, a8r * cr + a8i * ci, 0.0)
        si_ = st[ng:2 * ng, :] + jnp.where(tail, a8r * ci - a8i * cr, 0.0)
        for k in range(glog):
            sh = 1 << k
            pr = pw_ref[2 * (GROUP_LOG + k):2 * (GROUP_LOG + k) + 1, :]
            pi = pw_ref[2 * (GROUP_LOG + k) + 1:2 * (GROUP_LOG + k) + 2, :]
            keep = grow < ng - sh
            tr = jnp.where(keep, pltpu.roll(sr_, ng - sh, 0), 0.0)
            ti = jnp.where(keep, pltpu.roll(si_, ng - sh, 0), 0.0)
            sr_, si_ = sr_ + pr * tr + pi * ti, si_ + pr * ti - pi * tr
        head = grow == 0
        carry[0:1, :] = jnp.sum(jnp.where(head, sr_, 0.0), axis=0, keepdims=True)
        carry[1:2, :] = jnp.sum(jnp.where(head, si_, 0.0), axis=0, keepdims=True)
        st[0:ng, :] = jnp.where(tail, cr, pltpu.roll(sr_, ng - 1, 0))
        st[ng:2 * ng, :] = jnp.where(tail, ci, pltpu.roll(si_, ng - 1, 0))
        q8r = q8_ref[0:SUBLANES, :]
        q8i = q8_ref[SUBLANES:2 * SUBLANES, :]
        for g in range(ng):
            grp = slice(g * SUBLANES, (g + 1) * SUBLANES)
            pr = st[g:g + 1, :]
            pi = st[ng + g:ng + g + 1, :]
            gr_scr[grp, :] = gr_scr[grp, :] + q8r * pr - q8i * pi
            gi_scr[grp, :] = gi_scr[grp, :] + q8r * pi + q8i * pr
        gr = gr_scr[...]
        gi = gi_scr[...]
        first = rows == 0

        live = jnp.where(tile > 0, 1.0, 0.0)
        xpr = jnp.where(first, hr_ref[7:8, :] * live, pltpu.roll(xrv, 1, 0))
        xpi = jnp.where(first, hi_ref[7:8, :] * live, pltpu.roll(xiv, 1, 0))
        da_ref[0:1, :] += jnp.sum(gr * xpr + gi * xpi, axis=0, keepdims=True)
        da_ref[1:2, :] += jnp.sum(gi * xpr - gr * xpi, axis=0, keepdims=True)

        grb = gr.astype(BF16)
        gib = gi.astype(BF16)
        dbr_ref[...] += _dot_t0(ub, grb)
        dbi_ref[...] += _dot_t0(ub, gib)
        du_ref[...] = (_dot_t1(grb, br_ref[...]) + _dot_t1(gib, bi_ref[...]) + dy * d_ref[...]).astype(BF16)

    const = lambda shape: pl.BlockSpec(shape, lambda b, i: (0, 0))
    row = lambda w: pl.BlockSpec((t, w), lambda b, i: (b * nt + nt - 1 - i, 0))
    halo = pl.BlockSpec((8, gp), lambda b, i: (jnp.maximum((b * nt + nt - 1 - i) * tb - 1, 0), 0))
    extra, extra_specs, dwg_shape, aliases = _slab_out(nl, l, (ds5, 2 * ds5), dwg_buf, 16)
    aliases = {k: 7 for k in aliases}
    return pl.pallas_call(
        body, grid=(bl, nt),
        in_specs=[row(ds5), row(ds5), row(2 * ds5), row(gp), row(gp), halo, halo, row(ds5),
                  pl.BlockSpec((None, ds5, 2 * ds5), lambda b, i: (l, 0, 0)),
                  const((gp, ds5)), const((gp, ds5)), const((ds5, gp)), const((ds5, gp)),
                  const((2 * (GROUP_LOG + glog), gp)), const((2 * SUBLANES, gp)), const((1, ds5)), *extra_specs],
        out_specs=[row(ds5), const((1, ds5)), const((gp, ds5)), const((gp, ds5)),
                   const((ds5, gp)), const((ds5, gp)), const((2, gp)),
                   pl.BlockSpec((None, ds5, 2 * ds5), lambda b, i: (l, 0, 0))],
        out_shape=[_sds((n, ds5), BF16), _sds((1, ds5), F32), _sds((gp, ds5), F32),
                   _sds((gp, ds5), F32), _sds((ds5, gp), F32), _sds((ds5, gp), F32), _sds((2, gp), F32), dwg_shape],
        input_output_aliases=aliases,
        scratch_shapes=[pltpu.VMEM((2, gp), F32), pltpu.VMEM((t, gp), F32), pltpu.VMEM((t, gp), F32),
                        pltpu.VMEM((2 * ng, gp), F32)],
        compiler_params=_cp("arbitrary", "arbitrary"), name=name)(
            ds, yp, zg, xr, xi, xr, xi, proj, wglu, sp["cblk_r"], sp["cblk_in"],
            sp["bblk_r"], sp["bblk_i"], sp["pw"], sp["q8"], sp["d"], *extra)


def _head_norm(x, first):
    x2 = x * x
    sa = jnp.sum(jnp.where(first, x2, 0.0), axis=-1, keepdims=True)
    sb = jnp.sum(jnp.where(first, 0.0, x2), axis=-1, keepdims=True)
    r = jnp.where(first, lax.rsqrt(sa * (1.0 / HEAD_DIM) + EPS), lax.rsqrt(sb * (1.0 / HEAD_DIM) + EPS))
    return x * r, r


def _attn_specs(bl, s, datt, qoff):
    nq = s // ATT_TQ
    nb = datt // LANES
    col = lambda blk: (lambda b, h, q: (b * nq + q, qoff + blk * nb + h))
    win = lambda blk, j: (lambda b, h, q: (b * nq + jnp.maximum(q - 2 + j, 0), qoff + blk * nb + h))
    tile = lambda f: pl.BlockSpec((ATT_TQ, LANES), f)
    qs = tile(col(0))
    ks = [tile(win(1, j)) for j in range(3)]
    vs = [tile(win(2, j)) for j in range(3)]
    return nq, nb, qs, ks, vs


def _attn_probs(q_ref, k_refs, gq_ref, gk_ref, bias_ref):
    qt = pl.program_id(2)
    lane = lax.broadcasted_iota(jnp.int32, (1, LANES), 1)
    first = lane < HEAD_DIM
    qh, rq = _head_norm(q_ref[...].astype(F32), first)
    qn = qh * gq_ref[...]
    kc = jnp.concatenate([r[...] for r in k_refs], axis=0).astype(F32)
    kh, _ = _head_norm(kc, first)
    kn = (kh * gk_ref[...]).astype(BF16)
    kpos = (qt - 2) * ATT_TQ + lax.broadcasted_iota(jnp.int32, (1, 3 * ATT_TQ), 1)
    valid = kpos >= 0
    scale = HEAD_DIM ** -0.5
    masks = (first, jnp.logical_not(first))
    qas, ps = [], []
    for hh in range(2):
        qa = jnp.where(masks[hh], qn, 0.0).astype(BF16)
        sc = _dot_t1(qa, kn) * scale + bias_ref[hh]
        sc = jnp.where(valid, sc, NEG)
        e = jnp.exp(sc - jnp.max(sc, axis=-1, keepdims=True))
        ps.append(e / jnp.sum(e, axis=-1, keepdims=True))
        qas.append(qa)
    return first, masks, qh, rq, kn, qas, ps


def _attn_fwd(proj, gq2, gk2, bias, *, bl, s, datt, qoff, name, comm=None):
    n = bl * s
    nq, nb, qs, ks, vs = _attn_specs(bl, s, datt, qoff)
    nc = 0 if comm is None else len(comm[0])

    def body(q_ref, k0, k1, k2, v0, v1, v2, gq_ref, gk_ref, bias_ref, *rest):
        o_ref = rest[nc]
        if comm is not None:
            finish = _hosted_gather_steps(comm, rest[nc + 1:2 * nc + 1], rest[-2:],
                                          (pl.program_id(0) * nb + pl.program_id(1)) * nq + pl.program_id(2),
                                          bl * nb * nq)
        first, _, _, _, _, _, ps = _attn_probs(q_ref, (k0, k1, k2), gq_ref, gk_ref, bias_ref)
        vc = jnp.concatenate([v0[...], v1[...], v2[...]], axis=0)
        o0 = _dot(ps[0].astype(BF16), vc)
        o1 = _dot(ps[1].astype(BF16), vc)
        o_ref[...] = jnp.where(first, o0, o1).astype(BF16)
        if comm is not None:
            finish()

    gs = pl.BlockSpec((1, LANES), lambda b, h, q: (0, 0))
    c_in, c_ispec, c_ospec, c_oshape, c_scr, aliases = _hosted_gather(comm, 10, 1)
    res = pl.pallas_call(
        body, grid=(bl, nb, nq),
        in_specs=[qs, *ks, *vs, gs, gs, pl.BlockSpec((2, ATT_TQ, 3 * ATT_TQ), lambda b, h, q: (h, 0, 0))] + c_ispec,
        out_specs=[pl.BlockSpec((ATT_TQ, LANES), lambda b, h, q: (b * nq + q, h))] + c_ospec,
        out_shape=[_sds((n, datt), BF16)] + c_oshape,
        input_output_aliases=aliases, scratch_shapes=c_scr,
        compiler_params=_cp("arbitrary", "arbitrary", "arbitrary"), name=name)(
            proj, proj, proj, proj, proj, proj, proj, gq2, gk2, bias, *c_in)
    return res[0], res[1:]


def _attn_bwd(do, proj, gq2, gk2, bias, *, bl, s, datt, qoff, name, comm=None):
    n = bl * s
    nq, nb, qs, ks, vs = _attn_specs(bl, s, datt, qoff)
    srows = s + 2 * ATT_TQ
    scale = HEAD_DIM ** -0.5
    nc = 0 if comm is None else len(comm[0])

    def body(do_ref, q_ref, k0, k1, k2, v0, v1, v2, gq_ref, gk_ref, bias_ref, *rest):
        dq_ref, dk_ref, dv_ref, db_ref, dgq_ref = rest[nc:nc + 5]
        qt = pl.program_id(2)
        if comm is not None:
            start, wait = _exchange_ops(rest[:nc], rest[nc + 5:2 * nc + 5], rest[-2], rest[-1], comm[1], comm[2])
            step = (pl.program_id(0) * nb + pl.program_id(1)) * nq + qt

            @pl.when(step == 0)
            def _():
                start()

        @pl.when(qt == 0)
        def _():
            dk_ref[...] = jnp.zeros_like(dk_ref)
            dv_ref[...] = jnp.zeros_like(dv_ref)
            db_ref[...] = jnp.zeros_like(db_ref)
            dgq_ref[...] = jnp.zeros_like(dgq_ref)

        first, masks, qh, rq, kn, qas, ps = _attn_probs(q_ref, (k0, k1, k2), gq_ref, gk_ref, bias_ref)
        vc = jnp.concatenate([v0[...], v1[...], v2[...]], axis=0)
        dov = do_ref[...]
        dqn = jnp.zeros((ATT_TQ, LANES), F32)
        dkn = jnp.zeros((3 * ATT_TQ, LANES), F32)
        dv = jnp.zeros((3 * ATT_TQ, LANES), F32)
        for hh in range(2):
            doa = jnp.where(masks[hh], dov, jnp.zeros_like(dov))
            p = ps[hh]
            dp = _dot_t1(doa, vc)
            dsm = p * (dp - jnp.sum(dp * p, axis=-1, keepdims=True))
            db_ref[hh] += dsm
            dsc = (dsm * scale).astype(BF16)
            dqn = dqn + _dot(dsc, jnp.where(masks[hh], kn, jnp.zeros_like(kn)))
            dkn = dkn + _dot_t0(dsc, qas[hh])
            dv = dv + _dot_t0(p.astype(BF16), doa)
        start = pl.multiple_of(qt * ATT_TQ, ATT_TQ)
        dk_ref[pl.ds(start, 3 * ATT_TQ), :] += dkn
        dv_ref[pl.ds(start, 3 * ATT_TQ), :] += dv
        dgq_ref[...] += jnp.sum(dqn * qh, axis=0, keepdims=True)
        dqh = dqn * gq_ref[...]
        t = dqh * qh
        ma = jnp.sum(jnp.where(first, t, 0.0), axis=-1, keepdims=True) * (1.0 / HEAD_DIM)
        mb = jnp.sum(jnp.where(first, 0.0, t), axis=-1, keepdims=True) * (1.0 / HEAD_DIM)
        dq_ref[...] = (rq * (dqh - qh * jnp.where(first, ma, mb))).astype(BF16)
        if comm is not None:
            @pl.when(step == bl * nb * nq - 1)
            def _():
                wait()

    gs = pl.BlockSpec((1, LANES), lambda b, h, q: (0, 0))
    acc = pl.BlockSpec((None, srows, LANES), lambda b, h, q: (b, 0, h))
    comm_in = [] if comm is None else list(comm[0])
    comm_out = [] if comm is None else _parts_shapes(comm[0], comm[1])
    comm_scr = [] if comm is None else [pltpu.SemaphoreType.DMA((3 * nc,)), pltpu.SemaphoreType.DMA((3 * nc,))]
    return pl.pallas_call(
        body, grid=(bl, nb, nq),
        in_specs=[pl.BlockSpec((ATT_TQ, LANES), lambda b, h, q: (b * nq + q, h)), qs, *ks, *vs, gs, gs,
                  pl.BlockSpec((2, ATT_TQ, 3 * ATT_TQ), lambda b, h, q: (h, 0, 0))] + [_ANY] * nc,
        out_specs=[pl.BlockSpec((ATT_TQ, LANES), lambda b, h, q: (b * nq + q, h)), acc, acc,
                   pl.BlockSpec((None, 2, ATT_TQ, 3 * ATT_TQ), lambda b, h, q: (b, h, 0, 0)),
                   pl.BlockSpec((None, None, 1, LANES), lambda b, h, q: (b, h, 0, 0))] + [_ANY] * nc,
        out_shape=[_sds((n, datt), BF16), _sds((bl, srows, datt), F32), _sds((bl, srows, datt), F32),
                   _sds((bl, 2 * nb, ATT_TQ, 3 * ATT_TQ), F32), _sds((bl, nb, 1, LANES), F32)] + comm_out,
        scratch_shapes=comm_scr,
        compiler_params=_cp("arbitrary", "arbitrary", "arbitrary"), name=name)(
            do, proj, proj, proj, proj, proj, proj, proj, gq2, gk2, bias, *comm_in)


def _attn_kv_bwd(dkn, dv, proj, gk2, *, bl, s, datt, tm, koff, name):
    n = bl * s
    ns = s // tm
    off = 2 * ATT_TQ // tm
    nb = datt // LANES

    def body(dkn_ref, dv_ref, k_ref, gk_ref, dk_ref, dvo_ref, dgk_ref):
        lane = lax.broadcasted_iota(jnp.int32, (1, LANES), 1)
        first = lane < HEAD_DIM

        @pl.when((pl.program_id(0) == 0) & (pl.program_id(1) == 0) & (pl.program_id(2) == 0))
        def _():
            dgk_ref[...] = jnp.zeros_like(dgk_ref)

        dvo_ref[...] = dv_ref[...].astype(BF16)
        kh, rk = _head_norm(k_ref[...].astype(F32), first)
        dn = dkn_ref[...]
        dgk_ref[...] += jnp.sum(dn * kh, axis=0, keepdims=True)
        dh = dn * gk_ref[...]
        t = dh * kh
        ma = jnp.sum(jnp.where(first, t, 0.0), axis=-1, keepdims=True) * (1.0 / HEAD_DIM)
        mb = jnp.sum(jnp.where(first, 0.0, t), axis=-1, keepdims=True) * (1.0 / HEAD_DIM)
        dk_ref[...] = (rk * (dh - kh * jnp.where(first, ma, mb))).astype(BF16)

    accs = pl.BlockSpec((None, tm, LANES), lambda b, i, c: (b, i + off, c))
    outs = pl.BlockSpec((tm, LANES), lambda b, i, c: (b * ns + i, c))
    vec = pl.BlockSpec((1, LANES), lambda b, i, c: (0, 0))
    return pl.pallas_call(
        body, grid=(bl, ns, nb),
        in_specs=[accs, accs, pl.BlockSpec((tm, LANES), lambda b, i, c: (b * ns + i, koff + c)), vec],
        out_specs=[outs, outs, vec],
        out_shape=[_sds((n, datt), BF16), _sds((n, datt), BF16), _sds((1, LANES), F32)],
        compiler_params=_cp("arbitrary", "arbitrary", "arbitrary"), name=name)(dkn, dv, proj, gk2)


def _conv_fwd(proj, wdw, bdw, lng, lnb, *, bl, s, t, acol, name):
    n = bl * s
    dc = wdw.shape[1]
    nt = s // t
    hb = t // HALO

    def body(za_ref, zg_ref, ha_ref, hgt_ref, w_ref, b_ref, g_ref, be_ref, hg_ref, hc_ref, o_ref, ext):
        i = pl.program_id(1)
        hg = za_ref[...].astype(F32) * _sigmoid(zg_ref[...].astype(F32))
        live = jnp.where(i > 0, 1.0, 0.0)
        ext[0:HALO, :] = ha_ref[...].astype(F32) * _sigmoid(hgt_ref[...].astype(F32)) * live
        ext[HALO:HALO + t, :] = hg
        hg_ref[...] = hg
        acc = jnp.zeros((t, dc), F32) + b_ref[...]
        for j in range(CONV_W):
            acc = acc + w_ref[j:j + 1, :] * ext[pl.ds(HALO - (CONV_W - 1) + j, t), :]
        hc_ref[...] = acc
        mu = jnp.mean(acc, axis=-1, keepdims=True)
        xc = acc - mu
        rs = lax.rsqrt(jnp.mean(xc * xc, axis=-1, keepdims=True) + EPS)
        ln = xc * rs * g_ref[...] + be_ref[...]
        o_ref[...] = (ln * _sigmoid(ln)).astype(BF16)

    vec = pl.BlockSpec((1, dc), lambda b, i: (0, 0))
    row = pl.BlockSpec((t, dc), lambda b, i: (b * nt + i, 0))
    tile = lambda c: pl.BlockSpec((t, dc), lambda b, i: (b * nt + i, c))
    halo = lambda c: pl.BlockSpec((HALO, dc), lambda b, i: (jnp.maximum((b * nt + i) * hb - 1, 0), c))
    return pl.pallas_call(
        body, grid=(bl, nt),
        in_specs=[tile(acol), tile(acol + 1), halo(acol), halo(acol + 1),
                  pl.BlockSpec((HALO, dc), lambda b, i: (0, 0)), vec, vec, vec],
        out_specs=[row, row, row],
        out_shape=[_sds((n, dc), F32), _sds((n, dc), F32), _sds((n, dc), BF16)],
        scratch_shapes=[pltpu.VMEM((HALO + t, dc), F32)],
        compiler_params=_cp("parallel", "arbitrary"), name=name)(proj, proj, proj, proj, wdw, bdw, lng, lnb)


def _conv_bwd_ln(dco, hc, lng, lnb, *, tm, name):
    n, dc = hc.shape

    def body(d_ref, hc_ref, g_ref, be_ref, dhc_ref, dg_ref, db_ref):
        @pl.when(pl.program_id(0) == 0)
        def _():
            dg_ref[...] = jnp.zeros_like(dg_ref)
            db_ref[...] = jnp.zeros_like(db_ref)

        hcv = hc_ref[...]
        mu = jnp.mean(hcv, axis=-1, keepdims=True)
        xc = hcv - mu
        rs = lax.rsqrt(jnp.mean(xc * xc, axis=-1, keepdims=True) + EPS)
        xh = xc * rs
        ln = xh * g_ref[...] + be_ref[...]
        sg = _sigmoid(ln)
        dln = d_ref[...].astype(F32) * (sg * (1.0 + ln * (1.0 - sg)))
        db_ref[...] += jnp.sum(dln, axis=0, keepdims=True)
        dg_ref[...] += jnp.sum(dln * xh, axis=0, keepdims=True)
        dxh = dln * g_ref[...]
        dhc_ref[...] = rs * (dxh - jnp.mean(dxh, axis=-1, keepdims=True)
                             - xh * jnp.mean(dxh * xh, axis=-1, keepdims=True))

    vec = pl.BlockSpec((1, dc), lambda i: (0, 0))
    row = pl.BlockSpec((tm, dc), lambda i: (i, 0))
    return pl.pallas_call(
        body, grid=(n // tm,), in_specs=[row, row, vec, vec], out_specs=[row, vec, vec],
        out_shape=[_sds((n, dc), F32), _sds((1, dc), F32), _sds((1, dc), F32)],
        compiler_params=_cp("arbitrary"), name=name)(dco, hc, lng, lnb)


def _conv_bwd_dw(dhc, hg, proj, wdw, *, bl, s, t, acol, name):
    n = bl * s
    dc = wdw.shape[1]
    nt = s // t
    hb = t // HALO
    lastblk = n // HALO - 1

    def body(d_ref, dn_ref, hg_ref, hp_ref, za_ref, zg_ref, w_ref, dz_ref, dw_ref, dbias_ref, extd, exth):
        b = pl.program_id(0)
        i = pl.program_id(1)

        @pl.when((b == 0) & (i == 0))
        def _():
            dw_ref[...] = jnp.zeros_like(dw_ref)
            dbias_ref[...] = jnp.zeros_like(dbias_ref)

        dv = d_ref[...]
        extd[0:t, :] = dv
        extd[t:t + HALO, :] = dn_ref[...] * jnp.where(i < nt - 1, 1.0, 0.0)
        exth[0:HALO, :] = hp_ref[...] * jnp.where(i > 0, 1.0, 0.0)
        exth[HALO:HALO + t, :] = hg_ref[...]
        dbias_ref[...] += jnp.sum(dv, axis=0, keepdims=True)
        dhg = jnp.zeros((t, dc), F32)
        for j in range(CONV_W):
            dhg = dhg + w_ref[j:j + 1, :] * extd[pl.ds(CONV_W - 1 - j, t), :]
            dw_ref[j:j + 1, :] += jnp.sum(dv * exth[pl.ds(HALO - (CONV_W - 1) + j, t), :], axis=0, keepdims=True)
        za = za_ref[...].astype(F32)
        sg = _sigmoid(zg_ref[...].astype(F32))
        dz_ref[...] = jnp.concatenate([dhg * sg, dhg * za * sg * (1.0 - sg)], axis=1).astype(BF16)

    row = pl.BlockSpec((t, dc), lambda b, i: (b * nt + i, 0))
    nxt = pl.BlockSpec((HALO, dc), lambda b, i: (jnp.minimum((b * nt + i + 1) * hb, lastblk), 0))
    prv = pl.BlockSpec((HALO, dc), lambda b, i: (jnp.maximum((b * nt + i) * hb - 1, 0), 0))
    wsp = pl.BlockSpec((HALO, dc), lambda b, i: (0, 0))
    tile = lambda c: pl.BlockSpec((t, dc), lambda b, i: (b * nt + i, c))
    return pl.pallas_call(
        body, grid=(bl, nt),
        in_specs=[row, nxt, row, prv, tile(acol), tile(acol + 1), wsp],
        out_specs=[pl.BlockSpec((t, 2 * dc), lambda b, i: (b * nt + i, 0)), wsp,
                   pl.BlockSpec((1, dc), lambda b, i: (0, 0))],
        out_shape=[_sds((n, 2 * dc), BF16), _sds((HALO, dc), F32), _sds((1, dc), F32)],
        scratch_shapes=[pltpu.VMEM((t + HALO, dc), F32), pltpu.VMEM((HALO + t, dc), F32)],
        compiler_params=_cp("arbitrary", "arbitrary"), name=name)(dhc, dhc, hg, hg, proj, proj, wdw)


def _mix_out_fwd(x, brs, gl, bg, wbs, wout, l, *, tm, name):
    n, d = x.shape

    def body(x_ref, s_ref, a_ref, c_ref, g0, g1, g2, bg_ref, ws, wa, wc, wo, o_ref):
        merged = jnp.zeros((tm, d), F32)
        for k, (br, gr, w) in enumerate(((s_ref, g0, ws), (a_ref, g1, wa), (c_ref, g2, wc))):
            gate = _sigmoid(gr[...].astype(F32) + bg_ref[:, k * d:(k + 1) * d])
            merged = merged + gate * _dot(br[...], w[...])
        o_ref[...] = x_ref[...] + _dot(merged.astype(BF16), wo[...])

    row = lambda w: pl.BlockSpec((tm, w), lambda i: (i, 0))
    wsp = lambda a: pl.BlockSpec((None,) + a.shape[1:], lambda i: (l, 0, 0))
    gls = [pl.BlockSpec((tm, d), functools.partial(lambda k, i: (i, k), k)) for k in range(3)]
    return pl.pallas_call(
        body, grid=(n // tm,),
        in_specs=[row(d), *[row(b.shape[1]) for b in brs], *gls, pl.BlockSpec(bg.shape, lambda i: (0, 0)),
                  *[wsp(w) for w in wbs], wsp(wout)],
        out_specs=row(d), out_shape=_sds((n, d), F32),
        compiler_params=_cp("parallel"), name=name)(x, *brs, gl, gl, gl, bg, *wbs, wout)


def _mix_out_bwd(dx, brs, gl, bg, wbs, wout, l, nl, bufs, *, tm, name):
    n, d = dx.shape
    widths = [b.shape[1] for b in brs]

    def body(dx_ref, s_ref, a_ref, c_ref, g0, g1, g2, bg_ref, ws, wa, wc, wo, *rest):
        ds_ref, da_ref, dc_ref, dgl_ref, dbg_ref, dws, dwa, dwc, dwo = rest[-9:]

        @pl.when(pl.program_id(0) == 0)
        def _():
            for r in (dbg_ref, dws, dwa, dwc, dwo):
                r[...] = jnp.zeros_like(r)

        dxb = dx_ref[...].astype(BF16)
        dm = _dot_t1(dxb, wo[...])
        merged = jnp.zeros((tm, d), F32)
        for k, (br, gr, w, dbr, dw) in enumerate(((s_ref, g0, ws, ds_ref, dws), (a_ref, g1, wa, da_ref, dwa),
                                                   (c_ref, g2, wc, dc_ref, dwc))):
            gate = _sigmoid(gr[...].astype(F32) + bg_ref[:, k * d:(k + 1) * d])
            brv = br[...]
            wv = w[...]
            y = _dot(brv, wv)
            merged = merged + gate * y
            dyb = (dm * gate).astype(BF16)
            dbr[...] = _dot_t1(dyb, wv).astype(BF16)
            dw[...] += _dot_t0(brv, dyb)
            dgl = dm * y * gate * (1.0 - gate)
            dgl_ref[:, k * d:(k + 1) * d] = dgl.astype(BF16)
            dbg_ref[:, k * d:(k + 1) * d] += jnp.sum(dgl, axis=0, keepdims=True)
        dwo[...] += _dot_t0(merged.astype(BF16), dxb)

    row = lambda w: pl.BlockSpec((tm, w), lambda i: (i, 0))
    wsp = lambda shape: pl.BlockSpec((None,) + tuple(shape), lambda i: (l, 0, 0))
    gls = [pl.BlockSpec((tm, d), functools.partial(lambda k, i: (i, k), k)) for k in range(3)]
    slabs = [(w, d) for w in widths] + [(d, d)]
    n_in = 12
    extra = [] if bufs is None else list(bufs)
    aliases = {} if bufs is None else {n_in + k: 5 + k for k in range(4)}
    return pl.pallas_call(
        body, grid=(n // tm,),
        in_specs=[row(d), *[row(w) for w in widths], *gls, pl.BlockSpec(bg.shape, lambda i: (0, 0)),
                  *[wsp(w.shape[1:]) for w in wbs], wsp(wout.shape[1:]), *[_ANY for _ in extra]],
        out_specs=[*[row(w) for w in widths], row(3 * d), pl.BlockSpec((1, 3 * d), lambda i: (0, 0)),
                   *[wsp(sh) for sh in slabs]],
        out_shape=[*[_sds((n, w), BF16) for w in widths], _sds((n, 3 * d), BF16), _sds((1, 3 * d), F32),
                   *[_sds((nl,) + sh, F32) for sh in slabs]],
        input_output_aliases=aliases,
        compiler_params=_cp("arbitrary"), name=name)(dx, *brs, gl, gl, gl, bg, *wbs, wout, *extra)


def _adamw(w, g, m, v, *, name):
    r, c = w.shape
    tm = _tile(r, 256)
    c1 = 1.0 - ADAM_B1 ** ADAM_STEP
    c2 = 1.0 - ADAM_B2 ** ADAM_STEP

    def body(w_ref, g_ref, m_ref, v_ref, d_ref, nm_ref, nv_ref):
        gv = g_ref[...]
        mn = ADAM_B1 * m_ref[...] + (1.0 - ADAM_B1) * gv
        vn = ADAM_B2 * v_ref[...] + (1.0 - ADAM_B2) * (gv * gv)
        nm_ref[...] = mn
        nv_ref[...] = vn
        d_ref[...] = -ADAM_LR * ((mn / c1) / (jnp.sqrt(vn / c2) + ADAM_EPS) + ADAM_WD * w_ref[...])

    blk = pl.BlockSpec((tm, c), lambda i: (i, 0))
    return pl.pallas_call(
        body, grid=(r // tm,), in_specs=[blk] * 4, out_specs=[blk] * 3,
        out_shape=[_sds((r, c), F32)] * 3, compiler_params=_cp("parallel"), name=name)(w, g, m, v)


def _add_sibling(g, recv, lyr, c_idx, *, name):
    _, a, b = g.shape
    ta = _tile(a, 256)

    def body(c_ref, g_ref, r_ref, o_ref):
        o_ref[...] = (g_ref[...] + r_ref[...]).astype(BF16)

    row = lambda i, cr: jnp.where(cr[0] == lyr, i, 0)
    return pl.pallas_call(
        body,
        grid_spec=pltpu.PrefetchScalarGridSpec(
            num_scalar_prefetch=1, grid=(a // ta,),
            in_specs=[pl.BlockSpec((None, ta, b), lambda i, cr: (lyr, row(i, cr), 0)),
                      pl.BlockSpec((ta, b), lambda i, cr: (row(i, cr), 0))],
            out_specs=pl.BlockSpec((ta, b), lambda i, cr: (row(i, cr), 0))),
        out_shape=_sds((a, b), BF16), compiler_params=_cp("arbitrary"), name=name)(c_idx, g, recv)


def _add_chips(rsum, parts, axis, s_idx, c_idx, lyr, buf, *, name):
    _, a, b = parts.shape
    ta = _tile(a, 256)
    na = a // ta

    def body(s_ref, c_ref, own_ref, p0, p1, p2, p3, *rest):
        o_ref = rest[-1]
        own = own_ref[...].astype(F32)
        terms = [jnp.where(s_ref[0] == s, own, p[...].astype(F32)) for s, p in enumerate((p0, p1, p2, p3))]
        o_ref[...] = ((terms[0] + terms[1]) + terms[2]) + terms[3]

    row = lambda i, cr: jnp.where(cr[0] == lyr, i, 0)
    own_spec = (pl.BlockSpec((ta, b), lambda i, sr, cr: (sr[0] * na + row(i, cr), 0)) if axis == 1
                else pl.BlockSpec((ta, b), lambda i, sr, cr: (row(i, cr), sr[0])))
    part_spec = lambda s: pl.BlockSpec((None, ta, b),
                                       lambda i, sr, cr: (jnp.where(sr[0] == s, s ^ 1, s), row(i, cr), 0))
    extra, extra_specs, out_shape, aliases = _slab_out(2, lyr, (a, b), buf, 7)
    return pl.pallas_call(
        body,
        grid_spec=pltpu.PrefetchScalarGridSpec(
            num_scalar_prefetch=2, grid=(na,),
            in_specs=[own_spec] + [part_spec(s) for s in range(N_CHIPS)] + extra_specs,
            out_specs=pl.BlockSpec((None, ta, b), lambda i, sr, cr: (lyr, row(i, cr), 0))),
        out_shape=out_shape, input_output_aliases=aliases, compiler_params=_cp("arbitrary"), name=name)(
            s_idx, c_idx, rsum, parts, parts, parts, parts, *extra)


def _place_shard(wloc, axis, s_idx, *, name):
    nl, a, b = wloc.shape
    ta = _tile(a, 256)
    na = a // ta
    full = (nl, a * N_CHIPS, b) if axis == 1 else (nl, a, b * N_CHIPS)

    def body(sc_ref, w_ref, o_ref):
        o_ref[...] = w_ref[...].astype(BF16)

    out_spec = (pl.BlockSpec((None, ta, b), lambda l, i, sc: (l, sc[0] * na + i, 0)) if axis == 1
                else pl.BlockSpec((None, ta, b), lambda l, i, sc: (l, i, sc[0])))
    return pl.pallas_call(
        body,
        grid_spec=pltpu.PrefetchScalarGridSpec(
            num_scalar_prefetch=1, grid=(nl, na),
            in_specs=[pl.BlockSpec((None, ta, b), lambda l, i, sc: (l, i, 0))], out_specs=out_spec),
        out_shape=_sds(full, BF16), compiler_params=_cp("parallel", "parallel"), name=name)(s_idx, wloc)


def _blockdiag(w):
    g, r, c = w.shape
    eye = jnp.eye(g, dtype=w.dtype)
    return (w[:, :, None, :] * eye[:, None, :, None]).reshape(g * r, g * c)


def _s5_prep(lre, lim, log_dt, b_re, b_im, c_re, c_im, d_skip):
    lr = jnp.minimum(lre, -1e-4)
    li = lim
    dt = jnp.exp(log_dt)[:, None]
    mag = jnp.exp(lr * dt)
    ar = mag * jnp.cos(li * dt)
    ai = mag * jnp.sin(li * dt)
    den = lr * lr + li * li
    coef_r = ((ar - 1.0) * lr + ai * li) / den
    coef_i = (ai * lr - (ar - 1.0) * li) / den
    bbar_r = coef_r[..., None] * b_re - coef_i[..., None] * b_im
    bbar_i = coef_r[..., None] * b_im + coef_i[..., None] * b_re
    a = jnp.stack([ar.reshape(-1), ai.reshape(-1)])
    return dict(
        a=a,
        bblk_r=_blockdiag(bbar_r.transpose(0, 2, 1)), bblk_i=_blockdiag(bbar_i.transpose(0, 2, 1)),
        cblk_r=_blockdiag(c_re.transpose(0, 2, 1)), cblk_in=_blockdiag(-c_im.transpose(0, 2, 1)),
        d=d_skip.reshape(1, -1))


def _s5_powers(a, nlog):
    ar, ai = a[0], a[1]
    pr, pi = ar, ai
    rows = []
    for _ in range(nlog):
        rows += [pr, pi]
        pr, pi = pr * pr - pi * pi, 2.0 * pr * pi
    qr, qi = [ar], [ai]
    for _ in range(SUBLANES - 1):
        qr, qi = qr + [qr[-1] * ar - qi[-1] * ai], qi + [qr[-1] * ai + qi[-1] * ar]
    p8 = jnp.stack(qr + qi)
    q8 = jnp.stack(qr[::-1] + [-v for v in qi[::-1]])
    return jnp.stack(rows), p8, q8


def _bias_table(rel_bias):
    h = rel_bias.shape[0]
    tq, tw = ATT_TQ, 3 * ATT_TQ
    n_hi = tw - 1 - MAX_REL + 1
    n_lo = tq + tw - 1 - n_hi - (2 * MAX_REL - 1)
    fr = jnp.concatenate([
        jnp.broadcast_to(rel_bias[:, 2 * MAX_REL:], (h, n_hi)),
        jnp.flip(rel_bias[:, 1:2 * MAX_REL], axis=1),
        jnp.broadcast_to(rel_bias[:, :1], (h, n_lo)),
        jnp.zeros((h, 1), rel_bias.dtype)], axis=1)
    ln = tq + tw
    flat = jnp.broadcast_to(fr[:, None, :], (h, tq, ln)).reshape(h, tq * ln)[:, :tq * (ln - 1)]
    tab = flat.reshape(h, tq, ln - 1)[:, :, tq - 1:tq - 1 + tw]
    qc = np.arange(tq)[:, None] // CHUNK + N_LEFT
    kc = np.arange(tw)[None, :] // CHUNK
    band = (kc <= qc) & (kc >= qc - N_LEFT)
    return jnp.where(jnp.asarray(band)[None], tab, NEG)


def _small_prep(w, l):
    g, p = w["s5_lambda_re"].shape[1:]
    b_shape, c_shape = (g, p, -1), (g, -1, p)
    sp = _s5_prep(w["s5_lambda_re"][l], w["s5_lambda_im"][l], w["s5_log_dt"][l], w["s5_b_re"][l].reshape(b_shape),
                  w["s5_b_im"][l].reshape(b_shape), w["s5_c_re"][l].reshape(c_shape), w["s5_c_im"][l].reshape(c_shape),
                  w["s5_d"][l])
    return sp, _bias_table(w["attn_rel_bias"][l])


_PREP_KEYS = ("s5_lambda_re", "s5_lambda_im", "s5_log_dt", "s5_b_re", "s5_b_im", "s5_c_re", "s5_c_im", "s5_d",
              "attn_rel_bias")
_BIG_KEYS = {"ffn1_w_up": 2, "ffn1_w_down": 1, "w_in": 2, "s5_w_glu": 2, "w_br_s5": 2, "w_br_attn": 2,
             "w_br_conv": 2, "w_out": 1, "ffn2_w_up": 2, "ffn2_w_down": 1}
_SMALL_KEYS = ("ffn1_norm", "mix_norm", "b_gate", "s5_lambda_re", "s5_lambda_im", "s5_log_dt", "s5_b_re", "s5_b_im",
               "s5_c_re", "s5_c_im", "s5_d", "attn_q_gain", "attn_k_gain", "attn_rel_bias", "conv_w_dw", "conv_b_dw",
               "conv_ln_g", "conv_ln_b", "ffn2_norm")
_WEIGHTS = ("ffn1_norm", "ffn1_w_up", "ffn1_w_down", "mix_norm", "w_in", "b_gate", "s5_lambda_re", "s5_lambda_im",
            "s5_log_dt", "s5_b_re", "s5_b_im", "s5_c_re", "s5_c_im", "s5_d", "s5_w_glu", "w_br_s5", "attn_q_gain",
            "attn_k_gain", "attn_rel_bias", "w_br_attn", "conv_w_dw", "conv_b_dw", "conv_ln_g", "conv_ln_b",
            "w_br_conv", "w_out", "ffn2_norm", "ffn2_w_up", "ffn2_w_down")


def _local_step(x3, target3, w, rs=None, gather=None):
    w = dict(w)
    bl, s, d = x3.shape
    nl = w["ffn1_norm"].shape[0]
    dff = w["ffn1_w_down"].shape[1]
    ds5 = w["s5_d"].shape[1]
    datt = w["w_br_attn"].shape[1]
    dc = w["conv_b_dw"].shape[1]
    n = bl * s
    x = x3.reshape(n, d)
    target = target3.reshape(n, d)
    tm = _tile(n, 512)
    tml = _tile(n, 1024)
    tmix = _tile(n, 256)
    ts5 = 256
    tconv = _tile(s, 512)
    tff = dff // 2
    ma = ds5 + 3 * datt + 2 * dc
    tna = ma // 3
    assert (3 * d) % tna == 0 and dff % 2 == 0
    qoff = ds5 // LANES
    koff = (ds5 + datt) // LANES
    acol = (ds5 + 3 * datt) // dc
    wbs = lambda: (w["w_br_s5"], w["w_br_attn"], w["w_br_conv"])

    def host(tag, l):
        if gather is None or l != 0:
            return None
        keys, kaxes, lyr = gather[tag]
        return [w[k] for k in keys], kaxes, lyr

    def hosted(tag, l, arrays):
        if gather is not None and l == 0:
            w.update(zip(gather[tag][0], arrays))

    saved = []
    for l in range(nl):
        (sp, bias), prep_vjp = jax.vjp(lambda ww: _small_prep(ww, l), {k: w[k] for k in _PREP_KEYS})
        spb = dict(sp)
        spb["pw"], spb["p8"], spb["q8"] = _s5_powers(lax.stop_gradient(sp["a"]), int(math.log2(ts5)))
        for k in ("bblk_r", "bblk_i", "cblk_r", "cblk_in"):
            spb[k] = sp[k].astype(BF16)
        g1 = w["ffn1_norm"][l][None]
        g2 = w["ffn2_norm"][l][None]
        gm = w["mix_norm"][l][None]
        gq2 = jnp.tile(w["attn_q_gain"][l], 2)[None]
        gk2 = jnp.tile(w["attn_k_gain"][l], 2)[None]
        wdw = jnp.pad(w["conv_w_dw"][l], ((0, HALO - CONV_W), (0, 0)))
        bdw, lng, lnb = w["conv_b_dw"][l][None], w["conv_ln_g"][l][None], w["conv_ln_b"][l][None]
        bg = w["b_gate"][l][None]

        x0 = x
        h1, ab1, *got = _norm_mm(x0, g1, w["ffn1_w_up"], l, tm=tml, tn=tff, ntiles=4, pieces=2, transposed=False,
                                 name=f"ffn1_up_{l}", comm=host("ffn1_up", l))
        hosted("ffn1_up", l, got)
        x1 = _ffn_down(ab1, w["ffn1_w_down"], l, x0, tm=tm, tk=tff, name=f"ffn1_down_{l}")
        h2, pa = _norm_mm(x1, gm, w["w_in"], l, tm=tml, tn=tna, ntiles=3, pieces=1, transposed=True, name=f"win_a_{l}")
        pa = pa[0]
        gl = _mm_t(h2, w["w_in"], l, tm=tml, tn=tna, off=3, ntiles=3 * d // tna, name=f"win_g_{l}")
        xr, xi, yp, zg, s5o = _s5_fwd(pa, spb, w["s5_w_glu"], l, bl=bl, s=s, t=ts5, name=f"s5_fwd_{l}")
        atto, got = _attn_fwd(pa, gq2, gk2, bias, bl=bl, s=s, datt=datt, qoff=qoff, name=f"attn_fwd_{l}",
                              comm=host("attn_fwd", l))
        hosted("attn_fwd", l, got)
        hg, hc, convo = _conv_fwd(pa, wdw, bdw, lng, lnb, bl=bl, s=s, t=tconv, acol=acol, name=f"conv_fwd_{l}")
        brs = (s5o, atto, convo)
        x2 = _mix_out_fwd(x1, brs, gl, bg, wbs(), w["w_out"], l, tm=tmix, name=f"mix_fwd_{l}")
        h3, ab2, *got = _norm_mm(x2, g2, w["ffn2_w_up"], l, tm=tml, tn=tff, ntiles=4, pieces=2, transposed=False,
                                 name=f"ffn2_up_{l}", comm=host("ffn2_up", l))
        hosted("ffn2_up", l, got)
        x = _ffn_down(ab2, w["ffn2_w_down"], l, x2, tm=tm, tk=tff, name=f"ffn2_down_{l}")
        saved.append(dict(spb=spb, bias=bias, prep_vjp=prep_vjp, g1=g1, g2=g2, gm=gm, gq2=gq2, gk2=gk2,
                          wdw=wdw, lng=lng, lnb=lnb, bg=bg, x0=x0, h1=h1, ab1=ab1, x1=x1, h2=h2, pa=pa, gl=gl,
                          xr=xr, xi=xi, yp=yp, zg=zg, hg=hg, hc=hc, brs=brs, x2=x2, h3=h3, ab2=ab2))

    dx, lsum = _loss_grad(x, target, tm=tm, name="loss")
    loss_part = 0.5 * jnp.sum(lsum) / d

    big = {k: None for k in _BIG_KEYS}
    small = {k: [None] * nl for k in _SMALL_KEYS}
    hooks = {"pending": None}
    assert rs is None or nl == 2
    for l in reversed(range(nl)):
        sv = saved[l]

        def ffn_bwd(dx, xin, h, ab, g, tag):
            wu, wd = w[tag + "_w_up"], w[tag + "_w_down"]
            dab, big[tag + "_w_down"] = _ffn_dact(dx, wd, l, ab, nl, big[tag + "_w_down"], tm=tm, tk=tff,
                                                  name=f"{tag}_dact_{l}")
            big[tag + "_w_up"] = _mm_tn(h[None], dab, l, nl, big[tag + "_w_up"], ta=d, tb=tff, tk=tml,
                                        name=f"{tag}_dwu_{l}")
            comm = None
            if rs is not None and l == 0:
                comm = ([big[k] for k in _BIG_KEYS], 1 if tag == "ffn2" else 0)
            dxo, dg, *recv = _ffn_dx(dab, wu, l, xin, g, dx, tm=tml, tk=tff, name=f"{tag}_dx_{l}", comm=comm)
            small[tag + "_norm"][l] = dg[0]
            if comm is not None and tag == "ffn2":
                hooks["pending"] = rs.sums(big, recv, 1)
            elif comm is not None:
                rs.recv0 = recv
            return dxo

        dx = ffn_bwd(dx, sv["x2"], sv["h3"], sv["ab2"], sv["g2"], "ffn2")

        mix_keys = ("w_br_s5", "w_br_attn", "w_br_conv", "w_out")
        bufs = None if big["w_out"] is None else [big[k] for k in mix_keys]
        ds5o, datto, dconvo, dgl, dbg, *dws = _mix_out_bwd(
            dx, sv["brs"], sv["gl"], sv["bg"], wbs(), w["w_out"], l, nl, bufs, tm=tmix, name=f"mix_bwd_{l}")
        small["b_gate"][l] = dbg[0]
        big.update(zip(mix_keys, dws))

        dhc, dlng, dlnb = _conv_bwd_ln(dconvo, sv["hc"], sv["lng"], sv["lnb"], tm=tm, name=f"conv_bwd_ln_{l}")
        dz, dwdw, dbdw = _conv_bwd_dw(dhc, sv["hg"], sv["pa"], sv["wdw"], bl=bl, s=s, t=tconv, acol=acol,
                                      name=f"conv_bwd_dw_{l}")
        small["conv_w_dw"][l] = dwdw[:CONV_W]
        small["conv_b_dw"][l], small["conv_ln_g"][l], small["conv_ln_b"][l] = dbdw[0], dlng[0], dlnb[0]

        comm = hooks["pending"] if l == 0 else None
        dq, dkn, dvw, dbias, dgq, *hosted = _attn_bwd(datto, sv["pa"], sv["gq2"], sv["gk2"], sv["bias"], bl=bl, s=s,
                                                      datt=datt, qoff=qoff, name=f"attn_bwd_{l}", comm=comm)
        if comm is not None:
            rs.parts = hosted
        dk, dv, dgk = _attn_kv_bwd(dkn, dvw, sv["pa"], sv["gk2"], bl=bl, s=s, datt=datt, tm=_tile(s, 512), koff=koff,
                                   name=f"attn_kv_bwd_{l}")
        small["attn_q_gain"][l] = jnp.sum(dgq.reshape(-1, HEAD_DIM), axis=0)
        small["attn_k_gain"][l] = jnp.sum(dgk.reshape(-1, HEAD_DIM), axis=0)

        du, dd, dcr, dci, dbr, dbi, da, big["s5_w_glu"] = _s5_bwd(
            ds5o, sv["yp"], sv["zg"], sv["xr"], sv["xi"], sv["pa"], sv["spb"], w["s5_w_glu"], l, nl, big["s5_w_glu"],
            bl=bl, s=s, t=ts5, name=f"s5_bwd_{l}")
        prep_ct = (dict(a=da, bblk_r=dbr, bblk_i=dbi, cblk_r=dcr, cblk_in=dci, d=dd), jnp.sum(dbias, axis=0))
        (dprep,) = sv["prep_vjp"](prep_ct)
        for k in _PREP_KEYS:
            small[k][l] = dprep[k][l]

        dpa = jnp.concatenate([du, dq, dk, dv, dz], axis=1)
        big["w_in"] = _dwin_t(dpa, dgl, sv["h2"], l, nl, big["w_in"], ta=tna, tk=tml, name=f"dwin_{l}")
        dx, dgm = _mix_dx(dpa, dgl, w["w_in"], l, sv["x1"], sv["gm"], dx, tm=tml, tk=tna, name=f"mix_dx_{l}")
        small["mix_norm"][l] = dgm[0]

        dx = ffn_bwd(dx, sv["x0"], sv["h1"], sv["ab1"], sv["g1"], "ffn1")

    small = {k: jnp.stack(v) for k, v in small.items()}
    return loss_part, dx.reshape(bl, s, d), big, small


def _place():
    x, y, c = lax.axis_index("x"), lax.axis_index("y"), lax.axis_index("c")
    chips = [(1 - x, y), (x, 1 - y), (1 - x, 1 - y)]
    return x, y, c, chips


def _remote(src, dst, send_sems, recv_sems, k, dev):
    return pltpu.make_async_remote_copy(src_ref=src, dst_ref=dst, send_sem=send_sems.at[k], recv_sem=recv_sems.at[k],
                                        device_id=dev, device_id_type=MESH)


def _window(ref, lead, s, axis, blk):
    if axis == 1:
        sl = (pl.ds(pl.multiple_of(s * blk, 16), blk), slice(None))
    else:
        sl = (slice(None), pl.ds(pl.multiple_of(s * blk, LANES), blk))
    return ref.at[sl] if lead is None else ref.at[(lead,) + sl]


def _gather_ops(bufs, axes, send_sems, recv_sems, lyr):
    x, y, c, chips = _place()
    s_me = 2 * x + y
    sibling = (x, y, 1 - lyr)
    nw = len(bufs)
    blks = [f.shape[ax] // N_CHIPS for f, ax in zip(bufs, axes)]
    win = lambda i, s: _window(bufs[i], lyr, s, axes[i], blks[i])
    pairs = [(i, j, cx, cy) for i in range(nw) for j, (cx, cy) in enumerate(chips)]
    sends = [_remote(win(i, s_me), win(i, s_me), send_sems, recv_sems, 6 * i + j, (cx, cy, lyr)) for i, j, cx, cy in pairs]
    passed = [_remote(win(i, 2 * cx + cy), win(i, 2 * cx + cy), send_sems, recv_sems, 6 * i + 3 + j, sibling)
              for i, j, cx, cy in pairs]

    def start():
        @pl.when(c == lyr)
        def _():
            for cp in sends:
                cp.start()

    def forward():
        @pl.when(c == lyr)
        def _():
            for (i, j, cx, cy), fw in zip(pairs, passed):
                piece = win(i, 2 * cx + cy)
                _remote(piece, piece, send_sems, recv_sems, 6 * i + j, (cx, cy, lyr)).wait_recv()
                fw.start()

    def wait():
        @pl.when(c == lyr)
        def _():
            for cp in sends + passed:
                cp.wait_send()

        @pl.when(c != lyr)
        def _():
            for fw in passed:
                fw.wait_recv()

    return start, forward, wait


def _all_gather_weights(fulls, axes, taps):
    nw = len(fulls)

    def body(*refs):
        taps_in = refs[nw]
        outs, taps_out = refs[nw + 1:2 * nw + 1], refs[2 * nw + 1]
        send_sems, recv_sems, tap_send, tap_recv, local_sem = refs[-5:]
        x, y, c, chips = _place()
        s_me = 2 * x + y
        start, forward, wait = _gather_ops(outs, axes, send_sems, recv_sems, 0)
        own_taps = pltpu.make_async_copy(taps_in, taps_out.at[s_me], local_sem)
        own_taps.start()
        tap_sends = [_remote(taps_in, taps_out.at[s_me], tap_send, tap_recv, j, (cx, cy, c))
                     for j, (cx, cy) in enumerate(chips)]
        for cp in tap_sends:
            cp.start()
        start()
        forward()
        wait()
        for j, (cx, cy) in enumerate(chips):
            slab = taps_out.at[2 * cx + cy]
            _remote(slab, slab, tap_send, tap_recv, j, (cx, cy, c)).wait_recv()
        for cp in tap_sends:
            cp.wait_send()
        own_taps.wait()

    return pl.pallas_call(
        body, in_specs=[_ANY] * (nw + 1), out_specs=[_ANY] * (nw + 1),
        out_shape=[_sds(f.shape, f.dtype) for f in fulls] + [_sds((N_CHIPS,) + taps.shape, taps.dtype)],
        input_output_aliases={i: i for i in range(nw)},
        scratch_shapes=[pltpu.SemaphoreType.DMA((6 * nw,)), pltpu.SemaphoreType.DMA((6 * nw,)),
                        pltpu.SemaphoreType.DMA((3,)), pltpu.SemaphoreType.DMA((3,)), pltpu.SemaphoreType.DMA],
        name="all_gather_weights")(*fulls, taps)


def _hosted_gather(comm, n_in, n_out):
    if comm is None:
        return [], [], [], [], [], {}
    bufs = list(comm[0])
    nw = len(bufs)
    return (bufs, [_ANY] * nw, [_ANY] * nw, [_sds(f.shape, f.dtype) for f in bufs],
            [pltpu.SemaphoreType.DMA((6 * nw,)), pltpu.SemaphoreType.DMA((6 * nw,))],
            {n_in + k: n_out + k for k in range(nw)})


def _hosted_gather_steps(comm, out_refs, sems, step, total):
    start, forward, wait = _gather_ops(out_refs, comm[1], sems[0], sems[1], comm[2])

    @pl.when(step == 0)
    def _():
        start()

    def finish():
        @pl.when(step == (3 * total) // 4)
        def _():
            forward()

        @pl.when(step == total - 1)
        def _():
            wait()

    return finish


def _swap_ops(ins, outs, send_sems, recv_sems, lyr):
    x, y, c, _ = _place()
    cps = [_remote(ins[i].at[lyr], outs[i], send_sems, recv_sems, i, (x, y, lyr)) for i in range(len(ins))]

    def start():
        @pl.when(c != lyr)
        def _():
            for cp in cps:
                cp.start()

    def wait():
        @pl.when(c != lyr)
        def _():
            for cp in cps:
                cp.wait_send()

        @pl.when(c == lyr)
        def _():
            for cp in cps:
                cp.wait_recv()

    return start, wait


def _exchange_ops(ins, outs, send_sems, recv_sems, axes, lyr):
    x, y, c, chips = _place()
    s_me = 2 * x + y
    nw = len(ins)
    blks = [r.shape[ax - 1] // N_CHIPS for r, ax in zip(ins, axes)]
    win = lambda i, s: _window(ins[i], None, s, axes[i], blks[i])
    sends = [_remote(win(i, 2 * cx + cy), outs[i].at[s_me], send_sems, recv_sems, 3 * i + j, (cx, cy, lyr))
             for i in range(nw) for j, (cx, cy) in enumerate(chips)]

    def start():
        @pl.when(c == lyr)
        def _():
            for cp in sends:
                cp.start()

    def wait():
        @pl.when(c == lyr)
        def _():
            for i in range(nw):
                for j, (cx, cy) in enumerate(chips):
                    slab = outs[i].at[2 * cx + cy]
                    _remote(slab, slab, send_sems, recv_sems, 3 * i + j, (cx, cy, lyr)).wait_recv()
            for cp in sends:
                cp.wait_send()

    return start, wait


def _parts_shapes(rsums, axes):
    shard = [tuple(dim // N_CHIPS if i == ax - 1 else dim for i, dim in enumerate(r.shape)) for r, ax in zip(rsums, axes)]
    return [_sds((N_CHIPS,) + sh, r.dtype) for sh, r in zip(shard, rsums)]


def _rs_exchange(rsums, axes, lyr):
    nw = len(rsums)

    def body(*refs):
        start, wait = _exchange_ops(refs[:nw], refs[nw:2 * nw], refs[-2], refs[-1], axes, lyr)
        start()
        wait()

    return pl.pallas_call(
        body, in_specs=[_ANY] * nw, out_specs=[_ANY] * nw, out_shape=_parts_shapes(rsums, axes),
        scratch_shapes=[pltpu.SemaphoreType.DMA((3 * nw,)), pltpu.SemaphoreType.DMA((3 * nw,))],
        name=f"rs_exchange_l{lyr}")(*rsums)


def _rs_join(ts):
    nw = len(ts)

    def body(*refs):
        outs = refs[nw:2 * nw]
        send_sems, recv_sems = refs[-2:]
        x, y, c, _ = _place()
        sends = [_remote(outs[i].at[c], outs[i].at[c], send_sems, recv_sems, i, (x, y, 1 - c)) for i in range(nw)]
        for cp in sends:
            cp.start()
        for i in range(nw):
            slab = outs[i].at[1 - c]
            _remote(slab, slab, send_sems, recv_sems, i, (x, y, 1 - c)).wait_recv()
        for cp in sends:
            cp.wait_send()

    return pl.pallas_call(
        body, in_specs=[_ANY] * nw, out_specs=[_ANY] * nw, out_shape=[_sds(t.shape, t.dtype) for t in ts],
        input_output_aliases={i: i for i in range(nw)},
        scratch_shapes=[pltpu.SemaphoreType.DMA((nw,)), pltpu.SemaphoreType.DMA((nw,))],
        name="rs_join_layers")(*ts)


def _all_reduce_small(arrs):
    na = len(arrs)
    nd = 8

    def body(*refs):
        ins, outs, recvs = refs[:na], refs[na:2 * na], refs[2 * na:3 * na]
        send_sems, recv_sems = refs[-2:]
        x, y, c, _ = _place()
        me = 4 * x + 2 * y + c
        for i in range(na):
            recvs[i][0] = ins[i][...]
        cps = []
        for rel in range(1, nd):
            dev = (1 - x if rel & 4 else x, 1 - y if rel & 2 else y, 1 - c if rel & 1 else c)
            for i in range(na):
                cp = _remote(ins[i], recvs[i].at[rel], send_sems, recv_sems, (rel - 1) * na + i, dev)
                cp.start()
                cps.append(cp)
        for rel in range(1, nd):
            for i in range(na):
                _remote(ins[i], recvs[i].at[rel], send_sems, recv_sems, (rel - 1) * na + i, (x, y, c)).wait_recv()
        for i in range(na):
            acc = recvs[i][me]
            for dv in range(1, nd):
                acc = acc + recvs[i][lax.bitwise_xor(me, dv)]
            outs[i][...] = acc
        for cp in cps:
            cp.wait_send()

    vm = pl.BlockSpec(memory_space=pltpu.VMEM)
    nsem = (nd - 1) * na
    return pl.pallas_call(
        body, in_specs=[vm] * na, out_specs=[vm] * na, out_shape=[_sds(t.shape, F32) for t in arrs],
        scratch_shapes=[pltpu.VMEM((nd,) + t.shape, F32) for t in arrs]
        + [pltpu.SemaphoreType.DMA((nsem,)), pltpu.SemaphoreType.DMA((nsem,))],
        compiler_params=pltpu.CompilerParams(vmem_limit_bytes=VMEM_LIMIT), name="all_reduce_small")(*arrs)


def _adamw_small(ws, gs, ms, vs):
    na = len(ws)
    c1 = 1.0 - ADAM_B1 ** ADAM_STEP
    c2 = 1.0 - ADAM_B2 ** ADAM_STEP

    def body(*refs):
        w_r, g_r, m_r, v_r = (refs[k * na:(k + 1) * na] for k in range(4))
        d_r, nm_r, nv_r = (refs[(4 + k) * na:(5 + k) * na] for k in range(3))
        for i in range(na):
            gv = g_r[i][...]
            mn = ADAM_B1 * m_r[i][...] + (1.0 - ADAM_B1) * gv
            vn = ADAM_B2 * v_r[i][...] + (1.0 - ADAM_B2) * (gv * gv)
            nm_r[i][...] = mn
            nv_r[i][...] = vn
            d_r[i][...] = -ADAM_LR * ((mn / c1) / (jnp.sqrt(vn / c2) + ADAM_EPS) + ADAM_WD * w_r[i][...])

    vm = pl.BlockSpec(memory_space=pltpu.VMEM)
    res = pl.pallas_call(
        body, in_specs=[vm] * (4 * na), out_specs=[vm] * (3 * na), out_shape=[_sds(t.shape, F32) for t in ws] * 3,
        compiler_params=pltpu.CompilerParams(vmem_limit_bytes=VMEM_LIMIT), name="adamw_small")(*ws, *gs, *ms, *vs)
    return res[:na], res[na:2 * na], res[2 * na:]


def kernel(x, ffn1_norm, ffn1_w_up, ffn1_w_down, mix_norm, w_in, b_gate, s5_lambda_re, s5_lambda_im, s5_log_dt, s5_b_re, s5_b_im, s5_c_re, s5_c_im, s5_d, s5_w_glu, w_br_s5, attn_q_gain, attn_k_gain, attn_rel_bias, w_br_attn, conv_w_dw, conv_b_dw, conv_ln_g, conv_ln_b, w_br_conv, w_out, ffn2_norm, ffn2_w_up, ffn2_w_down, loss_target, m_ffn1_norm, m_ffn1_w_up, m_ffn1_w_down, m_mix_norm, m_w_in, m_b_gate, m_s5_lambda_re, m_s5_lambda_im, m_s5_log_dt, m_s5_b_re, m_s5_b_im, m_s5_c_re, m_s5_c_im, m_s5_d, m_s5_w_glu, m_w_br_s5, m_attn_q_gain, m_attn_k_gain, m_attn_rel_bias, m_w_br_attn, m_conv_w_dw, m_conv_b_dw, m_conv_ln_g, m_conv_ln_b, m_w_br_conv, m_w_out, m_ffn2_norm, m_ffn2_w_up, m_ffn2_w_down, v_ffn1_norm, v_ffn1_w_up, v_ffn1_w_down, v_mix_norm, v_w_in, v_b_gate, v_s5_lambda_re, v_s5_lambda_im, v_s5_log_dt, v_s5_b_re, v_s5_b_im, v_s5_c_re, v_s5_c_im, v_s5_d, v_s5_w_glu, v_w_br_s5, v_attn_q_gain, v_attn_k_gain, v_attn_rel_bias, v_w_br_attn, v_conv_w_dw, v_conv_b_dw, v_conv_ln_g, v_conv_ln_b, v_w_br_conv, v_w_out, v_ffn2_norm, v_ffn2_w_up, v_ffn2_w_down):
    a = dict(locals())
    xi, yi, ci = lax.axis_index("x"), lax.axis_index("y"), lax.axis_index("c")
    s_me = 2 * xi + yi
    big_keys = list(_BIG_KEYS)
    axes = [1 if k == "w_in" else _BIG_KEYS[k] for k in big_keys]

    s_idx = s_me.astype(jnp.int32).reshape(1)
    c_idx = ci.astype(jnp.int32).reshape(1)
    placed = {k: _place_shard(jnp.swapaxes(a[k], 1, 2).astype(BF16) if k == "w_in" else a[k], ax, s_idx,
                              name=f"place_{k}") for k, ax in zip(big_keys, axes)}
    axis_of = dict(zip(big_keys, axes))
    first = ["ffn1_w_up", "ffn1_w_down"]
    *fulls, taps = _all_gather_weights([placed[k] for k in first], [axis_of[k] for k in first], a["conv_w_dw"])
    placed.update(zip(first, fulls))
    flat = lambda t: t.reshape(t.shape[0], t.shape[1], -1) if t.ndim == 4 else t
    w = {k: flat(a[k]) for k in _WEIGHTS}
    w.update(placed)
    w["conv_w_dw"] = jnp.moveaxis(taps, 0, 2).reshape(taps.shape[1], taps.shape[2], -1)

    class _ReduceScatter:
        parts = recv0 = rsums1 = None

        def sums(self, big, recv, lyr):
            rsums = [_add_sibling(big[k], r, lyr, c_idx, name=f"rs_add_sibling_{k}_l{lyr}")
                     for r, k in zip(recv, big_keys)]
            if lyr == 1:
                self.rsums1 = rsums
            return rsums, axes, lyr

    rs = _ReduceScatter()
    early = ("ffn1_w_up", "ffn1_w_down", "w_in")
    sets = {"ffn1_up": ([k for k in big_keys if k not in first], 0),
            "attn_fwd": ([k for k in big_keys if k in early], 1),
            "ffn2_up": ([k for k in big_keys if k not in early], 1)}
    gather = {tag: (keys, [axis_of[k] for k in keys], lyr) for tag, (keys, lyr) in sets.items()}
    loss_part, grad_x, gbig, gsmall = _local_step(a["x"], a["loss_target"], w, rs, gather)
    loss = lax.psum(loss_part, ("x", "y", "c"))

    rsums0 = rs.sums(gbig, rs.recv0, 0)[0]
    mine = [None] * len(big_keys)
    for lyr, rsums, parts in ((1, rs.rsums1, rs.parts), (0, rsums0, _rs_exchange(rsums0, axes, 0))):
        mine = [_add_chips(r, p, ax, s_idx, c_idx, lyr, buf, name=f"rs_add_chips_{k}_l{lyr}")
                for r, p, ax, buf, k in zip(rsums, parts, axes, mine, big_keys)]
    gb = dict(zip(big_keys, _rs_join(mine)))
    gb["w_in"] = jnp.swapaxes(gb["w_in"], 1, 2)

    small_keys = list(_SMALL_KEYS)
    gs = dict(zip(small_keys, _all_reduce_small([gsmall[k] for k in small_keys])))
    blk = a["conv_w_dw"].shape[2]
    gs["conv_w_dw"] = lax.dynamic_slice_in_dim(gs["conv_w_dw"], s_me * blk, blk, axis=2)

    delta, new_m, new_v = {}, {}, {}
    for k in big_keys:
        shp = a[k].shape
        two_d = lambda t: t.reshape(-1, shp[-1])
        d_, m_, v_ = _adamw(two_d(a[k]), two_d(gb[k]), two_d(a["m_" + k]), two_d(a["v_" + k]), name=f"adamw_{k}")
        delta[k], new_m[k], new_v[k] = d_.reshape(shp), m_.reshape(shp), v_.reshape(shp)
    res = _adamw_small([flat(a[k]) for k in small_keys], [gs[k] for k in small_keys],
                       [flat(a["m_" + k]) for k in small_keys], [flat(a["v_" + k]) for k in small_keys])
    for dst, vals in zip((delta, new_m, new_v), res):
        dst.update({k: t.reshape(a[k].shape) for k, t in zip(small_keys, vals)})
    grads = {**gb, **{k: t.reshape(a[k].shape) for k, t in gs.items()}}

    return (loss, grad_x, *[grads[k] for k in _WEIGHTS], *[delta[k] for k in _WEIGHTS],
            *[new_m[k] for k in _WEIGHTS], *[new_v[k] for k in _WEIGHTS])
```

```python
import functools
import math

import numpy as np
import jax
import jax.numpy as jnp
from jax import lax
from jax.experimental import pallas as pl
from jax.experimental.pallas import tpu as pltpu

F32 = jnp.float32
BF16 = jnp.bfloat16
EPS = 1e-6
VMEM_LIMIT = 56 * 1024 * 1024
LANES = 128
HEAD_DIM = 64
CHUNK = 64
N_LEFT = 8
MAX_REL = 128
ATT_TQ = 256
CONV_W = 31
HALO = 32
ROW_CHUNK = 256
SUBLANES = 8
GROUP_LOG = 3
NEG = -1e30
N_CHIPS = 4
PACK_COLS = 1024

ADAM_LR = 0.001
ADAM_B1 = 0.9
ADAM_B2 = 0.999
ADAM_EPS = 1e-08
ADAM_WD = 0.01
ADAM_STEP = 10

MESH = pl.DeviceIdType.MESH
_ANY = pl.BlockSpec(memory_space=pl.ANY)


def _cp(*sem):
    return pltpu.CompilerParams(dimension_semantics=sem, vmem_limit_bytes=VMEM_LIMIT)


def _sds(shape, dtype):
    return jax.ShapeDtypeStruct(shape, dtype)


def _tile(n, pref):
    t = min(n, pref)
    while n % t:
        t -= 8
    return t


def _sigmoid(x):
    return jax.nn.sigmoid(x)


_GELU_C = math.sqrt(2.0 / math.pi)


def _gelu(y):
    return 0.5 * y * (1.0 + jnp.tanh(_GELU_C * (y + 0.044715 * y * y * y)))


def _gelu_grad(y):
    th = jnp.tanh(_GELU_C * (y + 0.044715 * y * y * y))
    return 0.5 * (1.0 + th) + 0.5 * y * (1.0 - th * th) * _GELU_C * (1.0 + 3.0 * 0.044715 * y * y)


def _dot(a, b):
    return jnp.dot(a, b, preferred_element_type=F32)


def _dot_t0(a, b):
    return lax.dot_general(a, b, (((0,), (0,)), ((), ())), preferred_element_type=F32)


def _dot_t1(a, b):
    return lax.dot_general(a, b, (((1,), (1,)), ((), ())), preferred_element_type=F32)


def _slab_out(nl, l, shape, buf, n_in):
    sds = _sds((nl,) + tuple(shape), F32)
    if buf is None:
        return [], [], sds, {}
    return [buf], [_ANY], sds, {n_in: 0}


def _norm_mm(x, g, w, l, *, tm, tn, ntiles, pieces, transposed, name, comm=None):
    n, d = x.shape
    m = ntiles * tn
    mp = m // pieces
    npj = mp // tn
    nc = 0 if comm is None else len(comm[0])

    def body(x_ref, g_ref, w_ref, *rest):
        h_ref, y_ref = rest[nc:nc + 2]
        h_scr = rest[2 * nc + 2]
        if comm is not None:
            finish = _hosted_gather_steps(comm, rest[nc + 2:2 * nc + 2], rest[-2:],
                                          pl.program_id(0) * ntiles + pl.program_id(1), (n // tm) * ntiles)

        @pl.when(pl.program_id(1) == 0)
        def _():
            for r0 in range(0, tm, ROW_CHUNK):
                rows = slice(r0, r0 + ROW_CHUNK)
                xv = x_ref[rows, :]
                r = lax.rsqrt(jnp.mean(xv * xv, axis=-1, keepdims=True) + EPS)
                hb = (xv * r * g_ref[...]).astype(BF16)
                h_scr[rows, :] = hb
                h_ref[rows, :] = hb

        mm = _dot_t1 if transposed else _dot
        y_ref[...] = mm(h_scr[...], w_ref[...]).astype(BF16)
        if comm is not None:
            finish()

    wspec = (pl.BlockSpec((None, tn, d), lambda i, j: (l, j, 0)) if transposed
             else pl.BlockSpec((None, d, tn), lambda i, j: (l, 0, j)))
    c_in, c_ispec, c_ospec, c_oshape, c_scr, aliases = _hosted_gather(comm, 3, 2)
    return pl.pallas_call(
        body, grid=(n // tm, ntiles),
        in_specs=[pl.BlockSpec((tm, d), lambda i, j: (i, 0)), pl.BlockSpec((1, d), lambda i, j: (0, 0)), wspec] + c_ispec,
        out_specs=[pl.BlockSpec((tm, d), lambda i, j: (i, 0)),
                   pl.BlockSpec((None, tm, tn), lambda i, j: (j // npj, i, j % npj))] + c_ospec,
        out_shape=[_sds((n, d), BF16), _sds((pieces, n, mp), BF16)] + c_oshape,
        input_output_aliases=aliases,
        scratch_shapes=[pltpu.VMEM((tm, d), BF16)] + c_scr,
        compiler_params=_cp("arbitrary", "arbitrary"), name=name)(x, g, w, *c_in)


def _mm_t(a, w, l, *, tm, tn, off, ntiles, name):
    n, k = a.shape

    def body(a_ref, w_ref, y_ref):
        y_ref[...] = _dot_t1(a_ref[...], w_ref[...]).astype(BF16)

    return pl.pallas_call(
        body, grid=(n // tm, ntiles),
        in_specs=[pl.BlockSpec((tm, k), lambda i, j: (i, 0)), pl.BlockSpec((None, tn, k), lambda i, j: (l, off + j, 0))],
        out_specs=pl.BlockSpec((tm, tn), lambda i, j: (i, j)),
        out_shape=_sds((n, ntiles * tn), BF16),
        compiler_params=_cp("parallel", "arbitrary"), name=name)(a, w)


def _ffn_down(ab, wd, l, x, *, tm, tk, name):
    _, n, dff = ab.shape
    d = x.shape[1]
    nk = dff // tk

    def body(a_ref, b_ref, wd_ref, x_ref, o_ref, acc):
        k = pl.program_id(1)
        a = a_ref[...].astype(F32)
        b = b_ref[...].astype(F32)
        act = (a * _sigmoid(a) * b).astype(BF16)
        part = _dot(act, wd_ref[pl.ds(pl.multiple_of(k * tk, tk), tk), :])

        @pl.when(k == 0)
        def _():
            acc[...] = part

        @pl.when(k > 0)
        def _():
            acc[...] += part

        @pl.when(k == nk - 1)
        def _():
            o_ref[...] = x_ref[...] + 0.5 * acc[...]

    return pl.pallas_call(
        body, grid=(n // tm, nk),
        in_specs=[pl.BlockSpec((None, tm, tk), lambda i, k: (0, i, k)),
                  pl.BlockSpec((None, tm, tk), lambda i, k: (1, i, k)),
                  pl.BlockSpec((None, dff, d), lambda i, k: (l, 0, 0)),
                  pl.BlockSpec((tm, d), lambda i, k: (i, 0))],
        out_specs=pl.BlockSpec((tm, d), lambda i, k: (i, 0)),
        out_shape=_sds((n, d), F32),
        scratch_shapes=[pltpu.VMEM((tm, d), F32)],
        compiler_params=_cp("parallel", "arbitrary"), name=name)(ab, ab, wd, x)


def _ffn_dact(dx, wd, l, ab, nl, dwd_buf, *, tm, tk, name, comm=None):
    n, d = dx.shape
    dff = ab.shape[2]
    half = ((tk // LANES + 1) // 2) * LANES
    chunks = ((0, half), (half, tk))
    ne = 0 if dwd_buf is None else 1
    nc = 0 if comm is None else len(comm[0])
    ni = n // tm

    def body(dx_ref, wd_ref, a_ref, b_ref, *rest):
        dab_ref, dwd_ref = rest[ne + nc:ne + nc + 2]
        if comm is not None:
            start, wait = _exchange_ops(rest[ne:ne + nc], rest[ne + nc + 2:ne + 2 * nc + 2], rest[-2], rest[-1],
                                        comm[1], comm[2])
            step = pl.program_id(0) * ni + pl.program_id(1)

            @pl.when(step == 0)
            def _():
                start()

        do = (0.5 * dx_ref[...]).astype(BF16)

        @pl.when(pl.program_id(1) == 0)
        def _():
            dwd_ref[...] = jnp.zeros_like(dwd_ref)

        for c0, c1 in chunks:
            dact = _dot_t1(do, wd_ref[c0:c1, :])
            a = a_ref[:, c0:c1].astype(F32)
            b = b_ref[:, c0:c1].astype(F32)
            sg = _sigmoid(a)
            silu = a * sg
            dab_ref[0, :, c0:c1] = (dact * b * (sg * (1.0 + a * (1.0 - sg)))).astype(BF16)
            dab_ref[1, :, c0:c1] = (dact * silu).astype(BF16)
            dwd_ref[c0:c1, :] += _dot_t0((silu * b).astype(BF16), do)

        if comm is not None:
            @pl.when(step == (dff // tk) * ni - 1)
            def _():
                wait()

    extra, extra_specs, dwd_shape, aliases = _slab_out(nl, l, (dff, d), dwd_buf, 4)
    aliases = {k: 1 for k in aliases}
    comm_in = [] if comm is None else list(comm[0])
    comm_out = [] if comm is None else _parts_shapes(comm[0], comm[1])
    comm_scr = [] if comm is None else [pltpu.SemaphoreType.DMA((3 * nc,)), pltpu.SemaphoreType.DMA((3 * nc,))]
    return pl.pallas_call(
        body, grid=(dff // tk, ni),
        in_specs=[pl.BlockSpec((tm, d), lambda j, i: (i, 0)),
                  pl.BlockSpec((None, tk, d), lambda j, i: (l, j, 0)),
                  pl.BlockSpec((None, tm, tk), lambda j, i: (0, i, j)),
                  pl.BlockSpec((None, tm, tk), lambda j, i: (1, i, j)), *extra_specs] + [_ANY] * nc,
        out_specs=[pl.BlockSpec((2, tm, tk), lambda j, i: (0, i, j)),
                   pl.BlockSpec((None, tk, d), lambda j, i: (l, j, 0))] + [_ANY] * nc,
        out_shape=[_sds((2, n, dff), BF16), dwd_shape] + comm_out,
        input_output_aliases=aliases, scratch_shapes=comm_scr,
        compiler_params=_cp("arbitrary", "arbitrary"), name=name)(dx, wd, ab, ab, *extra, *comm_in)


def _rms_bwd_epilogue(acc, x_ref, g_ref, dres_ref, dx_ref, dg_ref, i):
    dgp = jnp.zeros(dg_ref.shape, F32)
    for r0 in range(0, acc.shape[0], ROW_CHUNK):
        rows = slice(r0, r0 + ROW_CHUNK)
        dh = acc[rows, :]
        xv = x_ref[rows, :]
        r = lax.rsqrt(jnp.mean(xv * xv, axis=-1, keepdims=True) + EPS)
        xn = xv * r
        dgp = dgp + jnp.sum(dh * xn, axis=0, keepdims=True)
        dxh = dh * g_ref[...]
        dx_ref[rows, :] = dres_ref[rows, :] + r * (dxh - xn * jnp.mean(dxh * xn, axis=-1, keepdims=True))

    @pl.when(i == 0)
    def _():
        dg_ref[...] = dgp

    @pl.when(i > 0)
    def _():
        dg_ref[...] += dgp


def _ffn_dx(dab, wu, l, x, g, dres, *, tm, tk, name, comm=None):
    p, n, mp = dab.shape
    d = x.shape[1]
    nkp = mp // tk
    nk = p * nkp
    ni = n // tm
    nc = 0 if comm is None else len(comm[0])

    def body(dy_ref, w_ref, x_ref, g_ref, dres_ref, *rest):
        dx_ref, dg_ref = rest[nc:nc + 2]
        acc = rest[2 * nc + 2]
        k = pl.program_id(1)
        if comm is not None:
            start, wait = _swap_ops(rest[:nc], rest[nc + 2:2 * nc + 2], rest[-2], rest[-1], comm[1])

            @pl.when((pl.program_id(0) == 0) & (k == 0))
            def _():
                start()

        part = _dot_t1(dy_ref[...], w_ref[...])

        @pl.when(k == 0)
        def _():
            acc[...] = part

        @pl.when(k > 0)
        def _():
            acc[...] += part

        @pl.when(k == nk - 1)
        def _():
            _rms_bwd_epilogue(acc, x_ref, g_ref, dres_ref, dx_ref, dg_ref, pl.program_id(0))

        if comm is not None:
            @pl.when((pl.program_id(0) == ni - 1) & (k == nk - 1))
            def _():
                wait()

    comm_in = [] if comm is None else list(comm[0])
    comm_out = [_sds(t.shape[1:], t.dtype) for t in comm_in]
    comm_scr = [] if comm is None else [pltpu.SemaphoreType.DMA((nc,)), pltpu.SemaphoreType.DMA((nc,))]
    return pl.pallas_call(
        body, grid=(ni, nk),
        in_specs=[pl.BlockSpec((None, tm, tk), lambda i, k: (k // nkp, i, k % nkp)),
                  pl.BlockSpec((None, d, tk), lambda i, k: (l, 0, k)),
                  pl.BlockSpec((tm, d), lambda i, k: (i, 0)),
                  pl.BlockSpec((1, d), lambda i, k: (0, 0)),
                  pl.BlockSpec((tm, d), lambda i, k: (i, 0))] + [_ANY] * nc,
        out_specs=[pl.BlockSpec((tm, d), lambda i, k: (i, 0)), pl.BlockSpec((1, d), lambda i, k: (0, 0))] + [_ANY] * nc,
        out_shape=[_sds((n, d), F32), _sds((1, d), F32)] + comm_out,
        scratch_shapes=[pltpu.VMEM((tm, d), F32)] + comm_scr,
        compiler_params=_cp("arbitrary", "arbitrary"), name=name)(dab, wu, x, g, dres, *comm_in)


def _mix_dx(dpa, dgl, wt, l, x, g, dres, *, tm, tk, name, comm=None):
    n, d = x.shape
    n1 = dpa.shape[1] // tk
    n2 = dgl.shape[1] // tk
    nk = n1 + n2
    ni = n // tm
    nc = 0 if comm is None else len(comm[0])

    def body(d1_ref, d2_ref, w_ref, x_ref, g_ref, dres_ref, *rest):
        dx_ref, dg_ref = rest[nc:nc + 2]
        acc = rest[2 * nc + 2]
        k = pl.program_id(1)
        if comm is not None:
            start, wait = _swap_ops(rest[:nc], rest[nc + 2:2 * nc + 2], rest[-2], rest[-1], comm[1])

            @pl.when((pl.program_id(0) == 0) & (k == 0))
            def _():
                start()

        @pl.when(k == 0)
        def _():
            acc[...] = _dot(d1_ref[...], w_ref[...])

        @pl.when((k > 0) & (k < n1))
        def _():
            acc[...] += _dot(d1_ref[...], w_ref[...])

        @pl.when(k >= n1)
        def _():
            acc[...] += _dot(d2_ref[...], w_ref[...])

        @pl.when(k == nk - 1)
        def _():
            _rms_bwd_epilogue(acc, x_ref, g_ref, dres_ref, dx_ref, dg_ref, pl.program_id(0))

        if comm is not None:
            @pl.when((pl.program_id(0) == ni - 1) & (k == nk - 1))
            def _():
                wait()

    comm_in = [] if comm is None else list(comm[0])
    comm_out = [_sds(t.shape[1:], t.dtype) for t in comm_in]
    comm_scr = [] if comm is None else [pltpu.SemaphoreType.DMA((nc,)), pltpu.SemaphoreType.DMA((nc,))]
    return pl.pallas_call(
        body, grid=(ni, nk),
        in_specs=[pl.BlockSpec((tm, tk), lambda i, k: (i, jnp.minimum(k, n1 - 1))),
                  pl.BlockSpec((tm, tk), lambda i, k: (i, jnp.maximum(k - n1, 0))),
                  pl.BlockSpec((None, tk, d), lambda i, k: (l, k, 0)),
                  pl.BlockSpec((tm, d), lambda i, k: (i, 0)),
                  pl.BlockSpec((1, d), lambda i, k: (0, 0)),
                  pl.BlockSpec((tm, d), lambda i, k: (i, 0))] + [_ANY] * nc,
        out_specs=[pl.BlockSpec((tm, d), lambda i, k: (i, 0)), pl.BlockSpec((1, d), lambda i, k: (0, 0))] + [_ANY] * nc,
        out_shape=[_sds((n, d), F32), _sds((1, d), F32)] + comm_out,
        scratch_shapes=[pltpu.VMEM((tm, d), F32)] + comm_scr,
        compiler_params=_cp("arbitrary", "arbitrary"), name=name)(dpa, dgl, wt, x, g, dres, *comm_in)


def _mm_tn(a, b, l, nl, buf, *, ta, tb, tk, name):
    pa, n, ka = a.shape
    pb, _, kb = b.shape
    nap = ka // ta
    nbp = kb // tb

    def body(a_ref, b_ref, *rest):
        o_ref = rest[-1]

        @pl.when(pl.program_id(2) == 0)
        def _():
            o_ref[...] = jnp.zeros_like(o_ref)

        o_ref[...] += _dot_t0(a_ref[...], b_ref[...])

    extra, extra_specs, out_shape, aliases = _slab_out(nl, l, (pa * ka, pb * kb), buf, 2)
    return pl.pallas_call(
        body, grid=(pa * nap, pb * nbp, n // tk),
        in_specs=[pl.BlockSpec((None, tk, ta), lambda i, j, k: (i // nap, k, i % nap)),
                  pl.BlockSpec((None, tk, tb), lambda i, j, k: (j // nbp, k, j % nbp)), *extra_specs],
        out_specs=pl.BlockSpec((None, ta, tb), lambda i, j, k: (l, i, j)),
        out_shape=out_shape, input_output_aliases=aliases,
        compiler_params=_cp("parallel", "parallel", "arbitrary"), name=name)(a, b, *extra)


def _dwin_t(dpa, dgl, h, l, nl, buf, *, ta, tk, name):
    n, d = h.shape
    n1 = dpa.shape[1] // ta
    n2 = dgl.shape[1] // ta

    def body(a1_ref, a2_ref, h_ref, *rest):
        o_ref = rest[-1]
        i = pl.program_id(0)

        @pl.when(pl.program_id(1) == 0)
        def _():
            o_ref[...] = jnp.zeros_like(o_ref)

        @pl.when(i < n1)
        def _():
            o_ref[...] += _dot_t0(a1_ref[...], h_ref[...])

        @pl.when(i >= n1)
        def _():
            o_ref[...] += _dot_t0(a2_ref[...], h_ref[...])

    extra, extra_specs, out_shape, aliases = _slab_out(nl, l, ((n1 + n2) * ta, d), buf, 3)
    return pl.pallas_call(
        body, grid=(n1 + n2, n // tk),
        in_specs=[pl.BlockSpec((tk, ta), lambda i, k: (jnp.where(i < n1, k, 0), jnp.minimum(i, n1 - 1))),
                  pl.BlockSpec((tk, ta), lambda i, k: (jnp.where(i >= n1, k, 0), jnp.maximum(i - n1, 0))),
                  pl.BlockSpec((tk, d), lambda i, k: (k, 0)), *extra_specs],
        out_specs=pl.BlockSpec((None, ta, d), lambda i, k: (l, i, 0)),
        out_shape=out_shape, input_output_aliases=aliases,
        compiler_params=_cp("parallel", "arbitrary"), name=name)(dpa, dgl, h, *extra)


def _loss_grad(y, t, *, tm, name):
    n, d = y.shape

    def body(y_ref, t_ref, dy_ref, l_ref):
        e = y_ref[...] - t_ref[...]
        dy_ref[...] = e * (1.0 / d)
        part = jnp.sum(e * e, axis=0, keepdims=True)

        @pl.when(pl.program_id(0) == 0)
        def _():
            l_ref[...] = part

        @pl.when(pl.program_id(0) > 0)
        def _():
            l_ref[...] += part

    return pl.pallas_call(
        body, grid=(n // tm,),
        in_specs=[pl.BlockSpec((tm, d), lambda i: (i, 0)), pl.BlockSpec((tm, d), lambda i: (i, 0))],
        out_specs=[pl.BlockSpec((tm, d), lambda i: (i, 0)), pl.BlockSpec((1, d), lambda i: (0, 0))],
        out_shape=[_sds((n, d), F32), _sds((1, d), F32)],
        compiler_params=_cp("arbitrary"), name=name)(y, t)


def _s5_fwd(proj, sp, wglu, l, *, bl, s, t, name):
    n = bl * s
    ds5, gp = sp["bblk_r"].shape
    nt = s // t
    ng = t // SUBLANES
    glog = int(math.log2(ng))

    def body(u_ref, br_ref, bi_ref, pw_ref, p8_ref, cr_ref, ci_ref, d_ref, wg_ref,
             xr_ref, xi_ref, yp_ref, zg_ref, o_ref, carry, st):
        @pl.when(pl.program_id(1) == 0)
        def _():
            carry[...] = jnp.zeros_like(carry)

        u = u_ref[...]
        sub = lax.broadcasted_iota(jnp.int32, (t, gp), 0) % SUBLANES
        xr = _dot(u, br_ref[...])
        xi = _dot(u, bi_ref[...])
        for k in range(GROUP_LOG):
            sh = 1 << k
            pr = pw_ref[2 * k:2 * k + 1, :]
            pi = pw_ref[2 * k + 1:2 * k + 2, :]
            keep = sub >= sh
            sr = jnp.where(keep, pltpu.roll(xr, sh, 0), 0.0)
            si = jnp.where(keep, pltpu.roll(xi, sh, 0), 0.0)
            xr, xi = xr + pr * sr - pi * si, xi + pr * si + pi * sr
        xr_ref[...] = xr
        xi_ref[...] = xi
        grow = lax.broadcasted_iota(jnp.int32, (ng, gp), 0)
        cr = carry[0:1, :]
        ci = carry[1:2, :]
        a8r = pw_ref[2 * GROUP_LOG:2 * GROUP_LOG + 1, :]
        a8i = pw_ref[2 * GROUP_LOG + 1:2 * GROUP_LOG + 2, :]
        head = grow == 0
        for g in range(ng):
            st[g:g + 1, :] = xr_ref[(g + 1) * SUBLANES - 1:(g + 1) * SUBLANES, :]
            st[ng + g:ng + g + 1, :] = xi_ref[(g + 1) * SUBLANES - 1:(g + 1) * SUBLANES, :]
        sr_ = st[0:ng, :] + jnp.where(head, a8r * cr - a8i * ci, 0.0)
        si_ = st[ng:2 * ng, :] + jnp.where(head, a8r * ci + a8i * cr, 0.0)
        for k in range(glog):
            sh = 1 << k
            pr = pw_ref[2 * (GROUP_LOG + k):2 * (GROUP_LOG + k) + 1, :]
            pi = pw_ref[2 * (GROUP_LOG + k) + 1:2 * (GROUP_LOG + k) + 2, :]
            keep = grow >= sh
            tr = jnp.where(keep, pltpu.roll(sr_, sh, 0), 0.0)
            ti = jnp.where(keep, pltpu.roll(si_, sh, 0), 0.0)
            sr_, si_ = sr_ + pr * tr - pi * ti, si_ + pr * ti + pi * tr
        tail = grow == ng - 1
        carry[0:1, :] = jnp.sum(jnp.where(tail, sr_, 0.0), axis=0, keepdims=True)
        carry[1:2, :] = jnp.sum(jnp.where(tail, si_, 0.0), axis=0, keepdims=True)
        st[0:ng, :] = jnp.where(head, cr, pltpu.roll(sr_, 1, 0))
        st[ng:2 * ng, :] = jnp.where(head, ci, pltpu.roll(si_, 1, 0))
        p8r = p8_ref[0:SUBLANES, :]
        p8i = p8_ref[SUBLANES:2 * SUBLANES, :]
        for g in range(ng):
            grp = slice(g * SUBLANES, (g + 1) * SUBLANES)
            pr = st[g:g + 1, :]
            pi = st[ng + g:ng + g + 1, :]
            xr_ref[grp, :] = xr_ref[grp, :] + p8r * pr - p8i * pi
            xi_ref[grp, :] = xi_ref[grp, :] + p8r * pi + p8i * pr
        xr = xr_ref[...]
        xi = xi_ref[...]
        y = _dot(xr.astype(BF16), cr_ref[...]) + _dot(xi.astype(BF16), ci_ref[...]) + d_ref[...] * u.astype(F32)
        yp_ref[...] = y
        zg = _dot(_gelu(y).astype(BF16), wg_ref[...])
        zg_ref[...] = zg
        o_ref[...] = (zg[:, :ds5] * _sigmoid(zg[:, ds5:])).astype(BF16)

    const = lambda shape: pl.BlockSpec(shape, lambda b, i: (0, 0))
    row = lambda w: pl.BlockSpec((t, w), lambda b, i: (b * nt + i, 0))
    return pl.pallas_call(
        body, grid=(bl, nt),
        in_specs=[row(ds5), const((ds5, gp)), const((ds5, gp)), const((2 * (GROUP_LOG + glog), gp)),
                  const((2 * SUBLANES, gp)),
                  const((gp, ds5)), const((gp, ds5)), const((1, ds5)),
                  pl.BlockSpec((None, ds5, 2 * ds5), lambda b, i: (l, 0, 0))],
        out_specs=[row(gp), row(gp), row(ds5), row(2 * ds5), row(ds5)],
        out_shape=[_sds((n, gp), F32), _sds((n, gp), F32), _sds((n, ds5), F32), _sds((n, 2 * ds5), F32),
                   _sds((n, ds5), BF16)],
        scratch_shapes=[pltpu.VMEM((2, gp), F32), pltpu.VMEM((2 * ng, gp), F32)],
        compiler_params=_cp("arbitrary", "arbitrary"), name=name)(
            proj, sp["bblk_r"], sp["bblk_i"], sp["pw"], sp["p8"], sp["cblk_r"], sp["cblk_in"], sp["d"], wglu)


def _s5_bwd(ds, yp, zg, xr, xi, proj, sp, wglu, l, nl, dwg_buf, *, bl, s, t, name):
    n = bl * s
    ds5, gp = sp["bblk_r"].shape
    nt = s // t
    tb = t // 8
    ng = t // SUBLANES
    glog = int(math.log2(ng))

    def body(ds_ref, yp_ref, zg_ref, xr_ref, xi_ref, hr_ref, hi_ref, u_ref, wg_ref, cr_ref, ci_ref,
             br_ref, bi_ref, pw_ref, q8_ref, d_ref, *rest):
        du_ref, dd_ref, dcr_ref, dci_ref, dbr_ref, dbi_ref, da_ref, dwg_ref, carry, gr_scr, gi_scr, st = rest[-12:]
        b = pl.program_id(0)
        i = pl.program_id(1)
        tile = nt - 1 - i

        @pl.when((b == 0) & (i == 0))
        def _():
            for r in (dwg_ref, dd_ref, dcr_ref, dci_ref, dbr_ref, dbi_ref, da_ref):
                r[...] = jnp.zeros_like(r)

        @pl.when(i == 0)
        def _():
            carry[...] = jnp.zeros_like(carry)

        dsv = ds_ref[...].astype(F32)
        zgv = zg_ref[...]
        za = zgv[:, :ds5]
        sg = _sigmoid(zgv[:, ds5:])
        dzg = jnp.concatenate([dsv * sg, dsv * za * sg * (1.0 - sg)], axis=1).astype(BF16)
        y = yp_ref[...]
        dwg_ref[...] += _dot_t0(_gelu(y).astype(BF16), dzg)
        dy = _dot_t1(dzg, wg_ref[...]) * _gelu_grad(y)
        ub = u_ref[...]
        uf = ub.astype(F32)
        dd_ref[...] += jnp.sum(dy * uf, axis=0, keepdims=True)
        dyb = dy.astype(BF16)
        xrv = xr_ref[...]
        xiv = xi_ref[...]
        dcr_ref[...] += _dot_t0(xrv.astype(BF16), dyb)
        dci_ref[...] += _dot_t0(xiv.astype(BF16), dyb)

        rows = lax.broadcasted_iota(jnp.int32, (t, gp), 0)
        sub = rows % SUBLANES
        gr = _dot_t1(dyb, cr_ref[...])
        gi = _dot_t1(dyb, ci_ref[...])
        for k in range(GROUP_LOG):
            sh = 1 << k
            pr = pw_ref[2 * k:2 * k + 1, :]
            pi = pw_ref[2 * k + 1:2 * k + 2, :]
            keep = sub < SUBLANES - sh
            sr = jnp.where(keep, pltpu.roll(gr, t - sh, 0), 0.0)
            si = jnp.where(keep, pltpu.roll(gi, t - sh, 0), 0.0)
            gr, gi = gr + pr * sr + pi * si, gi + pr * si - pi * sr
        gr_scr[...] = gr
        gi_scr[...] = gi
        grow = lax.broadcasted_iota(jnp.int32, (ng, gp), 0)
        cr = carry[0:1, :]
        ci = carry[1:2, :]
        a8r = pw_ref[2 * GROUP_LOG:2 * GROUP_LOG + 1, :]
        a8i = pw_ref[2 * GROUP_LOG + 1:2 * GROUP_LOG + 2, :]
        tail = grow == ng - 1
        for g in range(ng):
            st[g:g + 1, :] = gr_scr[g * SUBLANES:g * SUBLANES + 1, :]
            st[ng + g:ng + g + 1, :] = gi_scr[g * SUBLANES:g * SUBLANES + 1, :]
        sr_ = st[0:ng, :] + jnp.where(tail, a8r * cr + a8i * ci, 0.0)
        si_ = st[ng:2 * ng, :] + jnp.where(tail, a8r * ci - a8i * cr, 0.0)
        for k in range(glog):
            sh = 1 << k
            pr = pw_ref[2 * (GROUP_LOG + k):2 * (GROUP_LOG + k) + 1, :]
            pi = pw_ref[2 * (GROUP_LOG + k) + 1:2 * (GROUP_LOG + k) + 2, :]
            keep = grow < ng - sh
            tr = jnp.where(keep, pltpu.roll(sr_, ng - sh, 0), 0.0)
            ti = jnp.where(keep, pltpu.roll(si_, ng - sh, 0), 0.0)
            sr_, si_ = sr_ + pr * tr + pi * ti, si_ + pr * ti - pi * tr
        head = grow == 0
        carry[0:1, :] = jnp.sum(jnp.where(head, sr_, 0.0), axis=0, keepdims=True)
        carry[1:2, :] = jnp.sum(jnp.where(head, si_, 0.0), axis=0, keepdims=True)
        st[0:ng, :] = jnp.where(tail, cr, pltpu.roll(sr_, ng - 1, 0))
        st[ng:2 * ng, :] = jnp.where(tail, ci, pltpu.roll(si_, ng - 1, 0))
        q8r = q8_ref[0:SUBLANES, :]
        q8i = q8_ref[SUBLANES:2 * SUBLANES, :]
        for g in range(ng):
            grp = slice(g * SUBLANES, (g + 1) * SUBLANES)
            pr = st[g:g + 1, :]
            pi = st[ng + g:ng + g + 1, :]
            gr_scr[grp, :] = gr_scr[grp, :] + q8r * pr - q8i * pi
            gi_scr[grp, :] = gi_scr[grp, :] + q8r * pi + q8i * pr
        gr = gr_scr[...]
        gi = gi_scr[...]
        first = rows == 0

        live = jnp.where(tile > 0, 1.0, 0.0)
        xpr = jnp.where(first, hr_ref[7:8, :] * live, pltpu.roll(xrv, 1, 0))
        xpi = jnp.where(first, hi_ref[7:8, :] * live, pltpu.roll(xiv, 1, 0))
        da_ref[0:1, :] += jnp.sum(gr * xpr + gi * xpi, axis=0, keepdims=True)
        da_ref[1:2, :] += jnp.sum(gi * xpr - gr * xpi, axis=0, keepdims=True)

        grb = gr.astype(BF16)
        gib = gi.astype(BF16)
        dbr_ref[...] += _dot_t0(ub, grb)
        dbi_ref[...] += _dot_t0(ub, gib)
        du_ref[...] = (_dot_t1(grb, br_ref[...]) + _dot_t1(gib, bi_ref[...]) + dy * d_ref[...]).astype(BF16)

    const = lambda shape: pl.BlockSpec(shape, lambda b, i: (0, 0))
    row = lambda w: pl.BlockSpec((t, w), lambda b, i: (b * nt + nt - 1 - i, 0))
    halo = pl.BlockSpec((8, gp), lambda b, i: (jnp.maximum((b * nt + nt - 1 - i) * tb - 1, 0), 0))
    extra, extra_specs, dwg_shape, aliases = _slab_out(nl, l, (ds5, 2 * ds5), dwg_buf, 16)
    aliases = {k: 7 for k in aliases}
    return pl.pallas_call(
        body, grid=(bl, nt),
        in_specs=[row(ds5), row(ds5), row(2 * ds5), row(gp), row(gp), halo, halo, row(ds5),
                  pl.BlockSpec((None, ds5, 2 * ds5), lambda b, i: (l, 0, 0)),
                  const((gp, ds5)), const((gp, ds5)), const((ds5, gp)), const((ds5, gp)),
                  const((2 * (GROUP_LOG + glog), gp)), const((2 * SUBLANES, gp)), const((1, ds5)), *extra_specs],
        out_specs=[row(ds5), const((1, ds5)), const((gp, ds5)), const((gp, ds5)),
                   const((ds5, gp)), const((ds5, gp)), const((2, gp)),
                   pl.BlockSpec((None, ds5, 2 * ds5), lambda b, i: (l, 0, 0))],
        out_shape=[_sds((n, ds5), BF16), _sds((1, ds5), F32), _sds((gp, ds5), F32),
                   _sds((gp, ds5), F32), _sds((ds5, gp), F32), _sds((ds5, gp), F32), _sds((2, gp), F32), dwg_shape],
        input_output_aliases=aliases,
        scratch_shapes=[pltpu.VMEM((2, gp), F32), pltpu.VMEM((t, gp), F32), pltpu.VMEM((t, gp), F32),
                        pltpu.VMEM((2 * ng, gp), F32)],
        compiler_params=_cp("arbitrary", "arbitrary"), name=name)(
            ds, yp, zg, xr, xi, xr, xi, proj, wglu, sp["cblk_r"], sp["cblk_in"],
            sp["bblk_r"], sp["bblk_i"], sp["pw"], sp["q8"], sp["d"], *extra)


def _head_norm(x, first):
    x2 = x * x
    sa = jnp.sum(jnp.where(first, x2, 0.0), axis=-1, keepdims=True)
    sb = jnp.sum(jnp.where(first, 0.0, x2), axis=-1, keepdims=True)
    r = jnp.where(first, lax.rsqrt(sa * (1.0 / HEAD_DIM) + EPS), lax.rsqrt(sb * (1.0 / HEAD_DIM) + EPS))
    return x * r, r


def _attn_specs(bl, s, datt, qoff):
    nq = s // ATT_TQ
    nb = datt // LANES
    col = lambda blk: (lambda b, h, q: (b * nq + q, qoff + blk * nb + h))
    win = lambda blk, j: (lambda b, h, q: (b * nq + jnp.maximum(q - 2 + j, 0), qoff + blk * nb + h))
    tile = lambda f: pl.BlockSpec((ATT_TQ, LANES), f)
    qs = tile(col(0))
    ks = [tile(win(1, j)) for j in range(3)]
    vs = [tile(win(2, j)) for j in range(3)]
    return nq, nb, qs, ks, vs


def _attn_probs(q_ref, k_refs, gq_ref, gk_ref, bias_ref):
    qt = pl.program_id(2)
    lane = lax.broadcasted_iota(jnp.int32, (1, LANES), 1)
    first = lane < HEAD_DIM
    qh, rq = _head_norm(q_ref[...].astype(F32), first)
    qn = qh * gq_ref[...]
    kc = jnp.concatenate([r[...] for r in k_refs], axis=0).astype(F32)
    kh, _ = _head_norm(kc, first)
    kn = (kh * gk_ref[...]).astype(BF16)
    kpos = (qt - 2) * ATT_TQ + lax.broadcasted_iota(jnp.int32, (1, 3 * ATT_TQ), 1)
    valid = kpos >= 0
    scale = HEAD_DIM ** -0.5
    masks = (first, jnp.logical_not(first))
    qas, ps = [], []
    for hh in range(2):
        qa = jnp.where(masks[hh], qn, 0.0).astype(BF16)
        sc = _dot_t1(qa, kn) * scale + bias_ref[hh]
        sc = jnp.where(valid, sc, NEG)
        e = jnp.exp(sc - jnp.max(sc, axis=-1, keepdims=True))
        ps.append(e / jnp.sum(e, axis=-1, keepdims=True))
        qas.append(qa)
    return first, masks, qh, rq, kn, qas, ps


def _attn_fwd(proj, gq2, gk2, bias, *, bl, s, datt, qoff, name, comm=None):
    n = bl * s
    nq, nb, qs, ks, vs = _attn_specs(bl, s, datt, qoff)
    nc = 0 if comm is None else len(comm[0])

    def body(q_ref, k0, k1, k2, v0, v1, v2, gq_ref, gk_ref, bias_ref, *rest):
        o_ref = rest[nc]
        if comm is not None:
            finish = _hosted_gather_steps(comm, rest[nc + 1:2 * nc + 1], rest[-2:],
                                          (pl.program_id(0) * nb + pl.program_id(1)) * nq + pl.program_id(2),
                                          bl * nb * nq)
        first, _, _, _, _, _, ps = _attn_probs(q_ref, (k0, k1, k2), gq_ref, gk_ref, bias_ref)
        vc = jnp.concatenate([v0[...], v1[...], v2[...]], axis=0)
        o0 = _dot(ps[0].astype(BF16), vc)
        o1 = _dot(ps[1].astype(BF16), vc)
        o_ref[...] = jnp.where(first, o0, o1).astype(BF16)
        if comm is not None:
            finish()

    gs = pl.BlockSpec((1, LANES), lambda b, h, q: (0, 0))
    c_in, c_ispec, c_ospec, c_oshape, c_scr, aliases = _hosted_gather(comm, 10, 1)
    res = pl.pallas_call(
        body, grid=(bl, nb, nq),
        in_specs=[qs, *ks, *vs, gs, gs, pl.BlockSpec((2, ATT_TQ, 3 * ATT_TQ), lambda b, h, q: (h, 0, 0))] + c_ispec,
        out_specs=[pl.BlockSpec((ATT_TQ, LANES), lambda b, h, q: (b * nq + q, h))] + c_ospec,
        out_shape=[_sds((n, datt), BF16)] + c_oshape,
        input_output_aliases=aliases, scratch_shapes=c_scr,
        compiler_params=_cp("arbitrary", "arbitrary", "arbitrary"), name=name)(
            proj, proj, proj, proj, proj, proj, proj, gq2, gk2, bias, *c_in)
    return res[0], res[1:]


def _attn_bwd(do, proj, gq2, gk2, bias, *, bl, s, datt, qoff, name, comm=None):
    n = bl * s
    nq, nb, qs, ks, vs = _attn_specs(bl, s, datt, qoff)
    srows = s + 2 * ATT_TQ
    scale = HEAD_DIM ** -0.5
    nc = 0 if comm is None else len(comm[0])

    def body(do_ref, q_ref, k0, k1, k2, v0, v1, v2, gq_ref, gk_ref, bias_ref, *rest):
        dq_ref, dk_ref, dv_ref, db_ref, dgq_ref = rest[nc:nc + 5]
        qt = pl.program_id(2)
        if comm is not None:
            start, wait = _exchange_ops(rest[:nc], rest[nc + 5:2 * nc + 5], rest[-2], rest[-1], comm[1], comm[2])
            step = (pl.program_id(0) * nb + pl.program_id(1)) * nq + qt

            @pl.when(step == 0)
            def _():
                start()

        @pl.when(qt == 0)
        def _():
            dk_ref[...] = jnp.zeros_like(dk_ref)
            dv_ref[...] = jnp.zeros_like(dv_ref)
            db_ref[...] = jnp.zeros_like(db_ref)
            dgq_ref[...] = jnp.zeros_like(dgq_ref)

        first, masks, qh, rq, kn, qas, ps = _attn_probs(q_ref, (k0, k1, k2), gq_ref, gk_ref, bias_ref)
        vc = jnp.concatenate([v0[...], v1[...], v2[...]], axis=0)
        dov = do_ref[...]
        dqn = jnp.zeros((ATT_TQ, LANES), F32)
        dkn = jnp.zeros((3 * ATT_TQ, LANES), F32)
        dv = jnp.zeros((3 * ATT_TQ, LANES), F32)
        for hh in range(2):
            doa = jnp.where(masks[hh], dov, jnp.zeros_like(dov))
            p = ps[hh]
            dp = _dot_t1(doa, vc)
            dsm = p * (dp - jnp.sum(dp * p, axis=-1, keepdims=True))
            db_ref[hh] += dsm
            dsc = (dsm * scale).astype(BF16)
            dqn = dqn + _dot(dsc, jnp.where(masks[hh], kn, jnp.zeros_like(kn)))
            dkn = dkn + _dot_t0(dsc, qas[hh])
            dv = dv + _dot_t0(p.astype(BF16), doa)
        start = pl.multiple_of(qt * ATT_TQ, ATT_TQ)
        dk_ref[pl.ds(start, 3 * ATT_TQ), :] += dkn
        dv_ref[pl.ds(start, 3 * ATT_TQ), :] += dv
        dgq_ref[...] += jnp.sum(dqn * qh, axis=0, keepdims=True)
        dqh = dqn * gq_ref[...]
        t = dqh * qh
        ma = jnp.sum(jnp.where(first, t, 0.0), axis=-1, keepdims=True) * (1.0 / HEAD_DIM)
        mb = jnp.sum(jnp.where(first, 0.0, t), axis=-1, keepdims=True) * (1.0 / HEAD_DIM)
        dq_ref[...] = (rq * (dqh - qh * jnp.where(first, ma, mb))).astype(BF16)
        if comm is not None:
            @pl.when(step == bl * nb * nq - 1)
            def _():
                wait()

    gs = pl.BlockSpec((1, LANES), lambda b, h, q: (0, 0))
    acc = pl.BlockSpec((None, srows, LANES), lambda b, h, q: (b, 0, h))
    comm_in = [] if comm is None else list(comm[0])
    comm_out = [] if comm is None else _parts_shapes(comm[0], comm[1])
    comm_scr = [] if comm is None else [pltpu.SemaphoreType.DMA((3 * nc,)), pltpu.SemaphoreType.DMA((3 * nc,))]
    return pl.pallas_call(
        body, grid=(bl, nb, nq),
        in_specs=[pl.BlockSpec((ATT_TQ, LANES), lambda b, h, q: (b * nq + q, h)), qs, *ks, *vs, gs, gs,
                  pl.BlockSpec((2, ATT_TQ, 3 * ATT_TQ), lambda b, h, q: (h, 0, 0))] + [_ANY] * nc,
        out_specs=[pl.BlockSpec((ATT_TQ, LANES), lambda b, h, q: (b * nq + q, h)), acc, acc,
                   pl.BlockSpec((None, 2, ATT_TQ, 3 * ATT_TQ), lambda b, h, q: (b, h, 0, 0)),
                   pl.BlockSpec((None, None, 1, LANES), lambda b, h, q: (b, h, 0, 0))] + [_ANY] * nc,
        out_shape=[_sds((n, datt), BF16), _sds((bl, srows, datt), F32), _sds((bl, srows, datt), F32),
                   _sds((bl, 2 * nb, ATT_TQ, 3 * ATT_TQ), F32), _sds((bl, nb, 1, LANES), F32)] + comm_out,
        scratch_shapes=comm_scr,
        compiler_params=_cp("arbitrary", "arbitrary", "arbitrary"), name=name)(
            do, proj, proj, proj, proj, proj, proj, proj, gq2, gk2, bias, *comm_in)


def _attn_kv_bwd(dkn, dv, proj, gk2, *, bl, s, datt, tm, koff, name):
    n = bl * s
    ns = s // tm
    off = 2 * ATT_TQ // tm
    nb = datt // LANES

    def body(dkn_ref, dv_ref, k_ref, gk_ref, dk_ref, dvo_ref, dgk_ref):
        lane = lax.broadcasted_iota(jnp.int32, (1, LANES), 1)
        first = lane < HEAD_DIM

        @pl.when((pl.program_id(0) == 0) & (pl.program_id(1) == 0) & (pl.program_id(2) == 0))
        def _():
            dgk_ref[...] = jnp.zeros_like(dgk_ref)

        dvo_ref[...] = dv_ref[...].astype(BF16)
        kh, rk = _head_norm(k_ref[...].astype(F32), first)
        dn = dkn_ref[...]
        dgk_ref[...] += jnp.sum(dn * kh, axis=0, keepdims=True)
        dh = dn * gk_ref[...]
        t = dh * kh
        ma = jnp.sum(jnp.where(first, t, 0.0), axis=-1, keepdims=True) * (1.0 / HEAD_DIM)
        mb = jnp.sum(jnp.where(first, 0.0, t), axis=-1, keepdims=True) * (1.0 / HEAD_DIM)
        dk_ref[...] = (rk * (dh - kh * jnp.where(first, ma, mb))).astype(BF16)

    accs = pl.BlockSpec((None, tm, LANES), lambda b, i, c: (b, i + off, c))
    outs = pl.BlockSpec((tm, LANES), lambda b, i, c: (b * ns + i, c))
    vec = pl.BlockSpec((1, LANES), lambda b, i, c: (0, 0))
    return pl.pallas_call(
        body, grid=(bl, ns, nb),
        in_specs=[accs, accs, pl.BlockSpec((tm, LANES), lambda b, i, c: (b * ns + i, koff + c)), vec],
        out_specs=[outs, outs, vec],
        out_shape=[_sds((n, datt), BF16), _sds((n, datt), BF16), _sds((1, LANES), F32)],
        compiler_params=_cp("arbitrary", "arbitrary", "arbitrary"), name=name)(dkn, dv, proj, gk2)


def _conv_fwd(proj, wdw, bdw, lng, lnb, *, bl, s, t, acol, name):
    n = bl * s
    dc = wdw.shape[1]
    nt = s // t
    hb = t // HALO

    def body(za_ref, zg_ref, ha_ref, hgt_ref, w_ref, b_ref, g_ref, be_ref, hg_ref, hc_ref, o_ref, ext):
        i = pl.program_id(1)
        hg = za_ref[...].astype(F32) * _sigmoid(zg_ref[...].astype(F32))
        live = jnp.where(i > 0, 1.0, 0.0)
        ext[0:HALO, :] = ha_ref[...].astype(F32) * _sigmoid(hgt_ref[...].astype(F32)) * live
        ext[HALO:HALO + t, :] = hg
        hg_ref[...] = hg
        acc = jnp.zeros((t, dc), F32) + b_ref[...]
        for j in range(CONV_W):
            acc = acc + w_ref[j:j + 1, :] * ext[pl.ds(HALO - (CONV_W - 1) + j, t), :]
        hc_ref[...] = acc
        mu = jnp.mean(acc, axis=-1, keepdims=True)
        xc = acc - mu
        rs = lax.rsqrt(jnp.mean(xc * xc, axis=-1, keepdims=True) + EPS)
        ln = xc * rs * g_ref[...] + be_ref[...]
        o_ref[...] = (ln * _sigmoid(ln)).astype(BF16)

    vec = pl.BlockSpec((1, dc), lambda b, i: (0, 0))
    row = pl.BlockSpec((t, dc), lambda b, i: (b * nt + i, 0))
    tile = lambda c: pl.BlockSpec((t, dc), lambda b, i: (b * nt + i, c))
    halo = lambda c: pl.BlockSpec((HALO, dc), lambda b, i: (jnp.maximum((b * nt + i) * hb - 1, 0), c))
    return pl.pallas_call(
        body, grid=(bl, nt),
        in_specs=[tile(acol), tile(acol + 1), halo(acol), halo(acol + 1),
                  pl.BlockSpec((HALO, dc), lambda b, i: (0, 0)), vec, vec, vec],
        out_specs=[row, row, row],
        out_shape=[_sds((n, dc), F32), _sds((n, dc), F32), _sds((n, dc), BF16)],
        scratch_shapes=[pltpu.VMEM((HALO + t, dc), F32)],
        compiler_params=_cp("parallel", "arbitrary"), name=name)(proj, proj, proj, proj, wdw, bdw, lng, lnb)


def _conv_bwd_ln(dco, hc, lng, lnb, *, tm, name):
    n, dc = hc.shape

    def body(d_ref, hc_ref, g_ref, be_ref, dhc_ref, dg_ref, db_ref):
        @pl.when(pl.program_id(0) == 0)
        def _():
            dg_ref[...] = jnp.zeros_like(dg_ref)
            db_ref[...] = jnp.zeros_like(db_ref)

        hcv = hc_ref[...]
        mu = jnp.mean(hcv, axis=-1, keepdims=True)
        xc = hcv - mu
        rs = lax.rsqrt(jnp.mean(xc * xc, axis=-1, keepdims=True) + EPS)
        xh = xc * rs
        ln = xh * g_ref[...] + be_ref[...]
        sg = _sigmoid(ln)
        dln = d_ref[...].astype(F32) * (sg * (1.0 + ln * (1.0 - sg)))
        db_ref[...] += jnp.sum(dln, axis=0, keepdims=True)
        dg_ref[...] += jnp.sum(dln * xh, axis=0, keepdims=True)
        dxh = dln * g_ref[...]
        dhc_ref[...] = rs * (dxh - jnp.mean(dxh, axis=-1, keepdims=True)
                             - xh * jnp.mean(dxh * xh, axis=-1, keepdims=True))

    vec = pl.BlockSpec((1, dc), lambda i: (0, 0))
    row = pl.BlockSpec((tm, dc), lambda i: (i, 0))
    return pl.pallas_call(
        body, grid=(n // tm,), in_specs=[row, row, vec, vec], out_specs=[row, vec, vec],
        out_shape=[_sds((n, dc), F32), _sds((1, dc), F32), _sds((1, dc), F32)],
        compiler_params=_cp("arbitrary"), name=name)(dco, hc, lng, lnb)


def _conv_bwd_dw(dhc, hg, proj, wdw, *, bl, s, t, acol, name):
    n = bl * s
    dc = wdw.shape[1]
    nt = s // t
    hb = t // HALO
    lastblk = n // HALO - 1

    def body(d_ref, dn_ref, hg_ref, hp_ref, za_ref, zg_ref, w_ref, dz_ref, dw_ref, dbias_ref, extd, exth):
        b = pl.program_id(0)
        i = pl.program_id(1)

        @pl.when((b == 0) & (i == 0))
        def _():
            dw_ref[...] = jnp.zeros_like(dw_ref)
            dbias_ref[...] = jnp.zeros_like(dbias_ref)

        dv = d_ref[...]
        extd[0:t, :] = dv
        extd[t:t + HALO, :] = dn_ref[...] * jnp.where(i < nt - 1, 1.0, 0.0)
        exth[0:HALO, :] = hp_ref[...] * jnp.where(i > 0, 1.0, 0.0)
        exth[HALO:HALO + t, :] = hg_ref[...]
        dbias_ref[...] += jnp.sum(dv, axis=0, keepdims=True)
        dhg = jnp.zeros((t, dc), F32)
        for j in range(CONV_W):
            dhg = dhg + w_ref[j:j + 1, :] * extd[pl.ds(CONV_W - 1 - j, t), :]
            dw_ref[j:j + 1, :] += jnp.sum(dv * exth[pl.ds(HALO - (CONV_W - 1) + j, t), :], axis=0, keepdims=True)
        za = za_ref[...].astype(F32)
        sg = _sigmoid(zg_ref[...].astype(F32))
        dz_ref[...] = jnp.concatenate([dhg * sg, dhg * za * sg * (1.0 - sg)], axis=1).astype(BF16)

    row = pl.BlockSpec((t, dc), lambda b, i: (b * nt + i, 0))
    nxt = pl.BlockSpec((HALO, dc), lambda b, i: (jnp.minimum((b * nt + i + 1) * hb, lastblk), 0))
    prv = pl.BlockSpec((HALO, dc), lambda b, i: (jnp.maximum((b * nt + i) * hb - 1, 0), 0))
    wsp = pl.BlockSpec((HALO, dc), lambda b, i: (0, 0))
    tile = lambda c: pl.BlockSpec((t, dc), lambda b, i: (b * nt + i, c))
    return pl.pallas_call(
        body, grid=(bl, nt),
        in_specs=[row, nxt, row, prv, tile(acol), tile(acol + 1), wsp],
        out_specs=[pl.BlockSpec((t, 2 * dc), lambda b, i: (b * nt + i, 0)), wsp,
                   pl.BlockSpec((1, dc), lambda b, i: (0, 0))],
        out_shape=[_sds((n, 2 * dc), BF16), _sds((HALO, dc), F32), _sds((1, dc), F32)],
        scratch_shapes=[pltpu.VMEM((t + HALO, dc), F32), pltpu.VMEM((HALO + t, dc), F32)],
        compiler_params=_cp("arbitrary", "arbitrary"), name=name)(dhc, dhc, hg, hg, proj, proj, wdw)


def _mix_out_fwd(x, brs, gl, bg, wbs, wout, l, *, tm, name):
    n, d = x.shape

    def body(x_ref, s_ref, a_ref, c_ref, g0, g1, g2, bg_ref, ws, wa, wc, wo, o_ref):
        merged = jnp.zeros((tm, d), F32)
        for k, (br, gr, w) in enumerate(((s_ref, g0, ws), (a_ref, g1, wa), (c_ref, g2, wc))):
            gate = _sigmoid(gr[...].astype(F32) + bg_ref[:, k * d:(k + 1) * d])
            merged = merged + gate * _dot(br[...], w[...])
        o_ref[...] = x_ref[...] + _dot(merged.astype(BF16), wo[...])

    row = lambda w: pl.BlockSpec((tm, w), lambda i: (i, 0))
    wsp = lambda a: pl.BlockSpec((None,) + a.shape[1:], lambda i: (l, 0, 0))
    gls = [pl.BlockSpec((tm, d), functools.partial(lambda k, i: (i, k), k)) for k in range(3)]
    return pl.pallas_call(
        body, grid=(n // tm,),
        in_specs=[row(d), *[row(b.shape[1]) for b in brs], *gls, pl.BlockSpec(bg.shape, lambda i: (0, 0)),
                  *[wsp(w) for w in wbs], wsp(wout)],
        out_specs=row(d), out_shape=_sds((n, d), F32),
        compiler_params=_cp("parallel"), name=name)(x, *brs, gl, gl, gl, bg, *wbs, wout)


def _mix_out_bwd(dx, brs, gl, bg, wbs, wout, l, nl, bufs, *, tm, name):
    n, d = dx.shape
    widths = [b.shape[1] for b in brs]

    def body(dx_ref, s_ref, a_ref, c_ref, g0, g1, g2, bg_ref, ws, wa, wc, wo, *rest):
        ds_ref, da_ref, dc_ref, dgl_ref, dbg_ref, dws, dwa, dwc, dwo = rest[-9:]

        @pl.when(pl.program_id(0) == 0)
        def _():
            for r in (dbg_ref, dws, dwa, dwc, dwo):
                r[...] = jnp.zeros_like(r)

        dxb = dx_ref[...].astype(BF16)
        dm = _dot_t1(dxb, wo[...])
        merged = jnp.zeros((tm, d), F32)
        for k, (br, gr, w, dbr, dw) in enumerate(((s_ref, g0, ws, ds_ref, dws), (a_ref, g1, wa, da_ref, dwa),
                                                   (c_ref, g2, wc, dc_ref, dwc))):
            gate = _sigmoid(gr[...].astype(F32) + bg_ref[:, k * d:(k + 1) * d])
            brv = br[...]
            wv = w[...]
            y = _dot(brv, wv)
            merged = merged + gate * y
            dyb = (dm * gate).astype(BF16)
            dbr[...] = _dot_t1(dyb, wv).astype(BF16)
            dw[...] += _dot_t0(brv, dyb)
            dgl = dm * y * gate * (1.0 - gate)
            dgl_ref[:, k * d:(k + 1) * d] = dgl.astype(BF16)
            dbg_ref[:, k * d:(k + 1) * d] += jnp.sum(dgl, axis=0, keepdims=True)
        dwo[...] += _dot_t0(merged.astype(BF16), dxb)

    row = lambda w: pl.BlockSpec((tm, w), lambda i: (i, 0))
    wsp = lambda shape: pl.BlockSpec((None,) + tuple(shape), lambda i: (l, 0, 0))
    gls = [pl.BlockSpec((tm, d), functools.partial(lambda k, i: (i, k), k)) for k in range(3)]
    slabs = [(w, d) for w in widths] + [(d, d)]
    n_in = 12
    extra = [] if bufs is None else list(bufs)
    aliases = {} if bufs is None else {n_in + k: 5 + k for k in range(4)}
    return pl.pallas_call(
        body, grid=(n // tm,),
        in_specs=[row(d), *[row(w) for w in widths], *gls, pl.BlockSpec(bg.shape, lambda i: (0, 0)),
                  *[wsp(w.shape[1:]) for w in wbs], wsp(wout.shape[1:]), *[_ANY for _ in extra]],
        out_specs=[*[row(w) for w in widths], row(3 * d), pl.BlockSpec((1, 3 * d), lambda i: (0, 0)),
                   *[wsp(sh) for sh in slabs]],
        out_shape=[*[_sds((n, w), BF16) for w in widths], _sds((n, 3 * d), BF16), _sds((1, 3 * d), F32),
                   *[_sds((nl,) + sh, F32) for sh in slabs]],
        input_output_aliases=aliases,
        compiler_params=_cp("arbitrary"), name=name)(dx, *brs, gl, gl, gl, bg, *wbs, wout, *extra)


def _adamw(w, g, m, v, *, name):
    r, c = w.shape
    tm = _tile(r, 256)
    c1 = 1.0 - ADAM_B1 ** ADAM_STEP
    c2 = 1.0 - ADAM_B2 ** ADAM_STEP

    def body(w_ref, g_ref, m_ref, v_ref, d_ref, nm_ref, nv_ref):
        gv = g_ref[...]
        mn = ADAM_B1 * m_ref[...] + (1.0 - ADAM_B1) * gv
        vn = ADAM_B2 * v_ref[...] + (1.0 - ADAM_B2) * (gv * gv)
        nm_ref[...] = mn
        nv_ref[...] = vn
        d_ref[...] = -ADAM_LR * ((mn / c1) / (jnp.sqrt(vn / c2) + ADAM_EPS) + ADAM_WD * w_ref[...])

    blk = pl.BlockSpec((tm, c), lambda i: (i, 0))
    return pl.pallas_call(
        body, grid=(r // tm,), in_specs=[blk] * 4, out_specs=[blk] * 3,
        out_shape=[_sds((r, c), F32)] * 3, compiler_params=_cp("parallel"), name=name)(w, g, m, v)


def _add_sibling(g, recv, lyr, c_idx, *, name):
    _, a, b = g.shape
    ta = _tile(a, 256)

    def body(c_ref, g_ref, r_ref, o_ref):
        o_ref[...] = (g_ref[...] + r_ref[...]).astype(BF16)

    row = lambda i, cr: jnp.where(cr[0] == lyr, i, 0)
    return pl.pallas_call(
        body,
        grid_spec=pltpu.PrefetchScalarGridSpec(
            num_scalar_prefetch=1, grid=(a // ta,),
            in_specs=[pl.BlockSpec((None, ta, b), lambda i, cr: (lyr, row(i, cr), 0)),
                      pl.BlockSpec((ta, b), lambda i, cr: (row(i, cr), 0))],
            out_specs=pl.BlockSpec((ta, b), lambda i, cr: (row(i, cr), 0))),
        out_shape=_sds((a, b), BF16), compiler_params=_cp("arbitrary"), name=name)(c_idx, g, recv)


def _add_chips(rsum, parts, axis, s_idx, c_idx, lyr, buf, *, name):
    _, a, b = parts.shape
    ta = _tile(a, 256)
    na = a // ta

    def body(s_ref, c_ref, own_ref, p0, p1, p2, p3, *rest):
        o_ref = rest[-1]
        own = own_ref[...].astype(F32)
        terms = [jnp.where(s_ref[0] == s, own, p[...].astype(F32)) for s, p in enumerate((p0, p1, p2, p3))]
        o_ref[...] = ((terms[0] + terms[1]) + terms[2]) + terms[3]

    row = lambda i, cr: jnp.where(cr[0] == lyr, i, 0)
    own_spec = (pl.BlockSpec((ta, b), lambda i, sr, cr: (sr[0] * na + row(i, cr), 0)) if axis == 1
                else pl.BlockSpec((ta, b), lambda i, sr, cr: (row(i, cr), sr[0])))
    part_spec = lambda s: pl.BlockSpec((None, ta, b),
                                       lambda i, sr, cr: (jnp.where(sr[0] == s, s ^ 1, s), row(i, cr), 0))
    extra, extra_specs, out_shape, aliases = _slab_out(2, lyr, (a, b), buf, 7)
    return pl.pallas_call(
        body,
        grid_spec=pltpu.PrefetchScalarGridSpec(
            num_scalar_prefetch=2, grid=(na,),
            in_specs=[own_spec] + [part_spec(s) for s in range(N_CHIPS)] + extra_specs,
            out_specs=pl.BlockSpec((None, ta, b), lambda i, sr, cr: (lyr, row(i, cr), 0))),
        out_shape=out_shape, input_output_aliases=aliases, compiler_params=_cp("arbitrary"), name=name)(
            s_idx, c_idx, rsum, parts, parts, parts, parts, *extra)


def _place_shard(wloc, axis, s_idx, *, name):
    nl, a, b = wloc.shape
    ta = _tile(a, 256)
    na = a // ta
    full = (nl, a * N_CHIPS, b) if axis == 1 else (nl, a, b * N_CHIPS)

    def body(sc_ref, w_ref, o_ref):
        o_ref[...] = w_ref[...].astype(BF16)

    out_spec = (pl.BlockSpec((None, ta, b), lambda l, i, sc: (l, sc[0] * na + i, 0)) if axis == 1
                else pl.BlockSpec((None, ta, b), lambda l, i, sc: (l, i, sc[0])))
    return pl.pallas_call(
        body,
        grid_spec=pltpu.PrefetchScalarGridSpec(
            num_scalar_prefetch=1, grid=(nl, na),
            in_specs=[pl.BlockSpec((None, ta, b), lambda l, i, sc: (l, i, 0))], out_specs=out_spec),
        out_shape=_sds(full, BF16), compiler_params=_cp("parallel", "parallel"), name=name)(s_idx, wloc)


def _blockdiag(w):
    g, r, c = w.shape
    eye = jnp.eye(g, dtype=w.dtype)
    return (w[:, :, None, :] * eye[:, None, :, None]).reshape(g * r, g * c)


def _s5_prep(lre, lim, log_dt, b_re, b_im, c_re, c_im, d_skip):
    lr = jnp.minimum(lre, -1e-4)
    li = lim
    dt = jnp.exp(log_dt)[:, None]
    mag = jnp.exp(lr * dt)
    ar = mag * jnp.cos(li * dt)
    ai = mag * jnp.sin(li * dt)
    den = lr * lr + li * li
    coef_r = ((ar - 1.0) * lr + ai * li) / den
    coef_i = (ai * lr - (ar - 1.0) * li) / den
    bbar_r = coef_r[..., None] * b_re - coef_i[..., None] * b_im
    bbar_i = coef_r[..., None] * b_im + coef_i[..., None] * b_re
    a = jnp.stack([ar.reshape(-1), ai.reshape(-1)])
    return dict(
        a=a,
        bblk_r=_blockdiag(bbar_r.transpose(0, 2, 1)), bblk_i=_blockdiag(bbar_i.transpose(0, 2, 1)),
        cblk_r=_blockdiag(c_re.transpose(0, 2, 1)), cblk_in=_blockdiag(-c_im.transpose(0, 2, 1)),
        d=d_skip.reshape(1, -1))


def _s5_powers(a, nlog):
    ar, ai = a[0], a[1]
    pr, pi = ar, ai
    rows = []
    for _ in range(nlog):
        rows += [pr, pi]
        pr, pi = pr * pr - pi * pi, 2.0 * pr * pi
    qr, qi = [ar], [ai]
    for _ in range(SUBLANES - 1):
        qr, qi = qr + [qr[-1] * ar - qi[-1] * ai], qi + [qr[-1] * ai + qi[-1] * ar]
    p8 = jnp.stack(qr + qi)
    q8 = jnp.stack(qr[::-1] + [-v for v in qi[::-1]])
    return jnp.stack(rows), p8, q8


def _bias_table(rel_bias):
    h = rel_bias.shape[0]
    tq, tw = ATT_TQ, 3 * ATT_TQ
    n_hi = tw - 1 - MAX_REL + 1
    n_lo = tq + tw - 1 - n_hi - (2 * MAX_REL - 1)
    fr = jnp.concatenate([
        jnp.broadcast_to(rel_bias[:, 2 * MAX_REL:], (h, n_hi)),
        jnp.flip(rel_bias[:, 1:2 * MAX_REL], axis=1),
        jnp.broadcast_to(rel_bias[:, :1], (h, n_lo)),
        jnp.zeros((h, 1), rel_bias.dtype)], axis=1)
    ln = tq + tw
    flat = jnp.broadcast_to(fr[:, None, :], (h, tq, ln)).reshape(h, tq * ln)[:, :tq * (ln - 1)]
    tab = flat.reshape(h, tq, ln - 1)[:, :, tq - 1:tq - 1 + tw]
    qc = np.arange(tq)[:, None] // CHUNK + N_LEFT
    kc = np.arange(tw)[None, :] // CHUNK
    band = (kc <= qc) & (kc >= qc - N_LEFT)
    return jnp.where(jnp.asarray(band)[None], tab, NEG)


def _small_prep(w, l):
    g, p = w["s5_lambda_re"].shape[1:]
    b_shape, c_shape = (g, p, -1), (g, -1, p)
    sp = _s5_prep(w["s5_lambda_re"][l], w["s5_lambda_im"][l], w["s5_log_dt"][l], w["s5_b_re"][l].reshape(b_shape),
                  w["s5_b_im"][l].reshape(b_shape), w["s5_c_re"][l].reshape(c_shape), w["s5_c_im"][l].reshape(c_shape),
                  w["s5_d"][l])
    return sp, _bias_table(w["attn_rel_bias"][l])


_PREP_KEYS = ("s5_lambda_re", "s5_lambda_im", "s5_log_dt", "s5_b_re", "s5_b_im", "s5_c_re", "s5_c_im", "s5_d",
              "attn_rel_bias")
_BIG_KEYS = {"ffn1_w_up": 2, "ffn1_w_down": 1, "w_in": 2, "s5_w_glu": 2, "w_br_s5": 2, "w_br_attn": 2,
             "w_br_conv": 2, "w_out": 1, "ffn2_w_up": 2, "ffn2_w_down": 1}
_SMALL_KEYS = ("ffn1_norm", "mix_norm", "b_gate", "s5_lambda_re", "s5_lambda_im", "s5_log_dt", "s5_b_re", "s5_b_im",
               "s5_c_re", "s5_c_im", "s5_d", "attn_q_gain", "attn_k_gain", "attn_rel_bias", "conv_w_dw", "conv_b_dw",
               "conv_ln_g", "conv_ln_b", "ffn2_norm")
_WEIGHTS = ("ffn1_norm", "ffn1_w_up", "ffn1_w_down", "mix_norm", "w_in", "b_gate", "s5_lambda_re", "s5_lambda_im",
            "s5_log_dt", "s5_b_re", "s5_b_im", "s5_c_re", "s5_c_im", "s5_d", "s5_w_glu", "w_br_s5", "attn_q_gain",
            "attn_k_gain", "attn_rel_bias", "w_br_attn", "conv_w_dw", "conv_b_dw", "conv_ln_g", "conv_ln_b",
            "w_br_conv", "w_out", "ffn2_norm", "ffn2_w_up", "ffn2_w_down")


def _local_step(x3, target3, w, rs=None, gather=None):
    w = dict(w)
    bl, s, d = x3.shape
    nl = w["ffn1_norm"].shape[0]
    dff = w["ffn1_w_down"].shape[1]
    ds5 = w["s5_d"].shape[1]
    datt = w["w_br_attn"].shape[1]
    dc = w["conv_b_dw"].shape[1]
    n = bl * s
    x = x3.reshape(n, d)
    target = target3.reshape(n, d)
    tm = _tile(n, 512)
    tml = _tile(n, 1024)
    tmix = _tile(n, 256)
    ts5 = 256
    tconv = _tile(s, 512)
    tff = dff // 2
    ma = ds5 + 3 * datt + 2 * dc
    tna = ma // 3
    assert (3 * d) % tna == 0 and dff % 2 == 0
    qoff = ds5 // LANES
    koff = (ds5 + datt) // LANES
    acol = (ds5 + 3 * datt) // dc
    wbs = lambda: (w["w_br_s5"], w["w_br_attn"], w["w_br_conv"])

    def host(tag, l):
        if gather is None or l != 0:
            return None
        keys, kaxes, lyr = gather[tag]
        return [w[k] for k in keys], kaxes, lyr

    def hosted(tag, l, arrays):
        if gather is not None and l == 0:
            w.update(zip(gather[tag][0], arrays))

    saved = []
    for l in range(nl):
        (sp, bias), prep_vjp = jax.vjp(lambda ww: _small_prep(ww, l), {k: w[k] for k in _PREP_KEYS})
        spb = dict(sp)
        spb["pw"], spb["p8"], spb["q8"] = _s5_powers(lax.stop_gradient(sp["a"]), int(math.log2(ts5)))
        for k in ("bblk_r", "bblk_i", "cblk_r", "cblk_in"):
            spb[k] = sp[k].astype(BF16)
        g1 = w["ffn1_norm"][l][None]
        g2 = w["ffn2_norm"][l][None]
        gm = w["mix_norm"][l][None]
        gq2 = jnp.tile(w["attn_q_gain"][l], 2)[None]
        gk2 = jnp.tile(w["attn_k_gain"][l], 2)[None]
        wdw = jnp.pad(w["conv_w_dw"][l], ((0, HALO - CONV_W), (0, 0)))
        bdw, lng, lnb = w["conv_b_dw"][l][None], w["conv_ln_g"][l][None], w["conv_ln_b"][l][None]
        bg = w["b_gate"][l][None]

        x0 = x
        h1, ab1, *got = _norm_mm(x0, g1, w["ffn1_w_up"], l, tm=tml, tn=tff, ntiles=4, pieces=2, transposed=False,
                                 name=f"ffn1_up_{l}", comm=host("ffn1_up", l))
        hosted("ffn1_up", l, got)
        x1 = _ffn_down(ab1, w["ffn1_w_down"], l, x0, tm=tm, tk=tff, name=f"ffn1_down_{l}")
        h2, pa = _norm_mm(x1, gm, w["w_in"], l, tm=tml, tn=tna, ntiles=3, pieces=1, transposed=True, name=f"win_a_{l}")
        pa = pa[0]
        gl = _mm_t(h2, w["w_in"], l, tm=tml, tn=tna, off=3, ntiles=3 * d // tna, name=f"win_g_{l}")
        xr, xi, yp, zg, s5o = _s5_fwd(pa, spb, w["s5_w_glu"], l, bl=bl, s=s, t=ts5, name=f"s5_fwd_{l}")
        atto, got = _attn_fwd(pa, gq2, gk2, bias, bl=bl, s=s, datt=datt, qoff=qoff, name=f"attn_fwd_{l}",
                              comm=host("attn_fwd", l))
        hosted("attn_fwd", l, got)
        hg, hc, convo = _conv_fwd(pa, wdw, bdw, lng, lnb, bl=bl, s=s, t=tconv, acol=acol, name=f"conv_fwd_{l}")
        brs = (s5o, atto, convo)
        x2 = _mix_out_fwd(x1, brs, gl, bg, wbs(), w["w_out"], l, tm=tmix, name=f"mix_fwd_{l}")
        h3, ab2, *got = _norm_mm(x2, g2, w["ffn2_w_up"], l, tm=tml, tn=tff, ntiles=4, pieces=2, transposed=False,
                                 name=f"ffn2_up_{l}", comm=host("ffn2_up", l))
        hosted("ffn2_up", l, got)
        x = _ffn_down(ab2, w["ffn2_w_down"], l, x2, tm=tm, tk=tff, name=f"ffn2_down_{l}")
        saved.append(dict(spb=spb, bias=bias, prep_vjp=prep_vjp, g1=g1, g2=g2, gm=gm, gq2=gq2, gk2=gk2,
                          wdw=wdw, lng=lng, lnb=lnb, bg=bg, x0=x0, h1=h1, ab1=ab1, x1=x1, h2=h2, pa=pa, gl=gl,
                          xr=xr, xi=xi, yp=yp, zg=zg, hg=hg, hc=hc, brs=brs, x2=x2, h3=h3, ab2=ab2))

    dx, lsum = _loss_grad(x, target, tm=tm, name="loss")
    loss_part = 0.5 * jnp.sum(lsum) / d

    big = {k: None for k in _BIG_KEYS}
    small = {k: [None] * nl for k in _SMALL_KEYS}
    hooks = {"pending": None, "early": None}
    assert rs is None or nl == 2
    for l in reversed(range(nl)):
        sv = saved[l]

        def ffn_bwd(dx, xin, h, ab, g, tag):
            wu, wd = w[tag + "_w_up"], w[tag + "_w_down"]
            comm = hooks["early"] if (tag == "ffn1" and l == 0) else None
            dab, big[tag + "_w_down"], *parts = _ffn_dact(dx, wd, l, ab, nl, big[tag + "_w_down"], tm=tm, tk=tff,
                                                          name=f"{tag}_dact_{l}", comm=comm)
            if comm is not None:
                rs.parts_early = parts
            big[tag + "_w_up"] = _mm_tn(h[None], dab, l, nl, big[tag + "_w_up"], ta=d, tb=tff, tk=tml,
                                        name=f"{tag}_dwu_{l}")
            comm = None
            if rs is not None and l == 0:
                keys = list(_BIG_KEYS) if tag == "ffn2" else rs.late
                comm = ([big[k] for k in keys], 1 if tag == "ffn2" else 0)
            dxo, dg, *recv = _ffn_dx(dab, wu, l, xin, g, dx, tm=tml, tk=tff, name=f"{tag}_dx_{l}", comm=comm)
            small[tag + "_norm"][l] = dg[0]
            if comm is not None and tag == "ffn2":
                hooks["pending"] = rs.sums(big, recv, 1, keys)
            elif comm is not None:
                rs.recv_late = recv
            return dxo

        dx = ffn_bwd(dx, sv["x2"], sv["h3"], sv["ab2"], sv["g2"], "ffn2")

        mix_keys = ("w_br_s5", "w_br_attn", "w_br_conv", "w_out")
        bufs = None if big["w_out"] is None else [big[k] for k in mix_keys]
        ds5o, datto, dconvo, dgl, dbg, *dws = _mix_out_bwd(
            dx, sv["brs"], sv["gl"], sv["bg"], wbs(), w["w_out"], l, nl, bufs, tm=tmix, name=f"mix_bwd_{l}")
        small["b_gate"][l] = dbg[0]
        big.update(zip(mix_keys, dws))

        dhc, dlng, dlnb = _conv_bwd_ln(dconvo, sv["hc"], sv["lng"], sv["lnb"], tm=tm, name=f"conv_bwd_ln_{l}")
        dz, dwdw, dbdw = _conv_bwd_dw(dhc, sv["hg"], sv["pa"], sv["wdw"], bl=bl, s=s, t=tconv, acol=acol,
                                      name=f"conv_bwd_dw_{l}")
        small["conv_w_dw"][l] = dwdw[:CONV_W]
        small["conv_b_dw"][l], small["conv_ln_g"][l], small["conv_ln_b"][l] = dbdw[0], dlng[0], dlnb[0]

        comm = hooks["pending"] if l == 0 else None
        dq, dkn, dvw, dbias, dgq, *hosted = _attn_bwd(datto, sv["pa"], sv["gq2"], sv["gk2"], sv["bias"], bl=bl, s=s,
                                                      datt=datt, qoff=qoff, name=f"attn_bwd_{l}", comm=comm)
        if comm is not None:
            rs.parts = hosted
        dk, dv, dgk = _attn_kv_bwd(dkn, dvw, sv["pa"], sv["gk2"], bl=bl, s=s, datt=datt, tm=_tile(s, 512), koff=koff,
                                   name=f"attn_kv_bwd_{l}")
        small["attn_q_gain"][l] = jnp.sum(dgq.reshape(-1, HEAD_DIM), axis=0)
        small["attn_k_gain"][l] = jnp.sum(dgk.reshape(-1, HEAD_DIM), axis=0)

        du, dd, dcr, dci, dbr, dbi, da, big["s5_w_glu"] = _s5_bwd(
            ds5o, sv["yp"], sv["zg"], sv["xr"], sv["xi"], sv["pa"], sv["spb"], w["s5_w_glu"], l, nl, big["s5_w_glu"],
            bl=bl, s=s, t=ts5, name=f"s5_bwd_{l}")
        prep_ct = (dict(a=da, bblk_r=dbr, bblk_i=dbi, cblk_r=dcr, cblk_in=dci, d=dd), jnp.sum(dbias, axis=0))
        (dprep,) = sv["prep_vjp"](prep_ct)
        for k in _PREP_KEYS:
            small[k][l] = dprep[k][l]

        dpa = jnp.concatenate([du, dq, dk, dv, dz], axis=1)
        big["w_in"] = _dwin_t(dpa, dgl, sv["h2"], l, nl, big["w_in"], ta=tna, tk=tml, name=f"dwin_{l}")
        comm = ([big[k] for k in rs.early], 0) if (rs is not None and l == 0) else None
        dx, dgm, *recv = _mix_dx(dpa, dgl, w["w_in"], l, sv["x1"], sv["gm"], dx, tm=tml, tk=tna, name=f"mix_dx_{l}",
                                 comm=comm)
        small["mix_norm"][l] = dgm[0]
        if comm is not None:
            hooks["early"] = rs.sums(big, recv, 0, rs.early)

        dx = ffn_bwd(dx, sv["x0"], sv["h1"], sv["ab1"], sv["g1"], "ffn1")

    small = {k: jnp.stack(v) for k, v in small.items()}
    return loss_part, dx.reshape(bl, s, d), big, small


def _place():
    x, y, c = lax.axis_index("x"), lax.axis_index("y"), lax.axis_index("c")
    chips = [(1 - x, y), (x, 1 - y), (1 - x, 1 - y)]
    return x, y, c, chips


def _remote(src, dst, send_sems, recv_sems, k, dev):
    return pltpu.make_async_remote_copy(src_ref=src, dst_ref=dst, send_sem=send_sems.at[k], recv_sem=recv_sems.at[k],
                                        device_id=dev, device_id_type=MESH)


def _window(ref, lead, s, axis, blk):
    if axis == 1:
        sl = (pl.ds(pl.multiple_of(s * blk, 16), blk), slice(None))
    else:
        sl = (slice(None), pl.ds(pl.multiple_of(s * blk, LANES), blk))
    return ref.at[sl] if lead is None else ref.at[(lead,) + sl]


def _gather_ops(bufs, axes, send_sems, recv_sems, lyr):
    x, y, c, chips = _place()
    s_me = 2 * x + y
    sibling = (x, y, 1 - lyr)
    nw = len(bufs)
    blks = [f.shape[ax] // N_CHIPS for f, ax in zip(bufs, axes)]
    win = lambda i, s: _window(bufs[i], lyr, s, axes[i], blks[i])
    pairs = [(i, j, cx, cy) for i in range(nw) for j, (cx, cy) in enumerate(chips)]
    sends = [_remote(win(i, s_me), win(i, s_me), send_sems, recv_sems, 6 * i + j, (cx, cy, lyr)) for i, j, cx, cy in pairs]
    passed = [_remote(win(i, 2 * cx + cy), win(i, 2 * cx + cy), send_sems, recv_sems, 6 * i + 3 + j, sibling)
              for i, j, cx, cy in pairs]

    def start():
        @pl.when(c == lyr)
        def _():
            for cp in sends:
                cp.start()

    def forward():
        @pl.when(c == lyr)
        def _():
            for (i, j, cx, cy), fw in zip(pairs, passed):
                piece = win(i, 2 * cx + cy)
                _remote(piece, piece, send_sems, recv_sems, 6 * i + j, (cx, cy, lyr)).wait_recv()
                fw.start()

    def wait():
        @pl.when(c == lyr)
        def _():
            for cp in sends + passed:
                cp.wait_send()

        @pl.when(c != lyr)
        def _():
            for fw in passed:
                fw.wait_recv()

    return start, forward, wait


def _all_gather_weights(fulls, axes, taps):
    nw = len(fulls)

    def body(*refs):
        taps_in = refs[nw]
        outs, taps_out = refs[nw + 1:2 * nw + 1], refs[2 * nw + 1]
        send_sems, recv_sems, tap_send, tap_recv, local_sem = refs[-5:]
        x, y, c, chips = _place()
        s_me = 2 * x + y
        start, forward, wait = _gather_ops(outs, axes, send_sems, recv_sems, 0)
        own_taps = pltpu.make_async_copy(taps_in, taps_out.at[s_me], local_sem)
        own_taps.start()
        tap_sends = [_remote(taps_in, taps_out.at[s_me], tap_send, tap_recv, j, (cx, cy, c))
                     for j, (cx, cy) in enumerate(chips)]
        for cp in tap_sends:
            cp.start()
        start()
        forward()
        wait()
        for j, (cx, cy) in enumerate(chips):
            slab = taps_out.at[2 * cx + cy]
            _remote(slab, slab, tap_send, tap_recv, j, (cx, cy, c)).wait_recv()
        for cp in tap_sends:
            cp.wait_send()
        own_taps.wait()

    return pl.pallas_call(
        body, in_specs=[_ANY] * (nw + 1), out_specs=[_ANY] * (nw + 1),
        out_shape=[_sds(f.shape, f.dtype) for f in fulls] + [_sds((N_CHIPS,) + taps.shape, taps.dtype)],
        input_output_aliases={i: i for i in range(nw)},
        scratch_shapes=[pltpu.SemaphoreType.DMA((6 * nw,)), pltpu.SemaphoreType.DMA((6 * nw,)),
                        pltpu.SemaphoreType.DMA((3,)), pltpu.SemaphoreType.DMA((3,)), pltpu.SemaphoreType.DMA],
        name="all_gather_weights")(*fulls, taps)


def _hosted_gather(comm, n_in, n_out):
    if comm is None:
        return [], [], [], [], [], {}
    bufs = list(comm[0])
    nw = len(bufs)
    return (bufs, [_ANY] * nw, [_ANY] * nw, [_sds(f.shape, f.dtype) for f in bufs],
            [pltpu.SemaphoreType.DMA((6 * nw,)), pltpu.SemaphoreType.DMA((6 * nw,))],
            {n_in + k: n_out + k for k in range(nw)})


def _hosted_gather_steps(comm, out_refs, sems, step, total):
    start, forward, wait = _gather_ops(out_refs, comm[1], sems[0], sems[1], comm[2])

    @pl.when(step == 0)
    def _():
        start()

    def finish():
        @pl.when(step == (3 * total) // 4)
        def _():
            forward()

        @pl.when(step == total - 1)
        def _():
            wait()

    return finish


def _swap_ops(ins, outs, send_sems, recv_sems, lyr):
    x, y, c, _ = _place()
    cps = [_remote(ins[i].at[lyr], outs[i], send_sems, recv_sems, i, (x, y, lyr)) for i in range(len(ins))]

    def start():
        @pl.when(c != lyr)
        def _():
            for cp in cps:
                cp.start()

    def wait():
        @pl.when(c != lyr)
        def _():
            for cp in cps:
                cp.wait_send()

        @pl.when(c == lyr)
        def _():
            for cp in cps:
                cp.wait_recv()

    return start, wait


def _exchange_ops(ins, outs, send_sems, recv_sems, axes, lyr):
    x, y, c, chips = _place()
    s_me = 2 * x + y
    nw = len(ins)
    blks = [r.shape[ax - 1] // N_CHIPS for r, ax in zip(ins, axes)]
    win = lambda i, s: _window(ins[i], None, s, axes[i], blks[i])
    sends = [_remote(win(i, 2 * cx + cy), outs[i].at[s_me], send_sems, recv_sems, 3 * i + j, (cx, cy, lyr))
             for i in range(nw) for j, (cx, cy) in enumerate(chips)]

    def start():
        @pl.when(c == lyr)
        def _():
            for cp in sends:
                cp.start()

    def wait():
        @pl.when(c == lyr)
        def _():
            for i in range(nw):
                for j, (cx, cy) in enumerate(chips):
                    slab = outs[i].at[2 * cx + cy]
                    _remote(slab, slab, send_sems, recv_sems, 3 * i + j, (cx, cy, lyr)).wait_recv()
            for cp in sends:
                cp.wait_send()

    return start, wait


def _parts_shapes(rsums, axes):
    shard = [tuple(dim // N_CHIPS if i == ax - 1 else dim for i, dim in enumerate(r.shape)) for r, ax in zip(rsums, axes)]
    return [_sds((N_CHIPS,) + sh, r.dtype) for sh, r in zip(shard, rsums)]


def _rs_exchange(rsums, axes, lyr):
    nw = len(rsums)

    def body(*refs):
        start, wait = _exchange_ops(refs[:nw], refs[nw:2 * nw], refs[-2], refs[-1], axes, lyr)
        start()
        wait()

    return pl.pallas_call(
        body, in_specs=[_ANY] * nw, out_specs=[_ANY] * nw, out_shape=_parts_shapes(rsums, axes),
        scratch_shapes=[pltpu.SemaphoreType.DMA((3 * nw,)), pltpu.SemaphoreType.DMA((3 * nw,))],
        name=f"rs_exchange_l{lyr}")(*rsums)


def _rs_join(ts):
    nw = len(ts)

    def body(*refs):
        outs = refs[nw:2 * nw]
        send_sems, recv_sems = refs[-2:]
        x, y, c, _ = _place()
        sends = [_remote(outs[i].at[c], outs[i].at[c], send_sems, recv_sems, i, (x, y, 1 - c)) for i in range(nw)]
        for cp in sends:
            cp.start()
        for i in range(nw):
            slab = outs[i].at[1 - c]
            _remote(slab, slab, send_sems, recv_sems, i, (x, y, 1 - c)).wait_recv()
        for cp in sends:
            cp.wait_send()

    return pl.pallas_call(
        body, in_specs=[_ANY] * nw, out_specs=[_ANY] * nw, out_shape=[_sds(t.shape, t.dtype) for t in ts],
        input_output_aliases={i: i for i in range(nw)},
        scratch_shapes=[pltpu.SemaphoreType.DMA((nw,)), pltpu.SemaphoreType.DMA((nw,))],
        name="rs_join_layers")(*ts)


def _all_reduce_small(arrs):
    na = len(arrs)
    nd = 8

    def body(*refs):
        ins, outs, recvs = refs[:na], refs[na:2 * na], refs[2 * na:3 * na]
        send_sems, recv_sems = refs[-2:]
        x, y, c, _ = _place()
        me = 4 * x + 2 * y + c
        for i in range(na):
            recvs[i][0] = ins[i][...]
        cps = []
        for rel in range(1, nd):
            dev = (1 - x if rel & 4 else x, 1 - y if rel & 2 else y, 1 - c if rel & 1 else c)
            for i in range(na):
                cp = _remote(ins[i], recvs[i].at[rel], send_sems, recv_sems, (rel - 1) * na + i, dev)
                cp.start()
                cps.append(cp)
        for rel in range(1, nd):
            for i in range(na):
                _remote(ins[i], recvs[i].at[rel], send_sems, recv_sems, (rel - 1) * na + i, (x, y, c)).wait_recv()
        for i in range(na):
            acc = recvs[i][me]
            for dv in range(1, nd):
                acc = acc + recvs[i][lax.bitwise_xor(me, dv)]
            outs[i][...] = acc
        for cp in cps:
            cp.wait_send()

    vm = pl.BlockSpec(memory_space=pltpu.VMEM)
    nsem = (nd - 1) * na
    return pl.pallas_call(
        body, in_specs=[vm] * na, out_specs=[vm] * na, out_shape=[_sds(t.shape, F32) for t in arrs],
        scratch_shapes=[pltpu.VMEM((nd,) + t.shape, F32) for t in arrs]
        + [pltpu.SemaphoreType.DMA((nsem,)), pltpu.SemaphoreType.DMA((nsem,))],
        compiler_params=pltpu.CompilerParams(vmem_limit_bytes=VMEM_LIMIT), name="all_reduce_small")(*arrs)


def _adamw_small(ws, gs, ms, vs):
    na = len(ws)
    c1 = 1.0 - ADAM_B1 ** ADAM_STEP
    c2 = 1.0 - ADAM_B2 ** ADAM_STEP

    def body(*refs):
        w_r, g_r, m_r, v_r = (refs[k * na:(k + 1) * na] for k in range(4))
        d_r, nm_r, nv_r = (refs[(4 + k) * na:(5 + k) * na] for k in range(3))
        for i in range(na):
            gv = g_r[i][...]
            mn = ADAM_B1 * m_r[i][...] + (1.0 - ADAM_B1) * gv
            vn = ADAM_B2 * v_r[i][...] + (1.0 - ADAM_B2) * (gv * gv)
            nm_r[i][...] = mn
            nv_r[i][...] = vn
            d_r[i][...] = -ADAM_LR * ((mn / c1) / (jnp.sqrt(vn / c2) + ADAM_EPS) + ADAM_WD * w_r[i][...])

    vm = pl.BlockSpec(memory_space=pltpu.VMEM)
    res = pl.pallas_call(
        body, in_specs=[vm] * (4 * na), out_specs=[vm] * (3 * na), out_shape=[_sds(t.shape, F32) for t in ws] * 3,
        compiler_params=pltpu.CompilerParams(vmem_limit_bytes=VMEM_LIMIT), name="adamw_small")(*ws, *gs, *ms, *vs)
    return res[:na], res[na:2 * na], res[2 * na:]


def kernel(x, ffn1_norm, ffn1_w_up, ffn1_w_down, mix_norm, w_in, b_gate, s5_lambda_re, s5_lambda_im, s5_log_dt, s5_b_re, s5_b_im, s5_c_re, s5_c_im, s5_d, s5_w_glu, w_br_s5, attn_q_gain, attn_k_gain, attn_rel_bias, w_br_attn, conv_w_dw, conv_b_dw, conv_ln_g, conv_ln_b, w_br_conv, w_out, ffn2_norm, ffn2_w_up, ffn2_w_down, loss_target, m_ffn1_norm, m_ffn1_w_up, m_ffn1_w_down, m_mix_norm, m_w_in, m_b_gate, m_s5_lambda_re, m_s5_lambda_im, m_s5_log_dt, m_s5_b_re, m_s5_b_im, m_s5_c_re, m_s5_c_im, m_s5_d, m_s5_w_glu, m_w_br_s5, m_attn_q_gain, m_attn_k_gain, m_attn_rel_bias, m_w_br_attn, m_conv_w_dw, m_conv_b_dw, m_conv_ln_g, m_conv_ln_b, m_w_br_conv, m_w_out, m_ffn2_norm, m_ffn2_w_up, m_ffn2_w_down, v_ffn1_norm, v_ffn1_w_up, v_ffn1_w_down, v_mix_norm, v_w_in, v_b_gate, v_s5_lambda_re, v_s5_lambda_im, v_s5_log_dt, v_s5_b_re, v_s5_b_im, v_s5_c_re, v_s5_c_im, v_s5_d, v_s5_w_glu, v_w_br_s5, v_attn_q_gain, v_attn_k_gain, v_attn_rel_bias, v_w_br_attn, v_conv_w_dw, v_conv_b_dw, v_conv_ln_g, v_conv_ln_b, v_w_br_conv, v_w_out, v_ffn2_norm, v_ffn2_w_up, v_ffn2_w_down):
    a = dict(locals())
    xi, yi, ci = lax.axis_index("x"), lax.axis_index("y"), lax.axis_index("c")
    s_me = 2 * xi + yi
    big_keys = list(_BIG_KEYS)
    axes = [1 if k == "w_in" else _BIG_KEYS[k] for k in big_keys]

    s_idx = s_me.astype(jnp.int32).reshape(1)
    c_idx = ci.astype(jnp.int32).reshape(1)
    placed = {k: _place_shard(jnp.swapaxes(a[k], 1, 2).astype(BF16) if k == "w_in" else a[k], ax, s_idx,
                              name=f"place_{k}") for k, ax in zip(big_keys, axes)}
    axis_of = dict(zip(big_keys, axes))
    first = ["ffn1_w_up", "ffn1_w_down"]
    *fulls, taps = _all_gather_weights([placed[k] for k in first], [axis_of[k] for k in first], a["conv_w_dw"])
    placed.update(zip(first, fulls))
    flat = lambda t: t.reshape(t.shape[0], t.shape[1], -1) if t.ndim == 4 else t
    w = {k: flat(a[k]) for k in _WEIGHTS}
    w.update(placed)
    w["conv_w_dw"] = jnp.moveaxis(taps, 0, 2).reshape(taps.shape[1], taps.shape[2], -1)

    class _ReduceScatter:
        parts = parts_early = recv_late = None

        def __init__(self):
            self.rsums = {}
            self.late = ["ffn1_w_up", "ffn1_w_down"]
            self.early = [k for k in big_keys if k not in self.late]

        def sums(self, big, recv, lyr, keys):
            rsums = [_add_sibling(big[k], r, lyr, c_idx, name=f"rs_add_sibling_{k}_l{lyr}") for r, k in zip(recv, keys)]
            self.rsums.update({(k, lyr): r for k, r in zip(keys, rsums)})
            return rsums, [axis_of[k] for k in keys], lyr

    rs = _ReduceScatter()
    early = ("ffn1_w_up", "ffn1_w_down", "w_in")
    sets = {"ffn1_up": ([k for k in big_keys if k not in first], 0),
            "attn_fwd": ([k for k in big_keys if k in early], 1),
            "ffn2_up": ([k for k in big_keys if k not in early], 1)}
    gather = {tag: (keys, [axis_of[k] for k in keys], lyr) for tag, (keys, lyr) in sets.items()}
    loss_part, grad_x, gbig, gsmall = _local_step(a["x"], a["loss_target"], w, rs, gather)
    loss = lax.psum(loss_part, ("x", "y", "c"))

    rsums_late, axes_late, _ = rs.sums(gbig, rs.recv_late, 0, rs.late)
    parts0 = dict(zip(rs.early, rs.parts_early))
    parts0.update(zip(rs.late, _rs_exchange(rsums_late, axes_late, 0)))
    mine = [None] * len(big_keys)
    for lyr, parts in ((1, dict(zip(big_keys, rs.parts))), (0, parts0)):
        mine = [_add_chips(rs.rsums[k, lyr], parts[k], ax, s_idx, c_idx, lyr, buf, name=f"rs_add_chips_{k}_l{lyr}")
                for ax, buf, k in zip(axes, mine, big_keys)]
    gb = dict(zip(big_keys, _rs_join(mine)))
    gb["w_in"] = jnp.swapaxes(gb["w_in"], 1, 2)

    small_keys = list(_SMALL_KEYS)
    gs = dict(zip(small_keys, _all_reduce_small([gsmall[k] for k in small_keys])))
    blk = a["conv_w_dw"].shape[2]
    gs["conv_w_dw"] = lax.dynamic_slice_in_dim(gs["conv_w_dw"], s_me * blk, blk, axis=2)

    delta, new_m, new_v = {}, {}, {}
    for k in big_keys:
        shp = a[k].shape
        two_d = lambda t: t.reshape(-1, shp[-1])
        d_, m_, v_ = _adamw(two_d(a[k]), two_d(gb[k]), two_d(a["m_" + k]), two_d(a["v_" + k]), name=f"adamw_{k}")
        delta[k], new_m[k], new_v[k] = d_.reshape(shp), m_.reshape(shp), v_.reshape(shp)
    res = _adamw_small([flat(a[k]) for k in small_keys], [gs[k] for k in small_keys],
                       [flat(a["m_" + k]) for k in small_keys], [flat(a["v_" + k]) for k in small_keys])
    for dst, vals in zip((delta, new_m, new_v), res):
        dst.update({k: t.reshape(a[k].shape) for k, t in zip(small_keys, vals)})
    grads = {**gb, **{k: t.reshape(a[k].shape) for k, t in gs.items()}}

    return (loss, grad_x, *[grads[k] for k in _WEIGHTS], *[delta[k] for k in _WEIGHTS],
            *[new_m[k] for k in _WEIGHTS], *[new_v[k] for k in _WEIGHTS])
```

```python
import functools
import math

import numpy as np
import jax
import jax.numpy as jnp
from jax import lax
from jax.experimental import pallas as pl
from jax.experimental.pallas import tpu as pltpu

F32 = jnp.float32
BF16 = jnp.bfloat16
EPS = 1e-6
VMEM_LIMIT = 56 * 1024 * 1024
LANES = 128
HEAD_DIM = 64
CHUNK = 64
N_LEFT = 8
MAX_REL = 128
ATT_TQ = 256
CONV_W = 31
HALO = 32
ROW_CHUNK = 256
SUBLANES = 8
GROUP_LOG = 3
NEG = -1e30
N_CHIPS = 4
PACK_COLS = 1024

ADAM_LR = 0.001
ADAM_B1 = 0.9
ADAM_B2 = 0.999
ADAM_EPS = 1e-08
ADAM_WD = 0.01
ADAM_STEP = 10

MESH = pl.DeviceIdType.MESH
_ANY = pl.BlockSpec(memory_space=pl.ANY)


def _cp(*sem):
    return pltpu.CompilerParams(dimension_semantics=sem, vmem_limit_bytes=VMEM_LIMIT)


def _sds(shape, dtype):
    return jax.ShapeDtypeStruct(shape, dtype)


def _tile(n, pref):
    t = min(n, pref)
    while n % t:
        t -= 8
    return t


def _sigmoid(x):
    return jax.nn.sigmoid(x)


_GELU_C = math.sqrt(2.0 / math.pi)


def _gelu(y):
    return 0.5 * y * (1.0 + jnp.tanh(_GELU_C * (y + 0.044715 * y * y * y)))


def _gelu_grad(y):
    th = jnp.tanh(_GELU_C * (y + 0.044715 * y * y * y))
    return 0.5 * (1.0 + th) + 0.5 * y * (1.0 - th * th) * _GELU_C * (1.0 + 3.0 * 0.044715 * y * y)


def _dot(a, b):
    return jnp.dot(a, b, preferred_element_type=F32)


def _dot_t0(a, b):
    return lax.dot_general(a, b, (((0,), (0,)), ((), ())), preferred_element_type=F32)


def _dot_t1(a, b):
    return lax.dot_general(a, b, (((1,), (1,)), ((), ())), preferred_element_type=F32)


def _slab_out(nl, l, shape, buf, n_in):
    sds = _sds((nl,) + tuple(shape), F32)
    if buf is None:
        return [], [], sds, {}
    return [buf], [_ANY], sds, {n_in: 0}


def _norm_mm(x, g, w, l, *, tm, tn, ntiles, pieces, transposed, name, comm=None):
    n, d = x.shape
    m = ntiles * tn
    mp = m // pieces
    npj = mp // tn
    nc = 0 if comm is None else len(comm[0])

    def body(x_ref, g_ref, w_ref, *rest):
        h_ref, y_ref = rest[nc:nc + 2]
        h_scr = rest[2 * nc + 2]
        if comm is not None:
            finish = _hosted_gather_steps(comm, rest[nc + 2:2 * nc + 2], rest[-2:],
                                          pl.program_id(0) * ntiles + pl.program_id(1), (n // tm) * ntiles)

        @pl.when(pl.program_id(1) == 0)
        def _():
            for r0 in range(0, tm, ROW_CHUNK):
                rows = slice(r0, r0 + ROW_CHUNK)
                xv = x_ref[rows, :]
                r = lax.rsqrt(jnp.mean(xv * xv, axis=-1, keepdims=True) + EPS)
                hb = (xv * r * g_ref[...]).astype(BF16)
                h_scr[rows, :] = hb
                h_ref[rows, :] = hb

        mm = _dot_t1 if transposed else _dot
        y_ref[...] = mm(h_scr[...], w_ref[...]).astype(BF16)
        if comm is not None:
            finish()

    wspec = (pl.BlockSpec((None, tn, d), lambda i, j: (l, j, 0)) if transposed
             else pl.BlockSpec((None, d, tn), lambda i, j: (l, 0, j)))
    c_in, c_ispec, c_ospec, c_oshape, c_scr, aliases = _hosted_gather(comm, 3, 2)
    return pl.pallas_call(
        body, grid=(n // tm, ntiles),
        in_specs=[pl.BlockSpec((tm, d), lambda i, j: (i, 0)), pl.BlockSpec((1, d), lambda i, j: (0, 0)), wspec] + c_ispec,
        out_specs=[pl.BlockSpec((tm, d), lambda i, j: (i, 0)),
                   pl.BlockSpec((None, tm, tn), lambda i, j: (j // npj, i, j % npj))] + c_ospec,
        out_shape=[_sds((n, d), BF16), _sds((pieces, n, mp), BF16)] + c_oshape,
        input_output_aliases=aliases,
        scratch_shapes=[pltpu.VMEM((tm, d), BF16)] + c_scr,
        compiler_params=_cp("arbitrary", "arbitrary"), name=name)(x, g, w, *c_in)


def _mm_t(a, w, l, *, tm, tn, off, ntiles, name):
    n, k = a.shape

    def body(a_ref, w_ref, y_ref):
        y_ref[...] = _dot_t1(a_ref[...], w_ref[...]).astype(BF16)

    return pl.pallas_call(
        body, grid=(n // tm, ntiles),
        in_specs=[pl.BlockSpec((tm, k), lambda i, j: (i, 0)), pl.BlockSpec((None, tn, k), lambda i, j: (l, off + j, 0))],
        out_specs=pl.BlockSpec((tm, tn), lambda i, j: (i, j)),
        out_shape=_sds((n, ntiles * tn), BF16),
        compiler_params=_cp("parallel", "arbitrary"), name=name)(a, w)


def _ffn_down(ab, wd, l, x, *, tm, tk, name):
    _, n, dff = ab.shape
    d = x.shape[1]
    nk = dff // tk

    def body(a_ref, b_ref, wd_ref, x_ref, o_ref, acc):
        k = pl.program_id(1)
        a = a_ref[...].astype(F32)
        b = b_ref[...].astype(F32)
        act = (a * _sigmoid(a) * b).astype(BF16)
        part = _dot(act, wd_ref[pl.ds(pl.multiple_of(k * tk, tk), tk), :])

        @pl.when(k == 0)
        def _():
            acc[...] = part

        @pl.when(k > 0)
        def _():
            acc[...] += part

        @pl.when(k == nk - 1)
        def _():
            o_ref[...] = x_ref[...] + 0.5 * acc[...]

    return pl.pallas_call(
        body, grid=(n // tm, nk),
        in_specs=[pl.BlockSpec((None, tm, tk), lambda i, k: (0, i, k)),
                  pl.BlockSpec((None, tm, tk), lambda i, k: (1, i, k)),
                  pl.BlockSpec((None, dff, d), lambda i, k: (l, 0, 0)),
                  pl.BlockSpec((tm, d), lambda i, k: (i, 0))],
        out_specs=pl.BlockSpec((tm, d), lambda i, k: (i, 0)),
        out_shape=_sds((n, d), F32),
        scratch_shapes=[pltpu.VMEM((tm, d), F32)],
        compiler_params=_cp("parallel", "arbitrary"), name=name)(ab, ab, wd, x)


def _ffn_dact(dx, wd, l, ab, nl, dwd_buf, *, tm, tk, name, comm=None):
    n, d = dx.shape
    dff = ab.shape[2]
    half = ((tk // LANES + 1) // 2) * LANES
    chunks = ((0, half), (half, tk))
    ne = 0 if dwd_buf is None else 1
    nc = 0 if comm is None else len(comm[0])
    ni = n // tm

    def body(dx_ref, wd_ref, a_ref, b_ref, *rest):
        dab_ref, dwd_ref = rest[ne + nc:ne + nc + 2]
        if comm is not None:
            start, wait = _exchange_ops(rest[ne:ne + nc], rest[ne + nc + 2:ne + 2 * nc + 2], rest[-2], rest[-1],
                                        comm[1], comm[2])
            step = pl.program_id(0) * ni + pl.program_id(1)

            @pl.when(step == 0)
            def _():
                start()

        do = (0.5 * dx_ref[...]).astype(BF16)

        @pl.when(pl.program_id(1) == 0)
        def _():
            dwd_ref[...] = jnp.zeros_like(dwd_ref)

        for c0, c1 in chunks:
            dact = _dot_t1(do, wd_ref[c0:c1, :])
            a = a_ref[:, c0:c1].astype(F32)
            b = b_ref[:, c0:c1].astype(F32)
            sg = _sigmoid(a)
            silu = a * sg
            dab_ref[0, :, c0:c1] = (dact * b * (sg * (1.0 + a * (1.0 - sg)))).astype(BF16)
            dab_ref[1, :, c0:c1] = (dact * silu).astype(BF16)
            dwd_ref[c0:c1, :] += _dot_t0((silu * b).astype(BF16), do)

        if comm is not None:
            @pl.when(step == (dff // tk) * ni - 1)
            def _():
                wait()

    extra, extra_specs, dwd_shape, aliases = _slab_out(nl, l, (dff, d), dwd_buf, 4)
    aliases = {k: 1 for k in aliases}
    comm_in = [] if comm is None else list(comm[0])
    comm_out = [] if comm is None else _parts_shapes(comm[0], comm[1])
    comm_scr = [] if comm is None else [pltpu.SemaphoreType.DMA((3 * nc,)), pltpu.SemaphoreType.DMA((3 * nc,))]
    return pl.pallas_call(
        body, grid=(dff // tk, ni),
        in_specs=[pl.BlockSpec((tm, d), lambda j, i: (i, 0)),
                  pl.BlockSpec((None, tk, d), lambda j, i: (l, j, 0)),
                  pl.BlockSpec((None, tm, tk), lambda j, i: (0, i, j)),
                  pl.BlockSpec((None, tm, tk), lambda j, i: (1, i, j)), *extra_specs] + [_ANY] * nc,
        out_specs=[pl.BlockSpec((2, tm, tk), lambda j, i: (0, i, j)),
                   pl.BlockSpec((None, tk, d), lambda j, i: (l, j, 0))] + [_ANY] * nc,
        out_shape=[_sds((2, n, dff), BF16), dwd_shape] + comm_out,
        input_output_aliases=aliases, scratch_shapes=comm_scr,
        compiler_params=_cp("arbitrary", "arbitrary"), name=name)(dx, wd, ab, ab, *extra, *comm_in)


def _rms_bwd_epilogue(acc, x_ref, g_ref, dres_ref, dx_ref, dg_ref, i):
    dgp = jnp.zeros(dg_ref.shape, F32)
    for r0 in range(0, acc.shape[0], ROW_CHUNK):
        rows = slice(r0, r0 + ROW_CHUNK)
        dh = acc[rows, :]
        xv = x_ref[rows, :]
        r = lax.rsqrt(jnp.mean(xv * xv, axis=-1, keepdims=True) + EPS)
        xn = xv * r
        dgp = dgp + jnp.sum(dh * xn, axis=0, keepdims=True)
        dxh = dh * g_ref[...]
        dx_ref[rows, :] = dres_ref[rows, :] + r * (dxh - xn * jnp.mean(dxh * xn, axis=-1, keepdims=True))

    @pl.when(i == 0)
    def _():
        dg_ref[...] = dgp

    @pl.when(i > 0)
    def _():
        dg_ref[...] += dgp


def _ffn_dx(dab, wu, l, x, g, dres, *, tm, tk, name, comm=None):
    p, n, mp = dab.shape
    d = x.shape[1]
    nkp = mp // tk
    nk = p * nkp
    ni = n // tm
    nc = 0 if comm is None else len(comm[0])

    def body(dy_ref, w_ref, x_ref, g_ref, dres_ref, *rest):
        dx_ref, dg_ref = rest[nc:nc + 2]
        acc = rest[2 * nc + 2]
        k = pl.program_id(1)
        if comm is not None:
            start, wait = _swap_ops(rest[:nc], rest[nc + 2:2 * nc + 2], rest[-2], rest[-1], comm[1])

            @pl.when((pl.program_id(0) == 0) & (k == 0))
            def _():
                start()

        part = _dot_t1(dy_ref[...], w_ref[...])

        @pl.when(k == 0)
        def _():
            acc[...] = part

        @pl.when(k > 0)
        def _():
            acc[...] += part

        @pl.when(k == nk - 1)
        def _():
            _rms_bwd_epilogue(acc, x_ref, g_ref, dres_ref, dx_ref, dg_ref, pl.program_id(0))

        if comm is not None:
            @pl.when((pl.program_id(0) == ni - 1) & (k == nk - 1))
            def _():
                wait()

    comm_in = [] if comm is None else list(comm[0])
    comm_out = [_sds(t.shape[1:], t.dtype) for t in comm_in]
    comm_scr = [] if comm is None else [pltpu.SemaphoreType.DMA((nc,)), pltpu.SemaphoreType.DMA((nc,))]
    return pl.pallas_call(
        body, grid=(ni, nk),
        in_specs=[pl.BlockSpec((None, tm, tk), lambda i, k: (k // nkp, i, k % nkp)),
                  pl.BlockSpec((None, d, tk), lambda i, k: (l, 0, k)),
                  pl.BlockSpec((tm, d), lambda i, k: (i, 0)),
                  pl.BlockSpec((1, d), lambda i, k: (0, 0)),
                  pl.BlockSpec((tm, d), lambda i, k: (i, 0))] + [_ANY] * nc,
        out_specs=[pl.BlockSpec((tm, d), lambda i, k: (i, 0)), pl.BlockSpec((1, d), lambda i, k: (0, 0))] + [_ANY] * nc,
        out_shape=[_sds((n, d), F32), _sds((1, d), F32)] + comm_out,
        scratch_shapes=[pltpu.VMEM((tm, d), F32)] + comm_scr,
        compiler_params=_cp("arbitrary", "arbitrary"), name=name)(dab, wu, x, g, dres, *comm_in)


def _mix_dx(dpa, dgl, wt, l, x, g, dres, *, tm, tk, name, comm=None):
    n, d = x.shape
    n1 = dpa.shape[1] // tk
    n2 = dgl.shape[1] // tk
    nk = n1 + n2
    ni = n // tm
    nc = 0 if comm is None else len(comm[0])

    def body(d1_ref, d2_ref, w_ref, x_ref, g_ref, dres_ref, *rest):
        dx_ref, dg_ref = rest[nc:nc + 2]
        acc = rest[2 * nc + 2]
        k = pl.program_id(1)
        if comm is not None:
            start, wait = _swap_ops(rest[:nc], rest[nc + 2:2 * nc + 2], rest[-2], rest[-1], comm[1])

            @pl.when((pl.program_id(0) == 0) & (k == 0))
            def _():
                start()

        @pl.when(k == 0)
        def _():
            acc[...] = _dot(d1_ref[...], w_ref[...])

        @pl.when((k > 0) & (k < n1))
        def _():
            acc[...] += _dot(d1_ref[...], w_ref[...])

        @pl.when(k >= n1)
        def _():
            acc[...] += _dot(d2_ref[...], w_ref[...])

        @pl.when(k == nk - 1)
        def _():
            _rms_bwd_epilogue(acc, x_ref, g_ref, dres_ref, dx_ref, dg_ref, pl.program_id(0))

        if comm is not None:
            @pl.when((pl.program_id(0) == ni - 1) & (k == nk - 1))
            def _():
                wait()

    comm_in = [] if comm is None else list(comm[0])
    comm_out = [_sds(t.shape[1:], t.dtype) for t in comm_in]
    comm_scr = [] if comm is None else [pltpu.SemaphoreType.DMA((nc,)), pltpu.SemaphoreType.DMA((nc,))]
    return pl.pallas_call(
        body, grid=(ni, nk),
        in_specs=[pl.BlockSpec((tm, tk), lambda i, k: (i, jnp.minimum(k, n1 - 1))),
                  pl.BlockSpec((tm, tk), lambda i, k: (i, jnp.maximum(k - n1, 0))),
                  pl.BlockSpec((None, tk, d), lambda i, k: (l, k, 0)),
                  pl.BlockSpec((tm, d), lambda i, k: (i, 0)),
                  pl.BlockSpec((1, d), lambda i, k: (0, 0)),
                  pl.BlockSpec((tm, d), lambda i, k: (i, 0))] + [_ANY] * nc,
        out_specs=[pl.BlockSpec((tm, d), lambda i, k: (i, 0)), pl.BlockSpec((1, d), lambda i, k: (0, 0))] + [_ANY] * nc,
        out_shape=[_sds((n, d), F32), _sds((1, d), F32)] + comm_out,
        scratch_shapes=[pltpu.VMEM((tm, d), F32)] + comm_scr,
        compiler_params=_cp("arbitrary", "arbitrary"), name=name)(dpa, dgl, wt, x, g, dres, *comm_in)


def _mm_tn(a, b, l, nl, buf, *, ta, tb, tk, name):
    pa, n, ka = a.shape
    pb, _, kb = b.shape
    nap = ka // ta
    nbp = kb // tb

    def body(a_ref, b_ref, *rest):
        o_ref = rest[-1]

        @pl.when(pl.program_id(2) == 0)
        def _():
            o_ref[...] = jnp.zeros_like(o_ref)

        o_ref[...] += _dot_t0(a_ref[...], b_ref[...])

    extra, extra_specs, out_shape, aliases = _slab_out(nl, l, (pa * ka, pb * kb), buf, 2)
    return pl.pallas_call(
        body, grid=(pa * nap, pb * nbp, n // tk),
        in_specs=[pl.BlockSpec((None, tk, ta), lambda i, j, k: (i // nap, k, i % nap)),
                  pl.BlockSpec((None, tk, tb), lambda i, j, k: (j // nbp, k, j % nbp)), *extra_specs],
        out_specs=pl.BlockSpec((None, ta, tb), lambda i, j, k: (l, i, j)),
        out_shape=out_shape, input_output_aliases=aliases,
        compiler_params=_cp("parallel", "parallel", "arbitrary"), name=name)(a, b, *extra)


def _dwin_t(dpa, dgl, h, l, nl, buf, *, ta, tk, name):
    n, d = h.shape
    n1 = dpa.shape[1] // ta
    n2 = dgl.shape[1] // ta

    def body(a1_ref, a2_ref, h_ref, *rest):
        o_ref = rest[-1]
        i = pl.program_id(0)

        @pl.when(pl.program_id(1) == 0)
        def _():
            o_ref[...] = jnp.zeros_like(o_ref)

        @pl.when(i < n1)
        def _():
            o_ref[...] += _dot_t0(a1_ref[...], h_ref[...])

        @pl.when(i >= n1)
        def _():
            o_ref[...] += _dot_t0(a2_ref[...], h_ref[...])

    extra, extra_specs, out_shape, aliases = _slab_out(nl, l, ((n1 + n2) * ta, d), buf, 3)
    return pl.pallas_call(
        body, grid=(n1 + n2, n // tk),
        in_specs=[pl.BlockSpec((tk, ta), lambda i, k: (jnp.where(i < n1, k, 0), jnp.minimum(i, n1 - 1))),
                  pl.BlockSpec((tk, ta), lambda i, k: (jnp.where(i >= n1, k, 0), jnp.maximum(i - n1, 0))),
                  pl.BlockSpec((tk, d), lambda i, k: (k, 0)), *extra_specs],
        out_specs=pl.BlockSpec((None, ta, d), lambda i, k: (l, i, 0)),
        out_shape=out_shape, input_output_aliases=aliases,
        compiler_params=_cp("parallel", "arbitrary"), name=name)(dpa, dgl, h, *extra)


def _loss_grad(y, t, *, tm, name):
    n, d = y.shape

    def body(y_ref, t_ref, dy_ref, l_ref):
        e = y_ref[...] - t_ref[...]
        dy_ref[...] = e * (1.0 / d)
        part = jnp.sum(e * e, axis=0, keepdims=True)

        @pl.when(pl.program_id(0) == 0)
        def _():
            l_ref[...] = part

        @pl.when(pl.program_id(0) > 0)
        def _():
            l_ref[...] += part

    return pl.pallas_call(
        body, grid=(n // tm,),
        in_specs=[pl.BlockSpec((tm, d), lambda i: (i, 0)), pl.BlockSpec((tm, d), lambda i: (i, 0))],
        out_specs=[pl.BlockSpec((tm, d), lambda i: (i, 0)), pl.BlockSpec((1, d), lambda i: (0, 0))],
        out_shape=[_sds((n, d), F32), _sds((1, d), F32)],
        compiler_params=_cp("arbitrary"), name=name)(y, t)


def _s5_fwd(proj, sp, wglu, l, *, bl, s, t, name):
    n = bl * s
    ds5, gp = sp["bblk_r"].shape
    nt = s // t
    ng = t // SUBLANES
    glog = int(math.log2(ng))

    def body(u_ref, br_ref, bi_ref, pw_ref, p8_ref, cr_ref, ci_ref, d_ref, wg_ref,
             xr_ref, xi_ref, yp_ref, zg_ref, o_ref, carry, st):
        @pl.when(pl.program_id(1) == 0)
        def _():
            carry[...] = jnp.zeros_like(carry)

        u = u_ref[...]
        sub = lax.broadcasted_iota(jnp.int32, (t, gp), 0) % SUBLANES
        xr = _dot(u, br_ref[...])
        xi = _dot(u, bi_ref[...])
        for k in range(GROUP_LOG):
            sh = 1 << k
            pr = pw_ref[2 * k:2 * k + 1, :]
            pi = pw_ref[2 * k + 1:2 * k + 2, :]
            keep = sub >= sh
            sr = jnp.where(keep, pltpu.roll(xr, sh, 0), 0.0)
            si = jnp.where(keep, pltpu.roll(xi, sh, 0), 0.0)
            xr, xi = xr + pr * sr - pi * si, xi + pr * si + pi * sr
        xr_ref[...] = xr
        xi_ref[...] = xi
        grow = lax.broadcasted_iota(jnp.int32, (ng, gp), 0)
        cr = carry[0:1, :]
        ci = carry[1:2, :]
        a8r = pw_ref[2 * GROUP_LOG:2 * GROUP_LOG + 1, :]
        a8i = pw_ref[2 * GROUP_LOG + 1:2 * GROUP_LOG + 2, :]
        head = grow == 0
        for g in range(ng):
            st[g:g + 1, :] = xr_ref[(g + 1) * SUBLANES - 1:(g + 1) * SUBLANES, :]
            st[ng + g:ng + g + 1, :] = xi_ref[(g + 1) * SUBLANES - 1:(g + 1) * SUBLANES, :]
        sr_ = st[0:ng, :] + jnp.where(head, a8r * cr - a8i * ci, 0.0)
        si_ = st[ng:2 * ng, :] + jnp.where(head, a8r * ci + a8i * cr, 0.0)
        for k in range(glog):
            sh = 1 << k
            pr = pw_ref[2 * (GROUP_LOG + k):2 * (GROUP_LOG + k) + 1, :]
            pi = pw_ref[2 * (GROUP_LOG + k) + 1:2 * (GROUP_LOG + k) + 2, :]
            keep = grow >= sh
            tr = jnp.where(keep, pltpu.roll(sr_, sh, 0), 0.0)
            ti = jnp.where(keep, pltpu.roll(si_, sh, 0), 0.0)
            sr_, si_ = sr_ + pr * tr - pi * ti, si_ + pr * ti + pi * tr
        tail = grow == ng - 1
        carry[0:1, :] = jnp.sum(jnp.where(tail, sr_, 0.0), axis=0, keepdims=True)
        carry[1:2, :] = jnp.sum(jnp.where(tail, si_, 0.0), axis=0, keepdims=True)
        st[0:ng, :] = jnp.where(head, cr, pltpu.roll(sr_, 1, 0))
        st[ng:2 * ng, :] = jnp.where(head, ci, pltpu.roll(si_, 1, 0))
        p8r = p8_ref[0:SUBLANES, :]
        p8i = p8_ref[SUBLANES:2 * SUBLANES, :]
        for g in range(ng):
            grp = slice(g * SUBLANES, (g + 1) * SUBLANES)
            pr = st[g:g + 1, :]
            pi = st[ng + g:ng + g + 1, :]
            xr_ref[grp, :] = xr_ref[grp, :] + p8r * pr - p8i * pi
            xi_ref[grp, :] = xi_ref[grp, :] + p8r * pi + p8i * pr
        xr = xr_ref[...]
        xi = xi_ref[...]
        y = _dot(xr.astype(BF16), cr_ref[...]) + _dot(xi.astype(BF16), ci_ref[...]) + d_ref[...] * u.astype(F32)
        yp_ref[...] = y
        zg = _dot(_gelu(y).astype(BF16), wg_ref[...])
        zg_ref[...] = zg
        o_ref[...] = (zg[:, :ds5] * _sigmoid(zg[:, ds5:])).astype(BF16)

    const = lambda shape: pl.BlockSpec(shape, lambda b, i: (0, 0))
    row = lambda w: pl.BlockSpec((t, w), lambda b, i: (b * nt + i, 0))
    return pl.pallas_call(
        body, grid=(bl, nt),
        in_specs=[row(ds5), const((ds5, gp)), const((ds5, gp)), const((2 * (GROUP_LOG + glog), gp)),
                  const((2 * SUBLANES, gp)),
                  const((gp, ds5)), const((gp, ds5)), const((1, ds5)),
                  pl.BlockSpec((None, ds5, 2 * ds5), lambda b, i: (l, 0, 0))],
        out_specs=[row(gp), row(gp), row(ds5), row(2 * ds5), row(ds5)],
        out_shape=[_sds((n, gp), F32), _sds((n, gp), F32), _sds((n, ds5), F32), _sds((n, 2 * ds5), F32),
                   _sds((n, ds5), BF16)],
        scratch_shapes=[pltpu.VMEM((2, gp), F32), pltpu.VMEM((2 * ng, gp), F32)],
        compiler_params=_cp("arbitrary", "arbitrary"), name=name)(
            proj, sp["bblk_r"], sp["bblk_i"], sp["pw"], sp["p8"], sp["cblk_r"], sp["cblk_in"], sp["d"], wglu)


def _s5_bwd(ds, yp, zg, xr, xi, proj, sp, wglu, l, nl, dwg_buf, *, bl, s, t, name):
    n = bl * s
    ds5, gp = sp["bblk_r"].shape
    nt = s // t
    tb = t // 8
    ng = t // SUBLANES
    glog = int(math.log2(ng))

    def body(ds_ref, yp_ref, zg_ref, xr_ref, xi_ref, hr_ref, hi_ref, u_ref, wg_ref, cr_ref, ci_ref,
             br_ref, bi_ref, pw_ref, q8_ref, d_ref, *rest):
        du_ref, dd_ref, dcr_ref, dci_ref, dbr_ref, dbi_ref, da_ref, dwg_ref, carry, gr_scr, gi_scr, st = rest[-12:]
        b = pl.program_id(0)
        i = pl.program_id(1)
        tile = nt - 1 - i

        @pl.when((b == 0) & (i == 0))
        def _():
            for r in (dwg_ref, dd_ref, dcr_ref, dci_ref, dbr_ref, dbi_ref, da_ref):
                r[...] = jnp.zeros_like(r)

        @pl.when(i == 0)
        def _():
            carry[...] = jnp.zeros_like(carry)

        dsv = ds_ref[...].astype(F32)
        zgv = zg_ref[...]
        za = zgv[:, :ds5]
        sg = _sigmoid(zgv[:, ds5:])
        dzg = jnp.concatenate([dsv * sg, dsv * za * sg * (1.0 - sg)], axis=1).astype(BF16)
        y = yp_ref[...]
        dwg_ref[...] += _dot_t0(_gelu(y).astype(BF16), dzg)
        dy = _dot_t1(dzg, wg_ref[...]) * _gelu_grad(y)
        ub = u_ref[...]
        uf = ub.astype(F32)
        dd_ref[...] += jnp.sum(dy * uf, axis=0, keepdims=True)
        dyb = dy.astype(BF16)
        xrv = xr_ref[...]
        xiv = xi_ref[...]
        dcr_ref[...] += _dot_t0(xrv.astype(BF16), dyb)
        dci_ref[...] += _dot_t0(xiv.astype(BF16), dyb)

        rows = lax.broadcasted_iota(jnp.int32, (t, gp), 0)
        sub = rows % SUBLANES
        gr = _dot_t1(dyb, cr_ref[...])
        gi = _dot_t1(dyb, ci_ref[...])
        for k in range(GROUP_LOG):
            sh = 1 << k
            pr = pw_ref[2 * k:2 * k + 1, :]
            pi = pw_ref[2 * k + 1:2 * k + 2, :]
            keep = sub < SUBLANES - sh
            sr = jnp.where(keep, pltpu.roll(gr, t - sh, 0), 0.0)
            si = jnp.where(keep, pltpu.roll(gi, t - sh, 0), 0.0)
            gr, gi = gr + pr * sr + pi * si, gi + pr * si - pi * sr
        gr_scr[...] = gr
        gi_scr[...] = gi
        grow = lax.broadcasted_iota(jnp.int32, (ng, gp), 0)
        cr = carry[0:1, :]
        ci = carry[1:2, :]
        a8r = pw_ref[2 * GROUP_LOG:2 * GROUP_LOG + 1, :]
        a8i = pw_ref[2 * GROUP_LOG + 1:2 * GROUP_LOG + 2, :]
        tail = grow == ng - 1
        for g in range(ng):
            st[g:g + 1, :] = gr_scr[g * SUBLANES:g * SUBLANES + 1, :]
            st[ng + g:ng + g + 1, :] = gi_scr[g * SUBLANES:g * SUBLANES + 1, :]
        sr_ = st[0:ng, :] + jnp.where(tail, a8r * cr + a8i * ci, 0.0)
        si_ = st[ng:2 * ng, :] + jnp.where(tail, a8r * ci - a8i * cr, 0.0)
        for k in range(glog):
            sh = 1 << k
            pr = pw_ref[2 * (GROUP_LOG + k):2 * (GROUP_LOG + k) + 1, :]
            pi = pw_ref[2 * (GROUP_LOG + k) + 1:2 * (GROUP_LOG + k) + 2, :]
            keep = grow < ng - sh
            tr = jnp.where(keep, pltpu.roll(sr_, ng - sh, 0), 0.0)
            ti = jnp.where(keep, pltpu.roll(si_, ng - sh, 0), 0.0)
            sr_, si_ = sr_ + pr * tr + pi * ti, si_ + pr * ti - pi * tr
        head = grow == 0
        carry[0:1, :] = jnp.sum(jnp.where(head, sr_, 0.0), axis=0, keepdims=True)
        carry[1:2, :] = jnp.sum(jnp.where(head, si_, 0.0), axis=0, keepdims=True)
        st[0:ng, :] = jnp.where(tail, cr, pltpu.roll(sr_, ng - 1, 0))
        st[ng:2 * ng, :] = jnp.where(tail, ci, pltpu.roll(si_, ng - 1, 0))
        q8r = q8_ref[0:SUBLANES, :]
        q8i = q8_ref[SUBLANES:2 * SUBLANES, :]
        for g in range(ng):
            grp = slice(g * SUBLANES, (g + 1) * SUBLANES)
            pr = st[g:g + 1, :]
            pi = st[ng + g:ng + g + 1, :]
            gr_scr[grp, :] = gr_scr[grp, :] + q8r * pr - q8i * pi
            gi_scr[grp, :] = gi_scr[grp, :] + q8r * pi + q8i * pr
        gr = gr_scr[...]
        gi = gi_scr[...]
        first = rows == 0

        live = jnp.where(tile > 0, 1.0, 0.0)
        xpr = jnp.where(first, hr_ref[7:8, :] * live, pltpu.roll(xrv, 1, 0))
        xpi = jnp.where(first, hi_ref[7:8, :] * live, pltpu.roll(xiv, 1, 0))
        da_ref[0:1, :] += jnp.sum(gr * xpr + gi * xpi, axis=0, keepdims=True)
        da_ref[1:2, :] += jnp.sum(gi * xpr - gr * xpi, axis=0, keepdims=True)

        grb = gr.astype(BF16)
        gib = gi.astype(BF16)
        dbr_ref[...] += _dot_t0(ub, grb)
        dbi_ref[...] += _dot_t0(ub, gib)
        du_ref[...] = (_dot_t1(grb, br_ref[...]) + _dot_t1(gib, bi_ref[...]) + dy * d_ref[...]).astype(BF16)

    const = lambda shape: pl.BlockSpec(shape, lambda b, i: (0, 0))
    row = lambda w: pl.BlockSpec((t, w), lambda b, i: (b * nt + nt - 1 - i, 0))
    halo = pl.BlockSpec((8, gp), lambda b, i: (jnp.maximum((b * nt + nt - 1 - i) * tb - 1, 0), 0))
    extra, extra_specs, dwg_shape, aliases = _slab_out(nl, l, (ds5, 2 * ds5), dwg_buf, 16)
    aliases = {k: 7 for k in aliases}
    return pl.pallas_call(
        body, grid=(bl, nt),
        in_specs=[row(ds5), row(ds5), row(2 * ds5), row(gp), row(gp), halo, halo, row(ds5),
                  pl.BlockSpec((None, ds5, 2 * ds5), lambda b, i: (l, 0, 0)),
                  const((gp, ds5)), const((gp, ds5)), const((ds5, gp)), const((ds5, gp)),
                  const((2 * (GROUP_LOG + glog), gp)), const((2 * SUBLANES, gp)), const((1, ds5)), *extra_specs],
        out_specs=[row(ds5), const((1, ds5)), const((gp, ds5)), const((gp, ds5)),
                   const((ds5, gp)), const((ds5, gp)), const((2, gp)),
                   pl.BlockSpec((None, ds5, 2 * ds5), lambda b, i: (l, 0, 0))],
        out_shape=[_sds((n, ds5), BF16), _sds((1, ds5), F32), _sds((gp, ds5), F32),
                   _sds((gp, ds5), F32), _sds((ds5, gp), F32), _sds((ds5, gp), F32), _sds((2, gp), F32), dwg_shape],
        input_output_aliases=aliases,
        scratch_shapes=[pltpu.VMEM((2, gp), F32), pltpu.VMEM((t, gp), F32), pltpu.VMEM((t, gp), F32),
                        pltpu.VMEM((2 * ng, gp), F32)],
        compiler_params=_cp("arbitrary", "arbitrary"), name=name)(
            ds, yp, zg, xr, xi, xr, xi, proj, wglu, sp["cblk_r"], sp["cblk_in"],
            sp["bblk_r"], sp["bblk_i"], sp["pw"], sp["q8"], sp["d"], *extra)


def _head_norm(x, first):
    x2 = x * x
    sa = jnp.sum(jnp.where(first, x2, 0.0), axis=-1, keepdims=True)
    sb = jnp.sum(jnp.where(first, 0.0, x2), axis=-1, keepdims=True)
    r = jnp.where(first, lax.rsqrt(sa * (1.0 / HEAD_DIM) + EPS), lax.rsqrt(sb * (1.0 / HEAD_DIM) + EPS))
    return x * r, r


def _attn_specs(bl, s, datt, qoff):
    nq = s // ATT_TQ
    nb = datt // LANES
    col = lambda blk: (lambda b, h, q: (b * nq + q, qoff + blk * nb + h))
    win = lambda blk, j: (lambda b, h, q: (b * nq + jnp.maximum(q - 2 + j, 0), qoff + blk * nb + h))
    tile = lambda f: pl.BlockSpec((ATT_TQ, LANES), f)
    qs = tile(col(0))
    ks = [tile(win(1, j)) for j in range(3)]
    vs = [tile(win(2, j)) for j in range(3)]
    return nq, nb, qs, ks, vs


def _attn_probs(q_ref, k_refs, gq_ref, gk_ref, bias_ref):
    qt = pl.program_id(2)
    lane = lax.broadcasted_iota(jnp.int32, (1, LANES), 1)
    first = lane < HEAD_DIM
    qh, rq = _head_norm(q_ref[...].astype(F32), first)
    qn = qh * gq_ref[...]
    kc = jnp.concatenate([r[...] for r in k_refs], axis=0).astype(F32)
    kh, _ = _head_norm(kc, first)
    kn = (kh * gk_ref[...]).astype(BF16)
    kpos = (qt - 2) * ATT_TQ + lax.broadcasted_iota(jnp.int32, (1, 3 * ATT_TQ), 1)
    valid = kpos >= 0
    scale = HEAD_DIM ** -0.5
    masks = (first, jnp.logical_not(first))
    qas, ps = [], []
    for hh in range(2):
        qa = jnp.where(masks[hh], qn, 0.0).astype(BF16)
        sc = _dot_t1(qa, kn) * scale + bias_ref[hh]
        sc = jnp.where(valid, sc, NEG)
        e = jnp.exp(sc - jnp.max(sc, axis=-1, keepdims=True))
        ps.append(e / jnp.sum(e, axis=-1, keepdims=True))
        qas.append(qa)
    return first, masks, qh, rq, kn, qas, ps


def _attn_fwd(proj, gq2, gk2, bias, *, bl, s, datt, qoff, name, comm=None):
    n = bl * s
    nq, nb, qs, ks, vs = _attn_specs(bl, s, datt, qoff)
    nc = 0 if comm is None else len(comm[0])

    def body(q_ref, k0, k1, k2, v0, v1, v2, gq_ref, gk_ref, bias_ref, *rest):
        o_ref = rest[nc]
        if comm is not None:
            finish = _hosted_gather_steps(comm, rest[nc + 1:2 * nc + 1], rest[-2:],
                                          (pl.program_id(0) * nb + pl.program_id(1)) * nq + pl.program_id(2),
                                          bl * nb * nq)
        first, _, _, _, _, _, ps = _attn_probs(q_ref, (k0, k1, k2), gq_ref, gk_ref, bias_ref)
        vc = jnp.concatenate([v0[...], v1[...], v2[...]], axis=0)
        o0 = _dot(ps[0].astype(BF16), vc)
        o1 = _dot(ps[1].astype(BF16), vc)
        o_ref[...] = jnp.where(first, o0, o1).astype(BF16)
        if comm is not None:
            finish()

    gs = pl.BlockSpec((1, LANES), lambda b, h, q: (0, 0))
    c_in, c_ispec, c_ospec, c_oshape, c_scr, aliases = _hosted_gather(comm, 10, 1)
    res = pl.pallas_call(
        body, grid=(bl, nb, nq),
        in_specs=[qs, *ks, *vs, gs, gs, pl.BlockSpec((2, ATT_TQ, 3 * ATT_TQ), lambda b, h, q: (h, 0, 0))] + c_ispec,
        out_specs=[pl.BlockSpec((ATT_TQ, LANES), lambda b, h, q: (b * nq + q, h))] + c_ospec,
        out_shape=[_sds((n, datt), BF16)] + c_oshape,
        input_output_aliases=aliases, scratch_shapes=c_scr,
        compiler_params=_cp("arbitrary", "arbitrary", "arbitrary"), name=name)(
            proj, proj, proj, proj, proj, proj, proj, gq2, gk2, bias, *c_in)
    return res[0], res[1:]


def _attn_bwd(do, proj, gq2, gk2, bias, *, bl, s, datt, qoff, name, comm=None):
    n = bl * s
    nq, nb, qs, ks, vs = _attn_specs(bl, s, datt, qoff)
    srows = s + 2 * ATT_TQ
    scale = HEAD_DIM ** -0.5
    nc = 0 if comm is None else len(comm[0])

    def body(do_ref, q_ref, k0, k1, k2, v0, v1, v2, gq_ref, gk_ref, bias_ref, *rest):
        dq_ref, dk_ref, dv_ref, db_ref, dgq_ref = rest[nc:nc + 5]
        qt = pl.program_id(2)
        if comm is not None:
            start, wait = _exchange_ops(rest[:nc], rest[nc + 5:2 * nc + 5], rest[-2], rest[-1], comm[1], comm[2])
            step = (pl.program_id(0) * nb + pl.program_id(1)) * nq + qt

            @pl.when(step == 0)
            def _():
                start()

        @pl.when(qt == 0)
        def _():
            dk_ref[...] = jnp.zeros_like(dk_ref)
            dv_ref[...] = jnp.zeros_like(dv_ref)
            db_ref[...] = jnp.zeros_like(db_ref)
            dgq_ref[...] = jnp.zeros_like(dgq_ref)

        first, masks, qh, rq, kn, qas, ps = _attn_probs(q_ref, (k0, k1, k2), gq_ref, gk_ref, bias_ref)
        vc = jnp.concatenate([v0[...], v1[...], v2[...]], axis=0)
        dov = do_ref[...]
        dqn = jnp.zeros((ATT_TQ, LANES), F32)
        dkn = jnp.zeros((3 * ATT_TQ, LANES), F32)
        dv = jnp.zeros((3 * ATT_TQ, LANES), F32)
        for hh in range(2):
            doa = jnp.where(masks[hh], dov, jnp.zeros_like(dov))
            p = ps[hh]
            dp = _dot_t1(doa, vc)
            dsm = p * (dp - jnp.sum(dp * p, axis=-1, keepdims=True))
            db_ref[hh] += dsm
            dsc = (dsm * scale).astype(BF16)
            dqn = dqn + _dot(dsc, jnp.where(masks[hh], kn, jnp.zeros_like(kn)))
            dkn = dkn + _dot_t0(dsc, qas[hh])
            dv = dv + _dot_t0(p.astype(BF16), doa)
        start = pl.multiple_of(qt * ATT_TQ, ATT_TQ)
        dk_ref[pl.ds(start, 3 * ATT_TQ), :] += dkn
        dv_ref[pl.ds(start, 3 * ATT_TQ), :] += dv
        dgq_ref[...] += jnp.sum(dqn * qh, axis=0, keepdims=True)
        dqh = dqn * gq_ref[...]
        t = dqh * qh
        ma = jnp.sum(jnp.where(first, t, 0.0), axis=-1, keepdims=True) * (1.0 / HEAD_DIM)
        mb = jnp.sum(jnp.where(first, 0.0, t), axis=-1, keepdims=True) * (1.0 / HEAD_DIM)
        dq_ref[...] = (rq * (dqh - qh * jnp.where(first, ma, mb))).astype(BF16)
        if comm is not None:
            @pl.when(step == bl * nb * nq - 1)
            def _():
                wait()

    gs = pl.BlockSpec((1, LANES), lambda b, h, q: (0, 0))
    acc = pl.BlockSpec((None, srows, LANES), lambda b, h, q: (b, 0, h))
    comm_in = [] if comm is None else list(comm[0])
    comm_out = [] if comm is None else _parts_shapes(comm[0], comm[1])
    comm_scr = [] if comm is None else [pltpu.SemaphoreType.DMA((3 * nc,)), pltpu.SemaphoreType.DMA((3 * nc,))]
    return pl.pallas_call(
        body, grid=(bl, nb, nq),
        in_specs=[pl.BlockSpec((ATT_TQ, LANES), lambda b, h, q: (b * nq + q, h)), qs, *ks, *vs, gs, gs,
                  pl.BlockSpec((2, ATT_TQ, 3 * ATT_TQ), lambda b, h, q: (h, 0, 0))] + [_ANY] * nc,
        out_specs=[pl.BlockSpec((ATT_TQ, LANES), lambda b, h, q: (b * nq + q, h)), acc, acc,
                   pl.BlockSpec((None, 2, ATT_TQ, 3 * ATT_TQ), lambda b, h, q: (b, h, 0, 0)),
                   pl.BlockSpec((None, None, 1, LANES), lambda b, h, q: (b, h, 0, 0))] + [_ANY] * nc,
        out_shape=[_sds((n, datt), BF16), _sds((bl, srows, datt), F32), _sds((bl, srows, datt), F32),
                   _sds((bl, 2 * nb, ATT_TQ, 3 * ATT_TQ), F32), _sds((bl, nb, 1, LANES), F32)] + comm_out,
        scratch_shapes=comm_scr,
        compiler_params=_cp("arbitrary", "arbitrary", "arbitrary"), name=name)(
            do, proj, proj, proj, proj, proj, proj, proj, gq2, gk2, bias, *comm_in)


def _attn_kv_bwd(dkn, dv, proj, gk2, *, bl, s, datt, tm, koff, name):
    n = bl * s
    ns = s // tm
    off = 2 * ATT_TQ // tm
    nb = datt // LANES

    def body(dkn_ref, dv_ref, k_ref, gk_ref, dk_ref, dvo_ref, dgk_ref):
        lane = lax.broadcasted_iota(jnp.int32, (1, LANES), 1)
        first = lane < HEAD_DIM

        @pl.when((pl.program_id(0) == 0) & (pl.program_id(1) == 0) & (pl.program_id(2) == 0))
        def _():
            dgk_ref[...] = jnp.zeros_like(dgk_ref)

        dvo_ref[...] = dv_ref[...].astype(BF16)
        kh, rk = _head_norm(k_ref[...].astype(F32), first)
        dn = dkn_ref[...]
        dgk_ref[...] += jnp.sum(dn * kh, axis=0, keepdims=True)
        dh = dn * gk_ref[...]
        t = dh * kh
        ma = jnp.sum(jnp.where(first, t, 0.0), axis=-1, keepdims=True) * (1.0 / HEAD_DIM)
        mb = jnp.sum(jnp.where(first, 0.0, t), axis=-1, keepdims=True) * (1.0 / HEAD_DIM)
        dk_ref[...] = (rk * (dh - kh * jnp.where(first, ma, mb))).astype(BF16)

    accs = pl.BlockSpec((None, tm, LANES), lambda b, i, c: (b, i + off, c))
    outs = pl.BlockSpec((tm, LANES), lambda b, i, c: (b * ns + i, c))
    vec = pl.BlockSpec((1, LANES), lambda b, i, c: (0, 0))
    return pl.pallas_call(
        body, grid=(bl, ns, nb),
        in_specs=[accs, accs, pl.BlockSpec((tm, LANES), lambda b, i, c: (b * ns + i, koff + c)), vec],
        out_specs=[outs, outs, vec],
        out_shape=[_sds((n, datt), BF16), _sds((n, datt), BF16), _sds((1, LANES), F32)],
        compiler_params=_cp("arbitrary", "arbitrary", "arbitrary"), name=name)(dkn, dv, proj, gk2)


def _conv_fwd(proj, wdw, bdw, lng, lnb, *, bl, s, t, acol, name):
    n = bl * s
    dc = wdw.shape[1]
    nt = s // t
    hb = t // HALO

    def body(za_ref, zg_ref, ha_ref, hgt_ref, w_ref, b_ref, g_ref, be_ref, hg_ref, hc_ref, o_ref, ext):
        i = pl.program_id(1)
        hg = za_ref[...].astype(F32) * _sigmoid(zg_ref[...].astype(F32))
        live = jnp.where(i > 0, 1.0, 0.0)
        ext[0:HALO, :] = ha_ref[...].astype(F32) * _sigmoid(hgt_ref[...].astype(F32)) * live
        ext[HALO:HALO + t, :] = hg
        hg_ref[...] = hg
        acc = jnp.zeros((t, dc), F32) + b_ref[...]
        for j in range(CONV_W):
            acc = acc + w_ref[j:j + 1, :] * ext[pl.ds(HALO - (CONV_W - 1) + j, t), :]
        hc_ref[...] = acc
        mu = jnp.mean(acc, axis=-1, keepdims=True)
        xc = acc - mu
        rs = lax.rsqrt(jnp.mean(xc * xc, axis=-1, keepdims=True) + EPS)
        ln = xc * rs * g_ref[...] + be_ref[...]
        o_ref[...] = (ln * _sigmoid(ln)).astype(BF16)

    vec = pl.BlockSpec((1, dc), lambda b, i: (0, 0))
    row = pl.BlockSpec((t, dc), lambda b, i: (b * nt + i, 0))
    tile = lambda c: pl.BlockSpec((t, dc), lambda b, i: (b * nt + i, c))
    halo = lambda c: pl.BlockSpec((HALO, dc), lambda b, i: (jnp.maximum((b * nt + i) * hb - 1, 0), c))
    return pl.pallas_call(
        body, grid=(bl, nt),
        in_specs=[tile(acol), tile(acol + 1), halo(acol), halo(acol + 1),
                  pl.BlockSpec((HALO, dc), lambda b, i: (0, 0)), vec, vec, vec],
        out_specs=[row, row, row],
        out_shape=[_sds((n, dc), F32), _sds((n, dc), F32), _sds((n, dc), BF16)],
        scratch_shapes=[pltpu.VMEM((HALO + t, dc), F32)],
        compiler_params=_cp("parallel", "arbitrary"), name=name)(proj, proj, proj, proj, wdw, bdw, lng, lnb)


def _conv_bwd_ln(dco, hc, lng, lnb, *, tm, name):
    n, dc = hc.shape

    def body(d_ref, hc_ref, g_ref, be_ref, dhc_ref, dg_ref, db_ref):
        @pl.when(pl.program_id(0) == 0)
        def _():
            dg_ref[...] = jnp.zeros_like(dg_ref)
            db_ref[...] = jnp.zeros_like(db_ref)

        hcv = hc_ref[...]
        mu = jnp.mean(hcv, axis=-1, keepdims=True)
        xc = hcv - mu
        rs = lax.rsqrt(jnp.mean(xc * xc, axis=-1, keepdims=True) + EPS)
        xh = xc * rs
        ln = xh * g_ref[...] + be_ref[...]
        sg = _sigmoid(ln)
        dln = d_ref[...].astype(F32) * (sg * (1.0 + ln * (1.0 - sg)))
        db_ref[...] += jnp.sum(dln, axis=0, keepdims=True)
        dg_ref[...] += jnp.sum(dln * xh, axis=0, keepdims=True)
        dxh = dln * g_ref[...]
        dhc_ref[...] = rs * (dxh - jnp.mean(dxh, axis=-1, keepdims=True)
                             - xh * jnp.mean(dxh * xh, axis=-1, keepdims=True))

    vec = pl.BlockSpec((1, dc), lambda i: (0, 0))
    row = pl.BlockSpec((tm, dc), lambda i: (i, 0))
    return pl.pallas_call(
        body, grid=(n // tm,), in_specs=[row, row, vec, vec], out_specs=[row, vec, vec],
        out_shape=[_sds((n, dc), F32), _sds((1, dc), F32), _sds((1, dc), F32)],
        compiler_params=_cp("arbitrary"), name=name)(dco, hc, lng, lnb)


def _conv_bwd_dw(dhc, hg, proj, wdw, *, bl, s, t, acol, name):
    n = bl * s
    dc = wdw.shape[1]
    nt = s // t
    hb = t // HALO
    lastblk = n // HALO - 1

    def body(d_ref, dn_ref, hg_ref, hp_ref, za_ref, zg_ref, w_ref, dz_ref, dw_ref, dbias_ref, extd, exth):
        b = pl.program_id(0)
        i = pl.program_id(1)

        @pl.when((b == 0) & (i == 0))
        def _():
            dw_ref[...] = jnp.zeros_like(dw_ref)
            dbias_ref[...] = jnp.zeros_like(dbias_ref)

        dv = d_ref[...]
        extd[0:t, :] = dv
        extd[t:t + HALO, :] = dn_ref[...] * jnp.where(i < nt - 1, 1.0, 0.0)
        exth[0:HALO, :] = hp_ref[...] * jnp.where(i > 0, 1.0, 0.0)
        exth[HALO:HALO + t, :] = hg_ref[...]
        dbias_ref[...] += jnp.sum(dv, axis=0, keepdims=True)
        dhg = jnp.zeros((t, dc), F32)
        for j in range(CONV_W):
            dhg = dhg + w_ref[j:j + 1, :] * extd[pl.ds(CONV_W - 1 - j, t), :]
            dw_ref[j:j + 1, :] += jnp.sum(dv * exth[pl.ds(HALO - (CONV_W - 1) + j, t), :], axis=0, keepdims=True)
        za = za_ref[...].astype(F32)
        sg = _sigmoid(zg_ref[...].astype(F32))
        dz_ref[...] = jnp.concatenate([dhg * sg, dhg * za * sg * (1.0 - sg)], axis=1).astype(BF16)

    row = pl.BlockSpec((t, dc), lambda b, i: (b * nt + i, 0))
    nxt = pl.BlockSpec((HALO, dc), lambda b, i: (jnp.minimum((b * nt + i + 1) * hb, lastblk), 0))
    prv = pl.BlockSpec((HALO, dc), lambda b, i: (jnp.maximum((b * nt + i) * hb - 1, 0), 0))
    wsp = pl.BlockSpec((HALO, dc), lambda b, i: (0, 0))
    tile = lambda c: pl.BlockSpec((t, dc), lambda b, i: (b * nt + i, c))
    return pl.pallas_call(
        body, grid=(bl, nt),
        in_specs=[row, nxt, row, prv, tile(acol), tile(acol + 1), wsp],
        out_specs=[pl.BlockSpec((t, 2 * dc), lambda b, i: (b * nt + i, 0)), wsp,
                   pl.BlockSpec((1, dc), lambda b, i: (0, 0))],
        out_shape=[_sds((n, 2 * dc), BF16), _sds((HALO, dc), F32), _sds((1, dc), F32)],
        scratch_shapes=[pltpu.VMEM((t + HALO, dc), F32), pltpu.VMEM((HALO + t, dc), F32)],
        compiler_params=_cp("arbitrary", "arbitrary"), name=name)(dhc, dhc, hg, hg, proj, proj, wdw)


def _mix_out_fwd(x, brs, gl, bg, wbs, wout, l, *, tm, name):
    n, d = x.shape

    def body(x_ref, s_ref, a_ref, c_ref, g0, g1, g2, bg_ref, ws, wa, wc, wo, o_ref):
        merged = jnp.zeros((tm, d), F32)
        for k, (br, gr, w) in enumerate(((s_ref, g0, ws), (a_ref, g1, wa), (c_ref, g2, wc))):
            gate = _sigmoid(gr[...].astype(F32) + bg_ref[:, k * d:(k + 1) * d])
            merged = merged + gate * _dot(br[...], w[...])
        o_ref[...] = x_ref[...] + _dot(merged.astype(BF16), wo[...])

    row = lambda w: pl.BlockSpec((tm, w), lambda i: (i, 0))
    wsp = lambda a: pl.BlockSpec((None,) + a.shape[1:], lambda i: (l, 0, 0))
    gls = [pl.BlockSpec((tm, d), functools.partial(lambda k, i: (i, k), k)) for k in range(3)]
    return pl.pallas_call(
        body, grid=(n // tm,),
        in_specs=[row(d), *[row(b.shape[1]) for b in brs], *gls, pl.BlockSpec(bg.shape, lambda i: (0, 0)),
                  *[wsp(w) for w in wbs], wsp(wout)],
        out_specs=row(d), out_shape=_sds((n, d), F32),
        compiler_params=_cp("parallel"), name=name)(x, *brs, gl, gl, gl, bg, *wbs, wout)


def _mix_out_bwd(dx, brs, gl, bg, wbs, wout, l, nl, bufs, *, tm, name):
    n, d = dx.shape
    widths = [b.shape[1] for b in brs]

    def body(dx_ref, s_ref, a_ref, c_ref, g0, g1, g2, bg_ref, ws, wa, wc, wo, *rest):
        ds_ref, da_ref, dc_ref, dgl_ref, dbg_ref, dws, dwa, dwc, dwo = rest[-9:]

        @pl.when(pl.program_id(0) == 0)
        def _():
            for r in (dbg_ref, dws, dwa, dwc, dwo):
                r[...] = jnp.zeros_like(r)

        dxb = dx_ref[...].astype(BF16)
        dm = _dot_t1(dxb, wo[...])
        merged = jnp.zeros((tm, d), F32)
        for k, (br, gr, w, dbr, dw) in enumerate(((s_ref, g0, ws, ds_ref, dws), (a_ref, g1, wa, da_ref, dwa),
                                                   (c_ref, g2, wc, dc_ref, dwc))):
            gate = _sigmoid(gr[...].astype(F32) + bg_ref[:, k * d:(k + 1) * d])
            brv = br[...]
            wv = w[...]
            y = _dot(brv, wv)
            merged = merged + gate * y
            dyb = (dm * gate).astype(BF16)
            dbr[...] = _dot_t1(dyb, wv).astype(BF16)
            dw[...] += _dot_t0(brv, dyb)
            dgl = dm * y * gate * (1.0 - gate)
            dgl_ref[:, k * d:(k + 1) * d] = dgl.astype(BF16)
            dbg_ref[:, k * d:(k + 1) * d] += jnp.sum(dgl, axis=0, keepdims=True)
        dwo[...] += _dot_t0(merged.astype(BF16), dxb)

    row = lambda w: pl.BlockSpec((tm, w), lambda i: (i, 0))
    wsp = lambda shape: pl.BlockSpec((None,) + tuple(shape), lambda i: (l, 0, 0))
    gls = [pl.BlockSpec((tm, d), functools.partial(lambda k, i: (i, k), k)) for k in range(3)]
    slabs = [(w, d) for w in widths] + [(d, d)]
    n_in = 12
    extra = [] if bufs is None else list(bufs)
    aliases = {} if bufs is None else {n_in + k: 5 + k for k in range(4)}
    return pl.pallas_call(
        body, grid=(n // tm,),
        in_specs=[row(d), *[row(w) for w in widths], *gls, pl.BlockSpec(bg.shape, lambda i: (0, 0)),
                  *[wsp(w.shape[1:]) for w in wbs], wsp(wout.shape[1:]), *[_ANY for _ in extra]],
        out_specs=[*[row(w) for w in widths], row(3 * d), pl.BlockSpec((1, 3 * d), lambda i: (0, 0)),
                   *[wsp(sh) for sh in slabs]],
        out_shape=[*[_sds((n, w), BF16) for w in widths], _sds((n, 3 * d), BF16), _sds((1, 3 * d), F32),
                   *[_sds((nl,) + sh, F32) for sh in slabs]],
        input_output_aliases=aliases,
        compiler_params=_cp("arbitrary"), name=name)(dx, *brs, gl, gl, gl, bg, *wbs, wout, *extra)


def _adamw(w, g, m, v, *, name):
    r, c = w.shape
    tm = _tile(r, 256)
    c1 = 1.0 - ADAM_B1 ** ADAM_STEP
    c2 = 1.0 - ADAM_B2 ** ADAM_STEP

    def body(w_ref, g_ref, m_ref, v_ref, d_ref, nm_ref, nv_ref):
        gv = g_ref[...]
        mn = ADAM_B1 * m_ref[...] + (1.0 - ADAM_B1) * gv
        vn = ADAM_B2 * v_ref[...] + (1.0 - ADAM_B2) * (gv * gv)
        nm_ref[...] = mn
        nv_ref[...] = vn
        d_ref[...] = -ADAM_LR * ((mn / c1) / (jnp.sqrt(vn / c2) + ADAM_EPS) + ADAM_WD * w_ref[...])

    blk = pl.BlockSpec((tm, c), lambda i: (i, 0))
    return pl.pallas_call(
        body, grid=(r // tm,), in_specs=[blk] * 4, out_specs=[blk] * 3,
        out_shape=[_sds((r, c), F32)] * 3, compiler_params=_cp("parallel"), name=name)(w, g, m, v)


def _add_sibling(g, recv, lyr, c_idx, *, name):
    _, a, b = g.shape
    ta = _tile(a, 256)

    def body(c_ref, g_ref, r_ref, o_ref):
        o_ref[...] = (g_ref[...] + r_ref[...]).astype(BF16)

    row = lambda i, cr: jnp.where(cr[0] == lyr, i, 0)
    return pl.pallas_call(
        body,
        grid_spec=pltpu.PrefetchScalarGridSpec(
            num_scalar_prefetch=1, grid=(a // ta,),
            in_specs=[pl.BlockSpec((None, ta, b), lambda i, cr: (lyr, row(i, cr), 0)),
                      pl.BlockSpec((ta, b), lambda i, cr: (row(i, cr), 0))],
            out_specs=pl.BlockSpec((ta, b), lambda i, cr: (row(i, cr), 0))),
        out_shape=_sds((a, b), BF16), compiler_params=_cp("arbitrary"), name=name)(c_idx, g, recv)


def _add_chips(rsum, parts, axis, s_idx, c_idx, lyr, buf, *, name):
    _, a, b = parts.shape
    ta = _tile(a, 256)
    na = a // ta

    def body(s_ref, c_ref, own_ref, p0, p1, p2, p3, *rest):
        o_ref = rest[-1]
        own = own_ref[...].astype(F32)
        terms = [jnp.where(s_ref[0] == s, own, p[...].astype(F32)) for s, p in enumerate((p0, p1, p2, p3))]
        o_ref[...] = ((terms[0] + terms[1]) + terms[2]) + terms[3]

    row = lambda i, cr: jnp.where(cr[0] == lyr, i, 0)
    own_spec = (pl.BlockSpec((ta, b), lambda i, sr, cr: (sr[0] * na + row(i, cr), 0)) if axis == 1
                else pl.BlockSpec((ta, b), lambda i, sr, cr: (row(i, cr), sr[0])))
    part_spec = lambda s: pl.BlockSpec((None, ta, b),
                                       lambda i, sr, cr: (jnp.where(sr[0] == s, s ^ 1, s), row(i, cr), 0))
    extra, extra_specs, out_shape, aliases = _slab_out(2, lyr, (a, b), buf, 7)
    return pl.pallas_call(
        body,
        grid_spec=pltpu.PrefetchScalarGridSpec(
            num_scalar_prefetch=2, grid=(na,),
            in_specs=[own_spec] + [part_spec(s) for s in range(N_CHIPS)] + extra_specs,
            out_specs=pl.BlockSpec((None, ta, b), lambda i, sr, cr: (lyr, row(i, cr), 0))),
        out_shape=out_shape, input_output_aliases=aliases, compiler_params=_cp("arbitrary"), name=name)(
            s_idx, c_idx, rsum, parts, parts, parts, parts, *extra)


def _place_shard(wloc, axis, s_idx, *, name):
    nl, a, b = wloc.shape
    ta = _tile(a, 256)
    na = a // ta
    full = (nl, a * N_CHIPS, b) if axis == 1 else (nl, a, b * N_CHIPS)

    def body(sc_ref, w_ref, o_ref):
        o_ref[...] = w_ref[...].astype(BF16)

    out_spec = (pl.BlockSpec((None, ta, b), lambda l, i, sc: (l, sc[0] * na + i, 0)) if axis == 1
                else pl.BlockSpec((None, ta, b), lambda l, i, sc: (l, i, sc[0])))
    return pl.pallas_call(
        body,
        grid_spec=pltpu.PrefetchScalarGridSpec(
            num_scalar_prefetch=1, grid=(nl, na),
            in_specs=[pl.BlockSpec((None, ta, b), lambda l, i, sc: (l, i, 0))], out_specs=out_spec),
        out_shape=_sds(full, BF16), compiler_params=_cp("parallel", "parallel"), name=name)(s_idx, wloc)


def _blockdiag(w):
    g, r, c = w.shape
    eye = jnp.eye(g, dtype=w.dtype)
    return (w[:, :, None, :] * eye[:, None, :, None]).reshape(g * r, g * c)


def _s5_prep(lre, lim, log_dt, b_re, b_im, c_re, c_im, d_skip):
    lr = jnp.minimum(lre, -1e-4)
    li = lim
    dt = jnp.exp(log_dt)[:, None]
    mag = jnp.exp(lr * dt)
    ar = mag * jnp.cos(li * dt)
    ai = mag * jnp.sin(li * dt)
    den = lr * lr + li * li
    coef_r = ((ar - 1.0) * lr + ai * li) / den
    coef_i = (ai * lr - (ar - 1.0) * li) / den
    bbar_r = coef_r[..., None] * b_re - coef_i[..., None] * b_im
    bbar_i = coef_r[..., None] * b_im + coef_i[..., None] * b_re
    a = jnp.stack([ar.reshape(-1), ai.reshape(-1)])
    return dict(
        a=a,
        bblk_r=_blockdiag(bbar_r.transpose(0, 2, 1)), bblk_i=_blockdiag(bbar_i.transpose(0, 2, 1)),
        cblk_r=_blockdiag(c_re.transpose(0, 2, 1)), cblk_in=_blockdiag(-c_im.transpose(0, 2, 1)),
        d=d_skip.reshape(1, -1))


def _s5_powers(a, nlog):
    ar, ai = a[0], a[1]
    pr, pi = ar, ai
    rows = []
    for _ in range(nlog):
        rows += [pr, pi]
        pr, pi = pr * pr - pi * pi, 2.0 * pr * pi
    qr, qi = [ar], [ai]
    for _ in range(SUBLANES - 1):
        qr, qi = qr + [qr[-1] * ar - qi[-1] * ai], qi + [qr[-1] * ai + qi[-1] * ar]
    p8 = jnp.stack(qr + qi)
    q8 = jnp.stack(qr[::-1] + [-v for v in qi[::-1]])
    return jnp.stack(rows), p8, q8


def _bias_table(rel_bias):
    h = rel_bias.shape[0]
    tq, tw = ATT_TQ, 3 * ATT_TQ
    n_hi = tw - 1 - MAX_REL + 1
    n_lo = tq + tw - 1 - n_hi - (2 * MAX_REL - 1)
    fr = jnp.concatenate([
        jnp.broadcast_to(rel_bias[:, 2 * MAX_REL:], (h, n_hi)),
        jnp.flip(rel_bias[:, 1:2 * MAX_REL], axis=1),
        jnp.broadcast_to(rel_bias[:, :1], (h, n_lo)),
        jnp.zeros((h, 1), rel_bias.dtype)], axis=1)
    ln = tq + tw
    flat = jnp.broadcast_to(fr[:, None, :], (h, tq, ln)).reshape(h, tq * ln)[:, :tq * (ln - 1)]
    tab = flat.reshape(h, tq, ln - 1)[:, :, tq - 1:tq - 1 + tw]
    qc = np.arange(tq)[:, None] // CHUNK + N_LEFT
    kc = np.arange(tw)[None, :] // CHUNK
    band = (kc <= qc) & (kc >= qc - N_LEFT)
    return jnp.where(jnp.asarray(band)[None], tab, NEG)


def _small_prep(w, l):
    g, p = w["s5_lambda_re"].shape[1:]
    b_shape, c_shape = (g, p, -1), (g, -1, p)
    sp = _s5_prep(w["s5_lambda_re"][l], w["s5_lambda_im"][l], w["s5_log_dt"][l], w["s5_b_re"][l].reshape(b_shape),
                  w["s5_b_im"][l].reshape(b_shape), w["s5_c_re"][l].reshape(c_shape), w["s5_c_im"][l].reshape(c_shape),
                  w["s5_d"][l])
    return sp, _bias_table(w["attn_rel_bias"][l])


_PREP_KEYS = ("s5_lambda_re", "s5_lambda_im", "s5_log_dt", "s5_b_re", "s5_b_im", "s5_c_re", "s5_c_im", "s5_d",
              "attn_rel_bias")
_BIG_KEYS = {"ffn1_w_up": 2, "ffn1_w_down": 1, "w_in": 2, "s5_w_glu": 2, "w_br_s5": 2, "w_br_attn": 2,
             "w_br_conv": 2, "w_out": 1, "ffn2_w_up": 2, "ffn2_w_down": 1}
_SMALL_KEYS = ("ffn1_norm", "mix_norm", "b_gate", "s5_lambda_re", "s5_lambda_im", "s5_log_dt", "s5_b_re", "s5_b_im",
               "s5_c_re", "s5_c_im", "s5_d", "attn_q_gain", "attn_k_gain", "attn_rel_bias", "conv_w_dw", "conv_b_dw",
               "conv_ln_g", "conv_ln_b", "ffn2_norm")
_WEIGHTS = ("ffn1_norm", "ffn1_w_up", "ffn1_w_down", "mix_norm", "w_in", "b_gate", "s5_lambda_re", "s5_lambda_im",
            "s5_log_dt", "s5_b_re", "s5_b_im", "s5_c_re", "s5_c_im", "s5_d", "s5_w_glu", "w_br_s5", "attn_q_gain",
            "attn_k_gain", "attn_rel_bias", "w_br_attn", "conv_w_dw", "conv_b_dw", "conv_ln_g", "conv_ln_b",
            "w_br_conv", "w_out", "ffn2_norm", "ffn2_w_up", "ffn2_w_down")


def _local_step(x3, target3, w, rs=None, gather=None):
    w = dict(w)
    bl, s, d = x3.shape
    nl = w["ffn1_norm"].shape[0]
    dff = w["ffn1_w_down"].shape[1]
    ds5 = w["s5_d"].shape[1]
    datt = w["w_br_attn"].shape[1]
    dc = w["conv_b_dw"].shape[1]
    n = bl * s
    x = x3.reshape(n, d)
    target = target3.reshape(n, d)
    tm = _tile(n, 512)
    tml = _tile(n, 1024)
    tmix = _tile(n, 256)
    ts5 = 256
    tconv = _tile(s, 512)
    tff = dff // 2
    ma = ds5 + 3 * datt + 2 * dc
    tna = ma // 3
    assert (3 * d) % tna == 0 and dff % 2 == 0
    qoff = ds5 // LANES
    koff = (ds5 + datt) // LANES
    acol = (ds5 + 3 * datt) // dc
    wbs = lambda: (w["w_br_s5"], w["w_br_attn"], w["w_br_conv"])

    def host(tag, l):
        if gather is None or l != 0:
            return None
        keys, kaxes, lyr = gather[tag]
        return [w[k] for k in keys], kaxes, lyr

    def hosted(tag, l, arrays):
        if gather is not None and l == 0:
            w.update(zip(gather[tag][0], arrays))

    saved = []
    for l in range(nl):
        (sp, bias), prep_vjp = jax.vjp(lambda ww: _small_prep(ww, l), {k: w[k] for k in _PREP_KEYS})
        spb = dict(sp)
        spb["pw"], spb["p8"], spb["q8"] = _s5_powers(lax.stop_gradient(sp["a"]), int(math.log2(ts5)))
        for k in ("bblk_r", "bblk_i", "cblk_r", "cblk_in"):
            spb[k] = sp[k].astype(BF16)
        g1 = w["ffn1_norm"][l][None]
        g2 = w["ffn2_norm"][l][None]
        gm = w["mix_norm"][l][None]
        gq2 = jnp.tile(w["attn_q_gain"][l], 2)[None]
        gk2 = jnp.tile(w["attn_k_gain"][l], 2)[None]
        wdw = jnp.pad(w["conv_w_dw"][l], ((0, HALO - CONV_W), (0, 0)))
        bdw, lng, lnb = w["conv_b_dw"][l][None], w["conv_ln_g"][l][None], w["conv_ln_b"][l][None]
        bg = w["b_gate"][l][None]

        x0 = x
        h1, ab1, *got = _norm_mm(x0, g1, w["ffn1_w_up"], l, tm=tml, tn=tff, ntiles=4, pieces=2, transposed=False,
                                 name=f"ffn1_up_{l}", comm=host("ffn1_up", l))
        hosted("ffn1_up", l, got)
        x1 = _ffn_down(ab1, w["ffn1_w_down"], l, x0, tm=tm, tk=tff, name=f"ffn1_down_{l}")
        h2, pa, *got = _norm_mm(x1, gm, w["w_in"], l, tm=tml, tn=tna, ntiles=3, pieces=1, transposed=True,
                                name=f"win_a_{l}", comm=host("win_a", l))
        hosted("win_a", l, got)
        pa = pa[0]
        gl = _mm_t(h2, w["w_in"], l, tm=tml, tn=tna, off=3, ntiles=3 * d // tna, name=f"win_g_{l}")
        xr, xi, yp, zg, s5o = _s5_fwd(pa, spb, w["s5_w_glu"], l, bl=bl, s=s, t=ts5, name=f"s5_fwd_{l}")
        atto, got = _attn_fwd(pa, gq2, gk2, bias, bl=bl, s=s, datt=datt, qoff=qoff, name=f"attn_fwd_{l}",
                              comm=host("attn_fwd", l))
        hosted("attn_fwd", l, got)
        hg, hc, convo = _conv_fwd(pa, wdw, bdw, lng, lnb, bl=bl, s=s, t=tconv, acol=acol, name=f"conv_fwd_{l}")
        brs = (s5o, atto, convo)
        x2 = _mix_out_fwd(x1, brs, gl, bg, wbs(), w["w_out"], l, tm=tmix, name=f"mix_fwd_{l}")
        h3, ab2, *got = _norm_mm(x2, g2, w["ffn2_w_up"], l, tm=tml, tn=tff, ntiles=4, pieces=2, transposed=False,
                                 name=f"ffn2_up_{l}", comm=host("ffn2_up", l))
        hosted("ffn2_up", l, got)
        x = _ffn_down(ab2, w["ffn2_w_down"], l, x2, tm=tm, tk=tff, name=f"ffn2_down_{l}")
        saved.append(dict(spb=spb, bias=bias, prep_vjp=prep_vjp, g1=g1, g2=g2, gm=gm, gq2=gq2, gk2=gk2,
                          wdw=wdw, lng=lng, lnb=lnb, bg=bg, x0=x0, h1=h1, ab1=ab1, x1=x1, h2=h2, pa=pa, gl=gl,
                          xr=xr, xi=xi, yp=yp, zg=zg, hg=hg, hc=hc, brs=brs, x2=x2, h3=h3, ab2=ab2))

    dx, lsum = _loss_grad(x, target, tm=tm, name="loss")
    loss_part = 0.5 * jnp.sum(lsum) / d

    big = {k: None for k in _BIG_KEYS}
    small = {k: [None] * nl for k in _SMALL_KEYS}
    hooks = {"pending": None, "early": None}
    assert rs is None or nl == 2
    for l in reversed(range(nl)):
        sv = saved[l]

        def ffn_bwd(dx, xin, h, ab, g, tag):
            wu, wd = w[tag + "_w_up"], w[tag + "_w_down"]
            comm = hooks["early"] if (tag == "ffn1" and l == 0) else None
            dab, big[tag + "_w_down"], *parts = _ffn_dact(dx, wd, l, ab, nl, big[tag + "_w_down"], tm=tm, tk=tff,
                                                          name=f"{tag}_dact_{l}", comm=comm)
            if comm is not None:
                rs.parts_early = parts
            big[tag + "_w_up"] = _mm_tn(h[None], dab, l, nl, big[tag + "_w_up"], ta=d, tb=tff, tk=tml,
                                        name=f"{tag}_dwu_{l}")
            comm = None
            if rs is not None and l == 0:
                keys = list(_BIG_KEYS) if tag == "ffn2" else rs.late
                comm = ([big[k] for k in keys], 1 if tag == "ffn2" else 0)
            dxo, dg, *recv = _ffn_dx(dab, wu, l, xin, g, dx, tm=tml, tk=tff, name=f"{tag}_dx_{l}", comm=comm)
            small[tag + "_norm"][l] = dg[0]
            if comm is not None and tag == "ffn2":
                hooks["pending"] = rs.sums(big, recv, 1, keys)
            elif comm is not None:
                rs.recv_late = recv
            return dxo

        dx = ffn_bwd(dx, sv["x2"], sv["h3"], sv["ab2"], sv["g2"], "ffn2")

        mix_keys = ("w_br_s5", "w_br_attn", "w_br_conv", "w_out")
        bufs = None if big["w_out"] is None else [big[k] for k in mix_keys]
        ds5o, datto, dconvo, dgl, dbg, *dws = _mix_out_bwd(
            dx, sv["brs"], sv["gl"], sv["bg"], wbs(), w["w_out"], l, nl, bufs, tm=tmix, name=f"mix_bwd_{l}")
        small["b_gate"][l] = dbg[0]
        big.update(zip(mix_keys, dws))

        dhc, dlng, dlnb = _conv_bwd_ln(dconvo, sv["hc"], sv["lng"], sv["lnb"], tm=tm, name=f"conv_bwd_ln_{l}")
        dz, dwdw, dbdw = _conv_bwd_dw(dhc, sv["hg"], sv["pa"], sv["wdw"], bl=bl, s=s, t=tconv, acol=acol,
                                      name=f"conv_bwd_dw_{l}")
        small["conv_w_dw"][l] = dwdw[:CONV_W]
        small["conv_b_dw"][l], small["conv_ln_g"][l], small["conv_ln_b"][l] = dbdw[0], dlng[0], dlnb[0]

        comm = hooks["pending"] if l == 0 else None
        dq, dkn, dvw, dbias, dgq, *hosted = _attn_bwd(datto, sv["pa"], sv["gq2"], sv["gk2"], sv["bias"], bl=bl, s=s,
                                                      datt=datt, qoff=qoff, name=f"attn_bwd_{l}", comm=comm)
        if comm is not None:
            rs.parts = hosted
        dk, dv, dgk = _attn_kv_bwd(dkn, dvw, sv["pa"], sv["gk2"], bl=bl, s=s, datt=datt, tm=_tile(s, 512), koff=koff,
                                   name=f"attn_kv_bwd_{l}")
        small["attn_q_gain"][l] = jnp.sum(dgq.reshape(-1, HEAD_DIM), axis=0)
        small["attn_k_gain"][l] = jnp.sum(dgk.reshape(-1, HEAD_DIM), axis=0)

        du, dd, dcr, dci, dbr, dbi, da, big["s5_w_glu"] = _s5_bwd(
            ds5o, sv["yp"], sv["zg"], sv["xr"], sv["xi"], sv["pa"], sv["spb"], w["s5_w_glu"], l, nl, big["s5_w_glu"],
            bl=bl, s=s, t=ts5, name=f"s5_bwd_{l}")
        prep_ct = (dict(a=da, bblk_r=dbr, bblk_i=dbi, cblk_r=dcr, cblk_in=dci, d=dd), jnp.sum(dbias, axis=0))
        (dprep,) = sv["prep_vjp"](prep_ct)
        for k in _PREP_KEYS:
            small[k][l] = dprep[k][l]

        dpa = jnp.concatenate([du, dq, dk, dv, dz], axis=1)
        big["w_in"] = _dwin_t(dpa, dgl, sv["h2"], l, nl, big["w_in"], ta=tna, tk=tml, name=f"dwin_{l}")
        comm = ([big[k] for k in rs.early], 0) if (rs is not None and l == 0) else None
        dx, dgm, *recv = _mix_dx(dpa, dgl, w["w_in"], l, sv["x1"], sv["gm"], dx, tm=tml, tk=tna, name=f"mix_dx_{l}",
                                 comm=comm)
        small["mix_norm"][l] = dgm[0]
        if comm is not None:
            hooks["early"] = rs.sums(big, recv, 0, rs.early)

        dx = ffn_bwd(dx, sv["x0"], sv["h1"], sv["ab1"], sv["g1"], "ffn1")

    small = {k: jnp.stack(v) for k, v in small.items()}
    return loss_part, dx.reshape(bl, s, d), big, small


def _place():
    x, y, c = lax.axis_index("x"), lax.axis_index("y"), lax.axis_index("c")
    chips = [(1 - x, y), (x, 1 - y), (1 - x, 1 - y)]
    return x, y, c, chips


def _remote(src, dst, send_sems, recv_sems, k, dev):
    return pltpu.make_async_remote_copy(src_ref=src, dst_ref=dst, send_sem=send_sems.at[k], recv_sem=recv_sems.at[k],
                                        device_id=dev, device_id_type=MESH)


def _window(ref, lead, s, axis, blk):
    if axis == 1:
        sl = (pl.ds(pl.multiple_of(s * blk, 16), blk), slice(None))
    else:
        sl = (slice(None), pl.ds(pl.multiple_of(s * blk, LANES), blk))
    return ref.at[sl] if lead is None else ref.at[(lead,) + sl]


def _gather_ops(bufs, axes, send_sems, recv_sems, lyr):
    x, y, c, chips = _place()
    s_me = 2 * x + y
    sibling = (x, y, 1 - lyr)
    nw = len(bufs)
    blks = [f.shape[ax] // N_CHIPS for f, ax in zip(bufs, axes)]
    win = lambda i, s: _window(bufs[i], lyr, s, axes[i], blks[i])
    pairs = [(i, j, cx, cy) for i in range(nw) for j, (cx, cy) in enumerate(chips)]
    sends = [_remote(win(i, s_me), win(i, s_me), send_sems, recv_sems, 6 * i + j, (cx, cy, lyr)) for i, j, cx, cy in pairs]
    passed = [_remote(win(i, 2 * cx + cy), win(i, 2 * cx + cy), send_sems, recv_sems, 6 * i + 3 + j, sibling)
              for i, j, cx, cy in pairs]

    def start():
        @pl.when(c == lyr)
        def _():
            for cp in sends:
                cp.start()

    def forward():
        @pl.when(c == lyr)
        def _():
            for (i, j, cx, cy), fw in zip(pairs, passed):
                piece = win(i, 2 * cx + cy)
                _remote(piece, piece, send_sems, recv_sems, 6 * i + j, (cx, cy, lyr)).wait_recv()
                fw.start()

    def wait():
        @pl.when(c == lyr)
        def _():
            for cp in sends + passed:
                cp.wait_send()

        @pl.when(c != lyr)
        def _():
            for fw in passed:
                fw.wait_recv()

    return start, forward, wait


def _all_gather_weights(fulls, axes, taps):
    nw = len(fulls)

    def body(*refs):
        taps_in = refs[nw]
        outs, taps_out = refs[nw + 1:2 * nw + 1], refs[2 * nw + 1]
        send_sems, recv_sems, tap_send, tap_recv, local_sem = refs[-5:]
        x, y, c, chips = _place()
        s_me = 2 * x + y
        start, forward, wait = _gather_ops(outs, axes, send_sems, recv_sems, 0)
        own_taps = pltpu.make_async_copy(taps_in, taps_out.at[s_me], local_sem)
        own_taps.start()
        tap_sends = [_remote(taps_in, taps_out.at[s_me], tap_send, tap_recv, j, (cx, cy, c))
                     for j, (cx, cy) in enumerate(chips)]
        for cp in tap_sends:
            cp.start()
        start()
        forward()
        wait()
        for j, (cx, cy) in enumerate(chips):
            slab = taps_out.at[2 * cx + cy]
            _remote(slab, slab, tap_send, tap_recv, j, (cx, cy, c)).wait_recv()
        for cp in tap_sends:
            cp.wait_send()
        own_taps.wait()

    return pl.pallas_call(
        body, in_specs=[_ANY] * (nw + 1), out_specs=[_ANY] * (nw + 1),
        out_shape=[_sds(f.shape, f.dtype) for f in fulls] + [_sds((N_CHIPS,) + taps.shape, taps.dtype)],
        input_output_aliases={i: i for i in range(nw)},
        scratch_shapes=[pltpu.SemaphoreType.DMA((6 * nw,)), pltpu.SemaphoreType.DMA((6 * nw,)),
                        pltpu.SemaphoreType.DMA((3,)), pltpu.SemaphoreType.DMA((3,)), pltpu.SemaphoreType.DMA],
        name="all_gather_weights")(*fulls, taps)


def _hosted_gather(comm, n_in, n_out):
    if comm is None:
        return [], [], [], [], [], {}
    bufs = list(comm[0])
    nw = len(bufs)
    return (bufs, [_ANY] * nw, [_ANY] * nw, [_sds(f.shape, f.dtype) for f in bufs],
            [pltpu.SemaphoreType.DMA((6 * nw,)), pltpu.SemaphoreType.DMA((6 * nw,))],
            {n_in + k: n_out + k for k in range(nw)})


def _hosted_gather_steps(comm, out_refs, sems, step, total):
    start, forward, wait = _gather_ops(out_refs, comm[1], sems[0], sems[1], comm[2])

    @pl.when(step == 0)
    def _():
        start()

    def finish():
        @pl.when(step == (3 * total) // 4)
        def _():
            forward()

        @pl.when(step == total - 1)
        def _():
            wait()

    return finish


def _swap_ops(ins, outs, send_sems, recv_sems, lyr):
    x, y, c, _ = _place()
    cps = [_remote(ins[i].at[lyr], outs[i], send_sems, recv_sems, i, (x, y, lyr)) for i in range(len(ins))]

    def start():
        @pl.when(c != lyr)
        def _():
            for cp in cps:
                cp.start()

    def wait():
        @pl.when(c != lyr)
        def _():
            for cp in cps:
                cp.wait_send()

        @pl.when(c == lyr)
        def _():
            for cp in cps:
                cp.wait_recv()

    return start, wait


def _exchange_ops(ins, outs, send_sems, recv_sems, axes, lyr):
    x, y, c, chips = _place()
    s_me = 2 * x + y
    nw = len(ins)
    blks = [r.shape[ax - 1] // N_CHIPS for r, ax in zip(ins, axes)]
    win = lambda i, s: _window(ins[i], None, s, axes[i], blks[i])
    sends = [_remote(win(i, 2 * cx + cy), outs[i].at[s_me], send_sems, recv_sems, 3 * i + j, (cx, cy, lyr))
             for i in range(nw) for j, (cx, cy) in enumerate(chips)]

    def start():
        @pl.when(c == lyr)
        def _():
            for cp in sends:
                cp.start()

    def wait():
        @pl.when(c == lyr)
        def _():
            for i in range(nw):
                for j, (cx, cy) in enumerate(chips):
                    slab = outs[i].at[2 * cx + cy]
                    _remote(slab, slab, send_sems, recv_sems, 3 * i + j, (cx, cy, lyr)).wait_recv()
            for cp in sends:
                cp.wait_send()

    return start, wait


def _parts_shapes(rsums, axes):
    shard = [tuple(dim // N_CHIPS if i == ax - 1 else dim for i, dim in enumerate(r.shape)) for r, ax in zip(rsums, axes)]
    return [_sds((N_CHIPS,) + sh, r.dtype) for sh, r in zip(shard, rsums)]


def _rs_exchange(rsums, axes, lyr):
    nw = len(rsums)

    def body(*refs):
        start, wait = _exchange_ops(refs[:nw], refs[nw:2 * nw], refs[-2], refs[-1], axes, lyr)
        start()
        wait()

    return pl.pallas_call(
        body, in_specs=[_ANY] * nw, out_specs=[_ANY] * nw, out_shape=_parts_shapes(rsums, axes),
        scratch_shapes=[pltpu.SemaphoreType.DMA((3 * nw,)), pltpu.SemaphoreType.DMA((3 * nw,))],
        name=f"rs_exchange_l{lyr}")(*rsums)


def _rs_join(ts):
    nw = len(ts)

    def body(*refs):
        outs = refs[nw:2 * nw]
        send_sems, recv_sems = refs[-2:]
        x, y, c, _ = _place()
        sends = [_remote(outs[i].at[c], outs[i].at[c], send_sems, recv_sems, i, (x, y, 1 - c)) for i in range(nw)]
        for cp in sends:
            cp.start()
        for i in range(nw):
            slab = outs[i].at[1 - c]
            _remote(slab, slab, send_sems, recv_sems, i, (x, y, 1 - c)).wait_recv()
        for cp in sends:
            cp.wait_send()

    return pl.pallas_call(
        body, in_specs=[_ANY] * nw, out_specs=[_ANY] * nw, out_shape=[_sds(t.shape, t.dtype) for t in ts],
        input_output_aliases={i: i for i in range(nw)},
        scratch_shapes=[pltpu.SemaphoreType.DMA((nw,)), pltpu.SemaphoreType.DMA((nw,))],
        name="rs_join_layers")(*ts)


def _all_reduce_small(arrs):
    na = len(arrs)
    nd = 8

    def body(*refs):
        ins, outs, recvs = refs[:na], refs[na:2 * na], refs[2 * na:3 * na]
        send_sems, recv_sems = refs[-2:]
        x, y, c, _ = _place()
        me = 4 * x + 2 * y + c
        for i in range(na):
            recvs[i][0] = ins[i][...]
        cps = []
        for rel in range(1, nd):
            dev = (1 - x if rel & 4 else x, 1 - y if rel & 2 else y, 1 - c if rel & 1 else c)
            for i in range(na):
                cp = _remote(ins[i], recvs[i].at[rel], send_sems, recv_sems, (rel - 1) * na + i, dev)
                cp.start()
                cps.append(cp)
        for rel in range(1, nd):
            for i in range(na):
                _remote(ins[i], recvs[i].at[rel], send_sems, recv_sems, (rel - 1) * na + i, (x, y, c)).wait_recv()
        for i in range(na):
            acc = recvs[i][me]
            for dv in range(1, nd):
                acc = acc + recvs[i][lax.bitwise_xor(me, dv)]
            outs[i][...] = acc
        for cp in cps:
            cp.wait_send()

    vm = pl.BlockSpec(memory_space=pltpu.VMEM)
    nsem = (nd - 1) * na
    return pl.pallas_call(
        body, in_specs=[vm] * na, out_specs=[vm] * na, out_shape=[_sds(t.shape, F32) for t in arrs],
        scratch_shapes=[pltpu.VMEM((nd,) + t.shape, F32) for t in arrs]
        + [pltpu.SemaphoreType.DMA((nsem,)), pltpu.SemaphoreType.DMA((nsem,))],
        compiler_params=pltpu.CompilerParams(vmem_limit_bytes=VMEM_LIMIT), name="all_reduce_small")(*arrs)


def _adamw_small(ws, gs, ms, vs):
    na = len(ws)
    c1 = 1.0 - ADAM_B1 ** ADAM_STEP
    c2 = 1.0 - ADAM_B2 ** ADAM_STEP

    def body(*refs):
        w_r, g_r, m_r, v_r = (refs[k * na:(k + 1) * na] for k in range(4))
        d_r, nm_r, nv_r = (refs[(4 + k) * na:(5 + k) * na] for k in range(3))
        for i in range(na):
            gv = g_r[i][...]
            mn = ADAM_B1 * m_r[i][...] + (1.0 - ADAM_B1) * gv
            vn = ADAM_B2 * v_r[i][...] + (1.0 - ADAM_B2) * (gv * gv)
            nm_r[i][...] = mn
            nv_r[i][...] = vn
            d_r[i][...] = -ADAM_LR * ((mn / c1) / (jnp.sqrt(vn / c2) + ADAM_EPS) + ADAM_WD * w_r[i][...])

    vm = pl.BlockSpec(memory_space=pltpu.VMEM)
    res = pl.pallas_call(
        body, in_specs=[vm] * (4 * na), out_specs=[vm] * (3 * na), out_shape=[_sds(t.shape, F32) for t in ws] * 3,
        compiler_params=pltpu.CompilerParams(vmem_limit_bytes=VMEM_LIMIT), name="adamw_small")(*ws, *gs, *ms, *vs)
    return res[:na], res[na:2 * na], res[2 * na:]


def kernel(x, ffn1_norm, ffn1_w_up, ffn1_w_down, mix_norm, w_in, b_gate, s5_lambda_re, s5_lambda_im, s5_log_dt, s5_b_re, s5_b_im, s5_c_re, s5_c_im, s5_d, s5_w_glu, w_br_s5, attn_q_gain, attn_k_gain, attn_rel_bias, w_br_attn, conv_w_dw, conv_b_dw, conv_ln_g, conv_ln_b, w_br_conv, w_out, ffn2_norm, ffn2_w_up, ffn2_w_down, loss_target, m_ffn1_norm, m_ffn1_w_up, m_ffn1_w_down, m_mix_norm, m_w_in, m_b_gate, m_s5_lambda_re, m_s5_lambda_im, m_s5_log_dt, m_s5_b_re, m_s5_b_im, m_s5_c_re, m_s5_c_im, m_s5_d, m_s5_w_glu, m_w_br_s5, m_attn_q_gain, m_attn_k_gain, m_attn_rel_bias, m_w_br_attn, m_conv_w_dw, m_conv_b_dw, m_conv_ln_g, m_conv_ln_b, m_w_br_conv, m_w_out, m_ffn2_norm, m_ffn2_w_up, m_ffn2_w_down, v_ffn1_norm, v_ffn1_w_up, v_ffn1_w_down, v_mix_norm, v_w_in, v_b_gate, v_s5_lambda_re, v_s5_lambda_im, v_s5_log_dt, v_s5_b_re, v_s5_b_im, v_s5_c_re, v_s5_c_im, v_s5_d, v_s5_w_glu, v_w_br_s5, v_attn_q_gain, v_attn_k_gain, v_attn_rel_bias, v_w_br_attn, v_conv_w_dw, v_conv_b_dw, v_conv_ln_g, v_conv_ln_b, v_w_br_conv, v_w_out, v_ffn2_norm, v_ffn2_w_up, v_ffn2_w_down):
    a = dict(locals())
    xi, yi, ci = lax.axis_index("x"), lax.axis_index("y"), lax.axis_index("c")
    s_me = 2 * xi + yi
    big_keys = list(_BIG_KEYS)
    axes = [1 if k == "w_in" else _BIG_KEYS[k] for k in big_keys]

    s_idx = s_me.astype(jnp.int32).reshape(1)
    c_idx = ci.astype(jnp.int32).reshape(1)
    placed = {k: _place_shard(jnp.swapaxes(a[k], 1, 2).astype(BF16) if k == "w_in" else a[k], ax, s_idx,
                              name=f"place_{k}") for k, ax in zip(big_keys, axes)}
    axis_of = dict(zip(big_keys, axes))
    first = ["ffn1_w_up", "ffn1_w_down"]
    *fulls, taps = _all_gather_weights([placed[k] for k in first], [axis_of[k] for k in first], a["conv_w_dw"])
    placed.update(zip(first, fulls))
    flat = lambda t: t.reshape(t.shape[0], t.shape[1], -1) if t.ndim == 4 else t
    w = {k: flat(a[k]) for k in _WEIGHTS}
    w.update(placed)
    w["conv_w_dw"] = jnp.moveaxis(taps, 0, 2).reshape(taps.shape[1], taps.shape[2], -1)

    class _ReduceScatter:
        parts = parts_early = recv_late = None

        def __init__(self):
            self.rsums = {}
            self.late = ["ffn1_w_up", "ffn1_w_down"]
            self.early = [k for k in big_keys if k not in self.late]

        def sums(self, big, recv, lyr, keys):
            rsums = [_add_sibling(big[k], r, lyr, c_idx, name=f"rs_add_sibling_{k}_l{lyr}") for r, k in zip(recv, keys)]
            self.rsums.update({(k, lyr): r for k, r in zip(keys, rsums)})
            return rsums, [axis_of[k] for k in keys], lyr

    rs = _ReduceScatter()
    early = ("ffn1_w_up", "ffn1_w_down", "w_in")
    second = ["ffn2_w_up", "ffn2_w_down"]
    sets = {"ffn1_up": ([k for k in big_keys if k not in first + second], 0),
            "win_a": (second, 0),
            "attn_fwd": ([k for k in big_keys if k in early], 1),
            "ffn2_up": ([k for k in big_keys if k not in early], 1)}
    gather = {tag: (keys, [axis_of[k] for k in keys], lyr) for tag, (keys, lyr) in sets.items()}
    loss_part, grad_x, gbig, gsmall = _local_step(a["x"], a["loss_target"], w, rs, gather)
    loss = lax.psum(loss_part, ("x", "y", "c"))

    rsums_late, axes_late, _ = rs.sums(gbig, rs.recv_late, 0, rs.late)
    parts0 = dict(zip(rs.early, rs.parts_early))
    parts0.update(zip(rs.late, _rs_exchange(rsums_late, axes_late, 0)))
    mine = [None] * len(big_keys)
    for lyr, parts in ((1, dict(zip(big_keys, rs.parts))), (0, parts0)):
        mine = [_add_chips(rs.rsums[k, lyr], parts[k], ax, s_idx, c_idx, lyr, buf, name=f"rs_add_chips_{k}_l{lyr}")
                for ax, buf, k in zip(axes, mine, big_keys)]
    gb = dict(zip(big_keys, _rs_join(mine)))
    gb["w_in"] = jnp.swapaxes(gb["w_in"], 1, 2)

    small_keys = list(_SMALL_KEYS)
    gs = dict(zip(small_keys, _all_reduce_small([gsmall[k] for k in small_keys])))
    blk = a["conv_w_dw"].shape[2]
    gs["conv_w_dw"] = lax.dynamic_slice_in_dim(gs["conv_w_dw"], s_me * blk, blk, axis=2)

    delta, new_m, new_v = {}, {}, {}
    for k in big_keys:
        shp = a[k].shape
        two_d = lambda t: t.reshape(-1, shp[-1])
        d_, m_, v_ = _adamw(two_d(a[k]), two_d(gb[k]), two_d(a["m_" + k]), two_d(a["v_" + k]), name=f"adamw_{k}")
        delta[k], new_m[k], new_v[k] = d_.reshape(shp), m_.reshape(shp), v_.reshape(shp)
    res = _adamw_small([flat(a[k]) for k in small_keys], [gs[k] for k in small_keys],
                       [flat(a["m_" + k]) for k in small_keys], [flat(a["v_" + k]) for k in small_keys])
    for dst, vals in zip((delta, new_m, new_v), res):
        dst.update({k: t.reshape(a[k].shape) for k, t in zip(small_keys, vals)})
    grads = {**gb, **{k: t.reshape(a[k].shape) for k, t in gs.items()}}

    return (loss, grad_x, *[grads[k] for k in _WEIGHTS], *[delta[k] for k in _WEIGHTS],
            *[new_m[k] for k in _WEIGHTS], *[new_v[k] for k in _WEIGHTS])
```

```python
import functools
import math

import numpy as np
import jax
import jax.numpy as jnp
from jax import lax
from jax.experimental import pallas as pl
from jax.experimental.pallas import tpu as pltpu

F32 = jnp.float32
BF16 = jnp.bfloat16
EPS = 1e-6
VMEM_LIMIT = 56 * 1024 * 1024
LANES = 128
HEAD_DIM = 64
CHUNK = 64
N_LEFT = 8
MAX_REL = 128
ATT_TQ = 256
CONV_W = 31
HALO = 32
ROW_CHUNK = 256
SUBLANES = 8
GROUP_LOG = 3
NEG = -1e30
N_CHIPS = 4
PACK_COLS = 1024

ADAM_LR = 0.001
ADAM_B1 = 0.9
ADAM_B2 = 0.999
ADAM_EPS = 1e-08
ADAM_WD = 0.01
ADAM_STEP = 10

MESH = pl.DeviceIdType.MESH
_ANY = pl.BlockSpec(memory_space=pl.ANY)


def _cp(*sem):
    return pltpu.CompilerParams(dimension_semantics=sem, vmem_limit_bytes=VMEM_LIMIT)


def _sds(shape, dtype):
    return jax.ShapeDtypeStruct(shape, dtype)


def _tile(n, pref):
    t = min(n, pref)
    while n % t:
        t -= 8
    return t


def _sigmoid(x):
    return jax.nn.sigmoid(x)


_GELU_C = math.sqrt(2.0 / math.pi)


def _gelu(y):
    return 0.5 * y * (1.0 + jnp.tanh(_GELU_C * (y + 0.044715 * y * y * y)))


def _gelu_grad(y):
    th = jnp.tanh(_GELU_C * (y + 0.044715 * y * y * y))
    return 0.5 * (1.0 + th) + 0.5 * y * (1.0 - th * th) * _GELU_C * (1.0 + 3.0 * 0.044715 * y * y)


def _dot(a, b):
    return jnp.dot(a, b, preferred_element_type=F32)


def _dot_t0(a, b):
    return lax.dot_general(a, b, (((0,), (0,)), ((), ())), preferred_element_type=F32)


def _dot_t1(a, b):
    return lax.dot_general(a, b, (((1,), (1,)), ((), ())), preferred_element_type=F32)


def _slab_out(nl, l, shape, buf, n_in):
    sds = _sds((nl,) + tuple(shape), F32)
    if buf is None:
        return [], [], sds, {}
    return [buf], [_ANY], sds, {n_in: 0}


def _norm_mm(x, g, w, l, *, tm, tn, ntiles, pieces, transposed, name, comm=None):
    n, d = x.shape
    m = ntiles * tn
    mp = m // pieces
    npj = mp // tn
    nc = 0 if comm is None else len(comm[0])

    def body(x_ref, g_ref, w_ref, *rest):
        h_ref, y_ref = rest[nc:nc + 2]
        h_scr = rest[2 * nc + 2]
        if comm is not None:
            finish = _hosted_gather_steps(comm, rest[nc + 2:2 * nc + 2], rest[-2:],
                                          pl.program_id(0) * ntiles + pl.program_id(1), (n // tm) * ntiles)

        @pl.when(pl.program_id(1) == 0)
        def _():
            for r0 in range(0, tm, ROW_CHUNK):
                rows = slice(r0, r0 + ROW_CHUNK)
                xv = x_ref[rows, :]
                r = lax.rsqrt(jnp.mean(xv * xv, axis=-1, keepdims=True) + EPS)
                hb = (xv * r * g_ref[...]).astype(BF16)
                h_scr[rows, :] = hb
                h_ref[rows, :] = hb

        mm = _dot_t1 if transposed else _dot
        y_ref[...] = mm(h_scr[...], w_ref[...]).astype(BF16)
        if comm is not None:
            finish()

    wspec = (pl.BlockSpec((None, tn, d), lambda i, j: (l, j, 0)) if transposed
             else pl.BlockSpec((None, d, tn), lambda i, j: (l, 0, j)))
    c_in, c_ispec, c_ospec, c_oshape, c_scr, aliases = _hosted_gather(comm, 3, 2)
    return pl.pallas_call(
        body, grid=(n // tm, ntiles),
        in_specs=[pl.BlockSpec((tm, d), lambda i, j: (i, 0)), pl.BlockSpec((1, d), lambda i, j: (0, 0)), wspec] + c_ispec,
        out_specs=[pl.BlockSpec((tm, d), lambda i, j: (i, 0)),
                   pl.BlockSpec((None, tm, tn), lambda i, j: (j // npj, i, j % npj))] + c_ospec,
        out_shape=[_sds((n, d), BF16), _sds((pieces, n, mp), BF16)] + c_oshape,
        input_output_aliases=aliases,
        scratch_shapes=[pltpu.VMEM((tm, d), BF16)] + c_scr,
        compiler_params=_cp("arbitrary", "arbitrary"), name=name)(x, g, w, *c_in)


def _mm_t(a, w, l, *, tm, tn, off, ntiles, name):
    n, k = a.shape

    def body(a_ref, w_ref, y_ref):
        y_ref[...] = _dot_t1(a_ref[...], w_ref[...]).astype(BF16)

    return pl.pallas_call(
        body, grid=(n // tm, ntiles),
        in_specs=[pl.BlockSpec((tm, k), lambda i, j: (i, 0)), pl.BlockSpec((None, tn, k), lambda i, j: (l, off + j, 0))],
        out_specs=pl.BlockSpec((tm, tn), lambda i, j: (i, j)),
        out_shape=_sds((n, ntiles * tn), BF16),
        compiler_params=_cp("parallel", "arbitrary"), name=name)(a, w)


def _ffn_down(ab, wd, l, x, *, tm, tk, name):
    _, n, dff = ab.shape
    d = x.shape[1]
    nk = dff // tk

    def body(a_ref, b_ref, wd_ref, x_ref, o_ref, acc):
        k = pl.program_id(1)
        a = a_ref[...].astype(F32)
        b = b_ref[...].astype(F32)
        act = (a * _sigmoid(a) * b).astype(BF16)
        part = _dot(act, wd_ref[pl.ds(pl.multiple_of(k * tk, tk), tk), :])

        @pl.when(k == 0)
        def _():
            acc[...] = part

        @pl.when(k > 0)
        def _():
            acc[...] += part

        @pl.when(k == nk - 1)
        def _():
            o_ref[...] = x_ref[...] + 0.5 * acc[...]

    return pl.pallas_call(
        body, grid=(n // tm, nk),
        in_specs=[pl.BlockSpec((None, tm, tk), lambda i, k: (0, i, k)),
                  pl.BlockSpec((None, tm, tk), lambda i, k: (1, i, k)),
                  pl.BlockSpec((None, dff, d), lambda i, k: (l, 0, 0)),
                  pl.BlockSpec((tm, d), lambda i, k: (i, 0))],
        out_specs=pl.BlockSpec((tm, d), lambda i, k: (i, 0)),
        out_shape=_sds((n, d), F32),
        scratch_shapes=[pltpu.VMEM((tm, d), F32)],
        compiler_params=_cp("parallel", "arbitrary"), name=name)(ab, ab, wd, x)


def _ffn_dact(dx, wd, l, ab, nl, dwd_buf, *, tm, tk, name, comm=None):
    n, d = dx.shape
    dff = ab.shape[2]
    half = ((tk // LANES + 1) // 2) * LANES
    chunks = ((0, half), (half, tk))
    ne = 0 if dwd_buf is None else 1
    nc = 0 if comm is None else len(comm[0])
    ni = n // tm

    def body(dx_ref, wd_ref, a_ref, b_ref, *rest):
        dab_ref, dwd_ref = rest[ne + nc:ne + nc + 2]
        if comm is not None:
            start, wait = _exchange_ops(rest[ne:ne + nc], rest[ne + nc + 2:ne + 2 * nc + 2], rest[-2], rest[-1],
                                        comm[1], comm[2])
            step = pl.program_id(0) * ni + pl.program_id(1)

            @pl.when(step == 0)
            def _():
                start()

        do = (0.5 * dx_ref[...]).astype(BF16)

        @pl.when(pl.program_id(1) == 0)
        def _():
            dwd_ref[...] = jnp.zeros_like(dwd_ref)

        for c0, c1 in chunks:
            dact = _dot_t1(do, wd_ref[c0:c1, :])
            a = a_ref[:, c0:c1].astype(F32)
            b = b_ref[:, c0:c1].astype(F32)
            sg = _sigmoid(a)
            silu = a * sg
            dab_ref[0, :, c0:c1] = (dact * b * (sg * (1.0 + a * (1.0 - sg)))).astype(BF16)
            dab_ref[1, :, c0:c1] = (dact * silu).astype(BF16)
            dwd_ref[c0:c1, :] += _dot_t0((silu * b).astype(BF16), do)

        if comm is not None:
            @pl.when(step == (dff // tk) * ni - 1)
            def _():
                wait()

    extra, extra_specs, dwd_shape, aliases = _slab_out(nl, l, (dff, d), dwd_buf, 4)
    aliases = {k: 1 for k in aliases}
    comm_in = [] if comm is None else list(comm[0])
    comm_out = [] if comm is None else _parts_shapes(comm[0], comm[1])
    comm_scr = [] if comm is None else [pltpu.SemaphoreType.DMA((3 * nc,)), pltpu.SemaphoreType.DMA((3 * nc,))]
    return pl.pallas_call(
        body, grid=(dff // tk, ni),
        in_specs=[pl.BlockSpec((tm, d), lambda j, i: (i, 0)),
                  pl.BlockSpec((None, tk, d), lambda j, i: (l, j, 0)),
                  pl.BlockSpec((None, tm, tk), lambda j, i: (0, i, j)),
                  pl.BlockSpec((None, tm, tk), lambda j, i: (1, i, j)), *extra_specs] + [_ANY] * nc,
        out_specs=[pl.BlockSpec((2, tm, tk), lambda j, i: (0, i, j)),
                   pl.BlockSpec((None, tk, d), lambda j, i: (l, j, 0))] + [_ANY] * nc,
        out_shape=[_sds((2, n, dff), BF16), dwd_shape] + comm_out,
        input_output_aliases=aliases, scratch_shapes=comm_scr,
        compiler_params=_cp("arbitrary", "arbitrary"), name=name)(dx, wd, ab, ab, *extra, *comm_in)


def _rms_bwd_epilogue(acc, x_ref, g_ref, dres_ref, dx_ref, dg_ref, i):
    dgp = jnp.zeros(dg_ref.shape, F32)
    for r0 in range(0, acc.shape[0], ROW_CHUNK):
        rows = slice(r0, r0 + ROW_CHUNK)
        dh = acc[rows, :]
        xv = x_ref[rows, :]
        r = lax.rsqrt(jnp.mean(xv * xv, axis=-1, keepdims=True) + EPS)
        xn = xv * r
        dgp = dgp + jnp.sum(dh * xn, axis=0, keepdims=True)
        dxh = dh * g_ref[...]
        dx_ref[rows, :] = dres_ref[rows, :] + r * (dxh - xn * jnp.mean(dxh * xn, axis=-1, keepdims=True))

    @pl.when(i == 0)
    def _():
        dg_ref[...] = dgp

    @pl.when(i > 0)
    def _():
        dg_ref[...] += dgp


def _ffn_dx(dab, wu, l, x, g, dres, *, tm, tk, name, comm=None):
    p, n, mp = dab.shape
    d = x.shape[1]
    nkp = mp // tk
    nk = p * nkp
    ni = n // tm
    nc = 0 if comm is None else len(comm[0])

    def body(dy_ref, w_ref, x_ref, g_ref, dres_ref, *rest):
        dx_ref, dg_ref = rest[nc:nc + 2]
        acc = rest[2 * nc + 2]
        k = pl.program_id(1)
        if comm is not None:
            start, wait = _swap_ops(rest[:nc], rest[nc + 2:2 * nc + 2], rest[-2], rest[-1], comm[1])

            @pl.when((pl.program_id(0) == 0) & (k == 0))
            def _():
                start()

        part = _dot_t1(dy_ref[...], w_ref[...])

        @pl.when(k == 0)
        def _():
            acc[...] = part

        @pl.when(k > 0)
        def _():
            acc[...] += part

        @pl.when(k == nk - 1)
        def _():
            _rms_bwd_epilogue(acc, x_ref, g_ref, dres_ref, dx_ref, dg_ref, pl.program_id(0))

        if comm is not None:
            @pl.when((pl.program_id(0) == ni - 1) & (k == nk - 1))
            def _():
                wait()

    comm_in = [] if comm is None else list(comm[0])
    comm_out = [_sds(t.shape[1:], t.dtype) for t in comm_in]
    comm_scr = [] if comm is None else [pltpu.SemaphoreType.DMA((nc,)), pltpu.SemaphoreType.DMA((nc,))]
    return pl.pallas_call(
        body, grid=(ni, nk),
        in_specs=[pl.BlockSpec((None, tm, tk), lambda i, k: (k // nkp, i, k % nkp)),
                  pl.BlockSpec((None, d, tk), lambda i, k: (l, 0, k)),
                  pl.BlockSpec((tm, d), lambda i, k: (i, 0)),
                  pl.BlockSpec((1, d), lambda i, k: (0, 0)),
                  pl.BlockSpec((tm, d), lambda i, k: (i, 0))] + [_ANY] * nc,
        out_specs=[pl.BlockSpec((tm, d), lambda i, k: (i, 0)), pl.BlockSpec((1, d), lambda i, k: (0, 0))] + [_ANY] * nc,
        out_shape=[_sds((n, d), F32), _sds((1, d), F32)] + comm_out,
        scratch_shapes=[pltpu.VMEM((tm, d), F32)] + comm_scr,
        compiler_params=_cp("arbitrary", "arbitrary"), name=name)(dab, wu, x, g, dres, *comm_in)


def _mix_dx(dpa, dgl, wt, l, x, g, dres, *, tm, tk, name, comm=None):
    n, d = x.shape
    n1 = dpa.shape[1] // tk
    n2 = dgl.shape[1] // tk
    nk = n1 + n2
    ni = n // tm
    nc = 0 if comm is None else len(comm[0])

    def body(d1_ref, d2_ref, w_ref, x_ref, g_ref, dres_ref, *rest):
        dx_ref, dg_ref = rest[nc:nc + 2]
        acc = rest[2 * nc + 2]
        k = pl.program_id(1)
        if comm is not None:
            start, wait = _swap_ops(rest[:nc], rest[nc + 2:2 * nc + 2], rest[-2], rest[-1], comm[1])

            @pl.when((pl.program_id(0) == 0) & (k == 0))
            def _():
                start()

        @pl.when(k == 0)
        def _():
            acc[...] = _dot(d1_ref[...], w_ref[...])

        @pl.when((k > 0) & (k < n1))
        def _():
            acc[...] += _dot(d1_ref[...], w_ref[...])

        @pl.when(k >= n1)
        def _():
            acc[...] += _dot(d2_ref[...], w_ref[...])

        @pl.when(k == nk - 1)
        def _():
            _rms_bwd_epilogue(acc, x_ref, g_ref, dres_ref, dx_ref, dg_ref, pl.program_id(0))

        if comm is not None:
            @pl.when((pl.program_id(0) == ni - 1) & (k == nk - 1))
            def _():
                wait()

    comm_in = [] if comm is None else list(comm[0])
    comm_out = [_sds(t.shape[1:], t.dtype) for t in comm_in]
    comm_scr = [] if comm is None else [pltpu.SemaphoreType.DMA((nc,)), pltpu.SemaphoreType.DMA((nc,))]
    return pl.pallas_call(
        body, grid=(ni, nk),
        in_specs=[pl.BlockSpec((tm, tk), lambda i, k: (i, jnp.minimum(k, n1 - 1))),
                  pl.BlockSpec((tm, tk), lambda i, k: (i, jnp.maximum(k - n1, 0))),
                  pl.BlockSpec((None, tk, d), lambda i, k: (l, k, 0)),
                  pl.BlockSpec((tm, d), lambda i, k: (i, 0)),
                  pl.BlockSpec((1, d), lambda i, k: (0, 0)),
                  pl.BlockSpec((tm, d), lambda i, k: (i, 0))] + [_ANY] * nc,
        out_specs=[pl.BlockSpec((tm, d), lambda i, k: (i, 0)), pl.BlockSpec((1, d), lambda i, k: (0, 0))] + [_ANY] * nc,
        out_shape=[_sds((n, d), F32), _sds((1, d), F32)] + comm_out,
        scratch_shapes=[pltpu.VMEM((tm, d), F32)] + comm_scr,
        compiler_params=_cp("arbitrary", "arbitrary"), name=name)(dpa, dgl, wt, x, g, dres, *comm_in)


def _mm_tn(a, b, l, nl, buf, *, ta, tb, tk, name):
    pa, n, ka = a.shape
    pb, _, kb = b.shape
    nap = ka // ta
    nbp = kb // tb

    def body(a_ref, b_ref, *rest):
        o_ref = rest[-1]

        @pl.when(pl.program_id(2) == 0)
        def _():
            o_ref[...] = jnp.zeros_like(o_ref)

        o_ref[...] += _dot_t0(a_ref[...], b_ref[...])

    extra, extra_specs, out_shape, aliases = _slab_out(nl, l, (pa * ka, pb * kb), buf, 2)
    return pl.pallas_call(
        body, grid=(pa * nap, pb * nbp, n // tk),
        in_specs=[pl.BlockSpec((None, tk, ta), lambda i, j, k: (i // nap, k, i % nap)),
                  pl.BlockSpec((None, tk, tb), lambda i, j, k: (j // nbp, k, j % nbp)), *extra_specs],
        out_specs=pl.BlockSpec((None, ta, tb), lambda i, j, k: (l, i, j)),
        out_shape=out_shape, input_output_aliases=aliases,
        compiler_params=_cp("parallel", "parallel", "arbitrary"), name=name)(a, b, *extra)


def _dwin_t(dpa, dgl, h, l, nl, buf, *, ta, tk, name):
    n, d = h.shape
    n1 = dpa.shape[1] // ta
    n2 = dgl.shape[1] // ta

    def body(a1_ref, a2_ref, h_ref, *rest):
        o_ref = rest[-1]
        i = pl.program_id(0)

        @pl.when(pl.program_id(1) == 0)
        def _():
            o_ref[...] = jnp.zeros_like(o_ref)

        @pl.when(i < n1)
        def _():
            o_ref[...] += _dot_t0(a1_ref[...], h_ref[...])

        @pl.when(i >= n1)
        def _():
            o_ref[...] += _dot_t0(a2_ref[...], h_ref[...])

    extra, extra_specs, out_shape, aliases = _slab_out(nl, l, ((n1 + n2) * ta, d), buf, 3)
    return pl.pallas_call(
        body, grid=(n1 + n2, n // tk),
        in_specs=[pl.BlockSpec((tk, ta), lambda i, k: (jnp.where(i < n1, k, 0), jnp.minimum(i, n1 - 1))),
                  pl.BlockSpec((tk, ta), lambda i, k: (jnp.where(i >= n1, k, 0), jnp.maximum(i - n1, 0))),
                  pl.BlockSpec((tk, d), lambda i, k: (k, 0)), *extra_specs],
        out_specs=pl.BlockSpec((None, ta, d), lambda i, k: (l, i, 0)),
        out_shape=out_shape, input_output_aliases=aliases,
        compiler_params=_cp("parallel", "arbitrary"), name=name)(dpa, dgl, h, *extra)


def _loss_grad(y, t, *, tm, name):
    n, d = y.shape

    def body(y_ref, t_ref, dy_ref, l_ref):
        e = y_ref[...] - t_ref[...]
        dy_ref[...] = e * (1.0 / d)
        part = jnp.sum(e * e, axis=0, keepdims=True)

        @pl.when(pl.program_id(0) == 0)
        def _():
            l_ref[...] = part

        @pl.when(pl.program_id(0) > 0)
        def _():
            l_ref[...] += part

    return pl.pallas_call(
        body, grid=(n // tm,),
        in_specs=[pl.BlockSpec((tm, d), lambda i: (i, 0)), pl.BlockSpec((tm, d), lambda i: (i, 0))],
        out_specs=[pl.BlockSpec((tm, d), lambda i: (i, 0)), pl.BlockSpec((1, d), lambda i: (0, 0))],
        out_shape=[_sds((n, d), F32), _sds((1, d), F32)],
        compiler_params=_cp("arbitrary"), name=name)(y, t)


def _s5_fwd(proj, sp, wglu, l, *, bl, s, t, name):
    n = bl * s
    ds5, gp = sp["bblk_r"].shape
    nt = s // t
    ng = t // SUBLANES
    glog = int(math.log2(ng))

    def body(u_ref, br_ref, bi_ref, pw_ref, p8_ref, cr_ref, ci_ref, d_ref, wg_ref,
             xr_ref, xi_ref, yp_ref, zg_ref, o_ref, carry, st):
        @pl.when(pl.program_id(1) == 0)
        def _():
            carry[...] = jnp.zeros_like(carry)

        u = u_ref[...]
        sub = lax.broadcasted_iota(jnp.int32, (t, gp), 0) % SUBLANES
        xr = _dot(u, br_ref[...])
        xi = _dot(u, bi_ref[...])
        for k in range(GROUP_LOG):
            sh = 1 << k
            pr = pw_ref[2 * k:2 * k + 1, :]
            pi = pw_ref[2 * k + 1:2 * k + 2, :]
            keep = sub >= sh
            sr = jnp.where(keep, pltpu.roll(xr, sh, 0), 0.0)
            si = jnp.where(keep, pltpu.roll(xi, sh, 0), 0.0)
            xr, xi = xr + pr * sr - pi * si, xi + pr * si + pi * sr
        xr_ref[...] = xr
        xi_ref[...] = xi
        grow = lax.broadcasted_iota(jnp.int32, (ng, gp), 0)
        cr = carry[0:1, :]
        ci = carry[1:2, :]
        a8r = pw_ref[2 * GROUP_LOG:2 * GROUP_LOG + 1, :]
        a8i = pw_ref[2 * GROUP_LOG + 1:2 * GROUP_LOG + 2, :]
        head = grow == 0
        for g in range(ng):
            st[g:g + 1, :] = xr_ref[(g + 1) * SUBLANES - 1:(g + 1) * SUBLANES, :]
            st[ng + g:ng + g + 1, :] = xi_ref[(g + 1) * SUBLANES - 1:(g + 1) * SUBLANES, :]
        sr_ = st[0:ng, :] + jnp.where(head, a8r * cr - a8i * ci, 0.0)
        si_ = st[ng:2 * ng, :] + jnp.where(head, a8r * ci + a8i * cr, 0.0)
        for k in range(glog):
            sh = 1 << k
            pr = pw_ref[2 * (GROUP_LOG + k):2 * (GROUP_LOG + k) + 1, :]
            pi = pw_ref[2 * (GROUP_LOG + k) + 1:2 * (GROUP_LOG + k) + 2, :]
            keep = grow >= sh
            tr = jnp.where(keep, pltpu.roll(sr_, sh, 0), 0.0)
            ti = jnp.where(keep, pltpu.roll(si_, sh, 0), 0.0)
            sr_, si_ = sr_ + pr * tr - pi * ti, si_ + pr * ti + pi * tr
        tail = grow == ng - 1
        carry[0:1, :] = jnp.sum(jnp.where(tail, sr_, 0.0), axis=0, keepdims=True)
        carry[1:2, :] = jnp.sum(jnp.where(tail, si_, 0.0), axis=0, keepdims=True)
        st[0:ng, :] = jnp.where(head, cr, pltpu.roll(sr_, 1, 0))
        st[ng:2 * ng, :] = jnp.where(head, ci, pltpu.roll(si_, 1, 0))
        p8r = p8_ref[0:SUBLANES, :]
        p8i = p8_ref[SUBLANES:2 * SUBLANES, :]
        for g in range(ng):
            grp = slice(g * SUBLANES, (g + 1) * SUBLANES)
            pr = st[g:g + 1, :]
            pi = st[ng + g:ng + g + 1, :]
            xr_ref[grp, :] = xr_ref[grp, :] + p8r * pr - p8i * pi
            xi_ref[grp, :] = xi_ref[grp, :] + p8r * pi + p8i * pr
        xr = xr_ref[...]
        xi = xi_ref[...]
        y = _dot(xr.astype(BF16), cr_ref[...]) + _dot(xi.astype(BF16), ci_ref[...]) + d_ref[...] * u.astype(F32)
        yp_ref[...] = y
        zg = _dot(_gelu(y).astype(BF16), wg_ref[...])
        zg_ref[...] = zg
        o_ref[...] = (zg[:, :ds5] * _sigmoid(zg[:, ds5:])).astype(BF16)

    const = lambda shape: pl.BlockSpec(shape, lambda b, i: (0, 0))
    row = lambda w: pl.BlockSpec((t, w), lambda b, i: (b * nt + i, 0))
    return pl.pallas_call(
        body, grid=(bl, nt),
        in_specs=[row(ds5), const((ds5, gp)), const((ds5, gp)), const((2 * (GROUP_LOG + glog), gp)),
                  const((2 * SUBLANES, gp)),
                  const((gp, ds5)), const((gp, ds5)), const((1, ds5)),
                  pl.BlockSpec((None, ds5, 2 * ds5), lambda b, i: (l, 0, 0))],
        out_specs=[row(gp), row(gp), row(ds5), row(2 * ds5), row(ds5)],
        out_shape=[_sds((n, gp), F32), _sds((n, gp), F32), _sds((n, ds5), F32), _sds((n, 2 * ds5), F32),
                   _sds((n, ds5), BF16)],
        scratch_shapes=[pltpu.VMEM((2, gp), F32), pltpu.VMEM((2 * ng, gp), F32)],
        compiler_params=_cp("arbitrary", "arbitrary"), name=name)(
            proj, sp["bblk_r"], sp["bblk_i"], sp["pw"], sp["p8"], sp["cblk_r"], sp["cblk_in"], sp["d"], wglu)


def _s5_bwd(ds, yp, zg, xr, xi, proj, sp, wglu, l, nl, dwg_buf, *, bl, s, t, name):
    n = bl * s
    ds5, gp = sp["bblk_r"].shape
    nt = s // t
    tb = t // 8
    ng = t // SUBLANES
    glog = int(math.log2(ng))

    def body(ds_ref, yp_ref, zg_ref, xr_ref, xi_ref, hr_ref, hi_ref, u_ref, wg_ref, cr_ref, ci_ref,
             br_ref, bi_ref, pw_ref, q8_ref, d_ref, *rest):
        du_ref, dd_ref, dcr_ref, dci_ref, dbr_ref, dbi_ref, da_ref, dwg_ref, carry, gr_scr, gi_scr, st = rest[-12:]
        b = pl.program_id(0)
        i = pl.program_id(1)
        tile = nt - 1 - i

        @pl.when((b == 0) & (i == 0))
        def _():
            for r in (dwg_ref, dd_ref, dcr_ref, dci_ref, dbr_ref, dbi_ref, da_ref):
                r[...] = jnp.zeros_like(r)

        @pl.when(i == 0)
        def _():
            carry[...] = jnp.zeros_like(carry)

        dsv = ds_ref[...].astype(F32)
        zgv = zg_ref[...]
        za = zgv[:, :ds5]
        sg = _sigmoid(zgv[:, ds5:])
        dzg = jnp.concatenate([dsv * sg, dsv * za * sg * (1.0 - sg)], axis=1).astype(BF16)
        y = yp_ref[...]
        dwg_ref[...] += _dot_t0(_gelu(y).astype(BF16), dzg)
        dy = _dot_t1(dzg, wg_ref[...]) * _gelu_grad(y)
        ub = u_ref[...]
        uf = ub.astype(F32)
        dd_ref[...] += jnp.sum(dy * uf, axis=0, keepdims=True)
        dyb = dy.astype(BF16)
        xrv = xr_ref[...]
        xiv = xi_ref[...]
        dcr_ref[...] += _dot_t0(xrv.astype(BF16), dyb)
        dci_ref[...] += _dot_t0(xiv.astype(BF16), dyb)

        rows = lax.broadcasted_iota(jnp.int32, (t, gp), 0)
        sub = rows % SUBLANES
        gr = _dot_t1(dyb, cr_ref[...])
        gi = _dot_t1(dyb, ci_ref[...])
        for k in range(GROUP_LOG):
            sh = 1 << k
            pr = pw_ref[2 * k:2 * k + 1, :]
            pi = pw_ref[2 * k + 1:2 * k + 2, :]
            keep = sub < SUBLANES - sh
            sr = jnp.where(keep, pltpu.roll(gr, t - sh, 0), 0.0)
            si = jnp.where(keep, pltpu.roll(gi, t - sh, 0), 0.0)
            gr, gi = gr + pr * sr + pi * si, gi + pr * si - pi * sr
        gr_scr[...] = gr
        gi_scr[...] = gi
        grow = lax.broadcasted_iota(jnp.int32, (ng, gp), 0)
        cr = carry[0:1, :]
        ci = carry[1:2, :]
        a8r = pw_ref[2 * GROUP_LOG:2 * GROUP_LOG + 1, :]
        a8i = pw_ref[2 * GROUP_LOG + 1:2 * GROUP_LOG + 2, :]
        tail = grow == ng - 1
        for g in range(ng):
            st[g:g + 1, :] = gr_scr[g * SUBLANES:g * SUBLANES + 1, :]
            st[ng + g:ng + g + 1, :] = gi_scr[g * SUBLANES:g * SUBLANES + 1, :]
        sr_ = st[0:ng, :] + jnp.where(tail, a8r * cr + a8i * ci, 0.0)
        si_ = st[ng:2 * ng, :] + jnp.where(tail, a8r * ci - a8i * cr, 0.0)
        for k in range(glog):
            sh = 1 << k
            pr = pw_ref[2 * (GROUP_LOG + k):2 * (GROUP_LOG + k) + 1, :]
            pi = pw_ref[2 * (GROUP_LOG + k) + 1:2 * (GROUP_LOG + k) + 2, :]
            keep = grow < ng - sh
            tr = jnp.where(keep, pltpu.roll(sr_, ng - sh, 0), 0.0)
            ti = jnp.where(keep, pltpu.roll(si_, ng - sh, 0), 0.0)
            sr_, si_ = sr_ + pr * tr + pi * ti, si_ + pr * ti - pi * tr
        head = grow == 0
        carry[0:1, :] = jnp.sum(jnp.where(head, sr_, 0.0), axis=0, keepdims=True)
        carry[1:2, :] = jnp.sum(jnp.where(head, si_, 0.0), axis=0, keepdims=True)
        st[0:ng, :] = jnp.where(tail, cr, pltpu.roll(sr_, ng - 1, 0))
        st[ng:2 * ng, :] = jnp.where(tail, ci, pltpu.roll(si_, ng - 1, 0))
        q8r = q8_ref[0:SUBLANES, :]
        q8i = q8_ref[SUBLANES:2 * SUBLANES, :]
        for g in range(ng):
            grp = slice(g * SUBLANES, (g + 1) * SUBLANES)
            pr = st[g:g + 1, :]
            pi = st[ng + g:ng + g + 1, :]
            gr_scr[grp, :] = gr_scr[grp, :] + q8r * pr - q8i * pi
            gi_scr[grp, :] = gi_scr[grp, :] + q8r * pi + q8i * pr
        gr = gr_scr[...]
        gi = gi_scr[...]
        first = rows == 0

        live = jnp.where(tile > 0, 1.0, 0.0)
        xpr = jnp.where(first, hr_ref[7:8, :] * live, pltpu.roll(xrv, 1, 0))
        xpi = jnp.where(first, hi_ref[7:8, :] * live, pltpu.roll(xiv, 1, 0))
        da_ref[0:1, :] += jnp.sum(gr * xpr + gi * xpi, axis=0, keepdims=True)
        da_ref[1:2, :] += jnp.sum(gi * xpr - gr * xpi, axis=0, keepdims=True)

        grb = gr.astype(BF16)
        gib = gi.astype(BF16)
        dbr_ref[...] += _dot_t0(ub, grb)
        dbi_ref[...] += _dot_t0(ub, gib)
        du_ref[...] = (_dot_t1(grb, br_ref[...]) + _dot_t1(gib, bi_ref[...]) + dy * d_ref[...]).astype(BF16)

    const = lambda shape: pl.BlockSpec(shape, lambda b, i: (0, 0))
    row = lambda w: pl.BlockSpec((t, w), lambda b, i: (b * nt + nt - 1 - i, 0))
    halo = pl.BlockSpec((8, gp), lambda b, i: (jnp.maximum((b * nt + nt - 1 - i) * tb - 1, 0), 0))
    extra, extra_specs, dwg_shape, aliases = _slab_out(nl, l, (ds5, 2 * ds5), dwg_buf, 16)
    aliases = {k: 7 for k in aliases}
    return pl.pallas_call(
        body, grid=(bl, nt),
        in_specs=[row(ds5), row(ds5), row(2 * ds5), row(gp), row(gp), halo, halo, row(ds5),
                  pl.BlockSpec((None, ds5, 2 * ds5), lambda b, i: (l, 0, 0)),
                  const((gp, ds5)), const((gp, ds5)), const((ds5, gp)), const((ds5, gp)),
                  const((2 * (GROUP_LOG + glog), gp)), const((2 * SUBLANES, gp)), const((1, ds5)), *extra_specs],
        out_specs=[row(ds5), const((1, ds5)), const((gp, ds5)), const((gp, ds5)),
                   const((ds5, gp)), const((ds5, gp)), const((2, gp)),
                   pl.BlockSpec((None, ds5, 2 * ds5), lambda b, i: (l, 0, 0))],
        out_shape=[_sds((n, ds5), BF16), _sds((1, ds5), F32), _sds((gp, ds5), F32),
                   _sds((gp, ds5), F32), _sds((ds5, gp), F32), _sds((ds5, gp), F32), _sds((2, gp), F32), dwg_shape],
        input_output_aliases=aliases,
        scratch_shapes=[pltpu.VMEM((2, gp), F32), pltpu.VMEM((t, gp), F32), pltpu.VMEM((t, gp), F32),
                        pltpu.VMEM((2 * ng, gp), F32)],
        compiler_params=_cp("arbitrary", "arbitrary"), name=name)(
            ds, yp, zg, xr, xi, xr, xi, proj, wglu, sp["cblk_r"], sp["cblk_in"],
            sp["bblk_r"], sp["bblk_i"], sp["pw"], sp["q8"], sp["d"], *extra)


def _head_norm(x, first):
    x2 = x * x
    sa = jnp.sum(jnp.where(first, x2, 0.0), axis=-1, keepdims=True)
    sb = jnp.sum(jnp.where(first, 0.0, x2), axis=-1, keepdims=True)
    r = jnp.where(first, lax.rsqrt(sa * (1.0 / HEAD_DIM) + EPS), lax.rsqrt(sb * (1.0 / HEAD_DIM) + EPS))
    return x * r, r


def _attn_specs(bl, s, datt, qoff):
    nq = s // ATT_TQ
    nb = datt // LANES
    col = lambda blk: (lambda b, h, q: (b * nq + q, qoff + blk * nb + h))
    win = lambda blk, j: (lambda b, h, q: (b * nq + jnp.maximum(q - 2 + j, 0), qoff + blk * nb + h))
    tile = lambda f: pl.BlockSpec((ATT_TQ, LANES), f)
    qs = tile(col(0))
    ks = [tile(win(1, j)) for j in range(3)]
    vs = [tile(win(2, j)) for j in range(3)]
    return nq, nb, qs, ks, vs


def _attn_probs(q_ref, k_refs, gq_ref, gk_ref, bias_ref):
    qt = pl.program_id(2)
    lane = lax.broadcasted_iota(jnp.int32, (1, LANES), 1)
    first = lane < HEAD_DIM
    qh, rq = _head_norm(q_ref[...].astype(F32), first)
    qn = qh * gq_ref[...]
    kc = jnp.concatenate([r[...] for r in k_refs], axis=0).astype(F32)
    kh, _ = _head_norm(kc, first)
    kn = (kh * gk_ref[...]).astype(BF16)
    kpos = (qt - 2) * ATT_TQ + lax.broadcasted_iota(jnp.int32, (1, 3 * ATT_TQ), 1)
    valid = kpos >= 0
    scale = HEAD_DIM ** -0.5
    masks = (first, jnp.logical_not(first))
    qas, ps = [], []
    for hh in range(2):
        qa = jnp.where(masks[hh], qn, 0.0).astype(BF16)
        sc = _dot_t1(qa, kn) * scale + bias_ref[hh]
        sc = jnp.where(valid, sc, NEG)
        e = jnp.exp(sc - jnp.max(sc, axis=-1, keepdims=True))
        ps.append(e / jnp.sum(e, axis=-1, keepdims=True))
        qas.append(qa)
    return first, masks, qh, rq, kn, qas, ps


def _attn_fwd(proj, gq2, gk2, bias, *, bl, s, datt, qoff, name, comm=None):
    n = bl * s
    nq, nb, qs, ks, vs = _attn_specs(bl, s, datt, qoff)
    nc = 0 if comm is None else len(comm[0])

    def body(q_ref, k0, k1, k2, v0, v1, v2, gq_ref, gk_ref, bias_ref, *rest):
        o_ref = rest[nc]
        if comm is not None:
            finish = _hosted_gather_steps(comm, rest[nc + 1:2 * nc + 1], rest[-2:],
                                          (pl.program_id(0) * nb + pl.program_id(1)) * nq + pl.program_id(2),
                                          bl * nb * nq)
        first, _, _, _, _, _, ps = _attn_probs(q_ref, (k0, k1, k2), gq_ref, gk_ref, bias_ref)
        vc = jnp.concatenate([v0[...], v1[...], v2[...]], axis=0)
        o0 = _dot(ps[0].astype(BF16), vc)
        o1 = _dot(ps[1].astype(BF16), vc)
        o_ref[...] = jnp.where(first, o0, o1).astype(BF16)
        if comm is not None:
            finish()

    gs = pl.BlockSpec((1, LANES), lambda b, h, q: (0, 0))
    c_in, c_ispec, c_ospec, c_oshape, c_scr, aliases = _hosted_gather(comm, 10, 1)
    res = pl.pallas_call(
        body, grid=(bl, nb, nq),
        in_specs=[qs, *ks, *vs, gs, gs, pl.BlockSpec((2, ATT_TQ, 3 * ATT_TQ), lambda b, h, q: (h, 0, 0))] + c_ispec,
        out_specs=[pl.BlockSpec((ATT_TQ, LANES), lambda b, h, q: (b * nq + q, h))] + c_ospec,
        out_shape=[_sds((n, datt), BF16)] + c_oshape,
        input_output_aliases=aliases, scratch_shapes=c_scr,
        compiler_params=_cp("arbitrary", "arbitrary", "arbitrary"), name=name)(
            proj, proj, proj, proj, proj, proj, proj, gq2, gk2, bias, *c_in)
    return res[0], res[1:]


def _attn_bwd(do, proj, gq2, gk2, bias, *, bl, s, datt, qoff, name, comm=None):
    n = bl * s
    nq, nb, qs, ks, vs = _attn_specs(bl, s, datt, qoff)
    srows = s + 2 * ATT_TQ
    scale = HEAD_DIM ** -0.5
    nc = 0 if comm is None else len(comm[0])

    def body(do_ref, q_ref, k0, k1, k2, v0, v1, v2, gq_ref, gk_ref, bias_ref, *rest):
        dq_ref, dk_ref, dv_ref, db_ref, dgq_ref = rest[nc:nc + 5]
        qt = pl.program_id(2)
        if comm is not None:
            start, wait = _exchange_ops(rest[:nc], rest[nc + 5:2 * nc + 5], rest[-2], rest[-1], comm[1], comm[2])
            step = (pl.program_id(0) * nb + pl.program_id(1)) * nq + qt

            @pl.when(step == 0)
            def _():
                start()

        @pl.when(qt == 0)
        def _():
            dk_ref[...] = jnp.zeros_like(dk_ref)
            dv_ref[...] = jnp.zeros_like(dv_ref)
            db_ref[...] = jnp.zeros_like(db_ref)
            dgq_ref[...] = jnp.zeros_like(dgq_ref)

        first, masks, qh, rq, kn, qas, ps = _attn_probs(q_ref, (k0, k1, k2), gq_ref, gk_ref, bias_ref)
        vc = jnp.concatenate([v0[...], v1[...], v2[...]], axis=0)
        dov = do_ref[...]
        dqn = jnp.zeros((ATT_TQ, LANES), F32)
        dkn = jnp.zeros((3 * ATT_TQ, LANES), F32)
        dv = jnp.zeros((3 * ATT_TQ, LANES), F32)
        for hh in range(2):
            doa = jnp.where(masks[hh], dov, jnp.zeros_like(dov))
            p = ps[hh]
            dp = _dot_t1(doa, vc)
            dsm = p * (dp - jnp.sum(dp * p, axis=-1, keepdims=True))
            db_ref[hh] += dsm
            dsc = (dsm * scale).astype(BF16)
            dqn = dqn + _dot(dsc, jnp.where(masks[hh], kn, jnp.zeros_like(kn)))
            dkn = dkn + _dot_t0(dsc, qas[hh])
            dv = dv + _dot_t0(p.astype(BF16), doa)
        start = pl.multiple_of(qt * ATT_TQ, ATT_TQ)
        dk_ref[pl.ds(start, 3 * ATT_TQ), :] += dkn
        dv_ref[pl.ds(start, 3 * ATT_TQ), :] += dv
        dgq_ref[...] += jnp.sum(dqn * qh, axis=0, keepdims=True)
        dqh = dqn * gq_ref[...]
        t = dqh * qh
        ma = jnp.sum(jnp.where(first, t, 0.0), axis=-1, keepdims=True) * (1.0 / HEAD_DIM)
        mb = jnp.sum(jnp.where(first, 0.0, t), axis=-1, keepdims=True) * (1.0 / HEAD_DIM)
        dq_ref[...] = (rq * (dqh - qh * jnp.where(first, ma, mb))).astype(BF16)
        if comm is not None:
            @pl.when(step == bl * nb * nq - 1)
            def _():
                wait()

    gs = pl.BlockSpec((1, LANES), lambda b, h, q: (0, 0))
    acc = pl.BlockSpec((None, srows, LANES), lambda b, h, q: (b, 0, h))
    comm_in = [] if comm is None else list(comm[0])
    comm_out = [] if comm is None else _parts_shapes(comm[0], comm[1])
    comm_scr = [] if comm is None else [pltpu.SemaphoreType.DMA((3 * nc,)), pltpu.SemaphoreType.DMA((3 * nc,))]
    return pl.pallas_call(
        body, grid=(bl, nb, nq),
        in_specs=[pl.BlockSpec((ATT_TQ, LANES), lambda b, h, q: (b * nq + q, h)), qs, *ks, *vs, gs, gs,
                  pl.BlockSpec((2, ATT_TQ, 3 * ATT_TQ), lambda b, h, q: (h, 0, 0))] + [_ANY] * nc,
        out_specs=[pl.BlockSpec((ATT_TQ, LANES), lambda b, h, q: (b * nq + q, h)), acc, acc,
                   pl.BlockSpec((None, 2, ATT_TQ, 3 * ATT_TQ), lambda b, h, q: (b, h, 0, 0)),
                   pl.BlockSpec((None, None, 1, LANES), lambda b, h, q: (b, h, 0, 0))] + [_ANY] * nc,
        out_shape=[_sds((n, datt), BF16), _sds((bl, srows, datt), F32), _sds((bl, srows, datt), F32),
                   _sds((bl, 2 * nb, ATT_TQ, 3 * ATT_TQ), F32), _sds((bl, nb, 1, LANES), F32)] + comm_out,
        scratch_shapes=comm_scr,
        compiler_params=_cp("arbitrary", "arbitrary", "arbitrary"), name=name)(
            do, proj, proj, proj, proj, proj, proj, proj, gq2, gk2, bias, *comm_in)


def _attn_kv_bwd(dkn, dv, proj, gk2, *, bl, s, datt, tm, koff, name):
    n = bl * s
    ns = s // tm
    off = 2 * ATT_TQ // tm
    nb = datt // LANES

    def body(dkn_ref, dv_ref, k_ref, gk_ref, dk_ref, dvo_ref, dgk_ref):
        lane = lax.broadcasted_iota(jnp.int32, (1, LANES), 1)
        first = lane < HEAD_DIM

        @pl.when((pl.program_id(0) == 0) & (pl.program_id(1) == 0) & (pl.program_id(2) == 0))
        def _():
            dgk_ref[...] = jnp.zeros_like(dgk_ref)

        dvo_ref[...] = dv_ref[...].astype(BF16)
        kh, rk = _head_norm(k_ref[...].astype(F32), first)
        dn = dkn_ref[...]
        dgk_ref[...] += jnp.sum(dn * kh, axis=0, keepdims=True)
        dh = dn * gk_ref[...]
        t = dh * kh
        ma = jnp.sum(jnp.where(first, t, 0.0), axis=-1, keepdims=True) * (1.0 / HEAD_DIM)
        mb = jnp.sum(jnp.where(first, 0.0, t), axis=-1, keepdims=True) * (1.0 / HEAD_DIM)
        dk_ref[...] = (rk * (dh - kh * jnp.where(first, ma, mb))).astype(BF16)

    accs = pl.BlockSpec((None, tm, LANES), lambda b, i, c: (b, i + off, c))
    outs = pl.BlockSpec((tm, LANES), lambda b, i, c: (b * ns + i, c))
    vec = pl.BlockSpec((1, LANES), lambda b, i, c: (0, 0))
    return pl.pallas_call(
        body, grid=(bl, ns, nb),
        in_specs=[accs, accs, pl.BlockSpec((tm, LANES), lambda b, i, c: (b * ns + i, koff + c)), vec],
        out_specs=[outs, outs, vec],
        out_shape=[_sds((n, datt), BF16), _sds((n, datt), BF16), _sds((1, LANES), F32)],
        compiler_params=_cp("arbitrary", "arbitrary", "arbitrary"), name=name)(dkn, dv, proj, gk2)


def _conv_fwd(proj, wdw, bdw, lng, lnb, *, bl, s, t, acol, name):
    n = bl * s
    dc = wdw.shape[1]
    nt = s // t
    hb = t // HALO

    def body(za_ref, zg_ref, ha_ref, hgt_ref, w_ref, b_ref, g_ref, be_ref, hg_ref, hc_ref, o_ref, ext):
        i = pl.program_id(1)
        hg = za_ref[...].astype(F32) * _sigmoid(zg_ref[...].astype(F32))
        live = jnp.where(i > 0, 1.0, 0.0)
        ext[0:HALO, :] = ha_ref[...].astype(F32) * _sigmoid(hgt_ref[...].astype(F32)) * live
        ext[HALO:HALO + t, :] = hg
        hg_ref[...] = hg
        acc = jnp.zeros((t, dc), F32) + b_ref[...]
        for j in range(CONV_W):
            acc = acc + w_ref[j:j + 1, :] * ext[pl.ds(HALO - (CONV_W - 1) + j, t), :]
        hc_ref[...] = acc
        mu = jnp.mean(acc, axis=-1, keepdims=True)
        xc = acc - mu
        rs = lax.rsqrt(jnp.mean(xc * xc, axis=-1, keepdims=True) + EPS)
        ln = xc * rs * g_ref[...] + be_ref[...]
        o_ref[...] = (ln * _sigmoid(ln)).astype(BF16)

    vec = pl.BlockSpec((1, dc), lambda b, i: (0, 0))
    row = pl.BlockSpec((t, dc), lambda b, i: (b * nt + i, 0))
    tile = lambda c: pl.BlockSpec((t, dc), lambda b, i: (b * nt + i, c))
    halo = lambda c: pl.BlockSpec((HALO, dc), lambda b, i: (jnp.maximum((b * nt + i) * hb - 1, 0), c))
    return pl.pallas_call(
        body, grid=(bl, nt),
        in_specs=[tile(acol), tile(acol + 1), halo(acol), halo(acol + 1),
                  pl.BlockSpec((HALO, dc), lambda b, i: (0, 0)), vec, vec, vec],
        out_specs=[row, row, row],
        out_shape=[_sds((n, dc), F32), _sds((n, dc), F32), _sds((n, dc), BF16)],
        scratch_shapes=[pltpu.VMEM((HALO + t, dc), F32)],
        compiler_params=_cp("parallel", "arbitrary"), name=name)(proj, proj, proj, proj, wdw, bdw, lng, lnb)


def _conv_bwd_ln(dco, hc, lng, lnb, *, tm, name):
    n, dc = hc.shape

    def body(d_ref, hc_ref, g_ref, be_ref, dhc_ref, dg_ref, db_ref):
        @pl.when(pl.program_id(0) == 0)
        def _():
            dg_ref[...] = jnp.zeros_like(dg_ref)
            db_ref[...] = jnp.zeros_like(db_ref)

        hcv = hc_ref[...]
        mu = jnp.mean(hcv, axis=-1, keepdims=True)
        xc = hcv - mu
        rs = lax.rsqrt(jnp.mean(xc * xc, axis=-1, keepdims=True) + EPS)
        xh = xc * rs
        ln = xh * g_ref[...] + be_ref[...]
        sg = _sigmoid(ln)
        dln = d_ref[...].astype(F32) * (sg * (1.0 + ln * (1.0 - sg)))
        db_ref[...] += jnp.sum(dln, axis=0, keepdims=True)
        dg_ref[...] += jnp.sum(dln * xh, axis=0, keepdims=True)
        dxh = dln * g_ref[...]
        dhc_ref[...] = rs * (dxh - jnp.mean(dxh, axis=-1, keepdims=True)
                             - xh * jnp.mean(dxh * xh, axis=-1, keepdims=True))

    vec = pl.BlockSpec((1, dc), lambda i: (0, 0))
    row = pl.BlockSpec((tm, dc), lambda i: (i, 0))
    return pl.pallas_call(
        body, grid=(n // tm,), in_specs=[row, row, vec, vec], out_specs=[row, vec, vec],
        out_shape=[_sds((n, dc), F32), _sds((1, dc), F32), _sds((1, dc), F32)],
        compiler_params=_cp("arbitrary"), name=name)(dco, hc, lng, lnb)


def _conv_bwd_dw(dhc, hg, proj, wdw, *, bl, s, t, acol, name):
    n = bl * s
    dc = wdw.shape[1]
    nt = s // t
    hb = t // HALO
    lastblk = n // HALO - 1

    def body(d_ref, dn_ref, hg_ref, hp_ref, za_ref, zg_ref, w_ref, dz_ref, dw_ref, dbias_ref, extd, exth):
        b = pl.program_id(0)
        i = pl.program_id(1)

        @pl.when((b == 0) & (i == 0))
        def _():
            dw_ref[...] = jnp.zeros_like(dw_ref)
            dbias_ref[...] = jnp.zeros_like(dbias_ref)

        dv = d_ref[...]
        extd[0:t, :] = dv
        extd[t:t + HALO, :] = dn_ref[...] * jnp.where(i < nt - 1, 1.0, 0.0)
        exth[0:HALO, :] = hp_ref[...] * jnp.where(i > 0, 1.0, 0.0)
        exth[HALO:HALO + t, :] = hg_ref[...]
        dbias_ref[...] += jnp.sum(dv, axis=0, keepdims=True)
        dhg = jnp.zeros((t, dc), F32)
        for j in range(CONV_W):
            dhg = dhg + w_ref[j:j + 1, :] * extd[pl.ds(CONV_W - 1 - j, t), :]
            dw_ref[j:j + 1, :] += jnp.sum(dv * exth[pl.ds(HALO - (CONV_W - 1) + j, t), :], axis=0, keepdims=True)
        za = za_ref[...].astype(F32)
        sg = _sigmoid(zg_ref[...].astype(F32))
        dz_ref[...] = jnp.concatenate([dhg * sg, dhg * za * sg * (1.0 - sg)], axis=1).astype(BF16)

    row = pl.BlockSpec((t, dc), lambda b, i: (b * nt + i, 0))
    nxt = pl.BlockSpec((HALO, dc), lambda b, i: (jnp.minimum((b * nt + i + 1) * hb, lastblk), 0))
    prv = pl.BlockSpec((HALO, dc), lambda b, i: (jnp.maximum((b * nt + i) * hb - 1, 0), 0))
    wsp = pl.BlockSpec((HALO, dc), lambda b, i: (0, 0))
    tile = lambda c: pl.BlockSpec((t, dc), lambda b, i: (b * nt + i, c))
    return pl.pallas_call(
        body, grid=(bl, nt),
        in_specs=[row, nxt, row, prv, tile(acol), tile(acol + 1), wsp],
        out_specs=[pl.BlockSpec((t, 2 * dc), lambda b, i: (b * nt + i, 0)), wsp,
                   pl.BlockSpec((1, dc), lambda b, i: (0, 0))],
        out_shape=[_sds((n, 2 * dc), BF16), _sds((HALO, dc), F32), _sds((1, dc), F32)],
        scratch_shapes=[pltpu.VMEM((t + HALO, dc), F32), pltpu.VMEM((HALO + t, dc), F32)],
        compiler_params=_cp("arbitrary", "arbitrary"), name=name)(dhc, dhc, hg, hg, proj, proj, wdw)


def _mix_out_fwd(x, brs, gl, bg, wbs, wout, l, *, tm, name):
    n, d = x.shape

    def body(x_ref, s_ref, a_ref, c_ref, g0, g1, g2, bg_ref, ws, wa, wc, wo, o_ref):
        merged = jnp.zeros((tm, d), F32)
        for k, (br, gr, w) in enumerate(((s_ref, g0, ws), (a_ref, g1, wa), (c_ref, g2, wc))):
            gate = _sigmoid(gr[...].astype(F32) + bg_ref[:, k * d:(k + 1) * d])
            merged = merged + gate * _dot(br[...], w[...])
        o_ref[...] = x_ref[...] + _dot(merged.astype(BF16), wo[...])

    row = lambda w: pl.BlockSpec((tm, w), lambda i: (i, 0))
    wsp = lambda a: pl.BlockSpec((None,) + a.shape[1:], lambda i: (l, 0, 0))
    gls = [pl.BlockSpec((tm, d), functools.partial(lambda k, i: (i, k), k)) for k in range(3)]
    return pl.pallas_call(
        body, grid=(n // tm,),
        in_specs=[row(d), *[row(b.shape[1]) for b in brs], *gls, pl.BlockSpec(bg.shape, lambda i: (0, 0)),
                  *[wsp(w) for w in wbs], wsp(wout)],
        out_specs=row(d), out_shape=_sds((n, d), F32),
        compiler_params=_cp("parallel"), name=name)(x, *brs, gl, gl, gl, bg, *wbs, wout)


def _mix_out_bwd(dx, brs, gl, bg, wbs, wout, l, nl, bufs, *, tm, name):
    n, d = dx.shape
    widths = [b.shape[1] for b in brs]

    def body(dx_ref, s_ref, a_ref, c_ref, g0, g1, g2, bg_ref, ws, wa, wc, wo, *rest):
        ds_ref, da_ref, dc_ref, dgl_ref, dbg_ref, dws, dwa, dwc, dwo = rest[-9:]

        @pl.when(pl.program_id(0) == 0)
        def _():
            for r in (dbg_ref, dws, dwa, dwc, dwo):
                r[...] = jnp.zeros_like(r)

        dxb = dx_ref[...].astype(BF16)
        dm = _dot_t1(dxb, wo[...])
        merged = jnp.zeros((tm, d), F32)
        for k, (br, gr, w, dbr, dw) in enumerate(((s_ref, g0, ws, ds_ref, dws), (a_ref, g1, wa, da_ref, dwa),
                                                   (c_ref, g2, wc, dc_ref, dwc))):
            gate = _sigmoid(gr[...].astype(F32) + bg_ref[:, k * d:(k + 1) * d])
            brv = br[...]
            wv = w[...]
            y = _dot(brv, wv)
            merged = merged + gate * y
            dyb = (dm * gate).astype(BF16)
            dbr[...] = _dot_t1(dyb, wv).astype(BF16)
            dw[...] += _dot_t0(brv, dyb)
            dgl = dm * y * gate * (1.0 - gate)
            dgl_ref[:, k * d:(k + 1) * d] = dgl.astype(BF16)
            dbg_ref[:, k * d:(k + 1) * d] += jnp.sum(dgl, axis=0, keepdims=True)
        dwo[...] += _dot_t0(merged.astype(BF16), dxb)

    row = lambda w: pl.BlockSpec((tm, w), lambda i: (i, 0))
    wsp = lambda shape: pl.BlockSpec((None,) + tuple(shape), lambda i: (l, 0, 0))
    gls = [pl.BlockSpec((tm, d), functools.partial(lambda k, i: (i, k), k)) for k in range(3)]
    slabs = [(w, d) for w in widths] + [(d, d)]
    n_in = 12
    extra = [] if bufs is None else list(bufs)
    aliases = {} if bufs is None else {n_in + k: 5 + k for k in range(4)}
    return pl.pallas_call(
        body, grid=(n // tm,),
        in_specs=[row(d), *[row(w) for w in widths], *gls, pl.BlockSpec(bg.shape, lambda i: (0, 0)),
                  *[wsp(w.shape[1:]) for w in wbs], wsp(wout.shape[1:]), *[_ANY for _ in extra]],
        out_specs=[*[row(w) for w in widths], row(3 * d), pl.BlockSpec((1, 3 * d), lambda i: (0, 0)),
                   *[wsp(sh) for sh in slabs]],
        out_shape=[*[_sds((n, w), BF16) for w in widths], _sds((n, 3 * d), BF16), _sds((1, 3 * d), F32),
                   *[_sds((nl,) + sh, F32) for sh in slabs]],
        input_output_aliases=aliases,
        compiler_params=_cp("arbitrary"), name=name)(dx, *brs, gl, gl, gl, bg, *wbs, wout, *extra)


def _adamw(w, g, m, v, *, name):
    r, c = w.shape
    tm = _tile(r, 256)
    c1 = 1.0 - ADAM_B1 ** ADAM_STEP
    c2 = 1.0 - ADAM_B2 ** ADAM_STEP

    def body(w_ref, g_ref, m_ref, v_ref, d_ref, nm_ref, nv_ref):
        gv = g_ref[...]
        mn = ADAM_B1 * m_ref[...] + (1.0 - ADAM_B1) * gv
        vn = ADAM_B2 * v_ref[...] + (1.0 - ADAM_B2) * (gv * gv)
        nm_ref[...] = mn
        nv_ref[...] = vn
        d_ref[...] = -ADAM_LR * ((mn / c1) / (jnp.sqrt(vn / c2) + ADAM_EPS) + ADAM_WD * w_ref[...])

    blk = pl.BlockSpec((tm, c), lambda i: (i, 0))
    return pl.pallas_call(
        body, grid=(r // tm,), in_specs=[blk] * 4, out_specs=[blk] * 3,
        out_shape=[_sds((r, c), F32)] * 3, compiler_params=_cp("parallel"), name=name)(w, g, m, v)


def _add_sibling(g, recv, lyr, c_idx, *, name):
    _, a, b = g.shape
    ta = _tile(a, 256)

    def body(c_ref, g_ref, r_ref, o_ref):
        @pl.when(c_ref[0] == lyr)
        def _():
            o_ref[...] = (g_ref[...] + r_ref[...]).astype(BF16)

    row = lambda i, cr: jnp.where(cr[0] == lyr, i, 0)
    return pl.pallas_call(
        body,
        grid_spec=pltpu.PrefetchScalarGridSpec(
            num_scalar_prefetch=1, grid=(a // ta,),
            in_specs=[pl.BlockSpec((None, ta, b), lambda i, cr: (lyr, row(i, cr), 0)),
                      pl.BlockSpec((ta, b), lambda i, cr: (row(i, cr), 0))],
            out_specs=pl.BlockSpec((ta, b), lambda i, cr: (row(i, cr), 0))),
        out_shape=_sds((a, b), BF16), compiler_params=_cp("arbitrary"), name=name)(c_idx, g, recv)


def _add_chips(rsum, parts, axis, s_idx, c_idx, lyr, buf, *, name):
    _, a, b = parts.shape
    ta = _tile(a, 256)
    na = a // ta

    def body(s_ref, c_ref, own_ref, p0, p1, p2, p3, *rest):
        o_ref = rest[-1]

        @pl.when(c_ref[0] == lyr)
        def _():
            own = own_ref[...].astype(F32)
            terms = [jnp.where(s_ref[0] == s, own, p[...].astype(F32)) for s, p in enumerate((p0, p1, p2, p3))]
            o_ref[...] = ((terms[0] + terms[1]) + terms[2]) + terms[3]

    row = lambda i, cr: jnp.where(cr[0] == lyr, i, 0)
    own_spec = (pl.BlockSpec((ta, b), lambda i, sr, cr: (sr[0] * na + row(i, cr), 0)) if axis == 1
                else pl.BlockSpec((ta, b), lambda i, sr, cr: (row(i, cr), sr[0])))
    part_spec = lambda s: pl.BlockSpec((None, ta, b),
                                       lambda i, sr, cr: (jnp.where(sr[0] == s, s ^ 1, s), row(i, cr), 0))
    extra, extra_specs, out_shape, aliases = _slab_out(2, lyr, (a, b), buf, 7)
    return pl.pallas_call(
        body,
        grid_spec=pltpu.PrefetchScalarGridSpec(
            num_scalar_prefetch=2, grid=(na,),
            in_specs=[own_spec] + [part_spec(s) for s in range(N_CHIPS)] + extra_specs,
            out_specs=pl.BlockSpec((None, ta, b), lambda i, sr, cr: (lyr, row(i, cr), 0))),
        out_shape=out_shape, input_output_aliases=aliases, compiler_params=_cp("arbitrary"), name=name)(
            s_idx, c_idx, rsum, parts, parts, parts, parts, *extra)


def _place_shard(wloc, axis, s_idx, *, name):
    nl, a, b = wloc.shape
    ta = _tile(a, 256)
    na = a // ta
    full = (nl, a * N_CHIPS, b) if axis == 1 else (nl, a, b * N_CHIPS)

    def body(sc_ref, w_ref, o_ref):
        o_ref[...] = w_ref[...].astype(BF16)

    out_spec = (pl.BlockSpec((None, ta, b), lambda l, i, sc: (l, sc[0] * na + i, 0)) if axis == 1
                else pl.BlockSpec((None, ta, b), lambda l, i, sc: (l, i, sc[0])))
    return pl.pallas_call(
        body,
        grid_spec=pltpu.PrefetchScalarGridSpec(
            num_scalar_prefetch=1, grid=(nl, na),
            in_specs=[pl.BlockSpec((None, ta, b), lambda l, i, sc: (l, i, 0))], out_specs=out_spec),
        out_shape=_sds(full, BF16), compiler_params=_cp("parallel", "parallel"), name=name)(s_idx, wloc)


def _blockdiag(w):
    g, r, c = w.shape
    eye = jnp.eye(g, dtype=w.dtype)
    return (w[:, :, None, :] * eye[:, None, :, None]).reshape(g * r, g * c)


def _s5_prep(lre, lim, log_dt, b_re, b_im, c_re, c_im, d_skip):
    lr = jnp.minimum(lre, -1e-4)
    li = lim
    dt = jnp.exp(log_dt)[:, None]
    mag = jnp.exp(lr * dt)
    ar = mag * jnp.cos(li * dt)
    ai = mag * jnp.sin(li * dt)
    den = lr * lr + li * li
    coef_r = ((ar - 1.0) * lr + ai * li) / den
    coef_i = (ai * lr - (ar - 1.0) * li) / den
    bbar_r = coef_r[..., None] * b_re - coef_i[..., None] * b_im
    bbar_i = coef_r[..., None] * b_im + coef_i[..., None] * b_re
    a = jnp.stack([ar.reshape(-1), ai.reshape(-1)])
    return dict(
        a=a,
        bblk_r=_blockdiag(bbar_r.transpose(0, 2, 1)), bblk_i=_blockdiag(bbar_i.transpose(0, 2, 1)),
        cblk_r=_blockdiag(c_re.transpose(0, 2, 1)), cblk_in=_blockdiag(-c_im.transpose(0, 2, 1)),
        d=d_skip.reshape(1, -1))


def _s5_powers(a, nlog):
    ar, ai = a[0], a[1]
    pr, pi = ar, ai
    rows = []
    for _ in range(nlog):
        rows += [pr, pi]
        pr, pi = pr * pr - pi * pi, 2.0 * pr * pi
    qr, qi = [ar], [ai]
    for _ in range(SUBLANES - 1):
        qr, qi = qr + [qr[-1] * ar - qi[-1] * ai], qi + [qr[-1] * ai + qi[-1] * ar]
    p8 = jnp.stack(qr + qi)
    q8 = jnp.stack(qr[::-1] + [-v for v in qi[::-1]])
    return jnp.stack(rows), p8, q8


def _bias_table(rel_bias):
    h = rel_bias.shape[0]
    tq, tw = ATT_TQ, 3 * ATT_TQ
    n_hi = tw - 1 - MAX_REL + 1
    n_lo = tq + tw - 1 - n_hi - (2 * MAX_REL - 1)
    fr = jnp.concatenate([
        jnp.broadcast_to(rel_bias[:, 2 * MAX_REL:], (h, n_hi)),
        jnp.flip(rel_bias[:, 1:2 * MAX_REL], axis=1),
        jnp.broadcast_to(rel_bias[:, :1], (h, n_lo)),
        jnp.zeros((h, 1), rel_bias.dtype)], axis=1)
    ln = tq + tw
    flat = jnp.broadcast_to(fr[:, None, :], (h, tq, ln)).reshape(h, tq * ln)[:, :tq * (ln - 1)]
    tab = flat.reshape(h, tq, ln - 1)[:, :, tq - 1:tq - 1 + tw]
    qc = np.arange(tq)[:, None] // CHUNK + N_LEFT
    kc = np.arange(tw)[None, :] // CHUNK
    band = (kc <= qc) & (kc >= qc - N_LEFT)
    return jnp.where(jnp.asarray(band)[None], tab, NEG)


def _small_prep(w, l):
    g, p = w["s5_lambda_re"].shape[1:]
    b_shape, c_shape = (g, p, -1), (g, -1, p)
    sp = _s5_prep(w["s5_lambda_re"][l], w["s5_lambda_im"][l], w["s5_log_dt"][l], w["s5_b_re"][l].reshape(b_shape),
                  w["s5_b_im"][l].reshape(b_shape), w["s5_c_re"][l].reshape(c_shape), w["s5_c_im"][l].reshape(c_shape),
                  w["s5_d"][l])
    return sp, _bias_table(w["attn_rel_bias"][l])


_PREP_KEYS = ("s5_lambda_re", "s5_lambda_im", "s5_log_dt", "s5_b_re", "s5_b_im", "s5_c_re", "s5_c_im", "s5_d",
              "attn_rel_bias")
_BIG_KEYS = {"ffn1_w_up": 2, "ffn1_w_down": 1, "w_in": 2, "s5_w_glu": 2, "w_br_s5": 2, "w_br_attn": 2,
             "w_br_conv": 2, "w_out": 1, "ffn2_w_up": 2, "ffn2_w_down": 1}
_SMALL_KEYS = ("ffn1_norm", "mix_norm", "b_gate", "s5_lambda_re", "s5_lambda_im", "s5_log_dt", "s5_b_re", "s5_b_im",
               "s5_c_re", "s5_c_im", "s5_d", "attn_q_gain", "attn_k_gain", "attn_rel_bias", "conv_w_dw", "conv_b_dw",
               "conv_ln_g", "conv_ln_b", "ffn2_norm")
_WEIGHTS = ("ffn1_norm", "ffn1_w_up", "ffn1_w_down", "mix_norm", "w_in", "b_gate", "s5_lambda_re", "s5_lambda_im",
            "s5_log_dt", "s5_b_re", "s5_b_im", "s5_c_re", "s5_c_im", "s5_d", "s5_w_glu", "w_br_s5", "attn_q_gain",
            "attn_k_gain", "attn_rel_bias", "w_br_attn", "conv_w_dw", "conv_b_dw", "conv_ln_g", "conv_ln_b",
            "w_br_conv", "w_out", "ffn2_norm", "ffn2_w_up", "ffn2_w_down")


def _local_step(x3, target3, w, rs=None, gather=None):
    w = dict(w)
    bl, s, d = x3.shape
    nl = w["ffn1_norm"].shape[0]
    dff = w["ffn1_w_down"].shape[1]
    ds5 = w["s5_d"].shape[1]
    datt = w["w_br_attn"].shape[1]
    dc = w["conv_b_dw"].shape[1]
    n = bl * s
    x = x3.reshape(n, d)
    target = target3.reshape(n, d)
    tm = _tile(n, 512)
    tml = _tile(n, 1024)
    tmix = _tile(n, 256)
    ts5 = 256
    tconv = _tile(s, 512)
    tff = dff // 2
    ma = ds5 + 3 * datt + 2 * dc
    tna = ma // 3
    assert (3 * d) % tna == 0 and dff % 2 == 0
    qoff = ds5 // LANES
    koff = (ds5 + datt) // LANES
    acol = (ds5 + 3 * datt) // dc
    wbs = lambda: (w["w_br_s5"], w["w_br_attn"], w["w_br_conv"])

    def host(tag, l):
        if gather is None or l != 0:
            return None
        keys, kaxes, lyr = gather[tag]
        return [w[k] for k in keys], kaxes, lyr

    def hosted(tag, l, arrays):
        if gather is not None and l == 0:
            w.update(zip(gather[tag][0], arrays))

    saved = []
    for l in range(nl):
        (sp, bias), prep_vjp = jax.vjp(lambda ww: _small_prep(ww, l), {k: w[k] for k in _PREP_KEYS})
        spb = dict(sp)
        spb["pw"], spb["p8"], spb["q8"] = _s5_powers(lax.stop_gradient(sp["a"]), int(math.log2(ts5)))
        for k in ("bblk_r", "bblk_i", "cblk_r", "cblk_in"):
            spb[k] = sp[k].astype(BF16)
        g1 = w["ffn1_norm"][l][None]
        g2 = w["ffn2_norm"][l][None]
        gm = w["mix_norm"][l][None]
        gq2 = jnp.tile(w["attn_q_gain"][l], 2)[None]
        gk2 = jnp.tile(w["attn_k_gain"][l], 2)[None]
        wdw = jnp.pad(w["conv_w_dw"][l], ((0, HALO - CONV_W), (0, 0)))
        bdw, lng, lnb = w["conv_b_dw"][l][None], w["conv_ln_g"][l][None], w["conv_ln_b"][l][None]
        bg = w["b_gate"][l][None]

        x0 = x
        h1, ab1, *got = _norm_mm(x0, g1, w["ffn1_w_up"], l, tm=tml, tn=tff, ntiles=4, pieces=2, transposed=False,
                                 name=f"ffn1_up_{l}", comm=host("ffn1_up", l))
        hosted("ffn1_up", l, got)
        x1 = _ffn_down(ab1, w["ffn1_w_down"], l, x0, tm=tm, tk=tff, name=f"ffn1_down_{l}")
        h2, pa, *got = _norm_mm(x1, gm, w["w_in"], l, tm=tml, tn=tna, ntiles=3, pieces=1, transposed=True,
                                name=f"win_a_{l}", comm=host("win_a", l))
        hosted("win_a", l, got)
        pa = pa[0]
        gl = _mm_t(h2, w["w_in"], l, tm=tml, tn=tna, off=3, ntiles=3 * d // tna, name=f"win_g_{l}")
        xr, xi, yp, zg, s5o = _s5_fwd(pa, spb, w["s5_w_glu"], l, bl=bl, s=s, t=ts5, name=f"s5_fwd_{l}")
        atto, got = _attn_fwd(pa, gq2, gk2, bias, bl=bl, s=s, datt=datt, qoff=qoff, name=f"attn_fwd_{l}",
                              comm=host("attn_fwd", l))
        hosted("attn_fwd", l, got)
        hg, hc, convo = _conv_fwd(pa, wdw, bdw, lng, lnb, bl=bl, s=s, t=tconv, acol=acol, name=f"conv_fwd_{l}")
        brs = (s5o, atto, convo)
        x2 = _mix_out_fwd(x1, brs, gl, bg, wbs(), w["w_out"], l, tm=tmix, name=f"mix_fwd_{l}")
        h3, ab2, *got = _norm_mm(x2, g2, w["ffn2_w_up"], l, tm=tml, tn=tff, ntiles=4, pieces=2, transposed=False,
                                 name=f"ffn2_up_{l}", comm=host("ffn2_up", l))
        hosted("ffn2_up", l, got)
        x = _ffn_down(ab2, w["ffn2_w_down"], l, x2, tm=tm, tk=tff, name=f"ffn2_down_{l}")
        saved.append(dict(spb=spb, bias=bias, prep_vjp=prep_vjp, g1=g1, g2=g2, gm=gm, gq2=gq2, gk2=gk2,
                          wdw=wdw, lng=lng, lnb=lnb, bg=bg, x0=x0, h1=h1, ab1=ab1, x1=x1, h2=h2, pa=pa, gl=gl,
                          xr=xr, xi=xi, yp=yp, zg=zg, hg=hg, hc=hc, brs=brs, x2=x2, h3=h3, ab2=ab2))

    dx, lsum = _loss_grad(x, target, tm=tm, name="loss")
    loss_part = 0.5 * jnp.sum(lsum) / d

    big = {k: None for k in _BIG_KEYS}
    small = {k: [None] * nl for k in _SMALL_KEYS}
    hooks = {"pending": None, "early": None}
    assert rs is None or nl == 2
    for l in reversed(range(nl)):
        sv = saved[l]

        def ffn_bwd(dx, xin, h, ab, g, tag):
            wu, wd = w[tag + "_w_up"], w[tag + "_w_down"]
            comm = hooks["early"] if (tag == "ffn1" and l == 0) else None
            dab, big[tag + "_w_down"], *parts = _ffn_dact(dx, wd, l, ab, nl, big[tag + "_w_down"], tm=tm, tk=tff,
                                                          name=f"{tag}_dact_{l}", comm=comm)
            if comm is not None:
                rs.parts_early = parts
            big[tag + "_w_up"] = _mm_tn(h[None], dab, l, nl, big[tag + "_w_up"], ta=d, tb=tff, tk=tml,
                                        name=f"{tag}_dwu_{l}")
            comm = None
            if rs is not None and l == 0:
                keys = list(_BIG_KEYS) if tag == "ffn2" else rs.late
                comm = ([big[k] for k in keys], 1 if tag == "ffn2" else 0)
            dxo, dg, *recv = _ffn_dx(dab, wu, l, xin, g, dx, tm=tml, tk=tff, name=f"{tag}_dx_{l}", comm=comm)
            small[tag + "_norm"][l] = dg[0]
            if comm is not None and tag == "ffn2":
                hooks["pending"] = rs.sums(big, recv, 1, keys)
            elif comm is not None:
                rs.recv_late = recv
            return dxo

        dx = ffn_bwd(dx, sv["x2"], sv["h3"], sv["ab2"], sv["g2"], "ffn2")

        mix_keys = ("w_br_s5", "w_br_attn", "w_br_conv", "w_out")
        bufs = None if big["w_out"] is None else [big[k] for k in mix_keys]
        ds5o, datto, dconvo, dgl, dbg, *dws = _mix_out_bwd(
            dx, sv["brs"], sv["gl"], sv["bg"], wbs(), w["w_out"], l, nl, bufs, tm=tmix, name=f"mix_bwd_{l}")
        small["b_gate"][l] = dbg[0]
        big.update(zip(mix_keys, dws))

        dhc, dlng, dlnb = _conv_bwd_ln(dconvo, sv["hc"], sv["lng"], sv["lnb"], tm=tm, name=f"conv_bwd_ln_{l}")
        dz, dwdw, dbdw = _conv_bwd_dw(dhc, sv["hg"], sv["pa"], sv["wdw"], bl=bl, s=s, t=tconv, acol=acol,
                                      name=f"conv_bwd_dw_{l}")
        small["conv_w_dw"][l] = dwdw[:CONV_W]
        small["conv_b_dw"][l], small["conv_ln_g"][l], small["conv_ln_b"][l] = dbdw[0], dlng[0], dlnb[0]

        comm = hooks["pending"] if l == 0 else None
        dq, dkn, dvw, dbias, dgq, *hosted = _attn_bwd(datto, sv["pa"], sv["gq2"], sv["gk2"], sv["bias"], bl=bl, s=s,
                                                      datt=datt, qoff=qoff, name=f"attn_bwd_{l}", comm=comm)
        if comm is not None:
            rs.parts = hosted
        dk, dv, dgk = _attn_kv_bwd(dkn, dvw, sv["pa"], sv["gk2"], bl=bl, s=s, datt=datt, tm=_tile(s, 512), koff=koff,
                                   name=f"attn_kv_bwd_{l}")
        small["attn_q_gain"][l] = jnp.sum(dgq.reshape(-1, HEAD_DIM), axis=0)
        small["attn_k_gain"][l] = jnp.sum(dgk.reshape(-1, HEAD_DIM), axis=0)

        du, dd, dcr, dci, dbr, dbi, da, big["s5_w_glu"] = _s5_bwd(
            ds5o, sv["yp"], sv["zg"], sv["xr"], sv["xi"], sv["pa"], sv["spb"], w["s5_w_glu"], l, nl, big["s5_w_glu"],
            bl=bl, s=s, t=ts5, name=f"s5_bwd_{l}")
        prep_ct = (dict(a=da, bblk_r=dbr, bblk_i=dbi, cblk_r=dcr, cblk_in=dci, d=dd), jnp.sum(dbias, axis=0))
        (dprep,) = sv["prep_vjp"](prep_ct)
        for k in _PREP_KEYS:
            small[k][l] = dprep[k][l]

        dpa = jnp.concatenate([du, dq, dk, dv, dz], axis=1)
        big["w_in"] = _dwin_t(dpa, dgl, sv["h2"], l, nl, big["w_in"], ta=tna, tk=tml, name=f"dwin_{l}")
        comm = ([big[k] for k in rs.early], 0) if (rs is not None and l == 0) else None
        dx, dgm, *recv = _mix_dx(dpa, dgl, w["w_in"], l, sv["x1"], sv["gm"], dx, tm=tml, tk=tna, name=f"mix_dx_{l}",
                                 comm=comm)
        small["mix_norm"][l] = dgm[0]
        if comm is not None:
            hooks["early"] = rs.sums(big, recv, 0, rs.early)

        dx = ffn_bwd(dx, sv["x0"], sv["h1"], sv["ab1"], sv["g1"], "ffn1")

    small = {k: jnp.stack(v) for k, v in small.items()}
    return loss_part, dx.reshape(bl, s, d), big, small


def _place():
    x, y, c = lax.axis_index("x"), lax.axis_index("y"), lax.axis_index("c")
    chips = [(1 - x, y), (x, 1 - y), (1 - x, 1 - y)]
    return x, y, c, chips


def _remote(src, dst, send_sems, recv_sems, k, dev):
    return pltpu.make_async_remote_copy(src_ref=src, dst_ref=dst, send_sem=send_sems.at[k], recv_sem=recv_sems.at[k],
                                        device_id=dev, device_id_type=MESH)


def _window(ref, lead, s, axis, blk):
    if axis == 1:
        sl = (pl.ds(pl.multiple_of(s * blk, 16), blk), slice(None))
    else:
        sl = (slice(None), pl.ds(pl.multiple_of(s * blk, LANES), blk))
    return ref.at[sl] if lead is None else ref.at[(lead,) + sl]


def _gather_ops(bufs, axes, send_sems, recv_sems, lyr):
    x, y, c, chips = _place()
    s_me = 2 * x + y
    sibling = (x, y, 1 - lyr)
    nw = len(bufs)
    blks = [f.shape[ax] // N_CHIPS for f, ax in zip(bufs, axes)]
    win = lambda i, s: _window(bufs[i], lyr, s, axes[i], blks[i])
    pairs = [(i, j, cx, cy) for i in range(nw) for j, (cx, cy) in enumerate(chips)]
    sends = [_remote(win(i, s_me), win(i, s_me), send_sems, recv_sems, 6 * i + j, (cx, cy, lyr)) for i, j, cx, cy in pairs]
    passed = [_remote(win(i, 2 * cx + cy), win(i, 2 * cx + cy), send_sems, recv_sems, 6 * i + 3 + j, sibling)
              for i, j, cx, cy in pairs]

    def start():
        @pl.when(c == lyr)
        def _():
            for cp in sends:
                cp.start()

    def forward():
        @pl.when(c == lyr)
        def _():
            for (i, j, cx, cy), fw in zip(pairs, passed):
                piece = win(i, 2 * cx + cy)
                _remote(piece, piece, send_sems, recv_sems, 6 * i + j, (cx, cy, lyr)).wait_recv()
                fw.start()

    def wait():
        @pl.when(c == lyr)
        def _():
            for cp in sends + passed:
                cp.wait_send()

        @pl.when(c != lyr)
        def _():
            for fw in passed:
                fw.wait_recv()

    return start, forward, wait


def _all_gather_weights(fulls, axes, taps):
    nw = len(fulls)

    def body(*refs):
        taps_in = refs[nw]
        outs, taps_out = refs[nw + 1:2 * nw + 1], refs[2 * nw + 1]
        send_sems, recv_sems, tap_send, tap_recv, local_sem = refs[-5:]
        x, y, c, chips = _place()
        s_me = 2 * x + y
        start, forward, wait = _gather_ops(outs, axes, send_sems, recv_sems, 0)
        own_taps = pltpu.make_async_copy(taps_in, taps_out.at[s_me], local_sem)
        own_taps.start()
        tap_sends = [_remote(taps_in, taps_out.at[s_me], tap_send, tap_recv, j, (cx, cy, c))
                     for j, (cx, cy) in enumerate(chips)]
        for cp in tap_sends:
            cp.start()
        start()
        forward()
        wait()
        for j, (cx, cy) in enumerate(chips):
            slab = taps_out.at[2 * cx + cy]
            _remote(slab, slab, tap_send, tap_recv, j, (cx, cy, c)).wait_recv()
        for cp in tap_sends:
            cp.wait_send()
        own_taps.wait()

    return pl.pallas_call(
        body, in_specs=[_ANY] * (nw + 1), out_specs=[_ANY] * (nw + 1),
        out_shape=[_sds(f.shape, f.dtype) for f in fulls] + [_sds((N_CHIPS,) + taps.shape, taps.dtype)],
        input_output_aliases={i: i for i in range(nw)},
        scratch_shapes=[pltpu.SemaphoreType.DMA((6 * nw,)), pltpu.SemaphoreType.DMA((6 * nw,)),
                        pltpu.SemaphoreType.DMA((3,)), pltpu.SemaphoreType.DMA((3,)), pltpu.SemaphoreType.DMA],
        name="all_gather_weights")(*fulls, taps)


def _hosted_gather(comm, n_in, n_out):
    if comm is None:
        return [], [], [], [], [], {}
    bufs = list(comm[0])
    nw = len(bufs)
    return (bufs, [_ANY] * nw, [_ANY] * nw, [_sds(f.shape, f.dtype) for f in bufs],
            [pltpu.SemaphoreType.DMA((6 * nw,)), pltpu.SemaphoreType.DMA((6 * nw,))],
            {n_in + k: n_out + k for k in range(nw)})


def _hosted_gather_steps(comm, out_refs, sems, step, total):
    start, forward, wait = _gather_ops(out_refs, comm[1], sems[0], sems[1], comm[2])

    @pl.when(step == 0)
    def _():
        start()

    def finish():
        @pl.when(step == (3 * total) // 4)
        def _():
            forward()

        @pl.when(step == total - 1)
        def _():
            wait()

    return finish


def _swap_ops(ins, outs, send_sems, recv_sems, lyr):
    x, y, c, _ = _place()
    cps = [_remote(ins[i].at[lyr], outs[i], send_sems, recv_sems, i, (x, y, lyr)) for i in range(len(ins))]

    def start():
        @pl.when(c != lyr)
        def _():
            for cp in cps:
                cp.start()

    def wait():
        @pl.when(c != lyr)
        def _():
            for cp in cps:
                cp.wait_send()

        @pl.when(c == lyr)
        def _():
            for cp in cps:
                cp.wait_recv()

    return start, wait


def _exchange_ops(ins, outs, send_sems, recv_sems, axes, lyr):
    x, y, c, chips = _place()
    s_me = 2 * x + y
    nw = len(ins)
    blks = [r.shape[ax - 1] // N_CHIPS for r, ax in zip(ins, axes)]
    win = lambda i, s: _window(ins[i], None, s, axes[i], blks[i])
    sends = [_remote(win(i, 2 * cx + cy), outs[i].at[s_me], send_sems, recv_sems, 3 * i + j, (cx, cy, lyr))
             for i in range(nw) for j, (cx, cy) in enumerate(chips)]

    def start():
        @pl.when(c == lyr)
        def _():
            for cp in sends:
                cp.start()

    def wait():
        @pl.when(c == lyr)
        def _():
            for i in range(nw):
                for j, (cx, cy) in enumerate(chips):
                    slab = outs[i].at[2 * cx + cy]
                    _remote(slab, slab, send_sems, recv_sems, 3 * i + j, (cx, cy, lyr)).wait_recv()
            for cp in sends:
                cp.wait_send()

    return start, wait


def _parts_shapes(rsums, axes):
    shard = [tuple(dim // N_CHIPS if i == ax - 1 else dim for i, dim in enumerate(r.shape)) for r, ax in zip(rsums, axes)]
    return [_sds((N_CHIPS,) + sh, r.dtype) for sh, r in zip(shard, rsums)]


def _rs_exchange(rsums, axes, lyr):
    nw = len(rsums)

    def body(*refs):
        start, wait = _exchange_ops(refs[:nw], refs[nw:2 * nw], refs[-2], refs[-1], axes, lyr)
        start()
        wait()

    return pl.pallas_call(
        body, in_specs=[_ANY] * nw, out_specs=[_ANY] * nw, out_shape=_parts_shapes(rsums, axes),
        scratch_shapes=[pltpu.SemaphoreType.DMA((3 * nw,)), pltpu.SemaphoreType.DMA((3 * nw,))],
        name=f"rs_exchange_l{lyr}")(*rsums)


def _rs_join(ts):
    nw = len(ts)

    def body(*refs):
        outs = refs[nw:2 * nw]
        send_sems, recv_sems = refs[-2:]
        x, y, c, _ = _place()
        sends = [_remote(outs[i].at[c], outs[i].at[c], send_sems, recv_sems, i, (x, y, 1 - c)) for i in range(nw)]
        for cp in sends:
            cp.start()
        for i in range(nw):
            slab = outs[i].at[1 - c]
            _remote(slab, slab, send_sems, recv_sems, i, (x, y, 1 - c)).wait_recv()
        for cp in sends:
            cp.wait_send()

    return pl.pallas_call(
        body, in_specs=[_ANY] * nw, out_specs=[_ANY] * nw, out_shape=[_sds(t.shape, t.dtype) for t in ts],
        input_output_aliases={i: i for i in range(nw)},
        scratch_shapes=[pltpu.SemaphoreType.DMA((nw,)), pltpu.SemaphoreType.DMA((nw,))],
        name="rs_join_layers")(*ts)


def _all_reduce_small(arrs):
    na = len(arrs)
    nd = 8

    def body(*refs):
        ins, outs, recvs = refs[:na], refs[na:2 * na], refs[2 * na:3 * na]
        send_sems, recv_sems = refs[-2:]
        x, y, c, _ = _place()
        me = 4 * x + 2 * y + c
        for i in range(na):
            recvs[i][0] = ins[i][...]
        cps = []
        for rel in range(1, nd):
            dev = (1 - x if rel & 4 else x, 1 - y if rel & 2 else y, 1 - c if rel & 1 else c)
            for i in range(na):
                cp = _remote(ins[i], recvs[i].at[rel], send_sems, recv_sems, (rel - 1) * na + i, dev)
                cp.start()
                cps.append(cp)
        for rel in range(1, nd):
            for i in range(na):
                _remote(ins[i], recvs[i].at[rel], send_sems, recv_sems, (rel - 1) * na + i, (x, y, c)).wait_recv()
        for i in range(na):
            acc = recvs[i][me]
            for dv in range(1, nd):
                acc = acc + recvs[i][lax.bitwise_xor(me, dv)]
            outs[i][...] = acc
        for cp in cps:
            cp.wait_send()

    vm = pl.BlockSpec(memory_space=pltpu.VMEM)
    nsem = (nd - 1) * na
    return pl.pallas_call(
        body, in_specs=[vm] * na, out_specs=[vm] * na, out_shape=[_sds(t.shape, F32) for t in arrs],
        scratch_shapes=[pltpu.VMEM((nd,) + t.shape, F32) for t in arrs]
        + [pltpu.SemaphoreType.DMA((nsem,)), pltpu.SemaphoreType.DMA((nsem,))],
        compiler_params=pltpu.CompilerParams(vmem_limit_bytes=VMEM_LIMIT), name="all_reduce_small")(*arrs)


def _adamw_small(ws, gs, ms, vs):
    na = len(ws)
    c1 = 1.0 - ADAM_B1 ** ADAM_STEP
    c2 = 1.0 - ADAM_B2 ** ADAM_STEP

    def body(*refs):
        w_r, g_r, m_r, v_r = (refs[k * na:(k + 1) * na] for k in range(4))
        d_r, nm_r, nv_r = (refs[(4 + k) * na:(5 + k) * na] for k in range(3))
        for i in range(na):
            gv = g_r[i][...]
            mn = ADAM_B1 * m_r[i][...] + (1.0 - ADAM_B1) * gv
            vn = ADAM_B2 * v_r[i][...] + (1.0 - ADAM_B2) * (gv * gv)
            nm_r[i][...] = mn
            nv_r[i][...] = vn
            d_r[i][...] = -ADAM_LR * ((mn / c1) / (jnp.sqrt(vn / c2) + ADAM_EPS) + ADAM_WD * w_r[i][...])

    vm = pl.BlockSpec(memory_space=pltpu.VMEM)
    res = pl.pallas_call(
        body, in_specs=[vm] * (4 * na), out_specs=[vm] * (3 * na), out_shape=[_sds(t.shape, F32) for t in ws] * 3,
        compiler_params=pltpu.CompilerParams(vmem_limit_bytes=VMEM_LIMIT), name="adamw_small")(*ws, *gs, *ms, *vs)
    return res[:na], res[na:2 * na], res[2 * na:]


def kernel(x, ffn1_norm, ffn1_w_up, ffn1_w_down, mix_norm, w_in, b_gate, s5_lambda_re, s5_lambda_im, s5_log_dt, s5_b_re, s5_b_im, s5_c_re, s5_c_im, s5_d, s5_w_glu, w_br_s5, attn_q_gain, attn_k_gain, attn_rel_bias, w_br_attn, conv_w_dw, conv_b_dw, conv_ln_g, conv_ln_b, w_br_conv, w_out, ffn2_norm, ffn2_w_up, ffn2_w_down, loss_target, m_ffn1_norm, m_ffn1_w_up, m_ffn1_w_down, m_mix_norm, m_w_in, m_b_gate, m_s5_lambda_re, m_s5_lambda_im, m_s5_log_dt, m_s5_b_re, m_s5_b_im, m_s5_c_re, m_s5_c_im, m_s5_d, m_s5_w_glu, m_w_br_s5, m_attn_q_gain, m_attn_k_gain, m_attn_rel_bias, m_w_br_attn, m_conv_w_dw, m_conv_b_dw, m_conv_ln_g, m_conv_ln_b, m_w_br_conv, m_w_out, m_ffn2_norm, m_ffn2_w_up, m_ffn2_w_down, v_ffn1_norm, v_ffn1_w_up, v_ffn1_w_down, v_mix_norm, v_w_in, v_b_gate, v_s5_lambda_re, v_s5_lambda_im, v_s5_log_dt, v_s5_b_re, v_s5_b_im, v_s5_c_re, v_s5_c_im, v_s5_d, v_s5_w_glu, v_w_br_s5, v_attn_q_gain, v_attn_k_gain, v_attn_rel_bias, v_w_br_attn, v_conv_w_dw, v_conv_b_dw, v_conv_ln_g, v_conv_ln_b, v_w_br_conv, v_w_out, v_ffn2_norm, v_ffn2_w_up, v_ffn2_w_down):
    a = dict(locals())
    xi, yi, ci = lax.axis_index("x"), lax.axis_index("y"), lax.axis_index("c")
    s_me = 2 * xi + yi
    big_keys = list(_BIG_KEYS)
    axes = [1 if k == "w_in" else _BIG_KEYS[k] for k in big_keys]

    s_idx = s_me.astype(jnp.int32).reshape(1)
    c_idx = ci.astype(jnp.int32).reshape(1)
    placed = {k: _place_shard(jnp.swapaxes(a[k], 1, 2).astype(BF16) if k == "w_in" else a[k], ax, s_idx,
                              name=f"place_{k}") for k, ax in zip(big_keys, axes)}
    axis_of = dict(zip(big_keys, axes))
    first = ["ffn1_w_up", "ffn1_w_down"]
    *fulls, taps = _all_gather_weights([placed[k] for k in first], [axis_of[k] for k in first], a["conv_w_dw"])
    placed.update(zip(first, fulls))
    flat = lambda t: t.reshape(t.shape[0], t.shape[1], -1) if t.ndim == 4 else t
    w = {k: flat(a[k]) for k in _WEIGHTS}
    w.update(placed)
    w["conv_w_dw"] = jnp.moveaxis(taps, 0, 2).reshape(taps.shape[1], taps.shape[2], -1)

    class _ReduceScatter:
        parts = parts_early = recv_late = None

        def __init__(self):
            self.rsums = {}
            self.late = ["ffn1_w_up", "ffn1_w_down"]
            self.early = [k for k in big_keys if k not in self.late]

        def sums(self, big, recv, lyr, keys):
            rsums = [_add_sibling(big[k], r, lyr, c_idx, name=f"rs_add_sibling_{k}_l{lyr}") for r, k in zip(recv, keys)]
            self.rsums.update({(k, lyr): r for k, r in zip(keys, rsums)})
            return rsums, [axis_of[k] for k in keys], lyr

    rs = _ReduceScatter()
    early = ("ffn1_w_up", "ffn1_w_down", "w_in")
    second = ["ffn2_w_up"]
    sets = {"ffn1_up": ([k for k in big_keys if k not in first + second], 0),
            "win_a": (second, 0),
            "attn_fwd": ([k for k in big_keys if k in early], 1),
            "ffn2_up": ([k for k in big_keys if k not in early], 1)}
    gather = {tag: (keys, [axis_of[k] for k in keys], lyr) for tag, (keys, lyr) in sets.items()}
    loss_part, grad_x, gbig, gsmall = _local_step(a["x"], a["loss_target"], w, rs, gather)
    loss = lax.psum(loss_part, ("x", "y", "c"))

    rsums_late, axes_late, _ = rs.sums(gbig, rs.recv_late, 0, rs.late)
    parts0 = dict(zip(rs.early, rs.parts_early))
    parts0.update(zip(rs.late, _rs_exchange(rsums_late, axes_late, 0)))
    mine = [None] * len(big_keys)
    for lyr, parts in ((1, dict(zip(big_keys, rs.parts))), (0, parts0)):
        mine = [_add_chips(rs.rsums[k, lyr], parts[k], ax, s_idx, c_idx, lyr, buf, name=f"rs_add_chips_{k}_l{lyr}")
                for ax, buf, k in zip(axes, mine, big_keys)]
    gb = dict(zip(big_keys, _rs_join(mine)))
    gb["w_in"] = jnp.swapaxes(gb["w_in"], 1, 2)

    small_keys = list(_SMALL_KEYS)
    gs = dict(zip(small_keys, _all_reduce_small([gsmall[k] for k in small_keys])))
    blk = a["conv_w_dw"].shape[2]
    gs["conv_w_dw"] = lax.dynamic_slice_in_dim(gs["conv_w_dw"], s_me * blk, blk, axis=2)

    delta, new_m, new_v = {}, {}, {}
    for k in big_keys:
        shp = a[k].shape
        two_d = lambda t: t.reshape(-1, shp[-1])
        d_, m_, v_ = _adamw(two_d(a[k]), two_d(gb[k]), two_d(a["m_" + k]), two_d(a["v_" + k]), name=f"adamw_{k}")
        delta[k], new_m[k], new_v[k] = d_.reshape(shp), m_.reshape(shp), v_.reshape(shp)
    res = _adamw_small([flat(a[k]) for k in small_keys], [gs[k] for k in small_keys],
                       [flat(a["m_" + k]) for k in small_keys], [flat(a["v_" + k]) for k in small_keys])
    for dst, vals in zip((delta, new_m, new_v), res):
        dst.update({k: t.reshape(a[k].shape) for k, t in zip(small_keys, vals)})
    grads = {**gb, **{k: t.reshape(a[k].shape) for k, t in gs.items()}}

    return (loss, grad_x, *[grads[k] for k in _WEIGHTS], *[delta[k] for k in _WEIGHTS],
            *[new_m[k] for k in _WEIGHTS], *[new_v[k] for k in _WEIGHTS])
```

```python
import functools
import math

import numpy as np
import jax
import jax.numpy as jnp
from jax import lax
from jax.experimental import pallas as pl
from jax.experimental.pallas import tpu as pltpu

F32 = jnp.float32
BF16 = jnp.bfloat16
EPS = 1e-6
VMEM_LIMIT = 56 * 1024 * 1024
LANES = 128
HEAD_DIM = 64
CHUNK = 64
N_LEFT = 8
MAX_REL = 128
ATT_TQ = 256
CONV_W = 31
HALO = 32
ROW_CHUNK = 256
SUBLANES = 8
GROUP_LOG = 3
NEG = -1e30
N_CHIPS = 4
PACK_COLS = 1024

ADAM_LR = 0.001
ADAM_B1 = 0.9
ADAM_B2 = 0.999
ADAM_EPS = 1e-08
ADAM_WD = 0.01
ADAM_STEP = 10

MESH = pl.DeviceIdType.MESH
_ANY = pl.BlockSpec(memory_space=pl.ANY)


def _cp(*sem):
    return pltpu.CompilerParams(dimension_semantics=sem, vmem_limit_bytes=VMEM_LIMIT)


def _sds(shape, dtype):
    return jax.ShapeDtypeStruct(shape, dtype)


def _tile(n, pref):
    t = min(n, pref)
    while n % t:
        t -= 8
    return t


def _sigmoid(x):
    return jax.nn.sigmoid(x)


_GELU_C = math.sqrt(2.0 / math.pi)


def _gelu(y):
    return 0.5 * y * (1.0 + jnp.tanh(_GELU_C * (y + 0.044715 * y * y * y)))


def _gelu_grad(y):
    th = jnp.tanh(_GELU_C * (y + 0.044715 * y * y * y))
    return 0.5 * (1.0 + th) + 0.5 * y * (1.0 - th * th) * _GELU_C * (1.0 + 3.0 * 0.044715 * y * y)


def _dot(a, b):
    return jnp.dot(a, b, preferred_element_type=F32)


def _dot_t0(a, b):
    return lax.dot_general(a, b, (((0,), (0,)), ((), ())), preferred_element_type=F32)


def _dot_t1(a, b):
    return lax.dot_general(a, b, (((1,), (1,)), ((), ())), preferred_element_type=F32)


def _slab_out(nl, l, shape, buf, n_in):
    sds = _sds((nl,) + tuple(shape), F32)
    if buf is None:
        return [], [], sds, {}
    return [buf], [_ANY], sds, {n_in: 0}


def _norm_mm(x, g, w, l, *, tm, tn, ntiles, pieces, transposed, name, comm=None):
    n, d = x.shape
    m = ntiles * tn
    mp = m // pieces
    npj = mp // tn
    nc = 0 if comm is None else len(comm[0])

    def body(x_ref, g_ref, w_ref, *rest):
        h_ref, y_ref = rest[nc:nc + 2]
        h_scr = rest[2 * nc + 2]
        if comm is not None:
            finish = _hosted_gather_steps(comm, rest[nc + 2:2 * nc + 2], rest[-2:],
                                          pl.program_id(0) * ntiles + pl.program_id(1), (n // tm) * ntiles)

        @pl.when(pl.program_id(1) == 0)
        def _():
            for r0 in range(0, tm, ROW_CHUNK):
                rows = slice(r0, r0 + ROW_CHUNK)
                xv = x_ref[rows, :]
                r = lax.rsqrt(jnp.mean(xv * xv, axis=-1, keepdims=True) + EPS)
                hb = (xv * r * g_ref[...]).astype(BF16)
                h_scr[rows, :] = hb
                h_ref[rows, :] = hb

        mm = _dot_t1 if transposed else _dot
        y_ref[...] = mm(h_scr[...], w_ref[...]).astype(BF16)
        if comm is not None:
            finish()

    wspec = (pl.BlockSpec((None, tn, d), lambda i, j: (l, j, 0)) if transposed
             else pl.BlockSpec((None, d, tn), lambda i, j: (l, 0, j)))
    c_in, c_ispec, c_ospec, c_oshape, c_scr, aliases = _hosted_gather(comm, 3, 2)
    return pl.pallas_call(
        body, grid=(n // tm, ntiles),
        in_specs=[pl.BlockSpec((tm, d), lambda i, j: (i, 0)), pl.BlockSpec((1, d), lambda i, j: (0, 0)), wspec] + c_ispec,
        out_specs=[pl.BlockSpec((tm, d), lambda i, j: (i, 0)),
                   pl.BlockSpec((None, tm, tn), lambda i, j: (j // npj, i, j % npj))] + c_ospec,
        out_shape=[_sds((n, d), BF16), _sds((pieces, n, mp), BF16)] + c_oshape,
        input_output_aliases=aliases,
        scratch_shapes=[pltpu.VMEM((tm, d), BF16)] + c_scr,
        compiler_params=_cp("arbitrary", "arbitrary"), name=name)(x, g, w, *c_in)


def _mm_t(a, w, l, *, tm, tn, off, ntiles, name):
    n, k = a.shape

    def body(a_ref, w_ref, y_ref):
        y_ref[...] = _dot_t1(a_ref[...], w_ref[...]).astype(BF16)

    return pl.pallas_call(
        body, grid=(n // tm, ntiles),
        in_specs=[pl.BlockSpec((tm, k), lambda i, j: (i, 0)), pl.BlockSpec((None, tn, k), lambda i, j: (l, off + j, 0))],
        out_specs=pl.BlockSpec((tm, tn), lambda i, j: (i, j)),
        out_shape=_sds((n, ntiles * tn), BF16),
        compiler_params=_cp("parallel", "arbitrary"), name=name)(a, w)


def _ffn_down(ab, wd, l, x, *, tm, tk, name):
    _, n, dff = ab.shape
    d = x.shape[1]
    nk = dff // tk

    def body(a_ref, b_ref, wd_ref, x_ref, o_ref, acc):
        k = pl.program_id(1)
        a = a_ref[...].astype(F32)
        b = b_ref[...].astype(F32)
        act = (a * _sigmoid(a) * b).astype(BF16)
        part = _dot(act, wd_ref[pl.ds(pl.multiple_of(k * tk, tk), tk), :])

        @pl.when(k == 0)
        def _():
            acc[...] = part

        @pl.when(k > 0)
        def _():
            acc[...] += part

        @pl.when(k == nk - 1)
        def _():
            o_ref[...] = x_ref[...] + 0.5 * acc[...]

    return pl.pallas_call(
        body, grid=(n // tm, nk),
        in_specs=[pl.BlockSpec((None, tm, tk), lambda i, k: (0, i, k)),
                  pl.BlockSpec((None, tm, tk), lambda i, k: (1, i, k)),
                  pl.BlockSpec((None, dff, d), lambda i, k: (l, 0, 0)),
                  pl.BlockSpec((tm, d), lambda i, k: (i, 0))],
        out_specs=pl.BlockSpec((tm, d), lambda i, k: (i, 0)),
        out_shape=_sds((n, d), F32),
        scratch_shapes=[pltpu.VMEM((tm, d), F32)],
        compiler_params=_cp("parallel", "arbitrary"), name=name)(ab, ab, wd, x)


def _ffn_dact(dx, wd, l, ab, nl, dwd_buf, *, tm, tk, name, comm=None):
    n, d = dx.shape
    dff = ab.shape[2]
    half = ((tk // LANES + 1) // 2) * LANES
    chunks = ((0, half), (half, tk))
    ne = 0 if dwd_buf is None else 1
    nc = 0 if comm is None else len(comm[0])
    ni = n // tm

    def body(dx_ref, wd_ref, a_ref, b_ref, *rest):
        dab_ref, dwd_ref = rest[ne + nc:ne + nc + 2]
        if comm is not None:
            start, wait = _exchange_ops(rest[ne:ne + nc], rest[ne + nc + 2:ne + 2 * nc + 2], rest[-2], rest[-1],
                                        comm[1], comm[2])
            step = pl.program_id(0) * ni + pl.program_id(1)

            @pl.when(step == 0)
            def _():
                start()

        do = (0.5 * dx_ref[...]).astype(BF16)

        @pl.when(pl.program_id(1) == 0)
        def _():
            dwd_ref[...] = jnp.zeros_like(dwd_ref)

        for c0, c1 in chunks:
            dact = _dot_t1(do, wd_ref[c0:c1, :])
            a = a_ref[:, c0:c1].astype(F32)
            b = b_ref[:, c0:c1].astype(F32)
            sg = _sigmoid(a)
            silu = a * sg
            dab_ref[0, :, c0:c1] = (dact * b * (sg * (1.0 + a * (1.0 - sg)))).astype(BF16)
            dab_ref[1, :, c0:c1] = (dact * silu).astype(BF16)
            dwd_ref[c0:c1, :] += _dot_t0((silu * b).astype(BF16), do)

        if comm is not None:
            @pl.when(step == (dff // tk) * ni - 1)
            def _():
                wait()

    extra, extra_specs, dwd_shape, aliases = _slab_out(nl, l, (dff, d), dwd_buf, 4)
    aliases = {k: 1 for k in aliases}
    comm_in = [] if comm is None else list(comm[0])
    comm_out = [] if comm is None else _parts_shapes(comm[0], comm[1])
    comm_scr = [] if comm is None else [pltpu.SemaphoreType.DMA((3 * nc,)), pltpu.SemaphoreType.DMA((3 * nc,))]
    return pl.pallas_call(
        body, grid=(dff // tk, ni),
        in_specs=[pl.BlockSpec((tm, d), lambda j, i: (i, 0)),
                  pl.BlockSpec((None, tk, d), lambda j, i: (l, j, 0)),
                  pl.BlockSpec((None, tm, tk), lambda j, i: (0, i, j)),
                  pl.BlockSpec((None, tm, tk), lambda j, i: (1, i, j)), *extra_specs] + [_ANY] * nc,
        out_specs=[pl.BlockSpec((2, tm, tk), lambda j, i: (0, i, j)),
                   pl.BlockSpec((None, tk, d), lambda j, i: (l, j, 0))] + [_ANY] * nc,
        out_shape=[_sds((2, n, dff), BF16), dwd_shape] + comm_out,
        input_output_aliases=aliases, scratch_shapes=comm_scr,
        compiler_params=_cp("arbitrary", "arbitrary"), name=name)(dx, wd, ab, ab, *extra, *comm_in)


def _rms_bwd_epilogue(acc, x_ref, g_ref, dres_ref, dx_ref, dg_ref, i):
    dgp = jnp.zeros(dg_ref.shape, F32)
    for r0 in range(0, acc.shape[0], ROW_CHUNK):
        rows = slice(r0, r0 + ROW_CHUNK)
        dh = acc[rows, :]
        xv = x_ref[rows, :]
        r = lax.rsqrt(jnp.mean(xv * xv, axis=-1, keepdims=True) + EPS)
        xn = xv * r
        dgp = dgp + jnp.sum(dh * xn, axis=0, keepdims=True)
        dxh = dh * g_ref[...]
        dx_ref[rows, :] = dres_ref[rows, :] + r * (dxh - xn * jnp.mean(dxh * xn, axis=-1, keepdims=True))

    @pl.when(i == 0)
    def _():
        dg_ref[...] = dgp

    @pl.when(i > 0)
    def _():
        dg_ref[...] += dgp


def _ffn_dx(dab, wu, l, x, g, dres, *, tm, tk, name, comm=None):
    p, n, mp = dab.shape
    d = x.shape[1]
    nkp = mp // tk
    nk = p * nkp
    ni = n // tm
    nc = 0 if comm is None else len(comm[0])

    def body(dy_ref, w_ref, x_ref, g_ref, dres_ref, *rest):
        dx_ref, dg_ref = rest[nc:nc + 2]
        acc = rest[2 * nc + 2]
        k = pl.program_id(1)
        if comm is not None:
            start, wait = _swap_ops(rest[:nc], rest[nc + 2:2 * nc + 2], rest[-2], rest[-1], comm[1])

            @pl.when((pl.program_id(0) == 0) & (k == 0))
            def _():
                start()

        part = _dot_t1(dy_ref[...], w_ref[...])

        @pl.when(k == 0)
        def _():
            acc[...] = part

        @pl.when(k > 0)
        def _():
            acc[...] += part

        @pl.when(k == nk - 1)
        def _():
            _rms_bwd_epilogue(acc, x_ref, g_ref, dres_ref, dx_ref, dg_ref, pl.program_id(0))

        if comm is not None:
            @pl.when((pl.program_id(0) == ni - 1) & (k == nk - 1))
            def _():
                wait()

    comm_in = [] if comm is None else list(comm[0])
    comm_out = [_sds(t.shape[1:], t.dtype) for t in comm_in]
    comm_scr = [] if comm is None else [pltpu.SemaphoreType.DMA((nc,)), pltpu.SemaphoreType.DMA((nc,))]
    return pl.pallas_call(
        body, grid=(ni, nk),
        in_specs=[pl.BlockSpec((None, tm, tk), lambda i, k: (k // nkp, i, k % nkp)),
                  pl.BlockSpec((None, d, tk), lambda i, k: (l, 0, k)),
                  pl.BlockSpec((tm, d), lambda i, k: (i, 0)),
                  pl.BlockSpec((1, d), lambda i, k: (0, 0)),
                  pl.BlockSpec((tm, d), lambda i, k: (i, 0))] + [_ANY] * nc,
        out_specs=[pl.BlockSpec((tm, d), lambda i, k: (i, 0)), pl.BlockSpec((1, d), lambda i, k: (0, 0))] + [_ANY] * nc,
        out_shape=[_sds((n, d), F32), _sds((1, d), F32)] + comm_out,
        scratch_shapes=[pltpu.VMEM((tm, d), F32)] + comm_scr,
        compiler_params=_cp("arbitrary", "arbitrary"), name=name)(dab, wu, x, g, dres, *comm_in)


def _mix_dx(dpa, dgl, wt, l, x, g, dres, *, tm, tk, name, comm=None):
    n, d = x.shape
    n1 = dpa.shape[1] // tk
    n2 = dgl.shape[1] // tk
    nk = n1 + n2
    ni = n // tm
    nc = 0 if comm is None else len(comm[0])

    def body(d1_ref, d2_ref, w_ref, x_ref, g_ref, dres_ref, *rest):
        dx_ref, dg_ref = rest[nc:nc + 2]
        acc = rest[2 * nc + 2]
        k = pl.program_id(1)
        if comm is not None:
            start, wait = _swap_ops(rest[:nc], rest[nc + 2:2 * nc + 2], rest[-2], rest[-1], comm[1])

            @pl.when((pl.program_id(0) == 0) & (k == 0))
            def _():
                start()

        @pl.when(k == 0)
        def _():
            acc[...] = _dot(d1_ref[...], w_ref[...])

        @pl.when((k > 0) & (k < n1))
        def _():
            acc[...] += _dot(d1_ref[...], w_ref[...])

        @pl.when(k >= n1)
        def _():
            acc[...] += _dot(d2_ref[...], w_ref[...])

        @pl.when(k == nk - 1)
        def _():
            _rms_bwd_epilogue(acc, x_ref, g_ref, dres_ref, dx_ref, dg_ref, pl.program_id(0))

        if comm is not None:
            @pl.when((pl.program_id(0) == ni - 1) & (k == nk - 1))
            def _():
                wait()

    comm_in = [] if comm is None else list(comm[0])
    comm_out = [_sds(t.shape[1:], t.dtype) for t in comm_in]
    comm_scr = [] if comm is None else [pltpu.SemaphoreType.DMA((nc,)), pltpu.SemaphoreType.DMA((nc,))]
    return pl.pallas_call(
        body, grid=(ni, nk),
        in_specs=[pl.BlockSpec((tm, tk), lambda i, k: (i, jnp.minimum(k, n1 - 1))),
                  pl.BlockSpec((tm, tk), lambda i, k: (i, jnp.maximum(k - n1, 0))),
                  pl.BlockSpec((None, tk, d), lambda i, k: (l, k, 0)),
                  pl.BlockSpec((tm, d), lambda i, k: (i, 0)),
                  pl.BlockSpec((1, d), lambda i, k: (0, 0)),
                  pl.BlockSpec((tm, d), lambda i, k: (i, 0))] + [_ANY] * nc,
        out_specs=[pl.BlockSpec((tm, d), lambda i, k: (i, 0)), pl.BlockSpec((1, d), lambda i, k: (0, 0))] + [_ANY] * nc,
        out_shape=[_sds((n, d), F32), _sds((1, d), F32)] + comm_out,
        scratch_shapes=[pltpu.VMEM((tm, d), F32)] + comm_scr,
        compiler_params=_cp("arbitrary", "arbitrary"), name=name)(dpa, dgl, wt, x, g, dres, *comm_in)


def _mm_tn(a, b, l, nl, buf, *, ta, tb, tk, name):
    pa, n, ka = a.shape
    pb, _, kb = b.shape
    nap = ka // ta
    nbp = kb // tb

    def body(a_ref, b_ref, *rest):
        o_ref = rest[-1]

        @pl.when(pl.program_id(2) == 0)
        def _():
            o_ref[...] = jnp.zeros_like(o_ref)

        o_ref[...] += _dot_t0(a_ref[...], b_ref[...])

    extra, extra_specs, out_shape, aliases = _slab_out(nl, l, (pa * ka, pb * kb), buf, 2)
    return pl.pallas_call(
        body, grid=(pa * nap, pb * nbp, n // tk),
        in_specs=[pl.BlockSpec((None, tk, ta), lambda i, j, k: (i // nap, k, i % nap)),
                  pl.BlockSpec((None, tk, tb), lambda i, j, k: (j // nbp, k, j % nbp)), *extra_specs],
        out_specs=pl.BlockSpec((None, ta, tb), lambda i, j, k: (l, i, j)),
        out_shape=out_shape, input_output_aliases=aliases,
        compiler_params=_cp("parallel", "parallel", "arbitrary"), name=name)(a, b, *extra)


def _dwin_t(dpa, dgl, h, l, nl, buf, *, ta, tk, name):
    n, d = h.shape
    n1 = dpa.shape[1] // ta
    n2 = dgl.shape[1] // ta

    def body(a1_ref, a2_ref, h_ref, *rest):
        o_ref = rest[-1]
        i = pl.program_id(0)

        @pl.when(pl.program_id(1) == 0)
        def _():
            o_ref[...] = jnp.zeros_like(o_ref)

        @pl.when(i < n1)
        def _():
            o_ref[...] += _dot_t0(a1_ref[...], h_ref[...])

        @pl.when(i >= n1)
        def _():
            o_ref[...] += _dot_t0(a2_ref[...], h_ref[...])

    extra, extra_specs, out_shape, aliases = _slab_out(nl, l, ((n1 + n2) * ta, d), buf, 3)
    return pl.pallas_call(
        body, grid=(n1 + n2, n // tk),
        in_specs=[pl.BlockSpec((tk, ta), lambda i, k: (jnp.where(i < n1, k, 0), jnp.minimum(i, n1 - 1))),
                  pl.BlockSpec((tk, ta), lambda i, k: (jnp.where(i >= n1, k, 0), jnp.maximum(i - n1, 0))),
                  pl.BlockSpec((tk, d), lambda i, k: (k, 0)), *extra_specs],
        out_specs=pl.BlockSpec((None, ta, d), lambda i, k: (l, i, 0)),
        out_shape=out_shape, input_output_aliases=aliases,
        compiler_params=_cp("parallel", "arbitrary"), name=name)(dpa, dgl, h, *extra)


def _loss_grad(y, t, *, tm, name):
    n, d = y.shape

    def body(y_ref, t_ref, dy_ref, l_ref):
        e = y_ref[...] - t_ref[...]
        dy_ref[...] = e * (1.0 / d)
        part = jnp.sum(e * e, axis=0, keepdims=True)

        @pl.when(pl.program_id(0) == 0)
        def _():
            l_ref[...] = part

        @pl.when(pl.program_id(0) > 0)
        def _():
            l_ref[...] += part

    return pl.pallas_call(
        body, grid=(n // tm,),
        in_specs=[pl.BlockSpec((tm, d), lambda i: (i, 0)), pl.BlockSpec((tm, d), lambda i: (i, 0))],
        out_specs=[pl.BlockSpec((tm, d), lambda i: (i, 0)), pl.BlockSpec((1, d), lambda i: (0, 0))],
        out_shape=[_sds((n, d), F32), _sds((1, d), F32)],
        compiler_params=_cp("arbitrary"), name=name)(y, t)


def _s5_fwd(proj, sp, wglu, l, *, bl, s, t, name):
    n = bl * s
    ds5, gp = sp["bblk_r"].shape
    nt = s // t
    ng = t // SUBLANES
    glog = int(math.log2(ng))

    def body(u_ref, br_ref, bi_ref, pw_ref, p8_ref, cr_ref, ci_ref, d_ref, wg_ref,
             xr_ref, xi_ref, yp_ref, zg_ref, o_ref, carry, st):
        @pl.when(pl.program_id(1) == 0)
        def _():
            carry[...] = jnp.zeros_like(carry)

        u = u_ref[...]
        sub = lax.broadcasted_iota(jnp.int32, (t, gp), 0) % SUBLANES
        xr = _dot(u, br_ref[...])
        xi = _dot(u, bi_ref[...])
        for k in range(GROUP_LOG):
            sh = 1 << k
            pr = pw_ref[2 * k:2 * k + 1, :]
            pi = pw_ref[2 * k + 1:2 * k + 2, :]
            keep = sub >= sh
            sr = jnp.where(keep, pltpu.roll(xr, sh, 0), 0.0)
            si = jnp.where(keep, pltpu.roll(xi, sh, 0), 0.0)
            xr, xi = xr + pr * sr - pi * si, xi + pr * si + pi * sr
        xr_ref[...] = xr
        xi_ref[...] = xi
        grow = lax.broadcasted_iota(jnp.int32, (ng, gp), 0)
        cr = carry[0:1, :]
        ci = carry[1:2, :]
        a8r = pw_ref[2 * GROUP_LOG:2 * GROUP_LOG + 1, :]
        a8i = pw_ref[2 * GROUP_LOG + 1:2 * GROUP_LOG + 2, :]
        head = grow == 0
        for g in range(ng):
            st[g:g + 1, :] = xr_ref[(g + 1) * SUBLANES - 1:(g + 1) * SUBLANES, :]
            st[ng + g:ng + g + 1, :] = xi_ref[(g + 1) * SUBLANES - 1:(g + 1) * SUBLANES, :]
        sr_ = st[0:ng, :] + jnp.where(head, a8r * cr - a8i * ci, 0.0)
        si_ = st[ng:2 * ng, :] + jnp.where(head, a8r * ci + a8i * cr, 0.0)
        for k in range(glog):
            sh = 1 << k
            pr = pw_ref[2 * (GROUP_LOG + k):2 * (GROUP_LOG + k) + 1, :]
            pi = pw_ref[2 * (GROUP_LOG + k) + 1:2 * (GROUP_LOG + k) + 2, :]
            keep = grow >= sh
            tr = jnp.where(keep, pltpu.roll(sr_, sh, 0), 0.0)
            ti = jnp.where(keep, pltpu.roll(si_, sh, 0), 0.0)
            sr_, si_ = sr_ + pr * tr - pi * ti, si_ + pr * ti + pi * tr
        tail = grow == ng - 1
        carry[0:1, :] = jnp.sum(jnp.where(tail, sr_, 0.0), axis=0, keepdims=True)
        carry[1:2, :] = jnp.sum(jnp.where(tail, si_, 0.0), axis=0, keepdims=True)
        st[0:ng, :] = jnp.where(head, cr, pltpu.roll(sr_, 1, 0))
        st[ng:2 * ng, :] = jnp.where(head, ci, pltpu.roll(si_, 1, 0))
        p8r = p8_ref[0:SUBLANES, :]
        p8i = p8_ref[SUBLANES:2 * SUBLANES, :]
        for g in range(ng):
            grp = slice(g * SUBLANES, (g + 1) * SUBLANES)
            pr = st[g:g + 1, :]
            pi = st[ng + g:ng + g + 1, :]
            xr_ref[grp, :] = xr_ref[grp, :] + p8r * pr - p8i * pi
            xi_ref[grp, :] = xi_ref[grp, :] + p8r * pi + p8i * pr
        xr = xr_ref[...]
        xi = xi_ref[...]
        y = _dot(xr.astype(BF16), cr_ref[...]) + _dot(xi.astype(BF16), ci_ref[...]) + d_ref[...] * u.astype(F32)
        yp_ref[...] = y
        zg = _dot(_gelu(y).astype(BF16), wg_ref[...])
        zg_ref[...] = zg
        o_ref[...] = (zg[:, :ds5] * _sigmoid(zg[:, ds5:])).astype(BF16)

    const = lambda shape: pl.BlockSpec(shape, lambda b, i: (0, 0))
    row = lambda w: pl.BlockSpec((t, w), lambda b, i: (b * nt + i, 0))
    return pl.pallas_call(
        body, grid=(bl, nt),
        in_specs=[row(ds5), const((ds5, gp)), const((ds5, gp)), const((2 * (GROUP_LOG + glog), gp)),
                  const((2 * SUBLANES, gp)),
                  const((gp, ds5)), const((gp, ds5)), const((1, ds5)),
                  pl.BlockSpec((None, ds5, 2 * ds5), lambda b, i: (l, 0, 0))],
        out_specs=[row(gp), row(gp), row(ds5), row(2 * ds5), row(ds5)],
        out_shape=[_sds((n, gp), F32), _sds((n, gp), F32), _sds((n, ds5), F32), _sds((n, 2 * ds5), F32),
                   _sds((n, ds5), BF16)],
        scratch_shapes=[pltpu.VMEM((2, gp), F32), pltpu.VMEM((2 * ng, gp), F32)],
        compiler_params=_cp("arbitrary", "arbitrary"), name=name)(
            proj, sp["bblk_r"], sp["bblk_i"], sp["pw"], sp["p8"], sp["cblk_r"], sp["cblk_in"], sp["d"], wglu)


def _s5_bwd(ds, yp, zg, xr, xi, proj, sp, wglu, l, nl, dwg_buf, *, bl, s, t, name):
    n = bl * s
    ds5, gp = sp["bblk_r"].shape
    nt = s // t
    tb = t // 8
    ng = t // SUBLANES
    glog = int(math.log2(ng))

    def body(ds_ref, yp_ref, zg_ref, xr_ref, xi_ref, hr_ref, hi_ref, u_ref, wg_ref, cr_ref, ci_ref,
             br_ref, bi_ref, pw_ref, q8_ref, d_ref, *rest):
        du_ref, dd_ref, dcr_ref, dci_ref, dbr_ref, dbi_ref, da_ref, dwg_ref, carry, gr_scr, gi_scr, st = rest[-12:]
        b = pl.program_id(0)
        i = pl.program_id(1)
        tile = nt - 1 - i

        @pl.when((b == 0) & (i == 0))
        def _():
            for r in (dwg_ref, dd_ref, dcr_ref, dci_ref, dbr_ref, dbi_ref, da_ref):
                r[...] = jnp.zeros_like(r)

        @pl.when(i == 0)
        def _():
            carry[...] = jnp.zeros_like(carry)

        dsv = ds_ref[...].astype(F32)
        zgv = zg_ref[...]
        za = zgv[:, :ds5]
        sg = _sigmoid(zgv[:, ds5:])
        dzg = jnp.concatenate([dsv * sg, dsv * za * sg * (1.0 - sg)], axis=1).astype(BF16)
        y = yp_ref[...]
        dwg_ref[...] += _dot_t0(_gelu(y).astype(BF16), dzg)
        dy = _dot_t1(dzg, wg_ref[...]) * _gelu_grad(y)
        ub = u_ref[...]
        uf = ub.astype(F32)
        dd_ref[...] += jnp.sum(dy * uf, axis=0, keepdims=True)
        dyb = dy.astype(BF16)
        xrv = xr_ref[...]
        xiv = xi_ref[...]
        dcr_ref[...] += _dot_t0(xrv.astype(BF16), dyb)
        dci_ref[...] += _dot_t0(xiv.astype(BF16), dyb)

        rows = lax.broadcasted_iota(jnp.int32, (t, gp), 0)
        sub = rows % SUBLANES
        gr = _dot_t1(dyb, cr_ref[...])
        gi = _dot_t1(dyb, ci_ref[...])
        for k in range(GROUP_LOG):
            sh = 1 << k
            pr = pw_ref[2 * k:2 * k + 1, :]
            pi = pw_ref[2 * k + 1:2 * k + 2, :]
            keep = sub < SUBLANES - sh
            sr = jnp.where(keep, pltpu.roll(gr, t - sh, 0), 0.0)
            si = jnp.where(keep, pltpu.roll(gi, t - sh, 0), 0.0)
            gr, gi = gr + pr * sr + pi * si, gi + pr * si - pi * sr
        gr_scr[...] = gr
        gi_scr[...] = gi
        grow = lax.broadcasted_iota(jnp.int32, (ng, gp), 0)
        cr = carry[0:1, :]
        ci = carry[1:2, :]
        a8r = pw_ref[2 * GROUP_LOG:2 * GROUP_LOG + 1, :]
        a8i = pw_ref[2 * GROUP_LOG + 1:2 * GROUP_LOG + 2, :]
        tail = grow == ng - 1
        for g in range(ng):
            st[g:g + 1, :] = gr_scr[g * SUBLANES:g * SUBLANES + 1, :]
            st[ng + g:ng + g + 1, :] = gi_scr[g * SUBLANES:g * SUBLANES + 1, :]
        sr_ = st[0:ng, :] + jnp.where(tail, a8r * cr + a8i * ci, 0.0)
        si_ = st[ng:2 * ng, :] + jnp.where(tail, a8r * ci - a8i * cr, 0.0)
        for k in range(glog):
            sh = 1 << k
            pr = pw_ref[2 * (GROUP_LOG + k):2 * (GROUP_LOG + k) + 1, :]
            pi = pw_ref[2 * (GROUP_LOG + k) + 1:2 * (GROUP_LOG + k) + 2, :]
            keep = grow < ng - sh
            tr = jnp.where(keep, pltpu.roll(sr_, ng - sh, 0), 0.0)
            ti = jnp.where(keep, pltpu.roll(si_, ng - sh, 0), 0.0)
            sr_, si_ = sr_ + pr * tr + pi * ti, si_ + pr * ti - pi * tr
        head = grow == 0
        carry[0:1, :] = jnp.sum(jnp.where(head, sr_, 0.0), axis=0, keepdims=True)
        carry[1:2, :] = jnp.sum(jnp.where(head, si_, 0.0), axis=0, keepdims=True)
        st[0:ng, :] = jnp.where(tail, cr, pltpu.roll(sr_, ng - 1, 0))
        st[ng:2 * ng, :] = jnp.where(tail, ci, pltpu.roll(si_, ng - 1, 0))
        q8r = q8_ref[0:SUBLANES, :]
        q8i = q8_ref[SUBLANES:2 * SUBLANES, :]
        for g in range(ng):
            grp = slice(g * SUBLANES, (g + 1) * SUBLANES)
            pr = st[g:g + 1, :]
            pi = st[ng + g:ng + g + 1, :]
            gr_scr[grp, :] = gr_scr[grp, :] + q8r * pr - q8i * pi
            gi_scr[grp, :] = gi_scr[grp, :] + q8r * pi + q8i * pr
        gr = gr_scr[...]
        gi = gi_scr[...]
        first = rows == 0

        live = jnp.where(tile > 0, 1.0, 0.0)
        xpr = jnp.where(first, hr_ref[7:8, :] * live, pltpu.roll(xrv, 1, 0))
        xpi = jnp.where(first, hi_ref[7:8, :] * live, pltpu.roll(xiv, 1, 0))
        da_ref[0:1, :] += jnp.sum(gr * xpr + gi * xpi, axis=0, keepdims=True)
        da_ref[1:2, :] += jnp.sum(gi * xpr - gr * xpi, axis=0, keepdims=True)

        grb = gr.astype(BF16)
        gib = gi.astype(BF16)
        dbr_ref[...] += _dot_t0(ub, grb)
        dbi_ref[...] += _dot_t0(ub, gib)
        du_ref[...] = (_dot_t1(grb, br_ref[...]) + _dot_t1(gib, bi_ref[...]) + dy * d_ref[...]).astype(BF16)

    const = lambda shape: pl.BlockSpec(shape, lambda b, i: (0, 0))
    row = lambda w: pl.BlockSpec((t, w), lambda b, i: (b * nt + nt - 1 - i, 0))
    halo = pl.BlockSpec((8, gp), lambda b, i: (jnp.maximum((b * nt + nt - 1 - i) * tb - 1, 0), 0))
    extra, extra_specs, dwg_shape, aliases = _slab_out(nl, l, (ds5, 2 * ds5), dwg_buf, 16)
    aliases = {k: 7 for k in aliases}
    return pl.pallas_call(
        body, grid=(bl, nt),
        in_specs=[row(ds5), row(ds5), row(2 * ds5), row(gp), row(gp), halo, halo, row(ds5),
                  pl.BlockSpec((None, ds5, 2 * ds5), lambda b, i: (l, 0, 0)),
                  const((gp, ds5)), const((gp, ds5)), const((ds5, gp)), const((ds5, gp)),
                  const((2 * (GROUP_LOG + glog), gp)), const((2 * SUBLANES, gp)), const((1, ds5)), *extra_specs],
        out_specs=[row(ds5), const((1, ds5)), const((gp, ds5)), const((gp, ds5)),
                   const((ds5, gp)), const((ds5, gp)), const((2, gp)),
                   pl.BlockSpec((None, ds5, 2 * ds5), lambda b, i: (l, 0, 0))],
        out_shape=[_sds((n, ds5), BF16), _sds((1, ds5), F32), _sds((gp, ds5), F32),
                   _sds((gp, ds5), F32), _sds((ds5, gp), F32), _sds((ds5, gp), F32), _sds((2, gp), F32), dwg_shape],
        input_output_aliases=aliases,
        scratch_shapes=[pltpu.VMEM((2, gp), F32), pltpu.VMEM((t, gp), F32), pltpu.VMEM((t, gp), F32),
                        pltpu.VMEM((2 * ng, gp), F32)],
        compiler_params=_cp("arbitrary", "arbitrary"), name=name)(
            ds, yp, zg, xr, xi, xr, xi, proj, wglu, sp["cblk_r"], sp["cblk_in"],
            sp["bblk_r"], sp["bblk_i"], sp["pw"], sp["q8"], sp["d"], *extra)


def _head_norm(x, first):
    x2 = x * x
    sa = jnp.sum(jnp.where(first, x2, 0.0), axis=-1, keepdims=True)
    sb = jnp.sum(jnp.where(first, 0.0, x2), axis=-1, keepdims=True)
    r = jnp.where(first, lax.rsqrt(sa * (1.0 / HEAD_DIM) + EPS), lax.rsqrt(sb * (1.0 / HEAD_DIM) + EPS))
    return x * r, r


def _attn_specs(bl, s, datt, qoff):
    nq = s // ATT_TQ
    nb = datt // LANES
    col = lambda blk: (lambda b, h, q: (b * nq + q, qoff + blk * nb + h))
    win = lambda blk, j: (lambda b, h, q: (b * nq + jnp.maximum(q - 2 + j, 0), qoff + blk * nb + h))
    tile = lambda f: pl.BlockSpec((ATT_TQ, LANES), f)
    qs = tile(col(0))
    ks = [tile(win(1, j)) for j in range(3)]
    vs = [tile(win(2, j)) for j in range(3)]
    return nq, nb, qs, ks, vs


ATT_ROWS = 128


def _attn_keys(k_refs, gk_ref):
    lane = lax.broadcasted_iota(jnp.int32, (1, LANES), 1)
    first = lane < HEAD_DIM
    kc = jnp.concatenate([r[...] for r in k_refs], axis=0).astype(F32)
    kh, _ = _head_norm(kc, first)
    return first, (kh * gk_ref[...]).astype(BF16)


def _attn_probs(q_ref, rows, first, kn, gq_ref, bias_ref):
    qt = pl.program_id(2)
    qh, rq = _head_norm(q_ref[rows, :].astype(F32), first)
    qn = qh * gq_ref[...]
    kpos = (qt - 2) * ATT_TQ + lax.broadcasted_iota(jnp.int32, (1, 3 * ATT_TQ), 1)
    valid = kpos >= 0
    scale = HEAD_DIM ** -0.5
    masks = (first, jnp.logical_not(first))
    qas, ps = [], []
    for hh in range(2):
        qa = jnp.where(masks[hh], qn, 0.0).astype(BF16)
        sc = _dot_t1(qa, kn) * scale + bias_ref[hh, rows, :]
        sc = jnp.where(valid, sc, NEG)
        e = jnp.exp(sc - jnp.max(sc, axis=-1, keepdims=True))
        ps.append(e / jnp.sum(e, axis=-1, keepdims=True))
        qas.append(qa)
    return masks, qh, rq, qas, ps


def _attn_fwd(proj, gq2, gk2, bias, *, bl, s, datt, qoff, name, comm=None):
    n = bl * s
    nq, nb, qs, ks, vs = _attn_specs(bl, s, datt, qoff)
    nc = 0 if comm is None else len(comm[0])

    def body(q_ref, k0, k1, k2, v0, v1, v2, gq_ref, gk_ref, bias_ref, *rest):
        o_ref = rest[nc]
        if comm is not None:
            finish = _hosted_gather_steps(comm, rest[nc + 1:2 * nc + 1], rest[-2:],
                                          (pl.program_id(0) * nb + pl.program_id(1)) * nq + pl.program_id(2),
                                          bl * nb * nq)
        first, kn = _attn_keys((k0, k1, k2), gk_ref)
        vc = jnp.concatenate([v0[...], v1[...], v2[...]], axis=0)
        for r0 in range(0, ATT_TQ, ATT_ROWS):
            rows = slice(r0, r0 + ATT_ROWS)
            _, _, _, _, ps = _attn_probs(q_ref, rows, first, kn, gq_ref, bias_ref)
            o0 = _dot(ps[0].astype(BF16), vc)
            o1 = _dot(ps[1].astype(BF16), vc)
            o_ref[rows, :] = jnp.where(first, o0, o1).astype(BF16)
        if comm is not None:
            finish()

    gs = pl.BlockSpec((1, LANES), lambda b, h, q: (0, 0))
    c_in, c_ispec, c_ospec, c_oshape, c_scr, aliases = _hosted_gather(comm, 10, 1)
    res = pl.pallas_call(
        body, grid=(bl, nb, nq),
        in_specs=[qs, *ks, *vs, gs, gs, pl.BlockSpec((2, ATT_TQ, 3 * ATT_TQ), lambda b, h, q: (h, 0, 0))] + c_ispec,
        out_specs=[pl.BlockSpec((ATT_TQ, LANES), lambda b, h, q: (b * nq + q, h))] + c_ospec,
        out_shape=[_sds((n, datt), BF16)] + c_oshape,
        input_output_aliases=aliases, scratch_shapes=c_scr,
        compiler_params=_cp("arbitrary", "arbitrary", "arbitrary"), name=name)(
            proj, proj, proj, proj, proj, proj, proj, gq2, gk2, bias, *c_in)
    return res[0], res[1:]


def _attn_bwd(do, proj, gq2, gk2, bias, *, bl, s, datt, qoff, name, comm=None):
    n = bl * s
    nq, nb, qs, ks, vs = _attn_specs(bl, s, datt, qoff)
    srows = s + 2 * ATT_TQ
    scale = HEAD_DIM ** -0.5
    nc = 0 if comm is None else len(comm[0])

    def body(do_ref, q_ref, k0, k1, k2, v0, v1, v2, gq_ref, gk_ref, bias_ref, *rest):
        dq_ref, dk_ref, dv_ref, db_ref, dgq_ref = rest[nc:nc + 5]
        qt = pl.program_id(2)
        if comm is not None:
            start, wait = _exchange_ops(rest[:nc], rest[nc + 5:2 * nc + 5], rest[-2], rest[-1], comm[1], comm[2])
            step = (pl.program_id(0) * nb + pl.program_id(1)) * nq + qt

            @pl.when(step == 0)
            def _():
                start()

        @pl.when(qt == 0)
        def _():
            dk_ref[...] = jnp.zeros_like(dk_ref)
            dv_ref[...] = jnp.zeros_like(dv_ref)
            db_ref[...] = jnp.zeros_like(db_ref)
            dgq_ref[...] = jnp.zeros_like(dgq_ref)

        first, kn = _attn_keys((k0, k1, k2), gk_ref)
        vc = jnp.concatenate([v0[...], v1[...], v2[...]], axis=0)
        dkn = jnp.zeros((3 * ATT_TQ, LANES), F32)
        dv = jnp.zeros((3 * ATT_TQ, LANES), F32)
        dgq = jnp.zeros((1, LANES), F32)
        for r0 in range(0, ATT_TQ, ATT_ROWS):
            rows = slice(r0, r0 + ATT_ROWS)
            masks, qh, rq, qas, ps = _attn_probs(q_ref, rows, first, kn, gq_ref, bias_ref)
            dov = do_ref[rows, :]
            dqn = jnp.zeros((ATT_ROWS, LANES), F32)
            for hh in range(2):
                doa = jnp.where(masks[hh], dov, jnp.zeros_like(dov))
                p = ps[hh]
                dp = _dot_t1(doa, vc)
                dsm = p * (dp - jnp.sum(dp * p, axis=-1, keepdims=True))
                db_ref[hh, rows, :] += dsm
                dsc = (dsm * scale).astype(BF16)
                dqn = dqn + _dot(dsc, jnp.where(masks[hh], kn, jnp.zeros_like(kn)))
                dkn = dkn + _dot_t0(dsc, qas[hh])
                dv = dv + _dot_t0(p.astype(BF16), doa)
            dgq = dgq + jnp.sum(dqn * qh, axis=0, keepdims=True)
            dqh = dqn * gq_ref[...]
            t = dqh * qh
            ma = jnp.sum(jnp.where(first, t, 0.0), axis=-1, keepdims=True) * (1.0 / HEAD_DIM)
            mb = jnp.sum(jnp.where(first, 0.0, t), axis=-1, keepdims=True) * (1.0 / HEAD_DIM)
            dq_ref[rows, :] = (rq * (dqh - qh * jnp.where(first, ma, mb))).astype(BF16)
        start = pl.multiple_of(qt * ATT_TQ, ATT_TQ)
        dk_ref[pl.ds(start, 3 * ATT_TQ), :] += dkn
        dv_ref[pl.ds(start, 3 * ATT_TQ), :] += dv
        dgq_ref[...] += dgq
        if comm is not None:
            @pl.when(step == bl * nb * nq - 1)
            def _():
                wait()

    gs = pl.BlockSpec((1, LANES), lambda b, h, q: (0, 0))
    acc = pl.BlockSpec((None, srows, LANES), lambda b, h, q: (b, 0, h))
    comm_in = [] if comm is None else list(comm[0])
    comm_out = [] if comm is None else _parts_shapes(comm[0], comm[1])
    comm_scr = [] if comm is None else [pltpu.SemaphoreType.DMA((3 * nc,)), pltpu.SemaphoreType.DMA((3 * nc,))]
    return pl.pallas_call(
        body, grid=(bl, nb, nq),
        in_specs=[pl.BlockSpec((ATT_TQ, LANES), lambda b, h, q: (b * nq + q, h)), qs, *ks, *vs, gs, gs,
                  pl.BlockSpec((2, ATT_TQ, 3 * ATT_TQ), lambda b, h, q: (h, 0, 0))] + [_ANY] * nc,
        out_specs=[pl.BlockSpec((ATT_TQ, LANES), lambda b, h, q: (b * nq + q, h)), acc, acc,
                   pl.BlockSpec((None, 2, ATT_TQ, 3 * ATT_TQ), lambda b, h, q: (b, h, 0, 0)),
                   pl.BlockSpec((None, None, 1, LANES), lambda b, h, q: (b, h, 0, 0))] + [_ANY] * nc,
        out_shape=[_sds((n, datt), BF16), _sds((bl, srows, datt), F32), _sds((bl, srows, datt), F32),
                   _sds((bl, 2 * nb, ATT_TQ, 3 * ATT_TQ), F32), _sds((bl, nb, 1, LANES), F32)] + comm_out,
        scratch_shapes=comm_scr,
        compiler_params=_cp("arbitrary", "arbitrary", "arbitrary"), name=name)(
            do, proj, proj, proj, proj, proj, proj, proj, gq2, gk2, bias, *comm_in)


def _attn_kv_bwd(dkn, dv, proj, gk2, *, bl, s, datt, tm, koff, name):
    n = bl * s
    ns = s // tm
    off = 2 * ATT_TQ // tm
    nb = datt // LANES

    def body(dkn_ref, dv_ref, k_ref, gk_ref, dk_ref, dvo_ref, dgk_ref):
        lane = lax.broadcasted_iota(jnp.int32, (1, LANES), 1)
        first = lane < HEAD_DIM

        @pl.when((pl.program_id(0) == 0) & (pl.program_id(1) == 0) & (pl.program_id(2) == 0))
        def _():
            dgk_ref[...] = jnp.zeros_like(dgk_ref)

        dvo_ref[...] = dv_ref[...].astype(BF16)
        kh, rk = _head_norm(k_ref[...].astype(F32), first)
        dn = dkn_ref[...]
        dgk_ref[...] += jnp.sum(dn * kh, axis=0, keepdims=True)
        dh = dn * gk_ref[...]
        t = dh * kh
        ma = jnp.sum(jnp.where(first, t, 0.0), axis=-1, keepdims=True) * (1.0 / HEAD_DIM)
        mb = jnp.sum(jnp.where(first, 0.0, t), axis=-1, keepdims=True) * (1.0 / HEAD_DIM)
        dk_ref[...] = (rk * (dh - kh * jnp.where(first, ma, mb))).astype(BF16)

    accs = pl.BlockSpec((None, tm, LANES), lambda b, i, c: (b, i + off, c))
    outs = pl.BlockSpec((tm, LANES), lambda b, i, c: (b * ns + i, c))
    vec = pl.BlockSpec((1, LANES), lambda b, i, c: (0, 0))
    return pl.pallas_call(
        body, grid=(bl, ns, nb),
        in_specs=[accs, accs, pl.BlockSpec((tm, LANES), lambda b, i, c: (b * ns + i, koff + c)), vec],
        out_specs=[outs, outs, vec],
        out_shape=[_sds((n, datt), BF16), _sds((n, datt), BF16), _sds((1, LANES), F32)],
        compiler_params=_cp("arbitrary", "arbitrary", "arbitrary"), name=name)(dkn, dv, proj, gk2)


def _conv_fwd(proj, wdw, bdw, lng, lnb, *, bl, s, t, acol, name):
    n = bl * s
    dc = wdw.shape[1]
    nt = s // t
    hb = t // HALO

    def body(za_ref, zg_ref, ha_ref, hgt_ref, w_ref, b_ref, g_ref, be_ref, hg_ref, hc_ref, o_ref, ext):
        i = pl.program_id(1)
        hg = za_ref[...].astype(F32) * _sigmoid(zg_ref[...].astype(F32))
        live = jnp.where(i > 0, 1.0, 0.0)
        ext[0:HALO, :] = ha_ref[...].astype(F32) * _sigmoid(hgt_ref[...].astype(F32)) * live
        ext[HALO:HALO + t, :] = hg
        hg_ref[...] = hg
        acc = jnp.zeros((t, dc), F32) + b_ref[...]
        for j in range(CONV_W):
            acc = acc + w_ref[j:j + 1, :] * ext[pl.ds(HALO - (CONV_W - 1) + j, t), :]
        hc_ref[...] = acc
        mu = jnp.mean(acc, axis=-1, keepdims=True)
        xc = acc - mu
        rs = lax.rsqrt(jnp.mean(xc * xc, axis=-1, keepdims=True) + EPS)
        ln = xc * rs * g_ref[...] + be_ref[...]
        o_ref[...] = (ln * _sigmoid(ln)).astype(BF16)

    vec = pl.BlockSpec((1, dc), lambda b, i: (0, 0))
    row = pl.BlockSpec((t, dc), lambda b, i: (b * nt + i, 0))
    tile = lambda c: pl.BlockSpec((t, dc), lambda b, i: (b * nt + i, c))
    halo = lambda c: pl.BlockSpec((HALO, dc), lambda b, i: (jnp.maximum((b * nt + i) * hb - 1, 0), c))
    return pl.pallas_call(
        body, grid=(bl, nt),
        in_specs=[tile(acol), tile(acol + 1), halo(acol), halo(acol + 1),
                  pl.BlockSpec((HALO, dc), lambda b, i: (0, 0)), vec, vec, vec],
        out_specs=[row, row, row],
        out_shape=[_sds((n, dc), F32), _sds((n, dc), F32), _sds((n, dc), BF16)],
        scratch_shapes=[pltpu.VMEM((HALO + t, dc), F32)],
        compiler_params=_cp("parallel", "arbitrary"), name=name)(proj, proj, proj, proj, wdw, bdw, lng, lnb)


def _conv_bwd_ln(dco, hc, lng, lnb, *, tm, name):
    n, dc = hc.shape

    def body(d_ref, hc_ref, g_ref, be_ref, dhc_ref, dg_ref, db_ref):
        @pl.when(pl.program_id(0) == 0)
        def _():
            dg_ref[...] = jnp.zeros_like(dg_ref)
            db_ref[...] = jnp.zeros_like(db_ref)

        hcv = hc_ref[...]
        mu = jnp.mean(hcv, axis=-1, keepdims=True)
        xc = hcv - mu
        rs = lax.rsqrt(jnp.mean(xc * xc, axis=-1, keepdims=True) + EPS)
        xh = xc * rs
        ln = xh * g_ref[...] + be_ref[...]
        sg = _sigmoid(ln)
        dln = d_ref[...].astype(F32) * (sg * (1.0 + ln * (1.0 - sg)))
        db_ref[...] += jnp.sum(dln, axis=0, keepdims=True)
        dg_ref[...] += jnp.sum(dln * xh, axis=0, keepdims=True)
        dxh = dln * g_ref[...]
        dhc_ref[...] = rs * (dxh - jnp.mean(dxh, axis=-1, keepdims=True)
                             - xh * jnp.mean(dxh * xh, axis=-1, keepdims=True))

    vec = pl.BlockSpec((1, dc), lambda i: (0, 0))
    row = pl.BlockSpec((tm, dc), lambda i: (i, 0))
    return pl.pallas_call(
        body, grid=(n // tm,), in_specs=[row, row, vec, vec], out_specs=[row, vec, vec],
        out_shape=[_sds((n, dc), F32), _sds((1, dc), F32), _sds((1, dc), F32)],
        compiler_params=_cp("arbitrary"), name=name)(dco, hc, lng, lnb)


def _conv_bwd_dw(dhc, hg, proj, wdw, *, bl, s, t, acol, name):
    n = bl * s
    dc = wdw.shape[1]
    nt = s // t
    hb = t // HALO
    lastblk = n // HALO - 1

    def body(d_ref, dn_ref, hg_ref, hp_ref, za_ref, zg_ref, w_ref, dz_ref, dw_ref, dbias_ref, extd, exth):
        b = pl.program_id(0)
        i = pl.program_id(1)

        @pl.when((b == 0) & (i == 0))
        def _():
            dw_ref[...] = jnp.zeros_like(dw_ref)
            dbias_ref[...] = jnp.zeros_like(dbias_ref)

        dv = d_ref[...]
        extd[0:t, :] = dv
        extd[t:t + HALO, :] = dn_ref[...] * jnp.where(i < nt - 1, 1.0, 0.0)
        exth[0:HALO, :] = hp_ref[...] * jnp.where(i > 0, 1.0, 0.0)
        exth[HALO:HALO + t, :] = hg_ref[...]
        dbias_ref[...] += jnp.sum(dv, axis=0, keepdims=True)
        dhg = jnp.zeros((t, dc), F32)
        for j in range(CONV_W):
            dhg = dhg + w_ref[j:j + 1, :] * extd[pl.ds(CONV_W - 1 - j, t), :]
            dw_ref[j:j + 1, :] += jnp.sum(dv * exth[pl.ds(HALO - (CONV_W - 1) + j, t), :], axis=0, keepdims=True)
        za = za_ref[...].astype(F32)
        sg = _sigmoid(zg_ref[...].astype(F32))
        dz_ref[...] = jnp.concatenate([dhg * sg, dhg * za * sg * (1.0 - sg)], axis=1).astype(BF16)

    row = pl.BlockSpec((t, dc), lambda b, i: (b * nt + i, 0))
    nxt = pl.BlockSpec((HALO, dc), lambda b, i: (jnp.minimum((b * nt + i + 1) * hb, lastblk), 0))
    prv = pl.BlockSpec((HALO, dc), lambda b, i: (jnp.maximum((b * nt + i) * hb - 1, 0), 0))
    wsp = pl.BlockSpec((HALO, dc), lambda b, i: (0, 0))
    tile = lambda c: pl.BlockSpec((t, dc), lambda b, i: (b * nt + i, c))
    return pl.pallas_call(
        body, grid=(bl, nt),
        in_specs=[row, nxt, row, prv, tile(acol), tile(acol + 1), wsp],
        out_specs=[pl.BlockSpec((t, 2 * dc), lambda b, i: (b * nt + i, 0)), wsp,
                   pl.BlockSpec((1, dc), lambda b, i: (0, 0))],
        out_shape=[_sds((n, 2 * dc), BF16), _sds((HALO, dc), F32), _sds((1, dc), F32)],
        scratch_shapes=[pltpu.VMEM((t + HALO, dc), F32), pltpu.VMEM((HALO + t, dc), F32)],
        compiler_params=_cp("arbitrary", "arbitrary"), name=name)(dhc, dhc, hg, hg, proj, proj, wdw)


def _mix_out_fwd(x, brs, gl, bg, wbs, wout, l, *, tm, name):
    n, d = x.shape

    def body(x_ref, s_ref, a_ref, c_ref, g0, g1, g2, bg_ref, ws, wa, wc, wo, o_ref):
        merged = jnp.zeros((tm, d), F32)
        for k, (br, gr, w) in enumerate(((s_ref, g0, ws), (a_ref, g1, wa), (c_ref, g2, wc))):
            gate = _sigmoid(gr[...].astype(F32) + bg_ref[:, k * d:(k + 1) * d])
            merged = merged + gate * _dot(br[...], w[...])
        o_ref[...] = x_ref[...] + _dot(merged.astype(BF16), wo[...])

    row = lambda w: pl.BlockSpec((tm, w), lambda i: (i, 0))
    wsp = lambda a: pl.BlockSpec((None,) + a.shape[1:], lambda i: (l, 0, 0))
    gls = [pl.BlockSpec((tm, d), functools.partial(lambda k, i: (i, k), k)) for k in range(3)]
    return pl.pallas_call(
        body, grid=(n // tm,),
        in_specs=[row(d), *[row(b.shape[1]) for b in brs], *gls, pl.BlockSpec(bg.shape, lambda i: (0, 0)),
                  *[wsp(w) for w in wbs], wsp(wout)],
        out_specs=row(d), out_shape=_sds((n, d), F32),
        compiler_params=_cp("parallel"), name=name)(x, *brs, gl, gl, gl, bg, *wbs, wout)


def _mix_out_bwd(dx, brs, gl, bg, wbs, wout, l, nl, bufs, *, tm, name):
    n, d = dx.shape
    widths = [b.shape[1] for b in brs]

    def body(dx_ref, s_ref, a_ref, c_ref, g0, g1, g2, bg_ref, ws, wa, wc, wo, *rest):
        ds_ref, da_ref, dc_ref, dgl_ref, dbg_ref, dws, dwa, dwc, dwo = rest[-9:]

        @pl.when(pl.program_id(0) == 0)
        def _():
            for r in (dbg_ref, dws, dwa, dwc, dwo):
                r[...] = jnp.zeros_like(r)

        dxb = dx_ref[...].astype(BF16)
        dm = _dot_t1(dxb, wo[...])
        merged = jnp.zeros((tm, d), F32)
        for k, (br, gr, w, dbr, dw) in enumerate(((s_ref, g0, ws, ds_ref, dws), (a_ref, g1, wa, da_ref, dwa),
                                                   (c_ref, g2, wc, dc_ref, dwc))):
            gate = _sigmoid(gr[...].astype(F32) + bg_ref[:, k * d:(k + 1) * d])
            brv = br[...]
            wv = w[...]
            y = _dot(brv, wv)
            merged = merged + gate * y
            dyb = (dm * gate).astype(BF16)
            dbr[...] = _dot_t1(dyb, wv).astype(BF16)
            dw[...] += _dot_t0(brv, dyb)
            dgl = dm * y * gate * (1.0 - gate)
            dgl_ref[:, k * d:(k + 1) * d] = dgl.astype(BF16)
            dbg_ref[:, k * d:(k + 1) * d] += jnp.sum(dgl, axis=0, keepdims=True)
        dwo[...] += _dot_t0(merged.astype(BF16), dxb)

    row = lambda w: pl.BlockSpec((tm, w), lambda i: (i, 0))
    wsp = lambda shape: pl.BlockSpec((None,) + tuple(shape), lambda i: (l, 0, 0))
    gls = [pl.BlockSpec((tm, d), functools.partial(lambda k, i: (i, k), k)) for k in range(3)]
    slabs = [(w, d) for w in widths] + [(d, d)]
    n_in = 12
    extra = [] if bufs is None else list(bufs)
    aliases = {} if bufs is None else {n_in + k: 5 + k for k in range(4)}
    return pl.pallas_call(
        body, grid=(n // tm,),
        in_specs=[row(d), *[row(w) for w in widths], *gls, pl.BlockSpec(bg.shape, lambda i: (0, 0)),
                  *[wsp(w.shape[1:]) for w in wbs], wsp(wout.shape[1:]), *[_ANY for _ in extra]],
        out_specs=[*[row(w) for w in widths], row(3 * d), pl.BlockSpec((1, 3 * d), lambda i: (0, 0)),
                   *[wsp(sh) for sh in slabs]],
        out_shape=[*[_sds((n, w), BF16) for w in widths], _sds((n, 3 * d), BF16), _sds((1, 3 * d), F32),
                   *[_sds((nl,) + sh, F32) for sh in slabs]],
        input_output_aliases=aliases,
        compiler_params=_cp("arbitrary"), name=name)(dx, *brs, gl, gl, gl, bg, *wbs, wout, *extra)


def _adamw(w, g, m, v, *, name):
    r, c = w.shape
    tm = _tile(r, 256)
    c1 = 1.0 - ADAM_B1 ** ADAM_STEP
    c2 = 1.0 - ADAM_B2 ** ADAM_STEP

    def body(w_ref, g_ref, m_ref, v_ref, d_ref, nm_ref, nv_ref):
        gv = g_ref[...]
        mn = ADAM_B1 * m_ref[...] + (1.0 - ADAM_B1) * gv
        vn = ADAM_B2 * v_ref[...] + (1.0 - ADAM_B2) * (gv * gv)
        nm_ref[...] = mn
        nv_ref[...] = vn
        d_ref[...] = -ADAM_LR * ((mn / c1) / (jnp.sqrt(vn / c2) + ADAM_EPS) + ADAM_WD * w_ref[...])

    blk = pl.BlockSpec((tm, c), lambda i: (i, 0))
    return pl.pallas_call(
        body, grid=(r // tm,), in_specs=[blk] * 4, out_specs=[blk] * 3,
        out_shape=[_sds((r, c), F32)] * 3, compiler_params=_cp("parallel"), name=name)(w, g, m, v)


def _add_sibling(g, recv, lyr, c_idx, *, name):
    _, a, b = g.shape
    ta = _tile(a, 256)

    def body(c_ref, g_ref, r_ref, o_ref):
        @pl.when(c_ref[0] == lyr)
        def _():
            o_ref[...] = (g_ref[...] + r_ref[...]).astype(BF16)

    row = lambda i, cr: jnp.where(cr[0] == lyr, i, 0)
    return pl.pallas_call(
        body,
        grid_spec=pltpu.PrefetchScalarGridSpec(
            num_scalar_prefetch=1, grid=(a // ta,),
            in_specs=[pl.BlockSpec((None, ta, b), lambda i, cr: (lyr, row(i, cr), 0)),
                      pl.BlockSpec((ta, b), lambda i, cr: (row(i, cr), 0))],
            out_specs=pl.BlockSpec((ta, b), lambda i, cr: (row(i, cr), 0))),
        out_shape=_sds((a, b), BF16), compiler_params=_cp("arbitrary"), name=name)(c_idx, g, recv)


def _add_chips(rsum, parts, axis, s_idx, c_idx, lyr, buf, *, name):
    _, a, b = parts.shape
    ta = _tile(a, 256)
    na = a // ta

    def body(s_ref, c_ref, own_ref, p0, p1, p2, p3, *rest):
        o_ref = rest[-1]

        @pl.when(c_ref[0] == lyr)
        def _():
            own = own_ref[...].astype(F32)
            terms = [jnp.where(s_ref[0] == s, own, p[...].astype(F32)) for s, p in enumerate((p0, p1, p2, p3))]
            o_ref[...] = ((terms[0] + terms[1]) + terms[2]) + terms[3]

    row = lambda i, cr: jnp.where(cr[0] == lyr, i, 0)
    own_spec = (pl.BlockSpec((ta, b), lambda i, sr, cr: (sr[0] * na + row(i, cr), 0)) if axis == 1
                else pl.BlockSpec((ta, b), lambda i, sr, cr: (row(i, cr), sr[0])))
    part_spec = lambda s: pl.BlockSpec((None, ta, b),
                                       lambda i, sr, cr: (jnp.where(sr[0] == s, s ^ 1, s), row(i, cr), 0))
    extra, extra_specs, out_shape, aliases = _slab_out(2, lyr, (a, b), buf, 7)
    return pl.pallas_call(
        body,
        grid_spec=pltpu.PrefetchScalarGridSpec(
            num_scalar_prefetch=2, grid=(na,),
            in_specs=[own_spec] + [part_spec(s) for s in range(N_CHIPS)] + extra_specs,
            out_specs=pl.BlockSpec((None, ta, b), lambda i, sr, cr: (lyr, row(i, cr), 0))),
        out_shape=out_shape, input_output_aliases=aliases, compiler_params=_cp("arbitrary"), name=name)(
            s_idx, c_idx, rsum, parts, parts, parts, parts, *extra)


def _place_shard(wloc, axis, s_idx, *, name):
    nl, a, b = wloc.shape
    ta = _tile(a, 256)
    na = a // ta
    full = (nl, a * N_CHIPS, b) if axis == 1 else (nl, a, b * N_CHIPS)

    def body(sc_ref, w_ref, o_ref):
        o_ref[...] = w_ref[...].astype(BF16)

    out_spec = (pl.BlockSpec((None, ta, b), lambda l, i, sc: (l, sc[0] * na + i, 0)) if axis == 1
                else pl.BlockSpec((None, ta, b), lambda l, i, sc: (l, i, sc[0])))
    return pl.pallas_call(
        body,
        grid_spec=pltpu.PrefetchScalarGridSpec(
            num_scalar_prefetch=1, grid=(nl, na),
            in_specs=[pl.BlockSpec((None, ta, b), lambda l, i, sc: (l, i, 0))], out_specs=out_spec),
        out_shape=_sds(full, BF16), compiler_params=_cp("parallel", "parallel"), name=name)(s_idx, wloc)


def _blockdiag(w):
    g, r, c = w.shape
    eye = jnp.eye(g, dtype=w.dtype)
    return (w[:, :, None, :] * eye[:, None, :, None]).reshape(g * r, g * c)


def _s5_prep(lre, lim, log_dt, b_re, b_im, c_re, c_im, d_skip):
    lr = jnp.minimum(lre, -1e-4)
    li = lim
    dt = jnp.exp(log_dt)[:, None]
    mag = jnp.exp(lr * dt)
    ar = mag * jnp.cos(li * dt)
    ai = mag * jnp.sin(li * dt)
    den = lr * lr + li * li
    coef_r = ((ar - 1.0) * lr + ai * li) / den
    coef_i = (ai * lr - (ar - 1.0) * li) / den
    bbar_r = coef_r[..., None] * b_re - coef_i[..., None] * b_im
    bbar_i = coef_r[..., None] * b_im + coef_i[..., None] * b_re
    a = jnp.stack([ar.reshape(-1), ai.reshape(-1)])
    return dict(
        a=a,
        bblk_r=_blockdiag(bbar_r.transpose(0, 2, 1)), bblk_i=_blockdiag(bbar_i.transpose(0, 2, 1)),
        cblk_r=_blockdiag(c_re.transpose(0, 2, 1)), cblk_in=_blockdiag(-c_im.transpose(0, 2, 1)),
        d=d_skip.reshape(1, -1))


def _s5_powers(a, nlog):
    ar, ai = a[0], a[1]
    pr, pi = ar, ai
    rows = []
    for _ in range(nlog):
        rows += [pr, pi]
        pr, pi = pr * pr - pi * pi, 2.0 * pr * pi
    qr, qi = [ar], [ai]
    for _ in range(SUBLANES - 1):
        qr, qi = qr + [qr[-1] * ar - qi[-1] * ai], qi + [qr[-1] * ai + qi[-1] * ar]
    p8 = jnp.stack(qr + qi)
    q8 = jnp.stack(qr[::-1] + [-v for v in qi[::-1]])
    return jnp.stack(rows), p8, q8


def _bias_table(rel_bias):
    h = rel_bias.shape[0]
    tq, tw = ATT_TQ, 3 * ATT_TQ
    n_hi = tw - 1 - MAX_REL + 1
    n_lo = tq + tw - 1 - n_hi - (2 * MAX_REL - 1)
    fr = jnp.concatenate([
        jnp.broadcast_to(rel_bias[:, 2 * MAX_REL:], (h, n_hi)),
        jnp.flip(rel_bias[:, 1:2 * MAX_REL], axis=1),
        jnp.broadcast_to(rel_bias[:, :1], (h, n_lo)),
        jnp.zeros((h, 1), rel_bias.dtype)], axis=1)
    ln = tq + tw
    flat = jnp.broadcast_to(fr[:, None, :], (h, tq, ln)).reshape(h, tq * ln)[:, :tq * (ln - 1)]
    tab = flat.reshape(h, tq, ln - 1)[:, :, tq - 1:tq - 1 + tw]
    qc = np.arange(tq)[:, None] // CHUNK + N_LEFT
    kc = np.arange(tw)[None, :] // CHUNK
    band = (kc <= qc) & (kc >= qc - N_LEFT)
    return jnp.where(jnp.asarray(band)[None], tab, NEG)


def _small_prep(w, l):
    g, p = w["s5_lambda_re"].shape[1:]
    b_shape, c_shape = (g, p, -1), (g, -1, p)
    sp = _s5_prep(w["s5_lambda_re"][l], w["s5_lambda_im"][l], w["s5_log_dt"][l], w["s5_b_re"][l].reshape(b_shape),
                  w["s5_b_im"][l].reshape(b_shape), w["s5_c_re"][l].reshape(c_shape), w["s5_c_im"][l].reshape(c_shape),
                  w["s5_d"][l])
    return sp, _bias_table(w["attn_rel_bias"][l])


_PREP_KEYS = ("s5_lambda_re", "s5_lambda_im", "s5_log_dt", "s5_b_re", "s5_b_im", "s5_c_re", "s5_c_im", "s5_d",
              "attn_rel_bias")
_BIG_KEYS = {"ffn1_w_up": 2, "ffn1_w_down": 1, "w_in": 2, "s5_w_glu": 2, "w_br_s5": 2, "w_br_attn": 2,
             "w_br_conv": 2, "w_out": 1, "ffn2_w_up": 2, "ffn2_w_down": 1}
_SMALL_KEYS = ("ffn1_norm", "mix_norm", "b_gate", "s5_lambda_re", "s5_lambda_im", "s5_log_dt", "s5_b_re", "s5_b_im",
               "s5_c_re", "s5_c_im", "s5_d", "attn_q_gain", "attn_k_gain", "attn_rel_bias", "conv_w_dw", "conv_b_dw",
               "conv_ln_g", "conv_ln_b", "ffn2_norm")
_WEIGHTS = ("ffn1_norm", "ffn1_w_up", "ffn1_w_down", "mix_norm", "w_in", "b_gate", "s5_lambda_re", "s5_lambda_im",
            "s5_log_dt", "s5_b_re", "s5_b_im", "s5_c_re", "s5_c_im", "s5_d", "s5_w_glu", "w_br_s5", "attn_q_gain",
            "attn_k_gain", "attn_rel_bias", "w_br_attn", "conv_w_dw", "conv_b_dw", "conv_ln_g", "conv_ln_b",
            "w_br_conv", "w_out", "ffn2_norm", "ffn2_w_up", "ffn2_w_down")


def _local_step(x3, target3, w, rs=None, gather=None):
    w = dict(w)
    bl, s, d = x3.shape
    nl = w["ffn1_norm"].shape[0]
    dff = w["ffn1_w_down"].shape[1]
    ds5 = w["s5_d"].shape[1]
    datt = w["w_br_attn"].shape[1]
    dc = w["conv_b_dw"].shape[1]
    n = bl * s
    x = x3.reshape(n, d)
    target = target3.reshape(n, d)
    tm = _tile(n, 512)
    tml = _tile(n, 1024)
    tmix = _tile(n, 256)
    ts5 = 256
    tconv = _tile(s, 512)
    tff = dff // 2
    ma = ds5 + 3 * datt + 2 * dc
    tna = ma // 3
    assert (3 * d) % tna == 0 and dff % 2 == 0
    qoff = ds5 // LANES
    koff = (ds5 + datt) // LANES
    acol = (ds5 + 3 * datt) // dc
    wbs = lambda: (w["w_br_s5"], w["w_br_attn"], w["w_br_conv"])

    def host(tag, l):
        if gather is None or l != 0:
            return None
        keys, kaxes, lyr = gather[tag]
        return [w[k] for k in keys], kaxes, lyr

    def hosted(tag, l, arrays):
        if gather is not None and l == 0:
            w.update(zip(gather[tag][0], arrays))

    saved = []
    for l in range(nl):
        (sp, bias), prep_vjp = jax.vjp(lambda ww: _small_prep(ww, l), {k: w[k] for k in _PREP_KEYS})
        spb = dict(sp)
        spb["pw"], spb["p8"], spb["q8"] = _s5_powers(lax.stop_gradient(sp["a"]), int(math.log2(ts5)))
        for k in ("bblk_r", "bblk_i", "cblk_r", "cblk_in"):
            spb[k] = sp[k].astype(BF16)
        g1 = w["ffn1_norm"][l][None]
        g2 = w["ffn2_norm"][l][None]
        gm = w["mix_norm"][l][None]
        gq2 = jnp.tile(w["attn_q_gain"][l], 2)[None]
        gk2 = jnp.tile(w["attn_k_gain"][l], 2)[None]
        wdw = jnp.pad(w["conv_w_dw"][l], ((0, HALO - CONV_W), (0, 0)))
        bdw, lng, lnb = w["conv_b_dw"][l][None], w["conv_ln_g"][l][None], w["conv_ln_b"][l][None]
        bg = w["b_gate"][l][None]

        x0 = x
        h1, ab1, *got = _norm_mm(x0, g1, w["ffn1_w_up"], l, tm=tml, tn=tff, ntiles=4, pieces=2, transposed=False,
                                 name=f"ffn1_up_{l}", comm=host("ffn1_up", l))
        hosted("ffn1_up", l, got)
        x1 = _ffn_down(ab1, w["ffn1_w_down"], l, x0, tm=tm, tk=tff, name=f"ffn1_down_{l}")
        h2, pa, *got = _norm_mm(x1, gm, w["w_in"], l, tm=tml, tn=tna, ntiles=3, pieces=1, transposed=True,
                                name=f"win_a_{l}", comm=host("win_a", l))
        hosted("win_a", l, got)
        pa = pa[0]
        gl = _mm_t(h2, w["w_in"], l, tm=tml, tn=tna, off=3, ntiles=3 * d // tna, name=f"win_g_{l}")
        xr, xi, yp, zg, s5o = _s5_fwd(pa, spb, w["s5_w_glu"], l, bl=bl, s=s, t=ts5, name=f"s5_fwd_{l}")
        atto, got = _attn_fwd(pa, gq2, gk2, bias, bl=bl, s=s, datt=datt, qoff=qoff, name=f"attn_fwd_{l}",
                              comm=host("attn_fwd", l))
        hosted("attn_fwd", l, got)
        hg, hc, convo = _conv_fwd(pa, wdw, bdw, lng, lnb, bl=bl, s=s, t=tconv, acol=acol, name=f"conv_fwd_{l}")
        brs = (s5o, atto, convo)
        x2 = _mix_out_fwd(x1, brs, gl, bg, wbs(), w["w_out"], l, tm=tmix, name=f"mix_fwd_{l}")
        h3, ab2, *got = _norm_mm(x2, g2, w["ffn2_w_up"], l, tm=tml, tn=tff, ntiles=4, pieces=2, transposed=False,
                                 name=f"ffn2_up_{l}", comm=host("ffn2_up", l))
        hosted("ffn2_up", l, got)
        x = _ffn_down(ab2, w["ffn2_w_down"], l, x2, tm=tm, tk=tff, name=f"ffn2_down_{l}")
        saved.append(dict(spb=spb, bias=bias, prep_vjp=prep_vjp, g1=g1, g2=g2, gm=gm, gq2=gq2, gk2=gk2,
                          wdw=wdw, lng=lng, lnb=lnb, bg=bg, x0=x0, h1=h1, ab1=ab1, x1=x1, h2=h2, pa=pa, gl=gl,
                          xr=xr, xi=xi, yp=yp, zg=zg, hg=hg, hc=hc, brs=brs, x2=x2, h3=h3, ab2=ab2))

    dx, lsum = _loss_grad(x, target, tm=tm, name="loss")
    loss_part = 0.5 * jnp.sum(lsum) / d

    big = {k: None for k in _BIG_KEYS}
    small = {k: [None] * nl for k in _SMALL_KEYS}
    hooks = {"pending": None, "early": None}
    assert rs is None or nl == 2
    for l in reversed(range(nl)):
        sv = saved[l]

        def ffn_bwd(dx, xin, h, ab, g, tag):
            wu, wd = w[tag + "_w_up"], w[tag + "_w_down"]
            comm = hooks["early"] if (tag == "ffn1" and l == 0) else None
            dab, big[tag + "_w_down"], *parts = _ffn_dact(dx, wd, l, ab, nl, big[tag + "_w_down"], tm=tm, tk=tff,
                                                          name=f"{tag}_dact_{l}", comm=comm)
            if comm is not None:
                rs.parts_early = parts
            big[tag + "_w_up"] = _mm_tn(h[None], dab, l, nl, big[tag + "_w_up"], ta=d, tb=tff, tk=tml,
                                        name=f"{tag}_dwu_{l}")
            comm = None
            if rs is not None and l == 0:
                keys = list(_BIG_KEYS) if tag == "ffn2" else rs.late
                comm = ([big[k] for k in keys], 1 if tag == "ffn2" else 0)
            dxo, dg, *recv = _ffn_dx(dab, wu, l, xin, g, dx, tm=tml, tk=tff, name=f"{tag}_dx_{l}", comm=comm)
            small[tag + "_norm"][l] = dg[0]
            if comm is not None and tag == "ffn2":
                hooks["pending"] = rs.sums(big, recv, 1, keys)
            elif comm is not None:
                rs.recv_late = recv
            return dxo

        dx = ffn_bwd(dx, sv["x2"], sv["h3"], sv["ab2"], sv["g2"], "ffn2")

        mix_keys = ("w_br_s5", "w_br_attn", "w_br_conv", "w_out")
        bufs = None if big["w_out"] is None else [big[k] for k in mix_keys]
        ds5o, datto, dconvo, dgl, dbg, *dws = _mix_out_bwd(
            dx, sv["brs"], sv["gl"], sv["bg"], wbs(), w["w_out"], l, nl, bufs, tm=tmix, name=f"mix_bwd_{l}")
        small["b_gate"][l] = dbg[0]
        big.update(zip(mix_keys, dws))

        dhc, dlng, dlnb = _conv_bwd_ln(dconvo, sv["hc"], sv["lng"], sv["lnb"], tm=tm, name=f"conv_bwd_ln_{l}")
        dz, dwdw, dbdw = _conv_bwd_dw(dhc, sv["hg"], sv["pa"], sv["wdw"], bl=bl, s=s, t=tconv, acol=acol,
                                      name=f"conv_bwd_dw_{l}")
        small["conv_w_dw"][l] = dwdw[:CONV_W]
        small["conv_b_dw"][l], small["conv_ln_g"][l], small["conv_ln_b"][l] = dbdw[0], dlng[0], dlnb[0]

        comm = hooks["pending"] if l == 0 else None
        dq, dkn, dvw, dbias, dgq, *hosted = _attn_bwd(datto, sv["pa"], sv["gq2"], sv["gk2"], sv["bias"], bl=bl, s=s,
                                                      datt=datt, qoff=qoff, name=f"attn_bwd_{l}", comm=comm)
        if comm is not None:
            rs.parts = hosted
        dk, dv, dgk = _attn_kv_bwd(dkn, dvw, sv["pa"], sv["gk2"], bl=bl, s=s, datt=datt, tm=_tile(s, 512), koff=koff,
                                   name=f"attn_kv_bwd_{l}")
        small["attn_q_gain"][l] = jnp.sum(dgq.reshape(-1, HEAD_DIM), axis=0)
        small["attn_k_gain"][l] = jnp.sum(dgk.reshape(-1, HEAD_DIM), axis=0)

        du, dd, dcr, dci, dbr, dbi, da, big["s5_w_glu"] = _s5_bwd(
            ds5o, sv["yp"], sv["zg"], sv["xr"], sv["xi"], sv["pa"], sv["spb"], w["s5_w_glu"], l, nl, big["s5_w_glu"],
            bl=bl, s=s, t=ts5, name=f"s5_bwd_{l}")
        prep_ct = (dict(a=da, bblk_r=dbr, bblk_i=dbi, cblk_r=dcr, cblk_in=dci, d=dd), jnp.sum(dbias, axis=0))
        (dprep,) = sv["prep_vjp"](prep_ct)
        for k in _PREP_KEYS:
            small[k][l] = dprep[k][l]

        dpa = jnp.concatenate([du, dq, dk, dv, dz], axis=1)
        big["w_in"] = _dwin_t(dpa, dgl, sv["h2"], l, nl, big["w_in"], ta=tna, tk=tml, name=f"dwin_{l}")
        comm = ([big[k] for k in rs.early], 0) if (rs is not None and l == 0) else None
        dx, dgm, *recv = _mix_dx(dpa, dgl, w["w_in"], l, sv["x1"], sv["gm"], dx, tm=tml, tk=tna, name=f"mix_dx_{l}",
                                 comm=comm)
        small["mix_norm"][l] = dgm[0]
        if comm is not None:
            hooks["early"] = rs.sums(big, recv, 0, rs.early)

        dx = ffn_bwd(dx, sv["x0"], sv["h1"], sv["ab1"], sv["g1"], "ffn1")

    small = {k: jnp.stack(v) for k, v in small.items()}
    return loss_part, dx.reshape(bl, s, d), big, small


def _place():
    x, y, c = lax.axis_index("x"), lax.axis_index("y"), lax.axis_index("c")
    chips = [(1 - x, y), (x, 1 - y), (1 - x, 1 - y)]
    return x, y, c, chips


def _remote(src, dst, send_sems, recv_sems, k, dev):
    return pltpu.make_async_remote_copy(src_ref=src, dst_ref=dst, send_sem=send_sems.at[k], recv_sem=recv_sems.at[k],
                                        device_id=dev, device_id_type=MESH)


def _window(ref, lead, s, axis, blk):
    if axis == 1:
        sl = (pl.ds(pl.multiple_of(s * blk, 16), blk), slice(None))
    else:
        sl = (slice(None), pl.ds(pl.multiple_of(s * blk, LANES), blk))
    return ref.at[sl] if lead is None else ref.at[(lead,) + sl]


def _gather_ops(bufs, axes, send_sems, recv_sems, lyr):
    x, y, c, chips = _place()
    s_me = 2 * x + y
    sibling = (x, y, 1 - lyr)
    nw = len(bufs)
    blks = [f.shape[ax] // N_CHIPS for f, ax in zip(bufs, axes)]
    win = lambda i, s: _window(bufs[i], lyr, s, axes[i], blks[i])
    pairs = [(i, j, cx, cy) for i in range(nw) for j, (cx, cy) in enumerate(chips)]
    sends = [_remote(win(i, s_me), win(i, s_me), send_sems, recv_sems, 6 * i + j, (cx, cy, lyr)) for i, j, cx, cy in pairs]
    passed = [_remote(win(i, 2 * cx + cy), win(i, 2 * cx + cy), send_sems, recv_sems, 6 * i + 3 + j, sibling)
              for i, j, cx, cy in pairs]

    def start():
        @pl.when(c == lyr)
        def _():
            for cp in sends:
                cp.start()

    def forward():
        @pl.when(c == lyr)
        def _():
            for (i, j, cx, cy), fw in zip(pairs, passed):
                piece = win(i, 2 * cx + cy)
                _remote(piece, piece, send_sems, recv_sems, 6 * i + j, (cx, cy, lyr)).wait_recv()
                fw.start()

    def wait():
        @pl.when(c == lyr)
        def _():
            for cp in sends + passed:
                cp.wait_send()

        @pl.when(c != lyr)
        def _():
            for fw in passed:
                fw.wait_recv()

    return start, forward, wait


def _all_gather_weights(fulls, axes, taps):
    nw = len(fulls)

    def body(*refs):
        taps_in = refs[nw]
        outs, taps_out = refs[nw + 1:2 * nw + 1], refs[2 * nw + 1]
        send_sems, recv_sems, tap_send, tap_recv, local_sem = refs[-5:]
        x, y, c, chips = _place()
        s_me = 2 * x + y
        start, forward, wait = _gather_ops(outs, axes, send_sems, recv_sems, 0)
        own_taps = pltpu.make_async_copy(taps_in, taps_out.at[s_me], local_sem)
        own_taps.start()
        tap_sends = [_remote(taps_in, taps_out.at[s_me], tap_send, tap_recv, j, (cx, cy, c))
                     for j, (cx, cy) in enumerate(chips)]
        for cp in tap_sends:
            cp.start()
        start()
        forward()
        wait()
        for j, (cx, cy) in enumerate(chips):
            slab = taps_out.at[2 * cx + cy]
            _remote(slab, slab, tap_send, tap_recv, j, (cx, cy, c)).wait_recv()
        for cp in tap_sends:
            cp.wait_send()
        own_taps.wait()

    return pl.pallas_call(
        body, in_specs=[_ANY] * (nw + 1), out_specs=[_ANY] * (nw + 1),
        out_shape=[_sds(f.shape, f.dtype) for f in fulls] + [_sds((N_CHIPS,) + taps.shape, taps.dtype)],
        input_output_aliases={i: i for i in range(nw)},
        scratch_shapes=[pltpu.SemaphoreType.DMA((6 * nw,)), pltpu.SemaphoreType.DMA((6 * nw,)),
                        pltpu.SemaphoreType.DMA((3,)), pltpu.SemaphoreType.DMA((3,)), pltpu.SemaphoreType.DMA],
        name="all_gather_weights")(*fulls, taps)


def _hosted_gather(comm, n_in, n_out):
    if comm is None:
        return [], [], [], [], [], {}
    bufs = list(comm[0])
    nw = len(bufs)
    return (bufs, [_ANY] * nw, [_ANY] * nw, [_sds(f.shape, f.dtype) for f in bufs],
            [pltpu.SemaphoreType.DMA((6 * nw,)), pltpu.SemaphoreType.DMA((6 * nw,))],
            {n_in + k: n_out + k for k in range(nw)})


def _hosted_gather_steps(comm, out_refs, sems, step, total):
    start, forward, wait = _gather_ops(out_refs, comm[1], sems[0], sems[1], comm[2])

    @pl.when(step == 0)
    def _():
        start()

    def finish():
        @pl.when(step == (3 * total) // 4)
        def _():
            forward()

        @pl.when(step == total - 1)
        def _():
            wait()

    return finish


def _swap_ops(ins, outs, send_sems, recv_sems, lyr):
    x, y, c, _ = _place()
    cps = [_remote(ins[i].at[lyr], outs[i], send_sems, recv_sems, i, (x, y, lyr)) for i in range(len(ins))]

    def start():
        @pl.when(c != lyr)
        def _():
            for cp in cps:
                cp.start()

    def wait():
        @pl.when(c != lyr)
        def _():
            for cp in cps:
                cp.wait_send()

        @pl.when(c == lyr)
        def _():
            for cp in cps:
                cp.wait_recv()

    return start, wait


def _exchange_ops(ins, outs, send_sems, recv_sems, axes, lyr):
    x, y, c, chips = _place()
    s_me = 2 * x + y
    nw = len(ins)
    blks = [r.shape[ax - 1] // N_CHIPS for r, ax in zip(ins, axes)]
    win = lambda i, s: _window(ins[i], None, s, axes[i], blks[i])
    sends = [_remote(win(i, 2 * cx + cy), outs[i].at[s_me], send_sems, recv_sems, 3 * i + j, (cx, cy, lyr))
             for i in range(nw) for j, (cx, cy) in enumerate(chips)]

    def start():
        @pl.when(c == lyr)
        def _():
            for cp in sends:
                cp.start()

    def wait():
        @pl.when(c == lyr)
        def _():
            for i in range(nw):
                for j, (cx, cy) in enumerate(chips):
                    slab = outs[i].at[2 * cx + cy]
                    _remote(slab, slab, send_sems, recv_sems, 3 * i + j, (cx, cy, lyr)).wait_recv()
            for cp in sends:
                cp.wait_send()

    return start, wait


def _parts_shapes(rsums, axes):
    shard = [tuple(dim // N_CHIPS if i == ax - 1 else dim for i, dim in enumerate(r.shape)) for r, ax in zip(rsums, axes)]
    return [_sds((N_CHIPS,) + sh, r.dtype) for sh, r in zip(shard, rsums)]


def _rs_exchange(rsums, axes, lyr):
    nw = len(rsums)

    def body(*refs):
        start, wait = _exchange_ops(refs[:nw], refs[nw:2 * nw], refs[-2], refs[-1], axes, lyr)
        start()
        wait()

    return pl.pallas_call(
        body, in_specs=[_ANY] * nw, out_specs=[_ANY] * nw, out_shape=_parts_shapes(rsums, axes),
        scratch_shapes=[pltpu.SemaphoreType.DMA((3 * nw,)), pltpu.SemaphoreType.DMA((3 * nw,))],
        name=f"rs_exchange_l{lyr}")(*rsums)


def _rs_join(ts):
    nw = len(ts)

    def body(*refs):
        outs = refs[nw:2 * nw]
        send_sems, recv_sems = refs[-2:]
        x, y, c, _ = _place()
        sends = [_remote(outs[i].at[c], outs[i].at[c], send_sems, recv_sems, i, (x, y, 1 - c)) for i in range(nw)]
        for cp in sends:
            cp.start()
        for i in range(nw):
            slab = outs[i].at[1 - c]
            _remote(slab, slab, send_sems, recv_sems, i, (x, y, 1 - c)).wait_recv()
        for cp in sends:
            cp.wait_send()

    return pl.pallas_call(
        body, in_specs=[_ANY] * nw, out_specs=[_ANY] * nw, out_shape=[_sds(t.shape, t.dtype) for t in ts],
        input_output_aliases={i: i for i in range(nw)},
        scratch_shapes=[pltpu.SemaphoreType.DMA((nw,)), pltpu.SemaphoreType.DMA((nw,))],
        name="rs_join_layers")(*ts)


def _all_reduce_small(arrs):
    na = len(arrs)
    nd = 8

    def body(*refs):
        ins, outs, recvs = refs[:na], refs[na:2 * na], refs[2 * na:3 * na]
        send_sems, recv_sems = refs[-2:]
        x, y, c, _ = _place()
        me = 4 * x + 2 * y + c
        for i in range(na):
            recvs[i][0] = ins[i][...]
        cps = []
        for rel in range(1, nd):
            dev = (1 - x if rel & 4 else x, 1 - y if rel & 2 else y, 1 - c if rel & 1 else c)
            for i in range(na):
                cp = _remote(ins[i], recvs[i].at[rel], send_sems, recv_sems, (rel - 1) * na + i, dev)
                cp.start()
                cps.append(cp)
        for rel in range(1, nd):
            for i in range(na):
                _remote(ins[i], recvs[i].at[rel], send_sems, recv_sems, (rel - 1) * na + i, (x, y, c)).wait_recv()
        for i in range(na):
            acc = recvs[i][me]
            for dv in range(1, nd):
                acc = acc + recvs[i][lax.bitwise_xor(me, dv)]
            outs[i][...] = acc
        for cp in cps:
            cp.wait_send()

    vm = pl.BlockSpec(memory_space=pltpu.VMEM)
    nsem = (nd - 1) * na
    return pl.pallas_call(
        body, in_specs=[vm] * na, out_specs=[vm] * na, out_shape=[_sds(t.shape, F32) for t in arrs],
        scratch_shapes=[pltpu.VMEM((nd,) + t.shape, F32) for t in arrs]
        + [pltpu.SemaphoreType.DMA((nsem,)), pltpu.SemaphoreType.DMA((nsem,))],
        compiler_params=pltpu.CompilerParams(vmem_limit_bytes=VMEM_LIMIT), name="all_reduce_small")(*arrs)


def _adamw_small(ws, gs, ms, vs):
    na = len(ws)
    c1 = 1.0 - ADAM_B1 ** ADAM_STEP
    c2 = 1.0 - ADAM_B2 ** ADAM_STEP

    def body(*refs):
        w_r, g_r, m_r, v_r = (refs[k * na:(k + 1) * na] for k in range(4))
        d_r, nm_r, nv_r = (refs[(4 + k) * na:(5 + k) * na] for k in range(3))
        for i in range(na):
            gv = g_r[i][...]
            mn = ADAM_B1 * m_r[i][...] + (1.0 - ADAM_B1) * gv
            vn = ADAM_B2 * v_r[i][...] + (1.0 - ADAM_B2) * (gv * gv)
            nm_r[i][...] = mn
            nv_r[i][...] = vn
            d_r[i][...] = -ADAM_LR * ((mn / c1) / (jnp.sqrt(vn / c2) + ADAM_EPS) + ADAM_WD * w_r[i][...])

    vm = pl.BlockSpec(memory_space=pltpu.VMEM)
    res = pl.pallas_call(
        body, in_specs=[vm] * (4 * na), out_specs=[vm] * (3 * na), out_shape=[_sds(t.shape, F32) for t in ws] * 3,
        compiler_params=pltpu.CompilerParams(vmem_limit_bytes=VMEM_LIMIT), name="adamw_small")(*ws, *gs, *ms, *vs)
    return res[:na], res[na:2 * na], res[2 * na:]


def kernel(x, ffn1_norm, ffn1_w_up, ffn1_w_down, mix_norm, w_in, b_gate, s5_lambda_re, s5_lambda_im, s5_log_dt, s5_b_re, s5_b_im, s5_c_re, s5_c_im, s5_d, s5_w_glu, w_br_s5, attn_q_gain, attn_k_gain, attn_rel_bias, w_br_attn, conv_w_dw, conv_b_dw, conv_ln_g, conv_ln_b, w_br_conv, w_out, ffn2_norm, ffn2_w_up, ffn2_w_down, loss_target, m_ffn1_norm, m_ffn1_w_up, m_ffn1_w_down, m_mix_norm, m_w_in, m_b_gate, m_s5_lambda_re, m_s5_lambda_im, m_s5_log_dt, m_s5_b_re, m_s5_b_im, m_s5_c_re, m_s5_c_im, m_s5_d, m_s5_w_glu, m_w_br_s5, m_attn_q_gain, m_attn_k_gain, m_attn_rel_bias, m_w_br_attn, m_conv_w_dw, m_conv_b_dw, m_conv_ln_g, m_conv_ln_b, m_w_br_conv, m_w_out, m_ffn2_norm, m_ffn2_w_up, m_ffn2_w_down, v_ffn1_norm, v_ffn1_w_up, v_ffn1_w_down, v_mix_norm, v_w_in, v_b_gate, v_s5_lambda_re, v_s5_lambda_im, v_s5_log_dt, v_s5_b_re, v_s5_b_im, v_s5_c_re, v_s5_c_im, v_s5_d, v_s5_w_glu, v_w_br_s5, v_attn_q_gain, v_attn_k_gain, v_attn_rel_bias, v_w_br_attn, v_conv_w_dw, v_conv_b_dw, v_conv_ln_g, v_conv_ln_b, v_w_br_conv, v_w_out, v_ffn2_norm, v_ffn2_w_up, v_ffn2_w_down):
    a = dict(locals())
    xi, yi, ci = lax.axis_index("x"), lax.axis_index("y"), lax.axis_index("c")
    s_me = 2 * xi + yi
    big_keys = list(_BIG_KEYS)
    axes = [1 if k == "w_in" else _BIG_KEYS[k] for k in big_keys]

    s_idx = s_me.astype(jnp.int32).reshape(1)
    c_idx = ci.astype(jnp.int32).reshape(1)
    placed = {k: _place_shard(jnp.swapaxes(a[k], 1, 2).astype(BF16) if k == "w_in" else a[k], ax, s_idx,
                              name=f"place_{k}") for k, ax in zip(big_keys, axes)}
    axis_of = dict(zip(big_keys, axes))
    first = ["ffn1_w_up", "ffn1_w_down"]
    *fulls, taps = _all_gather_weights([placed[k] for k in first], [axis_of[k] for k in first], a["conv_w_dw"])
    placed.update(zip(first, fulls))
    flat = lambda t: t.reshape(t.shape[0], t.shape[1], -1) if t.ndim == 4 else t
    w = {k: flat(a[k]) for k in _WEIGHTS}
    w.update(placed)
    w["conv_w_dw"] = jnp.moveaxis(taps, 0, 2).reshape(taps.shape[1], taps.shape[2], -1)

    class _ReduceScatter:
        parts = parts_early = recv_late = None

        def __init__(self):
            self.rsums = {}
            self.late = ["ffn1_w_up", "ffn1_w_down"]
            self.early = [k for k in big_keys if k not in self.late]

        def sums(self, big, recv, lyr, keys):
            rsums = [_add_sibling(big[k], r, lyr, c_idx, name=f"rs_add_sibling_{k}_l{lyr}") for r, k in zip(recv, keys)]
            self.rsums.update({(k, lyr): r for k, r in zip(keys, rsums)})
            return rsums, [axis_of[k] for k in keys], lyr

    rs = _ReduceScatter()
    early = ("ffn1_w_up", "ffn1_w_down", "w_in")
    second = ["ffn2_w_up"]
    sets = {"ffn1_up": ([k for k in big_keys if k not in first + second], 0),
            "win_a": (second, 0),
            "attn_fwd": ([k for k in big_keys if k in early], 1),
            "ffn2_up": ([k for k in big_keys if k not in early], 1)}
    gather = {tag: (keys, [axis_of[k] for k in keys], lyr) for tag, (keys, lyr) in sets.items()}
    loss_part, grad_x, gbig, gsmall = _local_step(a["x"], a["loss_target"], w, rs, gather)
    loss = lax.psum(loss_part, ("x", "y", "c"))

    rsums_late, axes_late, _ = rs.sums(gbig, rs.recv_late, 0, rs.late)
    parts0 = dict(zip(rs.early, rs.parts_early))
    parts0.update(zip(rs.late, _rs_exchange(rsums_late, axes_late, 0)))
    mine = [None] * len(big_keys)
    for lyr, parts in ((1, dict(zip(big_keys, rs.parts))), (0, parts0)):
        mine = [_add_chips(rs.rsums[k, lyr], parts[k], ax, s_idx, c_idx, lyr, buf, name=f"rs_add_chips_{k}_l{lyr}")
                for ax, buf, k in zip(axes, mine, big_keys)]
    gb = dict(zip(big_keys, _rs_join(mine)))
    gb["w_in"] = jnp.swapaxes(gb["w_in"], 1, 2)

    small_keys = list(_SMALL_KEYS)
    gs = dict(zip(small_keys, _all_reduce_small([gsmall[k] for k in small_keys])))
    blk = a["conv_w_dw"].shape[2]
    gs["conv_w_dw"] = lax.dynamic_slice_in_dim(gs["conv_w_dw"], s_me * blk, blk, axis=2)

    delta, new_m, new_v = {}, {}, {}
    for k in big_keys:
        shp = a[k].shape
        two_d = lambda t: t.reshape(-1, shp[-1])
        d_, m_, v_ = _adamw(two_d(a[k]), two_d(gb[k]), two_d(a["m_" + k]), two_d(a["v_" + k]), name=f"adamw_{k}")
        delta[k], new_m[k], new_v[k] = d_.reshape(shp), m_.reshape(shp), v_.reshape(shp)
    res = _adamw_small([flat(a[k]) for k in small_keys], [gs[k] for k in small_keys],
                       [flat(a["m_" + k]) for k in small_keys], [flat(a["v_" + k]) for k in small_keys])
    for dst, vals in zip((delta, new_m, new_v), res):
        dst.update({k: t.reshape(a[k].shape) for k, t in zip(small_keys, vals)})
    grads = {**gb, **{k: t.reshape(a[k].shape) for k, t in gs.items()}}

    return (loss, grad_x, *[grads[k] for k in _WEIGHTS], *[delta[k] for k in _WEIGHTS],
            *[new_m[k] for k in _WEIGHTS], *[new_v[k] for k in _WEIGHTS])
```
